```python
import math
import jax, jax.numpy as jnp
from jax import lax
import numpy as np

D_MODEL = 1024
BATCH = 8
SEQ = 2048
DEPTH = 2

HEAD_DIM = 64
N_MIX_GROUPS = 4
GROUP_WIDTH = D_MODEL // N_MIX_GROUPS
HEADS_PER_GROUP = GROUP_WIDTH // HEAD_DIM
D_MIX = GROUP_WIDTH * N_MIX_GROUPS
CHUNK = 64
Q_BLOCK = 128
CONV_W = 4
LRU_C = 8.0
MEM_LEN = 256
MEM_HEADS = 4
MEM_HEAD_DIM = 64
MEM_WIDTH = MEM_HEADS * MEM_HEAD_DIM
N_EXPERT_GROUPS = 4
EXPERTS_PER_GROUP = 8
N_EXPERTS = N_EXPERT_GROUPS * EXPERTS_PER_GROUP
TOP_K_IN_GROUP = 2
D_EXPERT = 256
EPS = 1e-6

COL_SIZES = (
    GROUP_WIDTH, GROUP_WIDTH, GROUP_WIDTH, GROUP_WIDTH,
    GROUP_WIDTH, GROUP_WIDTH, GROUP_WIDTH, HEADS_PER_GROUP,
    GROUP_WIDTH, GROUP_WIDTH,
    GROUP_WIDTH, GROUP_WIDTH, GROUP_WIDTH, HEADS_PER_GROUP, HEADS_PER_GROUP, GROUP_WIDTH,
)
N_IN = sum(COL_SIZES)

kernel_name = "hybrid_parallel_hgrn2_fox_rglru_gdn_hmoe"

F32 = jnp.float32


def _rmsnorm(x, g):
    xf = x.astype(F32)
    y = xf * lax.rsqrt(jnp.mean(xf * xf, axis=-1, keepdims=True) + EPS)
    return (y * g.astype(F32)).astype(x.dtype)


def _head_rms(x):
    return x * lax.rsqrt(jnp.mean(x * x, axis=-1, keepdims=True) + EPS)


def _l2norm(x):
    return x * lax.rsqrt(jnp.sum(x * x, axis=-1, keepdims=True) + EPS)


def _heads(t):
    b, s, w = t.shape
    return t.astype(F32).reshape(b, s, w // HEAD_DIM, HEAD_DIM)


def _causal_dwconv(x, w):
    c = x.shape[-1]
    return lax.conv_general_dilated(x, w[:, None, :], window_strides=(1,), padding=[(w.shape[0] - 1, 0)],
                                    dimension_numbers=('NWC', 'WIO', 'NWC'), feature_group_count=c)


def _to_chunks(t):
    b, s, h, d = t.shape
    return t.reshape(b, s // CHUNK, CHUNK, h, d).transpose(1, 0, 3, 2, 4)


def _from_chunks(t):
    n, b, h, c, d = t.shape
    return t.transpose(1, 0, 3, 2, 4).reshape(b, n * c, h, d)


def hgrn2_group(q, f_logit, i, g, lb):
    b, s, w = q.shape
    q = jax.nn.silu(_heads(q))
    fl = _heads(f_logit)
    lb = lb.astype(F32).reshape(HEADS_PER_GROUP, HEAD_DIM)
    log_f = jnp.logaddexp(jnp.log(lb), jnp.log1p(-lb) + jax.nn.log_sigmoid(fl))
    k = (1.0 - lb) * jax.nn.sigmoid(-fl)
    v = _heads(i)
    qc, kc, vc, lc = _to_chunks(q), _to_chunks(k), _to_chunks(v), _to_chunks(log_f)
    bc = jnp.cumsum(lc, axis=3)
    causal = jnp.tril(jnp.ones((CHUNK, CHUNK), bool))[:, :, None]

    def step(state, inp):
        qi, ki, vi, bi = inp
        diff = bi[:, :, :, None, :] - bi[:, :, None, :, :]
        decay = jnp.where(causal, jnp.exp(jnp.where(causal, diff, 0.0)), 0.0)
        scores = jnp.einsum('bhtk,bhsk,bhtsk->bhts', qi, ki, decay)
        o = jnp.einsum('bhts,bhsv->bhtv', scores, vi) + jnp.einsum('bhtk,bhkv->bhtv', qi * jnp.exp(bi), state)
        b_last = bi[:, :, -1:, :]
        state = jnp.exp(b_last[:, :, 0, :, None]) * state + jnp.einsum('bhsk,bhsv->bhkv', ki * jnp.exp(b_last - bi), vi)
        return state, o

    s0 = jnp.zeros((b, HEADS_PER_GROUP, HEAD_DIM, HEAD_DIM), F32)
    _, o = lax.scan(step, s0, (qc, kc, vc, bc))
    o = _from_chunks(o)
    y = _head_rms(o) * jax.nn.silu(_heads(g))
    return y.reshape(b, s, w)


def fox_group(q, k, v, f_logit, f_bias):
    b, s, w = q.shape
    q, k, v = _heads(q), _heads(k), _heads(v)
    log_f = jax.nn.log_sigmoid(f_logit.astype(F32) + f_bias.astype(F32))
    c = jnp.cumsum(log_f, axis=1).transpose(0, 2, 1)
    nb = s // Q_BLOCK
    qb = q.reshape(b, nb, Q_BLOCK, HEADS_PER_GROUP, HEAD_DIM).transpose(1, 0, 2, 3, 4)
    cb = c.reshape(b, HEADS_PER_GROUP, nb, Q_BLOCK).transpose(2, 0, 1, 3)
    qpos = jnp.arange(s).reshape(nb, Q_BLOCK)
    kpos = jnp.arange(s)
    scale = HEAD_DIM ** -0.5

    def block(args):
        qi, ci, pi = args
        sc = jnp.einsum('bqhd,bkhd->bhqk', qi, k) * scale + (ci[..., :, None] - c[:, :, None, :])
        sc = jnp.where(pi[:, None] >= kpos[None, :], sc, -jnp.inf)
        p = jax.nn.softmax(sc, axis=-1)
        return jnp.einsum('bhqk,bkhd->bqhd', p, v)

    o = lax.map(block, (qb, cb, qpos))
    o = o.transpose(1, 0, 2, 3, 4).reshape(b, s, HEADS_PER_GROUP, HEAD_DIM)
    return _head_rms(o).reshape(b, s, w)


def rglru_group(xb, gate, conv_w, conv_b, w_a, b_a, w_x, b_x, lam):
    b, s, w = xb.shape
    xc = _causal_dwconv(xb.astype(F32), conv_w.astype(F32)) + conv_b.astype(F32)
    xh = xc.reshape(b, s, HEADS_PER_GROUP, HEAD_DIM)
    r = jax.nn.sigmoid(jnp.einsum('bshi,hij->bshj', xh, w_a.astype(F32)).reshape(b, s, w) + b_a.astype(F32))
    ig = jax.nn.sigmoid(jnp.einsum('bshi,hij->bshj', xh, w_x.astype(F32)).reshape(b, s, w) + b_x.astype(F32))
    log_a = -LRU_C * r * jax.nn.softplus(-lam.astype(F32))
    a = jnp.exp(log_a)
    u = jnp.sqrt(-jnp.expm1(2.0 * log_a)) * (ig * xc)

    def combine(left, right):
        a_l, u_l = left
        a_r, u_r = right
        return a_l * a_r, a_r * u_l + u_r

    _, h = lax.associative_scan(combine, (a, u), axis=1)
    y = h * jax.nn.gelu(gate.astype(F32))
    return _head_rms(y.reshape(b, s, HEADS_PER_GROUP, HEAD_DIM)).reshape(b, s, w)


def gdn_group(q, k, v, beta_logit, a_logit, z, conv_w, a_log, dt_bias):
    b, s, w = q.shape
    qkv = jax.nn.silu(_causal_dwconv(jnp.concatenate([q, k, v], axis=-1).astype(F32), conv_w.astype(F32)))
    q, k, v = [_heads(t) for t in jnp.split(qkv, 3, axis=-1)]
    q = _l2norm(q) * HEAD_DIM ** -0.5
    k = _l2norm(k)
    beta = jax.nn.sigmoid(beta_logit.astype(F32))
    g = -jnp.exp(a_log.astype(F32)) * jax.nn.softplus(a_logit.astype(F32) + dt_bias.astype(F32))
    qc, kc, vc = _to_chunks(q), _to_chunks(k), _to_chunks(v)
    bc = _to_chunks(beta[..., None])[..., 0]
    G = jnp.cumsum(_to_chunks(g[..., None])[..., 0], axis=-1)
    incl = jnp.tril(jnp.ones((CHUNK, CHUNK), bool))
    strict = jnp.tril(jnp.ones((CHUNK, CHUNK), bool), -1)
    diff = G[..., :, None] - G[..., None, :]
    gamma = jnp.where(incl, jnp.exp(jnp.where(incl, diff, 0.0)), 0.0)
    k_beta = kc * bc[..., None]
    m = jnp.where(strict, jnp.einsum('...tk,...sk->...ts', k_beta, kc) * gamma, 0.0)
    eye = jnp.eye(CHUNK, dtype=F32)
    t_inv = lax.linalg.triangular_solve(eye + m, jnp.broadcast_to(eye, m.shape), left_side=True, lower=True)
    u = t_inv @ (vc * bc[..., None])
    wk = t_inv @ (k_beta * jnp.exp(G)[..., None])
    a_qk = jnp.einsum('...tk,...sk->...ts', qc, kc) * gamma
    q_dec = qc * jnp.exp(G)[..., None]
    k_dec = kc * jnp.exp(G[..., -1:] - G)[..., None]
    g_last = jnp.exp(G[..., -1])

    def step(state, inp):
        ui, wi, aqi, qdi, kdi, gli = inp
        v_new = ui - wi @ state
        o = qdi @ state + aqi @ v_new
        state = gli[..., None, None] * state + jnp.einsum('bhsk,bhsv->bhkv', kdi, v_new)
        return state, o

    s0 = jnp.zeros((b, HEADS_PER_GROUP, HEAD_DIM, HEAD_DIM), F32)
    _, o = lax.scan(step, s0, (u, wk, a_qk, q_dec, k_dec, g_last))
    o = _from_chunks(o)
    y = _head_rms(o) * jax.nn.silu(_heads(z))
    return y.reshape(b, s, w)


def mem_cross_attention(h, memn, w_q, w_kv, w_o):
    b, s, _ = h.shape
    q = (h @ w_q).reshape(b, s, MEM_HEADS, MEM_HEAD_DIM)
    k, v = jnp.split(memn @ w_kv, 2, axis=-1)
    k = k.reshape(b, MEM_LEN, MEM_HEADS, MEM_HEAD_DIM)
    v = v.reshape(b, MEM_LEN, MEM_HEADS, MEM_HEAD_DIM)
    sc = jnp.einsum('bshd,bmhd->bhsm', q, k).astype(F32) * MEM_HEAD_DIM ** -0.5
    p = jax.nn.softmax(sc, axis=-1).astype(v.dtype)
    o = jnp.einsum('bhsm,bmhd->bshd', p, v).reshape(b, s, MEM_WIDTH)
    return o @ w_o


def hier_moe(h, w_rg, b_rg, w_re, b_re, w_gu, w_dn):
    b, s, d = h.shape
    ht = h.reshape(-1, d)
    t = ht.shape[0]
    grp_prob = jax.nn.softmax((ht @ w_rg + b_rg).astype(F32), axis=-1)
    p_grp, g_sel = lax.top_k(grp_prob, 1)
    exp_logits = (ht @ w_re + b_re).astype(F32).reshape(t, N_EXPERT_GROUPS, EXPERTS_PER_GROUP)
    sel = jnp.take_along_axis(exp_logits, g_sel[:, :, None], axis=1)[:, 0]
    w_top, i_top = lax.top_k(jax.nn.softmax(sel, axis=-1), TOP_K_IN_GROUP)
    w_top = w_top / jnp.sum(w_top, axis=-1, keepdims=True)
    gates = p_grp * w_top
    idx = g_sel * EXPERTS_PER_GROUP + i_top
    combine = jnp.sum(jax.nn.one_hot(idx, N_EXPERTS, dtype=F32) * gates[..., None], axis=1).astype(h.dtype)
    out = jnp.zeros_like(ht)
    for gi in range(N_EXPERT_GROUPS):
        sl = slice(gi * EXPERTS_PER_GROUP, (gi + 1) * EXPERTS_PER_GROUP)
        gu = jnp.einsum('td,edf->tef', ht, w_gu[sl])
        g_act, up = jnp.split(gu, 2, axis=-1)
        act = jax.nn.silu(g_act) * up * combine[:, sl, None]
        out = out + jnp.einsum('tef,efd->td', act, w_dn[sl])
    return out.reshape(b, s, d)


def setup_inputs(seed: int = 0) -> dict:
    key = jax.random.key(seed)
    ks = jax.random.split(key, 34)
    L, H = DEPTH, HEADS_PER_GROUP

    def nrm(k, shape, scale):
        return scale * jax.random.normal(k, shape, F32)

    def gain(k, shape):
        return 1.0 + nrm(k, shape, 0.02)

    lru_u = jax.random.uniform(ks[12], (L, GROUP_WIDTH), F32, minval=0.9, maxval=0.999)
    dt = jnp.exp(jax.random.uniform(ks[15], (L, H), F32, minval=math.log(1e-3), maxval=math.log(1e-1)))
    return {
        "x": nrm(ks[0], (BATCH, SEQ, D_MODEL), 1.0),
        "mem": nrm(ks[1], (BATCH, MEM_LEN, D_MODEL), 1.0),
        "norm_mix": gain(ks[2], (L, D_MODEL)),
        "w_in": nrm(ks[3], (L, D_MODEL, N_IN), D_MODEL ** -0.5),
        "hgrn_lb": nrm(ks[4], (L, GROUP_WIDTH), 0.1),
        "fox_fb": 2.0 + nrm(ks[5], (L, H), 0.5),
        "lru_conv_w": nrm(ks[6], (L, CONV_W, GROUP_WIDTH), CONV_W ** -0.5),
        "lru_conv_b": nrm(ks[7], (L, GROUP_WIDTH), 0.01),
        "lru_wa": nrm(ks[8], (L, H, HEAD_DIM, HEAD_DIM), HEAD_DIM ** -0.5),
        "lru_ba": nrm(ks[9], (L, GROUP_WIDTH), 0.01),
        "lru_wx": nrm(ks[10], (L, H, HEAD_DIM, HEAD_DIM), HEAD_DIM ** -0.5),
        "lru_bx": nrm(ks[11], (L, GROUP_WIDTH), 0.01),
        "lru_lam": jnp.log(lru_u) - jnp.log1p(-lru_u),
        "gdn_conv_w": nrm(ks[13], (L, CONV_W, 3 * GROUP_WIDTH), CONV_W ** -0.5),
        "gdn_a_log": jnp.log(jax.random.uniform(ks[14], (L, H), F32, minval=1.0, maxval=16.0)),
        "gdn_dt_bias": dt + jnp.log(-jnp.expm1(-dt)),
        "mix_gain": gain(ks[16], (L, D_MIX)),
        "w_out": nrm(ks[17], (L, D_MIX, D_MODEL), D_MIX ** -0.5),
        "norm_mem": gain(ks[18], (L, D_MODEL)),
        "norm_memkv": gain(ks[19], (L, D_MODEL)),
        "w_mq": nrm(ks[20], (L, D_MODEL, MEM_WIDTH), D_MODEL ** -0.5),
        "w_mkv": nrm(ks[21], (L, D_MODEL, 2 * MEM_WIDTH), D_MODEL ** -0.5),
        "w_mo": nrm(ks[22], (L, MEM_WIDTH, D_MODEL), MEM_WIDTH ** -0.5),
        "norm_ffn": gain(ks[23], (L, D_MODEL)),
        "w_rg": nrm(ks[24], (L, D_MODEL, N_EXPERT_GROUPS), D_MODEL ** -0.5),
        "b_rg": nrm(ks[25], (L, N_EXPERT_GROUPS), 0.01),
        "w_re": nrm(ks[26], (L, D_MODEL, N_EXPERTS), D_MODEL ** -0.5),
        "b_re": nrm(ks[27], (L, N_EXPERTS), 0.01),
        "w_e_gu": nrm(ks[28], (L, N_EXPERTS, D_MODEL, 2 * D_EXPERT), D_MODEL ** -0.5),
        "w_e_dn": nrm(ks[29], (L, N_EXPERTS, D_EXPERT, D_MODEL), D_EXPERT ** -0.5),
        "norm_final": gain(ks[30], (D_MODEL,)),
    }


def reference(x, mem, norm_mix, w_in, hgrn_lb, fox_fb, lru_conv_w, lru_conv_b, lru_wa, lru_ba, lru_wx, lru_bx,
              lru_lam, gdn_conv_w, gdn_a_log, gdn_dt_bias, mix_gain, w_out, norm_mem, norm_memkv, w_mq, w_mkv,
              w_mo, norm_ffn, w_rg, b_rg, w_re, b_re, w_e_gu, w_e_dn, norm_final):
    lb_all = jnp.cumsum(jax.nn.softmax(hgrn_lb.astype(F32), axis=0), axis=0)
    lb_all = lb_all - lb_all[0]
    split_at = np.cumsum(COL_SIZES)[:-1].tolist()
    for l in range(DEPTH):
        h = _rmsnorm(x, norm_mix[l])
        parts = jnp.split(h @ w_in[l], split_at, axis=-1)
        (a_q, a_f, a_i, a_g, b_q, b_k, b_v, b_f, c_x, c_g, d_q, d_k, d_v, d_b, d_a, d_z) = parts
        y_a = hgrn2_group(a_q, a_f, a_i, a_g, lb_all[l])
        y_b = fox_group(b_q, b_k, b_v, b_f, fox_fb[l])
        y_c = rglru_group(c_x, c_g, lru_conv_w[l], lru_conv_b[l], lru_wa[l], lru_ba[l], lru_wx[l], lru_bx[l], lru_lam[l])
        y_d = gdn_group(d_q, d_k, d_v, d_b, d_a, d_z, gdn_conv_w[l], gdn_a_log[l], gdn_dt_bias[l])
        y = jnp.concatenate([y_a, y_b, y_c, y_d], axis=-1) * mix_gain[l].astype(F32)
        x = x + y.astype(x.dtype) @ w_out[l]
        h = _rmsnorm(x, norm_mem[l])
        memn = _rmsnorm(mem, norm_memkv[l])
        x = x + mem_cross_attention(h, memn, w_mq[l], w_mkv[l], w_mo[l])
        h = _rmsnorm(x, norm_ffn[l])
        x = x + hier_moe(h, w_rg[l], b_rg[l], w_re[l], b_re[l], w_e_gu[l], w_e_dn[l])
    return _rmsnorm(x, norm_final)
```

```python
import functools
import math

import jax
import jax.numpy as jnp
from jax import lax
from jax.experimental import pallas as pl
from jax.experimental.pallas import tpu as pltpu

F32 = jnp.float32
BF16 = jnp.bfloat16

HEAD_DIM = 64
GROUP_WIDTH = 256
HEADS = GROUP_WIDTH // HEAD_DIM
GDN_CHUNK = 64
HGRN_CHUNK = 16
CONV_W = 4
LRU_C = 8.0
EPS = 1e-6
N_EXPERT_GROUPS = 4
EXPERTS_PER_GROUP = 8
N_EXPERTS = N_EXPERT_GROUPS * EXPERTS_PER_GROUP
D_EXPERT = 256
SMALL_W = 128
SEQ_TILE = 256
VMEM_LIMIT = 56 * 1024 * 1024

(COL_AQ, COL_AF, COL_AI, COL_AG, COL_BQ, COL_BK, COL_BV, COL_CX, COL_CG,
 COL_DQ, COL_DK, COL_DV, COL_DZ) = range(13)
N_PROJ = 13 * GROUP_WIDTH + 2 * SMALL_W
COL_SMALL = 13 * GROUP_WIDTH // SMALL_W
LANE_FOX = 0
LANE_BETA = 4
LANE_GDEC = 8


def _cparams(*sem):
    return pltpu.CompilerParams(dimension_semantics=sem, vmem_limit_bytes=VMEM_LIMIT)


def _dot(a, b):
    return jnp.dot(a, b, preferred_element_type=F32)


def _dot_nt(a, b):
    return lax.dot_general(a, b, (((1,), (1,)), ((), ())), preferred_element_type=F32)


def _dot_tn(a, b):
    return lax.dot_general(a, b, (((0,), (0,)), ((), ())), preferred_element_type=F32)


def _split3(x):
    h = x.astype(BF16)
    r = x - h.astype(F32)
    m = r.astype(BF16)
    l = (r - m.astype(F32)).astype(BF16)
    return h, m, l


def _dot_exact_rhs(x, w_bf16):
    h, m, l = _split3(x)
    return _dot(h, w_bf16) + _dot(m, w_bf16) + _dot(l, w_bf16)


def _dot_exact_lhs(w_bf16, x):
    h, m, l = _split3(x)
    return _dot(w_bf16, h) + _dot(w_bf16, m) + _dot(w_bf16, l)


def _iota(shape, dim):
    return lax.broadcasted_iota(jnp.int32, shape, dim)


def _head_ones(n=GROUP_WIDTH):
    r = _iota((n, n), 0) // HEAD_DIM
    c = _iota((n, n), 1) // HEAD_DIM
    return r == c


def _sigmoid(x):
    return 1.0 / (1.0 + jnp.exp(-x))


def _silu(x):
    return x * _sigmoid(x)


def _log_sigmoid(x):
    return jnp.minimum(x, 0.0) - jnp.log1p(jnp.exp(-jnp.abs(x)))


def _softplus(x):
    return jnp.maximum(x, 0.0) + jnp.log1p(jnp.exp(-jnp.abs(x)))


def _gelu_tanh(x):
    return 0.5 * x * (1.0 + jnp.tanh(math.sqrt(2.0 / math.pi) * (x + 0.044715 * (x * x * x))))


def _head_mean_sq(x, ones_bf16):
    return _dot_exact_rhs(x * x, ones_bf16) * (1.0 / HEAD_DIM)


def _head_rms(x, ones_bf16):
    return x * lax.rsqrt(_head_mean_sq(x, ones_bf16) + EPS)


def _stack_heads(x):
    lane_head = _iota(x.shape, 1) // HEAD_DIM
    parts = []
    for h in range(HEADS):
        parts.append(jnp.where(lane_head == h, x, 0.0))
    return jnp.concatenate(parts, axis=0)


def _unstack_heads(xs, rows):
    out = xs[0:rows]
    for h in range(1, HEADS):
        out = out + xs[h * rows:(h + 1) * rows]
    return out


def _causal_conv4(x, prev8, w):
    r = x.shape[0]
    row8 = _iota((8, x.shape[1]), 0)
    acc = x * w[CONV_W - 1:CONV_W, :]
    for k in range(1, CONV_W):
        xs = pltpu.roll(x, k, 0)
        ps = pltpu.roll(prev8, k, 0)
        top = jnp.where(row8 < k, ps, xs[0:8])
        xs = jnp.concatenate([top, xs[8:r]], axis=0)
        acc = acc + xs * w[CONV_W - 1 - k:CONV_W - k, :]
    return acc


def _norm_matmul_kernel(x_ref, g_ref, w_ref, o_ref):
    x = x_ref[...]
    ms = jnp.mean(x * x, axis=-1, keepdims=True)
    h = (x * lax.rsqrt(ms + EPS)) * g_ref[...]
    o_ref[...] = _dot(h.astype(BF16), w_ref[...]).astype(o_ref.dtype)


def norm_matmul(x, g, w_bf16, tm, out_dtype=F32):
    m, d = x.shape
    n = w_bf16.shape[1]
    return pl.pallas_call(
        _norm_matmul_kernel,
        grid=(m // tm,),
        in_specs=[pl.BlockSpec((tm, d), lambda i: (i, 0)),
                  pl.BlockSpec((1, d), lambda i: (0, 0)),
                  pl.BlockSpec((d, n), lambda i: (0, 0))],
        out_specs=pl.BlockSpec((tm, n), lambda i: (i, 0)),
        out_shape=jax.ShapeDtypeStruct((m, n), out_dtype),
        compiler_params=_cparams("parallel"),
        name="norm_matmul",
    )(x, g.reshape(1, d), w_bf16)


def _small_prep_kernel(s_ref, p_ref, o_ref):
    S = s_ref.shape[1]
    W = s_ref.shape[2]
    blk = SEQ_TILE
    lane = _iota((blk, W), 1)
    is_fox = (lane >= LANE_FOX) & (lane < LANE_FOX + HEADS)
    is_beta = (lane >= LANE_BETA) & (lane < LANE_BETA + HEADS)
    is_gdec = (lane >= LANE_GDEC) & (lane < LANE_GDEC + HEADS)
    r = _iota((blk, blk), 0)
    c = _iota((blk, blk), 1)
    tril_all = jnp.where(r >= c, 1.0, 0.0).astype(BF16)
    tril_chunk = jnp.where((r >= c) & (r // GDN_CHUNK == c // GDN_CHUNK), 1.0, 0.0).astype(BF16)
    neg_exp_a = -jnp.exp(p_ref[1:2, :])
    carry = jnp.zeros((1, W), F32)
    for i in range(S // blk):
        sl = pl.ds(i * blk, blk)
        z = s_ref[0, sl, :] + p_ref[0:1, :]
        fox = jnp.where(is_fox, _log_sigmoid(z), 0.0)
        beta = jnp.where(is_beta, _sigmoid(z), 0.0)
        gdec = jnp.where(is_gdec, neg_exp_a * _softplus(z), 0.0)
        cf = _dot_exact_lhs(tril_all, fox) + carry
        cg = _dot_exact_lhs(tril_chunk, gdec)
        carry = cf[blk - 1:blk, :]
        o_ref[0, sl, :] = cf + cg + beta


def small_prep(small, params):
    b, s, w = small.shape
    return pl.pallas_call(
        _small_prep_kernel,
        grid=(b,),
        in_specs=[pl.BlockSpec((1, s, w), lambda i: (i, 0, 0)),
                  pl.BlockSpec((8, w), lambda i: (0, 0))],
        out_specs=pl.BlockSpec((1, s, w), lambda i: (i, 0, 0)),
        out_shape=jax.ShapeDtypeStruct((b, s, w), F32),
        compiler_params=_cparams("parallel"),
        name="small_prep",
    )(small, params)


def _hgrn2_kernel(q_ref, f_ref, i_ref, g_ref, lb_ref, o_ref, st_ref, qs, ks, vs, bs, os_):
    T = q_ref.shape[1]
    C = HGRN_CHUNK

    @pl.when(pl.program_id(1) == 0)
    def _():
        st_ref[...] = jnp.zeros_like(st_ref)

    same_head = _head_ones()
    ones_bf = jnp.where(same_head, 1.0, 0.0).astype(BF16)
    lb = lb_ref[...]
    fl = f_ref[0]
    log_lb = jnp.log(lb)
    c2 = jnp.log1p(-lb) + _log_sigmoid(fl)
    mx = jnp.maximum(log_lb, c2)
    log_f = mx + jnp.log1p(jnp.exp(-jnp.abs(log_lb - c2)))
    r = _iota((T, T), 0)
    c = _iota((T, T), 1)
    tril_chunk = jnp.where((r >= c) & (r // C == c // C), 1.0, 0.0).astype(BF16)
    bs[...] = _dot_exact_lhs(tril_chunk, log_f)
    qs[...] = _silu(q_ref[0])
    ks[...] = (1.0 - lb) * _sigmoid(-fl)
    vs[...] = i_ref[0]

    trow = _iota((C, GROUP_WIDTH), 0)

    def chunk(ci, carry):
        r0 = pl.multiple_of(ci * C, C)
        qc = qs[pl.ds(r0, C), :]
        kc = ks[pl.ds(r0, C), :]
        vc = vs[pl.ds(r0, C), :]
        bc = bs[pl.ds(r0, C), :]
        st = st_ref[...]
        o = _dot_nt((qc * jnp.exp(bc)).astype(BF16), st.astype(BF16))
        parts = []
        for s in range(C):
            m = trow >= s
            d = jnp.where(m, bc - bc[s:s + 1, :], 0.0)
            parts.append(jnp.where(m, jnp.exp(d) * (qc * kc[s:s + 1, :]), 0.0))
        a = jnp.concatenate(parts, axis=0).astype(BF16)
        sc = _dot(a, ones_bf)
        for s in range(C):
            o = o + sc[s * C:(s + 1) * C, :] * vc[s:s + 1, :]
        os_[pl.ds(r0, C), :] = o
        b_last = bc[C - 1:C, :]
        kd = kc * jnp.exp(b_last - bc)
        upd = _dot_tn(vc.astype(BF16), kd.astype(BF16))
        st_ref[...] = st * jnp.exp(b_last) + jnp.where(same_head, upd, 0.0)
        return carry

    lax.fori_loop(0, T // C, chunk, 0)
    o_ref[0] = _head_rms(os_[...], ones_bf) * _silu(g_ref[0])


def hgrn2(proj3, lb):
    b, s, _ = proj3.shape
    t = SEQ_TILE
    col = lambda cidx: pl.BlockSpec((1, t, GROUP_WIDTH), lambda i, j: (i, j, cidx))
    return pl.pallas_call(
        _hgrn2_kernel,
        grid=(b, s // t),
        in_specs=[col(COL_AQ), col(COL_AF), col(COL_AI), col(COL_AG),
                  pl.BlockSpec((1, GROUP_WIDTH), lambda i, j: (0, 0))],
        out_specs=pl.BlockSpec((1, t, GROUP_WIDTH), lambda i, j: (i, j, 0)),
        out_shape=jax.ShapeDtypeStruct((b, s, GROUP_WIDTH), F32),
        scratch_shapes=[pltpu.VMEM((GROUP_WIDTH, GROUP_WIDTH), F32)] + [pltpu.VMEM((t, GROUP_WIDTH), F32)] * 5,
        compiler_params=_cparams("parallel", "arbitrary"),
        name="hgrn2",
    )(proj3, proj3, proj3, proj3, lb.reshape(1, GROUP_WIDTH))


def _fox_kernel(q_ref, k_ref, v_ref, cq_ref, ck_ref, o_ref, acc_ref, m_ref, l_ref):
    tq = q_ref.shape[1]
    tk = tq
    qi = pl.program_id(1)
    scale = HEAD_DIM ** -0.5
    qs = _stack_heads(q_ref[0] * scale).astype(BF16)
    cq = cq_ref[0]
    cq_col = jnp.concatenate([cq[:, LANE_FOX + h:LANE_FOX + h + 1] for h in range(HEADS)], axis=0)
    rows = _iota((HEADS * tq, tk), 0)
    row_t = rows % tq
    row_h = rows // tq
    cols = _iota((HEADS * tq, tk), 1)

    acc_ref[...] = jnp.zeros_like(acc_ref)
    m_ref[...] = jnp.full_like(m_ref, -jnp.inf)
    l_ref[...] = jnp.zeros_like(l_ref)

    def step(kb, masked):
        k0 = pl.multiple_of(kb * tk, tk)
        kblk = k_ref[0, pl.ds(k0, tk), :].astype(BF16)
        vblk = v_ref[0, pl.ds(k0, tk), :].astype(BF16)
        ck = ck_ref[0, :, pl.ds(k0, tk)]
        ck_rows = jnp.zeros((HEADS * tq, tk), F32)
        for h in range(HEADS):
            ck_rows = jnp.where(row_h == h, ck[h:h + 1, :], ck_rows)
        s = _dot_nt(qs, kblk) + (cq_col - ck_rows)
        if masked:
            s = jnp.where(row_t >= cols, s, -jnp.inf)
        m_old = m_ref[...]
        m_new = jnp.maximum(m_old, jnp.max(s, axis=-1, keepdims=True))
        alpha = jnp.exp(m_old - m_new)
        p = jnp.exp(s - m_new)
        l_ref[...] = alpha * l_ref[...] + jnp.sum(p, axis=-1, keepdims=True)
        acc_ref[...] = alpha * acc_ref[...] + _dot(p.astype(BF16), vblk)
        m_ref[...] = m_new

    def body(kb, carry):
        step(kb, False)
        return carry

    lax.fori_loop(0, qi, body, 0)
    step(qi, True)

    acc = acc_ref[...] / l_ref[...]
    lane_head = _iota((tq, GROUP_WIDTH), 1) // HEAD_DIM
    o = jnp.zeros((tq, GROUP_WIDTH), F32)
    for h in range(HEADS):
        o = jnp.where(lane_head == h, acc[h * tq:(h + 1) * tq], o)
    ones_bf = jnp.where(_head_ones(), 1.0, 0.0).astype(BF16)
    o_ref[0] = _head_rms(o, ones_bf)


def fox_attention(proj3, sm, ck_row):
    b, s, _ = proj3.shape
    tq = SEQ_TILE
    return pl.pallas_call(
        _fox_kernel,
        grid=(b, s // tq),
        in_specs=[pl.BlockSpec((1, tq, GROUP_WIDTH), lambda i, j: (i, j, COL_BQ)),
                  pl.BlockSpec((1, s, GROUP_WIDTH), lambda i, j: (i, 0, COL_BK)),
                  pl.BlockSpec((1, s, GROUP_WIDTH), lambda i, j: (i, 0, COL_BV)),
                  pl.BlockSpec((1, tq, SMALL_W), lambda i, j: (i, j, 0)),
                  pl.BlockSpec((1, 8, s), lambda i, j: (i, 0, 0))],
        out_specs=pl.BlockSpec((1, tq, GROUP_WIDTH), lambda i, j: (i, j, 0)),
        out_shape=jax.ShapeDtypeStruct((b, s, GROUP_WIDTH), F32),
        scratch_shapes=[pltpu.VMEM((HEADS * tq, GROUP_WIDTH), F32),
                        pltpu.VMEM((HEADS * tq, 1), F32),
                        pltpu.VMEM((HEADS * tq, 1), F32)],
        compiler_params=_cparams("parallel", "arbitrary"),
        name="fox_attention",
    )(proj3, proj3, proj3, sm, ck_row)


def _lru_gates_kernel(x_ref, cw_ref, cb_ref, wa_ref, ba_ref, wx_ref, bx_ref, lam_ref, a_ref, u_ref, prev_ref):
    @pl.when(pl.program_id(1) == 0)
    def _():
        prev_ref[...] = jnp.zeros_like(prev_ref)

    x = x_ref[0]
    t = x.shape[0]
    xc = _causal_conv4(x, prev_ref[...], cw_ref[...]) + cb_ref[...]
    prev_ref[...] = x[t - 8:t, :]
    xb = xc.astype(BF16)
    r = _sigmoid(_dot(xb, wa_ref[...]) + ba_ref[...])
    ig = _sigmoid(_dot(xb, wx_ref[...]) + bx_ref[...])
    log_a = (-LRU_C * r) * _softplus(-lam_ref[...])
    a_ref[...] = jnp.exp(log_a)
    u_ref[...] = jnp.sqrt(1.0 - jnp.exp(2.0 * log_a)) * (ig * xc)


def lru_gates(proj3, conv_w, conv_b, wa_bd, ba, wx_bd, bx, lam):
    b, s, _ = proj3.shape
    t = SEQ_TILE
    gw = GROUP_WIDTH
    row = lambda: pl.BlockSpec((1, gw), lambda i, j: (0, 0))
    out = pl.BlockSpec((t, gw), lambda i, j: (j, i))
    return pl.pallas_call(
        _lru_gates_kernel,
        grid=(b, s // t),
        in_specs=[pl.BlockSpec((1, t, gw), lambda i, j: (i, j, COL_CX)),
                  pl.BlockSpec((CONV_W, gw), lambda i, j: (0, 0)), row(),
                  pl.BlockSpec((gw, gw), lambda i, j: (0, 0)), row(),
                  pl.BlockSpec((gw, gw), lambda i, j: (0, 0)), row(), row()],
        out_specs=[out, out],
        out_shape=[jax.ShapeDtypeStruct((s, b * gw), F32)] * 2,
        scratch_shapes=[pltpu.VMEM((8, gw), F32)],
        compiler_params=_cparams("parallel", "arbitrary"),
        name="lru_gates",
    )(proj3, conv_w, conv_b.reshape(1, gw), wa_bd, ba.reshape(1, gw), wx_bd, bx.reshape(1, gw), lam.reshape(1, gw))


def _lru_scan_kernel(a_ref, u_ref, h_ref, carry_ref):
    @pl.when(pl.program_id(0) == 0)
    def _():
        carry_ref[...] = jnp.zeros_like(carry_ref)

    n = a_ref.shape[0]

    def body(t, h):
        h = a_ref[t] * h + u_ref[t]
        h_ref[t] = h
        return h

    carry_ref[...] = lax.fori_loop(0, n, body, carry_ref[...], unroll=8)


def lru_scan(a_t, u_t, batch):
    s = a_t.shape[0]
    gw = GROUP_WIDTH
    t = SEQ_TILE
    a3 = a_t.reshape(s, batch, gw)
    u3 = u_t.reshape(s, batch, gw)
    spec = pl.BlockSpec((t, batch, gw), lambda i: (i, 0, 0))
    h = pl.pallas_call(
        _lru_scan_kernel,
        grid=(s // t,),
        in_specs=[spec, spec],
        out_specs=spec,
        out_shape=jax.ShapeDtypeStruct((s, batch, gw), F32),
        scratch_shapes=[pltpu.VMEM((batch, gw), F32)],
        compiler_params=_cparams("arbitrary"),
        name="lru_scan",
    )(a3, u3)
    return h.reshape(s, batch * gw)


def _neumann_inverse(m):
    n = m.shape[0]
    eye = jnp.where(_iota((n, n), 0) == _iota((n, n), 1), 1.0, 0.0)
    x = eye - m
    y = m
    for _ in range(5):
        yb = y.astype(BF16)
        y = _dot(yb, yb)
        x = x + _dot(x.astype(BF16), y.astype(BF16))
    return x


def _gdn_kernel(q_ref, k_ref, v_ref, z_ref, sm_ref, gr_ref, cw_ref, o_ref, st_ref, pq_ref, pk_ref, pv_ref):
    T = q_ref.shape[1]
    C = GDN_CHUNK

    @pl.when(pl.program_id(1) == 0)
    def _():
        st_ref[...] = jnp.zeros_like(st_ref)
        pq_ref[...] = jnp.zeros_like(pq_ref)
        pk_ref[...] = jnp.zeros_like(pk_ref)
        pv_ref[...] = jnp.zeros_like(pv_ref)

    same_head = _head_ones()
    ones_bf = jnp.where(same_head, 1.0, 0.0).astype(BF16)
    cw = cw_ref[...]
    gw = GROUP_WIDTH
    xq, xk, xv = q_ref[0], k_ref[0], v_ref[0]
    q = _silu(_causal_conv4(xq, pq_ref[...], cw[:, 0:gw]))
    k = _silu(_causal_conv4(xk, pk_ref[...], cw[:, gw:2 * gw]))
    v = _silu(_causal_conv4(xv, pv_ref[...], cw[:, 2 * gw:3 * gw]))
    pq_ref[...] = xq[T - 8:T, :]
    pk_ref[...] = xk[T - 8:T, :]
    pv_ref[...] = xv[T - 8:T, :]
    q = q * lax.rsqrt(_dot_exact_rhs(q * q, ones_bf) + EPS) * (HEAD_DIM ** -0.5)
    k = k * lax.rsqrt(_dot_exact_rhs(k * k, ones_bf) + EPS)

    sm = sm_ref[0]
    er = _iota((SMALL_W, gw), 0)
    ec = _iota((SMALL_W, gw), 1) // HEAD_DIM
    exp_beta = jnp.where(er == LANE_BETA + ec, 1.0, 0.0).astype(BF16)
    exp_g = jnp.where(er == LANE_GDEC + ec, 1.0, 0.0).astype(BF16)
    betax = _dot_exact_rhs(sm, exp_beta)
    gx = _dot_exact_rhs(sm, exp_g)

    n = HEADS * C
    rr = _iota((n, n), 0)
    cc = _iota((n, n), 1)
    incl = same_head & (rr % C >= cc % C)
    strict = same_head & (rr % C > cc % C)

    outs = []
    for ci in range(T // C):
        sl = slice(ci * C, (ci + 1) * C)
        qc, kc, vc, bx, gc = q[sl], k[sl], v[sl], betax[sl], gx[sl]
        grow = gr_ref[0, ci, 0:1, :]
        eg = jnp.exp(gc)
        kb = kc * bx
        ks = _stack_heads(kc)
        gcol = _dot_exact_rhs(_stack_heads(gc), ones_bf) * (1.0 / HEAD_DIM)
        diff = gcol - grow
        gamma = jnp.where(incl, jnp.exp(jnp.where(incl, diff, 0.0)), 0.0)
        ks_bf = ks.astype(BF16)
        m = jnp.where(strict, _dot_nt(_stack_heads(kb).astype(BF16), ks_bf) * gamma, 0.0)
        a_qk = _dot_nt(_stack_heads(qc).astype(BF16), ks_bf) * gamma
        t_inv = _neumann_inverse(m).astype(BF16)
        u = _dot(t_inv, _stack_heads(vc * bx).astype(BF16))
        wk = _dot(t_inv, _stack_heads(kb * eg).astype(BF16))
        st = st_ref[...]
        st_bf = st.astype(BF16)
        v_new = u - _dot(wk.astype(BF16), st_bf)
        o = _dot(_stack_heads(qc * eg).astype(BF16), st_bf) + _dot(a_qk.astype(BF16), v_new.astype(BF16))
        outs.append(_unstack_heads(o, C))
        g_last = gc[C - 1:C, :]
        k_dec = _stack_heads(kc * jnp.exp(g_last - gc))
        st_ref[...] = st * jnp.exp(g_last) + _dot_tn(k_dec.astype(BF16), v_new.astype(BF16))
    o = jnp.concatenate(outs, axis=0)
    o_ref[0] = _head_rms(o, ones_bf) * _silu(z_ref[0])


def gdn(proj3, sm, g_row, conv_w):
    b, s, _ = proj3.shape
    t = SEQ_TILE
    gw = GROUP_WIDTH
    col = lambda cidx: pl.BlockSpec((1, t, gw), lambda i, j: (i, j, cidx))
    return pl.pallas_call(
        _gdn_kernel,
        grid=(b, s // t),
        in_specs=[col(COL_DQ), col(COL_DK), col(COL_DV), col(COL_DZ),
                  pl.BlockSpec((1, t, SMALL_W), lambda i, j: (i, j, 0)),
                  pl.BlockSpec((1, t // GDN_CHUNK, 8, gw), lambda i, j: (i, j, 0, 0)),
                  pl.BlockSpec((CONV_W, 3 * gw), lambda i, j: (0, 0))],
        out_specs=pl.BlockSpec((1, t, gw), lambda i, j: (i, j, 0)),
        out_shape=jax.ShapeDtypeStruct((b, s, gw), F32),
        scratch_shapes=[pltpu.VMEM((gw, gw), F32)] + [pltpu.VMEM((8, gw), F32)] * 3,
        compiler_params=_cparams("parallel", "arbitrary"),
        name="gdn",
    )(proj3, proj3, proj3, proj3, sm, g_row, conv_w)


def _mix_out_kernel(x_ref, ya_ref, yb_ref, h_ref, cg_ref, yd_ref, gain_ref, w_ref, o_ref):
    gw = GROUP_WIDTH
    ones_bf = jnp.where(_head_ones(), 1.0, 0.0).astype(BF16)
    yc = _head_rms(h_ref[...] * _gelu_tanh(cg_ref[0]), ones_bf)
    gain = gain_ref[...]
    acc = x_ref[0]
    for i, y in enumerate((ya_ref[0], yb_ref[0], yc, yd_ref[0])):
        yg = (y * gain[:, i * gw:(i + 1) * gw]).astype(BF16)
        acc = acc + _dot(yg, w_ref[i * gw:(i + 1) * gw, :])
    o_ref[0] = acc


def mix_out(x3, ya, yb, h_t, proj3, yd, gain, w_out_bf16):
    b, s, d = x3.shape
    t = SEQ_TILE
    gw = GROUP_WIDTH
    grp = lambda: pl.BlockSpec((1, t, gw), lambda i, j: (i, j, 0))
    return pl.pallas_call(
        _mix_out_kernel,
        grid=(b, s // t),
        in_specs=[pl.BlockSpec((1, t, d), lambda i, j: (i, j, 0)),
                  grp(), grp(),
                  pl.BlockSpec((t, gw), lambda i, j: (j, i)),
                  pl.BlockSpec((1, t, gw), lambda i, j: (i, j, COL_CG)),
                  grp(),
                  pl.BlockSpec((1, 4 * gw), lambda i, j: (0, 0)),
                  pl.BlockSpec((4 * gw, d), lambda i, j: (0, 0))],
        out_specs=pl.BlockSpec((1, t, d), lambda i, j: (i, j, 0)),
        out_shape=jax.ShapeDtypeStruct((b, s, d), F32),
        compiler_params=_cparams("parallel", "parallel"),
        name="mix_out",
    )(x3, ya, yb, h_t, proj3, yd, gain.reshape(1, 4 * gw), w_out_bf16)


def _mem_attn_kernel(x_ref, g_ref, wq_ref, kv_ref, wo_ref, o_ref):
    x = x_ref[0]
    t = x.shape[0]
    gw = GROUP_WIDTH
    ms = jnp.mean(x * x, axis=-1, keepdims=True)
    h = ((x * lax.rsqrt(ms + EPS)) * g_ref[...]).astype(BF16)
    q = _dot(h, wq_ref[...])
    kv = kv_ref[0]
    k = kv[:, 0:gw].astype(BF16)
    v = kv[:, gw:2 * gw].astype(BF16)
    s = _dot_nt(_stack_heads(q).astype(BF16), k) * (HEAD_DIM ** -0.5)
    s = s - jnp.max(s, axis=-1, keepdims=True)
    p = jnp.exp(s)
    p = p / jnp.sum(p, axis=-1, keepdims=True)
    o = _dot(p.astype(BF16), v)
    lane_head = _iota((t, gw), 1) // HEAD_DIM
    oc = jnp.zeros((t, gw), F32)
    for hh in range(HEADS):
        oc = jnp.where(lane_head == hh, o[hh * t:(hh + 1) * t], oc)
    o_ref[0] = x + _dot(oc.astype(BF16), wo_ref[...])


def mem_attention(x3, g, wq_bf16, kv3, wo_bf16):
    b, s, d = x3.shape
    t = SEQ_TILE
    m = kv3.shape[1]
    return pl.pallas_call(
        _mem_attn_kernel,
        grid=(b, s // t),
        in_specs=[pl.BlockSpec((1, t, d), lambda i, j: (i, j, 0)),
                  pl.BlockSpec((1, d), lambda i, j: (0, 0)),
                  pl.BlockSpec((d, GROUP_WIDTH), lambda i, j: (0, 0)),
                  pl.BlockSpec((1, m, 2 * GROUP_WIDTH), lambda i, j: (i, 0, 0)),
                  pl.BlockSpec((GROUP_WIDTH, d), lambda i, j: (0, 0))],
        out_specs=pl.BlockSpec((1, t, d), lambda i, j: (i, j, 0)),
        out_shape=jax.ShapeDtypeStruct((b, s, d), F32),
        compiler_params=_cparams("parallel", "parallel"),
        name="mem_attention",
    )(x3, g.reshape(1, d), wq_bf16, kv3, wo_bf16)


def _router_kernel(x_ref, g_ref, w_ref, b_ref, h_ref, c_ref):
    x = x_ref[...]
    ms = jnp.mean(x * x, axis=-1, keepdims=True)
    h = (x * lax.rsqrt(ms + EPS)) * g_ref[...]
    h_ref[...] = h.astype(BF16)
    hh, hm, _ = _split3(h)
    w = w_ref[...]
    wh, wm, _ = _split3(w)
    logits = _dot(hh, wh) + _dot(hh, wm) + _dot(hm, wh) + b_ref[...]
    lane = _iota(logits.shape, 1)
    big = jnp.int32(1 << 30)
    neg = -jnp.inf
    is_grp = (lane >= N_EXPERTS) & (lane < N_EXPERTS + N_EXPERT_GROUPS)
    gl = jnp.where(is_grp, logits, neg)
    gmax = jnp.max(gl, axis=-1, keepdims=True)
    p_grp = 1.0 / jnp.sum(jnp.exp(gl - gmax), axis=-1, keepdims=True)
    g_sel = jnp.min(jnp.where(gl == gmax, lane, big), axis=-1, keepdims=True) - N_EXPERTS
    in_grp = (lane < N_EXPERTS) & (lane // EXPERTS_PER_GROUP == g_sel)
    el = jnp.where(in_grp, logits, neg)
    m1 = jnp.max(el, axis=-1, keepdims=True)
    i1 = jnp.min(jnp.where(el == m1, lane, big), axis=-1, keepdims=True)
    el2 = jnp.where(lane == i1, neg, el)
    m2 = jnp.max(el2, axis=-1, keepdims=True)
    i2 = jnp.min(jnp.where(el2 == m2, lane, big), axis=-1, keepdims=True)
    e21 = jnp.exp(m2 - m1)
    w1 = 1.0 / (1.0 + e21)
    w2 = e21 / (1.0 + e21)
    c_ref[...] = jnp.where(lane == i1, p_grp * w1, 0.0) + jnp.where(lane == i2, p_grp * w2, 0.0)


def router(x2, g, w_router, b_router, tm):
    m, d = x2.shape
    return pl.pallas_call(
        _router_kernel,
        grid=(m // tm,),
        in_specs=[pl.BlockSpec((tm, d), lambda i: (i, 0)),
                  pl.BlockSpec((1, d), lambda i: (0, 0)),
                  pl.BlockSpec((d, SMALL_W), lambda i: (0, 0)),
                  pl.BlockSpec((1, SMALL_W), lambda i: (0, 0))],
        out_specs=[pl.BlockSpec((tm, d), lambda i: (i, 0)),
                   pl.BlockSpec((tm, SMALL_W), lambda i: (i, 0))],
        out_shape=[jax.ShapeDtypeStruct((m, d), BF16), jax.ShapeDtypeStruct((m, SMALL_W), F32)],
        compiler_params=_cparams("parallel"),
        name="router",
    )(x2, g.reshape(1, d), w_router, b_router)


def _moe_dense_kernel(x_ref, h_ref, c_ref, wgu_ref, wdn_ref, o_ref):
    e = pl.program_id(1)

    @pl.when(e == 0)
    def _():
        o_ref[...] = x_ref[...]

    gu = _dot(h_ref[...], wgu_ref[0])
    c = c_ref[...]
    lane = _iota(c.shape, 1)
    ce = jnp.sum(jnp.where(lane == e, c, 0.0), axis=-1, keepdims=True)
    act = _silu(gu[:, 0:D_EXPERT]) * gu[:, D_EXPERT:2 * D_EXPERT] * ce
    o_ref[...] += _dot(act.astype(BF16), wdn_ref[0])


def moe_dense(x2, h_bf16, comb, wgu_bf16, wdn_bf16, tm):
    m, d = x2.shape
    ne = wgu_bf16.shape[0]
    return pl.pallas_call(
        _moe_dense_kernel,
        grid=(m // tm, ne),
        in_specs=[pl.BlockSpec((tm, d), lambda i, e: (i, 0)),
                  pl.BlockSpec((tm, d), lambda i, e: (i, 0)),
                  pl.BlockSpec((tm, SMALL_W), lambda i, e: (i, 0)),
                  pl.BlockSpec((1, d, 2 * D_EXPERT), lambda i, e: (e, 0, 0)),
                  pl.BlockSpec((1, D_EXPERT, d), lambda i, e: (e, 0, 0))],
        out_specs=pl.BlockSpec((tm, d), lambda i, e: (i, 0)),
        out_shape=jax.ShapeDtypeStruct((m, d), F32),
        compiler_params=_cparams("parallel", "arbitrary"),
        name="moe_dense",
    )(x2, h_bf16, comb, wgu_bf16, wdn_bf16)


def _rmsnorm_kernel(x_ref, g_ref, o_ref):
    x = x_ref[...]
    ms = jnp.mean(x * x, axis=-1, keepdims=True)
    o_ref[...] = (x * lax.rsqrt(ms + EPS)) * g_ref[...]


def rmsnorm(x2, g, tm):
    m, d = x2.shape
    return pl.pallas_call(
        _rmsnorm_kernel,
        grid=(m // tm,),
        in_specs=[pl.BlockSpec((tm, d), lambda i: (i, 0)), pl.BlockSpec((1, d), lambda i: (0, 0))],
        out_specs=pl.BlockSpec((tm, d), lambda i: (i, 0)),
        out_shape=jax.ShapeDtypeStruct((m, d), F32),
        compiler_params=_cparams("parallel"),
        name="rmsnorm",
    )(x2, g.reshape(1, d))


def _reorder_w_in(w):
    gw, h = GROUP_WIDTH, HEADS
    sizes = (gw, gw, gw, gw, gw, gw, gw, h, gw, gw, gw, gw, gw, h, h, gw)
    offs = [0]
    for sz in sizes:
        offs.append(offs[-1] + sz)
    seg = lambda i: w[:, offs[i]:offs[i + 1]]
    wide = [seg(i) for i in (0, 1, 2, 3, 4, 5, 6, 8, 9, 10, 11, 12, 15)]
    small = jnp.concatenate([seg(7), seg(13), seg(14)], axis=1)
    pad = jnp.zeros((w.shape[0], 2 * SMALL_W - small.shape[1]), w.dtype)
    return jnp.concatenate(wide + [small, pad], axis=1)


def _block_diag(w):
    h = w.shape[0]
    eye = jnp.eye(h, dtype=w.dtype)
    return (eye[:, None, :, None] * w[:, :, None, :]).reshape(h * HEAD_DIM, h * HEAD_DIM)


def kernel(x, mem, norm_mix, w_in, hgrn_lb, fox_fb, lru_conv_w, lru_conv_b, lru_wa, lru_ba, lru_wx, lru_bx, lru_lam, gdn_conv_w, gdn_a_log, gdn_dt_bias, mix_gain, w_out, norm_mem, norm_memkv, w_mq, w_mkv, w_mo, norm_ffn, w_rg, b_rg, w_re, b_re, w_e_gu, w_e_dn, norm_final):
    b, s, d = x.shape
    depth = w_in.shape[0]
    t_tok = b * s
    mlen = mem.shape[1]
    gw = GROUP_WIDTH

    lb_all = jnp.cumsum(jax.nn.softmax(hgrn_lb.astype(F32), axis=0), axis=0)
    lb_all = lb_all - lb_all[0]

    x = x.astype(F32)
    for l in range(depth):
        w_in_p = _reorder_w_in(w_in[l]).astype(BF16)
        proj = norm_matmul(x.reshape(t_tok, d), norm_mix[l], w_in_p, tm=256)
        proj3 = proj.reshape(b, s, N_PROJ)
        small = proj3[:, :, 13 * gw:13 * gw + SMALL_W]
        prm = jnp.zeros((8, SMALL_W), F32)
        prm = prm.at[0, LANE_FOX:LANE_FOX + HEADS].set(fox_fb[l].astype(F32))
        prm = prm.at[0, LANE_GDEC:LANE_GDEC + HEADS].set(gdn_dt_bias[l].astype(F32))
        prm = prm.at[1, LANE_GDEC:LANE_GDEC + HEADS].set(gdn_a_log[l].astype(F32))
        sm = small_prep(small, prm)
        ck_row = jnp.pad(jnp.swapaxes(sm[:, :, LANE_FOX:LANE_FOX + HEADS], 1, 2), ((0, 0), (0, 8 - HEADS), (0, 0)))
        g_row = sm[:, :, LANE_GDEC:LANE_GDEC + HEADS].reshape(b, s // GDN_CHUNK, GDN_CHUNK, HEADS)
        g_row = jnp.swapaxes(g_row, 2, 3).reshape(b, s // GDN_CHUNK, 1, gw)
        g_row = jnp.broadcast_to(g_row, (b, s // GDN_CHUNK, 8, gw))

        ya = hgrn2(proj3, lb_all[l])
        yb = fox_attention(proj3, sm, ck_row)
        a_t, u_t = lru_gates(proj3, lru_conv_w[l], lru_conv_b[l], _block_diag(lru_wa[l]).astype(BF16), lru_ba[l],
                             _block_diag(lru_wx[l]).astype(BF16), lru_bx[l], lru_lam[l])
        h_t = lru_scan(a_t, u_t, b)
        yd = gdn(proj3, sm, g_row, gdn_conv_w[l])
        x = mix_out(x, ya, yb, h_t, proj3, yd, mix_gain[l], w_out[l].astype(BF16))

        kv = norm_matmul(mem.reshape(b * mlen, d), norm_memkv[l], w_mkv[l].astype(BF16), tm=256)
        x = mem_attention(x, norm_mem[l], w_mq[l].astype(BF16), kv.reshape(b, mlen, 2 * gw), w_mo[l].astype(BF16))

        w_router = jnp.concatenate([w_re[l], w_rg[l], jnp.zeros((d, SMALL_W - N_EXPERTS - N_EXPERT_GROUPS), F32)], axis=1)
        b_router = jnp.concatenate([b_re[l], b_rg[l], jnp.zeros((SMALL_W - N_EXPERTS - N_EXPERT_GROUPS,), F32)]).reshape(1, SMALL_W)
        x2 = x.reshape(t_tok, d)
        h_bf, comb = router(x2, norm_ffn[l], w_router, b_router, tm=512)
        x = moe_dense(x2, h_bf, comb, w_e_gu[l].astype(BF16), w_e_dn[l].astype(BF16), tm=1024).reshape(b, s, d)

    return rmsnorm(x.reshape(t_tok, d), norm_final, tm=512).reshape(b, s, d)
```

```python
import functools
import math

import jax
import jax.numpy as jnp
from jax import lax
from jax.experimental import pallas as pl
from jax.experimental.pallas import tpu as pltpu

F32 = jnp.float32
BF16 = jnp.bfloat16

HEAD_DIM = 64
GROUP_WIDTH = 256
HEADS = GROUP_WIDTH // HEAD_DIM
GDN_CHUNK = 64
HGRN_CHUNK = 16
CONV_W = 4
LRU_C = 8.0
EPS = 1e-6
N_EXPERT_GROUPS = 4
EXPERTS_PER_GROUP = 8
N_EXPERTS = N_EXPERT_GROUPS * EXPERTS_PER_GROUP
D_EXPERT = 256
SMALL_W = 128
SEQ_TILE = 256
VMEM_LIMIT = 56 * 1024 * 1024

(COL_AQ, COL_AF, COL_AI, COL_AG, COL_BQ, COL_BK, COL_BV, COL_CX, COL_CG,
 COL_DQ, COL_DK, COL_DV, COL_DZ) = range(13)
N_PROJ = 13 * GROUP_WIDTH + 2 * SMALL_W
COL_SMALL = 13 * GROUP_WIDTH // SMALL_W
LANE_FOX = 0
LANE_BETA = 4
LANE_GDEC = 8


def _cparams(*sem):
    return pltpu.CompilerParams(dimension_semantics=sem, vmem_limit_bytes=VMEM_LIMIT)


def _dot(a, b):
    return jnp.dot(a, b, preferred_element_type=F32)


def _dot_nt(a, b):
    return lax.dot_general(a, b, (((1,), (1,)), ((), ())), preferred_element_type=F32)


def _dot_tn(a, b):
    return lax.dot_general(a, b, (((0,), (0,)), ((), ())), preferred_element_type=F32)


def _split3(x):
    h = x.astype(BF16)
    r = x - h.astype(F32)
    m = r.astype(BF16)
    l = (r - m.astype(F32)).astype(BF16)
    return h, m, l


def _dot_exact_rhs(x, w_bf16):
    h, m, l = _split3(x)
    return _dot(h, w_bf16) + _dot(m, w_bf16) + _dot(l, w_bf16)


def _dot_exact_lhs(w_bf16, x):
    h, m, l = _split3(x)
    return _dot(w_bf16, h) + _dot(w_bf16, m) + _dot(w_bf16, l)


def _iota(shape, dim):
    return lax.broadcasted_iota(jnp.int32, shape, dim)


def _head_ones(n=GROUP_WIDTH):
    r = _iota((n, n), 0) // HEAD_DIM
    c = _iota((n, n), 1) // HEAD_DIM
    return r == c


def _sigmoid(x):
    return 1.0 / (1.0 + jnp.exp(-x))


def _silu(x):
    return x * _sigmoid(x)


def _log_sigmoid(x):
    return jnp.minimum(x, 0.0) - jnp.log1p(jnp.exp(-jnp.abs(x)))


def _softplus(x):
    return jnp.maximum(x, 0.0) + jnp.log1p(jnp.exp(-jnp.abs(x)))


def _gelu_tanh(x):
    return 0.5 * x * (1.0 + jnp.tanh(math.sqrt(2.0 / math.pi) * (x + 0.044715 * (x * x * x))))


def _head_mean_sq(x, ones_bf16):
    return _dot_exact_rhs(x * x, ones_bf16) * (1.0 / HEAD_DIM)


def _head_rms(x, ones_bf16):
    return x * lax.rsqrt(_head_mean_sq(x, ones_bf16) + EPS)


def _stack_heads(x):
    lane_head = _iota(x.shape, 1) // HEAD_DIM
    parts = []
    for h in range(HEADS):
        parts.append(jnp.where(lane_head == h, x, 0.0))
    return jnp.concatenate(parts, axis=0)


def _unstack_heads(xs, rows):
    out = xs[0:rows]
    for h in range(1, HEADS):
        out = out + xs[h * rows:(h + 1) * rows]
    return out


def _causal_conv4(x, prev8, w):
    r = x.shape[0]
    row8 = _iota((8, x.shape[1]), 0)
    acc = x * w[CONV_W - 1:CONV_W, :]
    for k in range(1, CONV_W):
        xs = pltpu.roll(x, k, 0)
        ps = pltpu.roll(prev8, k, 0)
        top = jnp.where(row8 < k, ps, xs[0:8])
        xs = jnp.concatenate([top, xs[8:r]], axis=0)
        acc = acc + xs * w[CONV_W - 1 - k:CONV_W - k, :]
    return acc


def _norm_matmul_kernel(x_ref, g_ref, w_ref, o_ref):
    x = x_ref[...]
    ms = jnp.mean(x * x, axis=-1, keepdims=True)
    h = (x * lax.rsqrt(ms + EPS)) * g_ref[...]
    o_ref[...] = _dot(h.astype(BF16), w_ref[...]).astype(o_ref.dtype)


def norm_matmul(x, g, w_bf16, tm, out_dtype=F32):
    m, d = x.shape
    n = w_bf16.shape[1]
    return pl.pallas_call(
        _norm_matmul_kernel,
        grid=(m // tm,),
        in_specs=[pl.BlockSpec((tm, d), lambda i: (i, 0)),
                  pl.BlockSpec((1, d), lambda i: (0, 0)),
                  pl.BlockSpec((d, n), lambda i: (0, 0))],
        out_specs=pl.BlockSpec((tm, n), lambda i: (i, 0)),
        out_shape=jax.ShapeDtypeStruct((m, n), out_dtype),
        compiler_params=_cparams("parallel"),
        name="norm_matmul",
    )(x, g.reshape(1, d), w_bf16)


def _small_prep_kernel(s_ref, p_ref, o_ref, x_ref):
    S = s_ref.shape[1]
    W = s_ref.shape[2]
    blk = SEQ_TILE
    lane = _iota((blk, W), 1)
    er = _iota((W, HEADS * W), 0)
    ec = _iota((W, HEADS * W), 1) // W
    expand_fox = jnp.where(er == LANE_FOX + ec, 1.0, 0.0).astype(BF16)
    is_fox = (lane >= LANE_FOX) & (lane < LANE_FOX + HEADS)
    is_beta = (lane >= LANE_BETA) & (lane < LANE_BETA + HEADS)
    is_gdec = (lane >= LANE_GDEC) & (lane < LANE_GDEC + HEADS)
    r = _iota((blk, blk), 0)
    c = _iota((blk, blk), 1)
    tril_all = jnp.where(r >= c, 1.0, 0.0).astype(BF16)
    tril_chunk = jnp.where((r >= c) & (r // GDN_CHUNK == c // GDN_CHUNK), 1.0, 0.0).astype(BF16)
    neg_exp_a = -jnp.exp(p_ref[1:2, :])
    carry = jnp.zeros((1, W), F32)
    for i in range(S // blk):
        sl = pl.ds(i * blk, blk)
        z = s_ref[0, sl, :] + p_ref[0:1, :]
        fox = jnp.where(is_fox, _log_sigmoid(z), 0.0)
        beta = jnp.where(is_beta, _sigmoid(z), 0.0)
        gdec = jnp.where(is_gdec, neg_exp_a * _softplus(z), 0.0)
        cf = _dot_exact_lhs(tril_all, fox) + carry
        cg = _dot_exact_lhs(tril_chunk, gdec)
        carry = cf[blk - 1:blk, :]
        o_ref[0, sl, :] = cf + cg + beta
        x_ref[0, sl, :] = _dot_exact_rhs(cf, expand_fox)


def small_prep(proj3, params):
    b, s, _ = proj3.shape
    w = SMALL_W
    return pl.pallas_call(
        _small_prep_kernel,
        grid=(b,),
        in_specs=[pl.BlockSpec((1, s, w), lambda i: (i, 0, COL_SMALL)),
                  pl.BlockSpec((8, w), lambda i: (0, 0))],
        out_specs=[pl.BlockSpec((1, s, w), lambda i: (i, 0, 0)),
                   pl.BlockSpec((1, s, HEADS * w), lambda i: (i, 0, 0))],
        out_shape=[jax.ShapeDtypeStruct((b, s, w), F32), jax.ShapeDtypeStruct((b, s, HEADS * w), F32)],
        compiler_params=_cparams("parallel"),
        name="small_prep",
    )(proj3, params)


def _hgrn2_kernel(q_ref, f_ref, i_ref, g_ref, lb_ref, o_ref, st_ref, qs, ks, vs, bs, os_):
    T = q_ref.shape[1]
    C = HGRN_CHUNK

    @pl.when(pl.program_id(1) == 0)
    def _():
        st_ref[...] = jnp.zeros_like(st_ref)

    same_head = _head_ones()
    ones_bf = jnp.where(same_head, 1.0, 0.0).astype(BF16)
    lb = lb_ref[...]
    fl = f_ref[0]
    log_lb = jnp.log(lb)
    c2 = jnp.log1p(-lb) + _log_sigmoid(fl)
    mx = jnp.maximum(log_lb, c2)
    log_f = mx + jnp.log1p(jnp.exp(-jnp.abs(log_lb - c2)))
    r = _iota((T, T), 0)
    c = _iota((T, T), 1)
    tril_chunk = jnp.where((r >= c) & (r // C == c // C), 1.0, 0.0).astype(BF16)
    bs[...] = _dot_exact_lhs(tril_chunk, log_f)
    qs[...] = _silu(q_ref[0])
    ks[...] = (1.0 - lb) * _sigmoid(-fl)
    vs[...] = i_ref[0]

    trow = _iota((C, GROUP_WIDTH), 0)

    def chunk(ci, carry):
        r0 = pl.multiple_of(ci * C, C)
        qc = qs[pl.ds(r0, C), :]
        kc = ks[pl.ds(r0, C), :]
        vc = vs[pl.ds(r0, C), :]
        bc = bs[pl.ds(r0, C), :]
        st = st_ref[...]
        o = _dot_nt((qc * jnp.exp(bc)).astype(BF16), st.astype(BF16))
        parts = []
        for s in range(C):
            m = trow >= s
            d = jnp.where(m, bc - bc[s:s + 1, :], 0.0)
            parts.append(jnp.where(m, jnp.exp(d) * (qc * kc[s:s + 1, :]), 0.0))
        a = jnp.concatenate(parts, axis=0).astype(BF16)
        sc = _dot(a, ones_bf)
        for s in range(C):
            o = o + sc[s * C:(s + 1) * C, :] * vc[s:s + 1, :]
        os_[pl.ds(r0, C), :] = o
        b_last = bc[C - 1:C, :]
        kd = kc * jnp.exp(b_last - bc)
        upd = _dot_tn(vc.astype(BF16), kd.astype(BF16))
        st_ref[...] = st * jnp.exp(b_last) + jnp.where(same_head, upd, 0.0)
        return carry

    lax.fori_loop(0, T // C, chunk, 0)
    o_ref[0] = _head_rms(os_[...], ones_bf) * _silu(g_ref[0])


def hgrn2(proj3, lb):
    b, s, _ = proj3.shape
    t = SEQ_TILE
    col = lambda cidx: pl.BlockSpec((1, t, GROUP_WIDTH), lambda i, j: (i, j, cidx))
    return pl.pallas_call(
        _hgrn2_kernel,
        grid=(b, s // t),
        in_specs=[col(COL_AQ), col(COL_AF), col(COL_AI), col(COL_AG),
                  pl.BlockSpec((1, GROUP_WIDTH), lambda i, j: (0, 0))],
        out_specs=pl.BlockSpec((1, t, GROUP_WIDTH), lambda i, j: (i, j, 0)),
        out_shape=jax.ShapeDtypeStruct((b, s, GROUP_WIDTH), F32),
        scratch_shapes=[pltpu.VMEM((GROUP_WIDTH, GROUP_WIDTH), F32)] + [pltpu.VMEM((t, GROUP_WIDTH), F32)] * 5,
        compiler_params=_cparams("parallel", "arbitrary"),
        name="hgrn2",
    )(proj3, proj3, proj3, proj3, lb.reshape(1, GROUP_WIDTH))


def _fox_kernel(q_ref, k_ref, v_ref, cr_ref, cx_ref, o_ref, kb_ref, vt_ref, acc_ref):
    tq = q_ref.shape[1]
    tk = tq
    S = k_ref.shape[1]
    n = HEADS * tq
    qi = pl.program_id(1)

    @pl.when(qi == 0)
    def _():
        for i in range(S // tk):
            sl = pl.ds(i * tk, tk)
            kb_ref[sl, :] = k_ref[0, sl, :].astype(BF16)
            vt_ref[:, sl] = v_ref[0, sl, :].T.astype(BF16)

    qs = _stack_heads(q_ref[0] * (HEAD_DIM ** -0.5)).astype(BF16)
    q0 = pl.multiple_of(qi * tq, tq)
    cr = cr_ref[0, :, pl.ds(q0, tq)]
    cq_row = jnp.concatenate([cr[h:h + 1, :] for h in range(HEADS)], axis=1)
    acc_ref[...] = jnp.zeros_like(acc_ref)
    reps = tq // SMALL_W

    def step(kb, m_old, l_old, masked):
        k0 = pl.multiple_of(kb * tk, tk)
        st = _dot_nt(kb_ref[pl.ds(k0, tk), :], qs)
        cx = cx_ref[0, pl.ds(k0, tk), :]
        ck = jnp.concatenate([cx[:, h * SMALL_W:(h + 1) * SMALL_W] for h in range(HEADS) for _ in range(reps)], axis=1)
        st = st + (cq_row - ck)
        if masked:
            st = jnp.where(_iota((tk, n), 0) <= _iota((tk, n), 1) % tq, st, -jnp.inf)
        m_new = jnp.maximum(m_old, jnp.max(st, axis=0, keepdims=True))
        alpha = jnp.exp(m_old - m_new)
        p = jnp.exp(st - m_new)
        l_new = alpha * l_old + jnp.sum(p, axis=0, keepdims=True)
        acc_ref[...] = alpha * acc_ref[...] + _dot(vt_ref[:, pl.ds(k0, tk)], p.astype(BF16))
        return m_new, l_new

    m0 = jnp.full((1, n), -jnp.inf, F32)
    l0 = jnp.zeros((1, n), F32)
    m1, l1 = lax.fori_loop(0, qi, lambda kb, c: step(kb, c[0], c[1], False), (m0, l0))
    _, l2 = step(qi, m1, l1, True)

    acc = acc_ref[...] * (1.0 / l2)
    ot = jnp.concatenate([acc[h * HEAD_DIM:(h + 1) * HEAD_DIM, h * tq:(h + 1) * tq] for h in range(HEADS)], axis=0)
    ones_bf = jnp.where(_head_ones(), 1.0, 0.0).astype(BF16)
    o_ref[0] = _head_rms(ot.T, ones_bf)


def fox_attention(proj3, c_row, c_exp):
    b, s, _ = proj3.shape
    tq = SEQ_TILE
    gw = GROUP_WIDTH
    return pl.pallas_call(
        _fox_kernel,
        grid=(b, s // tq),
        in_specs=[pl.BlockSpec((1, tq, gw), lambda i, j: (i, j, COL_BQ)),
                  pl.BlockSpec((1, s, gw), lambda i, j: (i, 0, COL_BK)),
                  pl.BlockSpec((1, s, gw), lambda i, j: (i, 0, COL_BV)),
                  pl.BlockSpec((1, 8, s), lambda i, j: (i, 0, 0)),
                  pl.BlockSpec((1, s, HEADS * SMALL_W), lambda i, j: (i, 0, 0))],
        out_specs=pl.BlockSpec((1, tq, gw), lambda i, j: (i, j, 0)),
        out_shape=jax.ShapeDtypeStruct((b, s, gw), F32),
        scratch_shapes=[pltpu.VMEM((s, gw), BF16),
                        pltpu.VMEM((gw, s), BF16),
                        pltpu.VMEM((gw, HEADS * tq), F32)],
        compiler_params=_cparams("parallel", "arbitrary"),
        name="fox_attention",
    )(proj3, proj3, proj3, c_row, c_exp)


def _rglru_kernel(x_ref, g_ref, cw_ref, cb_ref, wa_ref, ba_ref, wx_ref, bx_ref, lam_ref, o_ref, prev_ref, h_ref):
    @pl.when(pl.program_id(1) == 0)
    def _():
        prev_ref[...] = jnp.zeros_like(prev_ref)
        h_ref[...] = jnp.zeros_like(h_ref)

    x = x_ref[0]
    t = x.shape[0]
    xc = _causal_conv4(x, prev_ref[...], cw_ref[...]) + cb_ref[...]
    prev_ref[...] = x[t - 8:t, :]
    xb = xc.astype(BF16)
    r = _sigmoid(_dot(xb, wa_ref[...]) + ba_ref[...])
    ig = _sigmoid(_dot(xb, wx_ref[...]) + bx_ref[...])
    log_a = (-LRU_C * r) * _softplus(-lam_ref[...])
    a = jnp.exp(log_a)
    u = jnp.sqrt(1.0 - jnp.exp(2.0 * log_a)) * (ig * xc)
    row = _iota(a.shape, 0)
    d = 1
    while d < t:
        valid = row >= d
        u = jnp.where(valid, a * pltpu.roll(u, d, 0) + u, u)
        a = jnp.where(valid, a * pltpu.roll(a, d, 0), a)
        d *= 2
    h = a * h_ref[...] + u
    h_ref[...] = h[t - 1:t, :]
    ones_bf = jnp.where(_head_ones(), 1.0, 0.0).astype(BF16)
    o_ref[0] = _head_rms(h * _gelu_tanh(g_ref[0]), ones_bf)


def rglru(proj3, conv_w, conv_b, wa_bd, ba, wx_bd, bx, lam):
    b, s, _ = proj3.shape
    t = SEQ_TILE
    gw = GROUP_WIDTH
    row = lambda: pl.BlockSpec((1, gw), lambda i, j: (0, 0))
    return pl.pallas_call(
        _rglru_kernel,
        grid=(b, s // t),
        in_specs=[pl.BlockSpec((1, t, gw), lambda i, j: (i, j, COL_CX)),
                  pl.BlockSpec((1, t, gw), lambda i, j: (i, j, COL_CG)),
                  pl.BlockSpec((CONV_W, gw), lambda i, j: (0, 0)), row(),
                  pl.BlockSpec((gw, gw), lambda i, j: (0, 0)), row(),
                  pl.BlockSpec((gw, gw), lambda i, j: (0, 0)), row(), row()],
        out_specs=pl.BlockSpec((1, t, gw), lambda i, j: (i, j, 0)),
        out_shape=jax.ShapeDtypeStruct((b, s, gw), F32),
        scratch_shapes=[pltpu.VMEM((8, gw), F32), pltpu.VMEM((1, gw), F32)],
        compiler_params=_cparams("parallel", "arbitrary"),
        name="rglru",
    )(proj3, proj3, conv_w, conv_b.reshape(1, gw), wa_bd, ba.reshape(1, gw), wx_bd, bx.reshape(1, gw), lam.reshape(1, gw))


def _neumann_inverse(m):
    n = m.shape[0]
    eye = jnp.where(_iota((n, n), 0) == _iota((n, n), 1), 1.0, 0.0)
    x = eye - m
    y = m
    for _ in range(5):
        yb = y.astype(BF16)
        y = _dot(yb, yb)
        x = x + _dot(x.astype(BF16), y.astype(BF16))
    return x


def _gdn_kernel(q_ref, k_ref, v_ref, z_ref, sm_ref, gr_ref, cw_ref, o_ref, st_ref, pq_ref, pk_ref, pv_ref):
    T = q_ref.shape[1]
    C = GDN_CHUNK

    @pl.when(pl.program_id(1) == 0)
    def _():
        st_ref[...] = jnp.zeros_like(st_ref)
        pq_ref[...] = jnp.zeros_like(pq_ref)
        pk_ref[...] = jnp.zeros_like(pk_ref)
        pv_ref[...] = jnp.zeros_like(pv_ref)

    same_head = _head_ones()
    ones_bf = jnp.where(same_head, 1.0, 0.0).astype(BF16)
    cw = cw_ref[...]
    gw = GROUP_WIDTH
    xq, xk, xv = q_ref[0], k_ref[0], v_ref[0]
    q = _silu(_causal_conv4(xq, pq_ref[...], cw[:, 0:gw]))
    k = _silu(_causal_conv4(xk, pk_ref[...], cw[:, gw:2 * gw]))
    v = _silu(_causal_conv4(xv, pv_ref[...], cw[:, 2 * gw:3 * gw]))
    pq_ref[...] = xq[T - 8:T, :]
    pk_ref[...] = xk[T - 8:T, :]
    pv_ref[...] = xv[T - 8:T, :]
    q = q * lax.rsqrt(_dot_exact_rhs(q * q, ones_bf) + EPS) * (HEAD_DIM ** -0.5)
    k = k * lax.rsqrt(_dot_exact_rhs(k * k, ones_bf) + EPS)

    sm = sm_ref[0]
    er = _iota((SMALL_W, gw), 0)
    ec = _iota((SMALL_W, gw), 1) // HEAD_DIM
    exp_beta = jnp.where(er == LANE_BETA + ec, 1.0, 0.0).astype(BF16)
    exp_g = jnp.where(er == LANE_GDEC + ec, 1.0, 0.0).astype(BF16)
    betax = _dot_exact_rhs(sm, exp_beta)
    gx = _dot_exact_rhs(sm, exp_g)

    n = HEADS * C
    rr = _iota((n, n), 0)
    cc = _iota((n, n), 1)
    incl = same_head & (rr % C >= cc % C)
    strict = same_head & (rr % C > cc % C)

    outs = []
    for ci in range(T // C):
        sl = slice(ci * C, (ci + 1) * C)
        qc, kc, vc, bx, gc = q[sl], k[sl], v[sl], betax[sl], gx[sl]
        grow = gr_ref[0, ci, 0:1, :]
        eg = jnp.exp(gc)
        kb = kc * bx
        ks = _stack_heads(kc)
        gcol = _dot_exact_rhs(_stack_heads(gc), ones_bf) * (1.0 / HEAD_DIM)
        diff = gcol - grow
        gamma = jnp.where(incl, jnp.exp(jnp.where(incl, diff, 0.0)), 0.0)
        ks_bf = ks.astype(BF16)
        m = jnp.where(strict, _dot_nt(_stack_heads(kb).astype(BF16), ks_bf) * gamma, 0.0)
        a_qk = _dot_nt(_stack_heads(qc).astype(BF16), ks_bf) * gamma
        t_inv = _neumann_inverse(m).astype(BF16)
        u = _dot(t_inv, _stack_heads(vc * bx).astype(BF16))
        wk = _dot(t_inv, _stack_heads(kb * eg).astype(BF16))
        st = st_ref[...]
        st_bf = st.astype(BF16)
        v_new = u - _dot(wk.astype(BF16), st_bf)
        o = _dot(_stack_heads(qc * eg).astype(BF16), st_bf) + _dot(a_qk.astype(BF16), v_new.astype(BF16))
        outs.append(_unstack_heads(o, C))
        g_last = gc[C - 1:C, :]
        k_dec = _stack_heads(kc * jnp.exp(g_last - gc))
        st_ref[...] = st * jnp.exp(g_last) + _dot_tn(k_dec.astype(BF16), v_new.astype(BF16))
    o = jnp.concatenate(outs, axis=0)
    o_ref[0] = _head_rms(o, ones_bf) * _silu(z_ref[0])


def gdn(proj3, sm, g_row, conv_w):
    b, s, _ = proj3.shape
    t = SEQ_TILE
    gw = GROUP_WIDTH
    col = lambda cidx: pl.BlockSpec((1, t, gw), lambda i, j: (i, j, cidx))
    return pl.pallas_call(
        _gdn_kernel,
        grid=(b, s // t),
        in_specs=[col(COL_DQ), col(COL_DK), col(COL_DV), col(COL_DZ),
                  pl.BlockSpec((1, t, SMALL_W), lambda i, j: (i, j, 0)),
                  pl.BlockSpec((1, t // GDN_CHUNK, 8, gw), lambda i, j: (i, j, 0, 0)),
                  pl.BlockSpec((CONV_W, 3 * gw), lambda i, j: (0, 0))],
        out_specs=pl.BlockSpec((1, t, gw), lambda i, j: (i, j, 0)),
        out_shape=jax.ShapeDtypeStruct((b, s, gw), F32),
        scratch_shapes=[pltpu.VMEM((gw, gw), F32)] + [pltpu.VMEM((8, gw), F32)] * 3,
        compiler_params=_cparams("parallel", "arbitrary"),
        name="gdn",
    )(proj3, proj3, proj3, proj3, sm, g_row, conv_w)


def _mix_out_kernel(x_ref, ya_ref, yb_ref, yc_ref, yd_ref, gain_ref, w_ref, o_ref):
    gw = GROUP_WIDTH
    gain = gain_ref[...]
    acc = x_ref[0]
    for i, y_ref in enumerate((ya_ref, yb_ref, yc_ref, yd_ref)):
        yg = (y_ref[0] * gain[:, i * gw:(i + 1) * gw]).astype(BF16)
        acc = acc + _dot(yg, w_ref[i * gw:(i + 1) * gw, :])
    o_ref[0] = acc


def mix_out(x3, ya, yb, yc, yd, gain, w_out_bf16):
    b, s, d = x3.shape
    t = SEQ_TILE
    gw = GROUP_WIDTH
    grp = lambda: pl.BlockSpec((1, t, gw), lambda i, j: (i, j, 0))
    return pl.pallas_call(
        _mix_out_kernel,
        grid=(b, s // t),
        in_specs=[pl.BlockSpec((1, t, d), lambda i, j: (i, j, 0)),
                  grp(), grp(), grp(), grp(),
                  pl.BlockSpec((1, 4 * gw), lambda i, j: (0, 0)),
                  pl.BlockSpec((4 * gw, d), lambda i, j: (0, 0))],
        out_specs=pl.BlockSpec((1, t, d), lambda i, j: (i, j, 0)),
        out_shape=jax.ShapeDtypeStruct((b, s, d), F32),
        compiler_params=_cparams("parallel", "parallel"),
        name="mix_out",
    )(x3, ya, yb, yc, yd, gain.reshape(1, 4 * gw), w_out_bf16)


def _mem_attn_kernel(x_ref, g_ref, wq_ref, kv_ref, wo_ref, o_ref):
    x = x_ref[0]
    t = x.shape[0]
    gw = GROUP_WIDTH
    ms = jnp.mean(x * x, axis=-1, keepdims=True)
    h = ((x * lax.rsqrt(ms + EPS)) * g_ref[...]).astype(BF16)
    q = _dot(h, wq_ref[...])
    kv = kv_ref[0]
    k = kv[:, 0:gw].astype(BF16)
    v = kv[:, gw:2 * gw].astype(BF16)
    s = _dot_nt(_stack_heads(q).astype(BF16), k) * (HEAD_DIM ** -0.5)
    s = s - jnp.max(s, axis=-1, keepdims=True)
    p = jnp.exp(s)
    p = p / jnp.sum(p, axis=-1, keepdims=True)
    o = _dot(p.astype(BF16), v)
    lane_head = _iota((t, gw), 1) // HEAD_DIM
    oc = jnp.zeros((t, gw), F32)
    for hh in range(HEADS):
        oc = jnp.where(lane_head == hh, o[hh * t:(hh + 1) * t], oc)
    o_ref[0] = x + _dot(oc.astype(BF16), wo_ref[...])


def mem_attention(x3, g, wq_bf16, kv3, wo_bf16):
    b, s, d = x3.shape
    t = SEQ_TILE
    m = kv3.shape[1]
    return pl.pallas_call(
        _mem_attn_kernel,
        grid=(b, s // t),
        in_specs=[pl.BlockSpec((1, t, d), lambda i, j: (i, j, 0)),
                  pl.BlockSpec((1, d), lambda i, j: (0, 0)),
                  pl.BlockSpec((d, GROUP_WIDTH), lambda i, j: (0, 0)),
                  pl.BlockSpec((1, m, 2 * GROUP_WIDTH), lambda i, j: (i, 0, 0)),
                  pl.BlockSpec((GROUP_WIDTH, d), lambda i, j: (0, 0))],
        out_specs=pl.BlockSpec((1, t, d), lambda i, j: (i, j, 0)),
        out_shape=jax.ShapeDtypeStruct((b, s, d), F32),
        compiler_params=_cparams("parallel", "parallel"),
        name="mem_attention",
    )(x3, g.reshape(1, d), wq_bf16, kv3, wo_bf16)


def _router_kernel(x_ref, g_ref, w_ref, b_ref, h_ref, c_ref):
    x = x_ref[...]
    ms = jnp.mean(x * x, axis=-1, keepdims=True)
    h = (x * lax.rsqrt(ms + EPS)) * g_ref[...]
    h_ref[...] = h.astype(BF16)
    hh, hm, _ = _split3(h)
    w = w_ref[...]
    wh, wm, _ = _split3(w)
    logits = _dot(hh, wh) + _dot(hh, wm) + _dot(hm, wh) + b_ref[...]
    lane = _iota(logits.shape, 1)
    big = jnp.int32(1 << 30)
    neg = -jnp.inf
    is_grp = (lane >= N_EXPERTS) & (lane < N_EXPERTS + N_EXPERT_GROUPS)
    gl = jnp.where(is_grp, logits, neg)
    gmax = jnp.max(gl, axis=-1, keepdims=True)
    p_grp = 1.0 / jnp.sum(jnp.exp(gl - gmax), axis=-1, keepdims=True)
    g_sel = jnp.min(jnp.where(gl == gmax, lane, big), axis=-1, keepdims=True) - N_EXPERTS
    in_grp = (lane < N_EXPERTS) & (lane // EXPERTS_PER_GROUP == g_sel)
    el = jnp.where(in_grp, logits, neg)
    m1 = jnp.max(el, axis=-1, keepdims=True)
    i1 = jnp.min(jnp.where(el == m1, lane, big), axis=-1, keepdims=True)
    el2 = jnp.where(lane == i1, neg, el)
    m2 = jnp.max(el2, axis=-1, keepdims=True)
    i2 = jnp.min(jnp.where(el2 == m2, lane, big), axis=-1, keepdims=True)
    e21 = jnp.exp(m2 - m1)
    w1 = 1.0 / (1.0 + e21)
    w2 = e21 / (1.0 + e21)
    c_ref[...] = jnp.where(lane == i1, p_grp * w1, 0.0) + jnp.where(lane == i2, p_grp * w2, 0.0)


def router(x2, g, w_router, b_router, tm):
    m, d = x2.shape
    return pl.pallas_call(
        _router_kernel,
        grid=(m // tm,),
        in_specs=[pl.BlockSpec((tm, d), lambda i: (i, 0)),
                  pl.BlockSpec((1, d), lambda i: (0, 0)),
                  pl.BlockSpec((d, SMALL_W), lambda i: (0, 0)),
                  pl.BlockSpec((1, SMALL_W), lambda i: (0, 0))],
        out_specs=[pl.BlockSpec((tm, d), lambda i: (i, 0)),
                   pl.BlockSpec((tm, SMALL_W), lambda i: (i, 0))],
        out_shape=[jax.ShapeDtypeStruct((m, d), BF16), jax.ShapeDtypeStruct((m, SMALL_W), F32)],
        compiler_params=_cparams("parallel"),
        name="router",
    )(x2, g.reshape(1, d), w_router, b_router)


def _moe_dense_kernel(x_ref, h_ref, c_ref, wgu_ref, wdn_ref, o_ref):
    e = pl.program_id(1)

    @pl.when(e == 0)
    def _():
        o_ref[...] = x_ref[...]

    gu = _dot(h_ref[...], wgu_ref[0])
    c = c_ref[...]
    lane = _iota(c.shape, 1)
    ce = jnp.sum(jnp.where(lane == e, c, 0.0), axis=-1, keepdims=True)
    act = _silu(gu[:, 0:D_EXPERT]) * gu[:, D_EXPERT:2 * D_EXPERT] * ce
    o_ref[...] += _dot(act.astype(BF16), wdn_ref[0])


def moe_dense(x2, h_bf16, comb, wgu_bf16, wdn_bf16, tm):
    m, d = x2.shape
    ne = wgu_bf16.shape[0]
    return pl.pallas_call(
        _moe_dense_kernel,
        grid=(m // tm, ne),
        in_specs=[pl.BlockSpec((tm, d), lambda i, e: (i, 0)),
                  pl.BlockSpec((tm, d), lambda i, e: (i, 0)),
                  pl.BlockSpec((tm, SMALL_W), lambda i, e: (i, 0)),
                  pl.BlockSpec((1, d, 2 * D_EXPERT), lambda i, e: (e, 0, 0)),
                  pl.BlockSpec((1, D_EXPERT, d), lambda i, e: (e, 0, 0))],
        out_specs=pl.BlockSpec((tm, d), lambda i, e: (i, 0)),
        out_shape=jax.ShapeDtypeStruct((m, d), F32),
        compiler_params=_cparams("parallel", "arbitrary"),
        name="moe_dense",
    )(x2, h_bf16, comb, wgu_bf16, wdn_bf16)


def _rmsnorm_kernel(x_ref, g_ref, o_ref):
    x = x_ref[...]
    ms = jnp.mean(x * x, axis=-1, keepdims=True)
    o_ref[...] = (x * lax.rsqrt(ms + EPS)) * g_ref[...]


def rmsnorm(x2, g, tm):
    m, d = x2.shape
    return pl.pallas_call(
        _rmsnorm_kernel,
        grid=(m // tm,),
        in_specs=[pl.BlockSpec((tm, d), lambda i: (i, 0)), pl.BlockSpec((1, d), lambda i: (0, 0))],
        out_specs=pl.BlockSpec((tm, d), lambda i: (i, 0)),
        out_shape=jax.ShapeDtypeStruct((m, d), F32),
        compiler_params=_cparams("parallel"),
        name="rmsnorm",
    )(x2, g.reshape(1, d))


def _w_in_prep_kernel(w_ref, o_ref):
    gw = GROUP_WIDTH
    a_end = 7 * gw
    c_start = a_end + HEADS
    c_end = c_start + 5 * gw
    z_start = c_end + 2 * HEADS
    assert a_end % SMALL_W == LANE_FOX and c_end % SMALL_W == LANE_BETA and LANE_GDEC == LANE_BETA + HEADS
    w = w_ref[...]
    o_ref[:, 0:a_end] = w[:, 0:a_end].astype(BF16)
    o_ref[:, a_end:a_end + 5 * gw] = w[:, c_start:c_end].astype(BF16)
    o_ref[:, 12 * gw:13 * gw] = w[:, z_start:z_start + gw].astype(BF16)
    blk_fox = w[:, a_end:a_end + SMALL_W]
    blk_gdn = w[:, c_end - LANE_BETA:c_end - LANE_BETA + SMALL_W]
    lane = _iota(blk_fox.shape, 1)
    small = jnp.where(lane < LANE_BETA, blk_fox, jnp.where(lane < LANE_GDEC + HEADS, blk_gdn, 0.0))
    o_ref[:, 13 * gw:13 * gw + SMALL_W] = small.astype(BF16)
    o_ref[:, 13 * gw + SMALL_W:] = jnp.zeros((w.shape[0], SMALL_W), BF16)


def w_in_prep(w):
    d, n = w.shape
    tr = 128
    return pl.pallas_call(
        _w_in_prep_kernel,
        grid=(d // tr,),
        in_specs=[pl.BlockSpec((tr, n), lambda i: (i, 0))],
        out_specs=pl.BlockSpec((tr, N_PROJ), lambda i: (i, 0)),
        out_shape=jax.ShapeDtypeStruct((d, N_PROJ), BF16),
        compiler_params=_cparams("parallel"),
        name="w_in_prep",
    )(w)


def _block_diag(w):
    h = w.shape[0]
    eye = jnp.eye(h, dtype=w.dtype)
    return (eye[:, None, :, None] * w[:, :, None, :]).reshape(h * HEAD_DIM, h * HEAD_DIM)


def kernel(x, mem, norm_mix, w_in, hgrn_lb, fox_fb, lru_conv_w, lru_conv_b, lru_wa, lru_ba, lru_wx, lru_bx, lru_lam, gdn_conv_w, gdn_a_log, gdn_dt_bias, mix_gain, w_out, norm_mem, norm_memkv, w_mq, w_mkv, w_mo, norm_ffn, w_rg, b_rg, w_re, b_re, w_e_gu, w_e_dn, norm_final):
    b, s, d = x.shape
    depth = w_in.shape[0]
    t_tok = b * s
    mlen = mem.shape[1]
    gw = GROUP_WIDTH

    lb_all = jnp.cumsum(jax.nn.softmax(hgrn_lb.astype(F32), axis=0), axis=0)
    lb_all = lb_all - lb_all[0]

    x = x.astype(F32)
    for l in range(depth):
        proj = norm_matmul(x.reshape(t_tok, d), norm_mix[l], w_in_prep(w_in[l]), tm=256)
        proj3 = proj.reshape(b, s, N_PROJ)
        prm = jnp.zeros((8, SMALL_W), F32)
        prm = prm.at[0, LANE_FOX:LANE_FOX + HEADS].set(fox_fb[l].astype(F32))
        prm = prm.at[0, LANE_GDEC:LANE_GDEC + HEADS].set(gdn_dt_bias[l].astype(F32))
        prm = prm.at[1, LANE_GDEC:LANE_GDEC + HEADS].set(gdn_a_log[l].astype(F32))
        sm, c_exp = small_prep(proj3, prm)
        c_row = jnp.pad(jnp.swapaxes(sm[:, :, LANE_FOX:LANE_FOX + HEADS], 1, 2), ((0, 0), (0, 8 - HEADS), (0, 0)))
        g_row = sm[:, :, LANE_GDEC:LANE_GDEC + HEADS].reshape(b, s // GDN_CHUNK, GDN_CHUNK, HEADS)
        g_row = jnp.swapaxes(g_row, 2, 3).reshape(b, s // GDN_CHUNK, 1, gw)
        g_row = jnp.broadcast_to(g_row, (b, s // GDN_CHUNK, 8, gw))

        ya = hgrn2(proj3, lb_all[l])
        yb = fox_attention(proj3, c_row, c_exp)
        yc = rglru(proj3, lru_conv_w[l], lru_conv_b[l], _block_diag(lru_wa[l]).astype(BF16), lru_ba[l],
                   _block_diag(lru_wx[l]).astype(BF16), lru_bx[l], lru_lam[l])
        yd = gdn(proj3, sm, g_row, gdn_conv_w[l])
        x = mix_out(x, ya, yb, yc, yd, mix_gain[l], w_out[l].astype(BF16))

        kv = norm_matmul(mem.reshape(b * mlen, d), norm_memkv[l], w_mkv[l].astype(BF16), tm=256)
        x = mem_attention(x, norm_mem[l], w_mq[l].astype(BF16), kv.reshape(b, mlen, 2 * gw), w_mo[l].astype(BF16))

        w_router = jnp.concatenate([w_re[l], w_rg[l], jnp.zeros((d, SMALL_W - N_EXPERTS - N_EXPERT_GROUPS), F32)], axis=1)
        b_router = jnp.concatenate([b_re[l], b_rg[l], jnp.zeros((SMALL_W - N_EXPERTS - N_EXPERT_GROUPS,), F32)]).reshape(1, SMALL_W)
        x2 = x.reshape(t_tok, d)
        h_bf, comb = router(x2, norm_ffn[l], w_router, b_router, tm=512)
        x = moe_dense(x2, h_bf, comb, w_e_gu[l].astype(BF16), w_e_dn[l].astype(BF16), tm=1024).reshape(b, s, d)

    return rmsnorm(x.reshape(t_tok, d), norm_final, tm=512).reshape(b, s, d)
```

```python
import functools
import math

import jax
import jax.numpy as jnp
from jax import lax
from jax.experimental import pallas as pl
from jax.experimental.pallas import tpu as pltpu

F32 = jnp.float32
BF16 = jnp.bfloat16

HEAD_DIM = 64
GROUP_WIDTH = 256
HEADS = GROUP_WIDTH // HEAD_DIM
GDN_CHUNK = 64
HGRN_CHUNK = 16
CONV_W = 4
LRU_C = 8.0
EPS = 1e-6
N_EXPERT_GROUPS = 4
EXPERTS_PER_GROUP = 8
N_EXPERTS = N_EXPERT_GROUPS * EXPERTS_PER_GROUP
D_EXPERT = 256
EXPERT_TILE = 256
SMALL_W = 128
SEQ_TILE = 256
VMEM_LIMIT = 56 * 1024 * 1024

(COL_AQ, COL_AF, COL_AI, COL_AG, COL_BQ, COL_BK, COL_BV, COL_CX, COL_CG,
 COL_DQ, COL_DK, COL_DV, COL_DZ) = range(13)
N_PROJ = 13 * GROUP_WIDTH + 2 * SMALL_W
COL_SMALL = 13 * GROUP_WIDTH // SMALL_W
LANE_FOX = 0
LANE_BETA = 4
LANE_GDEC = 8


def _cparams(*sem):
    return pltpu.CompilerParams(dimension_semantics=sem, vmem_limit_bytes=VMEM_LIMIT)


def _dot(a, b):
    return jnp.dot(a, b, preferred_element_type=F32)


def _dot_nt(a, b):
    return lax.dot_general(a, b, (((1,), (1,)), ((), ())), preferred_element_type=F32)


def _dot_tn(a, b):
    return lax.dot_general(a, b, (((0,), (0,)), ((), ())), preferred_element_type=F32)


def _split3(x):
    h = x.astype(BF16)
    r = x - h.astype(F32)
    m = r.astype(BF16)
    l = (r - m.astype(F32)).astype(BF16)
    return h, m, l


def _dot_exact_rhs(x, w_bf16):
    h, m, l = _split3(x)
    return _dot(h, w_bf16) + _dot(m, w_bf16) + _dot(l, w_bf16)


def _dot_exact_lhs(w_bf16, x):
    h, m, l = _split3(x)
    return _dot(w_bf16, h) + _dot(w_bf16, m) + _dot(w_bf16, l)


def _iota(shape, dim):
    return lax.broadcasted_iota(jnp.int32, shape, dim)


def _head_ones(n=GROUP_WIDTH):
    r = _iota((n, n), 0) // HEAD_DIM
    c = _iota((n, n), 1) // HEAD_DIM
    return r == c


def _sigmoid(x):
    return 1.0 / (1.0 + jnp.exp(-x))


def _silu(x):
    return x * _sigmoid(x)


def _log_sigmoid(x):
    return jnp.minimum(x, 0.0) - jnp.log1p(jnp.exp(-jnp.abs(x)))


def _softplus(x):
    return jnp.maximum(x, 0.0) + jnp.log1p(jnp.exp(-jnp.abs(x)))


def _gelu_tanh(x):
    return 0.5 * x * (1.0 + jnp.tanh(math.sqrt(2.0 / math.pi) * (x + 0.044715 * (x * x * x))))


def _head_mean_sq(x, ones_bf16):
    return _dot_exact_rhs(x * x, ones_bf16) * (1.0 / HEAD_DIM)


def _head_rms(x, ones_bf16):
    return x * lax.rsqrt(_head_mean_sq(x, ones_bf16) + EPS)


def _stack_heads(x):
    lane_head = _iota(x.shape, 1) // HEAD_DIM
    parts = []
    for h in range(HEADS):
        parts.append(jnp.where(lane_head == h, x, 0.0))
    return jnp.concatenate(parts, axis=0)


def _unstack_heads(xs, rows):
    out = xs[0:rows]
    for h in range(1, HEADS):
        out = out + xs[h * rows:(h + 1) * rows]
    return out


def _causal_conv4(x, prev8, w):
    r = x.shape[0]
    row8 = _iota((8, x.shape[1]), 0)
    acc = x * w[CONV_W - 1:CONV_W, :]
    for k in range(1, CONV_W):
        xs = pltpu.roll(x, k, 0)
        ps = pltpu.roll(prev8, k, 0)
        top = jnp.where(row8 < k, ps, xs[0:8])
        xs = jnp.concatenate([top, xs[8:r]], axis=0)
        acc = acc + xs * w[CONV_W - 1 - k:CONV_W - k, :]
    return acc


def _norm_matmul_kernel(x_ref, g_ref, w_ref, o_ref):
    x = x_ref[...]
    ms = jnp.mean(x * x, axis=-1, keepdims=True)
    h = (x * lax.rsqrt(ms + EPS)) * g_ref[...]
    o_ref[...] = _dot(h.astype(BF16), w_ref[...]).astype(o_ref.dtype)


def norm_matmul(x, g, w_bf16, tm, out_dtype=F32):
    m, d = x.shape
    n = w_bf16.shape[1]
    return pl.pallas_call(
        _norm_matmul_kernel,
        grid=(m // tm,),
        in_specs=[pl.BlockSpec((tm, d), lambda i: (i, 0)),
                  pl.BlockSpec((1, d), lambda i: (0, 0)),
                  pl.BlockSpec((d, n), lambda i: (0, 0))],
        out_specs=pl.BlockSpec((tm, n), lambda i: (i, 0)),
        out_shape=jax.ShapeDtypeStruct((m, n), out_dtype),
        compiler_params=_cparams("parallel"),
        name="norm_matmul",
    )(x, g.reshape(1, d), w_bf16)


def _small_prep_kernel(s_ref, p_ref, o_ref, x_ref):
    S = s_ref.shape[1]
    W = s_ref.shape[2]
    blk = SEQ_TILE
    lane = _iota((blk, W), 1)
    er = _iota((W, HEADS * W), 0)
    ec = _iota((W, HEADS * W), 1) // W
    expand_fox = jnp.where(er == LANE_FOX + ec, 1.0, 0.0).astype(BF16)
    is_fox = (lane >= LANE_FOX) & (lane < LANE_FOX + HEADS)
    is_beta = (lane >= LANE_BETA) & (lane < LANE_BETA + HEADS)
    is_gdec = (lane >= LANE_GDEC) & (lane < LANE_GDEC + HEADS)
    r = _iota((blk, blk), 0)
    c = _iota((blk, blk), 1)
    tril_all = jnp.where(r >= c, 1.0, 0.0).astype(BF16)
    tril_chunk = jnp.where((r >= c) & (r // GDN_CHUNK == c // GDN_CHUNK), 1.0, 0.0).astype(BF16)
    neg_exp_a = -jnp.exp(p_ref[1:2, :])
    carry = jnp.zeros((1, W), F32)
    for i in range(S // blk):
        sl = pl.ds(i * blk, blk)
        z = s_ref[0, sl, :] + p_ref[0:1, :]
        fox = jnp.where(is_fox, _log_sigmoid(z), 0.0)
        beta = jnp.where(is_beta, _sigmoid(z), 0.0)
        gdec = jnp.where(is_gdec, neg_exp_a * _softplus(z), 0.0)
        cf = _dot_exact_lhs(tril_all, fox) + carry
        cg = _dot_exact_lhs(tril_chunk, gdec)
        carry = cf[blk - 1:blk, :]
        o_ref[0, sl, :] = cf + cg + beta
        x_ref[0, sl, :] = _dot_exact_rhs(cf, expand_fox)


def small_prep(proj3, params):
    b, s, _ = proj3.shape
    w = SMALL_W
    return pl.pallas_call(
        _small_prep_kernel,
        grid=(b,),
        in_specs=[pl.BlockSpec((1, s, w), lambda i: (i, 0, COL_SMALL)),
                  pl.BlockSpec((8, w), lambda i: (0, 0))],
        out_specs=[pl.BlockSpec((1, s, w), lambda i: (i, 0, 0)),
                   pl.BlockSpec((1, s, HEADS * w), lambda i: (i, 0, 0))],
        out_shape=[jax.ShapeDtypeStruct((b, s, w), F32), jax.ShapeDtypeStruct((b, s, HEADS * w), F32)],
        compiler_params=_cparams("parallel"),
        name="small_prep",
    )(proj3, params)


def _hgrn2_kernel(q_ref, f_ref, i_ref, g_ref, lb_ref, o_ref, st_ref, qs, ks, vs, bs, os_):
    T = q_ref.shape[1]
    C = HGRN_CHUNK

    @pl.when(pl.program_id(1) == 0)
    def _():
        st_ref[...] = jnp.zeros_like(st_ref)

    same_head = _head_ones()
    ones_bf = jnp.where(same_head, 1.0, 0.0).astype(BF16)
    lb = lb_ref[...]
    fl = f_ref[0]
    log_lb = jnp.log(lb)
    c2 = jnp.log1p(-lb) + _log_sigmoid(fl)
    mx = jnp.maximum(log_lb, c2)
    log_f = mx + jnp.log1p(jnp.exp(-jnp.abs(log_lb - c2)))
    r = _iota((T, T), 0)
    c = _iota((T, T), 1)
    tril_chunk = jnp.where((r >= c) & (r // C == c // C), 1.0, 0.0).astype(BF16)
    bs[...] = _dot_exact_lhs(tril_chunk, log_f)
    qs[...] = _silu(q_ref[0])
    ks[...] = (1.0 - lb) * _sigmoid(-fl)
    vs[...] = i_ref[0]

    trow = _iota((C, GROUP_WIDTH), 0)

    def chunk(ci, carry):
        r0 = pl.multiple_of(ci * C, C)
        qc = qs[pl.ds(r0, C), :]
        kc = ks[pl.ds(r0, C), :]
        vc = vs[pl.ds(r0, C), :]
        bc = bs[pl.ds(r0, C), :]
        st = st_ref[...]
        o = _dot_nt((qc * jnp.exp(bc)).astype(BF16), st.astype(BF16))
        parts = []
        for s in range(C):
            m = trow >= s
            d = jnp.where(m, bc - bc[s:s + 1, :], 0.0)
            parts.append(jnp.where(m, jnp.exp(d) * (qc * kc[s:s + 1, :]), 0.0))
        a = jnp.concatenate(parts, axis=0).astype(BF16)
        sc = _dot(a, ones_bf)
        for s in range(C):
            o = o + sc[s * C:(s + 1) * C, :] * vc[s:s + 1, :]
        os_[pl.ds(r0, C), :] = o
        b_last = bc[C - 1:C, :]
        kd = kc * jnp.exp(b_last - bc)
        upd = _dot_tn(vc.astype(BF16), kd.astype(BF16))
        st_ref[...] = st * jnp.exp(b_last) + jnp.where(same_head, upd, 0.0)
        return carry

    lax.fori_loop(0, T // C, chunk, 0)
    o_ref[0] = _head_rms(os_[...], ones_bf) * _silu(g_ref[0])


def hgrn2(proj3, lb):
    b, s, _ = proj3.shape
    t = SEQ_TILE
    col = lambda cidx: pl.BlockSpec((1, t, GROUP_WIDTH), lambda i, j: (i, j, cidx))
    return pl.pallas_call(
        _hgrn2_kernel,
        grid=(b, s // t),
        in_specs=[col(COL_AQ), col(COL_AF), col(COL_AI), col(COL_AG),
                  pl.BlockSpec((1, GROUP_WIDTH), lambda i, j: (0, 0))],
        out_specs=pl.BlockSpec((1, t, GROUP_WIDTH), lambda i, j: (i, j, 0)),
        out_shape=jax.ShapeDtypeStruct((b, s, GROUP_WIDTH), F32),
        scratch_shapes=[pltpu.VMEM((GROUP_WIDTH, GROUP_WIDTH), F32)] + [pltpu.VMEM((t, GROUP_WIDTH), F32)] * 5,
        compiler_params=_cparams("parallel", "arbitrary"),
        name="hgrn2",
    )(proj3, proj3, proj3, proj3, lb.reshape(1, GROUP_WIDTH))


def _fox_kernel(q_ref, k_ref, v_ref, cr_ref, cx_ref, o_ref, kb_ref, vt_ref, acc_ref):
    tq = q_ref.shape[1]
    tk = tq
    S = k_ref.shape[1]
    n = HEADS * tq
    qi = pl.program_id(1)

    @pl.when(qi == 0)
    def _():
        for i in range(S // tk):
            sl = pl.ds(i * tk, tk)
            kb_ref[sl, :] = k_ref[0, sl, :].astype(BF16)
            vt_ref[:, sl] = v_ref[0, sl, :].T.astype(BF16)

    qs = _stack_heads(q_ref[0] * (HEAD_DIM ** -0.5)).astype(BF16)
    q0 = pl.multiple_of(qi * tq, tq)
    cr = cr_ref[0, :, pl.ds(q0, tq)]
    cq_row = jnp.concatenate([cr[h:h + 1, :] for h in range(HEADS)], axis=1)
    acc_ref[...] = jnp.zeros_like(acc_ref)
    reps = tq // SMALL_W

    def step(kb, m_old, l_old, masked):
        k0 = pl.multiple_of(kb * tk, tk)
        st = _dot_nt(kb_ref[pl.ds(k0, tk), :], qs)
        cx = cx_ref[0, pl.ds(k0, tk), :]
        ck = jnp.concatenate([cx[:, h * SMALL_W:(h + 1) * SMALL_W] for h in range(HEADS) for _ in range(reps)], axis=1)
        st = st + (cq_row - ck)
        if masked:
            st = jnp.where(_iota((tk, n), 0) <= _iota((tk, n), 1) % tq, st, -jnp.inf)
        m_new = jnp.maximum(m_old, jnp.max(st, axis=0, keepdims=True))
        alpha = jnp.exp(m_old - m_new)
        p = jnp.exp(st - m_new)
        l_new = alpha * l_old + jnp.sum(p, axis=0, keepdims=True)
        acc_ref[...] = alpha * acc_ref[...] + _dot(vt_ref[:, pl.ds(k0, tk)], p.astype(BF16))
        return m_new, l_new

    m0 = jnp.full((1, n), -jnp.inf, F32)
    l0 = jnp.zeros((1, n), F32)
    m1, l1 = lax.fori_loop(0, qi, lambda kb, c: step(kb, c[0], c[1], False), (m0, l0))
    _, l2 = step(qi, m1, l1, True)

    acc = acc_ref[...] * (1.0 / l2)
    ot = jnp.concatenate([acc[h * HEAD_DIM:(h + 1) * HEAD_DIM, h * tq:(h + 1) * tq] for h in range(HEADS)], axis=0)
    ones_bf = jnp.where(_head_ones(), 1.0, 0.0).astype(BF16)
    o_ref[0] = _head_rms(ot.T, ones_bf)


def fox_attention(proj3, c_row, c_exp):
    b, s, _ = proj3.shape
    tq = SEQ_TILE
    gw = GROUP_WIDTH
    return pl.pallas_call(
        _fox_kernel,
        grid=(b, s // tq),
        in_specs=[pl.BlockSpec((1, tq, gw), lambda i, j: (i, j, COL_BQ)),
                  pl.BlockSpec((1, s, gw), lambda i, j: (i, 0, COL_BK)),
                  pl.BlockSpec((1, s, gw), lambda i, j: (i, 0, COL_BV)),
                  pl.BlockSpec((1, 8, s), lambda i, j: (i, 0, 0)),
                  pl.BlockSpec((1, s, HEADS * SMALL_W), lambda i, j: (i, 0, 0))],
        out_specs=pl.BlockSpec((1, tq, gw), lambda i, j: (i, j, 0)),
        out_shape=jax.ShapeDtypeStruct((b, s, gw), F32),
        scratch_shapes=[pltpu.VMEM((s, gw), BF16),
                        pltpu.VMEM((gw, s), BF16),
                        pltpu.VMEM((gw, HEADS * tq), F32)],
        compiler_params=_cparams("parallel", "arbitrary"),
        name="fox_attention",
    )(proj3, proj3, proj3, c_row, c_exp)


def _rglru_kernel(x_ref, g_ref, cw_ref, cb_ref, wa_ref, ba_ref, wx_ref, bx_ref, lam_ref, o_ref, prev_ref, h_ref):
    @pl.when(pl.program_id(1) == 0)
    def _():
        prev_ref[...] = jnp.zeros_like(prev_ref)
        h_ref[...] = jnp.zeros_like(h_ref)

    x = x_ref[0]
    t = x.shape[0]
    xc = _causal_conv4(x, prev_ref[...], cw_ref[...]) + cb_ref[...]
    prev_ref[...] = x[t - 8:t, :]
    xb = xc.astype(BF16)
    r = _sigmoid(_dot(xb, wa_ref[...]) + ba_ref[...])
    ig = _sigmoid(_dot(xb, wx_ref[...]) + bx_ref[...])
    log_a = (-LRU_C * r) * _softplus(-lam_ref[...])
    a = jnp.exp(log_a)
    u = jnp.sqrt(1.0 - jnp.exp(2.0 * log_a)) * (ig * xc)
    row = _iota(a.shape, 0)
    d = 1
    while d < t:
        valid = row >= d
        u = jnp.where(valid, a * pltpu.roll(u, d, 0) + u, u)
        a = jnp.where(valid, a * pltpu.roll(a, d, 0), a)
        d *= 2
    h = a * h_ref[...] + u
    h_ref[...] = h[t - 1:t, :]
    ones_bf = jnp.where(_head_ones(), 1.0, 0.0).astype(BF16)
    o_ref[0] = _head_rms(h * _gelu_tanh(g_ref[0]), ones_bf)


def rglru(proj3, conv_w, conv_b, wa_bd, ba, wx_bd, bx, lam):
    b, s, _ = proj3.shape
    t = SEQ_TILE
    gw = GROUP_WIDTH
    row = lambda: pl.BlockSpec((1, gw), lambda i, j: (0, 0))
    return pl.pallas_call(
        _rglru_kernel,
        grid=(b, s // t),
        in_specs=[pl.BlockSpec((1, t, gw), lambda i, j: (i, j, COL_CX)),
                  pl.BlockSpec((1, t, gw), lambda i, j: (i, j, COL_CG)),
                  pl.BlockSpec((CONV_W, gw), lambda i, j: (0, 0)), row(),
                  pl.BlockSpec((gw, gw), lambda i, j: (0, 0)), row(),
                  pl.BlockSpec((gw, gw), lambda i, j: (0, 0)), row(), row()],
        out_specs=pl.BlockSpec((1, t, gw), lambda i, j: (i, j, 0)),
        out_shape=jax.ShapeDtypeStruct((b, s, gw), F32),
        scratch_shapes=[pltpu.VMEM((8, gw), F32), pltpu.VMEM((1, gw), F32)],
        compiler_params=_cparams("parallel", "arbitrary"),
        name="rglru",
    )(proj3, proj3, conv_w, conv_b.reshape(1, gw), wa_bd, ba.reshape(1, gw), wx_bd, bx.reshape(1, gw), lam.reshape(1, gw))


def _neumann_inverse(m):
    n = m.shape[0]
    eye = jnp.where(_iota((n, n), 0) == _iota((n, n), 1), 1.0, 0.0)
    x = eye - m
    y = m
    for _ in range(5):
        yb = y.astype(BF16)
        y = _dot(yb, yb)
        x = x + _dot(x.astype(BF16), y.astype(BF16))
    return x


def _gdn_kernel(q_ref, k_ref, v_ref, z_ref, sm_ref, gr_ref, cw_ref, o_ref, st_ref, pq_ref, pk_ref, pv_ref):
    T = q_ref.shape[1]
    C = GDN_CHUNK

    @pl.when(pl.program_id(1) == 0)
    def _():
        st_ref[...] = jnp.zeros_like(st_ref)
        pq_ref[...] = jnp.zeros_like(pq_ref)
        pk_ref[...] = jnp.zeros_like(pk_ref)
        pv_ref[...] = jnp.zeros_like(pv_ref)

    same_head = _head_ones()
    ones_bf = jnp.where(same_head, 1.0, 0.0).astype(BF16)
    cw = cw_ref[...]
    gw = GROUP_WIDTH
    xq, xk, xv = q_ref[0], k_ref[0], v_ref[0]
    q = _silu(_causal_conv4(xq, pq_ref[...], cw[:, 0:gw]))
    k = _silu(_causal_conv4(xk, pk_ref[...], cw[:, gw:2 * gw]))
    v = _silu(_causal_conv4(xv, pv_ref[...], cw[:, 2 * gw:3 * gw]))
    pq_ref[...] = xq[T - 8:T, :]
    pk_ref[...] = xk[T - 8:T, :]
    pv_ref[...] = xv[T - 8:T, :]
    q = q * lax.rsqrt(_dot_exact_rhs(q * q, ones_bf) + EPS) * (HEAD_DIM ** -0.5)
    k = k * lax.rsqrt(_dot_exact_rhs(k * k, ones_bf) + EPS)

    sm = sm_ref[0]
    er = _iota((SMALL_W, gw), 0)
    ec = _iota((SMALL_W, gw), 1) // HEAD_DIM
    exp_beta = jnp.where(er == LANE_BETA + ec, 1.0, 0.0).astype(BF16)
    exp_g = jnp.where(er == LANE_GDEC + ec, 1.0, 0.0).astype(BF16)
    betax = _dot_exact_rhs(sm, exp_beta)
    gx = _dot_exact_rhs(sm, exp_g)

    n = HEADS * C
    rr = _iota((n, n), 0)
    cc = _iota((n, n), 1)
    incl = same_head & (rr % C >= cc % C)
    strict = same_head & (rr % C > cc % C)

    outs = []
    for ci in range(T // C):
        sl = slice(ci * C, (ci + 1) * C)
        qc, kc, vc, bx, gc = q[sl], k[sl], v[sl], betax[sl], gx[sl]
        grow = gr_ref[0, ci, 0:1, :]
        eg = jnp.exp(gc)
        kb = kc * bx
        ks = _stack_heads(kc)
        gcol = _dot_exact_rhs(_stack_heads(gc), ones_bf) * (1.0 / HEAD_DIM)
        diff = gcol - grow
        gamma = jnp.where(incl, jnp.exp(jnp.where(incl, diff, 0.0)), 0.0)
        ks_bf = ks.astype(BF16)
        m = jnp.where(strict, _dot_nt(_stack_heads(kb).astype(BF16), ks_bf) * gamma, 0.0)
        a_qk = _dot_nt(_stack_heads(qc).astype(BF16), ks_bf) * gamma
        t_inv = _neumann_inverse(m).astype(BF16)
        u = _dot(t_inv, _stack_heads(vc * bx).astype(BF16))
        wk = _dot(t_inv, _stack_heads(kb * eg).astype(BF16))
        st = st_ref[...]
        st_bf = st.astype(BF16)
        v_new = u - _dot(wk.astype(BF16), st_bf)
        o = _dot(_stack_heads(qc * eg).astype(BF16), st_bf) + _dot(a_qk.astype(BF16), v_new.astype(BF16))
        outs.append(_unstack_heads(o, C))
        g_last = gc[C - 1:C, :]
        k_dec = _stack_heads(kc * jnp.exp(g_last - gc))
        st_ref[...] = st * jnp.exp(g_last) + _dot_tn(k_dec.astype(BF16), v_new.astype(BF16))
    o = jnp.concatenate(outs, axis=0)
    o_ref[0] = _head_rms(o, ones_bf) * _silu(z_ref[0])


def gdn(proj3, sm, g_row, conv_w):
    b, s, _ = proj3.shape
    t = SEQ_TILE
    gw = GROUP_WIDTH
    col = lambda cidx: pl.BlockSpec((1, t, gw), lambda i, j: (i, j, cidx))
    return pl.pallas_call(
        _gdn_kernel,
        grid=(b, s // t),
        in_specs=[col(COL_DQ), col(COL_DK), col(COL_DV), col(COL_DZ),
                  pl.BlockSpec((1, t, SMALL_W), lambda i, j: (i, j, 0)),
                  pl.BlockSpec((1, t // GDN_CHUNK, 8, gw), lambda i, j: (i, j, 0, 0)),
                  pl.BlockSpec((CONV_W, 3 * gw), lambda i, j: (0, 0))],
        out_specs=pl.BlockSpec((1, t, gw), lambda i, j: (i, j, 0)),
        out_shape=jax.ShapeDtypeStruct((b, s, gw), F32),
        scratch_shapes=[pltpu.VMEM((gw, gw), F32)] + [pltpu.VMEM((8, gw), F32)] * 3,
        compiler_params=_cparams("parallel", "arbitrary"),
        name="gdn",
    )(proj3, proj3, proj3, proj3, sm, g_row, conv_w)


def _mix_out_kernel(x_ref, ya_ref, yb_ref, yc_ref, yd_ref, gain_ref, w_ref, o_ref):
    gw = GROUP_WIDTH
    gain = gain_ref[...]
    acc = x_ref[0]
    for i, y_ref in enumerate((ya_ref, yb_ref, yc_ref, yd_ref)):
        yg = (y_ref[0] * gain[:, i * gw:(i + 1) * gw]).astype(BF16)
        acc = acc + _dot(yg, w_ref[i * gw:(i + 1) * gw, :])
    o_ref[0] = acc


def mix_out(x3, ya, yb, yc, yd, gain, w_out_bf16):
    b, s, d = x3.shape
    t = SEQ_TILE
    gw = GROUP_WIDTH
    grp = lambda: pl.BlockSpec((1, t, gw), lambda i, j: (i, j, 0))
    return pl.pallas_call(
        _mix_out_kernel,
        grid=(b, s // t),
        in_specs=[pl.BlockSpec((1, t, d), lambda i, j: (i, j, 0)),
                  grp(), grp(), grp(), grp(),
                  pl.BlockSpec((1, 4 * gw), lambda i, j: (0, 0)),
                  pl.BlockSpec((4 * gw, d), lambda i, j: (0, 0))],
        out_specs=pl.BlockSpec((1, t, d), lambda i, j: (i, j, 0)),
        out_shape=jax.ShapeDtypeStruct((b, s, d), F32),
        compiler_params=_cparams("parallel", "parallel"),
        name="mix_out",
    )(x3, ya, yb, yc, yd, gain.reshape(1, 4 * gw), w_out_bf16)


def _mem_attn_kernel(x_ref, g_ref, wq_ref, kv_ref, wo_ref, o_ref):
    x = x_ref[0]
    t = x.shape[0]
    gw = GROUP_WIDTH
    ms = jnp.mean(x * x, axis=-1, keepdims=True)
    h = ((x * lax.rsqrt(ms + EPS)) * g_ref[...]).astype(BF16)
    q = _dot(h, wq_ref[...])
    kv = kv_ref[0]
    k = kv[:, 0:gw].astype(BF16)
    v = kv[:, gw:2 * gw].astype(BF16)
    s = _dot_nt(_stack_heads(q).astype(BF16), k) * (HEAD_DIM ** -0.5)
    s = s - jnp.max(s, axis=-1, keepdims=True)
    p = jnp.exp(s)
    p = p / jnp.sum(p, axis=-1, keepdims=True)
    o = _dot(p.astype(BF16), v)
    lane_head = _iota((t, gw), 1) // HEAD_DIM
    oc = jnp.zeros((t, gw), F32)
    for hh in range(HEADS):
        oc = jnp.where(lane_head == hh, o[hh * t:(hh + 1) * t], oc)
    o_ref[0] = x + _dot(oc.astype(BF16), wo_ref[...])


def mem_attention(x3, g, wq_bf16, kv3, wo_bf16):
    b, s, d = x3.shape
    t = SEQ_TILE
    m = kv3.shape[1]
    return pl.pallas_call(
        _mem_attn_kernel,
        grid=(b, s // t),
        in_specs=[pl.BlockSpec((1, t, d), lambda i, j: (i, j, 0)),
                  pl.BlockSpec((1, d), lambda i, j: (0, 0)),
                  pl.BlockSpec((d, GROUP_WIDTH), lambda i, j: (0, 0)),
                  pl.BlockSpec((1, m, 2 * GROUP_WIDTH), lambda i, j: (i, 0, 0)),
                  pl.BlockSpec((GROUP_WIDTH, d), lambda i, j: (0, 0))],
        out_specs=pl.BlockSpec((1, t, d), lambda i, j: (i, j, 0)),
        out_shape=jax.ShapeDtypeStruct((b, s, d), F32),
        compiler_params=_cparams("parallel", "parallel"),
        name="mem_attention",
    )(x3, g.reshape(1, d), wq_bf16, kv3, wo_bf16)


def _router_kernel(x_ref, g_ref, w_ref, b_ref, h_ref, r_ref, n_ref, cnt_ref):
    @pl.when(pl.program_id(0) == 0)
    def _():
        cnt_ref[...] = jnp.zeros_like(cnt_ref)

    x = x_ref[...]
    tm = x.shape[0]
    ms = jnp.mean(x * x, axis=-1, keepdims=True)
    h = (x * lax.rsqrt(ms + EPS)) * g_ref[...]
    h_ref[...] = h
    hh, hm, _ = _split3(h)
    w = w_ref[...]
    wh, wm, _ = _split3(w)
    logits = _dot(hh, wh) + _dot(hh, wm) + _dot(hm, wh) + b_ref[...]
    lane = _iota(logits.shape, 1)
    big = jnp.int32(1 << 30)
    neg = -jnp.inf
    is_grp = (lane >= N_EXPERTS) & (lane < N_EXPERTS + N_EXPERT_GROUPS)
    gl = jnp.where(is_grp, logits, neg)
    gmax = jnp.max(gl, axis=-1, keepdims=True)
    p_grp = 1.0 / jnp.sum(jnp.exp(gl - gmax), axis=-1, keepdims=True)
    g_sel = jnp.min(jnp.where(gl == gmax, lane, big), axis=-1, keepdims=True) - N_EXPERTS
    in_grp = (lane < N_EXPERTS) & (lane // EXPERTS_PER_GROUP == g_sel)
    el = jnp.where(in_grp, logits, neg)
    m1 = jnp.max(el, axis=-1, keepdims=True)
    i1 = jnp.min(jnp.where(el == m1, lane, big), axis=-1, keepdims=True)
    el2 = jnp.where(lane == i1, neg, el)
    m2 = jnp.max(el2, axis=-1, keepdims=True)
    i2 = jnp.min(jnp.where(el2 == m2, lane, big), axis=-1, keepdims=True)
    e21 = jnp.exp(m2 - m1)
    w1 = 1.0 / (1.0 + e21)
    w2 = e21 / (1.0 + e21)
    oh1 = lane == i1
    oh2 = lane == i2
    onehot = jnp.where(oh1 | oh2, 1.0, 0.0)
    strict = jnp.where(_iota((tm, tm), 0) > _iota((tm, tm), 1), 1.0, 0.0).astype(BF16)
    before = _dot(strict, onehot.astype(BF16)) + cnt_ref[...]
    rank1 = jnp.sum(jnp.where(oh1, before, 0.0), axis=-1, keepdims=True)
    rank2 = jnp.sum(jnp.where(oh2, before, 0.0), axis=-1, keepdims=True)
    cnt = before[tm - 1:tm, :] + onehot[tm - 1:tm, :]
    cnt_ref[...] = cnt
    n_ref[...] = jnp.broadcast_to(cnt, n_ref.shape)
    cols = (i1.astype(F32), i2.astype(F32), p_grp * w1, p_grp * w2, rank1, rank2)
    route = jnp.zeros(logits.shape, F32)
    for j, col in enumerate(cols):
        route = jnp.where(lane == j, col, route)
    r_ref[...] = route


def router(x2, g, w_router, b_router, tm):
    m, d = x2.shape
    return pl.pallas_call(
        _router_kernel,
        grid=(m // tm,),
        in_specs=[pl.BlockSpec((tm, d), lambda i: (i, 0)),
                  pl.BlockSpec((1, d), lambda i: (0, 0)),
                  pl.BlockSpec((d, SMALL_W), lambda i: (0, 0)),
                  pl.BlockSpec((1, SMALL_W), lambda i: (0, 0))],
        out_specs=[pl.BlockSpec((tm, d), lambda i: (i, 0)),
                   pl.BlockSpec((tm, SMALL_W), lambda i: (i, 0)),
                   pl.BlockSpec((8, SMALL_W), lambda i: (0, 0))],
        out_shape=[jax.ShapeDtypeStruct((m, d), F32), jax.ShapeDtypeStruct((m, SMALL_W), F32),
                   jax.ShapeDtypeStruct((8, SMALL_W), F32)],
        scratch_shapes=[pltpu.VMEM((1, SMALL_W), F32)],
        compiler_params=_cparams("arbitrary"),
        name="router",
    )(x2, g.reshape(1, d), w_router, b_router)


def _row_copy(src_ref, src_row, dst_ref, dst_row, sem):
    return pltpu.make_async_copy(src_ref.at[pl.ds(src_row, 1)], dst_ref.at[pl.ds(dst_row, 1)], sem)


def _dispatch_kernel(dest_ref, h_ref, xs_in_ref, xs_ref, sem):
    del xs_in_ref
    tm = h_ref.shape[0]
    base = pl.program_id(0) * tm

    def start(t, c):
        for slot in range(2):
            _row_copy(h_ref, t, xs_ref, dest_ref[2 * (base + t) + slot], sem).start()
        return c

    def wait(t, c):
        for slot in range(2):
            _row_copy(h_ref, t, xs_ref, dest_ref[2 * (base + t) + slot], sem).wait()
        return c

    lax.fori_loop(0, tm, start, 0, unroll=8)
    lax.fori_loop(0, tm, wait, 0, unroll=8)


def moe_dispatch(h, dest, n_rows, tm):
    m, d = h.shape
    return pl.pallas_call(
        _dispatch_kernel,
        grid_spec=pltpu.PrefetchScalarGridSpec(
            num_scalar_prefetch=1,
            grid=(m // tm,),
            in_specs=[pl.BlockSpec((tm, d), lambda i, dest: (i, 0)),
                      pl.BlockSpec(memory_space=pl.ANY)],
            out_specs=pl.BlockSpec(memory_space=pl.ANY),
            scratch_shapes=[pltpu.SemaphoreType.DMA]),
        out_shape=jax.ShapeDtypeStruct((n_rows, d), F32),
        input_output_aliases={2: 0},
        compiler_params=_cparams("arbitrary"),
        name="moe_dispatch",
    )(dest, h, jnp.zeros((n_rows, d), F32))


def _experts_kernel(te_ref, xs_ref, wgu_ref, wdn_ref, y_ref, wgu_bf, wdn_bf):
    i = pl.program_id(0)
    prev = te_ref[jnp.maximum(i - 1, 0)]

    @pl.when((i == 0) | (te_ref[i] != prev))
    def _():
        wgu_bf[...] = wgu_ref[0].astype(BF16)
        wdn_bf[...] = wdn_ref[0].astype(BF16)

    gu = _dot(xs_ref[...].astype(BF16), wgu_bf[...])
    act = _silu(gu[:, 0:D_EXPERT]) * gu[:, D_EXPERT:2 * D_EXPERT]
    y_ref[...] = _dot(act.astype(BF16), wdn_bf[...])


def moe_experts(xs, tile_expert, w_gu, w_dn, tm):
    n_rows, d = xs.shape
    return pl.pallas_call(
        _experts_kernel,
        grid_spec=pltpu.PrefetchScalarGridSpec(
            num_scalar_prefetch=1,
            grid=(n_rows // tm,),
            in_specs=[pl.BlockSpec((tm, d), lambda i, te: (i, 0)),
                      pl.BlockSpec((1, d, 2 * D_EXPERT), lambda i, te: (te[i], 0, 0)),
                      pl.BlockSpec((1, D_EXPERT, d), lambda i, te: (te[i], 0, 0))],
            out_specs=pl.BlockSpec((tm, d), lambda i, te: (i, 0)),
            scratch_shapes=[pltpu.VMEM((d, 2 * D_EXPERT), BF16), pltpu.VMEM((D_EXPERT, d), BF16)]),
        out_shape=jax.ShapeDtypeStruct((n_rows, d), F32),
        compiler_params=_cparams("arbitrary"),
        name="moe_experts",
    )(tile_expert, xs, w_gu, w_dn)


def _combine_kernel(dest_ref, x_ref, r_ref, g_ref, ys_ref, o_ref, ybuf, sem, *, final_norm):
    tm = x_ref.shape[0]
    base = pl.program_id(0) * tm

    def start(t, c):
        for slot in range(2):
            _row_copy(ys_ref, dest_ref[2 * (base + t) + slot], ybuf.at[slot], t, sem).start()
        return c

    def wait(t, c):
        for slot in range(2):
            _row_copy(ys_ref, dest_ref[2 * (base + t) + slot], ybuf.at[slot], t, sem).wait()
        return c

    lax.fori_loop(0, tm, start, 0, unroll=8)
    r = r_ref[...]
    g1 = r[:, 2:3]
    g2 = r[:, 3:4]
    lax.fori_loop(0, tm, wait, 0, unroll=8)
    out = x_ref[...] + g1 * ybuf[0] + g2 * ybuf[1]
    if final_norm:
        ms = jnp.mean(out * out, axis=-1, keepdims=True)
        out = (out * lax.rsqrt(ms + EPS)) * g_ref[...]
    o_ref[...] = out


def moe_combine(x2, route, dest, ys, g_final, tm, final_norm):
    m, d = x2.shape
    return pl.pallas_call(
        functools.partial(_combine_kernel, final_norm=final_norm),
        grid_spec=pltpu.PrefetchScalarGridSpec(
            num_scalar_prefetch=1,
            grid=(m // tm,),
            in_specs=[pl.BlockSpec((tm, d), lambda i, dest: (i, 0)),
                      pl.BlockSpec((tm, SMALL_W), lambda i, dest: (i, 0)),
                      pl.BlockSpec((1, d), lambda i, dest: (0, 0)),
                      pl.BlockSpec(memory_space=pl.ANY)],
            out_specs=pl.BlockSpec((tm, d), lambda i, dest: (i, 0)),
            scratch_shapes=[pltpu.VMEM((2, tm, d), F32), pltpu.SemaphoreType.DMA]),
        out_shape=jax.ShapeDtypeStruct((m, d), F32),
        compiler_params=_cparams("arbitrary"),
        name="moe_combine",
    )(dest, x2, route, g_final.reshape(1, d), ys)


def hier_moe_routed(x2, g_ffn, w_router, b_router, w_gu, w_dn, g_final, final_norm):
    m, d = x2.shape
    te_rows = EXPERT_TILE
    n_rows = 2 * m + N_EXPERTS * te_rows
    h, route, counts = router(x2, g_ffn, w_router, b_router, tm=512)
    cnt = counts[0, 0:N_EXPERTS].astype(jnp.int32)
    padded = (cnt + te_rows - 1) // te_rows * te_rows
    seg_end = jnp.cumsum(padded)
    seg_start = seg_end - padded
    idx = route[:, 0:2].astype(jnp.int32)
    dest = (jnp.take(seg_start, idx) + route[:, 4:6].astype(jnp.int32)).reshape(2 * m)
    tile_start = jnp.arange(n_rows // te_rows, dtype=jnp.int32) * te_rows
    tile_expert = jnp.minimum(jnp.searchsorted(seg_end, tile_start, side="right"), N_EXPERTS - 1).astype(jnp.int32)
    xs = moe_dispatch(h, dest, n_rows, tm=512)
    ys = moe_experts(xs, tile_expert, w_gu, w_dn, tm=te_rows)
    return moe_combine(x2, route, dest, ys, g_final, tm=256, final_norm=final_norm)


def _w_in_prep_kernel(w_ref, o_ref):
    gw = GROUP_WIDTH
    a_end = 7 * gw
    c_start = a_end + HEADS
    c_end = c_start + 5 * gw
    z_start = c_end + 2 * HEADS
    assert a_end % SMALL_W == LANE_FOX and c_end % SMALL_W == LANE_BETA and LANE_GDEC == LANE_BETA + HEADS
    w = w_ref[...]
    o_ref[:, 0:a_end] = w[:, 0:a_end].astype(BF16)
    o_ref[:, a_end:a_end + 5 * gw] = w[:, c_start:c_end].astype(BF16)
    o_ref[:, 12 * gw:13 * gw] = w[:, z_start:z_start + gw].astype(BF16)
    blk_fox = w[:, a_end:a_end + SMALL_W]
    blk_gdn = w[:, c_end - LANE_BETA:c_end - LANE_BETA + SMALL_W]
    lane = _iota(blk_fox.shape, 1)
    small = jnp.where(lane < LANE_BETA, blk_fox, jnp.where(lane < LANE_GDEC + HEADS, blk_gdn, 0.0))
    o_ref[:, 13 * gw:13 * gw + SMALL_W] = small.astype(BF16)
    o_ref[:, 13 * gw + SMALL_W:] = jnp.zeros((w.shape[0], SMALL_W), BF16)


def w_in_prep(w):
    d, n = w.shape
    tr = 128
    return pl.pallas_call(
        _w_in_prep_kernel,
        grid=(d // tr,),
        in_specs=[pl.BlockSpec((tr, n), lambda i: (i, 0))],
        out_specs=pl.BlockSpec((tr, N_PROJ), lambda i: (i, 0)),
        out_shape=jax.ShapeDtypeStruct((d, N_PROJ), BF16),
        compiler_params=_cparams("parallel"),
        name="w_in_prep",
    )(w)


def _block_diag(w):
    h = w.shape[0]
    eye = jnp.eye(h, dtype=w.dtype)
    return (eye[:, None, :, None] * w[:, :, None, :]).reshape(h * HEAD_DIM, h * HEAD_DIM)


def kernel(x, mem, norm_mix, w_in, hgrn_lb, fox_fb, lru_conv_w, lru_conv_b, lru_wa, lru_ba, lru_wx, lru_bx, lru_lam, gdn_conv_w, gdn_a_log, gdn_dt_bias, mix_gain, w_out, norm_mem, norm_memkv, w_mq, w_mkv, w_mo, norm_ffn, w_rg, b_rg, w_re, b_re, w_e_gu, w_e_dn, norm_final):
    b, s, d = x.shape
    depth = w_in.shape[0]
    t_tok = b * s
    mlen = mem.shape[1]
    gw = GROUP_WIDTH

    lb_all = jnp.cumsum(jax.nn.softmax(hgrn_lb.astype(F32), axis=0), axis=0)
    lb_all = lb_all - lb_all[0]

    x = x.astype(F32)
    for l in range(depth):
        proj = norm_matmul(x.reshape(t_tok, d), norm_mix[l], w_in_prep(w_in[l]), tm=256)
        proj3 = proj.reshape(b, s, N_PROJ)
        prm = jnp.zeros((8, SMALL_W), F32)
        prm = prm.at[0, LANE_FOX:LANE_FOX + HEADS].set(fox_fb[l].astype(F32))
        prm = prm.at[0, LANE_GDEC:LANE_GDEC + HEADS].set(gdn_dt_bias[l].astype(F32))
        prm = prm.at[1, LANE_GDEC:LANE_GDEC + HEADS].set(gdn_a_log[l].astype(F32))
        sm, c_exp = small_prep(proj3, prm)
        c_row = jnp.pad(jnp.swapaxes(sm[:, :, LANE_FOX:LANE_FOX + HEADS], 1, 2), ((0, 0), (0, 8 - HEADS), (0, 0)))
        g_row = sm[:, :, LANE_GDEC:LANE_GDEC + HEADS].reshape(b, s // GDN_CHUNK, GDN_CHUNK, HEADS)
        g_row = jnp.swapaxes(g_row, 2, 3).reshape(b, s // GDN_CHUNK, 1, gw)
        g_row = jnp.broadcast_to(g_row, (b, s // GDN_CHUNK, 8, gw))

        ya = hgrn2(proj3, lb_all[l])
        yb = fox_attention(proj3, c_row, c_exp)
        yc = rglru(proj3, lru_conv_w[l], lru_conv_b[l], _block_diag(lru_wa[l]).astype(BF16), lru_ba[l],
                   _block_diag(lru_wx[l]).astype(BF16), lru_bx[l], lru_lam[l])
        yd = gdn(proj3, sm, g_row, gdn_conv_w[l])
        x = mix_out(x, ya, yb, yc, yd, mix_gain[l], w_out[l].astype(BF16))

        kv = norm_matmul(mem.reshape(b * mlen, d), norm_memkv[l], w_mkv[l].astype(BF16), tm=256)
        x = mem_attention(x, norm_mem[l], w_mq[l].astype(BF16), kv.reshape(b, mlen, 2 * gw), w_mo[l].astype(BF16))

        w_router = jnp.concatenate([w_re[l], w_rg[l], jnp.zeros((d, SMALL_W - N_EXPERTS - N_EXPERT_GROUPS), F32)], axis=1)
        b_router = jnp.concatenate([b_re[l], b_rg[l], jnp.zeros((SMALL_W - N_EXPERTS - N_EXPERT_GROUPS,), F32)]).reshape(1, SMALL_W)
        x = hier_moe_routed(x.reshape(t_tok, d), norm_ffn[l], w_router, b_router, w_e_gu[l], w_e_dn[l],
                            norm_final, final_norm=(l == depth - 1)).reshape(b, s, d)

    return x
```

```python
import functools
import math

import jax
import jax.numpy as jnp
from jax import lax
from jax.experimental import pallas as pl
from jax.experimental.pallas import tpu as pltpu

F32 = jnp.float32
BF16 = jnp.bfloat16

HEAD_DIM = 64
GROUP_WIDTH = 256
HEADS = GROUP_WIDTH // HEAD_DIM
GDN_CHUNK = 64
GDN_BATCH = 2
HGRN_CHUNK = 16
CONV_W = 4
LRU_C = 8.0
EPS = 1e-6
N_EXPERT_GROUPS = 4
EXPERTS_PER_GROUP = 8
N_EXPERTS = N_EXPERT_GROUPS * EXPERTS_PER_GROUP
D_EXPERT = 256
EXPERT_TILE = 256
SMALL_W = 128
SEQ_TILE = 256
VMEM_LIMIT = 56 * 1024 * 1024

(COL_AQ, COL_AF, COL_AI, COL_AG, COL_BQ, COL_BK, COL_BV, COL_CX, COL_CG,
 COL_DQ, COL_DK, COL_DV, COL_DZ) = range(13)
N_PROJ = 13 * GROUP_WIDTH + 2 * SMALL_W
COL_SMALL = 13 * GROUP_WIDTH // SMALL_W
LANE_FOX = 0
LANE_BETA = 4
LANE_GDEC = 8


def _cparams(*sem):
    return pltpu.CompilerParams(dimension_semantics=sem, vmem_limit_bytes=VMEM_LIMIT)


def _dot(a, b):
    return jnp.dot(a, b, preferred_element_type=F32)


def _dot_nt(a, b):
    return lax.dot_general(a, b, (((1,), (1,)), ((), ())), preferred_element_type=F32)


def _dot_tn(a, b):
    return lax.dot_general(a, b, (((0,), (0,)), ((), ())), preferred_element_type=F32)


def _split3(x):
    h = x.astype(BF16)
    r = x - h.astype(F32)
    m = r.astype(BF16)
    l = (r - m.astype(F32)).astype(BF16)
    return h, m, l


def _dot_exact_rhs(x, w_bf16):
    h, m, l = _split3(x)
    return _dot(h, w_bf16) + _dot(m, w_bf16) + _dot(l, w_bf16)


def _dot_exact_lhs(w_bf16, x):
    h, m, l = _split3(x)
    return _dot(w_bf16, h) + _dot(w_bf16, m) + _dot(w_bf16, l)


def _iota(shape, dim):
    return lax.broadcasted_iota(jnp.int32, shape, dim)


def _head_ones(n=GROUP_WIDTH):
    r = _iota((n, n), 0) // HEAD_DIM
    c = _iota((n, n), 1) // HEAD_DIM
    return r == c


def _sigmoid(x):
    return 1.0 / (1.0 + jnp.exp(-x))


def _silu(x):
    return x * _sigmoid(x)


def _log_sigmoid(x):
    return jnp.minimum(x, 0.0) - jnp.log1p(jnp.exp(-jnp.abs(x)))


def _softplus(x):
    return jnp.maximum(x, 0.0) + jnp.log1p(jnp.exp(-jnp.abs(x)))


def _gelu_tanh(x):
    return 0.5 * x * (1.0 + jnp.tanh(math.sqrt(2.0 / math.pi) * (x + 0.044715 * (x * x * x))))


def _head_mean_sq(x, ones_bf16):
    return _dot_exact_rhs(x * x, ones_bf16) * (1.0 / HEAD_DIM)


def _head_rms(x, ones_bf16):
    return x * lax.rsqrt(_head_mean_sq(x, ones_bf16) + EPS)


def _stack_heads(x):
    lane_head = _iota(x.shape, 1) // HEAD_DIM
    parts = []
    for h in range(HEADS):
        parts.append(jnp.where(lane_head == h, x, 0.0))
    return jnp.concatenate(parts, axis=0)


def _unstack_heads(xs, rows):
    out = xs[0:rows]
    for h in range(1, HEADS):
        out = out + xs[h * rows:(h + 1) * rows]
    return out


def _causal_conv4(x, prev8, w):
    r = x.shape[0]
    row8 = _iota((8, x.shape[1]), 0)
    acc = x * w[CONV_W - 1:CONV_W, :]
    for k in range(1, CONV_W):
        xs = pltpu.roll(x, k, 0)
        ps = pltpu.roll(prev8, k, 0)
        top = jnp.where(row8 < k, ps, xs[0:8])
        xs = jnp.concatenate([top, xs[8:r]], axis=0)
        acc = acc + xs * w[CONV_W - 1 - k:CONV_W - k, :]
    return acc


def _norm_matmul_kernel(x_ref, g_ref, w_ref, o_ref):
    x = x_ref[...]
    ms = jnp.mean(x * x, axis=-1, keepdims=True)
    h = (x * lax.rsqrt(ms + EPS)) * g_ref[...]
    o_ref[...] = _dot(h.astype(BF16), w_ref[...]).astype(o_ref.dtype)


def norm_matmul(x, g, w_bf16, tm, out_dtype=F32):
    m, d = x.shape
    n = w_bf16.shape[1]
    return pl.pallas_call(
        _norm_matmul_kernel,
        grid=(m // tm,),
        in_specs=[pl.BlockSpec((tm, d), lambda i: (i, 0)),
                  pl.BlockSpec((1, d), lambda i: (0, 0)),
                  pl.BlockSpec((d, n), lambda i: (0, 0))],
        out_specs=pl.BlockSpec((tm, n), lambda i: (i, 0)),
        out_shape=jax.ShapeDtypeStruct((m, n), out_dtype),
        compiler_params=_cparams("parallel"),
        name="norm_matmul",
    )(x, g.reshape(1, d), w_bf16)


def _small_prep_kernel(s_ref, p_ref, o_ref, x_ref):
    S = s_ref.shape[1]
    W = s_ref.shape[2]
    blk = SEQ_TILE
    lane = _iota((blk, W), 1)
    er = _iota((W, 2 * HEADS * W), 0)
    ec = _iota((W, 2 * HEADS * W), 1) // W
    expand = jnp.where(er == jnp.where(ec < HEADS, LANE_FOX + ec, LANE_GDEC + ec - HEADS), 1.0, 0.0).astype(BF16)
    is_fox = (lane >= LANE_FOX) & (lane < LANE_FOX + HEADS)
    is_beta = (lane >= LANE_BETA) & (lane < LANE_BETA + HEADS)
    is_gdec = (lane >= LANE_GDEC) & (lane < LANE_GDEC + HEADS)
    r = _iota((blk, blk), 0)
    c = _iota((blk, blk), 1)
    tril_all = jnp.where(r >= c, 1.0, 0.0).astype(BF16)
    tril_chunk = jnp.where((r >= c) & (r // GDN_CHUNK == c // GDN_CHUNK), 1.0, 0.0).astype(BF16)
    neg_exp_a = -jnp.exp(p_ref[1:2, :])
    carry = jnp.zeros((1, W), F32)
    for i in range(S // blk):
        sl = pl.ds(i * blk, blk)
        z = s_ref[0, sl, :] + p_ref[0:1, :]
        fox = jnp.where(is_fox, _log_sigmoid(z), 0.0)
        beta = jnp.where(is_beta, _sigmoid(z), 0.0)
        gdec = jnp.where(is_gdec, neg_exp_a * _softplus(z), 0.0)
        cf = _dot_exact_lhs(tril_all, fox) + carry
        cg = _dot_exact_lhs(tril_chunk, gdec)
        carry = cf[blk - 1:blk, :]
        o_ref[0, sl, :] = cf + cg + beta
        x_ref[0, sl, :] = _dot_exact_rhs(cf + cg, expand)


def small_prep(proj3, params):
    b, s, _ = proj3.shape
    w = SMALL_W
    return pl.pallas_call(
        _small_prep_kernel,
        grid=(b,),
        in_specs=[pl.BlockSpec((1, s, w), lambda i: (i, 0, COL_SMALL)),
                  pl.BlockSpec((8, w), lambda i: (0, 0))],
        out_specs=[pl.BlockSpec((1, s, w), lambda i: (i, 0, 0)),
                   pl.BlockSpec((1, s, 2 * HEADS * w), lambda i: (i, 0, 0))],
        out_shape=[jax.ShapeDtypeStruct((b, s, w), F32), jax.ShapeDtypeStruct((b, s, 2 * HEADS * w), F32)],
        compiler_params=_cparams("parallel"),
        name="small_prep",
    )(proj3, params)


def _hgrn2_kernel(q_ref, f_ref, i_ref, g_ref, lb_ref, o_ref, st_ref, qs, ks, vs, bs, os_):
    T = q_ref.shape[1]
    C = HGRN_CHUNK

    @pl.when(pl.program_id(1) == 0)
    def _():
        st_ref[...] = jnp.zeros_like(st_ref)

    same_head = _head_ones()
    ones_bf = jnp.where(same_head, 1.0, 0.0).astype(BF16)
    lb = lb_ref[...]
    fl = f_ref[0]
    log_lb = jnp.log(lb)
    c2 = jnp.log1p(-lb) + _log_sigmoid(fl)
    mx = jnp.maximum(log_lb, c2)
    log_f = mx + jnp.log1p(jnp.exp(-jnp.abs(log_lb - c2)))
    r = _iota((T, T), 0)
    c = _iota((T, T), 1)
    tril_chunk = jnp.where((r >= c) & (r // C == c // C), 1.0, 0.0).astype(BF16)
    bs[...] = _dot_exact_lhs(tril_chunk, log_f)
    qs[...] = _silu(q_ref[0])
    ks[...] = (1.0 - lb) * _sigmoid(-fl)
    vs[...] = i_ref[0]

    trow = _iota((C, GROUP_WIDTH), 0)

    def chunk(ci, carry):
        r0 = pl.multiple_of(ci * C, C)
        qc = qs[pl.ds(r0, C), :]
        kc = ks[pl.ds(r0, C), :]
        vc = vs[pl.ds(r0, C), :]
        bc = bs[pl.ds(r0, C), :]
        st = st_ref[...]
        o = _dot_nt((qc * jnp.exp(bc)).astype(BF16), st.astype(BF16))
        parts = []
        for s in range(C):
            parts.append(jnp.where(trow >= s, jnp.exp(bc - bc[s:s + 1, :]) * (qc * kc[s:s + 1, :]), 0.0))
        a = jnp.concatenate(parts, axis=0).astype(BF16)
        sc = _dot(a, ones_bf)
        for s in range(C):
            o = o + sc[s * C:(s + 1) * C, :] * vc[s:s + 1, :]
        os_[pl.ds(r0, C), :] = o
        b_last = bc[C - 1:C, :]
        kd = kc * jnp.exp(b_last - bc)
        upd = _dot_tn(vc.astype(BF16), kd.astype(BF16))
        st_ref[...] = st * jnp.exp(b_last) + jnp.where(same_head, upd, 0.0)
        return carry

    lax.fori_loop(0, T // C, chunk, 0, unroll=2)
    o_ref[0] = _head_rms(os_[...], ones_bf) * _silu(g_ref[0])


def hgrn2(proj3, lb):
    b, s, _ = proj3.shape
    t = SEQ_TILE
    col = lambda cidx: pl.BlockSpec((1, t, GROUP_WIDTH), lambda i, j: (i, j, cidx))
    return pl.pallas_call(
        _hgrn2_kernel,
        grid=(b, s // t),
        in_specs=[col(COL_AQ), col(COL_AF), col(COL_AI), col(COL_AG),
                  pl.BlockSpec((1, GROUP_WIDTH), lambda i, j: (0, 0))],
        out_specs=pl.BlockSpec((1, t, GROUP_WIDTH), lambda i, j: (i, j, 0)),
        out_shape=jax.ShapeDtypeStruct((b, s, GROUP_WIDTH), F32),
        scratch_shapes=[pltpu.VMEM((GROUP_WIDTH, GROUP_WIDTH), F32)] + [pltpu.VMEM((t, GROUP_WIDTH), F32)] * 5,
        compiler_params=_cparams("parallel", "arbitrary"),
        name="hgrn2",
    )(proj3, proj3, proj3, proj3, lb.reshape(1, GROUP_WIDTH))


def _fox_kernel(q_ref, k_ref, v_ref, cr_ref, cx_ref, o_ref, kb_ref, vt_ref, acc_ref):
    tq = q_ref.shape[1]
    tk = tq
    S = k_ref.shape[1]
    n = HEADS * tq
    qi = pl.program_id(1)

    @pl.when(qi == 0)
    def _():
        for i in range(S // tk):
            sl = pl.ds(i * tk, tk)
            kb_ref[sl, :] = k_ref[0, sl, :].astype(BF16)
            vt_ref[:, sl] = v_ref[0, sl, :].T.astype(BF16)

    qs = _stack_heads(q_ref[0] * (HEAD_DIM ** -0.5)).astype(BF16)
    q0 = pl.multiple_of(qi * tq, tq)
    cr = cr_ref[0, :, pl.ds(q0, tq)]
    cq_row = jnp.concatenate([cr[h:h + 1, :] for h in range(HEADS)], axis=1)
    acc_ref[...] = jnp.zeros_like(acc_ref)
    reps = tq // SMALL_W

    def step(kb, m_old, l_old, masked):
        k0 = pl.multiple_of(kb * tk, tk)
        st = _dot_nt(kb_ref[pl.ds(k0, tk), :], qs)
        cx = cx_ref[0, pl.ds(k0, tk), :]
        ck = jnp.concatenate([cx[:, h * SMALL_W:(h + 1) * SMALL_W] for h in range(HEADS) for _ in range(reps)], axis=1)
        st = st + (cq_row - ck)
        if masked:
            st = jnp.where(_iota((tk, n), 0) <= _iota((tk, n), 1) % tq, st, -jnp.inf)
        m_new = jnp.maximum(m_old, jnp.max(st, axis=0, keepdims=True))
        alpha = jnp.exp(m_old - m_new)
        p = jnp.exp(st - m_new)
        l_new = alpha * l_old + jnp.sum(p, axis=0, keepdims=True)
        acc_ref[...] = alpha * acc_ref[...] + _dot(vt_ref[:, pl.ds(k0, tk)], p.astype(BF16))
        return m_new, l_new

    m0 = jnp.full((1, n), -jnp.inf, F32)
    l0 = jnp.zeros((1, n), F32)
    m1, l1 = lax.fori_loop(0, qi, lambda kb, c: step(kb, c[0], c[1], False), (m0, l0))
    _, l2 = step(qi, m1, l1, True)

    acc = acc_ref[...] * (1.0 / l2)
    ot = jnp.concatenate([acc[h * HEAD_DIM:(h + 1) * HEAD_DIM, h * tq:(h + 1) * tq] for h in range(HEADS)], axis=0)
    ones_bf = jnp.where(_head_ones(), 1.0, 0.0).astype(BF16)
    o_ref[0] = _head_rms(ot.T, ones_bf)


def fox_attention(proj3, c_row, c_exp):
    b, s, _ = proj3.shape
    tq = SEQ_TILE
    gw = GROUP_WIDTH
    return pl.pallas_call(
        _fox_kernel,
        grid=(b, s // tq),
        in_specs=[pl.BlockSpec((1, tq, gw), lambda i, j: (i, j, COL_BQ)),
                  pl.BlockSpec((1, s, gw), lambda i, j: (i, 0, COL_BK)),
                  pl.BlockSpec((1, s, gw), lambda i, j: (i, 0, COL_BV)),
                  pl.BlockSpec((1, 8, s), lambda i, j: (i, 0, 0)),
                  pl.BlockSpec((1, s, HEADS * SMALL_W), lambda i, j: (i, 0, 0))],
        out_specs=pl.BlockSpec((1, tq, gw), lambda i, j: (i, j, 0)),
        out_shape=jax.ShapeDtypeStruct((b, s, gw), F32),
        scratch_shapes=[pltpu.VMEM((s, gw), BF16),
                        pltpu.VMEM((gw, s), BF16),
                        pltpu.VMEM((gw, HEADS * tq), F32)],
        compiler_params=_cparams("parallel", "arbitrary"),
        name="fox_attention",
    )(proj3, proj3, proj3, c_row, c_exp)


def _rglru_kernel(x_ref, g_ref, cw_ref, cb_ref, wa_ref, ba_ref, wx_ref, bx_ref, lam_ref, o_ref, prev_ref, h_ref):
    @pl.when(pl.program_id(1) == 0)
    def _():
        prev_ref[...] = jnp.zeros_like(prev_ref)
        h_ref[...] = jnp.zeros_like(h_ref)

    x = x_ref[0]
    t = x.shape[0]
    xc = _causal_conv4(x, prev_ref[...], cw_ref[...]) + cb_ref[...]
    prev_ref[...] = x[t - 8:t, :]
    xb = xc.astype(BF16)
    r = _sigmoid(_dot(xb, wa_ref[...]) + ba_ref[...])
    ig = _sigmoid(_dot(xb, wx_ref[...]) + bx_ref[...])
    log_a = (-LRU_C * r) * _softplus(-lam_ref[...])
    a = jnp.exp(log_a)
    u = jnp.sqrt(1.0 - jnp.exp(2.0 * log_a)) * (ig * xc)
    row = _iota(a.shape, 0)
    d = 1
    while d < t:
        valid = row >= d
        u = jnp.where(valid, a * pltpu.roll(u, d, 0) + u, u)
        a = jnp.where(valid, a * pltpu.roll(a, d, 0), a)
        d *= 2
    h = a * h_ref[...] + u
    h_ref[...] = h[t - 1:t, :]
    ones_bf = jnp.where(_head_ones(), 1.0, 0.0).astype(BF16)
    o_ref[0] = _head_rms(h * _gelu_tanh(g_ref[0]), ones_bf)


def rglru(proj3, conv_w, conv_b, wa_bd, ba, wx_bd, bx, lam):
    b, s, _ = proj3.shape
    t = SEQ_TILE
    gw = GROUP_WIDTH
    row = lambda: pl.BlockSpec((1, gw), lambda i, j: (0, 0))
    return pl.pallas_call(
        _rglru_kernel,
        grid=(b, s // t),
        in_specs=[pl.BlockSpec((1, t, gw), lambda i, j: (i, j, COL_CX)),
                  pl.BlockSpec((1, t, gw), lambda i, j: (i, j, COL_CG)),
                  pl.BlockSpec((CONV_W, gw), lambda i, j: (0, 0)), row(),
                  pl.BlockSpec((gw, gw), lambda i, j: (0, 0)), row(),
                  pl.BlockSpec((gw, gw), lambda i, j: (0, 0)), row(), row()],
        out_specs=pl.BlockSpec((1, t, gw), lambda i, j: (i, j, 0)),
        out_shape=jax.ShapeDtypeStruct((b, s, gw), F32),
        scratch_shapes=[pltpu.VMEM((8, gw), F32), pltpu.VMEM((1, gw), F32)],
        compiler_params=_cparams("parallel", "arbitrary"),
        name="rglru",
    )(proj3, proj3, conv_w, conv_b.reshape(1, gw), wa_bd, ba.reshape(1, gw), wx_bd, bx.reshape(1, gw), lam.reshape(1, gw))


def _gdn_kernel(q_ref, k_ref, v_ref, z_ref, sm_ref, ge_ref, gr_ref, cw_ref, o_ref, st_ref, pq_ref, pk_ref, pv_ref):
    nb = q_ref.shape[0]
    T = q_ref.shape[1]
    C = GDN_CHUNK

    @pl.when(pl.program_id(1) == 0)
    def _():
        st_ref[...] = jnp.zeros_like(st_ref)
        pq_ref[...] = jnp.zeros_like(pq_ref)
        pk_ref[...] = jnp.zeros_like(pk_ref)
        pv_ref[...] = jnp.zeros_like(pv_ref)

    same_head = _head_ones()
    ones_bf = jnp.where(same_head, 1.0, 0.0).astype(BF16)
    cw = cw_ref[...]
    gw = GROUP_WIDTH
    er = _iota((SMALL_W, gw), 0)
    ec = _iota((SMALL_W, gw), 1) // HEAD_DIM
    exp_beta = jnp.where(er == LANE_BETA + ec, 1.0, 0.0).astype(BF16)
    exp_g = jnp.where(er == LANE_GDEC + ec, 1.0, 0.0).astype(BF16)
    n = HEADS * C
    rr = _iota((n, n), 0)
    cc = _iota((n, n), 1)
    incl = same_head & (rr % C >= cc % C)
    strict = same_head & (rr % C > cc % C)

    prep = []
    for bb in range(nb):
        xq, xk, xv = q_ref[bb], k_ref[bb], v_ref[bb]
        q = _silu(_causal_conv4(xq, pq_ref[bb], cw[:, 0:gw]))
        k = _silu(_causal_conv4(xk, pk_ref[bb], cw[:, gw:2 * gw]))
        v = _silu(_causal_conv4(xv, pv_ref[bb], cw[:, 2 * gw:3 * gw]))
        pq_ref[bb] = xq[T - 8:T, :]
        pk_ref[bb] = xk[T - 8:T, :]
        pv_ref[bb] = xv[T - 8:T, :]
        q = q * lax.rsqrt(_dot_exact_rhs(q * q, ones_bf) + EPS) * (HEAD_DIM ** -0.5)
        k = k * lax.rsqrt(_dot_exact_rhs(k * k, ones_bf) + EPS)
        sm = sm_ref[bb]
        betax = _dot_exact_rhs(sm, exp_beta)
        gx = _dot_exact_rhs(sm, exp_g)
        prep.append((q, k, v, betax, gx))

    reps = gw // SMALL_W
    n_chunks = T // C
    eye = jnp.where(rr == cc, 1.0, 0.0)

    chains = []
    for ci in range(n_chunks):
        sl = slice(ci * C, (ci + 1) * C)
        for bb in range(nb):
            q, k, v, betax, gx = prep[bb]
            qc, kc, vc, bx, gc = q[sl], k[sl], v[sl], betax[sl], gx[sl]
            grow = gr_ref[bb, ci, 0:1, :]
            ge = ge_ref[bb, sl, :]
            gcol = jnp.concatenate(
                [jnp.concatenate([ge[:, h * SMALL_W:(h + 1) * SMALL_W]] * reps, axis=1) for h in range(HEADS)], axis=0)
            eg = jnp.exp(gc)
            kb = kc * bx
            g_last = gc[C - 1:C, :]
            gamma = jnp.where(incl, jnp.exp(jnp.where(incl, gcol - grow, 0.0)), 0.0)
            ks_bf = _stack_heads(kc).astype(BF16)
            m = jnp.where(strict, _dot_nt(_stack_heads(kb).astype(BF16), ks_bf) * gamma, 0.0)
            chains.append(dict(
                x=eye - m, y=m,
                a_qk=(_dot_nt(_stack_heads(qc).astype(BF16), ks_bf) * gamma).astype(BF16),
                rhs_u=_stack_heads(vc * bx).astype(BF16),
                rhs_w=_stack_heads(kb * eg).astype(BF16),
                q_dec=_stack_heads(qc * eg).astype(BF16),
                k_dec=_stack_heads(kc * jnp.exp(g_last - gc)).astype(BF16),
                decay=jnp.exp(g_last)))
    for _ in range(5):
        for ch in chains:
            yb = ch["y"].astype(BF16)
            ch["y"] = _dot(yb, yb)
        for ch in chains:
            ch["x"] = ch["x"] + _dot(ch["x"].astype(BF16), ch["y"].astype(BF16))
    for ch in chains:
        t_inv = ch["x"].astype(BF16)
        ch["u"] = _dot(t_inv, ch["rhs_u"])
        ch["wk"] = _dot(t_inv, ch["rhs_w"]).astype(BF16)

    outs = [[] for _ in range(nb)]
    for ci in range(n_chunks):
        for bb in range(nb):
            ch = chains[ci * nb + bb]
            st = st_ref[bb]
            st_bf = st.astype(BF16)
            v_new = (ch["u"] - _dot(ch["wk"], st_bf)).astype(BF16)
            o = _dot(ch["q_dec"], st_bf) + _dot(ch["a_qk"], v_new)
            outs[bb].append(_unstack_heads(o, C))
            st_ref[bb] = st * ch["decay"] + _dot_tn(ch["k_dec"], v_new)
    for bb in range(nb):
        o = jnp.concatenate(outs[bb], axis=0)
        o_ref[bb] = _head_rms(o, ones_bf) * _silu(z_ref[bb])


def gdn(proj3, sm, c_exp, g_row, conv_w):
    b, s, _ = proj3.shape
    t = SEQ_TILE
    gw = GROUP_WIDTH
    nb = GDN_BATCH
    col = lambda cidx: pl.BlockSpec((nb, t, gw), lambda i, j: (i, j, cidx))
    return pl.pallas_call(
        _gdn_kernel,
        grid=(b // nb, s // t),
        in_specs=[col(COL_DQ), col(COL_DK), col(COL_DV), col(COL_DZ),
                  pl.BlockSpec((nb, t, SMALL_W), lambda i, j: (i, j, 0)),
                  pl.BlockSpec((nb, t, HEADS * SMALL_W), lambda i, j: (i, j, 1)),
                  pl.BlockSpec((nb, t // GDN_CHUNK, 8, gw), lambda i, j: (i, j, 0, 0)),
                  pl.BlockSpec((CONV_W, 3 * gw), lambda i, j: (0, 0))],
        out_specs=pl.BlockSpec((nb, t, gw), lambda i, j: (i, j, 0)),
        out_shape=jax.ShapeDtypeStruct((b, s, gw), F32),
        scratch_shapes=[pltpu.VMEM((nb, gw, gw), F32)] + [pltpu.VMEM((nb, 8, gw), F32)] * 3,
        compiler_params=_cparams("parallel", "arbitrary"),
        name="gdn",
    )(proj3, proj3, proj3, proj3, sm, c_exp, g_row, conv_w)


def _mix_out_kernel(x_ref, ya_ref, yb_ref, yc_ref, yd_ref, gain_ref, w_ref, o_ref):
    gw = GROUP_WIDTH
    gain = gain_ref[...]
    acc = x_ref[0]
    for i, y_ref in enumerate((ya_ref, yb_ref, yc_ref, yd_ref)):
        yg = (y_ref[0] * gain[:, i * gw:(i + 1) * gw]).astype(BF16)
        acc = acc + _dot(yg, w_ref[i * gw:(i + 1) * gw, :])
    o_ref[0] = acc


def mix_out(x3, ya, yb, yc, yd, gain, w_out_bf16):
    b, s, d = x3.shape
    t = SEQ_TILE
    gw = GROUP_WIDTH
    grp = lambda: pl.BlockSpec((1, t, gw), lambda i, j: (i, j, 0))
    return pl.pallas_call(
        _mix_out_kernel,
        grid=(b, s // t),
        in_specs=[pl.BlockSpec((1, t, d), lambda i, j: (i, j, 0)),
                  grp(), grp(), grp(), grp(),
                  pl.BlockSpec((1, 4 * gw), lambda i, j: (0, 0)),
                  pl.BlockSpec((4 * gw, d), lambda i, j: (0, 0))],
        out_specs=pl.BlockSpec((1, t, d), lambda i, j: (i, j, 0)),
        out_shape=jax.ShapeDtypeStruct((b, s, d), F32),
        compiler_params=_cparams("parallel", "parallel"),
        name="mix_out",
    )(x3, ya, yb, yc, yd, gain.reshape(1, 4 * gw), w_out_bf16)


def _mem_attn_kernel(x_ref, g_ref, wq_ref, kv_ref, wo_ref, o_ref):
    x = x_ref[0]
    t = x.shape[0]
    gw = GROUP_WIDTH
    ms = jnp.mean(x * x, axis=-1, keepdims=True)
    h = ((x * lax.rsqrt(ms + EPS)) * g_ref[...]).astype(BF16)
    q = _dot(h, wq_ref[...])
    kv = kv_ref[0]
    k = kv[:, 0:gw].astype(BF16)
    v = kv[:, gw:2 * gw].astype(BF16)
    s = _dot_nt(k, _stack_heads(q).astype(BF16)) * (HEAD_DIM ** -0.5)
    s = s - jnp.max(s, axis=0, keepdims=True)
    p = jnp.exp(s)
    p = p * (1.0 / jnp.sum(p, axis=0, keepdims=True))
    ot = _dot_tn(v, p.astype(BF16))
    oc = jnp.concatenate([ot[hh * HEAD_DIM:(hh + 1) * HEAD_DIM, hh * t:(hh + 1) * t] for hh in range(HEADS)], axis=0).T
    o_ref[0] = x + _dot(oc.astype(BF16), wo_ref[...])


def mem_attention(x3, g, wq_bf16, kv3, wo_bf16):
    b, s, d = x3.shape
    t = SEQ_TILE
    m = kv3.shape[1]
    return pl.pallas_call(
        _mem_attn_kernel,
        grid=(b, s // t),
        in_specs=[pl.BlockSpec((1, t, d), lambda i, j: (i, j, 0)),
                  pl.BlockSpec((1, d), lambda i, j: (0, 0)),
                  pl.BlockSpec((d, GROUP_WIDTH), lambda i, j: (0, 0)),
                  pl.BlockSpec((1, m, 2 * GROUP_WIDTH), lambda i, j: (i, 0, 0)),
                  pl.BlockSpec((GROUP_WIDTH, d), lambda i, j: (0, 0))],
        out_specs=pl.BlockSpec((1, t, d), lambda i, j: (i, j, 0)),
        out_shape=jax.ShapeDtypeStruct((b, s, d), F32),
        compiler_params=_cparams("parallel", "parallel"),
        name="mem_attention",
    )(x3, g.reshape(1, d), wq_bf16, kv3, wo_bf16)


def _router_kernel(x_ref, g_ref, w_ref, b_ref, h_ref, r_ref, n_ref, cnt_ref):
    @pl.when(pl.program_id(0) == 0)
    def _():
        cnt_ref[...] = jnp.zeros_like(cnt_ref)

    x = x_ref[...]
    tm = x.shape[0]
    ms = jnp.mean(x * x, axis=-1, keepdims=True)
    h = (x * lax.rsqrt(ms + EPS)) * g_ref[...]
    h_ref[...] = h
    hh, hm, _ = _split3(h)
    w = w_ref[...]
    wh, wm, _ = _split3(w)
    logits = _dot(hh, wh) + _dot(hh, wm) + _dot(hm, wh) + b_ref[...]
    lane = _iota(logits.shape, 1)
    big = jnp.int32(1 << 30)
    neg = -jnp.inf
    is_grp = (lane >= N_EXPERTS) & (lane < N_EXPERTS + N_EXPERT_GROUPS)
    gl = jnp.where(is_grp, logits, neg)
    gmax = jnp.max(gl, axis=-1, keepdims=True)
    p_grp = 1.0 / jnp.sum(jnp.exp(gl - gmax), axis=-1, keepdims=True)
    g_sel = jnp.min(jnp.where(gl == gmax, lane, big), axis=-1, keepdims=True) - N_EXPERTS
    in_grp = (lane < N_EXPERTS) & (lane // EXPERTS_PER_GROUP == g_sel)
    el = jnp.where(in_grp, logits, neg)
    m1 = jnp.max(el, axis=-1, keepdims=True)
    i1 = jnp.min(jnp.where(el == m1, lane, big), axis=-1, keepdims=True)
    el2 = jnp.where(lane == i1, neg, el)
    m2 = jnp.max(el2, axis=-1, keepdims=True)
    i2 = jnp.min(jnp.where(el2 == m2, lane, big), axis=-1, keepdims=True)
    e21 = jnp.exp(m2 - m1)
    w1 = 1.0 / (1.0 + e21)
    w2 = e21 / (1.0 + e21)
    oh1 = lane == i1
    oh2 = lane == i2
    onehot = jnp.where(oh1 | oh2, 1.0, 0.0)
    strict = jnp.where(_iota((tm, tm), 0) > _iota((tm, tm), 1), 1.0, 0.0).astype(BF16)
    before = _dot(strict, onehot.astype(BF16)) + cnt_ref[...]
    rank1 = jnp.sum(jnp.where(oh1, before, 0.0), axis=-1, keepdims=True)
    rank2 = jnp.sum(jnp.where(oh2, before, 0.0), axis=-1, keepdims=True)
    cnt = before[tm - 1:tm, :] + onehot[tm - 1:tm, :]
    cnt_ref[...] = cnt
    n_ref[...] = jnp.broadcast_to(cnt, n_ref.shape)
    cols = (i1.astype(F32), i2.astype(F32), p_grp * w1, p_grp * w2, rank1, rank2)
    route = jnp.zeros(logits.shape, F32)
    for j, col in enumerate(cols):
        route = jnp.where(lane == j, col, route)
    r_ref[...] = route


def router(x2, g, w_router, b_router, tm):
    m, d = x2.shape
    return pl.pallas_call(
        _router_kernel,
        grid=(m // tm,),
        in_specs=[pl.BlockSpec((tm, d), lambda i: (i, 0)),
                  pl.BlockSpec((1, d), lambda i: (0, 0)),
                  pl.BlockSpec((d, SMALL_W), lambda i: (0, 0)),
                  pl.BlockSpec((1, SMALL_W), lambda i: (0, 0))],
        out_specs=[pl.BlockSpec((tm, d), lambda i: (i, 0)),
                   pl.BlockSpec((tm, SMALL_W), lambda i: (i, 0)),
                   pl.BlockSpec((8, SMALL_W), lambda i: (0, 0))],
        out_shape=[jax.ShapeDtypeStruct((m, d), F32), jax.ShapeDtypeStruct((m, SMALL_W), F32),
                   jax.ShapeDtypeStruct((8, SMALL_W), F32)],
        scratch_shapes=[pltpu.VMEM((1, SMALL_W), F32)],
        compiler_params=_cparams("arbitrary"),
        name="router",
    )(x2, g.reshape(1, d), w_router, b_router)


def _row_copy(src_ref, src_row, dst_ref, dst_row, sem):
    return pltpu.make_async_copy(src_ref.at[pl.ds(src_row, 1)], dst_ref.at[pl.ds(dst_row, 1)], sem)


def _dispatch_kernel(dest_ref, zrow_ref, h_ref, xs_ref, zbuf, sem):
    tm = h_ref.shape[0]
    base = pl.program_id(0) * tm

    @pl.when(pl.program_id(0) == 0)
    def _():
        zbuf[...] = jnp.zeros_like(zbuf)

        def zero_copy(e):
            row = pl.multiple_of(zrow_ref[e], EXPERT_TILE)
            return pltpu.make_async_copy(zbuf, xs_ref.at[pl.ds(row, EXPERT_TILE)], sem)

        for e in range(zrow_ref.shape[0]):
            @pl.when(zrow_ref[e] >= 0)
            def _():
                zero_copy(e).start()
        for e in range(zrow_ref.shape[0]):
            @pl.when(zrow_ref[e] >= 0)
            def _():
                zero_copy(e).wait()

    def start(t, c):
        for slot in range(2):
            _row_copy(h_ref, t, xs_ref, dest_ref[2 * (base + t) + slot], sem).start(priority=slot)
        return c

    def wait(t, c):
        for slot in range(2):
            _row_copy(h_ref, t, xs_ref, dest_ref[2 * (base + t) + slot], sem).wait()
        return c

    lax.fori_loop(0, tm, start, 0, unroll=8)
    lax.fori_loop(0, tm, wait, 0, unroll=8)


def moe_dispatch(h, dest, zrow, n_rows, tm):
    m, d = h.shape
    return pl.pallas_call(
        _dispatch_kernel,
        grid_spec=pltpu.PrefetchScalarGridSpec(
            num_scalar_prefetch=2,
            grid=(m // tm,),
            in_specs=[pl.BlockSpec((tm, d), lambda i, dest, zrow: (i, 0))],
            out_specs=pl.BlockSpec(memory_space=pl.ANY),
            scratch_shapes=[pltpu.VMEM((EXPERT_TILE, d), F32), pltpu.SemaphoreType.DMA]),
        out_shape=jax.ShapeDtypeStruct((n_rows, d), F32),
        compiler_params=_cparams("arbitrary"),
        name="moe_dispatch",
    )(dest, zrow, h)


def _experts_kernel(te_ref, nu_ref, xs_ref, wgu_ref, wdn_ref, y_ref, wgu_bf, wdn_bf):
    i = pl.program_id(0)

    @pl.when(i >= nu_ref[0])
    def _():
        y_ref[...] = jnp.zeros_like(y_ref)

    @pl.when(i < nu_ref[0])
    def _():
        prev = te_ref[jnp.maximum(i - 1, 0)]

        @pl.when((i == 0) | (te_ref[i] != prev))
        def _():
            wgu_bf[...] = wgu_ref[0, 0].astype(BF16)
            wdn_bf[...] = wdn_ref[0, 0].astype(BF16)

        gu = _dot(xs_ref[...].astype(BF16), wgu_bf[...])
        act = _silu(gu[:, 0:D_EXPERT]) * gu[:, D_EXPERT:2 * D_EXPERT]
        y_ref[...] = _dot(act.astype(BF16), wdn_bf[...])


def moe_experts(xs, tile_expert, n_used, w_gu, w_dn, layer, tm):
    n_rows, d = xs.shape
    tile = lambda i, te, nu: (jnp.minimum(i, nu[0] - 1), 0)
    wsel = lambda i, te, nu: (layer, te[jnp.minimum(i, nu[0] - 1)], 0, 0)
    return pl.pallas_call(
        _experts_kernel,
        grid_spec=pltpu.PrefetchScalarGridSpec(
            num_scalar_prefetch=2,
            grid=(n_rows // tm,),
            in_specs=[pl.BlockSpec((tm, d), tile),
                      pl.BlockSpec((1, 1, d, 2 * D_EXPERT), wsel),
                      pl.BlockSpec((1, 1, D_EXPERT, d), wsel)],
            out_specs=pl.BlockSpec((tm, d), lambda i, te, nu: (i, 0)),
            scratch_shapes=[pltpu.VMEM((d, 2 * D_EXPERT), BF16), pltpu.VMEM((D_EXPERT, d), BF16)]),
        out_shape=jax.ShapeDtypeStruct((n_rows, d), F32),
        compiler_params=_cparams("arbitrary"),
        name="moe_experts",
    )(tile_expert, n_used, xs, w_gu, w_dn)


def _combine_kernel(dest_ref, x_ref, r_ref, g_ref, ys_ref, o_ref, ybuf, sem, *, final_norm):
    tm = x_ref.shape[0]
    base = pl.program_id(0) * tm

    def start(t, c):
        for slot in range(2):
            _row_copy(ys_ref, dest_ref[2 * (base + t) + slot], ybuf.at[slot], t, sem).start(priority=slot)
        return c

    def wait(t, c):
        for slot in range(2):
            _row_copy(ys_ref, dest_ref[2 * (base + t) + slot], ybuf.at[slot], t, sem).wait()
        return c

    lax.fori_loop(0, tm, start, 0, unroll=8)
    r = r_ref[...]
    g1 = r[:, 2:3]
    g2 = r[:, 3:4]
    lax.fori_loop(0, tm, wait, 0, unroll=8)
    out = x_ref[...] + g1 * ybuf[0] + g2 * ybuf[1]
    if final_norm:
        ms = jnp.mean(out * out, axis=-1, keepdims=True)
        out = (out * lax.rsqrt(ms + EPS)) * g_ref[...]
    o_ref[...] = out


def moe_combine(x2, route, dest, ys, g_final, tm, final_norm):
    m, d = x2.shape
    return pl.pallas_call(
        functools.partial(_combine_kernel, final_norm=final_norm),
        grid_spec=pltpu.PrefetchScalarGridSpec(
            num_scalar_prefetch=1,
            grid=(m // tm,),
            in_specs=[pl.BlockSpec((tm, d), lambda i, dest: (i, 0)),
                      pl.BlockSpec((tm, SMALL_W), lambda i, dest: (i, 0)),
                      pl.BlockSpec((1, d), lambda i, dest: (0, 0)),
                      pl.BlockSpec(memory_space=pl.ANY)],
            out_specs=pl.BlockSpec((tm, d), lambda i, dest: (i, 0)),
            scratch_shapes=[pltpu.VMEM((2, tm, d), F32), pltpu.SemaphoreType.DMA]),
        out_shape=jax.ShapeDtypeStruct((m, d), F32),
        compiler_params=_cparams("arbitrary"),
        name="moe_combine",
    )(dest, x2, route, g_final.reshape(1, d), ys)


def hier_moe_routed(x2, g_ffn, w_router, b_router, w_gu, w_dn, layer, g_final, final_norm):
    m, d = x2.shape
    te_rows = EXPERT_TILE
    n_rows = 2 * m + N_EXPERTS * te_rows
    h, route, counts = router(x2, g_ffn, w_router, b_router, tm=512)
    cnt = counts[0, 0:N_EXPERTS].astype(jnp.int32)
    padded = (cnt + te_rows - 1) // te_rows * te_rows
    seg_end = jnp.cumsum(padded)
    seg_start = seg_end - padded
    idx = route[:, 0:2].astype(jnp.int32)
    onehot = idx[:, :, None] == jnp.arange(N_EXPERTS, dtype=jnp.int32)
    dest = (jnp.sum(jnp.where(onehot, seg_start, 0), axis=-1) + route[:, 4:6].astype(jnp.int32)).reshape(2 * m)
    tile_start = jnp.arange(n_rows // te_rows, dtype=jnp.int32) * te_rows
    tile_expert = jnp.minimum(jnp.sum((tile_start[:, None] >= seg_end[None, :]).astype(jnp.int32), axis=1), N_EXPERTS - 1)
    n_used = (seg_end[N_EXPERTS - 1] // te_rows).reshape(1)
    tail = n_used[0] + jnp.arange(N_EXPERTS, dtype=jnp.int32)
    zrow = jnp.concatenate([jnp.where(padded > 0, seg_end - te_rows, -1),
                            jnp.where(tail < n_rows // te_rows, tail * te_rows, -1)])
    xs = moe_dispatch(h, dest, zrow, n_rows, tm=512)
    ys = moe_experts(xs, tile_expert, n_used, w_gu, w_dn, layer, tm=te_rows)
    return moe_combine(x2, route, dest, ys, g_final, tm=256, final_norm=final_norm)


def _w_in_prep_kernel(w_ref, o_ref):
    gw = GROUP_WIDTH
    a_end = 7 * gw
    c_start = a_end + HEADS
    c_end = c_start + 5 * gw
    z_start = c_end + 2 * HEADS
    assert a_end % SMALL_W == LANE_FOX and c_end % SMALL_W == LANE_BETA and LANE_GDEC == LANE_BETA + HEADS
    w = w_ref[...]
    o_ref[:, 0:a_end] = w[:, 0:a_end].astype(BF16)
    o_ref[:, a_end:a_end + 5 * gw] = w[:, c_start:c_end].astype(BF16)
    o_ref[:, 12 * gw:13 * gw] = w[:, z_start:z_start + gw].astype(BF16)
    blk_fox = w[:, a_end:a_end + SMALL_W]
    blk_gdn = w[:, c_end - LANE_BETA:c_end - LANE_BETA + SMALL_W]
    lane = _iota(blk_fox.shape, 1)
    small = jnp.where(lane < LANE_BETA, blk_fox, jnp.where(lane < LANE_GDEC + HEADS, blk_gdn, 0.0))
    o_ref[:, 13 * gw:13 * gw + SMALL_W] = small.astype(BF16)
    o_ref[:, 13 * gw + SMALL_W:] = jnp.zeros((w.shape[0], SMALL_W), BF16)


def w_in_prep(w_all, layer):
    _, d, n = w_all.shape
    tr = 128
    return pl.pallas_call(
        _w_in_prep_kernel,
        grid=(d // tr,),
        in_specs=[pl.BlockSpec((None, tr, n), lambda i: (layer, i, 0))],
        out_specs=pl.BlockSpec((tr, N_PROJ), lambda i: (i, 0)),
        out_shape=jax.ShapeDtypeStruct((d, N_PROJ), BF16),
        compiler_params=_cparams("parallel"),
        name="w_in_prep",
    )(w_all)


def _block_diag(w):
    h = w.shape[0]
    eye = jnp.eye(h, dtype=w.dtype)
    return (eye[:, None, :, None] * w[:, :, None, :]).reshape(h * HEAD_DIM, h * HEAD_DIM)


def kernel(x, mem, norm_mix, w_in, hgrn_lb, fox_fb, lru_conv_w, lru_conv_b, lru_wa, lru_ba, lru_wx, lru_bx, lru_lam, gdn_conv_w, gdn_a_log, gdn_dt_bias, mix_gain, w_out, norm_mem, norm_memkv, w_mq, w_mkv, w_mo, norm_ffn, w_rg, b_rg, w_re, b_re, w_e_gu, w_e_dn, norm_final):
    b, s, d = x.shape
    depth = w_in.shape[0]
    t_tok = b * s
    mlen = mem.shape[1]
    gw = GROUP_WIDTH

    lb_all = jnp.cumsum(jax.nn.softmax(hgrn_lb.astype(F32), axis=0), axis=0)
    lb_all = lb_all - lb_all[0]

    x = x.astype(F32)
    for l in range(depth):
        proj = norm_matmul(x.reshape(t_tok, d), norm_mix[l], w_in_prep(w_in, l), tm=256)
        proj3 = proj.reshape(b, s, N_PROJ)
        prm = jnp.zeros((8, SMALL_W), F32)
        prm = prm.at[0, LANE_FOX:LANE_FOX + HEADS].set(fox_fb[l].astype(F32))
        prm = prm.at[0, LANE_GDEC:LANE_GDEC + HEADS].set(gdn_dt_bias[l].astype(F32))
        prm = prm.at[1, LANE_GDEC:LANE_GDEC + HEADS].set(gdn_a_log[l].astype(F32))
        sm, c_exp = small_prep(proj3, prm)
        c_row = jnp.pad(jnp.swapaxes(sm[:, :, LANE_FOX:LANE_FOX + HEADS], 1, 2), ((0, 0), (0, 8 - HEADS), (0, 0)))
        g_row = sm[:, :, LANE_GDEC:LANE_GDEC + HEADS].reshape(b, s // GDN_CHUNK, GDN_CHUNK, HEADS)
        g_row = jnp.swapaxes(g_row, 2, 3).reshape(b, s // GDN_CHUNK, 1, gw)
        g_row = jnp.broadcast_to(g_row, (b, s // GDN_CHUNK, 8, gw))

        ya = hgrn2(proj3, lb_all[l])
        yb = fox_attention(proj3, c_row, c_exp)
        yc = rglru(proj3, lru_conv_w[l], lru_conv_b[l], _block_diag(lru_wa[l]).astype(BF16), lru_ba[l],
                   _block_diag(lru_wx[l]).astype(BF16), lru_bx[l], lru_lam[l])
        yd = gdn(proj3, sm, c_exp, g_row, gdn_conv_w[l])
        x = mix_out(x, ya, yb, yc, yd, mix_gain[l], w_out[l].astype(BF16))

        kv = norm_matmul(mem.reshape(b * mlen, d), norm_memkv[l], w_mkv[l].astype(BF16), tm=256)
        x = mem_attention(x, norm_mem[l], w_mq[l].astype(BF16), kv.reshape(b, mlen, 2 * gw), w_mo[l].astype(BF16))

        w_router = jnp.concatenate([w_re[l], w_rg[l], jnp.zeros((d, SMALL_W - N_EXPERTS - N_EXPERT_GROUPS), F32)], axis=1)
        b_router = jnp.concatenate([b_re[l], b_rg[l], jnp.zeros((SMALL_W - N_EXPERTS - N_EXPERT_GROUPS,), F32)]).reshape(1, SMALL_W)
        x = hier_moe_routed(x.reshape(t_tok, d), norm_ffn[l], w_router, b_router, w_e_gu, w_e_dn, l,
                            norm_final, final_norm=(l == depth - 1)).reshape(b, s, d)

    return x
```

```python
import functools
import math

import jax
import jax.numpy as jnp
from jax import lax
from jax.experimental import pallas as pl
from jax.experimental.pallas import tpu as pltpu

F32 = jnp.float32
BF16 = jnp.bfloat16

HEAD_DIM = 64
GROUP_WIDTH = 256
HEADS = GROUP_WIDTH // HEAD_DIM
GDN_CHUNK = 64
GDN_BATCH = 2
FOX_BATCH = 2
HGRN_BATCH = 2
HGRN_CHUNK = 16
CONV_W = 4
LRU_C = 8.0
EPS = 1e-6
N_EXPERT_GROUPS = 4
EXPERTS_PER_GROUP = 8
N_EXPERTS = N_EXPERT_GROUPS * EXPERTS_PER_GROUP
D_EXPERT = 256
EXPERT_TILE = 256
SMALL_W = 128
SEQ_TILE = 256
VMEM_LIMIT = 56 * 1024 * 1024

(COL_AQ, COL_AF, COL_AI, COL_AG, COL_BQ, COL_BK, COL_BV, COL_CX, COL_CG,
 COL_DQ, COL_DK, COL_DV, COL_DZ) = range(13)
N_PROJ = 13 * GROUP_WIDTH + 2 * SMALL_W
COL_SMALL = 13 * GROUP_WIDTH // SMALL_W
LANE_FOX = 0
LANE_BETA = 4
LANE_GDEC = 8


def _cparams(*sem):
    return pltpu.CompilerParams(dimension_semantics=sem, vmem_limit_bytes=VMEM_LIMIT)


def _dot(a, b):
    return jnp.dot(a, b, preferred_element_type=F32)


def _dot_nt(a, b):
    return lax.dot_general(a, b, (((1,), (1,)), ((), ())), preferred_element_type=F32)


def _dot_tn(a, b):
    return lax.dot_general(a, b, (((0,), (0,)), ((), ())), preferred_element_type=F32)


def _split3(x):
    h = x.astype(BF16)
    r = x - h.astype(F32)
    m = r.astype(BF16)
    l = (r - m.astype(F32)).astype(BF16)
    return h, m, l


def _dot_exact_rhs(x, w_bf16):
    h, m, l = _split3(x)
    return _dot(h, w_bf16) + _dot(m, w_bf16) + _dot(l, w_bf16)


def _dot_exact_lhs(w_bf16, x):
    h, m, l = _split3(x)
    return _dot(w_bf16, h) + _dot(w_bf16, m) + _dot(w_bf16, l)


def _iota(shape, dim):
    return lax.broadcasted_iota(jnp.int32, shape, dim)


def _head_ones(n=GROUP_WIDTH):
    r = _iota((n, n), 0) // HEAD_DIM
    c = _iota((n, n), 1) // HEAD_DIM
    return r == c


def _sigmoid(x):
    return 1.0 / (1.0 + jnp.exp(-x))


def _silu(x):
    return x * _sigmoid(x)


def _log_sigmoid(x):
    return jnp.minimum(x, 0.0) - jnp.log1p(jnp.exp(-jnp.abs(x)))


def _softplus(x):
    return jnp.maximum(x, 0.0) + jnp.log1p(jnp.exp(-jnp.abs(x)))


def _gelu_tanh(x):
    return 0.5 * x * (1.0 + jnp.tanh(math.sqrt(2.0 / math.pi) * (x + 0.044715 * (x * x * x))))


def _pack_bf16_pairs(x):
    n = x.shape[1] // 2
    u = lax.bitcast_convert_type(x.astype(BF16).astype(F32), jnp.uint32)
    return u[:, :n] | (u[:, n:] >> 16)


def _unpack_bf16_pairs(p):
    hi = lax.bitcast_convert_type(p & jnp.uint32(0xFFFF0000), F32)
    lo = lax.bitcast_convert_type(p << 16, F32)
    return jnp.concatenate([hi, lo], axis=1)


def _head_mean_sq(x, ones_bf16):
    return _dot_exact_rhs(x * x, ones_bf16) * (1.0 / HEAD_DIM)


def _head_rms(x, ones_bf16):
    return x * lax.rsqrt(_head_mean_sq(x, ones_bf16) + EPS)


def _stack_heads(x):
    lane_head = _iota(x.shape, 1) // HEAD_DIM
    parts = []
    for h in range(HEADS):
        parts.append(jnp.where(lane_head == h, x, 0.0))
    return jnp.concatenate(parts, axis=0)


def _unstack_heads(xs, rows):
    out = xs[0:rows]
    for h in range(1, HEADS):
        out = out + xs[h * rows:(h + 1) * rows]
    return out


def _causal_conv4(x, prev8, w):
    r = x.shape[0]
    row8 = _iota((8, x.shape[1]), 0)
    acc = x * w[CONV_W - 1:CONV_W, :]
    for k in range(1, CONV_W):
        xs = pltpu.roll(x, k, 0)
        ps = pltpu.roll(prev8, k, 0)
        top = jnp.where(row8 < k, ps, xs[0:8])
        xs = jnp.concatenate([top, xs[8:r]], axis=0)
        acc = acc + xs * w[CONV_W - 1 - k:CONV_W - k, :]
    return acc


def _norm_matmul_kernel(x_ref, g_ref, w_ref, o_ref):
    x = x_ref[...]
    ms = jnp.mean(x * x, axis=-1, keepdims=True)
    h = (x * lax.rsqrt(ms + EPS)) * g_ref[...]
    o_ref[...] = _dot(h.astype(BF16), w_ref[...]).astype(o_ref.dtype)


def norm_matmul(x, g, w_bf16, tm, out_dtype=F32):
    m, d = x.shape
    n = w_bf16.shape[1]
    return pl.pallas_call(
        _norm_matmul_kernel,
        grid=(m // tm,),
        in_specs=[pl.BlockSpec((tm, d), lambda i: (i, 0)),
                  pl.BlockSpec((1, d), lambda i: (0, 0)),
                  pl.BlockSpec((d, n), lambda i: (0, 0))],
        out_specs=pl.BlockSpec((tm, n), lambda i: (i, 0)),
        out_shape=jax.ShapeDtypeStruct((m, n), out_dtype),
        compiler_params=_cparams("parallel"),
        name="norm_matmul",
    )(x, g.reshape(1, d), w_bf16)


def _small_prep_kernel(s_ref, p_ref, o_ref, x_ref):
    S = s_ref.shape[1]
    W = s_ref.shape[2]
    blk = SEQ_TILE
    lane = _iota((blk, W), 1)
    er = _iota((W, 2 * HEADS * W), 0)
    ec = _iota((W, 2 * HEADS * W), 1) // W
    expand = jnp.where(er == jnp.where(ec < HEADS, LANE_FOX + ec, LANE_GDEC + ec - HEADS), 1.0, 0.0).astype(BF16)
    is_fox = (lane >= LANE_FOX) & (lane < LANE_FOX + HEADS)
    is_beta = (lane >= LANE_BETA) & (lane < LANE_BETA + HEADS)
    is_gdec = (lane >= LANE_GDEC) & (lane < LANE_GDEC + HEADS)
    r = _iota((blk, blk), 0)
    c = _iota((blk, blk), 1)
    tril_all = jnp.where(r >= c, 1.0, 0.0).astype(BF16)
    tril_chunk = jnp.where((r >= c) & (r // GDN_CHUNK == c // GDN_CHUNK), 1.0, 0.0).astype(BF16)
    neg_exp_a = -jnp.exp(p_ref[1:2, :])
    carry = jnp.zeros((1, W), F32)
    for i in range(S // blk):
        sl = pl.ds(i * blk, blk)
        z = s_ref[0, sl, :] + p_ref[0:1, :]
        fox = jnp.where(is_fox, _log_sigmoid(z), 0.0)
        beta = jnp.where(is_beta, _sigmoid(z), 0.0)
        gdec = jnp.where(is_gdec, neg_exp_a * _softplus(z), 0.0)
        cf = _dot_exact_lhs(tril_all, fox) + carry
        cg = _dot_exact_lhs(tril_chunk, gdec)
        carry = cf[blk - 1:blk, :]
        o_ref[0, sl, :] = cf + cg + beta
        x_ref[0, sl, :] = _dot_exact_rhs(cf + cg, expand)


def small_prep(proj3, params):
    b, s, _ = proj3.shape
    w = SMALL_W
    return pl.pallas_call(
        _small_prep_kernel,
        grid=(b,),
        in_specs=[pl.BlockSpec((1, s, w), lambda i: (i, 0, COL_SMALL)),
                  pl.BlockSpec((8, w), lambda i: (0, 0))],
        out_specs=[pl.BlockSpec((1, s, w), lambda i: (i, 0, 0)),
                   pl.BlockSpec((1, s, 2 * HEADS * w), lambda i: (i, 0, 0))],
        out_shape=[jax.ShapeDtypeStruct((b, s, w), F32), jax.ShapeDtypeStruct((b, s, 2 * HEADS * w), F32)],
        compiler_params=_cparams("parallel"),
        name="small_prep",
    )(proj3, params)


def _hgrn2_kernel(q_ref, f_ref, i_ref, g_ref, lb_ref, o_ref, st_ref, qs, ks, vs, bs, os_):
    nb = q_ref.shape[0]
    T = q_ref.shape[1]
    C = HGRN_CHUNK
    seqs = range(nb)

    @pl.when(pl.program_id(1) == 0)
    def _():
        st_ref[...] = jnp.zeros_like(st_ref)

    same_head = _head_ones()
    ones_bf = jnp.where(same_head, 1.0, 0.0).astype(BF16)
    lb = lb_ref[...]
    log_lb = jnp.log(lb)
    r = _iota((T, T), 0)
    c = _iota((T, T), 1)
    tril_chunk = jnp.where((r >= c) & (r // C == c // C), 1.0, 0.0).astype(BF16)
    for bb in seqs:
        fl = f_ref[bb]
        c2 = jnp.log1p(-lb) + _log_sigmoid(fl)
        mx = jnp.maximum(log_lb, c2)
        log_f = mx + jnp.log1p(jnp.exp(-jnp.abs(log_lb - c2)))
        bs[bb] = _dot_exact_lhs(tril_chunk, log_f)
        qs[bb] = _silu(q_ref[bb])
        ks[bb] = (1.0 - lb) * _sigmoid(-fl)
        vs[bb] = i_ref[bb]

    trow = _iota((C, GROUP_WIDTH), 0)

    def chunk(ci, carry):
        r0 = pl.multiple_of(ci * C, C)
        qc = [qs[bb, pl.ds(r0, C), :] for bb in seqs]
        kc = [ks[bb, pl.ds(r0, C), :] for bb in seqs]
        vc = [vs[bb, pl.ds(r0, C), :] for bb in seqs]
        bc = [bs[bb, pl.ds(r0, C), :] for bb in seqs]
        st = [st_ref[bb] for bb in seqs]
        a = []
        for bb in seqs:
            parts = []
            for s in range(C):
                parts.append(jnp.where(trow >= s, jnp.exp(bc[bb] - bc[bb][s:s + 1, :]) * (qc[bb] * kc[bb][s:s + 1, :]), 0.0))
            a.append(jnp.concatenate(parts, axis=0).astype(BF16))
        sc = [_dot(a[bb], ones_bf) for bb in seqs]
        o = [_dot_nt((qc[bb] * jnp.exp(bc[bb])).astype(BF16), st[bb].astype(BF16)) for bb in seqs]
        b_last = [bc[bb][C - 1:C, :] for bb in seqs]
        upd = [_dot_tn(vc[bb].astype(BF16), (kc[bb] * jnp.exp(b_last[bb] - bc[bb])).astype(BF16)) for bb in seqs]
        for bb in seqs:
            acc = o[bb]
            for s in range(C):
                acc = acc + sc[bb][s * C:(s + 1) * C, :] * vc[bb][s:s + 1, :]
            os_[bb, pl.ds(r0, C), :] = acc
            st_ref[bb] = st[bb] * jnp.exp(b_last[bb]) + jnp.where(same_head, upd[bb], 0.0)
        return carry

    lax.fori_loop(0, T // C, chunk, 0)
    for bb in seqs:
        o_ref[bb] = _head_rms(os_[bb], ones_bf) * _silu(g_ref[bb])


def hgrn2(proj3, lb):
    b, s, _ = proj3.shape
    t = SEQ_TILE
    gw = GROUP_WIDTH
    nb = HGRN_BATCH
    col = lambda cidx: pl.BlockSpec((nb, t, gw), lambda i, j: (i, j, cidx))
    return pl.pallas_call(
        _hgrn2_kernel,
        grid=(b // nb, s // t),
        in_specs=[col(COL_AQ), col(COL_AF), col(COL_AI), col(COL_AG),
                  pl.BlockSpec((1, gw), lambda i, j: (0, 0))],
        out_specs=pl.BlockSpec((nb, t, gw), lambda i, j: (i, j, 0)),
        out_shape=jax.ShapeDtypeStruct((b, s, gw), F32),
        scratch_shapes=[pltpu.VMEM((nb, gw, gw), F32)] + [pltpu.VMEM((nb, t, gw), F32)] * 5,
        compiler_params=_cparams("parallel", "arbitrary"),
        name="hgrn2",
    )(proj3, proj3, proj3, proj3, lb.reshape(1, gw))


def _fox_kernel(q_ref, k_ref, v_ref, cr_ref, cx_ref, o_ref, kb_ref, vt_ref, acc_ref):
    nb = q_ref.shape[0]
    tq = q_ref.shape[1]
    tk = tq
    S = k_ref.shape[1]
    n = HEADS * tq
    qi = pl.program_id(1)
    seqs = range(nb)

    @pl.when(qi == 0)
    def _():
        for bb in seqs:
            for i in range(S // tk):
                sl = pl.ds(i * tk, tk)
                kb_ref[bb, sl, :] = k_ref[bb, sl, :].astype(BF16)
                vt_ref[bb, :, sl] = v_ref[bb, sl, :].T.astype(BF16)

    q0 = pl.multiple_of(qi * tq, tq)
    qs, cq_row = [], []
    for bb in seqs:
        qs.append(_stack_heads(q_ref[bb] * (HEAD_DIM ** -0.5)).astype(BF16))
        cr = cr_ref[bb, :, pl.ds(q0, tq)]
        cq_row.append(jnp.concatenate([cr[h:h + 1, :] for h in range(HEADS)], axis=1))
    acc_ref[...] = jnp.zeros_like(acc_ref)
    reps = tq // SMALL_W

    def step(kb, stats, masked):
        k0 = pl.multiple_of(kb * tk, tk)
        st = [_dot_nt(kb_ref[bb, pl.ds(k0, tk), :], qs[bb]) for bb in seqs]
        for bb in seqs:
            cx = cx_ref[bb, pl.ds(k0, tk), :]
            ck = jnp.concatenate([cx[:, h * SMALL_W:(h + 1) * SMALL_W] for h in range(HEADS) for _ in range(reps)], axis=1)
            st[bb] = st[bb] + (cq_row[bb] - ck)
            if masked:
                st[bb] = jnp.where(_iota((tk, n), 0) <= _iota((tk, n), 1) % tq, st[bb], -jnp.inf)
        m_new = [jnp.maximum(stats[bb][0], jnp.max(st[bb], axis=0, keepdims=True)) for bb in seqs]
        alpha = [jnp.exp(stats[bb][0] - m_new[bb]) for bb in seqs]
        p = [jnp.exp(st[bb] - m_new[bb]) for bb in seqs]
        l_new = [alpha[bb] * stats[bb][1] + jnp.sum(p[bb], axis=0, keepdims=True) for bb in seqs]
        pv = [_dot(vt_ref[bb, :, pl.ds(k0, tk)], p[bb].astype(BF16)) for bb in seqs]
        for bb in seqs:
            acc_ref[bb] = alpha[bb] * acc_ref[bb] + pv[bb]
        return tuple((m_new[bb], l_new[bb]) for bb in seqs)

    init = tuple((jnp.full((1, n), -jnp.inf, F32), jnp.zeros((1, n), F32)) for _ in seqs)
    stats = lax.fori_loop(0, qi, lambda kb, c: step(kb, c, False), init)
    stats = step(qi, stats, True)

    ones_bf = jnp.where(_head_ones(), 1.0, 0.0).astype(BF16)
    for bb in seqs:
        acc = acc_ref[bb] * (1.0 / stats[bb][1])
        ot = jnp.concatenate([acc[h * HEAD_DIM:(h + 1) * HEAD_DIM, h * tq:(h + 1) * tq] for h in range(HEADS)], axis=0)
        o_ref[bb] = _head_rms(ot.T, ones_bf)


def fox_attention(proj3, c_row, c_exp):
    b, s, _ = proj3.shape
    tq = SEQ_TILE
    gw = GROUP_WIDTH
    nb = FOX_BATCH
    return pl.pallas_call(
        _fox_kernel,
        grid=(b // nb, s // tq),
        in_specs=[pl.BlockSpec((nb, tq, gw), lambda i, j: (i, j, COL_BQ)),
                  pl.BlockSpec((nb, s, gw), lambda i, j: (i, 0, COL_BK)),
                  pl.BlockSpec((nb, s, gw), lambda i, j: (i, 0, COL_BV)),
                  pl.BlockSpec((nb, 8, s), lambda i, j: (i, 0, 0)),
                  pl.BlockSpec((nb, s, HEADS * SMALL_W), lambda i, j: (i, 0, 0))],
        out_specs=pl.BlockSpec((nb, tq, gw), lambda i, j: (i, j, 0)),
        out_shape=jax.ShapeDtypeStruct((b, s, gw), F32),
        scratch_shapes=[pltpu.VMEM((nb, s, gw), BF16),
                        pltpu.VMEM((nb, gw, s), BF16),
                        pltpu.VMEM((nb, gw, HEADS * tq), F32)],
        compiler_params=_cparams("parallel", "arbitrary"),
        name="fox_attention",
    )(proj3, proj3, proj3, c_row, c_exp)


def _rglru_kernel(x_ref, g_ref, cw_ref, cb_ref, wa_ref, ba_ref, wx_ref, bx_ref, lam_ref, o_ref, prev_ref, h_ref):
    @pl.when(pl.program_id(1) == 0)
    def _():
        prev_ref[...] = jnp.zeros_like(prev_ref)
        h_ref[...] = jnp.zeros_like(h_ref)

    x = x_ref[0]
    t = x.shape[0]
    xc = _causal_conv4(x, prev_ref[...], cw_ref[...]) + cb_ref[...]
    prev_ref[...] = x[t - 8:t, :]
    xb = xc.astype(BF16)
    r = _sigmoid(_dot(xb, wa_ref[...]) + ba_ref[...])
    ig = _sigmoid(_dot(xb, wx_ref[...]) + bx_ref[...])
    log_a = (-LRU_C * r) * _softplus(-lam_ref[...])
    a = jnp.exp(log_a)
    u = jnp.sqrt(1.0 - jnp.exp(2.0 * log_a)) * (ig * xc)
    row = _iota(a.shape, 0)
    d = 1
    while d < t:
        valid = row >= d
        u = jnp.where(valid, a * pltpu.roll(u, d, 0) + u, u)
        a = jnp.where(valid, a * pltpu.roll(a, d, 0), a)
        d *= 2
    h = a * h_ref[...] + u
    h_ref[...] = h[t - 1:t, :]
    ones_bf = jnp.where(_head_ones(), 1.0, 0.0).astype(BF16)
    o_ref[0] = _head_rms(h * _gelu_tanh(g_ref[0]), ones_bf)


def rglru(proj3, conv_w, conv_b, wa_bd, ba, wx_bd, bx, lam):
    b, s, _ = proj3.shape
    t = SEQ_TILE
    gw = GROUP_WIDTH
    row = lambda: pl.BlockSpec((1, gw), lambda i, j: (0, 0))
    return pl.pallas_call(
        _rglru_kernel,
        grid=(b, s // t),
        in_specs=[pl.BlockSpec((1, t, gw), lambda i, j: (i, j, COL_CX)),
                  pl.BlockSpec((1, t, gw), lambda i, j: (i, j, COL_CG)),
                  pl.BlockSpec((CONV_W, gw), lambda i, j: (0, 0)), row(),
                  pl.BlockSpec((gw, gw), lambda i, j: (0, 0)), row(),
                  pl.BlockSpec((gw, gw), lambda i, j: (0, 0)), row(), row()],
        out_specs=pl.BlockSpec((1, t, gw), lambda i, j: (i, j, 0)),
        out_shape=jax.ShapeDtypeStruct((b, s, gw), F32),
        scratch_shapes=[pltpu.VMEM((8, gw), F32), pltpu.VMEM((1, gw), F32)],
        compiler_params=_cparams("parallel", "arbitrary"),
        name="rglru",
    )(proj3, proj3, conv_w, conv_b.reshape(1, gw), wa_bd, ba.reshape(1, gw), wx_bd, bx.reshape(1, gw), lam.reshape(1, gw))


def _gdn_kernel(q_ref, k_ref, v_ref, z_ref, sm_ref, ge_ref, gr_ref, cw_ref, o_ref, st_ref, pq_ref, pk_ref, pv_ref):
    nb = q_ref.shape[0]
    T = q_ref.shape[1]
    C = GDN_CHUNK

    @pl.when(pl.program_id(1) == 0)
    def _():
        st_ref[...] = jnp.zeros_like(st_ref)
        pq_ref[...] = jnp.zeros_like(pq_ref)
        pk_ref[...] = jnp.zeros_like(pk_ref)
        pv_ref[...] = jnp.zeros_like(pv_ref)

    same_head = _head_ones()
    ones_bf = jnp.where(same_head, 1.0, 0.0).astype(BF16)
    cw = cw_ref[...]
    gw = GROUP_WIDTH
    er = _iota((SMALL_W, gw), 0)
    ec = _iota((SMALL_W, gw), 1) // HEAD_DIM
    exp_beta = jnp.where(er == LANE_BETA + ec, 1.0, 0.0).astype(BF16)
    exp_g = jnp.where(er == LANE_GDEC + ec, 1.0, 0.0).astype(BF16)
    n = HEADS * C
    rr = _iota((n, n), 0)
    cc = _iota((n, n), 1)
    incl = same_head & (rr % C >= cc % C)
    strict = same_head & (rr % C > cc % C)

    prep = []
    for bb in range(nb):
        xq, xk, xv = q_ref[bb], k_ref[bb], v_ref[bb]
        q = _silu(_causal_conv4(xq, pq_ref[bb], cw[:, 0:gw]))
        k = _silu(_causal_conv4(xk, pk_ref[bb], cw[:, gw:2 * gw]))
        v = _silu(_causal_conv4(xv, pv_ref[bb], cw[:, 2 * gw:3 * gw]))
        pq_ref[bb] = xq[T - 8:T, :]
        pk_ref[bb] = xk[T - 8:T, :]
        pv_ref[bb] = xv[T - 8:T, :]
        q = q * lax.rsqrt(_dot_exact_rhs(q * q, ones_bf) + EPS) * (HEAD_DIM ** -0.5)
        k = k * lax.rsqrt(_dot_exact_rhs(k * k, ones_bf) + EPS)
        sm = sm_ref[bb]
        betax = _dot_exact_rhs(sm, exp_beta)
        gx = _dot_exact_rhs(sm, exp_g)
        prep.append((q, k, v, betax, gx))

    reps = gw // SMALL_W
    n_chunks = T // C
    eye = jnp.where(rr == cc, 1.0, 0.0)

    chains = []
    for ci in range(n_chunks):
        sl = slice(ci * C, (ci + 1) * C)
        for bb in range(nb):
            q, k, v, betax, gx = prep[bb]
            qc, kc, vc, bx, gc = q[sl], k[sl], v[sl], betax[sl], gx[sl]
            grow = gr_ref[bb, ci, 0:1, :]
            ge = ge_ref[bb, sl, :]
            gcol = jnp.concatenate(
                [jnp.concatenate([ge[:, h * SMALL_W:(h + 1) * SMALL_W]] * reps, axis=1) for h in range(HEADS)], axis=0)
            eg = jnp.exp(gc)
            kb = kc * bx
            g_last = gc[C - 1:C, :]
            gamma = jnp.where(incl, jnp.exp(jnp.where(incl, gcol - grow, 0.0)), 0.0)
            ks_bf = _stack_heads(kc).astype(BF16)
            m = jnp.where(strict, _dot_nt(_stack_heads(kb).astype(BF16), ks_bf) * gamma, 0.0)
            chains.append(dict(
                x=eye - m, y=m,
                a_qk=(_dot_nt(_stack_heads(qc).astype(BF16), ks_bf) * gamma).astype(BF16),
                rhs_u=_stack_heads(vc * bx).astype(BF16),
                rhs_w=_stack_heads(kb * eg).astype(BF16),
                q_dec=_stack_heads(qc * eg).astype(BF16),
                k_dec=_stack_heads(kc * jnp.exp(g_last - gc)).astype(BF16),
                decay=jnp.exp(g_last)))
    for _ in range(5):
        for ch in chains:
            yb = ch["y"].astype(BF16)
            ch["y"] = _dot(yb, yb)
        for ch in chains:
            ch["x"] = ch["x"] + _dot(ch["x"].astype(BF16), ch["y"].astype(BF16))
    for ch in chains:
        t_inv = ch["x"].astype(BF16)
        ch["u"] = _dot(t_inv, ch["rhs_u"])
        ch["wk"] = _dot(t_inv, ch["rhs_w"]).astype(BF16)

    outs = [[] for _ in range(nb)]
    for ci in range(n_chunks):
        for bb in range(nb):
            ch = chains[ci * nb + bb]
            st = st_ref[bb]
            st_bf = st.astype(BF16)
            v_new = (ch["u"] - _dot(ch["wk"], st_bf)).astype(BF16)
            o = _dot(ch["q_dec"], st_bf) + _dot(ch["a_qk"], v_new)
            outs[bb].append(_unstack_heads(o, C))
            st_ref[bb] = st * ch["decay"] + _dot_tn(ch["k_dec"], v_new)
    for bb in range(nb):
        o = jnp.concatenate(outs[bb], axis=0)
        o_ref[bb] = _head_rms(o, ones_bf) * _silu(z_ref[bb])


def gdn(proj3, sm, c_exp, g_row, conv_w):
    b, s, _ = proj3.shape
    t = SEQ_TILE
    gw = GROUP_WIDTH
    nb = GDN_BATCH
    col = lambda cidx: pl.BlockSpec((nb, t, gw), lambda i, j: (i, j, cidx))
    return pl.pallas_call(
        _gdn_kernel,
        grid=(b // nb, s // t),
        in_specs=[col(COL_DQ), col(COL_DK), col(COL_DV), col(COL_DZ),
                  pl.BlockSpec((nb, t, SMALL_W), lambda i, j: (i, j, 0)),
                  pl.BlockSpec((nb, t, HEADS * SMALL_W), lambda i, j: (i, j, 1)),
                  pl.BlockSpec((nb, t // GDN_CHUNK, 8, gw), lambda i, j: (i, j, 0, 0)),
                  pl.BlockSpec((CONV_W, 3 * gw), lambda i, j: (0, 0))],
        out_specs=pl.BlockSpec((nb, t, gw), lambda i, j: (i, j, 0)),
        out_shape=jax.ShapeDtypeStruct((b, s, gw), F32),
        scratch_shapes=[pltpu.VMEM((nb, gw, gw), F32)] + [pltpu.VMEM((nb, 8, gw), F32)] * 3,
        compiler_params=_cparams("parallel", "arbitrary"),
        name="gdn",
    )(proj3, proj3, proj3, proj3, sm, c_exp, g_row, conv_w)


def _mix_out_kernel(x_ref, ya_ref, yb_ref, yc_ref, yd_ref, gain_ref, w_ref, o_ref):
    gw = GROUP_WIDTH
    gain = gain_ref[...]
    acc = x_ref[0]
    for i, y_ref in enumerate((ya_ref, yb_ref, yc_ref, yd_ref)):
        yg = (y_ref[0] * gain[:, i * gw:(i + 1) * gw]).astype(BF16)
        acc = acc + _dot(yg, w_ref[i * gw:(i + 1) * gw, :])
    o_ref[0] = acc


def mix_out(x3, ya, yb, yc, yd, gain, w_out_bf16):
    b, s, d = x3.shape
    t = SEQ_TILE
    gw = GROUP_WIDTH
    grp = lambda: pl.BlockSpec((1, t, gw), lambda i, j: (i, j, 0))
    return pl.pallas_call(
        _mix_out_kernel,
        grid=(b, s // t),
        in_specs=[pl.BlockSpec((1, t, d), lambda i, j: (i, j, 0)),
                  grp(), grp(), grp(), grp(),
                  pl.BlockSpec((1, 4 * gw), lambda i, j: (0, 0)),
                  pl.BlockSpec((4 * gw, d), lambda i, j: (0, 0))],
        out_specs=pl.BlockSpec((1, t, d), lambda i, j: (i, j, 0)),
        out_shape=jax.ShapeDtypeStruct((b, s, d), F32),
        compiler_params=_cparams("parallel", "parallel"),
        name="mix_out",
    )(x3, ya, yb, yc, yd, gain.reshape(1, 4 * gw), w_out_bf16)


def _mem_attn_kernel(x_ref, g_ref, wq_ref, kv_ref, wo_ref, o_ref):
    x = x_ref[0]
    t = x.shape[0]
    gw = GROUP_WIDTH
    ms = jnp.mean(x * x, axis=-1, keepdims=True)
    h = ((x * lax.rsqrt(ms + EPS)) * g_ref[...]).astype(BF16)
    q = _dot(h, wq_ref[...])
    kv = kv_ref[0]
    k = kv[:, 0:gw].astype(BF16)
    v = kv[:, gw:2 * gw].astype(BF16)
    s = _dot_nt(k, _stack_heads(q).astype(BF16)) * (HEAD_DIM ** -0.5)
    s = s - jnp.max(s, axis=0, keepdims=True)
    p = jnp.exp(s)
    p = p * (1.0 / jnp.sum(p, axis=0, keepdims=True))
    ot = _dot_tn(v, p.astype(BF16))
    oc = jnp.concatenate([ot[hh * HEAD_DIM:(hh + 1) * HEAD_DIM, hh * t:(hh + 1) * t] for hh in range(HEADS)], axis=0).T
    o_ref[0] = x + _dot(oc.astype(BF16), wo_ref[...])


def mem_attention(x3, g, wq_bf16, kv3, wo_bf16):
    b, s, d = x3.shape
    t = SEQ_TILE
    m = kv3.shape[1]
    return pl.pallas_call(
        _mem_attn_kernel,
        grid=(b, s // t),
        in_specs=[pl.BlockSpec((1, t, d), lambda i, j: (i, j, 0)),
                  pl.BlockSpec((1, d), lambda i, j: (0, 0)),
                  pl.BlockSpec((d, GROUP_WIDTH), lambda i, j: (0, 0)),
                  pl.BlockSpec((1, m, 2 * GROUP_WIDTH), lambda i, j: (i, 0, 0)),
                  pl.BlockSpec((GROUP_WIDTH, d), lambda i, j: (0, 0))],
        out_specs=pl.BlockSpec((1, t, d), lambda i, j: (i, j, 0)),
        out_shape=jax.ShapeDtypeStruct((b, s, d), F32),
        compiler_params=_cparams("parallel", "parallel"),
        name="mem_attention",
    )(x3, g.reshape(1, d), wq_bf16, kv3, wo_bf16)


def _router_kernel(x_ref, g_ref, w_ref, b_ref, h_ref, r_ref, n_ref, cnt_ref):
    @pl.when(pl.program_id(0) == 0)
    def _():
        cnt_ref[...] = jnp.zeros_like(cnt_ref)

    x = x_ref[...]
    tm = x.shape[0]
    ms = jnp.mean(x * x, axis=-1, keepdims=True)
    h = (x * lax.rsqrt(ms + EPS)) * g_ref[...]
    h_ref[...] = _pack_bf16_pairs(h)
    hh, hm, _ = _split3(h)
    w = w_ref[...]
    wh, wm, _ = _split3(w)
    logits = _dot(hh, wh) + _dot(hh, wm) + _dot(hm, wh) + b_ref[...]
    lane = _iota(logits.shape, 1)
    big = jnp.int32(1 << 30)
    neg = -jnp.inf
    is_grp = (lane >= N_EXPERTS) & (lane < N_EXPERTS + N_EXPERT_GROUPS)
    gl = jnp.where(is_grp, logits, neg)
    gmax = jnp.max(gl, axis=-1, keepdims=True)
    p_grp = 1.0 / jnp.sum(jnp.exp(gl - gmax), axis=-1, keepdims=True)
    g_sel = jnp.min(jnp.where(gl == gmax, lane, big), axis=-1, keepdims=True) - N_EXPERTS
    in_grp = (lane < N_EXPERTS) & (lane // EXPERTS_PER_GROUP == g_sel)
    el = jnp.where(in_grp, logits, neg)
    m1 = jnp.max(el, axis=-1, keepdims=True)
    i1 = jnp.min(jnp.where(el == m1, lane, big), axis=-1, keepdims=True)
    el2 = jnp.where(lane == i1, neg, el)
    m2 = jnp.max(el2, axis=-1, keepdims=True)
    i2 = jnp.min(jnp.where(el2 == m2, lane, big), axis=-1, keepdims=True)
    e21 = jnp.exp(m2 - m1)
    w1 = 1.0 / (1.0 + e21)
    w2 = e21 / (1.0 + e21)
    oh1 = lane == i1
    oh2 = lane == i2
    onehot = jnp.where(oh1 | oh2, 1.0, 0.0)
    strict = jnp.where(_iota((tm, tm), 0) > _iota((tm, tm), 1), 1.0, 0.0).astype(BF16)
    before = _dot(strict, onehot.astype(BF16)) + cnt_ref[...]
    rank1 = jnp.sum(jnp.where(oh1, before, 0.0), axis=-1, keepdims=True)
    rank2 = jnp.sum(jnp.where(oh2, before, 0.0), axis=-1, keepdims=True)
    cnt = before[tm - 1:tm, :] + onehot[tm - 1:tm, :]
    cnt_ref[...] = cnt
    n_ref[...] = jnp.broadcast_to(cnt, n_ref.shape)
    cols = (i1.astype(F32), i2.astype(F32), p_grp * w1, p_grp * w2, rank1, rank2)
    route = jnp.zeros(logits.shape, F32)
    for j, col in enumerate(cols):
        route = jnp.where(lane == j, col, route)
    r_ref[...] = route


def router(x2, g, w_router, b_router, tm):
    m, d = x2.shape
    return pl.pallas_call(
        _router_kernel,
        grid=(m // tm,),
        in_specs=[pl.BlockSpec((tm, d), lambda i: (i, 0)),
                  pl.BlockSpec((1, d), lambda i: (0, 0)),
                  pl.BlockSpec((d, SMALL_W), lambda i: (0, 0)),
                  pl.BlockSpec((1, SMALL_W), lambda i: (0, 0))],
        out_specs=[pl.BlockSpec((tm, d // 2), lambda i: (i, 0)),
                   pl.BlockSpec((tm, SMALL_W), lambda i: (i, 0)),
                   pl.BlockSpec((8, SMALL_W), lambda i: (0, 0))],
        out_shape=[jax.ShapeDtypeStruct((m, d // 2), jnp.uint32), jax.ShapeDtypeStruct((m, SMALL_W), F32),
                   jax.ShapeDtypeStruct((8, SMALL_W), F32)],
        scratch_shapes=[pltpu.VMEM((1, SMALL_W), F32)],
        compiler_params=_cparams("arbitrary"),
        name="router",
    )(x2, g.reshape(1, d), w_router, b_router)


def _row_copy(src_ref, src_row, dst_ref, dst_row, sem):
    return pltpu.make_async_copy(src_ref.at[pl.ds(src_row, 1)], dst_ref.at[pl.ds(dst_row, 1)], sem)


def _dispatch_kernel(dest_ref, zrow_ref, h_ref, xs_ref, zbuf, sem):
    tm = h_ref.shape[0]
    base = pl.program_id(0) * tm

    @pl.when(pl.program_id(0) == 0)
    def _():
        zbuf[...] = jnp.zeros_like(zbuf)

        def zero_copy(e):
            row = pl.multiple_of(zrow_ref[e], EXPERT_TILE)
            return pltpu.make_async_copy(zbuf, xs_ref.at[pl.ds(row, EXPERT_TILE)], sem)

        for e in range(zrow_ref.shape[0]):
            @pl.when(zrow_ref[e] >= 0)
            def _():
                zero_copy(e).start()
        for e in range(zrow_ref.shape[0]):
            @pl.when(zrow_ref[e] >= 0)
            def _():
                zero_copy(e).wait()

    def start(t, c):
        for slot in range(2):
            _row_copy(h_ref, t, xs_ref, dest_ref[2 * (base + t) + slot], sem).start(priority=slot)
        return c

    def wait(t, c):
        for slot in range(2):
            _row_copy(h_ref, t, xs_ref, dest_ref[2 * (base + t) + slot], sem).wait()
        return c

    lax.fori_loop(0, tm, start, 0, unroll=8)
    lax.fori_loop(0, tm, wait, 0, unroll=8)


def moe_dispatch(h, dest, zrow, n_rows, tm):
    m, d = h.shape
    return pl.pallas_call(
        _dispatch_kernel,
        grid_spec=pltpu.PrefetchScalarGridSpec(
            num_scalar_prefetch=2,
            grid=(m // tm,),
            in_specs=[pl.BlockSpec((tm, d), lambda i, dest, zrow: (i, 0))],
            out_specs=pl.BlockSpec(memory_space=pl.ANY),
            scratch_shapes=[pltpu.VMEM((EXPERT_TILE, d), h.dtype), pltpu.SemaphoreType.DMA]),
        out_shape=jax.ShapeDtypeStruct((n_rows, d), h.dtype),
        compiler_params=_cparams("arbitrary"),
        name="moe_dispatch",
    )(dest, zrow, h)


def _experts_kernel(te_ref, nu_ref, xs_ref, wgu_ref, wdn_ref, y_ref, wgu_bf, wdn_bf):
    i = pl.program_id(0)

    @pl.when(i >= nu_ref[0])
    def _():
        y_ref[...] = jnp.zeros_like(y_ref)

    @pl.when(i < nu_ref[0])
    def _():
        prev = te_ref[jnp.maximum(i - 1, 0)]

        @pl.when((i == 0) | (te_ref[i] != prev))
        def _():
            wgu_bf[...] = wgu_ref[0, 0].astype(BF16)
            wdn_bf[...] = wdn_ref[0, 0].astype(BF16)

        gu = _dot(_unpack_bf16_pairs(xs_ref[...]).astype(BF16), wgu_bf[...])
        act = _silu(gu[:, 0:D_EXPERT]) * gu[:, D_EXPERT:2 * D_EXPERT]
        y_ref[...] = _pack_bf16_pairs(_dot(act.astype(BF16), wdn_bf[...]))


def moe_experts(xs, tile_expert, n_used, w_gu, w_dn, layer, tm):
    n_rows, dp = xs.shape
    d = 2 * dp
    tile = lambda i, te, nu: (jnp.minimum(i, nu[0] - 1), 0)
    wsel = lambda i, te, nu: (layer, te[jnp.minimum(i, nu[0] - 1)], 0, 0)
    return pl.pallas_call(
        _experts_kernel,
        grid_spec=pltpu.PrefetchScalarGridSpec(
            num_scalar_prefetch=2,
            grid=(n_rows // tm,),
            in_specs=[pl.BlockSpec((tm, dp), tile),
                      pl.BlockSpec((1, 1, d, 2 * D_EXPERT), wsel),
                      pl.BlockSpec((1, 1, D_EXPERT, d), wsel)],
            out_specs=pl.BlockSpec((tm, dp), lambda i, te, nu: (i, 0)),
            scratch_shapes=[pltpu.VMEM((d, 2 * D_EXPERT), BF16), pltpu.VMEM((D_EXPERT, d), BF16)]),
        out_shape=jax.ShapeDtypeStruct((n_rows, dp), jnp.uint32),
        compiler_params=_cparams("arbitrary"),
        name="moe_experts",
    )(tile_expert, n_used, xs, w_gu, w_dn)


def _combine_kernel(dest_ref, x_ref, r_ref, g_ref, ys_ref, o_ref, ybuf, sem, *, final_norm):
    tm = x_ref.shape[0]
    base = pl.program_id(0) * tm

    def start(t, c):
        for slot in range(2):
            _row_copy(ys_ref, dest_ref[2 * (base + t) + slot], ybuf.at[slot], t, sem).start(priority=slot)
        return c

    def wait(t, c):
        for slot in range(2):
            _row_copy(ys_ref, dest_ref[2 * (base + t) + slot], ybuf.at[slot], t, sem).wait()
        return c

    lax.fori_loop(0, tm, start, 0, unroll=8)
    r = r_ref[...]
    g1 = r[:, 2:3]
    g2 = r[:, 3:4]
    lax.fori_loop(0, tm, wait, 0, unroll=8)
    out = x_ref[...] + g1 * _unpack_bf16_pairs(ybuf[0]) + g2 * _unpack_bf16_pairs(ybuf[1])
    if final_norm:
        ms = jnp.mean(out * out, axis=-1, keepdims=True)
        out = (out * lax.rsqrt(ms + EPS)) * g_ref[...]
    o_ref[...] = out


def moe_combine(x2, route, dest, ys, g_final, tm, final_norm):
    m, d = x2.shape
    return pl.pallas_call(
        functools.partial(_combine_kernel, final_norm=final_norm),
        grid_spec=pltpu.PrefetchScalarGridSpec(
            num_scalar_prefetch=1,
            grid=(m // tm,),
            in_specs=[pl.BlockSpec((tm, d), lambda i, dest: (i, 0)),
                      pl.BlockSpec((tm, SMALL_W), lambda i, dest: (i, 0)),
                      pl.BlockSpec((1, d), lambda i, dest: (0, 0)),
                      pl.BlockSpec(memory_space=pl.ANY)],
            out_specs=pl.BlockSpec((tm, d), lambda i, dest: (i, 0)),
            scratch_shapes=[pltpu.VMEM((2, tm, d // 2), jnp.uint32), pltpu.SemaphoreType.DMA]),
        out_shape=jax.ShapeDtypeStruct((m, d), F32),
        compiler_params=_cparams("arbitrary"),
        name="moe_combine",
    )(dest, x2, route, g_final.reshape(1, d), ys)


def hier_moe_routed(x2, g_ffn, w_router, b_router, w_gu, w_dn, layer, g_final, final_norm):
    m, d = x2.shape
    te_rows = EXPERT_TILE
    n_rows = 2 * m + N_EXPERTS * te_rows
    h, route, counts = router(x2, g_ffn, w_router, b_router, tm=512)
    cnt = counts[0, 0:N_EXPERTS].astype(jnp.int32)
    padded = (cnt + te_rows - 1) // te_rows * te_rows
    seg_end = jnp.cumsum(padded)
    seg_start = seg_end - padded
    idx = route[:, 0:2].astype(jnp.int32)
    onehot = idx[:, :, None] == jnp.arange(N_EXPERTS, dtype=jnp.int32)
    dest = (jnp.sum(jnp.where(onehot, seg_start, 0), axis=-1) + route[:, 4:6].astype(jnp.int32)).reshape(2 * m)
    tile_start = jnp.arange(n_rows // te_rows, dtype=jnp.int32) * te_rows
    tile_expert = jnp.minimum(jnp.sum((tile_start[:, None] >= seg_end[None, :]).astype(jnp.int32), axis=1), N_EXPERTS - 1)
    n_used = (seg_end[N_EXPERTS - 1] // te_rows).reshape(1)
    tail = n_used[0] + jnp.arange(N_EXPERTS, dtype=jnp.int32)
    zrow = jnp.concatenate([jnp.where(padded > 0, seg_end - te_rows, -1),
                            jnp.where(tail < n_rows // te_rows, tail * te_rows, -1)])
    xs = moe_dispatch(h, dest, zrow, n_rows, tm=512)
    ys = moe_experts(xs, tile_expert, n_used, w_gu, w_dn, layer, tm=te_rows)
    return moe_combine(x2, route, dest, ys, g_final, tm=256, final_norm=final_norm)


def _w_in_prep_kernel(w_ref, o_ref):
    gw = GROUP_WIDTH
    a_end = 7 * gw
    c_start = a_end + HEADS
    c_end = c_start + 5 * gw
    z_start = c_end + 2 * HEADS
    assert a_end % SMALL_W == LANE_FOX and c_end % SMALL_W == LANE_BETA and LANE_GDEC == LANE_BETA + HEADS
    w = w_ref[...]
    o_ref[:, 0:a_end] = w[:, 0:a_end].astype(BF16)
    o_ref[:, a_end:a_end + 5 * gw] = w[:, c_start:c_end].astype(BF16)
    o_ref[:, 12 * gw:13 * gw] = w[:, z_start:z_start + gw].astype(BF16)
    blk_fox = w[:, a_end:a_end + SMALL_W]
    blk_gdn = w[:, c_end - LANE_BETA:c_end - LANE_BETA + SMALL_W]
    lane = _iota(blk_fox.shape, 1)
    small = jnp.where(lane < LANE_BETA, blk_fox, jnp.where(lane < LANE_GDEC + HEADS, blk_gdn, 0.0))
    o_ref[:, 13 * gw:13 * gw + SMALL_W] = small.astype(BF16)
    o_ref[:, 13 * gw + SMALL_W:] = jnp.zeros((w.shape[0], SMALL_W), BF16)


def w_in_prep(w_all, layer):
    _, d, n = w_all.shape
    tr = 128
    return pl.pallas_call(
        _w_in_prep_kernel,
        grid=(d // tr,),
        in_specs=[pl.BlockSpec((None, tr, n), lambda i: (layer, i, 0))],
        out_specs=pl.BlockSpec((tr, N_PROJ), lambda i: (i, 0)),
        out_shape=jax.ShapeDtypeStruct((d, N_PROJ), BF16),
        compiler_params=_cparams("parallel"),
        name="w_in_prep",
    )(w_all)


def _block_diag(w):
    h = w.shape[0]
    eye = jnp.eye(h, dtype=w.dtype)
    return (eye[:, None, :, None] * w[:, :, None, :]).reshape(h * HEAD_DIM, h * HEAD_DIM)


def kernel(x, mem, norm_mix, w_in, hgrn_lb, fox_fb, lru_conv_w, lru_conv_b, lru_wa, lru_ba, lru_wx, lru_bx, lru_lam, gdn_conv_w, gdn_a_log, gdn_dt_bias, mix_gain, w_out, norm_mem, norm_memkv, w_mq, w_mkv, w_mo, norm_ffn, w_rg, b_rg, w_re, b_re, w_e_gu, w_e_dn, norm_final):
    b, s, d = x.shape
    depth = w_in.shape[0]
    t_tok = b * s
    mlen = mem.shape[1]
    gw = GROUP_WIDTH

    lb_all = jnp.cumsum(jax.nn.softmax(hgrn_lb.astype(F32), axis=0), axis=0)
    lb_all = lb_all - lb_all[0]

    x = x.astype(F32)
    for l in range(depth):
        proj = norm_matmul(x.reshape(t_tok, d), norm_mix[l], w_in_prep(w_in, l), tm=256)
        proj3 = proj.reshape(b, s, N_PROJ)
        prm = jnp.zeros((8, SMALL_W), F32)
        prm = prm.at[0, LANE_FOX:LANE_FOX + HEADS].set(fox_fb[l].astype(F32))
        prm = prm.at[0, LANE_GDEC:LANE_GDEC + HEADS].set(gdn_dt_bias[l].astype(F32))
        prm = prm.at[1, LANE_GDEC:LANE_GDEC + HEADS].set(gdn_a_log[l].astype(F32))
        sm, c_exp = small_prep(proj3, prm)
        c_row = jnp.pad(jnp.swapaxes(sm[:, :, LANE_FOX:LANE_FOX + HEADS], 1, 2), ((0, 0), (0, 8 - HEADS), (0, 0)))
        g_row = sm[:, :, LANE_GDEC:LANE_GDEC + HEADS].reshape(b, s // GDN_CHUNK, GDN_CHUNK, HEADS)
        g_row = jnp.swapaxes(g_row, 2, 3).reshape(b, s // GDN_CHUNK, 1, gw)
        g_row = jnp.broadcast_to(g_row, (b, s // GDN_CHUNK, 8, gw))

        ya = hgrn2(proj3, lb_all[l])
        yb = fox_attention(proj3, c_row, c_exp)
        yc = rglru(proj3, lru_conv_w[l], lru_conv_b[l], _block_diag(lru_wa[l]).astype(BF16), lru_ba[l],
                   _block_diag(lru_wx[l]).astype(BF16), lru_bx[l], lru_lam[l])
        yd = gdn(proj3, sm, c_exp, g_row, gdn_conv_w[l])
        x = mix_out(x, ya, yb, yc, yd, mix_gain[l], w_out[l].astype(BF16))

        kv = norm_matmul(mem.reshape(b * mlen, d), norm_memkv[l], w_mkv[l].astype(BF16), tm=256)
        x = mem_attention(x, norm_mem[l], w_mq[l].astype(BF16), kv.reshape(b, mlen, 2 * gw), w_mo[l].astype(BF16))

        w_router = jnp.concatenate([w_re[l], w_rg[l], jnp.zeros((d, SMALL_W - N_EXPERTS - N_EXPERT_GROUPS), F32)], axis=1)
        b_router = jnp.concatenate([b_re[l], b_rg[l], jnp.zeros((SMALL_W - N_EXPERTS - N_EXPERT_GROUPS,), F32)]).reshape(1, SMALL_W)
        x = hier_moe_routed(x.reshape(t_tok, d), norm_ffn[l], w_router, b_router, w_e_gu, w_e_dn, l,
                            norm_final, final_norm=(l == depth - 1)).reshape(b, s, d)

    return x
```

```python
import functools
import math

import jax
import jax.numpy as jnp
from jax import lax
from jax.experimental import pallas as pl
from jax.experimental.pallas import tpu as pltpu
from jax.experimental.pallas import tpu_sc as plsc

F32 = jnp.float32
BF16 = jnp.bfloat16

HEAD_DIM = 64
GROUP_WIDTH = 256
HEADS = GROUP_WIDTH // HEAD_DIM
GDN_CHUNK = 64
GDN_BATCH = 2
FOX_BATCH = 2
HGRN_BATCH = 2
HGRN_CHUNK = 16
CONV_W = 4
LRU_C = 8.0
EPS = 1e-6
N_EXPERT_GROUPS = 4
EXPERTS_PER_GROUP = 8
N_EXPERTS = N_EXPERT_GROUPS * EXPERTS_PER_GROUP
D_EXPERT = 256
SC_GATHER_CHUNK = 64
EXPERT_TILE = 256
SMALL_W = 128
SEQ_TILE = 256
VMEM_LIMIT = 56 * 1024 * 1024

(COL_AQ, COL_AF, COL_AI, COL_AG, COL_BQ, COL_BK, COL_BV, COL_CX, COL_CG,
 COL_DQ, COL_DK, COL_DV, COL_DZ) = range(13)
N_PROJ = 13 * GROUP_WIDTH + 2 * SMALL_W
COL_SMALL = 13 * GROUP_WIDTH // SMALL_W
LANE_FOX = 0
LANE_BETA = 4
LANE_GDEC = 8


def _cparams(*sem):
    return pltpu.CompilerParams(dimension_semantics=sem, vmem_limit_bytes=VMEM_LIMIT)


def _dot(a, b):
    return jnp.dot(a, b, preferred_element_type=F32)


def _dot_nt(a, b):
    return lax.dot_general(a, b, (((1,), (1,)), ((), ())), preferred_element_type=F32)


def _dot_tn(a, b):
    return lax.dot_general(a, b, (((0,), (0,)), ((), ())), preferred_element_type=F32)


def _split3(x):
    h = x.astype(BF16)
    r = x - h.astype(F32)
    m = r.astype(BF16)
    l = (r - m.astype(F32)).astype(BF16)
    return h, m, l


def _dot_exact_rhs(x, w_bf16):
    h, m, l = _split3(x)
    return _dot(h, w_bf16) + _dot(m, w_bf16) + _dot(l, w_bf16)


def _dot_exact_lhs(w_bf16, x):
    h, m, l = _split3(x)
    return _dot(w_bf16, h) + _dot(w_bf16, m) + _dot(w_bf16, l)


def _iota(shape, dim):
    return lax.broadcasted_iota(jnp.int32, shape, dim)


def _head_ones(n=GROUP_WIDTH):
    r = _iota((n, n), 0) // HEAD_DIM
    c = _iota((n, n), 1) // HEAD_DIM
    return r == c


def _sigmoid(x):
    return 1.0 / (1.0 + jnp.exp(-x))


def _silu(x):
    return x * _sigmoid(x)


def _log_sigmoid(x):
    return jnp.minimum(x, 0.0) - jnp.log1p(jnp.exp(-jnp.abs(x)))


def _softplus(x):
    return jnp.maximum(x, 0.0) + jnp.log1p(jnp.exp(-jnp.abs(x)))


def _gelu_tanh(x):
    return 0.5 * x * (1.0 + jnp.tanh(math.sqrt(2.0 / math.pi) * (x + 0.044715 * (x * x * x))))


def _pack_bf16_pairs(x):
    n = x.shape[1] // 2
    u = lax.bitcast_convert_type(x.astype(BF16).astype(F32), jnp.uint32)
    return u[:, :n] | (u[:, n:] >> 16)


def _unpack_bf16_pairs(p):
    hi = lax.bitcast_convert_type(p & jnp.uint32(0xFFFF0000), F32)
    lo = lax.bitcast_convert_type(p << 16, F32)
    return jnp.concatenate([hi, lo], axis=1)


def _head_mean_sq(x, ones_bf16):
    return _dot_exact_rhs(x * x, ones_bf16) * (1.0 / HEAD_DIM)


def _head_rms(x, ones_bf16):
    return x * lax.rsqrt(_head_mean_sq(x, ones_bf16) + EPS)


def _stack_heads(x):
    lane_head = _iota(x.shape, 1) // HEAD_DIM
    parts = []
    for h in range(HEADS):
        parts.append(jnp.where(lane_head == h, x, 0.0))
    return jnp.concatenate(parts, axis=0)


def _unstack_heads(xs, rows):
    out = xs[0:rows]
    for h in range(1, HEADS):
        out = out + xs[h * rows:(h + 1) * rows]
    return out


def _causal_conv4(x, prev8, w):
    r = x.shape[0]
    row8 = _iota((8, x.shape[1]), 0)
    acc = x * w[CONV_W - 1:CONV_W, :]
    for k in range(1, CONV_W):
        xs = pltpu.roll(x, k, 0)
        ps = pltpu.roll(prev8, k, 0)
        top = jnp.where(row8 < k, ps, xs[0:8])
        xs = jnp.concatenate([top, xs[8:r]], axis=0)
        acc = acc + xs * w[CONV_W - 1 - k:CONV_W - k, :]
    return acc


def _norm_matmul_kernel(x_ref, g_ref, w_ref, o_ref):
    x = x_ref[...]
    ms = jnp.mean(x * x, axis=-1, keepdims=True)
    h = (x * lax.rsqrt(ms + EPS)) * g_ref[...]
    o_ref[...] = _dot(h.astype(BF16), w_ref[...]).astype(o_ref.dtype)


def norm_matmul(x, g, w_bf16, tm, out_dtype=F32):
    m, d = x.shape
    n = w_bf16.shape[1]
    return pl.pallas_call(
        _norm_matmul_kernel,
        grid=(m // tm,),
        in_specs=[pl.BlockSpec((tm, d), lambda i: (i, 0)),
                  pl.BlockSpec((1, d), lambda i: (0, 0)),
                  pl.BlockSpec((d, n), lambda i: (0, 0))],
        out_specs=pl.BlockSpec((tm, n), lambda i: (i, 0)),
        out_shape=jax.ShapeDtypeStruct((m, n), out_dtype),
        compiler_params=_cparams("parallel"),
        name="norm_matmul",
    )(x, g.reshape(1, d), w_bf16)


def _small_prep_kernel(s_ref, p_ref, o_ref, x_ref):
    S = s_ref.shape[1]
    W = s_ref.shape[2]
    blk = SEQ_TILE
    lane = _iota((blk, W), 1)
    er = _iota((W, 2 * HEADS * W), 0)
    ec = _iota((W, 2 * HEADS * W), 1) // W
    expand = jnp.where(er == jnp.where(ec < HEADS, LANE_FOX + ec, LANE_GDEC + ec - HEADS), 1.0, 0.0).astype(BF16)
    is_fox = (lane >= LANE_FOX) & (lane < LANE_FOX + HEADS)
    is_beta = (lane >= LANE_BETA) & (lane < LANE_BETA + HEADS)
    is_gdec = (lane >= LANE_GDEC) & (lane < LANE_GDEC + HEADS)
    r = _iota((blk, blk), 0)
    c = _iota((blk, blk), 1)
    tril_all = jnp.where(r >= c, 1.0, 0.0).astype(BF16)
    tril_chunk = jnp.where((r >= c) & (r // GDN_CHUNK == c // GDN_CHUNK), 1.0, 0.0).astype(BF16)
    neg_exp_a = -jnp.exp(p_ref[1:2, :])
    carry = jnp.zeros((1, W), F32)
    for i in range(S // blk):
        sl = pl.ds(i * blk, blk)
        z = s_ref[0, sl, :] + p_ref[0:1, :]
        fox = jnp.where(is_fox, _log_sigmoid(z), 0.0)
        beta = jnp.where(is_beta, _sigmoid(z), 0.0)
        gdec = jnp.where(is_gdec, neg_exp_a * _softplus(z), 0.0)
        cf = _dot_exact_lhs(tril_all, fox) + carry
        cg = _dot_exact_lhs(tril_chunk, gdec)
        carry = cf[blk - 1:blk, :]
        o_ref[0, sl, :] = cf + cg + beta
        x_ref[0, sl, :] = _dot_exact_rhs(cf + cg, expand)


def small_prep(proj3, params):
    b, s, _ = proj3.shape
    w = SMALL_W
    return pl.pallas_call(
        _small_prep_kernel,
        grid=(b,),
        in_specs=[pl.BlockSpec((1, s, w), lambda i: (i, 0, COL_SMALL)),
                  pl.BlockSpec((8, w), lambda i: (0, 0))],
        out_specs=[pl.BlockSpec((1, s, w), lambda i: (i, 0, 0)),
                   pl.BlockSpec((1, s, 2 * HEADS * w), lambda i: (i, 0, 0))],
        out_shape=[jax.ShapeDtypeStruct((b, s, w), F32), jax.ShapeDtypeStruct((b, s, 2 * HEADS * w), F32)],
        compiler_params=_cparams("parallel"),
        name="small_prep",
    )(proj3, params)


def _hgrn2_kernel(q_ref, f_ref, i_ref, g_ref, lb_ref, o_ref, st_ref, qs, ks, vs, bs, os_):
    nb = q_ref.shape[0]
    T = q_ref.shape[1]
    C = HGRN_CHUNK
    seqs = range(nb)

    @pl.when(pl.program_id(1) == 0)
    def _():
        st_ref[...] = jnp.zeros_like(st_ref)

    same_head = _head_ones()
    ones_bf = jnp.where(same_head, 1.0, 0.0).astype(BF16)
    lb = lb_ref[...]
    log_lb = jnp.log(lb)
    r = _iota((T, T), 0)
    c = _iota((T, T), 1)
    tril_chunk = jnp.where((r >= c) & (r // C == c // C), 1.0, 0.0).astype(BF16)
    for bb in seqs:
        fl = f_ref[bb]
        c2 = jnp.log1p(-lb) + _log_sigmoid(fl)
        mx = jnp.maximum(log_lb, c2)
        log_f = mx + jnp.log1p(jnp.exp(-jnp.abs(log_lb - c2)))
        bs[bb] = _dot_exact_lhs(tril_chunk, log_f)
        qs[bb] = _silu(q_ref[bb])
        ks[bb] = (1.0 - lb) * _sigmoid(-fl)
        vs[bb] = i_ref[bb]

    trow = _iota((C, GROUP_WIDTH), 0)

    def chunk(ci, carry):
        r0 = pl.multiple_of(ci * C, C)
        qc = [qs[bb, pl.ds(r0, C), :] for bb in seqs]
        kc = [ks[bb, pl.ds(r0, C), :] for bb in seqs]
        vc = [vs[bb, pl.ds(r0, C), :] for bb in seqs]
        bc = [bs[bb, pl.ds(r0, C), :] for bb in seqs]
        st = [st_ref[bb] for bb in seqs]
        a = []
        for bb in seqs:
            parts = []
            for s in range(C):
                parts.append(jnp.where(trow >= s, jnp.exp(bc[bb] - bc[bb][s:s + 1, :]) * (qc[bb] * kc[bb][s:s + 1, :]), 0.0))
            a.append(jnp.concatenate(parts, axis=0).astype(BF16))
        sc = [_dot(a[bb], ones_bf) for bb in seqs]
        o = [_dot_nt((qc[bb] * jnp.exp(bc[bb])).astype(BF16), st[bb].astype(BF16)) for bb in seqs]
        b_last = [bc[bb][C - 1:C, :] for bb in seqs]
        upd = [_dot_tn(vc[bb].astype(BF16), (kc[bb] * jnp.exp(b_last[bb] - bc[bb])).astype(BF16)) for bb in seqs]
        for bb in seqs:
            acc = o[bb]
            for s in range(C):
                acc = acc + sc[bb][s * C:(s + 1) * C, :] * vc[bb][s:s + 1, :]
            os_[bb, pl.ds(r0, C), :] = acc
            st_ref[bb] = st[bb] * jnp.exp(b_last[bb]) + jnp.where(same_head, upd[bb], 0.0)
        return carry

    lax.fori_loop(0, T // C, chunk, 0)
    for bb in seqs:
        o_ref[bb] = _head_rms(os_[bb], ones_bf) * _silu(g_ref[bb])


def hgrn2(proj3, lb):
    b, s, _ = proj3.shape
    t = SEQ_TILE
    gw = GROUP_WIDTH
    nb = HGRN_BATCH
    col = lambda cidx: pl.BlockSpec((nb, t, gw), lambda i, j: (i, j, cidx))
    return pl.pallas_call(
        _hgrn2_kernel,
        grid=(b // nb, s // t),
        in_specs=[col(COL_AQ), col(COL_AF), col(COL_AI), col(COL_AG),
                  pl.BlockSpec((1, gw), lambda i, j: (0, 0))],
        out_specs=pl.BlockSpec((nb, t, gw), lambda i, j: (i, j, 0)),
        out_shape=jax.ShapeDtypeStruct((b, s, gw), F32),
        scratch_shapes=[pltpu.VMEM((nb, gw, gw), F32)] + [pltpu.VMEM((nb, t, gw), F32)] * 5,
        compiler_params=_cparams("parallel", "arbitrary"),
        name="hgrn2",
    )(proj3, proj3, proj3, proj3, lb.reshape(1, gw))


def _fox_kernel(q_ref, k_ref, v_ref, cr_ref, cx_ref, o_ref, kb_ref, vt_ref, acc_ref):
    nb = q_ref.shape[0]
    tq = q_ref.shape[1]
    tk = tq
    S = k_ref.shape[1]
    n = HEADS * tq
    qi = pl.program_id(1)
    seqs = range(nb)

    @pl.when(qi == 0)
    def _():
        for bb in seqs:
            for i in range(S // tk):
                sl = pl.ds(i * tk, tk)
                kb_ref[bb, sl, :] = k_ref[bb, sl, :].astype(BF16)
                vt_ref[bb, :, sl] = v_ref[bb, sl, :].T.astype(BF16)

    q0 = pl.multiple_of(qi * tq, tq)
    qs, cq_row = [], []
    for bb in seqs:
        qs.append(_stack_heads(q_ref[bb] * (HEAD_DIM ** -0.5)).astype(BF16))
        cr = cr_ref[bb, :, pl.ds(q0, tq)]
        cq_row.append(jnp.concatenate([cr[h:h + 1, :] for h in range(HEADS)], axis=1))
    acc_ref[...] = jnp.zeros_like(acc_ref)
    reps = tq // SMALL_W

    def step(kb, stats, masked):
        k0 = pl.multiple_of(kb * tk, tk)
        st = [_dot_nt(kb_ref[bb, pl.ds(k0, tk), :], qs[bb]) for bb in seqs]
        for bb in seqs:
            cx = cx_ref[bb, pl.ds(k0, tk), :]
            ck = jnp.concatenate([cx[:, h * SMALL_W:(h + 1) * SMALL_W] for h in range(HEADS) for _ in range(reps)], axis=1)
            st[bb] = st[bb] + (cq_row[bb] - ck)
            if masked:
                st[bb] = jnp.where(_iota((tk, n), 0) <= _iota((tk, n), 1) % tq, st[bb], -jnp.inf)
        m_new = [jnp.maximum(stats[bb][0], jnp.max(st[bb], axis=0, keepdims=True)) for bb in seqs]
        alpha = [jnp.exp(stats[bb][0] - m_new[bb]) for bb in seqs]
        p = [jnp.exp(st[bb] - m_new[bb]) for bb in seqs]
        l_new = [alpha[bb] * stats[bb][1] + jnp.sum(p[bb], axis=0, keepdims=True) for bb in seqs]
        pv = [_dot(vt_ref[bb, :, pl.ds(k0, tk)], p[bb].astype(BF16)) for bb in seqs]
        for bb in seqs:
            acc_ref[bb] = alpha[bb] * acc_ref[bb] + pv[bb]
        return tuple((m_new[bb], l_new[bb]) for bb in seqs)

    init = tuple((jnp.full((1, n), -jnp.inf, F32), jnp.zeros((1, n), F32)) for _ in seqs)
    stats = lax.fori_loop(0, qi, lambda kb, c: step(kb, c, False), init)
    stats = step(qi, stats, True)

    ones_bf = jnp.where(_head_ones(), 1.0, 0.0).astype(BF16)
    for bb in seqs:
        acc = acc_ref[bb] * (1.0 / stats[bb][1])
        ot = jnp.concatenate([acc[h * HEAD_DIM:(h + 1) * HEAD_DIM, h * tq:(h + 1) * tq] for h in range(HEADS)], axis=0)
        o_ref[bb] = _head_rms(ot.T, ones_bf)


def fox_attention(proj3, c_row, c_exp):
    b, s, _ = proj3.shape
    tq = SEQ_TILE
    gw = GROUP_WIDTH
    nb = FOX_BATCH
    return pl.pallas_call(
        _fox_kernel,
        grid=(b // nb, s // tq),
        in_specs=[pl.BlockSpec((nb, tq, gw), lambda i, j: (i, j, COL_BQ)),
                  pl.BlockSpec((nb, s, gw), lambda i, j: (i, 0, COL_BK)),
                  pl.BlockSpec((nb, s, gw), lambda i, j: (i, 0, COL_BV)),
                  pl.BlockSpec((nb, 8, s), lambda i, j: (i, 0, 0)),
                  pl.BlockSpec((nb, s, HEADS * SMALL_W), lambda i, j: (i, 0, 0))],
        out_specs=pl.BlockSpec((nb, tq, gw), lambda i, j: (i, j, 0)),
        out_shape=jax.ShapeDtypeStruct((b, s, gw), F32),
        scratch_shapes=[pltpu.VMEM((nb, s, gw), BF16),
                        pltpu.VMEM((nb, gw, s), BF16),
                        pltpu.VMEM((nb, gw, HEADS * tq), F32)],
        compiler_params=_cparams("parallel", "arbitrary"),
        name="fox_attention",
    )(proj3, proj3, proj3, c_row, c_exp)


def _rglru_kernel(x_ref, g_ref, cw_ref, cb_ref, wa_ref, ba_ref, wx_ref, bx_ref, lam_ref, o_ref, prev_ref, h_ref):
    @pl.when(pl.program_id(1) == 0)
    def _():
        prev_ref[...] = jnp.zeros_like(prev_ref)
        h_ref[...] = jnp.zeros_like(h_ref)

    x = x_ref[0]
    t = x.shape[0]
    xc = _causal_conv4(x, prev_ref[...], cw_ref[...]) + cb_ref[...]
    prev_ref[...] = x[t - 8:t, :]
    xb = xc.astype(BF16)
    r = _sigmoid(_dot(xb, wa_ref[...]) + ba_ref[...])
    ig = _sigmoid(_dot(xb, wx_ref[...]) + bx_ref[...])
    log_a = (-LRU_C * r) * _softplus(-lam_ref[...])
    a = jnp.exp(log_a)
    u = jnp.sqrt(1.0 - jnp.exp(2.0 * log_a)) * (ig * xc)
    row = _iota(a.shape, 0)
    d = 1
    while d < t:
        valid = row >= d
        u = jnp.where(valid, a * pltpu.roll(u, d, 0) + u, u)
        a = jnp.where(valid, a * pltpu.roll(a, d, 0), a)
        d *= 2
    h = a * h_ref[...] + u
    h_ref[...] = h[t - 1:t, :]
    ones_bf = jnp.where(_head_ones(), 1.0, 0.0).astype(BF16)
    o_ref[0] = _head_rms(h * _gelu_tanh(g_ref[0]), ones_bf)


def rglru(proj3, conv_w, conv_b, wa_bd, ba, wx_bd, bx, lam):
    b, s, _ = proj3.shape
    t = SEQ_TILE
    gw = GROUP_WIDTH
    row = lambda: pl.BlockSpec((1, gw), lambda i, j: (0, 0))
    return pl.pallas_call(
        _rglru_kernel,
        grid=(b, s // t),
        in_specs=[pl.BlockSpec((1, t, gw), lambda i, j: (i, j, COL_CX)),
                  pl.BlockSpec((1, t, gw), lambda i, j: (i, j, COL_CG)),
                  pl.BlockSpec((CONV_W, gw), lambda i, j: (0, 0)), row(),
                  pl.BlockSpec((gw, gw), lambda i, j: (0, 0)), row(),
                  pl.BlockSpec((gw, gw), lambda i, j: (0, 0)), row(), row()],
        out_specs=pl.BlockSpec((1, t, gw), lambda i, j: (i, j, 0)),
        out_shape=jax.ShapeDtypeStruct((b, s, gw), F32),
        scratch_shapes=[pltpu.VMEM((8, gw), F32), pltpu.VMEM((1, gw), F32)],
        compiler_params=_cparams("parallel", "arbitrary"),
        name="rglru",
    )(proj3, proj3, conv_w, conv_b.reshape(1, gw), wa_bd, ba.reshape(1, gw), wx_bd, bx.reshape(1, gw), lam.reshape(1, gw))


def _gdn_kernel(q_ref, k_ref, v_ref, z_ref, sm_ref, ge_ref, gr_ref, cw_ref, o_ref, st_ref, pq_ref, pk_ref, pv_ref):
    nb = q_ref.shape[0]
    T = q_ref.shape[1]
    C = GDN_CHUNK

    @pl.when(pl.program_id(1) == 0)
    def _():
        st_ref[...] = jnp.zeros_like(st_ref)
        pq_ref[...] = jnp.zeros_like(pq_ref)
        pk_ref[...] = jnp.zeros_like(pk_ref)
        pv_ref[...] = jnp.zeros_like(pv_ref)

    same_head = _head_ones()
    ones_bf = jnp.where(same_head, 1.0, 0.0).astype(BF16)
    cw = cw_ref[...]
    gw = GROUP_WIDTH
    er = _iota((SMALL_W, gw), 0)
    ec = _iota((SMALL_W, gw), 1) // HEAD_DIM
    exp_beta = jnp.where(er == LANE_BETA + ec, 1.0, 0.0).astype(BF16)
    exp_g = jnp.where(er == LANE_GDEC + ec, 1.0, 0.0).astype(BF16)
    n = HEADS * C
    rr = _iota((n, n), 0)
    cc = _iota((n, n), 1)
    incl = same_head & (rr % C >= cc % C)
    strict = same_head & (rr % C > cc % C)

    prep = []
    for bb in range(nb):
        xq, xk, xv = q_ref[bb], k_ref[bb], v_ref[bb]
        q = _silu(_causal_conv4(xq, pq_ref[bb], cw[:, 0:gw]))
        k = _silu(_causal_conv4(xk, pk_ref[bb], cw[:, gw:2 * gw]))
        v = _silu(_causal_conv4(xv, pv_ref[bb], cw[:, 2 * gw:3 * gw]))
        pq_ref[bb] = xq[T - 8:T, :]
        pk_ref[bb] = xk[T - 8:T, :]
        pv_ref[bb] = xv[T - 8:T, :]
        q = q * lax.rsqrt(_dot_exact_rhs(q * q, ones_bf) + EPS) * (HEAD_DIM ** -0.5)
        k = k * lax.rsqrt(_dot_exact_rhs(k * k, ones_bf) + EPS)
        sm = sm_ref[bb]
        betax = _dot_exact_rhs(sm, exp_beta)
        gx = _dot_exact_rhs(sm, exp_g)
        prep.append((q, k, v, betax, gx))

    reps = gw // SMALL_W
    n_chunks = T // C
    eye = jnp.where(rr == cc, 1.0, 0.0)

    chains = []
    for ci in range(n_chunks):
        sl = slice(ci * C, (ci + 1) * C)
        for bb in range(nb):
            q, k, v, betax, gx = prep[bb]
            qc, kc, vc, bx, gc = q[sl], k[sl], v[sl], betax[sl], gx[sl]
            grow = gr_ref[bb, ci, 0:1, :]
            ge = ge_ref[bb, sl, :]
            gcol = jnp.concatenate(
                [jnp.concatenate([ge[:, h * SMALL_W:(h + 1) * SMALL_W]] * reps, axis=1) for h in range(HEADS)], axis=0)
            eg = jnp.exp(gc)
            kb = kc * bx
            g_last = gc[C - 1:C, :]
            gamma = jnp.where(incl, jnp.exp(jnp.where(incl, gcol - grow, 0.0)), 0.0)
            ks_bf = _stack_heads(kc).astype(BF16)
            m = jnp.where(strict, _dot_nt(_stack_heads(kb).astype(BF16), ks_bf) * gamma, 0.0)
            chains.append(dict(
                x=eye - m, y=m,
                a_qk=(_dot_nt(_stack_heads(qc).astype(BF16), ks_bf) * gamma).astype(BF16),
                rhs_u=_stack_heads(vc * bx).astype(BF16),
                rhs_w=_stack_heads(kb * eg).astype(BF16),
                q_dec=_stack_heads(qc * eg).astype(BF16),
                k_dec=_stack_heads(kc * jnp.exp(g_last - gc)).astype(BF16),
                decay=jnp.exp(g_last)))
    for _ in range(5):
        for ch in chains:
            yb = ch["y"].astype(BF16)
            ch["y"] = _dot(yb, yb)
        for ch in chains:
            ch["x"] = ch["x"] + _dot(ch["x"].astype(BF16), ch["y"].astype(BF16))
    for ch in chains:
        t_inv = ch["x"].astype(BF16)
        ch["u"] = _dot(t_inv, ch["rhs_u"])
        ch["wk"] = _dot(t_inv, ch["rhs_w"]).astype(BF16)

    outs = [[] for _ in range(nb)]
    for ci in range(n_chunks):
        for bb in range(nb):
            ch = chains[ci * nb + bb]
            st = st_ref[bb]
            st_bf = st.astype(BF16)
            v_new = (ch["u"] - _dot(ch["wk"], st_bf)).astype(BF16)
            o = _dot(ch["q_dec"], st_bf) + _dot(ch["a_qk"], v_new)
            outs[bb].append(_unstack_heads(o, C))
            st_ref[bb] = st * ch["decay"] + _dot_tn(ch["k_dec"], v_new)
    for bb in range(nb):
        o = jnp.concatenate(outs[bb], axis=0)
        o_ref[bb] = _head_rms(o, ones_bf) * _silu(z_ref[bb])


def gdn(proj3, sm, c_exp, g_row, conv_w):
    b, s, _ = proj3.shape
    t = SEQ_TILE
    gw = GROUP_WIDTH
    nb = GDN_BATCH
    col = lambda cidx: pl.BlockSpec((nb, t, gw), lambda i, j: (i, j, cidx))
    return pl.pallas_call(
        _gdn_kernel,
        grid=(b // nb, s // t),
        in_specs=[col(COL_DQ), col(COL_DK), col(COL_DV), col(COL_DZ),
                  pl.BlockSpec((nb, t, SMALL_W), lambda i, j: (i, j, 0)),
                  pl.BlockSpec((nb, t, HEADS * SMALL_W), lambda i, j: (i, j, 1)),
                  pl.BlockSpec((nb, t // GDN_CHUNK, 8, gw), lambda i, j: (i, j, 0, 0)),
                  pl.BlockSpec((CONV_W, 3 * gw), lambda i, j: (0, 0))],
        out_specs=pl.BlockSpec((nb, t, gw), lambda i, j: (i, j, 0)),
        out_shape=jax.ShapeDtypeStruct((b, s, gw), F32),
        scratch_shapes=[pltpu.VMEM((nb, gw, gw), F32)] + [pltpu.VMEM((nb, 8, gw), F32)] * 3,
        compiler_params=_cparams("parallel", "arbitrary"),
        name="gdn",
    )(proj3, proj3, proj3, proj3, sm, c_exp, g_row, conv_w)


def _mix_out_kernel(x_ref, ya_ref, yb_ref, yc_ref, yd_ref, gain_ref, w_ref, o_ref):
    gw = GROUP_WIDTH
    gain = gain_ref[...]
    acc = x_ref[0]
    for i, y_ref in enumerate((ya_ref, yb_ref, yc_ref, yd_ref)):
        yg = (y_ref[0] * gain[:, i * gw:(i + 1) * gw]).astype(BF16)
        acc = acc + _dot(yg, w_ref[i * gw:(i + 1) * gw, :])
    o_ref[0] = acc


def mix_out(x3, ya, yb, yc, yd, gain, w_out_bf16):
    b, s, d = x3.shape
    t = SEQ_TILE
    gw = GROUP_WIDTH
    grp = lambda: pl.BlockSpec((1, t, gw), lambda i, j: (i, j, 0))
    return pl.pallas_call(
        _mix_out_kernel,
        grid=(b, s // t),
        in_specs=[pl.BlockSpec((1, t, d), lambda i, j: (i, j, 0)),
                  grp(), grp(), grp(), grp(),
                  pl.BlockSpec((1, 4 * gw), lambda i, j: (0, 0)),
                  pl.BlockSpec((4 * gw, d), lambda i, j: (0, 0))],
        out_specs=pl.BlockSpec((1, t, d), lambda i, j: (i, j, 0)),
        out_shape=jax.ShapeDtypeStruct((b, s, d), F32),
        compiler_params=_cparams("parallel", "parallel"),
        name="mix_out",
    )(x3, ya, yb, yc, yd, gain.reshape(1, 4 * gw), w_out_bf16)


def _mem_attn_kernel(x_ref, g_ref, wq_ref, kv_ref, wo_ref, o_ref):
    x = x_ref[0]
    t = x.shape[0]
    gw = GROUP_WIDTH
    ms = jnp.mean(x * x, axis=-1, keepdims=True)
    h = ((x * lax.rsqrt(ms + EPS)) * g_ref[...]).astype(BF16)
    q = _dot(h, wq_ref[...])
    kv = kv_ref[0]
    k = kv[:, 0:gw].astype(BF16)
    v = kv[:, gw:2 * gw].astype(BF16)
    s = _dot_nt(k, _stack_heads(q).astype(BF16)) * (HEAD_DIM ** -0.5)
    s = s - jnp.max(s, axis=0, keepdims=True)
    p = jnp.exp(s)
    p = p * (1.0 / jnp.sum(p, axis=0, keepdims=True))
    ot = _dot_tn(v, p.astype(BF16))
    oc = jnp.concatenate([ot[hh * HEAD_DIM:(hh + 1) * HEAD_DIM, hh * t:(hh + 1) * t] for hh in range(HEADS)], axis=0).T
    o_ref[0] = x + _dot(oc.astype(BF16), wo_ref[...])


def mem_attention(x3, g, wq_bf16, kv3, wo_bf16):
    b, s, d = x3.shape
    t = SEQ_TILE
    m = kv3.shape[1]
    return pl.pallas_call(
        _mem_attn_kernel,
        grid=(b, s // t),
        in_specs=[pl.BlockSpec((1, t, d), lambda i, j: (i, j, 0)),
                  pl.BlockSpec((1, d), lambda i, j: (0, 0)),
                  pl.BlockSpec((d, GROUP_WIDTH), lambda i, j: (0, 0)),
                  pl.BlockSpec((1, m, 2 * GROUP_WIDTH), lambda i, j: (i, 0, 0)),
                  pl.BlockSpec((GROUP_WIDTH, d), lambda i, j: (0, 0))],
        out_specs=pl.BlockSpec((1, t, d), lambda i, j: (i, j, 0)),
        out_shape=jax.ShapeDtypeStruct((b, s, d), F32),
        compiler_params=_cparams("parallel", "parallel"),
        name="mem_attention",
    )(x3, g.reshape(1, d), wq_bf16, kv3, wo_bf16)


def _router_kernel(x_ref, g_ref, w_ref, b_ref, h_ref, r_ref, n_ref, cnt_ref):
    @pl.when(pl.program_id(0) == 0)
    def _():
        cnt_ref[...] = jnp.zeros_like(cnt_ref)

    x = x_ref[...]
    tm = x.shape[0]
    ms = jnp.mean(x * x, axis=-1, keepdims=True)
    h = (x * lax.rsqrt(ms + EPS)) * g_ref[...]
    h_ref[...] = _pack_bf16_pairs(h)
    hh, hm, _ = _split3(h)
    w = w_ref[...]
    wh, wm, _ = _split3(w)
    logits = _dot(hh, wh) + _dot(hh, wm) + _dot(hm, wh) + b_ref[...]
    lane = _iota(logits.shape, 1)
    big = jnp.int32(1 << 30)
    neg = -jnp.inf
    is_grp = (lane >= N_EXPERTS) & (lane < N_EXPERTS + N_EXPERT_GROUPS)
    gl = jnp.where(is_grp, logits, neg)
    gmax = jnp.max(gl, axis=-1, keepdims=True)
    p_grp = 1.0 / jnp.sum(jnp.exp(gl - gmax), axis=-1, keepdims=True)
    g_sel = jnp.min(jnp.where(gl == gmax, lane, big), axis=-1, keepdims=True) - N_EXPERTS
    in_grp = (lane < N_EXPERTS) & (lane // EXPERTS_PER_GROUP == g_sel)
    el = jnp.where(in_grp, logits, neg)
    m1 = jnp.max(el, axis=-1, keepdims=True)
    i1 = jnp.min(jnp.where(el == m1, lane, big), axis=-1, keepdims=True)
    el2 = jnp.where(lane == i1, neg, el)
    m2 = jnp.max(el2, axis=-1, keepdims=True)
    i2 = jnp.min(jnp.where(el2 == m2, lane, big), axis=-1, keepdims=True)
    e21 = jnp.exp(m2 - m1)
    w1 = 1.0 / (1.0 + e21)
    w2 = e21 / (1.0 + e21)
    oh1 = lane == i1
    oh2 = lane == i2
    onehot = jnp.where(oh1 | oh2, 1.0, 0.0)
    strict = jnp.where(_iota((tm, tm), 0) > _iota((tm, tm), 1), 1.0, 0.0).astype(BF16)
    before = _dot(strict, onehot.astype(BF16)) + cnt_ref[...]
    rank1 = jnp.sum(jnp.where(oh1, before, 0.0), axis=-1, keepdims=True)
    rank2 = jnp.sum(jnp.where(oh2, before, 0.0), axis=-1, keepdims=True)
    cnt = before[tm - 1:tm, :] + onehot[tm - 1:tm, :]
    cnt_ref[...] = cnt
    n_ref[...] = jnp.broadcast_to(cnt, n_ref.shape)
    cols = (i1.astype(F32), i2.astype(F32), p_grp * w1, p_grp * w2, rank1, rank2)
    route = jnp.zeros(logits.shape, F32)
    for j, col in enumerate(cols):
        route = jnp.where(lane == j, col, route)
    r_ref[...] = route


def router(x2, g, w_router, b_router, tm):
    m, d = x2.shape
    return pl.pallas_call(
        _router_kernel,
        grid=(m // tm,),
        in_specs=[pl.BlockSpec((tm, d), lambda i: (i, 0)),
                  pl.BlockSpec((1, d), lambda i: (0, 0)),
                  pl.BlockSpec((d, SMALL_W), lambda i: (0, 0)),
                  pl.BlockSpec((1, SMALL_W), lambda i: (0, 0))],
        out_specs=[pl.BlockSpec((tm, d // 2), lambda i: (i, 0)),
                   pl.BlockSpec((tm, SMALL_W), lambda i: (i, 0)),
                   pl.BlockSpec((8, SMALL_W), lambda i: (0, 0))],
        out_shape=[jax.ShapeDtypeStruct((m, d // 2), jnp.uint32), jax.ShapeDtypeStruct((m, SMALL_W), F32),
                   jax.ShapeDtypeStruct((8, SMALL_W), F32)],
        scratch_shapes=[pltpu.VMEM((1, SMALL_W), F32)],
        compiler_params=_cparams("arbitrary"),
        name="router",
    )(x2, g.reshape(1, d), w_router, b_router)


def _row_copy(src_ref, src_row, dst_ref, dst_row, sem):
    return pltpu.make_async_copy(src_ref.at[pl.ds(src_row, 1)], dst_ref.at[pl.ds(dst_row, 1)], sem)


def _dispatch_kernel(dest_ref, zrow_ref, h_ref, xs_ref, zbuf, sem):
    tm = h_ref.shape[0]
    base = pl.program_id(0) * tm

    @pl.when(pl.program_id(0) == 0)
    def _():
        zbuf[...] = jnp.zeros_like(zbuf)

        def zero_copy(e):
            row = pl.multiple_of(zrow_ref[e], EXPERT_TILE)
            return pltpu.make_async_copy(zbuf, xs_ref.at[pl.ds(row, EXPERT_TILE)], sem)

        for e in range(zrow_ref.shape[0]):
            @pl.when(zrow_ref[e] >= 0)
            def _():
                zero_copy(e).start()
        for e in range(zrow_ref.shape[0]):
            @pl.when(zrow_ref[e] >= 0)
            def _():
                zero_copy(e).wait()

    def start(t, c):
        for slot in range(2):
            _row_copy(h_ref, t, xs_ref, dest_ref[2 * (base + t) + slot], sem).start(priority=slot)
        return c

    def wait(t, c):
        for slot in range(2):
            _row_copy(h_ref, t, xs_ref, dest_ref[2 * (base + t) + slot], sem).wait()
        return c

    lax.fori_loop(0, tm, start, 0, unroll=8)
    lax.fori_loop(0, tm, wait, 0, unroll=8)


def moe_dispatch(h, dest, zrow, n_rows, tm):
    m, d = h.shape
    return pl.pallas_call(
        _dispatch_kernel,
        grid_spec=pltpu.PrefetchScalarGridSpec(
            num_scalar_prefetch=2,
            grid=(m // tm,),
            in_specs=[pl.BlockSpec((tm, d), lambda i, dest, zrow: (i, 0))],
            out_specs=pl.BlockSpec(memory_space=pl.ANY),
            scratch_shapes=[pltpu.VMEM((EXPERT_TILE, d), h.dtype), pltpu.SemaphoreType.DMA]),
        out_shape=jax.ShapeDtypeStruct((n_rows, d), h.dtype),
        compiler_params=_cparams("arbitrary"),
        name="moe_dispatch",
    )(dest, zrow, h)


def _experts_kernel(te_ref, nu_ref, xs_ref, wgu_ref, wdn_ref, y_ref, wgu_bf, wdn_bf):
    i = pl.program_id(0)

    @pl.when(i >= nu_ref[0])
    def _():
        y_ref[...] = jnp.zeros_like(y_ref)

    @pl.when(i < nu_ref[0])
    def _():
        prev = te_ref[jnp.maximum(i - 1, 0)]

        @pl.when((i == 0) | (te_ref[i] != prev))
        def _():
            wgu_bf[...] = wgu_ref[0, 0].astype(BF16)
            wdn_bf[...] = wdn_ref[0, 0].astype(BF16)

        gu = _dot(_unpack_bf16_pairs(xs_ref[...]).astype(BF16), wgu_bf[...])
        act = _silu(gu[:, 0:D_EXPERT]) * gu[:, D_EXPERT:2 * D_EXPERT]
        y_ref[...] = _pack_bf16_pairs(_dot(act.astype(BF16), wdn_bf[...]))


def moe_experts(xs, tile_expert, n_used, w_gu, w_dn, layer, tm):
    n_rows, dp = xs.shape
    d = 2 * dp
    tile = lambda i, te, nu: (jnp.minimum(i, nu[0] - 1), 0)
    wsel = lambda i, te, nu: (layer, te[jnp.minimum(i, nu[0] - 1)], 0, 0)
    return pl.pallas_call(
        _experts_kernel,
        grid_spec=pltpu.PrefetchScalarGridSpec(
            num_scalar_prefetch=2,
            grid=(n_rows // tm,),
            in_specs=[pl.BlockSpec((tm, dp), tile),
                      pl.BlockSpec((1, 1, d, 2 * D_EXPERT), wsel),
                      pl.BlockSpec((1, 1, D_EXPERT, d), wsel)],
            out_specs=pl.BlockSpec((tm, dp), lambda i, te, nu: (i, 0)),
            scratch_shapes=[pltpu.VMEM((d, 2 * D_EXPERT), BF16), pltpu.VMEM((D_EXPERT, d), BF16)]),
        out_shape=jax.ShapeDtypeStruct((n_rows, dp), jnp.uint32),
        compiler_params=_cparams("arbitrary"),
        name="moe_experts",
    )(tile_expert, n_used, xs, w_gu, w_dn)


def _combine_kernel(dest_ref, x_ref, r_ref, g_ref, ys_ref, o_ref, ybuf, sem, *, final_norm):
    tm = x_ref.shape[0]
    base = pl.program_id(0) * tm

    def start(t, c):
        for slot in range(2):
            _row_copy(ys_ref, dest_ref[2 * (base + t) + slot], ybuf.at[slot], t, sem).start(priority=slot)
        return c

    def wait(t, c):
        for slot in range(2):
            _row_copy(ys_ref, dest_ref[2 * (base + t) + slot], ybuf.at[slot], t, sem).wait()
        return c

    lax.fori_loop(0, tm, start, 0, unroll=8)
    r = r_ref[...]
    g1 = r[:, 2:3]
    g2 = r[:, 3:4]
    lax.fori_loop(0, tm, wait, 0, unroll=8)
    out = x_ref[...] + g1 * _unpack_bf16_pairs(ybuf[0]) + g2 * _unpack_bf16_pairs(ybuf[1])
    if final_norm:
        ms = jnp.mean(out * out, axis=-1, keepdims=True)
        out = (out * lax.rsqrt(ms + EPS)) * g_ref[...]
    o_ref[...] = out


def moe_combine(x2, route, dest, ys, g_final, tm, final_norm):
    m, d = x2.shape
    return pl.pallas_call(
        functools.partial(_combine_kernel, final_norm=final_norm),
        grid_spec=pltpu.PrefetchScalarGridSpec(
            num_scalar_prefetch=1,
            grid=(m // tm,),
            in_specs=[pl.BlockSpec((tm, d), lambda i, dest: (i, 0)),
                      pl.BlockSpec((tm, SMALL_W), lambda i, dest: (i, 0)),
                      pl.BlockSpec((1, d), lambda i, dest: (0, 0)),
                      pl.BlockSpec(memory_space=pl.ANY)],
            out_specs=pl.BlockSpec((tm, d), lambda i, dest: (i, 0)),
            scratch_shapes=[pltpu.VMEM((2, tm, d // 2), jnp.uint32), pltpu.SemaphoreType.DMA]),
        out_shape=jax.ShapeDtypeStruct((m, d), F32),
        compiler_params=_cparams("arbitrary"),
        name="moe_combine",
    )(dest, x2, route, g_final.reshape(1, d), ys)


def sc_gather_rows(table, idx):
    n_idx = idx.shape[0]
    width = table.shape[1]
    info = plsc.get_sparse_core_info()
    n_workers = info.num_cores * info.num_subcores
    per_worker = n_idx // n_workers
    chunk = SC_GATHER_CHUNK
    assert n_idx % (n_workers * chunk) == 0 and width % info.num_lanes == 0
    mesh = plsc.VectorSubcoreMesh(core_axis_name="c", subcore_axis_name="s")

    def body(table_hbm, idx_hbm, out_hbm, idx_v, rows_v, sem):
        wid = lax.axis_index("s") * info.num_cores + lax.axis_index("c")
        base = wid * per_worker

        @pl.loop(0, per_worker // chunk)
        def _(j):
            off = pl.multiple_of(base + j * chunk, chunk)
            pltpu.sync_copy(idx_hbm.at[pl.ds(off, chunk)], idx_v)
            pltpu.async_copy(table_hbm.at[idx_v], rows_v, sem).wait()
            pltpu.sync_copy(rows_v, out_hbm.at[pl.ds(off, chunk)])

    return pl.kernel(
        body,
        out_type=jax.ShapeDtypeStruct((n_idx, width), table.dtype),
        mesh=mesh,
        scratch_types=[pltpu.VMEM((chunk,), jnp.int32), pltpu.VMEM((chunk, width), table.dtype),
                       pltpu.SemaphoreType.DMA],
        name="sc_gather_rows",
    )(table, idx)


def _combine_rows_kernel(x_ref, r_ref, g_ref, y_ref, o_ref, *, final_norm):
    r = r_ref[...]
    y = y_ref[...]
    half = y.shape[1] // 2
    out = x_ref[...] + r[:, 2:3] * _unpack_bf16_pairs(y[:, :half]) + r[:, 3:4] * _unpack_bf16_pairs(y[:, half:])
    if final_norm:
        ms = jnp.mean(out * out, axis=-1, keepdims=True)
        out = (out * lax.rsqrt(ms + EPS)) * g_ref[...]
    o_ref[...] = out


def moe_combine_rows(x2, route, y_pairs, g_final, tm, final_norm):
    m, d = x2.shape
    return pl.pallas_call(
        functools.partial(_combine_rows_kernel, final_norm=final_norm),
        grid=(m // tm,),
        in_specs=[pl.BlockSpec((tm, d), lambda i: (i, 0)),
                  pl.BlockSpec((tm, SMALL_W), lambda i: (i, 0)),
                  pl.BlockSpec((1, d), lambda i: (0, 0)),
                  pl.BlockSpec((tm, d), lambda i: (i, 0))],
        out_specs=pl.BlockSpec((tm, d), lambda i: (i, 0)),
        out_shape=jax.ShapeDtypeStruct((m, d), F32),
        compiler_params=_cparams("parallel"),
        name="moe_combine_rows",
    )(x2, route, g_final.reshape(1, d), y_pairs)


def hier_moe_routed(x2, g_ffn, w_router, b_router, w_gu, w_dn, layer, g_final, final_norm):
    m, d = x2.shape
    te_rows = EXPERT_TILE
    n_rows = 2 * m + N_EXPERTS * te_rows
    h, route, counts = router(x2, g_ffn, w_router, b_router, tm=512)
    cnt = counts[0, 0:N_EXPERTS].astype(jnp.int32)
    padded = (cnt + te_rows - 1) // te_rows * te_rows
    seg_end = jnp.cumsum(padded)
    seg_start = seg_end - padded
    idx = route[:, 0:2].astype(jnp.int32)
    onehot = idx[:, :, None] == jnp.arange(N_EXPERTS, dtype=jnp.int32)
    dest = (jnp.sum(jnp.where(onehot, seg_start, 0), axis=-1) + route[:, 4:6].astype(jnp.int32)).reshape(2 * m)
    tile_start = jnp.arange(n_rows // te_rows, dtype=jnp.int32) * te_rows
    tile_expert = jnp.minimum(jnp.sum((tile_start[:, None] >= seg_end[None, :]).astype(jnp.int32), axis=1), N_EXPERTS - 1)
    n_used = (seg_end[N_EXPERTS - 1] // te_rows).reshape(1)
    tail = n_used[0] + jnp.arange(N_EXPERTS, dtype=jnp.int32)
    zrow = jnp.concatenate([jnp.where(padded > 0, seg_end - te_rows, -1),
                            jnp.where(tail < n_rows // te_rows, tail * te_rows, -1)])
    xs = moe_dispatch(h, dest, zrow, n_rows, tm=512)
    ys = moe_experts(xs, tile_expert, n_used, w_gu, w_dn, layer, tm=te_rows)
    y_pairs = sc_gather_rows(ys, dest).reshape(m, d)
    return moe_combine_rows(x2, route, y_pairs, g_final, tm=256, final_norm=final_norm)


def _w_in_prep_kernel(w_ref, o_ref):
    gw = GROUP_WIDTH
    a_end = 7 * gw
    c_start = a_end + HEADS
    c_end = c_start + 5 * gw
    z_start = c_end + 2 * HEADS
    assert a_end % SMALL_W == LANE_FOX and c_end % SMALL_W == LANE_BETA and LANE_GDEC == LANE_BETA + HEADS
    w = w_ref[...]
    o_ref[:, 0:a_end] = w[:, 0:a_end].astype(BF16)
    o_ref[:, a_end:a_end + 5 * gw] = w[:, c_start:c_end].astype(BF16)
    o_ref[:, 12 * gw:13 * gw] = w[:, z_start:z_start + gw].astype(BF16)
    blk_fox = w[:, a_end:a_end + SMALL_W]
    blk_gdn = w[:, c_end - LANE_BETA:c_end - LANE_BETA + SMALL_W]
    lane = _iota(blk_fox.shape, 1)
    small = jnp.where(lane < LANE_BETA, blk_fox, jnp.where(lane < LANE_GDEC + HEADS, blk_gdn, 0.0))
    o_ref[:, 13 * gw:13 * gw + SMALL_W] = small.astype(BF16)
    o_ref[:, 13 * gw + SMALL_W:] = jnp.zeros((w.shape[0], SMALL_W), BF16)


def w_in_prep(w_all, layer):
    _, d, n = w_all.shape
    tr = 128
    return pl.pallas_call(
        _w_in_prep_kernel,
        grid=(d // tr,),
        in_specs=[pl.BlockSpec((None, tr, n), lambda i: (layer, i, 0))],
        out_specs=pl.BlockSpec((tr, N_PROJ), lambda i: (i, 0)),
        out_shape=jax.ShapeDtypeStruct((d, N_PROJ), BF16),
        compiler_params=_cparams("parallel"),
        name="w_in_prep",
    )(w_all)


def _block_diag(w):
    h = w.shape[0]
    eye = jnp.eye(h, dtype=w.dtype)
    return (eye[:, None, :, None] * w[:, :, None, :]).reshape(h * HEAD_DIM, h * HEAD_DIM)


def kernel(x, mem, norm_mix, w_in, hgrn_lb, fox_fb, lru_conv_w, lru_conv_b, lru_wa, lru_ba, lru_wx, lru_bx, lru_lam, gdn_conv_w, gdn_a_log, gdn_dt_bias, mix_gain, w_out, norm_mem, norm_memkv, w_mq, w_mkv, w_mo, norm_ffn, w_rg, b_rg, w_re, b_re, w_e_gu, w_e_dn, norm_final):
    b, s, d = x.shape
    depth = w_in.shape[0]
    t_tok = b * s
    mlen = mem.shape[1]
    gw = GROUP_WIDTH

    lb_all = jnp.cumsum(jax.nn.softmax(hgrn_lb.astype(F32), axis=0), axis=0)
    lb_all = lb_all - lb_all[0]

    x = x.astype(F32)
    for l in range(depth):
        proj = norm_matmul(x.reshape(t_tok, d), norm_mix[l], w_in_prep(w_in, l), tm=256)
        proj3 = proj.reshape(b, s, N_PROJ)
        prm = jnp.zeros((8, SMALL_W), F32)
        prm = prm.at[0, LANE_FOX:LANE_FOX + HEADS].set(fox_fb[l].astype(F32))
        prm = prm.at[0, LANE_GDEC:LANE_GDEC + HEADS].set(gdn_dt_bias[l].astype(F32))
        prm = prm.at[1, LANE_GDEC:LANE_GDEC + HEADS].set(gdn_a_log[l].astype(F32))
        sm, c_exp = small_prep(proj3, prm)
        c_row = jnp.pad(jnp.swapaxes(sm[:, :, LANE_FOX:LANE_FOX + HEADS], 1, 2), ((0, 0), (0, 8 - HEADS), (0, 0)))
        g_row = sm[:, :, LANE_GDEC:LANE_GDEC + HEADS].reshape(b, s // GDN_CHUNK, GDN_CHUNK, HEADS)
        g_row = jnp.swapaxes(g_row, 2, 3).reshape(b, s // GDN_CHUNK, 1, gw)
        g_row = jnp.broadcast_to(g_row, (b, s // GDN_CHUNK, 8, gw))

        ya = hgrn2(proj3, lb_all[l])
        yb = fox_attention(proj3, c_row, c_exp)
        yc = rglru(proj3, lru_conv_w[l], lru_conv_b[l], _block_diag(lru_wa[l]).astype(BF16), lru_ba[l],
                   _block_diag(lru_wx[l]).astype(BF16), lru_bx[l], lru_lam[l])
        yd = gdn(proj3, sm, c_exp, g_row, gdn_conv_w[l])
        x = mix_out(x, ya, yb, yc, yd, mix_gain[l], w_out[l].astype(BF16))

        kv = norm_matmul(mem.reshape(b * mlen, d), norm_memkv[l], w_mkv[l].astype(BF16), tm=256)
        x = mem_attention(x, norm_mem[l], w_mq[l].astype(BF16), kv.reshape(b, mlen, 2 * gw), w_mo[l].astype(BF16))

        w_router = jnp.concatenate([w_re[l], w_rg[l], jnp.zeros((d, SMALL_W - N_EXPERTS - N_EXPERT_GROUPS), F32)], axis=1)
        b_router = jnp.concatenate([b_re[l], b_rg[l], jnp.zeros((SMALL_W - N_EXPERTS - N_EXPERT_GROUPS,), F32)]).reshape(1, SMALL_W)
        x = hier_moe_routed(x.reshape(t_tok, d), norm_ffn[l], w_router, b_router, w_e_gu, w_e_dn, l,
                            norm_final, final_norm=(l == depth - 1)).reshape(b, s, d)

    return x
```

```python
import functools
import math

import jax
import jax.numpy as jnp
from jax import lax
from jax.experimental import pallas as pl
from jax.experimental.pallas import tpu as pltpu
from jax.experimental.pallas import tpu_sc as plsc

F32 = jnp.float32
BF16 = jnp.bfloat16

HEAD_DIM = 64
GROUP_WIDTH = 256
HEADS = GROUP_WIDTH // HEAD_DIM
GDN_CHUNK = 64
GDN_BATCH = 2
FOX_BATCH = 2
HGRN_BATCH = 2
HGRN_CHUNK = 16
CONV_W = 4
LRU_C = 8.0
EPS = 1e-6
N_EXPERT_GROUPS = 4
EXPERTS_PER_GROUP = 8
N_EXPERTS = N_EXPERT_GROUPS * EXPERTS_PER_GROUP
D_EXPERT = 256
SC_GATHER_CHUNK = 64
EXPERT_TILE = 256
SMALL_W = 128
SEQ_TILE = 256
VMEM_LIMIT = 56 * 1024 * 1024

(COL_AQ, COL_AF, COL_AI, COL_AG, COL_BQ, COL_BK, COL_BV, COL_CX, COL_CG,
 COL_DQ, COL_DK, COL_DV, COL_DZ) = range(13)
N_PROJ = 13 * GROUP_WIDTH + 2 * SMALL_W
COL_SMALL = 13 * GROUP_WIDTH // SMALL_W
LANE_FOX = 0
LANE_BETA = 4
LANE_GDEC = 8


def _cparams(*sem):
    return pltpu.CompilerParams(dimension_semantics=sem, vmem_limit_bytes=VMEM_LIMIT)


def _dot(a, b):
    return jnp.dot(a, b, preferred_element_type=F32)


def _dot_nt(a, b):
    return lax.dot_general(a, b, (((1,), (1,)), ((), ())), preferred_element_type=F32)


def _dot_tn(a, b):
    return lax.dot_general(a, b, (((0,), (0,)), ((), ())), preferred_element_type=F32)


def _split3(x):
    h = x.astype(BF16)
    r = x - h.astype(F32)
    m = r.astype(BF16)
    l = (r - m.astype(F32)).astype(BF16)
    return h, m, l


def _dot_exact_rhs(x, w_bf16):
    h, m, l = _split3(x)
    return _dot(h, w_bf16) + _dot(m, w_bf16) + _dot(l, w_bf16)


def _dot_exact_lhs(w_bf16, x):
    h, m, l = _split3(x)
    return _dot(w_bf16, h) + _dot(w_bf16, m) + _dot(w_bf16, l)


def _iota(shape, dim):
    return lax.broadcasted_iota(jnp.int32, shape, dim)


def _head_ones(n=GROUP_WIDTH):
    r = _iota((n, n), 0) // HEAD_DIM
    c = _iota((n, n), 1) // HEAD_DIM
    return r == c


def _sigmoid(x):
    return 1.0 / (1.0 + jnp.exp(-x))


def _silu(x):
    return x * _sigmoid(x)


def _log_sigmoid(x):
    return jnp.minimum(x, 0.0) - jnp.log1p(jnp.exp(-jnp.abs(x)))


def _softplus(x):
    return jnp.maximum(x, 0.0) + jnp.log1p(jnp.exp(-jnp.abs(x)))


def _gelu_tanh(x):
    return 0.5 * x * (1.0 + jnp.tanh(math.sqrt(2.0 / math.pi) * (x + 0.044715 * (x * x * x))))


def _pack_bf16_pairs(x):
    n = x.shape[1] // 2
    u = lax.bitcast_convert_type(x.astype(BF16).astype(F32), jnp.uint32)
    return u[:, :n] | (u[:, n:] >> 16)


def _unpack_bf16_pairs(p):
    hi = lax.bitcast_convert_type(p & jnp.uint32(0xFFFF0000), F32)
    lo = lax.bitcast_convert_type(p << 16, F32)
    return jnp.concatenate([hi, lo], axis=1)


def _head_mean_sq(x, ones_bf16):
    return _dot_exact_rhs(x * x, ones_bf16) * (1.0 / HEAD_DIM)


def _head_rms(x, ones_bf16):
    return x * lax.rsqrt(_head_mean_sq(x, ones_bf16) + EPS)


def _stack_heads(x):
    lane_head = _iota(x.shape, 1) // HEAD_DIM
    parts = []
    for h in range(HEADS):
        parts.append(jnp.where(lane_head == h, x, 0.0))
    return jnp.concatenate(parts, axis=0)


def _unstack_heads(xs, rows):
    out = xs[0:rows]
    for h in range(1, HEADS):
        out = out + xs[h * rows:(h + 1) * rows]
    return out


def _causal_conv4(x, prev8, w):
    r = x.shape[0]
    row8 = _iota((8, x.shape[1]), 0)
    acc = x * w[CONV_W - 1:CONV_W, :]
    for k in range(1, CONV_W):
        xs = pltpu.roll(x, k, 0)
        ps = pltpu.roll(prev8, k, 0)
        top = jnp.where(row8 < k, ps, xs[0:8])
        xs = jnp.concatenate([top, xs[8:r]], axis=0)
        acc = acc + xs * w[CONV_W - 1 - k:CONV_W - k, :]
    return acc


def _norm_matmul_kernel(x_ref, g_ref, w_ref, o_ref):
    x = x_ref[...]
    ms = jnp.mean(x * x, axis=-1, keepdims=True)
    h = (x * lax.rsqrt(ms + EPS)) * g_ref[...]
    o_ref[...] = _dot(h.astype(BF16), w_ref[...]).astype(o_ref.dtype)


def norm_matmul(x, g, w_bf16, tm, out_dtype=F32):
    m, d = x.shape
    n = w_bf16.shape[1]
    return pl.pallas_call(
        _norm_matmul_kernel,
        grid=(m // tm,),
        in_specs=[pl.BlockSpec((tm, d), lambda i: (i, 0)),
                  pl.BlockSpec((1, d), lambda i: (0, 0)),
                  pl.BlockSpec((d, n), lambda i: (0, 0))],
        out_specs=pl.BlockSpec((tm, n), lambda i: (i, 0)),
        out_shape=jax.ShapeDtypeStruct((m, n), out_dtype),
        compiler_params=_cparams("parallel"),
        name="norm_matmul",
    )(x, g.reshape(1, d), w_bf16)


def _small_prep_kernel(s_ref, p_ref, o_ref, x_ref):
    S = s_ref.shape[1]
    W = s_ref.shape[2]
    blk = SEQ_TILE
    lane = _iota((blk, W), 1)
    er = _iota((W, 2 * HEADS * W), 0)
    ec = _iota((W, 2 * HEADS * W), 1) // W
    expand = jnp.where(er == jnp.where(ec < HEADS, LANE_FOX + ec, LANE_GDEC + ec - HEADS), 1.0, 0.0).astype(BF16)
    is_fox = (lane >= LANE_FOX) & (lane < LANE_FOX + HEADS)
    is_beta = (lane >= LANE_BETA) & (lane < LANE_BETA + HEADS)
    is_gdec = (lane >= LANE_GDEC) & (lane < LANE_GDEC + HEADS)
    r = _iota((blk, blk), 0)
    c = _iota((blk, blk), 1)
    tril_all = jnp.where(r >= c, 1.0, 0.0).astype(BF16)
    tril_chunk = jnp.where((r >= c) & (r // GDN_CHUNK == c // GDN_CHUNK), 1.0, 0.0).astype(BF16)
    neg_exp_a = -jnp.exp(p_ref[1:2, :])
    carry = jnp.zeros((1, W), F32)
    for i in range(S // blk):
        sl = pl.ds(i * blk, blk)
        z = s_ref[0, sl, :] + p_ref[0:1, :]
        fox = jnp.where(is_fox, _log_sigmoid(z), 0.0)
        beta = jnp.where(is_beta, _sigmoid(z), 0.0)
        gdec = jnp.where(is_gdec, neg_exp_a * _softplus(z), 0.0)
        cf = _dot_exact_lhs(tril_all, fox) + carry
        cg = _dot_exact_lhs(tril_chunk, gdec)
        carry = cf[blk - 1:blk, :]
        o_ref[0, sl, :] = cf + cg + beta
        x_ref[0, sl, :] = _dot_exact_rhs(cf + cg, expand)


def small_prep(proj3, params):
    b, s, _ = proj3.shape
    w = SMALL_W
    return pl.pallas_call(
        _small_prep_kernel,
        grid=(b,),
        in_specs=[pl.BlockSpec((1, s, w), lambda i: (i, 0, COL_SMALL)),
                  pl.BlockSpec((8, w), lambda i: (0, 0))],
        out_specs=[pl.BlockSpec((1, s, w), lambda i: (i, 0, 0)),
                   pl.BlockSpec((1, s, 2 * HEADS * w), lambda i: (i, 0, 0))],
        out_shape=[jax.ShapeDtypeStruct((b, s, w), F32), jax.ShapeDtypeStruct((b, s, 2 * HEADS * w), F32)],
        compiler_params=_cparams("parallel"),
        name="small_prep",
    )(proj3, params)


def _hgrn2_kernel(q_ref, f_ref, i_ref, g_ref, lb_ref, o_ref, st_ref, qs, ks, vs, bs, os_):
    nb = q_ref.shape[0]
    T = q_ref.shape[1]
    C = HGRN_CHUNK
    seqs = range(nb)

    @pl.when(pl.program_id(1) == 0)
    def _():
        st_ref[...] = jnp.zeros_like(st_ref)

    same_head = _head_ones()
    ones_bf = jnp.where(same_head, 1.0, 0.0).astype(BF16)
    lb = lb_ref[...]
    log_lb = jnp.log(lb)
    r = _iota((T, T), 0)
    c = _iota((T, T), 1)
    tril_chunk = jnp.where((r >= c) & (r // C == c // C), 1.0, 0.0).astype(BF16)
    for bb in seqs:
        fl = f_ref[bb]
        c2 = jnp.log1p(-lb) + _log_sigmoid(fl)
        mx = jnp.maximum(log_lb, c2)
        log_f = mx + jnp.log1p(jnp.exp(-jnp.abs(log_lb - c2)))
        bs[bb] = _dot_exact_lhs(tril_chunk, log_f)
        qs[bb] = _silu(q_ref[bb])
        ks[bb] = (1.0 - lb) * _sigmoid(-fl)
        vs[bb] = i_ref[bb]

    trow = _iota((C, GROUP_WIDTH), 0)

    def chunk(ci, carry):
        r0 = pl.multiple_of(ci * C, C)
        qc = [qs[bb, pl.ds(r0, C), :] for bb in seqs]
        kc = [ks[bb, pl.ds(r0, C), :] for bb in seqs]
        vc = [vs[bb, pl.ds(r0, C), :] for bb in seqs]
        bc = [bs[bb, pl.ds(r0, C), :] for bb in seqs]
        st = [st_ref[bb] for bb in seqs]
        a = []
        for bb in seqs:
            parts = []
            for s in range(C):
                parts.append(jnp.where(trow >= s, jnp.exp(bc[bb] - bc[bb][s:s + 1, :]) * (qc[bb] * kc[bb][s:s + 1, :]), 0.0))
            a.append(jnp.concatenate(parts, axis=0).astype(BF16))
        sc = [_dot(a[bb], ones_bf) for bb in seqs]
        o = [_dot_nt((qc[bb] * jnp.exp(bc[bb])).astype(BF16), st[bb].astype(BF16)) for bb in seqs]
        b_last = [bc[bb][C - 1:C, :] for bb in seqs]
        upd = [_dot_tn(vc[bb].astype(BF16), (kc[bb] * jnp.exp(b_last[bb] - bc[bb])).astype(BF16)) for bb in seqs]
        for bb in seqs:
            acc = o[bb]
            for s in range(C):
                acc = acc + sc[bb][s * C:(s + 1) * C, :] * vc[bb][s:s + 1, :]
            os_[bb, pl.ds(r0, C), :] = acc
            st_ref[bb] = st[bb] * jnp.exp(b_last[bb]) + jnp.where(same_head, upd[bb], 0.0)
        return carry

    lax.fori_loop(0, T // C, chunk, 0)
    for bb in seqs:
        o_ref[bb] = _head_rms(os_[bb], ones_bf) * _silu(g_ref[bb])


def hgrn2(proj3, lb):
    b, s, _ = proj3.shape
    t = SEQ_TILE
    gw = GROUP_WIDTH
    nb = HGRN_BATCH
    col = lambda cidx: pl.BlockSpec((nb, t, gw), lambda i, j: (i, j, cidx))
    return pl.pallas_call(
        _hgrn2_kernel,
        grid=(b // nb, s // t),
        in_specs=[col(COL_AQ), col(COL_AF), col(COL_AI), col(COL_AG),
                  pl.BlockSpec((1, gw), lambda i, j: (0, 0))],
        out_specs=pl.BlockSpec((nb, t, gw), lambda i, j: (i, j, 0)),
        out_shape=jax.ShapeDtypeStruct((b, s, gw), F32),
        scratch_shapes=[pltpu.VMEM((nb, gw, gw), F32)] + [pltpu.VMEM((nb, t, gw), F32)] * 5,
        compiler_params=_cparams("parallel", "arbitrary"),
        name="hgrn2",
    )(proj3, proj3, proj3, proj3, lb.reshape(1, gw))


def _fox_kernel(q_ref, k_ref, v_ref, cr_ref, cx_ref, o_ref, kb_ref, vt_ref, acc_ref):
    nb = q_ref.shape[0]
    tq = q_ref.shape[1]
    tk = tq
    S = k_ref.shape[1]
    n = HEADS * tq
    qi = pl.program_id(1)
    seqs = range(nb)

    @pl.when(qi == 0)
    def _():
        for bb in seqs:
            for i in range(S // tk):
                sl = pl.ds(i * tk, tk)
                kb_ref[bb, sl, :] = k_ref[bb, sl, :].astype(BF16)
                vt_ref[bb, :, sl] = v_ref[bb, sl, :].T.astype(BF16)

    q0 = pl.multiple_of(qi * tq, tq)
    qs, cq_row = [], []
    for bb in seqs:
        qs.append(_stack_heads(q_ref[bb] * (HEAD_DIM ** -0.5)).astype(BF16))
        cr = cr_ref[bb, :, pl.ds(q0, tq)]
        cq_row.append(jnp.concatenate([cr[h:h + 1, :] for h in range(HEADS)], axis=1))
    acc_ref[...] = jnp.zeros_like(acc_ref)
    reps = tq // SMALL_W

    def step(kb, stats, masked):
        k0 = pl.multiple_of(kb * tk, tk)
        st = [_dot_nt(kb_ref[bb, pl.ds(k0, tk), :], qs[bb]) for bb in seqs]
        for bb in seqs:
            cx = cx_ref[bb, pl.ds(k0, tk), :]
            ck = jnp.concatenate([cx[:, h * SMALL_W:(h + 1) * SMALL_W] for h in range(HEADS) for _ in range(reps)], axis=1)
            st[bb] = st[bb] + (cq_row[bb] - ck)
            if masked:
                st[bb] = jnp.where(_iota((tk, n), 0) <= _iota((tk, n), 1) % tq, st[bb], -jnp.inf)
        m_new = [jnp.maximum(stats[bb][0], jnp.max(st[bb], axis=0, keepdims=True)) for bb in seqs]
        alpha = [jnp.exp(stats[bb][0] - m_new[bb]) for bb in seqs]
        p = [jnp.exp(st[bb] - m_new[bb]) for bb in seqs]
        l_new = [alpha[bb] * stats[bb][1] + jnp.sum(p[bb], axis=0, keepdims=True) for bb in seqs]
        pv = [_dot(vt_ref[bb, :, pl.ds(k0, tk)], p[bb].astype(BF16)) for bb in seqs]
        for bb in seqs:
            acc_ref[bb] = alpha[bb] * acc_ref[bb] + pv[bb]
        return tuple((m_new[bb], l_new[bb]) for bb in seqs)

    init = tuple((jnp.full((1, n), -jnp.inf, F32), jnp.zeros((1, n), F32)) for _ in seqs)
    stats = lax.fori_loop(0, qi, lambda kb, c: step(kb, c, False), init)
    stats = step(qi, stats, True)

    ones_bf = jnp.where(_head_ones(), 1.0, 0.0).astype(BF16)
    for bb in seqs:
        acc = acc_ref[bb] * (1.0 / stats[bb][1])
        ot = jnp.concatenate([acc[h * HEAD_DIM:(h + 1) * HEAD_DIM, h * tq:(h + 1) * tq] for h in range(HEADS)], axis=0)
        o_ref[bb] = _head_rms(ot.T, ones_bf)


def fox_attention(proj3, c_row, c_exp):
    b, s, _ = proj3.shape
    tq = SEQ_TILE
    gw = GROUP_WIDTH
    nb = FOX_BATCH
    return pl.pallas_call(
        _fox_kernel,
        grid=(b // nb, s // tq),
        in_specs=[pl.BlockSpec((nb, tq, gw), lambda i, j: (i, j, COL_BQ)),
                  pl.BlockSpec((nb, s, gw), lambda i, j: (i, 0, COL_BK)),
                  pl.BlockSpec((nb, s, gw), lambda i, j: (i, 0, COL_BV)),
                  pl.BlockSpec((nb, 8, s), lambda i, j: (i, 0, 0)),
                  pl.BlockSpec((nb, s, HEADS * SMALL_W), lambda i, j: (i, 0, 0))],
        out_specs=pl.BlockSpec((nb, tq, gw), lambda i, j: (i, j, 0)),
        out_shape=jax.ShapeDtypeStruct((b, s, gw), F32),
        scratch_shapes=[pltpu.VMEM((nb, s, gw), BF16),
                        pltpu.VMEM((nb, gw, s), BF16),
                        pltpu.VMEM((nb, gw, HEADS * tq), F32)],
        compiler_params=_cparams("parallel", "arbitrary"),
        name="fox_attention",
    )(proj3, proj3, proj3, c_row, c_exp)


def _rglru_kernel(x_ref, g_ref, cw_ref, cb_ref, wa_ref, ba_ref, wx_ref, bx_ref, lam_ref, o_ref, prev_ref, h_ref):
    @pl.when(pl.program_id(1) == 0)
    def _():
        prev_ref[...] = jnp.zeros_like(prev_ref)
        h_ref[...] = jnp.zeros_like(h_ref)

    x = x_ref[0]
    t = x.shape[0]
    xc = _causal_conv4(x, prev_ref[...], cw_ref[...]) + cb_ref[...]
    prev_ref[...] = x[t - 8:t, :]
    xb = xc.astype(BF16)
    r = _sigmoid(_dot(xb, wa_ref[...]) + ba_ref[...])
    ig = _sigmoid(_dot(xb, wx_ref[...]) + bx_ref[...])
    log_a = (-LRU_C * r) * _softplus(-lam_ref[...])
    a = jnp.exp(log_a)
    u = jnp.sqrt(1.0 - jnp.exp(2.0 * log_a)) * (ig * xc)
    row = _iota(a.shape, 0)
    d = 1
    while d < t:
        valid = row >= d
        u = jnp.where(valid, a * pltpu.roll(u, d, 0) + u, u)
        a = jnp.where(valid, a * pltpu.roll(a, d, 0), a)
        d *= 2
    h = a * h_ref[...] + u
    h_ref[...] = h[t - 1:t, :]
    ones_bf = jnp.where(_head_ones(), 1.0, 0.0).astype(BF16)
    o_ref[0] = _head_rms(h * _gelu_tanh(g_ref[0]), ones_bf)


def rglru(proj3, conv_w, conv_b, wa_bd, ba, wx_bd, bx, lam):
    b, s, _ = proj3.shape
    t = SEQ_TILE
    gw = GROUP_WIDTH
    row = lambda: pl.BlockSpec((1, gw), lambda i, j: (0, 0))
    return pl.pallas_call(
        _rglru_kernel,
        grid=(b, s // t),
        in_specs=[pl.BlockSpec((1, t, gw), lambda i, j: (i, j, COL_CX)),
                  pl.BlockSpec((1, t, gw), lambda i, j: (i, j, COL_CG)),
                  pl.BlockSpec((CONV_W, gw), lambda i, j: (0, 0)), row(),
                  pl.BlockSpec((gw, gw), lambda i, j: (0, 0)), row(),
                  pl.BlockSpec((gw, gw), lambda i, j: (0, 0)), row(), row()],
        out_specs=pl.BlockSpec((1, t, gw), lambda i, j: (i, j, 0)),
        out_shape=jax.ShapeDtypeStruct((b, s, gw), F32),
        scratch_shapes=[pltpu.VMEM((8, gw), F32), pltpu.VMEM((1, gw), F32)],
        compiler_params=_cparams("parallel", "arbitrary"),
        name="rglru",
    )(proj3, proj3, conv_w, conv_b.reshape(1, gw), wa_bd, ba.reshape(1, gw), wx_bd, bx.reshape(1, gw), lam.reshape(1, gw))


def _gdn_kernel(q_ref, k_ref, v_ref, z_ref, sm_ref, ge_ref, gr_ref, cw_ref, o_ref, st_ref, pq_ref, pk_ref, pv_ref):
    nb = q_ref.shape[0]
    T = q_ref.shape[1]
    C = GDN_CHUNK

    @pl.when(pl.program_id(1) == 0)
    def _():
        st_ref[...] = jnp.zeros_like(st_ref)
        pq_ref[...] = jnp.zeros_like(pq_ref)
        pk_ref[...] = jnp.zeros_like(pk_ref)
        pv_ref[...] = jnp.zeros_like(pv_ref)

    same_head = _head_ones()
    ones_bf = jnp.where(same_head, 1.0, 0.0).astype(BF16)
    cw = cw_ref[...]
    gw = GROUP_WIDTH
    er = _iota((SMALL_W, gw), 0)
    ec = _iota((SMALL_W, gw), 1) // HEAD_DIM
    exp_beta = jnp.where(er == LANE_BETA + ec, 1.0, 0.0).astype(BF16)
    exp_g = jnp.where(er == LANE_GDEC + ec, 1.0, 0.0).astype(BF16)
    n = HEADS * C
    rr = _iota((n, n), 0)
    cc = _iota((n, n), 1)
    incl = same_head & (rr % C >= cc % C)
    strict = same_head & (rr % C > cc % C)

    prep = []
    for bb in range(nb):
        xq, xk, xv = q_ref[bb], k_ref[bb], v_ref[bb]
        q = _silu(_causal_conv4(xq, pq_ref[bb], cw[:, 0:gw]))
        k = _silu(_causal_conv4(xk, pk_ref[bb], cw[:, gw:2 * gw]))
        v = _silu(_causal_conv4(xv, pv_ref[bb], cw[:, 2 * gw:3 * gw]))
        pq_ref[bb] = xq[T - 8:T, :]
        pk_ref[bb] = xk[T - 8:T, :]
        pv_ref[bb] = xv[T - 8:T, :]
        q = q * lax.rsqrt(_dot_exact_rhs(q * q, ones_bf) + EPS) * (HEAD_DIM ** -0.5)
        k = k * lax.rsqrt(_dot_exact_rhs(k * k, ones_bf) + EPS)
        sm = sm_ref[bb]
        betax = _dot_exact_rhs(sm, exp_beta)
        gx = _dot_exact_rhs(sm, exp_g)
        prep.append((q, k, v, betax, gx))

    reps = gw // SMALL_W
    n_chunks = T // C
    eye = jnp.where(rr == cc, 1.0, 0.0)

    chains = []
    for ci in range(n_chunks):
        sl = slice(ci * C, (ci + 1) * C)
        for bb in range(nb):
            q, k, v, betax, gx = prep[bb]
            qc, kc, vc, bx, gc = q[sl], k[sl], v[sl], betax[sl], gx[sl]
            grow = gr_ref[bb, ci, 0:1, :]
            ge = ge_ref[bb, sl, :]
            gcol = jnp.concatenate(
                [jnp.concatenate([ge[:, h * SMALL_W:(h + 1) * SMALL_W]] * reps, axis=1) for h in range(HEADS)], axis=0)
            eg = jnp.exp(gc)
            kb = kc * bx
            g_last = gc[C - 1:C, :]
            gamma = jnp.where(incl, jnp.exp(jnp.where(incl, gcol - grow, 0.0)), 0.0)
            ks_bf = _stack_heads(kc).astype(BF16)
            m = jnp.where(strict, _dot_nt(_stack_heads(kb).astype(BF16), ks_bf) * gamma, 0.0)
            chains.append(dict(
                x=eye - m, y=m,
                a_qk=(_dot_nt(_stack_heads(qc).astype(BF16), ks_bf) * gamma).astype(BF16),
                rhs_u=_stack_heads(vc * bx).astype(BF16),
                rhs_w=_stack_heads(kb * eg).astype(BF16),
                q_dec=_stack_heads(qc * eg).astype(BF16),
                k_dec=_stack_heads(kc * jnp.exp(g_last - gc)).astype(BF16),
                decay=jnp.exp(g_last)))
    for _ in range(5):
        for ch in chains:
            yb = ch["y"].astype(BF16)
            ch["y"] = _dot(yb, yb)
        for ch in chains:
            ch["x"] = ch["x"] + _dot(ch["x"].astype(BF16), ch["y"].astype(BF16))
    for ch in chains:
        t_inv = ch["x"].astype(BF16)
        ch["u"] = _dot(t_inv, ch["rhs_u"])
        ch["wk"] = _dot(t_inv, ch["rhs_w"]).astype(BF16)

    outs = [[] for _ in range(nb)]
    for ci in range(n_chunks):
        for bb in range(nb):
            ch = chains[ci * nb + bb]
            st = st_ref[bb]
            st_bf = st.astype(BF16)
            v_new = (ch["u"] - _dot(ch["wk"], st_bf)).astype(BF16)
            o = _dot(ch["q_dec"], st_bf) + _dot(ch["a_qk"], v_new)
            outs[bb].append(_unstack_heads(o, C))
            st_ref[bb] = st * ch["decay"] + _dot_tn(ch["k_dec"], v_new)
    for bb in range(nb):
        o = jnp.concatenate(outs[bb], axis=0)
        o_ref[bb] = _head_rms(o, ones_bf) * _silu(z_ref[bb])


def gdn(proj3, sm, c_exp, g_row, conv_w):
    b, s, _ = proj3.shape
    t = SEQ_TILE
    gw = GROUP_WIDTH
    nb = GDN_BATCH
    col = lambda cidx: pl.BlockSpec((nb, t, gw), lambda i, j: (i, j, cidx))
    return pl.pallas_call(
        _gdn_kernel,
        grid=(b // nb, s // t),
        in_specs=[col(COL_DQ), col(COL_DK), col(COL_DV), col(COL_DZ),
                  pl.BlockSpec((nb, t, SMALL_W), lambda i, j: (i, j, 0)),
                  pl.BlockSpec((nb, t, HEADS * SMALL_W), lambda i, j: (i, j, 1)),
                  pl.BlockSpec((nb, t // GDN_CHUNK, 8, gw), lambda i, j: (i, j, 0, 0)),
                  pl.BlockSpec((CONV_W, 3 * gw), lambda i, j: (0, 0))],
        out_specs=pl.BlockSpec((nb, t, gw), lambda i, j: (i, j, 0)),
        out_shape=jax.ShapeDtypeStruct((b, s, gw), F32),
        scratch_shapes=[pltpu.VMEM((nb, gw, gw), F32)] + [pltpu.VMEM((nb, 8, gw), F32)] * 3,
        compiler_params=_cparams("parallel", "arbitrary"),
        name="gdn",
    )(proj3, proj3, proj3, proj3, sm, c_exp, g_row, conv_w)


def _mix_out_kernel(x_ref, ya_ref, yb_ref, yc_ref, yd_ref, gain_ref, w_ref, o_ref):
    gw = GROUP_WIDTH
    gain = gain_ref[...]
    acc = x_ref[0]
    for i, y_ref in enumerate((ya_ref, yb_ref, yc_ref, yd_ref)):
        yg = (y_ref[0] * gain[:, i * gw:(i + 1) * gw]).astype(BF16)
        acc = acc + _dot(yg, w_ref[i * gw:(i + 1) * gw, :])
    o_ref[0] = acc


def mix_out(x3, ya, yb, yc, yd, gain, w_out_bf16):
    b, s, d = x3.shape
    t = SEQ_TILE
    gw = GROUP_WIDTH
    grp = lambda: pl.BlockSpec((1, t, gw), lambda i, j: (i, j, 0))
    return pl.pallas_call(
        _mix_out_kernel,
        grid=(b, s // t),
        in_specs=[pl.BlockSpec((1, t, d), lambda i, j: (i, j, 0)),
                  grp(), grp(), grp(), grp(),
                  pl.BlockSpec((1, 4 * gw), lambda i, j: (0, 0)),
                  pl.BlockSpec((4 * gw, d), lambda i, j: (0, 0))],
        out_specs=pl.BlockSpec((1, t, d), lambda i, j: (i, j, 0)),
        out_shape=jax.ShapeDtypeStruct((b, s, d), F32),
        compiler_params=_cparams("parallel", "parallel"),
        name="mix_out",
    )(x3, ya, yb, yc, yd, gain.reshape(1, 4 * gw), w_out_bf16)


def _mem_attn_kernel(x_ref, g_ref, wq_ref, kv_ref, wo_ref, o_ref):
    x = x_ref[0]
    t = x.shape[0]
    gw = GROUP_WIDTH
    ms = jnp.mean(x * x, axis=-1, keepdims=True)
    h = ((x * lax.rsqrt(ms + EPS)) * g_ref[...]).astype(BF16)
    q = _dot(h, wq_ref[...])
    kv = kv_ref[0]
    k = kv[:, 0:gw].astype(BF16)
    v = kv[:, gw:2 * gw].astype(BF16)
    s = _dot_nt(k, _stack_heads(q).astype(BF16)) * (HEAD_DIM ** -0.5)
    s = s - jnp.max(s, axis=0, keepdims=True)
    p = jnp.exp(s)
    p = p * (1.0 / jnp.sum(p, axis=0, keepdims=True))
    ot = _dot_tn(v, p.astype(BF16))
    oc = jnp.concatenate([ot[hh * HEAD_DIM:(hh + 1) * HEAD_DIM, hh * t:(hh + 1) * t] for hh in range(HEADS)], axis=0).T
    o_ref[0] = x + _dot(oc.astype(BF16), wo_ref[...])


def mem_attention(x3, g, wq_bf16, kv3, wo_bf16):
    b, s, d = x3.shape
    t = SEQ_TILE
    m = kv3.shape[1]
    return pl.pallas_call(
        _mem_attn_kernel,
        grid=(b, s // t),
        in_specs=[pl.BlockSpec((1, t, d), lambda i, j: (i, j, 0)),
                  pl.BlockSpec((1, d), lambda i, j: (0, 0)),
                  pl.BlockSpec((d, GROUP_WIDTH), lambda i, j: (0, 0)),
                  pl.BlockSpec((1, m, 2 * GROUP_WIDTH), lambda i, j: (i, 0, 0)),
                  pl.BlockSpec((GROUP_WIDTH, d), lambda i, j: (0, 0))],
        out_specs=pl.BlockSpec((1, t, d), lambda i, j: (i, j, 0)),
        out_shape=jax.ShapeDtypeStruct((b, s, d), F32),
        compiler_params=_cparams("parallel", "parallel"),
        name="mem_attention",
    )(x3, g.reshape(1, d), wq_bf16, kv3, wo_bf16)


def _router_kernel(x_ref, g_ref, w_ref, b_ref, h_ref, r_ref, n_ref, cnt_ref):
    @pl.when(pl.program_id(0) == 0)
    def _():
        cnt_ref[...] = jnp.zeros_like(cnt_ref)

    x = x_ref[...]
    tm = x.shape[0]
    ms = jnp.mean(x * x, axis=-1, keepdims=True)
    h = (x * lax.rsqrt(ms + EPS)) * g_ref[...]
    h_ref[...] = _pack_bf16_pairs(h)
    hh, hm, _ = _split3(h)
    w = w_ref[...]
    wh, wm, _ = _split3(w)
    logits = _dot(hh, wh) + _dot(hh, wm) + _dot(hm, wh) + b_ref[...]
    lane = _iota(logits.shape, 1)
    big = jnp.int32(1 << 30)
    neg = -jnp.inf
    is_grp = (lane >= N_EXPERTS) & (lane < N_EXPERTS + N_EXPERT_GROUPS)
    gl = jnp.where(is_grp, logits, neg)
    gmax = jnp.max(gl, axis=-1, keepdims=True)
    p_grp = 1.0 / jnp.sum(jnp.exp(gl - gmax), axis=-1, keepdims=True)
    g_sel = jnp.min(jnp.where(gl == gmax, lane, big), axis=-1, keepdims=True) - N_EXPERTS
    in_grp = (lane < N_EXPERTS) & (lane // EXPERTS_PER_GROUP == g_sel)
    el = jnp.where(in_grp, logits, neg)
    m1 = jnp.max(el, axis=-1, keepdims=True)
    i1 = jnp.min(jnp.where(el == m1, lane, big), axis=-1, keepdims=True)
    el2 = jnp.where(lane == i1, neg, el)
    m2 = jnp.max(el2, axis=-1, keepdims=True)
    i2 = jnp.min(jnp.where(el2 == m2, lane, big), axis=-1, keepdims=True)
    e21 = jnp.exp(m2 - m1)
    w1 = 1.0 / (1.0 + e21)
    w2 = e21 / (1.0 + e21)
    oh1 = lane == i1
    oh2 = lane == i2
    onehot = jnp.where(oh1 | oh2, 1.0, 0.0)
    strict = jnp.where(_iota((tm, tm), 0) > _iota((tm, tm), 1), 1.0, 0.0).astype(BF16)
    before = _dot(strict, onehot.astype(BF16)) + cnt_ref[...]
    rank1 = jnp.sum(jnp.where(oh1, before, 0.0), axis=-1, keepdims=True)
    rank2 = jnp.sum(jnp.where(oh2, before, 0.0), axis=-1, keepdims=True)
    cnt = before[tm - 1:tm, :] + onehot[tm - 1:tm, :]
    cnt_ref[...] = cnt
    n_ref[...] = jnp.broadcast_to(cnt, n_ref.shape)
    cols = (i1.astype(F32), i2.astype(F32), p_grp * w1, p_grp * w2, rank1, rank2)
    route = jnp.zeros(logits.shape, F32)
    for j, col in enumerate(cols):
        route = jnp.where(lane == j, col, route)
    r_ref[...] = route


def router(x2, g, w_router, b_router, tm):
    m, d = x2.shape
    return pl.pallas_call(
        _router_kernel,
        grid=(m // tm,),
        in_specs=[pl.BlockSpec((tm, d), lambda i: (i, 0)),
                  pl.BlockSpec((1, d), lambda i: (0, 0)),
                  pl.BlockSpec((d, SMALL_W), lambda i: (0, 0)),
                  pl.BlockSpec((1, SMALL_W), lambda i: (0, 0))],
        out_specs=[pl.BlockSpec((tm, d // 2), lambda i: (i, 0)),
                   pl.BlockSpec((tm, SMALL_W), lambda i: (i, 0)),
                   pl.BlockSpec((8, SMALL_W), lambda i: (0, 0))],
        out_shape=[jax.ShapeDtypeStruct((m, d // 2), jnp.uint32), jax.ShapeDtypeStruct((m, SMALL_W), F32),
                   jax.ShapeDtypeStruct((8, SMALL_W), F32)],
        scratch_shapes=[pltpu.VMEM((1, SMALL_W), F32)],
        compiler_params=_cparams("arbitrary"),
        name="router",
    )(x2, g.reshape(1, d), w_router, b_router)


def _experts_kernel(te_ref, nu_ref, xs_ref, wgu_ref, wdn_ref, y_ref, wgu_bf, wdn_bf):
    i = pl.program_id(0)

    @pl.when(i >= nu_ref[0])
    def _():
        y_ref[...] = jnp.zeros_like(y_ref)

    @pl.when(i < nu_ref[0])
    def _():
        prev = te_ref[jnp.maximum(i - 1, 0)]

        @pl.when((i == 0) | (te_ref[i] != prev))
        def _():
            wgu_bf[...] = wgu_ref[0, 0].astype(BF16)
            wdn_bf[...] = wdn_ref[0, 0].astype(BF16)

        gu = _dot(_unpack_bf16_pairs(xs_ref[...]).astype(BF16), wgu_bf[...])
        act = _silu(gu[:, 0:D_EXPERT]) * gu[:, D_EXPERT:2 * D_EXPERT]
        y_ref[...] = _pack_bf16_pairs(_dot(act.astype(BF16), wdn_bf[...]))


def moe_experts(xs, tile_expert, n_used, w_gu, w_dn, layer, tm):
    n_rows, dp = xs.shape
    d = 2 * dp
    tile = lambda i, te, nu: (jnp.minimum(i, nu[0] - 1), 0)
    wsel = lambda i, te, nu: (layer, te[jnp.minimum(i, nu[0] - 1)], 0, 0)
    return pl.pallas_call(
        _experts_kernel,
        grid_spec=pltpu.PrefetchScalarGridSpec(
            num_scalar_prefetch=2,
            grid=(n_rows // tm,),
            in_specs=[pl.BlockSpec((tm, dp), tile),
                      pl.BlockSpec((1, 1, d, 2 * D_EXPERT), wsel),
                      pl.BlockSpec((1, 1, D_EXPERT, d), wsel)],
            out_specs=pl.BlockSpec((tm, dp), lambda i, te, nu: (i, 0)),
            scratch_shapes=[pltpu.VMEM((d, 2 * D_EXPERT), BF16), pltpu.VMEM((D_EXPERT, d), BF16)]),
        out_shape=jax.ShapeDtypeStruct((n_rows, dp), jnp.uint32),
        compiler_params=_cparams("arbitrary"),
        name="moe_experts",
    )(tile_expert, n_used, xs, w_gu, w_dn)


def sc_gather_rows(table, idx):
    n_idx = idx.shape[0]
    width = table.shape[1]
    info = plsc.get_sparse_core_info()
    n_workers = info.num_cores * info.num_subcores
    per_worker = n_idx // n_workers
    chunk = SC_GATHER_CHUNK
    assert n_idx % (n_workers * chunk) == 0 and width % info.num_lanes == 0
    mesh = plsc.VectorSubcoreMesh(core_axis_name="c", subcore_axis_name="s")

    def body(table_hbm, idx_hbm, out_hbm, idx_v, rows_v, sem):
        wid = lax.axis_index("s") * info.num_cores + lax.axis_index("c")
        base = wid * per_worker

        @pl.loop(0, per_worker // chunk)
        def _(j):
            off = pl.multiple_of(base + j * chunk, chunk)
            pltpu.sync_copy(idx_hbm.at[pl.ds(off, chunk)], idx_v)
            pltpu.async_copy(table_hbm.at[idx_v], rows_v, sem).wait()
            pltpu.sync_copy(rows_v, out_hbm.at[pl.ds(off, chunk)])

    return pl.kernel(
        body,
        out_type=jax.ShapeDtypeStruct((n_idx, width), table.dtype),
        mesh=mesh,
        scratch_types=[pltpu.VMEM((chunk,), jnp.int32), pltpu.VMEM((chunk, width), table.dtype),
                       pltpu.SemaphoreType.DMA],
        name="sc_gather_rows",
    )(table, idx)


def _combine_rows_kernel(x_ref, r_ref, g_ref, y_ref, o_ref, *, final_norm):
    r = r_ref[...]
    y = y_ref[...]
    half = y.shape[1] // 2
    out = x_ref[...] + r[:, 2:3] * _unpack_bf16_pairs(y[:, :half]) + r[:, 3:4] * _unpack_bf16_pairs(y[:, half:])
    if final_norm:
        ms = jnp.mean(out * out, axis=-1, keepdims=True)
        out = (out * lax.rsqrt(ms + EPS)) * g_ref[...]
    o_ref[...] = out


def moe_combine_rows(x2, route, y_pairs, g_final, tm, final_norm):
    m, d = x2.shape
    return pl.pallas_call(
        functools.partial(_combine_rows_kernel, final_norm=final_norm),
        grid=(m // tm,),
        in_specs=[pl.BlockSpec((tm, d), lambda i: (i, 0)),
                  pl.BlockSpec((tm, SMALL_W), lambda i: (i, 0)),
                  pl.BlockSpec((1, d), lambda i: (0, 0)),
                  pl.BlockSpec((tm, d), lambda i: (i, 0))],
        out_specs=pl.BlockSpec((tm, d), lambda i: (i, 0)),
        out_shape=jax.ShapeDtypeStruct((m, d), F32),
        compiler_params=_cparams("parallel"),
        name="moe_combine_rows",
    )(x2, route, g_final.reshape(1, d), y_pairs)


def hier_moe_routed(x2, g_ffn, w_router, b_router, w_gu, w_dn, layer, g_final, final_norm):
    m, d = x2.shape
    te_rows = EXPERT_TILE
    n_rows = 2 * m + N_EXPERTS * te_rows
    h, route, counts = router(x2, g_ffn, w_router, b_router, tm=512)
    cnt = counts[0, 0:N_EXPERTS].astype(jnp.int32)
    padded = (cnt + te_rows - 1) // te_rows * te_rows
    seg_end = jnp.cumsum(padded)
    seg_start = seg_end - padded
    idx = route[:, 0:2].astype(jnp.int32)
    onehot = idx[:, :, None] == jnp.arange(N_EXPERTS, dtype=jnp.int32)
    dest = (jnp.sum(jnp.where(onehot, seg_start, 0), axis=-1) + route[:, 4:6].astype(jnp.int32)).reshape(2 * m)
    tile_start = jnp.arange(n_rows // te_rows, dtype=jnp.int32) * te_rows
    tile_expert = jnp.minimum(jnp.sum((tile_start[:, None] >= seg_end[None, :]).astype(jnp.int32), axis=1), N_EXPERTS - 1)
    n_used = (seg_end[N_EXPERTS - 1] // te_rows).reshape(1)
    pair_token = jnp.arange(2 * m, dtype=jnp.int32) // 2
    src = jnp.zeros((n_rows,), jnp.int32).at[dest].set(pair_token, unique_indices=True)
    xs = sc_gather_rows(h, src)
    ys = moe_experts(xs, tile_expert, n_used, w_gu, w_dn, layer, tm=te_rows)
    y_pairs = sc_gather_rows(ys, dest).reshape(m, d)
    return moe_combine_rows(x2, route, y_pairs, g_final, tm=256, final_norm=final_norm)


def _w_in_prep_kernel(w_ref, o_ref):
    gw = GROUP_WIDTH
    a_end = 7 * gw
    c_start = a_end + HEADS
    c_end = c_start + 5 * gw
    z_start = c_end + 2 * HEADS
    assert a_end % SMALL_W == LANE_FOX and c_end % SMALL_W == LANE_BETA and LANE_GDEC == LANE_BETA + HEADS
    w = w_ref[...]
    o_ref[:, 0:a_end] = w[:, 0:a_end].astype(BF16)
    o_ref[:, a_end:a_end + 5 * gw] = w[:, c_start:c_end].astype(BF16)
    o_ref[:, 12 * gw:13 * gw] = w[:, z_start:z_start + gw].astype(BF16)
    blk_fox = w[:, a_end:a_end + SMALL_W]
    blk_gdn = w[:, c_end - LANE_BETA:c_end - LANE_BETA + SMALL_W]
    lane = _iota(blk_fox.shape, 1)
    small = jnp.where(lane < LANE_BETA, blk_fox, jnp.where(lane < LANE_GDEC + HEADS, blk_gdn, 0.0))
    o_ref[:, 13 * gw:13 * gw + SMALL_W] = small.astype(BF16)
    o_ref[:, 13 * gw + SMALL_W:] = jnp.zeros((w.shape[0], SMALL_W), BF16)


def w_in_prep(w_all, layer):
    _, d, n = w_all.shape
    tr = 128
    return pl.pallas_call(
        _w_in_prep_kernel,
        grid=(d // tr,),
        in_specs=[pl.BlockSpec((None, tr, n), lambda i: (layer, i, 0))],
        out_specs=pl.BlockSpec((tr, N_PROJ), lambda i: (i, 0)),
        out_shape=jax.ShapeDtypeStruct((d, N_PROJ), BF16),
        compiler_params=_cparams("parallel"),
        name="w_in_prep",
    )(w_all)


def _block_diag(w):
    h = w.shape[0]
    eye = jnp.eye(h, dtype=w.dtype)
    return (eye[:, None, :, None] * w[:, :, None, :]).reshape(h * HEAD_DIM, h * HEAD_DIM)


def kernel(x, mem, norm_mix, w_in, hgrn_lb, fox_fb, lru_conv_w, lru_conv_b, lru_wa, lru_ba, lru_wx, lru_bx, lru_lam, gdn_conv_w, gdn_a_log, gdn_dt_bias, mix_gain, w_out, norm_mem, norm_memkv, w_mq, w_mkv, w_mo, norm_ffn, w_rg, b_rg, w_re, b_re, w_e_gu, w_e_dn, norm_final):
    b, s, d = x.shape
    depth = w_in.shape[0]
    t_tok = b * s
    mlen = mem.shape[1]
    gw = GROUP_WIDTH

    lb_all = jnp.cumsum(jax.nn.softmax(hgrn_lb.astype(F32), axis=0), axis=0)
    lb_all = lb_all - lb_all[0]

    x = x.astype(F32)
    for l in range(depth):
        proj = norm_matmul(x.reshape(t_tok, d), norm_mix[l], w_in_prep(w_in, l), tm=256)
        proj3 = proj.reshape(b, s, N_PROJ)
        prm = jnp.zeros((8, SMALL_W), F32)
        prm = prm.at[0, LANE_FOX:LANE_FOX + HEADS].set(fox_fb[l].astype(F32))
        prm = prm.at[0, LANE_GDEC:LANE_GDEC + HEADS].set(gdn_dt_bias[l].astype(F32))
        prm = prm.at[1, LANE_GDEC:LANE_GDEC + HEADS].set(gdn_a_log[l].astype(F32))
        sm, c_exp = small_prep(proj3, prm)
        c_row = jnp.pad(jnp.swapaxes(sm[:, :, LANE_FOX:LANE_FOX + HEADS], 1, 2), ((0, 0), (0, 8 - HEADS), (0, 0)))
        g_row = sm[:, :, LANE_GDEC:LANE_GDEC + HEADS].reshape(b, s // GDN_CHUNK, GDN_CHUNK, HEADS)
        g_row = jnp.swapaxes(g_row, 2, 3).reshape(b, s // GDN_CHUNK, 1, gw)
        g_row = jnp.broadcast_to(g_row, (b, s // GDN_CHUNK, 8, gw))

        ya = hgrn2(proj3, lb_all[l])
        yb = fox_attention(proj3, c_row, c_exp)
        yc = rglru(proj3, lru_conv_w[l], lru_conv_b[l], _block_diag(lru_wa[l]).astype(BF16), lru_ba[l],
                   _block_diag(lru_wx[l]).astype(BF16), lru_bx[l], lru_lam[l])
        yd = gdn(proj3, sm, c_exp, g_row, gdn_conv_w[l])
        x = mix_out(x, ya, yb, yc, yd, mix_gain[l], w_out[l].astype(BF16))

        kv = norm_matmul(mem.reshape(b * mlen, d), norm_memkv[l], w_mkv[l].astype(BF16), tm=256)
        x = mem_attention(x, norm_mem[l], w_mq[l].astype(BF16), kv.reshape(b, mlen, 2 * gw), w_mo[l].astype(BF16))

        w_router = jnp.concatenate([w_re[l], w_rg[l], jnp.zeros((d, SMALL_W - N_EXPERTS - N_EXPERT_GROUPS), F32)], axis=1)
        b_router = jnp.concatenate([b_re[l], b_rg[l], jnp.zeros((SMALL_W - N_EXPERTS - N_EXPERT_GROUPS,), F32)]).reshape(1, SMALL_W)
        x = hier_moe_routed(x.reshape(t_tok, d), norm_ffn[l], w_router, b_router, w_e_gu, w_e_dn, l,
                            norm_final, final_norm=(l == depth - 1)).reshape(b, s, d)

    return x
```

```python
import functools
import math

import jax
import jax.numpy as jnp
from jax import lax
from jax.experimental import pallas as pl
from jax.experimental.pallas import tpu as pltpu
from jax.experimental.pallas import tpu_sc as plsc

F32 = jnp.float32
BF16 = jnp.bfloat16

HEAD_DIM = 64
GROUP_WIDTH = 256
HEADS = GROUP_WIDTH // HEAD_DIM
GDN_CHUNK = 64
GDN_BATCH = 2
FOX_BATCH = 2
HGRN_BATCH = 2
HGRN_CHUNK = 16
CONV_W = 4
LRU_C = 8.0
EPS = 1e-6
N_EXPERT_GROUPS = 4
EXPERTS_PER_GROUP = 8
N_EXPERTS = N_EXPERT_GROUPS * EXPERTS_PER_GROUP
D_EXPERT = 256
SC_GATHER_CHUNK = 128
EXPERT_TILE = 256
SMALL_W = 128
SEQ_TILE = 256
VMEM_LIMIT = 56 * 1024 * 1024

(COL_AQ, COL_AF, COL_AI, COL_AG, COL_BQ, COL_BK, COL_BV, COL_CX, COL_CG,
 COL_DQ, COL_DK, COL_DV, COL_DZ) = range(13)
N_PROJ = 13 * GROUP_WIDTH + 2 * SMALL_W
COL_SMALL = 13 * GROUP_WIDTH // SMALL_W
LANE_FOX = 0
LANE_BETA = 4
LANE_GDEC = 8


def _cparams(*sem):
    return pltpu.CompilerParams(dimension_semantics=sem, vmem_limit_bytes=VMEM_LIMIT)


def _dot(a, b):
    return jnp.dot(a, b, preferred_element_type=F32)


def _dot_nt(a, b):
    return lax.dot_general(a, b, (((1,), (1,)), ((), ())), preferred_element_type=F32)


def _dot_tn(a, b):
    return lax.dot_general(a, b, (((0,), (0,)), ((), ())), preferred_element_type=F32)


def _split3(x):
    h = x.astype(BF16)
    r = x - h.astype(F32)
    m = r.astype(BF16)
    l = (r - m.astype(F32)).astype(BF16)
    return h, m, l


def _dot_exact_rhs(x, w_bf16):
    h, m, l = _split3(x)
    return _dot(h, w_bf16) + _dot(m, w_bf16) + _dot(l, w_bf16)


def _dot_exact_lhs(w_bf16, x):
    h, m, l = _split3(x)
    return _dot(w_bf16, h) + _dot(w_bf16, m) + _dot(w_bf16, l)


def _iota(shape, dim):
    return lax.broadcasted_iota(jnp.int32, shape, dim)


def _head_ones(n=GROUP_WIDTH):
    r = _iota((n, n), 0) // HEAD_DIM
    c = _iota((n, n), 1) // HEAD_DIM
    return r == c


def _sigmoid(x):
    return 1.0 / (1.0 + jnp.exp(-x))


def _silu(x):
    return x * _sigmoid(x)


def _log_sigmoid(x):
    return jnp.minimum(x, 0.0) - jnp.log1p(jnp.exp(-jnp.abs(x)))


def _softplus(x):
    return jnp.maximum(x, 0.0) + jnp.log1p(jnp.exp(-jnp.abs(x)))


def _gelu_tanh(x):
    return 0.5 * x * (1.0 + jnp.tanh(math.sqrt(2.0 / math.pi) * (x + 0.044715 * (x * x * x))))


def _pack_bf16_pairs(x):
    n = x.shape[1] // 2
    u = lax.bitcast_convert_type(x.astype(BF16).astype(F32), jnp.uint32)
    return u[:, :n] | (u[:, n:] >> 16)


def _unpack_bf16_pairs(p):
    hi = lax.bitcast_convert_type(p & jnp.uint32(0xFFFF0000), F32)
    lo = lax.bitcast_convert_type(p << 16, F32)
    return jnp.concatenate([hi, lo], axis=1)


def _head_mean_sq(x, ones_bf16):
    return _dot_exact_rhs(x * x, ones_bf16) * (1.0 / HEAD_DIM)


def _head_rms(x, ones_bf16):
    return x * lax.rsqrt(_head_mean_sq(x, ones_bf16) + EPS)


def _stack_heads(x):
    lane_head = _iota(x.shape, 1) // HEAD_DIM
    parts = []
    for h in range(HEADS):
        parts.append(jnp.where(lane_head == h, x, 0.0))
    return jnp.concatenate(parts, axis=0)


def _unstack_heads(xs, rows):
    out = xs[0:rows]
    for h in range(1, HEADS):
        out = out + xs[h * rows:(h + 1) * rows]
    return out


def _causal_conv4(x, prev8, w):
    r = x.shape[0]
    row8 = _iota((8, x.shape[1]), 0)
    acc = x * w[CONV_W - 1:CONV_W, :]
    for k in range(1, CONV_W):
        xs = pltpu.roll(x, k, 0)
        ps = pltpu.roll(prev8, k, 0)
        top = jnp.where(row8 < k, ps, xs[0:8])
        xs = jnp.concatenate([top, xs[8:r]], axis=0)
        acc = acc + xs * w[CONV_W - 1 - k:CONV_W - k, :]
    return acc


def _norm_matmul_kernel(x_ref, g_ref, w_ref, o_ref):
    x = x_ref[...]
    ms = jnp.mean(x * x, axis=-1, keepdims=True)
    h = (x * lax.rsqrt(ms + EPS)) * g_ref[...]
    o_ref[...] = _dot(h.astype(BF16), w_ref[...]).astype(o_ref.dtype)


def norm_matmul(x, g, w_bf16, tm, out_dtype=F32):
    m, d = x.shape
    n = w_bf16.shape[1]
    return pl.pallas_call(
        _norm_matmul_kernel,
        grid=(m // tm,),
        in_specs=[pl.BlockSpec((tm, d), lambda i: (i, 0)),
                  pl.BlockSpec((1, d), lambda i: (0, 0)),
                  pl.BlockSpec((d, n), lambda i: (0, 0))],
        out_specs=pl.BlockSpec((tm, n), lambda i: (i, 0)),
        out_shape=jax.ShapeDtypeStruct((m, n), out_dtype),
        compiler_params=_cparams("parallel"),
        name="norm_matmul",
    )(x, g.reshape(1, d), w_bf16)


def _small_prep_kernel(s_ref, p_ref, o_ref, x_ref):
    S = s_ref.shape[1]
    W = s_ref.shape[2]
    blk = SEQ_TILE
    lane = _iota((blk, W), 1)
    er = _iota((W, 2 * HEADS * W), 0)
    ec = _iota((W, 2 * HEADS * W), 1) // W
    expand = jnp.where(er == jnp.where(ec < HEADS, LANE_FOX + ec, LANE_GDEC + ec - HEADS), 1.0, 0.0).astype(BF16)
    is_fox = (lane >= LANE_FOX) & (lane < LANE_FOX + HEADS)
    is_beta = (lane >= LANE_BETA) & (lane < LANE_BETA + HEADS)
    is_gdec = (lane >= LANE_GDEC) & (lane < LANE_GDEC + HEADS)
    r = _iota((blk, blk), 0)
    c = _iota((blk, blk), 1)
    tril_all = jnp.where(r >= c, 1.0, 0.0).astype(BF16)
    tril_chunk = jnp.where((r >= c) & (r // GDN_CHUNK == c // GDN_CHUNK), 1.0, 0.0).astype(BF16)
    neg_exp_a = -jnp.exp(p_ref[1:2, :])
    carry = jnp.zeros((1, W), F32)
    for i in range(S // blk):
        sl = pl.ds(i * blk, blk)
        z = s_ref[0, sl, :] + p_ref[0:1, :]
        fox = jnp.where(is_fox, _log_sigmoid(z), 0.0)
        beta = jnp.where(is_beta, _sigmoid(z), 0.0)
        gdec = jnp.where(is_gdec, neg_exp_a * _softplus(z), 0.0)
        cf = _dot_exact_lhs(tril_all, fox) + carry
        cg = _dot_exact_lhs(tril_chunk, gdec)
        carry = cf[blk - 1:blk, :]
        o_ref[0, sl, :] = cf + cg + beta
        x_ref[0, sl, :] = _dot_exact_rhs(cf + cg, expand)


def small_prep(proj3, params):
    b, s, _ = proj3.shape
    w = SMALL_W
    return pl.pallas_call(
        _small_prep_kernel,
        grid=(b,),
        in_specs=[pl.BlockSpec((1, s, w), lambda i: (i, 0, COL_SMALL)),
                  pl.BlockSpec((8, w), lambda i: (0, 0))],
        out_specs=[pl.BlockSpec((1, s, w), lambda i: (i, 0, 0)),
                   pl.BlockSpec((1, s, 2 * HEADS * w), lambda i: (i, 0, 0))],
        out_shape=[jax.ShapeDtypeStruct((b, s, w), F32), jax.ShapeDtypeStruct((b, s, 2 * HEADS * w), F32)],
        compiler_params=_cparams("parallel"),
        name="small_prep",
    )(proj3, params)


def _hgrn2_kernel(q_ref, f_ref, i_ref, g_ref, lb_ref, o_ref, st_ref, qs, ks, vs, bs, os_):
    nb = q_ref.shape[0]
    T = q_ref.shape[1]
    C = HGRN_CHUNK
    seqs = range(nb)

    @pl.when(pl.program_id(1) == 0)
    def _():
        st_ref[...] = jnp.zeros_like(st_ref)

    same_head = _head_ones()
    ones_bf = jnp.where(same_head, 1.0, 0.0).astype(BF16)
    lb = lb_ref[...]
    log_lb = jnp.log(lb)
    r = _iota((T, T), 0)
    c = _iota((T, T), 1)
    tril_chunk = jnp.where((r >= c) & (r // C == c // C), 1.0, 0.0).astype(BF16)
    for bb in seqs:
        fl = f_ref[bb]
        c2 = jnp.log1p(-lb) + _log_sigmoid(fl)
        mx = jnp.maximum(log_lb, c2)
        log_f = mx + jnp.log1p(jnp.exp(-jnp.abs(log_lb - c2)))
        bs[bb] = _dot_exact_lhs(tril_chunk, log_f)
        qs[bb] = _silu(q_ref[bb])
        ks[bb] = (1.0 - lb) * _sigmoid(-fl)
        vs[bb] = i_ref[bb]

    trow = _iota((C, GROUP_WIDTH), 0)

    def chunk(ci, carry):
        r0 = pl.multiple_of(ci * C, C)
        qc = [qs[bb, pl.ds(r0, C), :] for bb in seqs]
        kc = [ks[bb, pl.ds(r0, C), :] for bb in seqs]
        vc = [vs[bb, pl.ds(r0, C), :] for bb in seqs]
        bc = [bs[bb, pl.ds(r0, C), :] for bb in seqs]
        st = [st_ref[bb] for bb in seqs]
        a = []
        for bb in seqs:
            parts = []
            for s in range(C):
                parts.append(jnp.where(trow >= s, jnp.exp(bc[bb] - bc[bb][s:s + 1, :]) * (qc[bb] * kc[bb][s:s + 1, :]), 0.0))
            a.append(jnp.concatenate(parts, axis=0).astype(BF16))
        sc = [_dot(a[bb], ones_bf) for bb in seqs]
        o = [_dot_nt((qc[bb] * jnp.exp(bc[bb])).astype(BF16), st[bb].astype(BF16)) for bb in seqs]
        b_last = [bc[bb][C - 1:C, :] for bb in seqs]
        upd = [_dot_tn(vc[bb].astype(BF16), (kc[bb] * jnp.exp(b_last[bb] - bc[bb])).astype(BF16)) for bb in seqs]
        for bb in seqs:
            acc = o[bb]
            for s in range(C):
                acc = acc + sc[bb][s * C:(s + 1) * C, :] * vc[bb][s:s + 1, :]
            os_[bb, pl.ds(r0, C), :] = acc
            st_ref[bb] = st[bb] * jnp.exp(b_last[bb]) + jnp.where(same_head, upd[bb], 0.0)
        return carry

    lax.fori_loop(0, T // C, chunk, 0)
    for bb in seqs:
        o_ref[bb] = _head_rms(os_[bb], ones_bf) * _silu(g_ref[bb])


def hgrn2(proj3, lb):
    b, s, _ = proj3.shape
    t = SEQ_TILE
    gw = GROUP_WIDTH
    nb = HGRN_BATCH
    col = lambda cidx: pl.BlockSpec((nb, t, gw), lambda i, j: (i, j, cidx))
    return pl.pallas_call(
        _hgrn2_kernel,
        grid=(b // nb, s // t),
        in_specs=[col(COL_AQ), col(COL_AF), col(COL_AI), col(COL_AG),
                  pl.BlockSpec((1, gw), lambda i, j: (0, 0))],
        out_specs=pl.BlockSpec((nb, t, gw), lambda i, j: (i, j, 0)),
        out_shape=jax.ShapeDtypeStruct((b, s, gw), F32),
        scratch_shapes=[pltpu.VMEM((nb, gw, gw), F32)] + [pltpu.VMEM((nb, t, gw), F32)] * 5,
        compiler_params=_cparams("parallel", "arbitrary"),
        name="hgrn2",
    )(proj3, proj3, proj3, proj3, lb.reshape(1, gw))


def _fox_kernel(q_ref, k_ref, v_ref, cr_ref, cx_ref, o_ref, kb_ref, vt_ref, acc_ref):
    nb = q_ref.shape[0]
    tq = q_ref.shape[1]
    tk = tq
    S = k_ref.shape[1]
    n = HEADS * tq
    qi = pl.program_id(1)
    seqs = range(nb)

    @pl.when(qi == 0)
    def _():
        for bb in seqs:
            for i in range(S // tk):
                sl = pl.ds(i * tk, tk)
                kb_ref[bb, sl, :] = k_ref[bb, sl, :].astype(BF16)
                vt_ref[bb, :, sl] = v_ref[bb, sl, :].T.astype(BF16)

    q0 = pl.multiple_of(qi * tq, tq)
    qs, cq_row = [], []
    for bb in seqs:
        qs.append(_stack_heads(q_ref[bb] * (HEAD_DIM ** -0.5)).astype(BF16))
        cr = cr_ref[bb, :, pl.ds(q0, tq)]
        cq_row.append(jnp.concatenate([cr[h:h + 1, :] for h in range(HEADS)], axis=1))
    acc_ref[...] = jnp.zeros_like(acc_ref)
    reps = tq // SMALL_W

    def step(kb, stats, masked):
        k0 = pl.multiple_of(kb * tk, tk)
        st = [_dot_nt(kb_ref[bb, pl.ds(k0, tk), :], qs[bb]) for bb in seqs]
        for bb in seqs:
            cx = cx_ref[bb, pl.ds(k0, tk), :]
            ck = jnp.concatenate([cx[:, h * SMALL_W:(h + 1) * SMALL_W] for h in range(HEADS) for _ in range(reps)], axis=1)
            st[bb] = st[bb] + (cq_row[bb] - ck)
            if masked:
                st[bb] = jnp.where(_iota((tk, n), 0) <= _iota((tk, n), 1) % tq, st[bb], -jnp.inf)
        m_new = [jnp.maximum(stats[bb][0], jnp.max(st[bb], axis=0, keepdims=True)) for bb in seqs]
        alpha = [jnp.exp(stats[bb][0] - m_new[bb]) for bb in seqs]
        p = [jnp.exp(st[bb] - m_new[bb]) for bb in seqs]
        l_new = [alpha[bb] * stats[bb][1] + jnp.sum(p[bb], axis=0, keepdims=True) for bb in seqs]
        pv = [_dot(vt_ref[bb, :, pl.ds(k0, tk)], p[bb].astype(BF16)) for bb in seqs]
        for bb in seqs:
            acc_ref[bb] = alpha[bb] * acc_ref[bb] + pv[bb]
        return tuple((m_new[bb], l_new[bb]) for bb in seqs)

    init = tuple((jnp.full((1, n), -jnp.inf, F32), jnp.zeros((1, n), F32)) for _ in seqs)
    stats = lax.fori_loop(0, qi, lambda kb, c: step(kb, c, False), init)
    stats = step(qi, stats, True)

    ones_bf = jnp.where(_head_ones(), 1.0, 0.0).astype(BF16)
    for bb in seqs:
        acc = acc_ref[bb] * (1.0 / stats[bb][1])
        ot = jnp.concatenate([acc[h * HEAD_DIM:(h + 1) * HEAD_DIM, h * tq:(h + 1) * tq] for h in range(HEADS)], axis=0)
        o_ref[bb] = _head_rms(ot.T, ones_bf)


def fox_attention(proj3, c_row, c_exp):
    b, s, _ = proj3.shape
    tq = SEQ_TILE
    gw = GROUP_WIDTH
    nb = FOX_BATCH
    return pl.pallas_call(
        _fox_kernel,
        grid=(b // nb, s // tq),
        in_specs=[pl.BlockSpec((nb, tq, gw), lambda i, j: (i, j, COL_BQ)),
                  pl.BlockSpec((nb, s, gw), lambda i, j: (i, 0, COL_BK)),
                  pl.BlockSpec((nb, s, gw), lambda i, j: (i, 0, COL_BV)),
                  pl.BlockSpec((nb, 8, s), lambda i, j: (i, 0, 0)),
                  pl.BlockSpec((nb, s, HEADS * SMALL_W), lambda i, j: (i, 0, 0))],
        out_specs=pl.BlockSpec((nb, tq, gw), lambda i, j: (i, j, 0)),
        out_shape=jax.ShapeDtypeStruct((b, s, gw), F32),
        scratch_shapes=[pltpu.VMEM((nb, s, gw), BF16),
                        pltpu.VMEM((nb, gw, s), BF16),
                        pltpu.VMEM((nb, gw, HEADS * tq), F32)],
        compiler_params=_cparams("parallel", "arbitrary"),
        name="fox_attention",
    )(proj3, proj3, proj3, c_row, c_exp)


def _rglru_kernel(x_ref, g_ref, cw_ref, cb_ref, wa_ref, ba_ref, wx_ref, bx_ref, lam_ref, o_ref, prev_ref, h_ref):
    @pl.when(pl.program_id(1) == 0)
    def _():
        prev_ref[...] = jnp.zeros_like(prev_ref)
        h_ref[...] = jnp.zeros_like(h_ref)

    x = x_ref[0]
    t = x.shape[0]
    xc = _causal_conv4(x, prev_ref[...], cw_ref[...]) + cb_ref[...]
    prev_ref[...] = x[t - 8:t, :]
    xb = xc.astype(BF16)
    r = _sigmoid(_dot(xb, wa_ref[...]) + ba_ref[...])
    ig = _sigmoid(_dot(xb, wx_ref[...]) + bx_ref[...])
    log_a = (-LRU_C * r) * _softplus(-lam_ref[...])
    a = jnp.exp(log_a)
    u = jnp.sqrt(1.0 - jnp.exp(2.0 * log_a)) * (ig * xc)
    row = _iota(a.shape, 0)
    d = 1
    while d < t:
        valid = row >= d
        u = jnp.where(valid, a * pltpu.roll(u, d, 0) + u, u)
        a = jnp.where(valid, a * pltpu.roll(a, d, 0), a)
        d *= 2
    h = a * h_ref[...] + u
    h_ref[...] = h[t - 1:t, :]
    ones_bf = jnp.where(_head_ones(), 1.0, 0.0).astype(BF16)
    o_ref[0] = _head_rms(h * _gelu_tanh(g_ref[0]), ones_bf)


def rglru(proj3, conv_w, conv_b, wa_bd, ba, wx_bd, bx, lam):
    b, s, _ = proj3.shape
    t = SEQ_TILE
    gw = GROUP_WIDTH
    row = lambda: pl.BlockSpec((1, gw), lambda i, j: (0, 0))
    return pl.pallas_call(
        _rglru_kernel,
        grid=(b, s // t),
        in_specs=[pl.BlockSpec((1, t, gw), lambda i, j: (i, j, COL_CX)),
                  pl.BlockSpec((1, t, gw), lambda i, j: (i, j, COL_CG)),
                  pl.BlockSpec((CONV_W, gw), lambda i, j: (0, 0)), row(),
                  pl.BlockSpec((gw, gw), lambda i, j: (0, 0)), row(),
                  pl.BlockSpec((gw, gw), lambda i, j: (0, 0)), row(), row()],
        out_specs=pl.BlockSpec((1, t, gw), lambda i, j: (i, j, 0)),
        out_shape=jax.ShapeDtypeStruct((b, s, gw), F32),
        scratch_shapes=[pltpu.VMEM((8, gw), F32), pltpu.VMEM((1, gw), F32)],
        compiler_params=_cparams("parallel", "arbitrary"),
        name="rglru",
    )(proj3, proj3, conv_w, conv_b.reshape(1, gw), wa_bd, ba.reshape(1, gw), wx_bd, bx.reshape(1, gw), lam.reshape(1, gw))


def _gdn_kernel(q_ref, k_ref, v_ref, z_ref, sm_ref, ge_ref, gr_ref, cw_ref, o_ref, st_ref, pq_ref, pk_ref, pv_ref):
    nb = q_ref.shape[0]
    T = q_ref.shape[1]
    C = GDN_CHUNK

    @pl.when(pl.program_id(1) == 0)
    def _():
        st_ref[...] = jnp.zeros_like(st_ref)
        pq_ref[...] = jnp.zeros_like(pq_ref)
        pk_ref[...] = jnp.zeros_like(pk_ref)
        pv_ref[...] = jnp.zeros_like(pv_ref)

    same_head = _head_ones()
    ones_bf = jnp.where(same_head, 1.0, 0.0).astype(BF16)
    cw = cw_ref[...]
    gw = GROUP_WIDTH
    er = _iota((SMALL_W, gw), 0)
    ec = _iota((SMALL_W, gw), 1) // HEAD_DIM
    exp_beta = jnp.where(er == LANE_BETA + ec, 1.0, 0.0).astype(BF16)
    exp_g = jnp.where(er == LANE_GDEC + ec, 1.0, 0.0).astype(BF16)
    n = HEADS * C
    rr = _iota((n, n), 0)
    cc = _iota((n, n), 1)
    incl = same_head & (rr % C >= cc % C)
    strict = same_head & (rr % C > cc % C)

    prep = []
    for bb in range(nb):
        xq, xk, xv = q_ref[bb], k_ref[bb], v_ref[bb]
        q = _silu(_causal_conv4(xq, pq_ref[bb], cw[:, 0:gw]))
        k = _silu(_causal_conv4(xk, pk_ref[bb], cw[:, gw:2 * gw]))
        v = _silu(_causal_conv4(xv, pv_ref[bb], cw[:, 2 * gw:3 * gw]))
        pq_ref[bb] = xq[T - 8:T, :]
        pk_ref[bb] = xk[T - 8:T, :]
        pv_ref[bb] = xv[T - 8:T, :]
        q = q * lax.rsqrt(_dot_exact_rhs(q * q, ones_bf) + EPS) * (HEAD_DIM ** -0.5)
        k = k * lax.rsqrt(_dot_exact_rhs(k * k, ones_bf) + EPS)
        sm = sm_ref[bb]
        betax = _dot_exact_rhs(sm, exp_beta)
        gx = _dot_exact_rhs(sm, exp_g)
        prep.append((q, k, v, betax, gx))

    reps = gw // SMALL_W
    n_chunks = T // C
    eye = jnp.where(rr == cc, 1.0, 0.0)

    chains = []
    for ci in range(n_chunks):
        sl = slice(ci * C, (ci + 1) * C)
        for bb in range(nb):
            q, k, v, betax, gx = prep[bb]
            qc, kc, vc, bx, gc = q[sl], k[sl], v[sl], betax[sl], gx[sl]
            grow = gr_ref[bb, ci, 0:1, :]
            ge = ge_ref[bb, sl, :]
            gcol = jnp.concatenate(
                [jnp.concatenate([ge[:, h * SMALL_W:(h + 1) * SMALL_W]] * reps, axis=1) for h in range(HEADS)], axis=0)
            eg = jnp.exp(gc)
            kb = kc * bx
            g_last = gc[C - 1:C, :]
            gamma = jnp.where(incl, jnp.exp(jnp.where(incl, gcol - grow, 0.0)), 0.0)
            ks_bf = _stack_heads(kc).astype(BF16)
            m = jnp.where(strict, _dot_nt(_stack_heads(kb).astype(BF16), ks_bf) * gamma, 0.0)
            chains.append(dict(
                x=eye - m, y=m,
                a_qk=(_dot_nt(_stack_heads(qc).astype(BF16), ks_bf) * gamma).astype(BF16),
                rhs_u=_stack_heads(vc * bx).astype(BF16),
                rhs_w=_stack_heads(kb * eg).astype(BF16),
                q_dec=_stack_heads(qc * eg).astype(BF16),
                k_dec=_stack_heads(kc * jnp.exp(g_last - gc)).astype(BF16),
                decay=jnp.exp(g_last)))
    for _ in range(5):
        for ch in chains:
            yb = ch["y"].astype(BF16)
            ch["y"] = _dot(yb, yb)
        for ch in chains:
            ch["x"] = ch["x"] + _dot(ch["x"].astype(BF16), ch["y"].astype(BF16))
    for ch in chains:
        t_inv = ch["x"].astype(BF16)
        ch["u"] = _dot(t_inv, ch["rhs_u"])
        ch["wk"] = _dot(t_inv, ch["rhs_w"]).astype(BF16)

    outs = [[] for _ in range(nb)]
    for ci in range(n_chunks):
        for bb in range(nb):
            ch = chains[ci * nb + bb]
            st = st_ref[bb]
            st_bf = st.astype(BF16)
            v_new = (ch["u"] - _dot(ch["wk"], st_bf)).astype(BF16)
            o = _dot(ch["q_dec"], st_bf) + _dot(ch["a_qk"], v_new)
            outs[bb].append(_unstack_heads(o, C))
            st_ref[bb] = st * ch["decay"] + _dot_tn(ch["k_dec"], v_new)
    for bb in range(nb):
        o = jnp.concatenate(outs[bb], axis=0)
        o_ref[bb] = _head_rms(o, ones_bf) * _silu(z_ref[bb])


def gdn(proj3, sm, c_exp, g_row, conv_w):
    b, s, _ = proj3.shape
    t = SEQ_TILE
    gw = GROUP_WIDTH
    nb = GDN_BATCH
    col = lambda cidx: pl.BlockSpec((nb, t, gw), lambda i, j: (i, j, cidx))
    return pl.pallas_call(
        _gdn_kernel,
        grid=(b // nb, s // t),
        in_specs=[col(COL_DQ), col(COL_DK), col(COL_DV), col(COL_DZ),
                  pl.BlockSpec((nb, t, SMALL_W), lambda i, j: (i, j, 0)),
                  pl.BlockSpec((nb, t, HEADS * SMALL_W), lambda i, j: (i, j, 1)),
                  pl.BlockSpec((nb, t // GDN_CHUNK, 8, gw), lambda i, j: (i, j, 0, 0)),
                  pl.BlockSpec((CONV_W, 3 * gw), lambda i, j: (0, 0))],
        out_specs=pl.BlockSpec((nb, t, gw), lambda i, j: (i, j, 0)),
        out_shape=jax.ShapeDtypeStruct((b, s, gw), F32),
        scratch_shapes=[pltpu.VMEM((nb, gw, gw), F32)] + [pltpu.VMEM((nb, 8, gw), F32)] * 3,
        compiler_params=_cparams("parallel", "arbitrary"),
        name="gdn",
    )(proj3, proj3, proj3, proj3, sm, c_exp, g_row, conv_w)


def _mix_out_kernel(x_ref, ya_ref, yb_ref, yc_ref, yd_ref, gain_ref, w_ref, o_ref):
    gw = GROUP_WIDTH
    gain = gain_ref[...]
    acc = x_ref[0]
    for i, y_ref in enumerate((ya_ref, yb_ref, yc_ref, yd_ref)):
        yg = (y_ref[0] * gain[:, i * gw:(i + 1) * gw]).astype(BF16)
        acc = acc + _dot(yg, w_ref[i * gw:(i + 1) * gw, :])
    o_ref[0] = acc


def mix_out(x3, ya, yb, yc, yd, gain, w_out_bf16):
    b, s, d = x3.shape
    t = SEQ_TILE
    gw = GROUP_WIDTH
    grp = lambda: pl.BlockSpec((1, t, gw), lambda i, j: (i, j, 0))
    return pl.pallas_call(
        _mix_out_kernel,
        grid=(b, s // t),
        in_specs=[pl.BlockSpec((1, t, d), lambda i, j: (i, j, 0)),
                  grp(), grp(), grp(), grp(),
                  pl.BlockSpec((1, 4 * gw), lambda i, j: (0, 0)),
                  pl.BlockSpec((4 * gw, d), lambda i, j: (0, 0))],
        out_specs=pl.BlockSpec((1, t, d), lambda i, j: (i, j, 0)),
        out_shape=jax.ShapeDtypeStruct((b, s, d), F32),
        compiler_params=_cparams("parallel", "parallel"),
        name="mix_out",
    )(x3, ya, yb, yc, yd, gain.reshape(1, 4 * gw), w_out_bf16)


def _mem_attn_kernel(x_ref, g_ref, wq_ref, kv_ref, wo_ref, o_ref):
    x = x_ref[0]
    t = x.shape[0]
    gw = GROUP_WIDTH
    ms = jnp.mean(x * x, axis=-1, keepdims=True)
    h = ((x * lax.rsqrt(ms + EPS)) * g_ref[...]).astype(BF16)
    q = _dot(h, wq_ref[...])
    kv = kv_ref[0]
    k = kv[:, 0:gw].astype(BF16)
    v = kv[:, gw:2 * gw].astype(BF16)
    s = _dot_nt(k, _stack_heads(q).astype(BF16)) * (HEAD_DIM ** -0.5)
    s = s - jnp.max(s, axis=0, keepdims=True)
    p = jnp.exp(s)
    p = p * (1.0 / jnp.sum(p, axis=0, keepdims=True))
    ot = _dot_tn(v, p.astype(BF16))
    oc = jnp.concatenate([ot[hh * HEAD_DIM:(hh + 1) * HEAD_DIM, hh * t:(hh + 1) * t] for hh in range(HEADS)], axis=0).T
    o_ref[0] = x + _dot(oc.astype(BF16), wo_ref[...])


def mem_attention(x3, g, wq_bf16, kv3, wo_bf16):
    b, s, d = x3.shape
    t = SEQ_TILE
    m = kv3.shape[1]
    return pl.pallas_call(
        _mem_attn_kernel,
        grid=(b, s // t),
        in_specs=[pl.BlockSpec((1, t, d), lambda i, j: (i, j, 0)),
                  pl.BlockSpec((1, d), lambda i, j: (0, 0)),
                  pl.BlockSpec((d, GROUP_WIDTH), lambda i, j: (0, 0)),
                  pl.BlockSpec((1, m, 2 * GROUP_WIDTH), lambda i, j: (i, 0, 0)),
                  pl.BlockSpec((GROUP_WIDTH, d), lambda i, j: (0, 0))],
        out_specs=pl.BlockSpec((1, t, d), lambda i, j: (i, j, 0)),
        out_shape=jax.ShapeDtypeStruct((b, s, d), F32),
        compiler_params=_cparams("parallel", "parallel"),
        name="mem_attention",
    )(x3, g.reshape(1, d), wq_bf16, kv3, wo_bf16)


def _router_kernel(x_ref, g_ref, w_ref, b_ref, h_ref, r_ref, n_ref, cnt_ref):
    @pl.when(pl.program_id(0) == 0)
    def _():
        cnt_ref[...] = jnp.zeros_like(cnt_ref)

    x = x_ref[...]
    tm = x.shape[0]
    ms = jnp.mean(x * x, axis=-1, keepdims=True)
    h = (x * lax.rsqrt(ms + EPS)) * g_ref[...]
    h_ref[...] = _pack_bf16_pairs(h)
    hh, hm, _ = _split3(h)
    w = w_ref[...]
    wh, wm, _ = _split3(w)
    logits = _dot(hh, wh) + _dot(hh, wm) + _dot(hm, wh) + b_ref[...]
    lane = _iota(logits.shape, 1)
    big = jnp.int32(1 << 30)
    neg = -jnp.inf
    is_grp = (lane >= N_EXPERTS) & (lane < N_EXPERTS + N_EXPERT_GROUPS)
    gl = jnp.where(is_grp, logits, neg)
    gmax = jnp.max(gl, axis=-1, keepdims=True)
    p_grp = 1.0 / jnp.sum(jnp.exp(gl - gmax), axis=-1, keepdims=True)
    g_sel = jnp.min(jnp.where(gl == gmax, lane, big), axis=-1, keepdims=True) - N_EXPERTS
    in_grp = (lane < N_EXPERTS) & (lane // EXPERTS_PER_GROUP == g_sel)
    el = jnp.where(in_grp, logits, neg)
    m1 = jnp.max(el, axis=-1, keepdims=True)
    i1 = jnp.min(jnp.where(el == m1, lane, big), axis=-1, keepdims=True)
    el2 = jnp.where(lane == i1, neg, el)
    m2 = jnp.max(el2, axis=-1, keepdims=True)
    i2 = jnp.min(jnp.where(el2 == m2, lane, big), axis=-1, keepdims=True)
    e21 = jnp.exp(m2 - m1)
    w1 = 1.0 / (1.0 + e21)
    w2 = e21 / (1.0 + e21)
    oh1 = lane == i1
    oh2 = lane == i2
    onehot = jnp.where(oh1 | oh2, 1.0, 0.0)
    strict = jnp.where(_iota((tm, tm), 0) > _iota((tm, tm), 1), 1.0, 0.0).astype(BF16)
    before = _dot(strict, onehot.astype(BF16)) + cnt_ref[...]
    rank1 = jnp.sum(jnp.where(oh1, before, 0.0), axis=-1, keepdims=True)
    rank2 = jnp.sum(jnp.where(oh2, before, 0.0), axis=-1, keepdims=True)
    cnt = before[tm - 1:tm, :] + onehot[tm - 1:tm, :]
    cnt_ref[...] = cnt
    n_ref[...] = jnp.broadcast_to(cnt, n_ref.shape)
    cols = (i1.astype(F32), i2.astype(F32), p_grp * w1, p_grp * w2, rank1, rank2)
    route = jnp.zeros(logits.shape, F32)
    for j, col in enumerate(cols):
        route = jnp.where(lane == j, col, route)
    r_ref[...] = route


def router(x2, g, w_router, b_router, tm):
    m, d = x2.shape
    return pl.pallas_call(
        _router_kernel,
        grid=(m // tm,),
        in_specs=[pl.BlockSpec((tm, d), lambda i: (i, 0)),
                  pl.BlockSpec((1, d), lambda i: (0, 0)),
                  pl.BlockSpec((d, SMALL_W), lambda i: (0, 0)),
                  pl.BlockSpec((1, SMALL_W), lambda i: (0, 0))],
        out_specs=[pl.BlockSpec((tm, d // 2), lambda i: (i, 0)),
                   pl.BlockSpec((tm, SMALL_W), lambda i: (i, 0)),
                   pl.BlockSpec((8, SMALL_W), lambda i: (0, 0))],
        out_shape=[jax.ShapeDtypeStruct((m, d // 2), jnp.uint32), jax.ShapeDtypeStruct((m, SMALL_W), F32),
                   jax.ShapeDtypeStruct((8, SMALL_W), F32)],
        scratch_shapes=[pltpu.VMEM((1, SMALL_W), F32)],
        compiler_params=_cparams("arbitrary"),
        name="router",
    )(x2, g.reshape(1, d), w_router, b_router)


def _row_copy(src_ref, src_row, dst_ref, dst_row, sem):
    return pltpu.make_async_copy(src_ref.at[pl.ds(src_row, 1)], dst_ref.at[pl.ds(dst_row, 1)], sem)


def _dispatch_kernel(dest_ref, zrow_ref, h_ref, xs_ref, zbuf, sem):
    tm = h_ref.shape[0]
    base = pl.program_id(0) * tm

    @pl.when(pl.program_id(0) == 0)
    def _():
        zbuf[...] = jnp.zeros_like(zbuf)

        def zero_copy(e):
            row = pl.multiple_of(zrow_ref[e], EXPERT_TILE)
            return pltpu.make_async_copy(zbuf, xs_ref.at[pl.ds(row, EXPERT_TILE)], sem)

        for e in range(zrow_ref.shape[0]):
            @pl.when(zrow_ref[e] >= 0)
            def _():
                zero_copy(e).start()
        for e in range(zrow_ref.shape[0]):
            @pl.when(zrow_ref[e] >= 0)
            def _():
                zero_copy(e).wait()

    def start(t, c):
        for slot in range(2):
            _row_copy(h_ref, t, xs_ref, dest_ref[slot * dest_ref.shape[0] // 2 + base + t], sem).start(priority=slot)
        return c

    def wait(t, c):
        for slot in range(2):
            _row_copy(h_ref, t, xs_ref, dest_ref[slot * dest_ref.shape[0] // 2 + base + t], sem).wait()
        return c

    lax.fori_loop(0, tm, start, 0, unroll=8)
    lax.fori_loop(0, tm, wait, 0, unroll=8)


def moe_dispatch(h, dest, zrow, n_rows, tm):
    m, d = h.shape
    return pl.pallas_call(
        _dispatch_kernel,
        grid_spec=pltpu.PrefetchScalarGridSpec(
            num_scalar_prefetch=2,
            grid=(m // tm,),
            in_specs=[pl.BlockSpec((tm, d), lambda i, dest, zrow: (i, 0))],
            out_specs=pl.BlockSpec(memory_space=pl.ANY),
            scratch_shapes=[pltpu.VMEM((EXPERT_TILE, d), h.dtype), pltpu.SemaphoreType.DMA]),
        out_shape=jax.ShapeDtypeStruct((n_rows, d), h.dtype),
        compiler_params=_cparams("arbitrary"),
        name="moe_dispatch",
    )(dest, zrow, h)


def _experts_kernel(te_ref, nu_ref, xs_ref, wgu_ref, wdn_ref, y_ref, wgu_bf, wdn_bf):
    i = pl.program_id(0)

    @pl.when(i >= nu_ref[0])
    def _():
        y_ref[...] = jnp.zeros_like(y_ref)

    @pl.when(i < nu_ref[0])
    def _():
        prev = te_ref[jnp.maximum(i - 1, 0)]

        @pl.when((i == 0) | (te_ref[i] != prev))
        def _():
            wgu_bf[...] = wgu_ref[0, 0].astype(BF16)
            wdn_bf[...] = wdn_ref[0, 0].astype(BF16)

        gu = _dot(_unpack_bf16_pairs(xs_ref[...]).astype(BF16), wgu_bf[...])
        act = _silu(gu[:, 0:D_EXPERT]) * gu[:, D_EXPERT:2 * D_EXPERT]
        y_ref[...] = _pack_bf16_pairs(_dot(act.astype(BF16), wdn_bf[...]))


def moe_experts(xs, tile_expert, n_used, w_gu, w_dn, layer, tm):
    n_rows, dp = xs.shape
    d = 2 * dp
    tile = lambda i, te, nu: (jnp.minimum(i, nu[0] - 1), 0)
    wsel = lambda i, te, nu: (layer, te[jnp.minimum(i, nu[0] - 1)], 0, 0)
    return pl.pallas_call(
        _experts_kernel,
        grid_spec=pltpu.PrefetchScalarGridSpec(
            num_scalar_prefetch=2,
            grid=(n_rows // tm,),
            in_specs=[pl.BlockSpec((tm, dp), tile),
                      pl.BlockSpec((1, 1, d, 2 * D_EXPERT), wsel),
                      pl.BlockSpec((1, 1, D_EXPERT, d), wsel)],
            out_specs=pl.BlockSpec((tm, dp), lambda i, te, nu: (i, 0)),
            scratch_shapes=[pltpu.VMEM((d, 2 * D_EXPERT), BF16), pltpu.VMEM((D_EXPERT, d), BF16)]),
        out_shape=jax.ShapeDtypeStruct((n_rows, dp), jnp.uint32),
        compiler_params=_cparams("arbitrary"),
        name="moe_experts",
    )(tile_expert, n_used, xs, w_gu, w_dn)


def sc_gather_rows(table, idx):
    n_idx = idx.shape[0]
    width = table.shape[1]
    info = plsc.get_sparse_core_info()
    n_workers = info.num_cores * info.num_subcores
    per_worker = n_idx // n_workers
    chunk = SC_GATHER_CHUNK
    assert n_idx % (n_workers * chunk) == 0 and width % info.num_lanes == 0
    mesh = plsc.VectorSubcoreMesh(core_axis_name="c", subcore_axis_name="s")

    def body(table_hbm, idx_hbm, out_hbm, idx_v, rows_v, sem):
        wid = lax.axis_index("s") * info.num_cores + lax.axis_index("c")
        base = wid * per_worker

        @pl.loop(0, per_worker // chunk)
        def _(j):
            off = pl.multiple_of(base + j * chunk, chunk)
            pltpu.sync_copy(idx_hbm.at[pl.ds(off, chunk)], idx_v)
            pltpu.async_copy(table_hbm.at[idx_v], rows_v, sem).wait()
            pltpu.sync_copy(rows_v, out_hbm.at[pl.ds(off, chunk)])

    return pl.kernel(
        body,
        out_type=jax.ShapeDtypeStruct((n_idx, width), table.dtype),
        mesh=mesh,
        scratch_types=[pltpu.VMEM((chunk,), jnp.int32), pltpu.VMEM((chunk, width), table.dtype),
                       pltpu.SemaphoreType.DMA],
        name="sc_gather_rows",
    )(table, idx)


def _combine_rows_kernel(x_ref, r_ref, g_ref, y1_ref, y2_ref, o_ref, *, final_norm):
    r = r_ref[...]
    out = x_ref[...] + r[:, 2:3] * _unpack_bf16_pairs(y1_ref[...]) + r[:, 3:4] * _unpack_bf16_pairs(y2_ref[...])
    if final_norm:
        ms = jnp.mean(out * out, axis=-1, keepdims=True)
        out = (out * lax.rsqrt(ms + EPS)) * g_ref[...]
    o_ref[...] = out


def moe_combine_rows(x2, route, y_rows, g_final, tm, final_norm):
    m, d = x2.shape
    steps = m // tm
    return pl.pallas_call(
        functools.partial(_combine_rows_kernel, final_norm=final_norm),
        grid=(steps,),
        in_specs=[pl.BlockSpec((tm, d), lambda i: (i, 0)),
                  pl.BlockSpec((tm, SMALL_W), lambda i: (i, 0)),
                  pl.BlockSpec((1, d), lambda i: (0, 0)),
                  pl.BlockSpec((tm, d // 2), lambda i: (i, 0)),
                  pl.BlockSpec((tm, d // 2), lambda i: (i + steps, 0))],
        out_specs=pl.BlockSpec((tm, d), lambda i: (i, 0)),
        out_shape=jax.ShapeDtypeStruct((m, d), F32),
        compiler_params=_cparams("parallel"),
        name="moe_combine_rows",
    )(x2, route, g_final.reshape(1, d), y_rows, y_rows)


def hier_moe_routed(x2, g_ffn, w_router, b_router, w_gu, w_dn, layer, g_final, final_norm):
    m, d = x2.shape
    te_rows = EXPERT_TILE
    n_rows = 2 * m + N_EXPERTS * te_rows
    h, route, counts = router(x2, g_ffn, w_router, b_router, tm=512)
    cnt = counts[0, 0:N_EXPERTS].astype(jnp.int32)
    padded = (cnt + te_rows - 1) // te_rows * te_rows
    seg_end = jnp.cumsum(padded)
    seg_start = seg_end - padded
    idx = route[:, 0:2].astype(jnp.int32)
    onehot = idx[:, :, None] == jnp.arange(N_EXPERTS, dtype=jnp.int32)
    dest = (jnp.sum(jnp.where(onehot, seg_start, 0), axis=-1) + route[:, 4:6].astype(jnp.int32)).T.reshape(2 * m)
    tile_start = jnp.arange(n_rows // te_rows, dtype=jnp.int32) * te_rows
    tile_expert = jnp.minimum(jnp.sum((tile_start[:, None] >= seg_end[None, :]).astype(jnp.int32), axis=1), N_EXPERTS - 1)
    n_used = (seg_end[N_EXPERTS - 1] // te_rows).reshape(1)
    tail = n_used[0] + jnp.arange(N_EXPERTS, dtype=jnp.int32)
    zrow = jnp.concatenate([jnp.where(padded > 0, seg_end - te_rows, -1),
                            jnp.where(tail < n_rows // te_rows, tail * te_rows, -1)])
    xs = moe_dispatch(h, dest, zrow, n_rows, tm=512)
    ys = moe_experts(xs, tile_expert, n_used, w_gu, w_dn, layer, tm=te_rows)
    y_rows = sc_gather_rows(ys, dest)
    return moe_combine_rows(x2, route, y_rows, g_final, tm=256, final_norm=final_norm)


def _w_in_prep_kernel(w_ref, o_ref):
    gw = GROUP_WIDTH
    a_end = 7 * gw
    c_start = a_end + HEADS
    c_end = c_start + 5 * gw
    z_start = c_end + 2 * HEADS
    assert a_end % SMALL_W == LANE_FOX and c_end % SMALL_W == LANE_BETA and LANE_GDEC == LANE_BETA + HEADS
    w = w_ref[...]
    o_ref[:, 0:a_end] = w[:, 0:a_end].astype(BF16)
    o_ref[:, a_end:a_end + 5 * gw] = w[:, c_start:c_end].astype(BF16)
    o_ref[:, 12 * gw:13 * gw] = w[:, z_start:z_start + gw].astype(BF16)
    blk_fox = w[:, a_end:a_end + SMALL_W]
    blk_gdn = w[:, c_end - LANE_BETA:c_end - LANE_BETA + SMALL_W]
    lane = _iota(blk_fox.shape, 1)
    small = jnp.where(lane < LANE_BETA, blk_fox, jnp.where(lane < LANE_GDEC + HEADS, blk_gdn, 0.0))
    o_ref[:, 13 * gw:13 * gw + SMALL_W] = small.astype(BF16)
    o_ref[:, 13 * gw + SMALL_W:] = jnp.zeros((w.shape[0], SMALL_W), BF16)


def w_in_prep(w_all, layer):
    _, d, n = w_all.shape
    tr = 128
    return pl.pallas_call(
        _w_in_prep_kernel,
        grid=(d // tr,),
        in_specs=[pl.BlockSpec((None, tr, n), lambda i: (layer, i, 0))],
        out_specs=pl.BlockSpec((tr, N_PROJ), lambda i: (i, 0)),
        out_shape=jax.ShapeDtypeStruct((d, N_PROJ), BF16),
        compiler_params=_cparams("parallel"),
        name="w_in_prep",
    )(w_all)


def _block_diag(w):
    h = w.shape[0]
    eye = jnp.eye(h, dtype=w.dtype)
    return (eye[:, None, :, None] * w[:, :, None, :]).reshape(h * HEAD_DIM, h * HEAD_DIM)


def kernel(x, mem, norm_mix, w_in, hgrn_lb, fox_fb, lru_conv_w, lru_conv_b, lru_wa, lru_ba, lru_wx, lru_bx, lru_lam, gdn_conv_w, gdn_a_log, gdn_dt_bias, mix_gain, w_out, norm_mem, norm_memkv, w_mq, w_mkv, w_mo, norm_ffn, w_rg, b_rg, w_re, b_re, w_e_gu, w_e_dn, norm_final):
    b, s, d = x.shape
    depth = w_in.shape[0]
    t_tok = b * s
    mlen = mem.shape[1]
    gw = GROUP_WIDTH

    lb_all = jnp.cumsum(jax.nn.softmax(hgrn_lb.astype(F32), axis=0), axis=0)
    lb_all = lb_all - lb_all[0]

    x = x.astype(F32)
    for l in range(depth):
        proj = norm_matmul(x.reshape(t_tok, d), norm_mix[l], w_in_prep(w_in, l), tm=256)
        proj3 = proj.reshape(b, s, N_PROJ)
        prm = jnp.zeros((8, SMALL_W), F32)
        prm = prm.at[0, LANE_FOX:LANE_FOX + HEADS].set(fox_fb[l].astype(F32))
        prm = prm.at[0, LANE_GDEC:LANE_GDEC + HEADS].set(gdn_dt_bias[l].astype(F32))
        prm = prm.at[1, LANE_GDEC:LANE_GDEC + HEADS].set(gdn_a_log[l].astype(F32))
        sm, c_exp = small_prep(proj3, prm)
        c_row = jnp.pad(jnp.swapaxes(sm[:, :, LANE_FOX:LANE_FOX + HEADS], 1, 2), ((0, 0), (0, 8 - HEADS), (0, 0)))
        g_row = sm[:, :, LANE_GDEC:LANE_GDEC + HEADS].reshape(b, s // GDN_CHUNK, GDN_CHUNK, HEADS)
        g_row = jnp.swapaxes(g_row, 2, 3).reshape(b, s // GDN_CHUNK, 1, gw)
        g_row = jnp.broadcast_to(g_row, (b, s // GDN_CHUNK, 8, gw))

        ya = hgrn2(proj3, lb_all[l])
        yb = fox_attention(proj3, c_row, c_exp)
        yc = rglru(proj3, lru_conv_w[l], lru_conv_b[l], _block_diag(lru_wa[l]).astype(BF16), lru_ba[l],
                   _block_diag(lru_wx[l]).astype(BF16), lru_bx[l], lru_lam[l])
        yd = gdn(proj3, sm, c_exp, g_row, gdn_conv_w[l])
        x = mix_out(x, ya, yb, yc, yd, mix_gain[l], w_out[l].astype(BF16))

        kv = norm_matmul(mem.reshape(b * mlen, d), norm_memkv[l], w_mkv[l].astype(BF16), tm=256)
        x = mem_attention(x, norm_mem[l], w_mq[l].astype(BF16), kv.reshape(b, mlen, 2 * gw), w_mo[l].astype(BF16))

        w_router = jnp.concatenate([w_re[l], w_rg[l], jnp.zeros((d, SMALL_W - N_EXPERTS - N_EXPERT_GROUPS), F32)], axis=1)
        b_router = jnp.concatenate([b_re[l], b_rg[l], jnp.zeros((SMALL_W - N_EXPERTS - N_EXPERT_GROUPS,), F32)]).reshape(1, SMALL_W)
        x = hier_moe_routed(x.reshape(t_tok, d), norm_ffn[l], w_router, b_router, w_e_gu, w_e_dn, l,
                            norm_final, final_norm=(l == depth - 1)).reshape(b, s, d)

    return x
```

```python
import functools
import math

import jax
import jax.numpy as jnp
from jax import lax
from jax.experimental import pallas as pl
from jax.experimental.pallas import tpu as pltpu
from jax.experimental.pallas import tpu_sc as plsc

F32 = jnp.float32
BF16 = jnp.bfloat16

HEAD_DIM = 64
GROUP_WIDTH = 256
HEADS = GROUP_WIDTH // HEAD_DIM
GDN_CHUNK = 64
GDN_BATCH = 2
FOX_BATCH = 2
HGRN_BATCH = 2
HGRN_CHUNK = 16
CONV_W = 4
LRU_C = 8.0
EPS = 1e-6
N_EXPERT_GROUPS = 4
EXPERTS_PER_GROUP = 8
N_EXPERTS = N_EXPERT_GROUPS * EXPERTS_PER_GROUP
D_EXPERT = 256
SC_GATHER_CHUNK = 128
EXPERT_TILE = 256
SMALL_W = 128
SEQ_TILE = 256
VMEM_LIMIT = 56 * 1024 * 1024

(COL_AQ, COL_AF, COL_AI, COL_AG, COL_BQ, COL_BK, COL_BV, COL_CX, COL_CG,
 COL_DQ, COL_DK, COL_DV, COL_DZ) = range(13)
N_PROJ = 13 * GROUP_WIDTH + 2 * SMALL_W
COL_SMALL = 13 * GROUP_WIDTH // SMALL_W
LANE_FOX = 0
LANE_BETA = 4
LANE_GDEC = 8


def _cparams(*sem):
    return pltpu.CompilerParams(dimension_semantics=sem, vmem_limit_bytes=VMEM_LIMIT)


def _dot(a, b):
    return jnp.dot(a, b, preferred_element_type=F32)


def _dot_nt(a, b):
    return lax.dot_general(a, b, (((1,), (1,)), ((), ())), preferred_element_type=F32)


def _dot_tn(a, b):
    return lax.dot_general(a, b, (((0,), (0,)), ((), ())), preferred_element_type=F32)


def _split3(x):
    h = x.astype(BF16)
    r = x - h.astype(F32)
    m = r.astype(BF16)
    l = (r - m.astype(F32)).astype(BF16)
    return h, m, l


def _dot_exact_rhs(x, w_bf16):
    h, m, l = _split3(x)
    return _dot(h, w_bf16) + _dot(m, w_bf16) + _dot(l, w_bf16)


def _dot_exact_lhs(w_bf16, x):
    h, m, l = _split3(x)
    return _dot(w_bf16, h) + _dot(w_bf16, m) + _dot(w_bf16, l)


def _iota(shape, dim):
    return lax.broadcasted_iota(jnp.int32, shape, dim)


def _head_ones(n=GROUP_WIDTH):
    r = _iota((n, n), 0) // HEAD_DIM
    c = _iota((n, n), 1) // HEAD_DIM
    return r == c


def _sigmoid(x):
    return 1.0 / (1.0 + jnp.exp(-x))


def _silu(x):
    return x * _sigmoid(x)


def _log_sigmoid(x):
    return jnp.minimum(x, 0.0) - jnp.log1p(jnp.exp(-jnp.abs(x)))


def _softplus(x):
    return jnp.maximum(x, 0.0) + jnp.log1p(jnp.exp(-jnp.abs(x)))


def _gelu_tanh(x):
    return 0.5 * x * (1.0 + jnp.tanh(math.sqrt(2.0 / math.pi) * (x + 0.044715 * (x * x * x))))


def _pack_bf16_pairs(x):
    n = x.shape[1] // 2
    u = lax.bitcast_convert_type(x.astype(BF16).astype(F32), jnp.uint32)
    return u[:, :n] | (u[:, n:] >> 16)


def _unpack_bf16_pairs(p):
    hi = lax.bitcast_convert_type(p & jnp.uint32(0xFFFF0000), F32)
    lo = lax.bitcast_convert_type(p << 16, F32)
    return jnp.concatenate([hi, lo], axis=1)


def _head_mean_sq(x, ones_bf16):
    return _dot_exact_rhs(x * x, ones_bf16) * (1.0 / HEAD_DIM)


def _head_rms(x, ones_bf16):
    return x * lax.rsqrt(_head_mean_sq(x, ones_bf16) + EPS)


def _stack_heads(x):
    lane_head = _iota(x.shape, 1) // HEAD_DIM
    parts = []
    for h in range(HEADS):
        parts.append(jnp.where(lane_head == h, x, 0.0))
    return jnp.concatenate(parts, axis=0)


def _unstack_heads(xs, rows):
    out = xs[0:rows]
    for h in range(1, HEADS):
        out = out + xs[h * rows:(h + 1) * rows]
    return out


def _causal_conv4(x, prev8, w):
    r = x.shape[0]
    row8 = _iota((8, x.shape[1]), 0)
    acc = x * w[CONV_W - 1:CONV_W, :]
    for k in range(1, CONV_W):
        xs = pltpu.roll(x, k, 0)
        ps = pltpu.roll(prev8, k, 0)
        top = jnp.where(row8 < k, ps, xs[0:8])
        xs = jnp.concatenate([top, xs[8:r]], axis=0)
        acc = acc + xs * w[CONV_W - 1 - k:CONV_W - k, :]
    return acc


def _norm_matmul_kernel(x_ref, g_ref, w_ref, o_ref):
    x = x_ref[...]
    ms = jnp.mean(x * x, axis=-1, keepdims=True)
    h = (x * lax.rsqrt(ms + EPS)) * g_ref[...]
    o_ref[...] = _dot(h.astype(BF16), w_ref[...]).astype(o_ref.dtype)


def norm_matmul(x, g, w_bf16, tm, out_dtype=F32):
    m, d = x.shape
    n = w_bf16.shape[1]
    return pl.pallas_call(
        _norm_matmul_kernel,
        grid=(m // tm,),
        in_specs=[pl.BlockSpec((tm, d), lambda i: (i, 0)),
                  pl.BlockSpec((1, d), lambda i: (0, 0)),
                  pl.BlockSpec((d, n), lambda i: (0, 0))],
        out_specs=pl.BlockSpec((tm, n), lambda i: (i, 0)),
        out_shape=jax.ShapeDtypeStruct((m, n), out_dtype),
        compiler_params=_cparams("parallel"),
        name="norm_matmul",
    )(x, g.reshape(1, d), w_bf16)


def _small_prep_kernel(s_ref, p_ref, o_ref, x_ref):
    S = s_ref.shape[1]
    W = s_ref.shape[2]
    blk = SEQ_TILE
    lane = _iota((blk, W), 1)
    er = _iota((W, 2 * HEADS * W), 0)
    ec = _iota((W, 2 * HEADS * W), 1) // W
    expand = jnp.where(er == jnp.where(ec < HEADS, LANE_FOX + ec, LANE_GDEC + ec - HEADS), 1.0, 0.0).astype(BF16)
    is_fox = (lane >= LANE_FOX) & (lane < LANE_FOX + HEADS)
    is_beta = (lane >= LANE_BETA) & (lane < LANE_BETA + HEADS)
    is_gdec = (lane >= LANE_GDEC) & (lane < LANE_GDEC + HEADS)
    r = _iota((blk, blk), 0)
    c = _iota((blk, blk), 1)
    tril_all = jnp.where(r >= c, 1.0, 0.0).astype(BF16)
    tril_chunk = jnp.where((r >= c) & (r // GDN_CHUNK == c // GDN_CHUNK), 1.0, 0.0).astype(BF16)
    neg_exp_a = -jnp.exp(p_ref[1:2, :])
    carry = jnp.zeros((1, W), F32)
    for i in range(S // blk):
        sl = pl.ds(i * blk, blk)
        z = s_ref[0, sl, :] + p_ref[0:1, :]
        fox = jnp.where(is_fox, _log_sigmoid(z), 0.0)
        beta = jnp.where(is_beta, _sigmoid(z), 0.0)
        gdec = jnp.where(is_gdec, neg_exp_a * _softplus(z), 0.0)
        cf = _dot_exact_lhs(tril_all, fox) + carry
        cg = _dot_exact_lhs(tril_chunk, gdec)
        carry = cf[blk - 1:blk, :]
        o_ref[0, sl, :] = cf + cg + beta
        x_ref[0, sl, :] = _dot_exact_rhs(cf + cg, expand)


def small_prep(proj3, params):
    b, s, _ = proj3.shape
    w = SMALL_W
    return pl.pallas_call(
        _small_prep_kernel,
        grid=(b,),
        in_specs=[pl.BlockSpec((1, s, w), lambda i: (i, 0, COL_SMALL)),
                  pl.BlockSpec((8, w), lambda i: (0, 0))],
        out_specs=[pl.BlockSpec((1, s, w), lambda i: (i, 0, 0)),
                   pl.BlockSpec((1, s, 2 * HEADS * w), lambda i: (i, 0, 0))],
        out_shape=[jax.ShapeDtypeStruct((b, s, w), F32), jax.ShapeDtypeStruct((b, s, 2 * HEADS * w), F32)],
        compiler_params=_cparams("parallel"),
        name="small_prep",
    )(proj3, params)


def _hgrn2_kernel(q_ref, f_ref, i_ref, g_ref, lb_ref, o_ref, st_ref, qs, ks, vs, bs, os_):
    nb = q_ref.shape[0]
    T = q_ref.shape[1]
    C = HGRN_CHUNK
    seqs = range(nb)

    @pl.when(pl.program_id(1) == 0)
    def _():
        st_ref[...] = jnp.zeros_like(st_ref)

    same_head = _head_ones()
    ones_bf = jnp.where(same_head, 1.0, 0.0).astype(BF16)
    lb = lb_ref[...]
    log_lb = jnp.log(lb)
    r = _iota((T, T), 0)
    c = _iota((T, T), 1)
    tril_chunk = jnp.where((r >= c) & (r // C == c // C), 1.0, 0.0).astype(BF16)
    for bb in seqs:
        fl = f_ref[bb]
        c2 = jnp.log1p(-lb) + _log_sigmoid(fl)
        mx = jnp.maximum(log_lb, c2)
        log_f = mx + jnp.log1p(jnp.exp(-jnp.abs(log_lb - c2)))
        bs[bb] = _dot_exact_lhs(tril_chunk, log_f)
        qs[bb] = _silu(q_ref[bb])
        ks[bb] = (1.0 - lb) * _sigmoid(-fl)
        vs[bb] = i_ref[bb]

    trow = _iota((C, GROUP_WIDTH), 0)

    def chunk(ci, carry):
        r0 = pl.multiple_of(ci * C, C)
        qc = [qs[bb, pl.ds(r0, C), :] for bb in seqs]
        kc = [ks[bb, pl.ds(r0, C), :] for bb in seqs]
        vc = [vs[bb, pl.ds(r0, C), :] for bb in seqs]
        bc = [bs[bb, pl.ds(r0, C), :] for bb in seqs]
        st = [st_ref[bb] for bb in seqs]
        a = []
        for bb in seqs:
            parts = []
            for s in range(C):
                parts.append(jnp.where(trow >= s, jnp.exp(bc[bb] - bc[bb][s:s + 1, :]) * (qc[bb] * kc[bb][s:s + 1, :]), 0.0))
            a.append(jnp.concatenate(parts, axis=0).astype(BF16))
        sc = [_dot(a[bb], ones_bf) for bb in seqs]
        o = [_dot_nt((qc[bb] * jnp.exp(bc[bb])).astype(BF16), st[bb].astype(BF16)) for bb in seqs]
        b_last = [bc[bb][C - 1:C, :] for bb in seqs]
        upd = [_dot_tn(vc[bb].astype(BF16), (kc[bb] * jnp.exp(b_last[bb] - bc[bb])).astype(BF16)) for bb in seqs]
        for bb in seqs:
            acc = o[bb]
            for s in range(C):
                acc = acc + sc[bb][s * C:(s + 1) * C, :] * vc[bb][s:s + 1, :]
            os_[bb, pl.ds(r0, C), :] = acc
            st_ref[bb] = st[bb] * jnp.exp(b_last[bb]) + jnp.where(same_head, upd[bb], 0.0)
        return carry

    lax.fori_loop(0, T // C, chunk, 0)
    for bb in seqs:
        o_ref[bb] = _head_rms(os_[bb], ones_bf) * _silu(g_ref[bb])


def hgrn2(proj3, lb):
    b, s, _ = proj3.shape
    t = SEQ_TILE
    gw = GROUP_WIDTH
    nb = HGRN_BATCH
    col = lambda cidx: pl.BlockSpec((nb, t, gw), lambda i, j: (i, j, cidx))
    return pl.pallas_call(
        _hgrn2_kernel,
        grid=(b // nb, s // t),
        in_specs=[col(COL_AQ), col(COL_AF), col(COL_AI), col(COL_AG),
                  pl.BlockSpec((1, gw), lambda i, j: (0, 0))],
        out_specs=pl.BlockSpec((nb, t, gw), lambda i, j: (i, j, 0)),
        out_shape=jax.ShapeDtypeStruct((b, s, gw), F32),
        scratch_shapes=[pltpu.VMEM((nb, gw, gw), F32)] + [pltpu.VMEM((nb, t, gw), F32)] * 5,
        compiler_params=_cparams("parallel", "arbitrary"),
        name="hgrn2",
    )(proj3, proj3, proj3, proj3, lb.reshape(1, gw))


def _fox_kernel(q_ref, k_ref, v_ref, cx_ref, o_ref, kb_ref, vt_ref, acc_ref):
    nb = q_ref.shape[0]
    tq = q_ref.shape[1]
    tk = tq
    S = k_ref.shape[1]
    n = HEADS * tq
    qi = pl.program_id(1)
    seqs = range(nb)

    @pl.when(qi == 0)
    def _():
        for bb in seqs:
            for i in range(S // tk):
                sl = pl.ds(i * tk, tk)
                kb_ref[bb, sl, :] = k_ref[bb, sl, :].astype(BF16)
                vt_ref[bb, :, sl] = v_ref[bb, sl, :].T.astype(BF16)

    qs = [_stack_heads(q_ref[bb] * (HEAD_DIM ** -0.5)).astype(BF16) for bb in seqs]
    acc_ref[...] = jnp.zeros_like(acc_ref)
    reps = tq // SMALL_W

    def step(kb, stats, masked):
        k0 = pl.multiple_of(kb * tk, tk)
        st = [_dot_nt(kb_ref[bb, pl.ds(k0, tk), :], qs[bb]) for bb in seqs]
        for bb in seqs:
            cx = cx_ref[bb, pl.ds(k0, tk), :]
            ck = jnp.concatenate([cx[:, h * SMALL_W:(h + 1) * SMALL_W] for h in range(HEADS) for _ in range(reps)], axis=1)
            st[bb] = st[bb] - ck
            if masked:
                st[bb] = jnp.where(_iota((tk, n), 0) <= _iota((tk, n), 1) % tq, st[bb], -jnp.inf)
        m_new = [jnp.maximum(stats[bb][0], jnp.max(st[bb], axis=0, keepdims=True)) for bb in seqs]
        alpha = [jnp.exp(stats[bb][0] - m_new[bb]) for bb in seqs]
        p = [jnp.exp(st[bb] - m_new[bb]) for bb in seqs]
        l_new = [alpha[bb] * stats[bb][1] + jnp.sum(p[bb], axis=0, keepdims=True) for bb in seqs]
        pv = [_dot(vt_ref[bb, :, pl.ds(k0, tk)], p[bb].astype(BF16)) for bb in seqs]
        for bb in seqs:
            acc_ref[bb] = alpha[bb] * acc_ref[bb] + pv[bb]
        return tuple((m_new[bb], l_new[bb]) for bb in seqs)

    init = tuple((jnp.full((1, n), -jnp.inf, F32), jnp.zeros((1, n), F32)) for _ in seqs)
    stats = lax.fori_loop(0, qi, lambda kb, c: step(kb, c, False), init)
    stats = step(qi, stats, True)

    ones_bf = jnp.where(_head_ones(), 1.0, 0.0).astype(BF16)
    for bb in seqs:
        acc = acc_ref[bb] * (1.0 / stats[bb][1])
        ot = jnp.concatenate([acc[h * HEAD_DIM:(h + 1) * HEAD_DIM, h * tq:(h + 1) * tq] for h in range(HEADS)], axis=0)
        o_ref[bb] = _head_rms(ot.T, ones_bf)


def fox_attention(proj3, c_exp):
    b, s, _ = proj3.shape
    tq = SEQ_TILE
    gw = GROUP_WIDTH
    nb = FOX_BATCH
    return pl.pallas_call(
        _fox_kernel,
        grid=(b // nb, s // tq),
        in_specs=[pl.BlockSpec((nb, tq, gw), lambda i, j: (i, j, COL_BQ)),
                  pl.BlockSpec((nb, s, gw), lambda i, j: (i, 0, COL_BK)),
                  pl.BlockSpec((nb, s, gw), lambda i, j: (i, 0, COL_BV)),
                  pl.BlockSpec((nb, s, HEADS * SMALL_W), lambda i, j: (i, 0, 0))],
        out_specs=pl.BlockSpec((nb, tq, gw), lambda i, j: (i, j, 0)),
        out_shape=jax.ShapeDtypeStruct((b, s, gw), F32),
        scratch_shapes=[pltpu.VMEM((nb, s, gw), BF16),
                        pltpu.VMEM((nb, gw, s), BF16),
                        pltpu.VMEM((nb, gw, HEADS * tq), F32)],
        compiler_params=_cparams("parallel", "arbitrary"),
        name="fox_attention",
    )(proj3, proj3, proj3, c_exp)


def _rglru_kernel(x_ref, g_ref, cw_ref, cb_ref, wa_ref, ba_ref, wx_ref, bx_ref, lam_ref, o_ref, prev_ref, h_ref):
    @pl.when(pl.program_id(1) == 0)
    def _():
        prev_ref[...] = jnp.zeros_like(prev_ref)
        h_ref[...] = jnp.zeros_like(h_ref)

    x = x_ref[0]
    t = x.shape[0]
    xc = _causal_conv4(x, prev_ref[...], cw_ref[...]) + cb_ref[...]
    prev_ref[...] = x[t - 8:t, :]
    xb = xc.astype(BF16)
    r = _sigmoid(_dot(xb, wa_ref[...]) + ba_ref[...])
    ig = _sigmoid(_dot(xb, wx_ref[...]) + bx_ref[...])
    log_a = (-LRU_C * r) * _softplus(-lam_ref[...])
    a = jnp.exp(log_a)
    u = jnp.sqrt(1.0 - jnp.exp(2.0 * log_a)) * (ig * xc)
    row = _iota(a.shape, 0)
    d = 1
    while d < t:
        valid = row >= d
        u = jnp.where(valid, a * pltpu.roll(u, d, 0) + u, u)
        a = jnp.where(valid, a * pltpu.roll(a, d, 0), a)
        d *= 2
    h = a * h_ref[...] + u
    h_ref[...] = h[t - 1:t, :]
    ones_bf = jnp.where(_head_ones(), 1.0, 0.0).astype(BF16)
    o_ref[0] = _head_rms(h * _gelu_tanh(g_ref[0]), ones_bf)


def rglru(proj3, conv_w, conv_b, wa_bd, ba, wx_bd, bx, lam):
    b, s, _ = proj3.shape
    t = SEQ_TILE
    gw = GROUP_WIDTH
    row = lambda: pl.BlockSpec((1, gw), lambda i, j: (0, 0))
    return pl.pallas_call(
        _rglru_kernel,
        grid=(b, s // t),
        in_specs=[pl.BlockSpec((1, t, gw), lambda i, j: (i, j, COL_CX)),
                  pl.BlockSpec((1, t, gw), lambda i, j: (i, j, COL_CG)),
                  pl.BlockSpec((CONV_W, gw), lambda i, j: (0, 0)), row(),
                  pl.BlockSpec((gw, gw), lambda i, j: (0, 0)), row(),
                  pl.BlockSpec((gw, gw), lambda i, j: (0, 0)), row(), row()],
        out_specs=pl.BlockSpec((1, t, gw), lambda i, j: (i, j, 0)),
        out_shape=jax.ShapeDtypeStruct((b, s, gw), F32),
        scratch_shapes=[pltpu.VMEM((8, gw), F32), pltpu.VMEM((1, gw), F32)],
        compiler_params=_cparams("parallel", "arbitrary"),
        name="rglru",
    )(proj3, proj3, conv_w, conv_b.reshape(1, gw), wa_bd, ba.reshape(1, gw), wx_bd, bx.reshape(1, gw), lam.reshape(1, gw))


def _gdn_kernel(q_ref, k_ref, v_ref, z_ref, sm_ref, ge_ref, gr_ref, cw_ref, o_ref, st_ref, pq_ref, pk_ref, pv_ref):
    nb = q_ref.shape[0]
    T = q_ref.shape[1]
    C = GDN_CHUNK

    @pl.when(pl.program_id(1) == 0)
    def _():
        st_ref[...] = jnp.zeros_like(st_ref)
        pq_ref[...] = jnp.zeros_like(pq_ref)
        pk_ref[...] = jnp.zeros_like(pk_ref)
        pv_ref[...] = jnp.zeros_like(pv_ref)

    same_head = _head_ones()
    ones_bf = jnp.where(same_head, 1.0, 0.0).astype(BF16)
    cw = cw_ref[...]
    gw = GROUP_WIDTH
    er = _iota((SMALL_W, gw), 0)
    ec = _iota((SMALL_W, gw), 1) // HEAD_DIM
    exp_beta = jnp.where(er == LANE_BETA + ec, 1.0, 0.0).astype(BF16)
    exp_g = jnp.where(er == LANE_GDEC + ec, 1.0, 0.0).astype(BF16)
    n = HEADS * C
    rr = _iota((n, n), 0)
    cc = _iota((n, n), 1)
    incl = same_head & (rr % C >= cc % C)
    strict = same_head & (rr % C > cc % C)

    prep = []
    for bb in range(nb):
        xq, xk, xv = q_ref[bb], k_ref[bb], v_ref[bb]
        q = _silu(_causal_conv4(xq, pq_ref[bb], cw[:, 0:gw]))
        k = _silu(_causal_conv4(xk, pk_ref[bb], cw[:, gw:2 * gw]))
        v = _silu(_causal_conv4(xv, pv_ref[bb], cw[:, 2 * gw:3 * gw]))
        pq_ref[bb] = xq[T - 8:T, :]
        pk_ref[bb] = xk[T - 8:T, :]
        pv_ref[bb] = xv[T - 8:T, :]
        q = q * lax.rsqrt(_dot_exact_rhs(q * q, ones_bf) + EPS) * (HEAD_DIM ** -0.5)
        k = k * lax.rsqrt(_dot_exact_rhs(k * k, ones_bf) + EPS)
        sm = sm_ref[bb]
        betax = _dot_exact_rhs(sm, exp_beta)
        gx = _dot_exact_rhs(sm, exp_g)
        prep.append((q, k, v, betax, gx))

    reps = gw // SMALL_W
    n_chunks = T // C
    eye = jnp.where(rr == cc, 1.0, 0.0)

    chains = []
    for ci in range(n_chunks):
        sl = slice(ci * C, (ci + 1) * C)
        for bb in range(nb):
            q, k, v, betax, gx = prep[bb]
            qc, kc, vc, bx, gc = q[sl], k[sl], v[sl], betax[sl], gx[sl]
            grow = gr_ref[bb, ci, 0:1, :]
            ge = ge_ref[bb, sl, :]
            gcol = jnp.concatenate(
                [jnp.concatenate([ge[:, h * SMALL_W:(h + 1) * SMALL_W]] * reps, axis=1) for h in range(HEADS)], axis=0)
            eg = jnp.exp(gc)
            kb = kc * bx
            g_last = gc[C - 1:C, :]
            gamma = jnp.where(incl, jnp.exp(jnp.where(incl, gcol - grow, 0.0)), 0.0)
            ks_bf = _stack_heads(kc).astype(BF16)
            m = jnp.where(strict, _dot_nt(_stack_heads(kb).astype(BF16), ks_bf) * gamma, 0.0)
            chains.append(dict(
                x=eye - m, y=m,
                a_qk=(_dot_nt(_stack_heads(qc).astype(BF16), ks_bf) * gamma).astype(BF16),
                rhs_u=_stack_heads(vc * bx).astype(BF16),
                rhs_w=_stack_heads(kb * eg).astype(BF16),
                q_dec=_stack_heads(qc * eg).astype(BF16),
                k_dec=_stack_heads(kc * jnp.exp(g_last - gc)).astype(BF16),
                decay=jnp.exp(g_last)))
    for _ in range(5):
        for ch in chains:
            yb = ch["y"].astype(BF16)
            ch["y"] = _dot(yb, yb)
        for ch in chains:
            ch["x"] = ch["x"] + _dot(ch["x"].astype(BF16), ch["y"].astype(BF16))
    for ch in chains:
        t_inv = ch["x"].astype(BF16)
        ch["u"] = _dot(t_inv, ch["rhs_u"])
        ch["wk"] = _dot(t_inv, ch["rhs_w"]).astype(BF16)

    outs = [[] for _ in range(nb)]
    for ci in range(n_chunks):
        for bb in range(nb):
            ch = chains[ci * nb + bb]
            st = st_ref[bb]
            st_bf = st.astype(BF16)
            v_new = (ch["u"] - _dot(ch["wk"], st_bf)).astype(BF16)
            o = _dot(ch["q_dec"], st_bf) + _dot(ch["a_qk"], v_new)
            outs[bb].append(_unstack_heads(o, C))
            st_ref[bb] = st * ch["decay"] + _dot_tn(ch["k_dec"], v_new)
    for bb in range(nb):
        o = jnp.concatenate(outs[bb], axis=0)
        o_ref[bb] = _head_rms(o, ones_bf) * _silu(z_ref[bb])


def gdn(proj3, sm, c_exp, g_row, conv_w):
    b, s, _ = proj3.shape
    t = SEQ_TILE
    gw = GROUP_WIDTH
    nb = GDN_BATCH
    col = lambda cidx: pl.BlockSpec((nb, t, gw), lambda i, j: (i, j, cidx))
    return pl.pallas_call(
        _gdn_kernel,
        grid=(b // nb, s // t),
        in_specs=[col(COL_DQ), col(COL_DK), col(COL_DV), col(COL_DZ),
                  pl.BlockSpec((nb, t, SMALL_W), lambda i, j: (i, j, 0)),
                  pl.BlockSpec((nb, t, HEADS * SMALL_W), lambda i, j: (i, j, 1)),
                  pl.BlockSpec((nb, t // GDN_CHUNK, 8, gw), lambda i, j: (i, j, 0, 0)),
                  pl.BlockSpec((CONV_W, 3 * gw), lambda i, j: (0, 0))],
        out_specs=pl.BlockSpec((nb, t, gw), lambda i, j: (i, j, 0)),
        out_shape=jax.ShapeDtypeStruct((b, s, gw), F32),
        scratch_shapes=[pltpu.VMEM((nb, gw, gw), F32)] + [pltpu.VMEM((nb, 8, gw), F32)] * 3,
        compiler_params=_cparams("parallel", "arbitrary"),
        name="gdn",
    )(proj3, proj3, proj3, proj3, sm, c_exp, g_row, conv_w)


def _mix_out_kernel(x_ref, ya_ref, yb_ref, yc_ref, yd_ref, gain_ref, w_ref, o_ref):
    gw = GROUP_WIDTH
    gain = gain_ref[...]
    acc = x_ref[0]
    for i, y_ref in enumerate((ya_ref, yb_ref, yc_ref, yd_ref)):
        yg = (y_ref[0] * gain[:, i * gw:(i + 1) * gw]).astype(BF16)
        acc = acc + _dot(yg, w_ref[i * gw:(i + 1) * gw, :])
    o_ref[0] = acc


def mix_out(x3, ya, yb, yc, yd, gain, w_out_bf16):
    b, s, d = x3.shape
    t = SEQ_TILE
    gw = GROUP_WIDTH
    grp = lambda: pl.BlockSpec((1, t, gw), lambda i, j: (i, j, 0))
    return pl.pallas_call(
        _mix_out_kernel,
        grid=(b, s // t),
        in_specs=[pl.BlockSpec((1, t, d), lambda i, j: (i, j, 0)),
                  grp(), grp(), grp(), grp(),
                  pl.BlockSpec((1, 4 * gw), lambda i, j: (0, 0)),
                  pl.BlockSpec((4 * gw, d), lambda i, j: (0, 0))],
        out_specs=pl.BlockSpec((1, t, d), lambda i, j: (i, j, 0)),
        out_shape=jax.ShapeDtypeStruct((b, s, d), F32),
        compiler_params=_cparams("parallel", "parallel"),
        name="mix_out",
    )(x3, ya, yb, yc, yd, gain.reshape(1, 4 * gw), w_out_bf16)


def _mem_attn_kernel(x_ref, g_ref, wq_ref, kv_ref, wo_ref, o_ref):
    x = x_ref[0]
    t = x.shape[0]
    gw = GROUP_WIDTH
    ms = jnp.mean(x * x, axis=-1, keepdims=True)
    h = ((x * lax.rsqrt(ms + EPS)) * g_ref[...]).astype(BF16)
    q = _dot(h, wq_ref[...])
    kv = kv_ref[0]
    k = kv[:, 0:gw].astype(BF16)
    v = kv[:, gw:2 * gw].astype(BF16)
    s = _dot_nt(k, _stack_heads(q).astype(BF16)) * (HEAD_DIM ** -0.5)
    s = s - jnp.max(s, axis=0, keepdims=True)
    p = jnp.exp(s)
    p = p * (1.0 / jnp.sum(p, axis=0, keepdims=True))
    ot = _dot_tn(v, p.astype(BF16))
    oc = jnp.concatenate([ot[hh * HEAD_DIM:(hh + 1) * HEAD_DIM, hh * t:(hh + 1) * t] for hh in range(HEADS)], axis=0).T
    o_ref[0] = x + _dot(oc.astype(BF16), wo_ref[...])


def mem_attention(x3, g, wq_bf16, kv3, wo_bf16):
    b, s, d = x3.shape
    t = SEQ_TILE
    m = kv3.shape[1]
    return pl.pallas_call(
        _mem_attn_kernel,
        grid=(b, s // t),
        in_specs=[pl.BlockSpec((1, t, d), lambda i, j: (i, j, 0)),
                  pl.BlockSpec((1, d), lambda i, j: (0, 0)),
                  pl.BlockSpec((d, GROUP_WIDTH), lambda i, j: (0, 0)),
                  pl.BlockSpec((1, m, 2 * GROUP_WIDTH), lambda i, j: (i, 0, 0)),
                  pl.BlockSpec((GROUP_WIDTH, d), lambda i, j: (0, 0))],
        out_specs=pl.BlockSpec((1, t, d), lambda i, j: (i, j, 0)),
        out_shape=jax.ShapeDtypeStruct((b, s, d), F32),
        compiler_params=_cparams("parallel", "parallel"),
        name="mem_attention",
    )(x3, g.reshape(1, d), wq_bf16, kv3, wo_bf16)


def _router_kernel(x_ref, g_ref, w_ref, b_ref, h_ref, r_ref, n_ref, cnt_ref):
    @pl.when(pl.program_id(0) == 0)
    def _():
        cnt_ref[...] = jnp.zeros_like(cnt_ref)

    x = x_ref[...]
    tm = x.shape[0]
    ms = jnp.mean(x * x, axis=-1, keepdims=True)
    h = (x * lax.rsqrt(ms + EPS)) * g_ref[...]
    h_ref[...] = _pack_bf16_pairs(h)
    hh, hm, _ = _split3(h)
    w = w_ref[...]
    wh, wm, _ = _split3(w)
    logits = _dot(hh, wh) + _dot(hh, wm) + _dot(hm, wh) + b_ref[...]
    lane = _iota(logits.shape, 1)
    big = jnp.int32(1 << 30)
    neg = -jnp.inf
    is_grp = (lane >= N_EXPERTS) & (lane < N_EXPERTS + N_EXPERT_GROUPS)
    gl = jnp.where(is_grp, logits, neg)
    gmax = jnp.max(gl, axis=-1, keepdims=True)
    p_grp = 1.0 / jnp.sum(jnp.exp(gl - gmax), axis=-1, keepdims=True)
    g_sel = jnp.min(jnp.where(gl == gmax, lane, big), axis=-1, keepdims=True) - N_EXPERTS
    in_grp = (lane < N_EXPERTS) & (lane // EXPERTS_PER_GROUP == g_sel)
    el = jnp.where(in_grp, logits, neg)
    m1 = jnp.max(el, axis=-1, keepdims=True)
    i1 = jnp.min(jnp.where(el == m1, lane, big), axis=-1, keepdims=True)
    el2 = jnp.where(lane == i1, neg, el)
    m2 = jnp.max(el2, axis=-1, keepdims=True)
    i2 = jnp.min(jnp.where(el2 == m2, lane, big), axis=-1, keepdims=True)
    e21 = jnp.exp(m2 - m1)
    w1 = 1.0 / (1.0 + e21)
    w2 = e21 / (1.0 + e21)
    oh1 = lane == i1
    oh2 = lane == i2
    onehot = jnp.where(oh1 | oh2, 1.0, 0.0)
    strict = jnp.where(_iota((tm, tm), 0) > _iota((tm, tm), 1), 1.0, 0.0).astype(BF16)
    before = _dot(strict, onehot.astype(BF16)) + cnt_ref[...]
    rank1 = jnp.sum(jnp.where(oh1, before, 0.0), axis=-1, keepdims=True)
    rank2 = jnp.sum(jnp.where(oh2, before, 0.0), axis=-1, keepdims=True)
    cnt = before[tm - 1:tm, :] + onehot[tm - 1:tm, :]
    cnt_ref[...] = cnt
    n_ref[...] = jnp.broadcast_to(cnt, n_ref.shape)
    cols = (i1.astype(F32), i2.astype(F32), p_grp * w1, p_grp * w2, rank1, rank2)
    route = jnp.zeros(logits.shape, F32)
    for j, col in enumerate(cols):
        route = jnp.where(lane == j, col, route)
    r_ref[...] = route


def router(x2, g, w_router, b_router, tm):
    m, d = x2.shape
    return pl.pallas_call(
        _router_kernel,
        grid=(m // tm,),
        in_specs=[pl.BlockSpec((tm, d), lambda i: (i, 0)),
                  pl.BlockSpec((1, d), lambda i: (0, 0)),
                  pl.BlockSpec((d, SMALL_W), lambda i: (0, 0)),
                  pl.BlockSpec((1, SMALL_W), lambda i: (0, 0))],
        out_specs=[pl.BlockSpec((tm, d // 2), lambda i: (i, 0)),
                   pl.BlockSpec((tm, SMALL_W), lambda i: (i, 0)),
                   pl.BlockSpec((8, SMALL_W), lambda i: (0, 0))],
        out_shape=[jax.ShapeDtypeStruct((m, d // 2), jnp.uint32), jax.ShapeDtypeStruct((m, SMALL_W), F32),
                   jax.ShapeDtypeStruct((8, SMALL_W), F32)],
        scratch_shapes=[pltpu.VMEM((1, SMALL_W), F32)],
        compiler_params=_cparams("arbitrary"),
        name="router",
    )(x2, g.reshape(1, d), w_router, b_router)


def _experts_kernel(te_ref, nu_ref, xs_ref, wgu_ref, wdn_ref, y_ref, wgu_bf, wdn_bf):
    i = pl.program_id(0)

    @pl.when(i >= nu_ref[0])
    def _():
        y_ref[...] = jnp.zeros_like(y_ref)

    @pl.when(i < nu_ref[0])
    def _():
        prev = te_ref[jnp.maximum(i - 1, 0)]

        @pl.when((i == 0) | (te_ref[i] != prev))
        def _():
            wgu_bf[...] = wgu_ref[0, 0].astype(BF16)
            wdn_bf[...] = wdn_ref[0, 0].astype(BF16)

        gu = _dot(_unpack_bf16_pairs(xs_ref[...]).astype(BF16), wgu_bf[...])
        act = _silu(gu[:, 0:D_EXPERT]) * gu[:, D_EXPERT:2 * D_EXPERT]
        y_ref[...] = _pack_bf16_pairs(_dot(act.astype(BF16), wdn_bf[...]))


def moe_experts(xs, tile_expert, n_used, w_gu, w_dn, layer, tm):
    n_rows, dp = xs.shape
    d = 2 * dp
    tile = lambda i, te, nu: (jnp.minimum(i, nu[0] - 1), 0)
    wsel = lambda i, te, nu: (layer, te[jnp.minimum(i, nu[0] - 1)], 0, 0)
    return pl.pallas_call(
        _experts_kernel,
        grid_spec=pltpu.PrefetchScalarGridSpec(
            num_scalar_prefetch=2,
            grid=(n_rows // tm,),
            in_specs=[pl.BlockSpec((tm, dp), tile),
                      pl.BlockSpec((1, 1, d, 2 * D_EXPERT), wsel),
                      pl.BlockSpec((1, 1, D_EXPERT, d), wsel)],
            out_specs=pl.BlockSpec((tm, dp), lambda i, te, nu: (i, 0)),
            scratch_shapes=[pltpu.VMEM((d, 2 * D_EXPERT), BF16), pltpu.VMEM((D_EXPERT, d), BF16)]),
        out_shape=jax.ShapeDtypeStruct((n_rows, dp), jnp.uint32),
        compiler_params=_cparams("arbitrary"),
        name="moe_experts",
    )(tile_expert, n_used, xs, w_gu, w_dn)


def sc_gather_rows(table, idx):
    n_idx = idx.shape[0]
    width = table.shape[1]
    info = plsc.get_sparse_core_info()
    n_workers = info.num_cores * info.num_subcores
    per_worker = n_idx // n_workers
    chunk = SC_GATHER_CHUNK
    assert n_idx % (n_workers * chunk) == 0 and width % info.num_lanes == 0
    mesh = plsc.VectorSubcoreMesh(core_axis_name="c", subcore_axis_name="s")

    def body(table_hbm, idx_hbm, out_hbm, idx_v, rows_v, sem):
        wid = lax.axis_index("s") * info.num_cores + lax.axis_index("c")
        base = wid * per_worker

        @pl.loop(0, per_worker // chunk)
        def _(j):
            off = pl.multiple_of(base + j * chunk, chunk)
            pltpu.sync_copy(idx_hbm.at[pl.ds(off, chunk)], idx_v)
            pltpu.async_copy(table_hbm.at[idx_v], rows_v, sem).wait()
            pltpu.sync_copy(rows_v, out_hbm.at[pl.ds(off, chunk)])

    return pl.kernel(
        body,
        out_type=jax.ShapeDtypeStruct((n_idx, width), table.dtype),
        mesh=mesh,
        scratch_types=[pltpu.VMEM((chunk,), jnp.int32), pltpu.VMEM((chunk, width), table.dtype),
                       pltpu.SemaphoreType.DMA],
        name="sc_gather_rows",
    )(table, idx)


def sc_dispatch_rows(rows, dest, pad_idx, zero_rows):
    m, width = rows.shape
    n_pad = pad_idx.shape[0]
    info = plsc.get_sparse_core_info()
    n_workers = info.num_cores * info.num_subcores
    chunk = SC_GATHER_CHUNK
    assert m % (n_workers * chunk) == 0 and n_pad % (n_workers * chunk) == 0 and zero_rows.shape == (chunk, width)
    rows_per_worker = m // n_workers
    pad_per_worker = n_pad // n_workers
    mesh = plsc.VectorSubcoreMesh(core_axis_name="c", subcore_axis_name="s")

    def body(rows_hbm, dest_hbm, pad_hbm, zero_hbm, out_hbm, idx_v, rows_v, sem):
        wid = lax.axis_index("s") * info.num_cores + lax.axis_index("c")

        @pl.loop(0, rows_per_worker // chunk)
        def _(j):
            off = pl.multiple_of(wid * rows_per_worker + j * chunk, chunk)
            pltpu.sync_copy(rows_hbm.at[pl.ds(off, chunk)], rows_v)
            for slot in range(2):
                pltpu.sync_copy(dest_hbm.at[pl.ds(slot * m + off, chunk)], idx_v)
                pltpu.async_copy(rows_v, out_hbm.at[idx_v], sem).wait()

        pltpu.sync_copy(zero_hbm, rows_v)

        @pl.loop(0, pad_per_worker // chunk)
        def _(j):
            off = pl.multiple_of(wid * pad_per_worker + j * chunk, chunk)
            pltpu.sync_copy(pad_hbm.at[pl.ds(off, chunk)], idx_v)
            pltpu.async_copy(rows_v, out_hbm.at[idx_v], sem).wait()

    return pl.kernel(
        body,
        out_type=jax.ShapeDtypeStruct((2 * m + n_pad, width), rows.dtype),
        mesh=mesh,
        scratch_types=[pltpu.VMEM((chunk,), jnp.int32), pltpu.VMEM((chunk, width), rows.dtype),
                       pltpu.SemaphoreType.DMA],
        name="sc_dispatch_rows",
    )(rows, dest, pad_idx, zero_rows)


def _combine_rows_kernel(x_ref, r_ref, g_ref, y1_ref, y2_ref, o_ref, *, final_norm):
    r = r_ref[...]
    out = x_ref[...] + r[:, 2:3] * _unpack_bf16_pairs(y1_ref[...]) + r[:, 3:4] * _unpack_bf16_pairs(y2_ref[...])
    if final_norm:
        ms = jnp.mean(out * out, axis=-1, keepdims=True)
        out = (out * lax.rsqrt(ms + EPS)) * g_ref[...]
    o_ref[...] = out


def moe_combine_rows(x2, route, y_rows, g_final, tm, final_norm):
    m, d = x2.shape
    steps = m // tm
    return pl.pallas_call(
        functools.partial(_combine_rows_kernel, final_norm=final_norm),
        grid=(steps,),
        in_specs=[pl.BlockSpec((tm, d), lambda i: (i, 0)),
                  pl.BlockSpec((tm, SMALL_W), lambda i: (i, 0)),
                  pl.BlockSpec((1, d), lambda i: (0, 0)),
                  pl.BlockSpec((tm, d // 2), lambda i: (i, 0)),
                  pl.BlockSpec((tm, d // 2), lambda i: (i + steps, 0))],
        out_specs=pl.BlockSpec((tm, d), lambda i: (i, 0)),
        out_shape=jax.ShapeDtypeStruct((m, d), F32),
        compiler_params=_cparams("parallel"),
        name="moe_combine_rows",
    )(x2, route, g_final.reshape(1, d), y_rows, y_rows)


def hier_moe_routed(x2, g_ffn, w_router, b_router, w_gu, w_dn, layer, g_final, final_norm):
    m, d = x2.shape
    te_rows = EXPERT_TILE
    n_rows = 2 * m + N_EXPERTS * te_rows
    h, route, counts = router(x2, g_ffn, w_router, b_router, tm=512)
    cnt = counts[0, 0:N_EXPERTS].astype(jnp.int32)
    padded = (cnt + te_rows - 1) // te_rows * te_rows
    seg_end = jnp.cumsum(padded)
    seg_start = seg_end - padded
    idx = route[:, 0:2].astype(jnp.int32)
    onehot = idx[:, :, None] == jnp.arange(N_EXPERTS, dtype=jnp.int32)
    dest = (jnp.sum(jnp.where(onehot, seg_start, 0), axis=-1) + route[:, 4:6].astype(jnp.int32)).T.reshape(2 * m)
    tile_start = jnp.arange(n_rows // te_rows, dtype=jnp.int32) * te_rows
    tile_expert = jnp.minimum(jnp.sum((tile_start[:, None] >= seg_end[None, :]).astype(jnp.int32), axis=1), N_EXPERTS - 1)
    n_used = (seg_end[N_EXPERTS - 1] // te_rows).reshape(1)
    n_pad = n_rows - 2 * m
    experts = jnp.arange(N_EXPERTS, dtype=jnp.int32)
    pad_end = jnp.cumsum(padded - cnt)
    k = jnp.arange(n_pad, dtype=jnp.int32)
    k_expert = jnp.sum((k[:, None] >= pad_end[None, :]).astype(jnp.int32), axis=1)
    first_pad = seg_start + cnt - (pad_end - (padded - cnt))
    in_segment = jnp.sum(jnp.where(k_expert[:, None] == experts[None, :], first_pad[None, :], 0), axis=1) + k
    pad_idx = jnp.where(k_expert < N_EXPERTS, in_segment, seg_end[N_EXPERTS - 1] + k - pad_end[N_EXPERTS - 1])
    xs = sc_dispatch_rows(h, dest, pad_idx, jnp.zeros((SC_GATHER_CHUNK, d // 2), h.dtype))
    ys = moe_experts(xs, tile_expert, n_used, w_gu, w_dn, layer, tm=te_rows)
    y_rows = sc_gather_rows(ys, dest)
    return moe_combine_rows(x2, route, y_rows, g_final, tm=256, final_norm=final_norm)


def _w_in_prep_kernel(w_ref, o_ref):
    gw = GROUP_WIDTH
    a_end = 7 * gw
    c_start = a_end + HEADS
    c_end = c_start + 5 * gw
    z_start = c_end + 2 * HEADS
    assert a_end % SMALL_W == LANE_FOX and c_end % SMALL_W == LANE_BETA and LANE_GDEC == LANE_BETA + HEADS
    w = w_ref[...]
    o_ref[:, 0:a_end] = w[:, 0:a_end].astype(BF16)
    o_ref[:, a_end:a_end + 5 * gw] = w[:, c_start:c_end].astype(BF16)
    o_ref[:, 12 * gw:13 * gw] = w[:, z_start:z_start + gw].astype(BF16)
    blk_fox = w[:, a_end:a_end + SMALL_W]
    blk_gdn = w[:, c_end - LANE_BETA:c_end - LANE_BETA + SMALL_W]
    lane = _iota(blk_fox.shape, 1)
    small = jnp.where(lane < LANE_BETA, blk_fox, jnp.where(lane < LANE_GDEC + HEADS, blk_gdn, 0.0))
    o_ref[:, 13 * gw:13 * gw + SMALL_W] = small.astype(BF16)
    o_ref[:, 13 * gw + SMALL_W:] = jnp.zeros((w.shape[0], SMALL_W), BF16)


def w_in_prep(w_all, layer):
    _, d, n = w_all.shape
    tr = 128
    return pl.pallas_call(
        _w_in_prep_kernel,
        grid=(d // tr,),
        in_specs=[pl.BlockSpec((None, tr, n), lambda i: (layer, i, 0))],
        out_specs=pl.BlockSpec((tr, N_PROJ), lambda i: (i, 0)),
        out_shape=jax.ShapeDtypeStruct((d, N_PROJ), BF16),
        compiler_params=_cparams("parallel"),
        name="w_in_prep",
    )(w_all)


def _block_diag(w):
    h = w.shape[0]
    eye = jnp.eye(h, dtype=w.dtype)
    return (eye[:, None, :, None] * w[:, :, None, :]).reshape(h * HEAD_DIM, h * HEAD_DIM)


def kernel(x, mem, norm_mix, w_in, hgrn_lb, fox_fb, lru_conv_w, lru_conv_b, lru_wa, lru_ba, lru_wx, lru_bx, lru_lam, gdn_conv_w, gdn_a_log, gdn_dt_bias, mix_gain, w_out, norm_mem, norm_memkv, w_mq, w_mkv, w_mo, norm_ffn, w_rg, b_rg, w_re, b_re, w_e_gu, w_e_dn, norm_final):
    b, s, d = x.shape
    depth = w_in.shape[0]
    t_tok = b * s
    mlen = mem.shape[1]
    gw = GROUP_WIDTH

    lb_all = jnp.cumsum(jax.nn.softmax(hgrn_lb.astype(F32), axis=0), axis=0)
    lb_all = lb_all - lb_all[0]

    x = x.astype(F32)
    for l in range(depth):
        proj = norm_matmul(x.reshape(t_tok, d), norm_mix[l], w_in_prep(w_in, l), tm=256)
        proj3 = proj.reshape(b, s, N_PROJ)
        prm = jnp.zeros((8, SMALL_W), F32)
        prm = prm.at[0, LANE_FOX:LANE_FOX + HEADS].set(fox_fb[l].astype(F32))
        prm = prm.at[0, LANE_GDEC:LANE_GDEC + HEADS].set(gdn_dt_bias[l].astype(F32))
        prm = prm.at[1, LANE_GDEC:LANE_GDEC + HEADS].set(gdn_a_log[l].astype(F32))
        sm, c_exp = small_prep(proj3, prm)
        g_row = sm[:, :, LANE_GDEC:LANE_GDEC + HEADS].reshape(b, s // GDN_CHUNK, GDN_CHUNK, HEADS)
        g_row = jnp.swapaxes(g_row, 2, 3).reshape(b, s // GDN_CHUNK, 1, gw)
        g_row = jnp.broadcast_to(g_row, (b, s // GDN_CHUNK, 8, gw))

        ya = hgrn2(proj3, lb_all[l])
        yb = fox_attention(proj3, c_exp)
        yc = rglru(proj3, lru_conv_w[l], lru_conv_b[l], _block_diag(lru_wa[l]).astype(BF16), lru_ba[l],
                   _block_diag(lru_wx[l]).astype(BF16), lru_bx[l], lru_lam[l])
        yd = gdn(proj3, sm, c_exp, g_row, gdn_conv_w[l])
        x = mix_out(x, ya, yb, yc, yd, mix_gain[l], w_out[l].astype(BF16))

        kv = norm_matmul(mem.reshape(b * mlen, d), norm_memkv[l], w_mkv[l].astype(BF16), tm=256)
        x = mem_attention(x, norm_mem[l], w_mq[l].astype(BF16), kv.reshape(b, mlen, 2 * gw), w_mo[l].astype(BF16))

        w_router = jnp.concatenate([w_re[l], w_rg[l], jnp.zeros((d, SMALL_W - N_EXPERTS - N_EXPERT_GROUPS), F32)], axis=1)
        b_router = jnp.concatenate([b_re[l], b_rg[l], jnp.zeros((SMALL_W - N_EXPERTS - N_EXPERT_GROUPS,), F32)]).reshape(1, SMALL_W)
        x = hier_moe_routed(x.reshape(t_tok, d), norm_ffn[l], w_router, b_router, w_e_gu, w_e_dn, l,
                            norm_final, final_norm=(l == depth - 1)).reshape(b, s, d)

    return x
```

```python
import functools
import math

import jax
import jax.numpy as jnp
from jax import lax
from jax.experimental import pallas as pl
from jax.experimental.pallas import tpu as pltpu
from jax.experimental.pallas import tpu_sc as plsc

F32 = jnp.float32
BF16 = jnp.bfloat16

HEAD_DIM = 64
GROUP_WIDTH = 256
HEADS = GROUP_WIDTH // HEAD_DIM
GDN_CHUNK = 64
GDN_BATCH = 2
FOX_BATCH = 2
HGRN_BATCH = 2
HGRN_CHUNK = 16
CONV_W = 4
LRU_C = 8.0
EPS = 1e-6
N_EXPERT_GROUPS = 4
EXPERTS_PER_GROUP = 8
N_EXPERTS = N_EXPERT_GROUPS * EXPERTS_PER_GROUP
D_EXPERT = 256
SC_GATHER_CHUNK = 128
EXPERT_TILE = 512
SMALL_W = 128
SEQ_TILE = 256
ROW_TILE = 512
VMEM_LIMIT = 56 * 1024 * 1024

(COL_AQ, COL_AF, COL_AI, COL_AG, COL_BQ, COL_BK, COL_BV, COL_CX, COL_CG,
 COL_DQ, COL_DK, COL_DV, COL_DZ) = range(13)
N_PROJ = 13 * GROUP_WIDTH + 2 * SMALL_W
COL_SMALL = 13 * GROUP_WIDTH // SMALL_W
LANE_FOX = 0
LANE_BETA = 4
LANE_GDEC = 8


def _cparams(*sem):
    return pltpu.CompilerParams(dimension_semantics=sem, vmem_limit_bytes=VMEM_LIMIT)


def _dot(a, b):
    return jnp.dot(a, b, preferred_element_type=F32)


def _dot_nt(a, b):
    return lax.dot_general(a, b, (((1,), (1,)), ((), ())), preferred_element_type=F32)


def _dot_tn(a, b):
    return lax.dot_general(a, b, (((0,), (0,)), ((), ())), preferred_element_type=F32)


def _split3(x):
    h = x.astype(BF16)
    r = x - h.astype(F32)
    m = r.astype(BF16)
    l = (r - m.astype(F32)).astype(BF16)
    return h, m, l


def _dot_exact_rhs(x, w_bf16):
    h, m, l = _split3(x)
    return _dot(h, w_bf16) + _dot(m, w_bf16) + _dot(l, w_bf16)


def _dot_exact_lhs(w_bf16, x):
    h, m, l = _split3(x)
    return _dot(w_bf16, h) + _dot(w_bf16, m) + _dot(w_bf16, l)


def _iota(shape, dim):
    return lax.broadcasted_iota(jnp.int32, shape, dim)


def _head_ones(n=GROUP_WIDTH):
    r = _iota((n, n), 0) // HEAD_DIM
    c = _iota((n, n), 1) // HEAD_DIM
    return r == c


def _sigmoid(x):
    return 1.0 / (1.0 + jnp.exp(-x))


def _silu(x):
    return x * _sigmoid(x)


def _log_sigmoid(x):
    return jnp.minimum(x, 0.0) - jnp.log1p(jnp.exp(-jnp.abs(x)))


def _softplus(x):
    return jnp.maximum(x, 0.0) + jnp.log1p(jnp.exp(-jnp.abs(x)))


def _gelu_tanh(x):
    return 0.5 * x * (1.0 + jnp.tanh(math.sqrt(2.0 / math.pi) * (x + 0.044715 * (x * x * x))))


def _pack_bf16_pairs(x):
    n = x.shape[1] // 2
    u = lax.bitcast_convert_type(x.astype(BF16).astype(F32), jnp.uint32)
    return u[:, :n] | (u[:, n:] >> 16)


def _unpack_bf16_pairs(p):
    hi = lax.bitcast_convert_type(p & jnp.uint32(0xFFFF0000), F32)
    lo = lax.bitcast_convert_type(p << 16, F32)
    return jnp.concatenate([hi, lo], axis=1)


def _head_mean_sq(x, ones_bf16):
    return _dot_exact_rhs(x * x, ones_bf16) * (1.0 / HEAD_DIM)


def _head_rms(x, ones_bf16):
    return x * lax.rsqrt(_head_mean_sq(x, ones_bf16) + EPS)


def _stack_heads(x):
    lane_head = _iota(x.shape, 1) // HEAD_DIM
    parts = []
    for h in range(HEADS):
        parts.append(jnp.where(lane_head == h, x, 0.0))
    return jnp.concatenate(parts, axis=0)


def _unstack_heads(xs, rows):
    out = xs[0:rows]
    for h in range(1, HEADS):
        out = out + xs[h * rows:(h + 1) * rows]
    return out


def _causal_conv4(x, prev8, w):
    r = x.shape[0]
    row8 = _iota((8, x.shape[1]), 0)
    acc = x * w[CONV_W - 1:CONV_W, :]
    for k in range(1, CONV_W):
        xs = pltpu.roll(x, k, 0)
        ps = pltpu.roll(prev8, k, 0)
        top = jnp.where(row8 < k, ps, xs[0:8])
        xs = jnp.concatenate([top, xs[8:r]], axis=0)
        acc = acc + xs * w[CONV_W - 1 - k:CONV_W - k, :]
    return acc


def _norm_matmul_kernel(x_ref, g_ref, w_ref, o_ref):
    x = x_ref[...]
    ms = jnp.mean(x * x, axis=-1, keepdims=True)
    h = (x * lax.rsqrt(ms + EPS)) * g_ref[...]
    o_ref[...] = _dot(h.astype(BF16), w_ref[...]).astype(o_ref.dtype)


def norm_matmul(x, g, w_bf16, tm, out_dtype=F32):
    m, d = x.shape
    n = w_bf16.shape[1]
    return pl.pallas_call(
        _norm_matmul_kernel,
        grid=(m // tm,),
        in_specs=[pl.BlockSpec((tm, d), lambda i: (i, 0)),
                  pl.BlockSpec((1, d), lambda i: (0, 0)),
                  pl.BlockSpec((d, n), lambda i: (0, 0))],
        out_specs=pl.BlockSpec((tm, n), lambda i: (i, 0)),
        out_shape=jax.ShapeDtypeStruct((m, n), out_dtype),
        compiler_params=_cparams("parallel"),
        name="norm_matmul",
    )(x, g.reshape(1, d), w_bf16)


def _small_prep_kernel(s_ref, p_ref, o_ref, x_ref):
    S = s_ref.shape[1]
    W = s_ref.shape[2]
    blk = SEQ_TILE
    lane = _iota((blk, W), 1)
    er = _iota((W, 2 * HEADS * W), 0)
    ec = _iota((W, 2 * HEADS * W), 1) // W
    expand = jnp.where(er == jnp.where(ec < HEADS, LANE_FOX + ec, LANE_GDEC + ec - HEADS), 1.0, 0.0).astype(BF16)
    is_fox = (lane >= LANE_FOX) & (lane < LANE_FOX + HEADS)
    is_beta = (lane >= LANE_BETA) & (lane < LANE_BETA + HEADS)
    is_gdec = (lane >= LANE_GDEC) & (lane < LANE_GDEC + HEADS)
    r = _iota((blk, blk), 0)
    c = _iota((blk, blk), 1)
    tril_all = jnp.where(r >= c, 1.0, 0.0).astype(BF16)
    tril_chunk = jnp.where((r >= c) & (r // GDN_CHUNK == c // GDN_CHUNK), 1.0, 0.0).astype(BF16)
    neg_exp_a = -jnp.exp(p_ref[1:2, :])
    carry = jnp.zeros((1, W), F32)
    for i in range(S // blk):
        sl = pl.ds(i * blk, blk)
        z = s_ref[0, sl, :] + p_ref[0:1, :]
        fox = jnp.where(is_fox, _log_sigmoid(z), 0.0)
        beta = jnp.where(is_beta, _sigmoid(z), 0.0)
        gdec = jnp.where(is_gdec, neg_exp_a * _softplus(z), 0.0)
        cf = _dot_exact_lhs(tril_all, fox) + carry
        cg = _dot_exact_lhs(tril_chunk, gdec)
        carry = cf[blk - 1:blk, :]
        o_ref[0, sl, :] = cf + cg + beta
        x_ref[0, sl, :] = _dot_exact_rhs(cf + cg, expand)


def small_prep(proj3, params):
    b, s, _ = proj3.shape
    w = SMALL_W
    return pl.pallas_call(
        _small_prep_kernel,
        grid=(b,),
        in_specs=[pl.BlockSpec((1, s, w), lambda i: (i, 0, COL_SMALL)),
                  pl.BlockSpec((8, w), lambda i: (0, 0))],
        out_specs=[pl.BlockSpec((1, s, w), lambda i: (i, 0, 0)),
                   pl.BlockSpec((1, s, 2 * HEADS * w), lambda i: (i, 0, 0))],
        out_shape=[jax.ShapeDtypeStruct((b, s, w), F32), jax.ShapeDtypeStruct((b, s, 2 * HEADS * w), F32)],
        compiler_params=_cparams("parallel"),
        name="small_prep",
    )(proj3, params)


def _hgrn2_kernel(q_ref, f_ref, i_ref, g_ref, lb_ref, o_ref, st_ref, qs, ks, vs, bs, os_):
    nb = q_ref.shape[0]
    T = q_ref.shape[1]
    C = HGRN_CHUNK
    seqs = range(nb)

    @pl.when(pl.program_id(1) == 0)
    def _():
        st_ref[...] = jnp.zeros_like(st_ref)

    same_head = _head_ones()
    ones_bf = jnp.where(same_head, 1.0, 0.0).astype(BF16)
    lb = lb_ref[...]
    log_lb = jnp.log(lb)
    r = _iota((T, T), 0)
    c = _iota((T, T), 1)
    tril_chunk = jnp.where((r >= c) & (r // C == c // C), 1.0, 0.0).astype(BF16)
    for bb in seqs:
        fl = f_ref[bb]
        c2 = jnp.log1p(-lb) + _log_sigmoid(fl)
        mx = jnp.maximum(log_lb, c2)
        log_f = mx + jnp.log1p(jnp.exp(-jnp.abs(log_lb - c2)))
        bs[bb] = _dot_exact_lhs(tril_chunk, log_f)
        qs[bb] = _silu(q_ref[bb])
        ks[bb] = (1.0 - lb) * _sigmoid(-fl)
        vs[bb] = i_ref[bb]

    trow = _iota((C, GROUP_WIDTH), 0)

    def chunk(ci, carry):
        r0 = pl.multiple_of(ci * C, C)
        qc = [qs[bb, pl.ds(r0, C), :] for bb in seqs]
        kc = [ks[bb, pl.ds(r0, C), :] for bb in seqs]
        vc = [vs[bb, pl.ds(r0, C), :] for bb in seqs]
        bc = [bs[bb, pl.ds(r0, C), :] for bb in seqs]
        st = [st_ref[bb] for bb in seqs]
        a = []
        for bb in seqs:
            parts = []
            for s in range(C):
                parts.append(jnp.where(trow >= s, jnp.exp(bc[bb] - bc[bb][s:s + 1, :]) * (qc[bb] * kc[bb][s:s + 1, :]), 0.0))
            a.append(jnp.concatenate(parts, axis=0).astype(BF16))
        sc = [_dot(a[bb], ones_bf) for bb in seqs]
        o = [_dot_nt((qc[bb] * jnp.exp(bc[bb])).astype(BF16), st[bb].astype(BF16)) for bb in seqs]
        b_last = [bc[bb][C - 1:C, :] for bb in seqs]
        upd = [_dot_tn(vc[bb].astype(BF16), (kc[bb] * jnp.exp(b_last[bb] - bc[bb])).astype(BF16)) for bb in seqs]
        for bb in seqs:
            acc = o[bb]
            for s in range(C):
                acc = acc + sc[bb][s * C:(s + 1) * C, :] * vc[bb][s:s + 1, :]
            os_[bb, pl.ds(r0, C), :] = acc
            st_ref[bb] = st[bb] * jnp.exp(b_last[bb]) + jnp.where(same_head, upd[bb], 0.0)
        return carry

    lax.fori_loop(0, T // C, chunk, 0)
    for bb in seqs:
        o_ref[bb] = _head_rms(os_[bb], ones_bf) * _silu(g_ref[bb])


def hgrn2(proj3, lb):
    b, s, _ = proj3.shape
    t = SEQ_TILE
    gw = GROUP_WIDTH
    nb = HGRN_BATCH
    col = lambda cidx: pl.BlockSpec((nb, t, gw), lambda i, j: (i, j, cidx))
    return pl.pallas_call(
        _hgrn2_kernel,
        grid=(b // nb, s // t),
        in_specs=[col(COL_AQ), col(COL_AF), col(COL_AI), col(COL_AG),
                  pl.BlockSpec((1, gw), lambda i, j: (0, 0))],
        out_specs=pl.BlockSpec((nb, t, gw), lambda i, j: (i, j, 0)),
        out_shape=jax.ShapeDtypeStruct((b, s, gw), F32),
        scratch_shapes=[pltpu.VMEM((nb, gw, gw), F32)] + [pltpu.VMEM((nb, t, gw), F32)] * 5,
        compiler_params=_cparams("parallel", "arbitrary"),
        name="hgrn2",
    )(proj3, proj3, proj3, proj3, lb.reshape(1, gw))


def _fox_kernel(q_ref, k_ref, v_ref, cx_ref, o_ref, kb_ref, vt_ref, acc_ref):
    nb = q_ref.shape[0]
    tq = q_ref.shape[1]
    tk = tq
    S = k_ref.shape[1]
    n = HEADS * tq
    qi = pl.program_id(1)
    seqs = range(nb)

    @pl.when(qi == 0)
    def _():
        for bb in seqs:
            for i in range(S // tk):
                sl = pl.ds(i * tk, tk)
                kb_ref[bb, sl, :] = k_ref[bb, sl, :].astype(BF16)
                vt_ref[bb, :, sl] = v_ref[bb, sl, :].T.astype(BF16)

    qs = [_stack_heads(q_ref[bb] * (HEAD_DIM ** -0.5)).astype(BF16) for bb in seqs]
    acc_ref[...] = jnp.zeros_like(acc_ref)
    reps = tq // SMALL_W

    def step(kb, stats, masked):
        k0 = pl.multiple_of(kb * tk, tk)
        st = [_dot_nt(kb_ref[bb, pl.ds(k0, tk), :], qs[bb]) for bb in seqs]
        for bb in seqs:
            cx = cx_ref[bb, pl.ds(k0, tk), :]
            ck = jnp.concatenate([cx[:, h * SMALL_W:(h + 1) * SMALL_W] for h in range(HEADS) for _ in range(reps)], axis=1)
            st[bb] = st[bb] - ck
            if masked:
                st[bb] = jnp.where(_iota((tk, n), 0) <= _iota((tk, n), 1) % tq, st[bb], -jnp.inf)
        m_new = [jnp.maximum(stats[bb][0], jnp.max(st[bb], axis=0, keepdims=True)) for bb in seqs]
        alpha = [jnp.exp(stats[bb][0] - m_new[bb]) for bb in seqs]
        p = [jnp.exp(st[bb] - m_new[bb]) for bb in seqs]
        l_new = [alpha[bb] * stats[bb][1] + jnp.sum(p[bb], axis=0, keepdims=True) for bb in seqs]
        pv = [_dot(vt_ref[bb, :, pl.ds(k0, tk)], p[bb].astype(BF16)) for bb in seqs]
        for bb in seqs:
            acc_ref[bb] = alpha[bb] * acc_ref[bb] + pv[bb]
        return tuple((m_new[bb], l_new[bb]) for bb in seqs)

    init = tuple((jnp.full((1, n), -jnp.inf, F32), jnp.zeros((1, n), F32)) for _ in seqs)
    stats = lax.fori_loop(0, qi, lambda kb, c: step(kb, c, False), init)
    stats = step(qi, stats, True)

    ones_bf = jnp.where(_head_ones(), 1.0, 0.0).astype(BF16)
    for bb in seqs:
        acc = acc_ref[bb] * (1.0 / stats[bb][1])
        ot = jnp.concatenate([acc[h * HEAD_DIM:(h + 1) * HEAD_DIM, h * tq:(h + 1) * tq] for h in range(HEADS)], axis=0)
        o_ref[bb] = _head_rms(ot.T, ones_bf)


def fox_attention(proj3, c_exp):
    b, s, _ = proj3.shape
    tq = SEQ_TILE
    gw = GROUP_WIDTH
    nb = FOX_BATCH
    return pl.pallas_call(
        _fox_kernel,
        grid=(b // nb, s // tq),
        in_specs=[pl.BlockSpec((nb, tq, gw), lambda i, j: (i, j, COL_BQ)),
                  pl.BlockSpec((nb, s, gw), lambda i, j: (i, 0, COL_BK)),
                  pl.BlockSpec((nb, s, gw), lambda i, j: (i, 0, COL_BV)),
                  pl.BlockSpec((nb, s, HEADS * SMALL_W), lambda i, j: (i, 0, 0))],
        out_specs=pl.BlockSpec((nb, tq, gw), lambda i, j: (i, j, 0)),
        out_shape=jax.ShapeDtypeStruct((b, s, gw), F32),
        scratch_shapes=[pltpu.VMEM((nb, s, gw), BF16),
                        pltpu.VMEM((nb, gw, s), BF16),
                        pltpu.VMEM((nb, gw, HEADS * tq), F32)],
        compiler_params=_cparams("parallel", "arbitrary"),
        name="fox_attention",
    )(proj3, proj3, proj3, c_exp)


def _rglru_kernel(x_ref, g_ref, cw_ref, cb_ref, wa_ref, ba_ref, wx_ref, bx_ref, lam_ref, o_ref, prev_ref, h_ref):
    @pl.when(pl.program_id(1) == 0)
    def _():
        prev_ref[...] = jnp.zeros_like(prev_ref)
        h_ref[...] = jnp.zeros_like(h_ref)

    x = x_ref[0]
    t = x.shape[0]
    xc = _causal_conv4(x, prev_ref[...], cw_ref[...]) + cb_ref[...]
    prev_ref[...] = x[t - 8:t, :]
    xb = xc.astype(BF16)
    r = _sigmoid(_dot(xb, wa_ref[...]) + ba_ref[...])
    ig = _sigmoid(_dot(xb, wx_ref[...]) + bx_ref[...])
    log_a = (-LRU_C * r) * _softplus(-lam_ref[...])
    a = jnp.exp(log_a)
    u = jnp.sqrt(1.0 - jnp.exp(2.0 * log_a)) * (ig * xc)
    row = _iota(a.shape, 0)
    d = 1
    while d < t:
        valid = row >= d
        u = jnp.where(valid, a * pltpu.roll(u, d, 0) + u, u)
        a = jnp.where(valid, a * pltpu.roll(a, d, 0), a)
        d *= 2
    h = a * h_ref[...] + u
    h_ref[...] = h[t - 1:t, :]
    ones_bf = jnp.where(_head_ones(), 1.0, 0.0).astype(BF16)
    o_ref[0] = _head_rms(h * _gelu_tanh(g_ref[0]), ones_bf)


def rglru(proj3, conv_w, conv_b, wa_bd, ba, wx_bd, bx, lam):
    b, s, _ = proj3.shape
    t = ROW_TILE
    gw = GROUP_WIDTH
    row = lambda: pl.BlockSpec((1, gw), lambda i, j: (0, 0))
    return pl.pallas_call(
        _rglru_kernel,
        grid=(b, s // t),
        in_specs=[pl.BlockSpec((1, t, gw), lambda i, j: (i, j, COL_CX)),
                  pl.BlockSpec((1, t, gw), lambda i, j: (i, j, COL_CG)),
                  pl.BlockSpec((CONV_W, gw), lambda i, j: (0, 0)), row(),
                  pl.BlockSpec((gw, gw), lambda i, j: (0, 0)), row(),
                  pl.BlockSpec((gw, gw), lambda i, j: (0, 0)), row(), row()],
        out_specs=pl.BlockSpec((1, t, gw), lambda i, j: (i, j, 0)),
        out_shape=jax.ShapeDtypeStruct((b, s, gw), F32),
        scratch_shapes=[pltpu.VMEM((8, gw), F32), pltpu.VMEM((1, gw), F32)],
        compiler_params=_cparams("parallel", "arbitrary"),
        name="rglru",
    )(proj3, proj3, conv_w, conv_b.reshape(1, gw), wa_bd, ba.reshape(1, gw), wx_bd, bx.reshape(1, gw), lam.reshape(1, gw))


def _gdn_kernel(q_ref, k_ref, v_ref, z_ref, sm_ref, ge_ref, gr_ref, cw_ref, o_ref, st_ref, pq_ref, pk_ref, pv_ref):
    nb = q_ref.shape[0]
    T = q_ref.shape[1]
    C = GDN_CHUNK

    @pl.when(pl.program_id(1) == 0)
    def _():
        st_ref[...] = jnp.zeros_like(st_ref)
        pq_ref[...] = jnp.zeros_like(pq_ref)
        pk_ref[...] = jnp.zeros_like(pk_ref)
        pv_ref[...] = jnp.zeros_like(pv_ref)

    same_head = _head_ones()
    ones_bf = jnp.where(same_head, 1.0, 0.0).astype(BF16)
    cw = cw_ref[...]
    gw = GROUP_WIDTH
    er = _iota((SMALL_W, gw), 0)
    ec = _iota((SMALL_W, gw), 1) // HEAD_DIM
    exp_beta = jnp.where(er == LANE_BETA + ec, 1.0, 0.0).astype(BF16)
    exp_g = jnp.where(er == LANE_GDEC + ec, 1.0, 0.0).astype(BF16)
    n = HEADS * C
    rr = _iota((n, n), 0)
    cc = _iota((n, n), 1)
    incl = same_head & (rr % C >= cc % C)
    strict = same_head & (rr % C > cc % C)

    prep = []
    for bb in range(nb):
        xq, xk, xv = q_ref[bb], k_ref[bb], v_ref[bb]
        q = _silu(_causal_conv4(xq, pq_ref[bb], cw[:, 0:gw]))
        k = _silu(_causal_conv4(xk, pk_ref[bb], cw[:, gw:2 * gw]))
        v = _silu(_causal_conv4(xv, pv_ref[bb], cw[:, 2 * gw:3 * gw]))
        pq_ref[bb] = xq[T - 8:T, :]
        pk_ref[bb] = xk[T - 8:T, :]
        pv_ref[bb] = xv[T - 8:T, :]
        q = q * lax.rsqrt(_dot_exact_rhs(q * q, ones_bf) + EPS) * (HEAD_DIM ** -0.5)
        k = k * lax.rsqrt(_dot_exact_rhs(k * k, ones_bf) + EPS)
        sm = sm_ref[bb]
        betax = _dot_exact_rhs(sm, exp_beta)
        gx = _dot_exact_rhs(sm, exp_g)
        prep.append((q, k, v, betax, gx))

    reps = gw // SMALL_W
    n_chunks = T // C
    eye = jnp.where(rr == cc, 1.0, 0.0)

    chains = []
    for ci in range(n_chunks):
        sl = slice(ci * C, (ci + 1) * C)
        for bb in range(nb):
            q, k, v, betax, gx = prep[bb]
            qc, kc, vc, bx, gc = q[sl], k[sl], v[sl], betax[sl], gx[sl]
            grow = gr_ref[bb, ci, 0:1, :]
            ge = ge_ref[bb, sl, :]
            gcol = jnp.concatenate(
                [jnp.concatenate([ge[:, h * SMALL_W:(h + 1) * SMALL_W]] * reps, axis=1) for h in range(HEADS)], axis=0)
            eg = jnp.exp(gc)
            kb = kc * bx
            g_last = gc[C - 1:C, :]
            gamma = jnp.where(incl, jnp.exp(jnp.where(incl, gcol - grow, 0.0)), 0.0)
            ks_bf = _stack_heads(kc).astype(BF16)
            m = jnp.where(strict, _dot_nt(_stack_heads(kb).astype(BF16), ks_bf) * gamma, 0.0)
            chains.append(dict(
                x=eye - m, y=m,
                a_qk=(_dot_nt(_stack_heads(qc).astype(BF16), ks_bf) * gamma).astype(BF16),
                rhs_u=_stack_heads(vc * bx).astype(BF16),
                rhs_w=_stack_heads(kb * eg).astype(BF16),
                q_dec=_stack_heads(qc * eg).astype(BF16),
                k_dec=_stack_heads(kc * jnp.exp(g_last - gc)).astype(BF16),
                decay=jnp.exp(g_last)))
    for _ in range(5):
        for ch in chains:
            yb = ch["y"].astype(BF16)
            ch["y"] = _dot(yb, yb)
        for ch in chains:
            ch["x"] = ch["x"] + _dot(ch["x"].astype(BF16), ch["y"].astype(BF16))
    for ch in chains:
        t_inv = ch["x"].astype(BF16)
        ch["u"] = _dot(t_inv, ch["rhs_u"])
        ch["wk"] = _dot(t_inv, ch["rhs_w"]).astype(BF16)

    outs = [[] for _ in range(nb)]
    for ci in range(n_chunks):
        for bb in range(nb):
            ch = chains[ci * nb + bb]
            st = st_ref[bb]
            st_bf = st.astype(BF16)
            v_new = (ch["u"] - _dot(ch["wk"], st_bf)).astype(BF16)
            o = _dot(ch["q_dec"], st_bf) + _dot(ch["a_qk"], v_new)
            outs[bb].append(_unstack_heads(o, C))
            st_ref[bb] = st * ch["decay"] + _dot_tn(ch["k_dec"], v_new)
    for bb in range(nb):
        o = jnp.concatenate(outs[bb], axis=0)
        o_ref[bb] = _head_rms(o, ones_bf) * _silu(z_ref[bb])


def gdn(proj3, sm, c_exp, g_row, conv_w):
    b, s, _ = proj3.shape
    t = SEQ_TILE
    gw = GROUP_WIDTH
    nb = GDN_BATCH
    col = lambda cidx: pl.BlockSpec((nb, t, gw), lambda i, j: (i, j, cidx))
    return pl.pallas_call(
        _gdn_kernel,
        grid=(b // nb, s // t),
        in_specs=[col(COL_DQ), col(COL_DK), col(COL_DV), col(COL_DZ),
                  pl.BlockSpec((nb, t, SMALL_W), lambda i, j: (i, j, 0)),
                  pl.BlockSpec((nb, t, HEADS * SMALL_W), lambda i, j: (i, j, 1)),
                  pl.BlockSpec((nb, t // GDN_CHUNK, 8, gw), lambda i, j: (i, j, 0, 0)),
                  pl.BlockSpec((CONV_W, 3 * gw), lambda i, j: (0, 0))],
        out_specs=pl.BlockSpec((nb, t, gw), lambda i, j: (i, j, 0)),
        out_shape=jax.ShapeDtypeStruct((b, s, gw), F32),
        scratch_shapes=[pltpu.VMEM((nb, gw, gw), F32)] + [pltpu.VMEM((nb, 8, gw), F32)] * 3,
        compiler_params=_cparams("parallel", "arbitrary"),
        name="gdn",
    )(proj3, proj3, proj3, proj3, sm, c_exp, g_row, conv_w)


def _mix_out_kernel(x_ref, ya_ref, yb_ref, yc_ref, yd_ref, gain_ref, w_ref, o_ref):
    gw = GROUP_WIDTH
    gain = gain_ref[...]
    acc = x_ref[0]
    for i, y_ref in enumerate((ya_ref, yb_ref, yc_ref, yd_ref)):
        yg = (y_ref[0] * gain[:, i * gw:(i + 1) * gw]).astype(BF16)
        acc = acc + _dot(yg, w_ref[i * gw:(i + 1) * gw, :])
    o_ref[0] = acc


def mix_out(x3, ya, yb, yc, yd, gain, w_out_bf16):
    b, s, d = x3.shape
    t = ROW_TILE
    gw = GROUP_WIDTH
    grp = lambda: pl.BlockSpec((1, t, gw), lambda i, j: (i, j, 0))
    return pl.pallas_call(
        _mix_out_kernel,
        grid=(b, s // t),
        in_specs=[pl.BlockSpec((1, t, d), lambda i, j: (i, j, 0)),
                  grp(), grp(), grp(), grp(),
                  pl.BlockSpec((1, 4 * gw), lambda i, j: (0, 0)),
                  pl.BlockSpec((4 * gw, d), lambda i, j: (0, 0))],
        out_specs=pl.BlockSpec((1, t, d), lambda i, j: (i, j, 0)),
        out_shape=jax.ShapeDtypeStruct((b, s, d), F32),
        compiler_params=_cparams("parallel", "parallel"),
        name="mix_out",
    )(x3, ya, yb, yc, yd, gain.reshape(1, 4 * gw), w_out_bf16)


def _mem_attn_kernel(x_ref, g_ref, wq_ref, kv_ref, wo_ref, o_ref):
    x = x_ref[0]
    t = x.shape[0]
    gw = GROUP_WIDTH
    ms = jnp.mean(x * x, axis=-1, keepdims=True)
    h = ((x * lax.rsqrt(ms + EPS)) * g_ref[...]).astype(BF16)
    q = _dot(h, wq_ref[...])
    kv = kv_ref[0]
    k = kv[:, 0:gw].astype(BF16)
    v = kv[:, gw:2 * gw].astype(BF16)
    s = _dot_nt(k, _stack_heads(q).astype(BF16)) * (HEAD_DIM ** -0.5)
    s = s - jnp.max(s, axis=0, keepdims=True)
    p = jnp.exp(s)
    p = p * (1.0 / jnp.sum(p, axis=0, keepdims=True))
    ot = _dot_tn(v, p.astype(BF16))
    oc = jnp.concatenate([ot[hh * HEAD_DIM:(hh + 1) * HEAD_DIM, hh * t:(hh + 1) * t] for hh in range(HEADS)], axis=0).T
    o_ref[0] = x + _dot(oc.astype(BF16), wo_ref[...])


def mem_attention(x3, g, wq_bf16, kv3, wo_bf16):
    b, s, d = x3.shape
    t = ROW_TILE
    m = kv3.shape[1]
    return pl.pallas_call(
        _mem_attn_kernel,
        grid=(b, s // t),
        in_specs=[pl.BlockSpec((1, t, d), lambda i, j: (i, j, 0)),
                  pl.BlockSpec((1, d), lambda i, j: (0, 0)),
                  pl.BlockSpec((d, GROUP_WIDTH), lambda i, j: (0, 0)),
                  pl.BlockSpec((1, m, 2 * GROUP_WIDTH), lambda i, j: (i, 0, 0)),
                  pl.BlockSpec((GROUP_WIDTH, d), lambda i, j: (0, 0))],
        out_specs=pl.BlockSpec((1, t, d), lambda i, j: (i, j, 0)),
        out_shape=jax.ShapeDtypeStruct((b, s, d), F32),
        compiler_params=_cparams("parallel", "parallel"),
        name="mem_attention",
    )(x3, g.reshape(1, d), wq_bf16, kv3, wo_bf16)


def _router_kernel(x_ref, g_ref, w_ref, b_ref, h_ref, r_ref, n_ref, cnt_ref):
    @pl.when(pl.program_id(0) == 0)
    def _():
        cnt_ref[...] = jnp.zeros_like(cnt_ref)

    x = x_ref[...]
    tm = x.shape[0]
    ms = jnp.mean(x * x, axis=-1, keepdims=True)
    h = (x * lax.rsqrt(ms + EPS)) * g_ref[...]
    h_ref[...] = _pack_bf16_pairs(h)
    hh, hm, _ = _split3(h)
    w = w_ref[...]
    wh, wm, _ = _split3(w)
    logits = _dot(hh, wh) + _dot(hh, wm) + _dot(hm, wh) + b_ref[...]
    lane = _iota(logits.shape, 1)
    big = jnp.int32(1 << 30)
    neg = -jnp.inf
    is_grp = (lane >= N_EXPERTS) & (lane < N_EXPERTS + N_EXPERT_GROUPS)
    gl = jnp.where(is_grp, logits, neg)
    gmax = jnp.max(gl, axis=-1, keepdims=True)
    p_grp = 1.0 / jnp.sum(jnp.exp(gl - gmax), axis=-1, keepdims=True)
    g_sel = jnp.min(jnp.where(gl == gmax, lane, big), axis=-1, keepdims=True) - N_EXPERTS
    in_grp = (lane < N_EXPERTS) & (lane // EXPERTS_PER_GROUP == g_sel)
    el = jnp.where(in_grp, logits, neg)
    m1 = jnp.max(el, axis=-1, keepdims=True)
    i1 = jnp.min(jnp.where(el == m1, lane, big), axis=-1, keepdims=True)
    el2 = jnp.where(lane == i1, neg, el)
    m2 = jnp.max(el2, axis=-1, keepdims=True)
    i2 = jnp.min(jnp.where(el2 == m2, lane, big), axis=-1, keepdims=True)
    e21 = jnp.exp(m2 - m1)
    w1 = 1.0 / (1.0 + e21)
    w2 = e21 / (1.0 + e21)
    oh1 = lane == i1
    oh2 = lane == i2
    onehot = jnp.where(oh1 | oh2, 1.0, 0.0)
    strict = jnp.where(_iota((tm, tm), 0) > _iota((tm, tm), 1), 1.0, 0.0).astype(BF16)
    before = _dot(strict, onehot.astype(BF16)) + cnt_ref[...]
    rank1 = jnp.sum(jnp.where(oh1, before, 0.0), axis=-1, keepdims=True)
    rank2 = jnp.sum(jnp.where(oh2, before, 0.0), axis=-1, keepdims=True)
    cnt = before[tm - 1:tm, :] + onehot[tm - 1:tm, :]
    cnt_ref[...] = cnt
    n_ref[...] = jnp.broadcast_to(cnt, n_ref.shape)
    cols = (i1.astype(F32), i2.astype(F32), p_grp * w1, p_grp * w2, rank1, rank2)
    route = jnp.zeros(logits.shape, F32)
    for j, col in enumerate(cols):
        route = jnp.where(lane == j, col, route)
    r_ref[...] = route


def router(x2, g, w_router, b_router, tm):
    m, d = x2.shape
    return pl.pallas_call(
        _router_kernel,
        grid=(m // tm,),
        in_specs=[pl.BlockSpec((tm, d), lambda i: (i, 0)),
                  pl.BlockSpec((1, d), lambda i: (0, 0)),
                  pl.BlockSpec((d, SMALL_W), lambda i: (0, 0)),
                  pl.BlockSpec((1, SMALL_W), lambda i: (0, 0))],
        out_specs=[pl.BlockSpec((tm, d // 2), lambda i: (i, 0)),
                   pl.BlockSpec((tm, SMALL_W), lambda i: (i, 0)),
                   pl.BlockSpec((8, SMALL_W), lambda i: (0, 0))],
        out_shape=[jax.ShapeDtypeStruct((m, d // 2), jnp.uint32), jax.ShapeDtypeStruct((m, SMALL_W), F32),
                   jax.ShapeDtypeStruct((8, SMALL_W), F32)],
        scratch_shapes=[pltpu.VMEM((1, SMALL_W), F32)],
        compiler_params=_cparams("arbitrary"),
        name="router",
    )(x2, g.reshape(1, d), w_router, b_router)


def _experts_kernel(te_ref, nu_ref, xs_ref, wgu_ref, wdn_ref, y_ref, wgu_bf, wdn_bf):
    i = pl.program_id(0)

    @pl.when(i >= nu_ref[0])
    def _():
        y_ref[...] = jnp.zeros_like(y_ref)

    @pl.when(i < nu_ref[0])
    def _():
        prev = te_ref[jnp.maximum(i - 1, 0)]

        @pl.when((i == 0) | (te_ref[i] != prev))
        def _():
            wgu_bf[...] = wgu_ref[0, 0].astype(BF16)
            wdn_bf[...] = wdn_ref[0, 0].astype(BF16)

        gu = _dot(_unpack_bf16_pairs(xs_ref[...]).astype(BF16), wgu_bf[...])
        act = _silu(gu[:, 0:D_EXPERT]) * gu[:, D_EXPERT:2 * D_EXPERT]
        y_ref[...] = _pack_bf16_pairs(_dot(act.astype(BF16), wdn_bf[...]))


def moe_experts(xs, tile_expert, n_used, w_gu, w_dn, layer, tm):
    n_rows, dp = xs.shape
    d = 2 * dp
    tile = lambda i, te, nu: (jnp.minimum(i, nu[0] - 1), 0)
    wsel = lambda i, te, nu: (layer, te[jnp.minimum(i, nu[0] - 1)], 0, 0)
    return pl.pallas_call(
        _experts_kernel,
        grid_spec=pltpu.PrefetchScalarGridSpec(
            num_scalar_prefetch=2,
            grid=(n_rows // tm,),
            in_specs=[pl.BlockSpec((tm, dp), tile),
                      pl.BlockSpec((1, 1, d, 2 * D_EXPERT), wsel),
                      pl.BlockSpec((1, 1, D_EXPERT, d), wsel)],
            out_specs=pl.BlockSpec((tm, dp), lambda i, te, nu: (i, 0)),
            scratch_shapes=[pltpu.VMEM((d, 2 * D_EXPERT), BF16), pltpu.VMEM((D_EXPERT, d), BF16)]),
        out_shape=jax.ShapeDtypeStruct((n_rows, dp), jnp.uint32),
        compiler_params=_cparams("arbitrary"),
        name="moe_experts",
    )(tile_expert, n_used, xs, w_gu, w_dn)


def sc_gather_rows(table, idx):
    n_idx = idx.shape[0]
    width = table.shape[1]
    info = plsc.get_sparse_core_info()
    n_workers = info.num_cores * info.num_subcores
    per_worker = n_idx // n_workers
    chunk = SC_GATHER_CHUNK
    assert n_idx % (n_workers * chunk) == 0 and width % info.num_lanes == 0
    mesh = plsc.VectorSubcoreMesh(core_axis_name="c", subcore_axis_name="s")

    def body(table_hbm, idx_hbm, out_hbm, idx_v, rows_v, sem):
        wid = lax.axis_index("s") * info.num_cores + lax.axis_index("c")
        base = wid * per_worker

        @pl.loop(0, per_worker // chunk)
        def _(j):
            off = pl.multiple_of(base + j * chunk, chunk)
            pltpu.sync_copy(idx_hbm.at[pl.ds(off, chunk)], idx_v)
            pltpu.async_copy(table_hbm.at[idx_v], rows_v, sem).wait()
            pltpu.sync_copy(rows_v, out_hbm.at[pl.ds(off, chunk)])

    return pl.kernel(
        body,
        out_type=jax.ShapeDtypeStruct((n_idx, width), table.dtype),
        mesh=mesh,
        scratch_types=[pltpu.VMEM((chunk,), jnp.int32), pltpu.VMEM((chunk, width), table.dtype),
                       pltpu.SemaphoreType.DMA],
        name="sc_gather_rows",
    )(table, idx)


def sc_dispatch_rows(rows, dest, pad_idx, zero_rows):
    m, width = rows.shape
    n_pad = pad_idx.shape[0]
    info = plsc.get_sparse_core_info()
    n_workers = info.num_cores * info.num_subcores
    chunk = SC_GATHER_CHUNK
    assert m % (n_workers * chunk) == 0 and n_pad % (n_workers * chunk) == 0 and zero_rows.shape == (chunk, width)
    rows_per_worker = m // n_workers
    pad_per_worker = n_pad // n_workers
    mesh = plsc.VectorSubcoreMesh(core_axis_name="c", subcore_axis_name="s")

    def body(rows_hbm, dest_hbm, pad_hbm, zero_hbm, out_hbm, idx_v, rows_v, sem):
        wid = lax.axis_index("s") * info.num_cores + lax.axis_index("c")

        @pl.loop(0, rows_per_worker // chunk)
        def _(j):
            off = pl.multiple_of(wid * rows_per_worker + j * chunk, chunk)
            pltpu.sync_copy(rows_hbm.at[pl.ds(off, chunk)], rows_v)
            for slot in range(2):
                pltpu.sync_copy(dest_hbm.at[pl.ds(slot * m + off, chunk)], idx_v)
                pltpu.async_copy(rows_v, out_hbm.at[idx_v], sem).wait()

        pltpu.sync_copy(zero_hbm, rows_v)

        @pl.loop(0, pad_per_worker // chunk)
        def _(j):
            off = pl.multiple_of(wid * pad_per_worker + j * chunk, chunk)
            pltpu.sync_copy(pad_hbm.at[pl.ds(off, chunk)], idx_v)
            pltpu.async_copy(rows_v, out_hbm.at[idx_v], sem).wait()

    return pl.kernel(
        body,
        out_type=jax.ShapeDtypeStruct((2 * m + n_pad, width), rows.dtype),
        mesh=mesh,
        scratch_types=[pltpu.VMEM((chunk,), jnp.int32), pltpu.VMEM((chunk, width), rows.dtype),
                       pltpu.SemaphoreType.DMA],
        name="sc_dispatch_rows",
    )(rows, dest, pad_idx, zero_rows)


def _combine_rows_kernel(x_ref, r_ref, g_ref, y1_ref, y2_ref, o_ref, *, final_norm):
    r = r_ref[...]
    out = x_ref[...] + r[:, 2:3] * _unpack_bf16_pairs(y1_ref[...]) + r[:, 3:4] * _unpack_bf16_pairs(y2_ref[...])
    if final_norm:
        ms = jnp.mean(out * out, axis=-1, keepdims=True)
        out = (out * lax.rsqrt(ms + EPS)) * g_ref[...]
    o_ref[...] = out


def moe_combine_rows(x2, route, y_rows, g_final, tm, final_norm):
    m, d = x2.shape
    steps = m // tm
    return pl.pallas_call(
        functools.partial(_combine_rows_kernel, final_norm=final_norm),
        grid=(steps,),
        in_specs=[pl.BlockSpec((tm, d), lambda i: (i, 0)),
                  pl.BlockSpec((tm, SMALL_W), lambda i: (i, 0)),
                  pl.BlockSpec((1, d), lambda i: (0, 0)),
                  pl.BlockSpec((tm, d // 2), lambda i: (i, 0)),
                  pl.BlockSpec((tm, d // 2), lambda i: (i + steps, 0))],
        out_specs=pl.BlockSpec((tm, d), lambda i: (i, 0)),
        out_shape=jax.ShapeDtypeStruct((m, d), F32),
        compiler_params=_cparams("parallel"),
        name="moe_combine_rows",
    )(x2, route, g_final.reshape(1, d), y_rows, y_rows)


def hier_moe_routed(x2, g_ffn, w_router, b_router, w_gu, w_dn, layer, g_final, final_norm):
    m, d = x2.shape
    te_rows = EXPERT_TILE
    n_rows = 2 * m + N_EXPERTS * te_rows
    h, route, counts = router(x2, g_ffn, w_router, b_router, tm=ROW_TILE)
    cnt = counts[0, 0:N_EXPERTS].astype(jnp.int32)
    padded = (cnt + te_rows - 1) // te_rows * te_rows
    seg_end = jnp.cumsum(padded)
    seg_start = seg_end - padded
    idx = route[:, 0:2].astype(jnp.int32)
    onehot = idx[:, :, None] == jnp.arange(N_EXPERTS, dtype=jnp.int32)
    dest = (jnp.sum(jnp.where(onehot, seg_start, 0), axis=-1) + route[:, 4:6].astype(jnp.int32)).T.reshape(2 * m)
    tile_start = jnp.arange(n_rows // te_rows, dtype=jnp.int32) * te_rows
    tile_expert = jnp.minimum(jnp.sum((tile_start[:, None] >= seg_end[None, :]).astype(jnp.int32), axis=1), N_EXPERTS - 1)
    n_used = (seg_end[N_EXPERTS - 1] // te_rows).reshape(1)
    n_pad = n_rows - 2 * m
    experts = jnp.arange(N_EXPERTS, dtype=jnp.int32)
    pad_end = jnp.cumsum(padded - cnt)
    k = jnp.arange(n_pad, dtype=jnp.int32)
    k_expert = jnp.sum((k[:, None] >= pad_end[None, :]).astype(jnp.int32), axis=1)
    first_pad = seg_start + cnt - (pad_end - (padded - cnt))
    in_segment = jnp.sum(jnp.where(k_expert[:, None] == experts[None, :], first_pad[None, :], 0), axis=1) + k
    pad_idx = jnp.where(k_expert < N_EXPERTS, in_segment, seg_end[N_EXPERTS - 1] + k - pad_end[N_EXPERTS - 1])
    xs = sc_dispatch_rows(h, dest, pad_idx, jnp.zeros((SC_GATHER_CHUNK, d // 2), h.dtype))
    ys = moe_experts(xs, tile_expert, n_used, w_gu, w_dn, layer, tm=te_rows)
    y_rows = sc_gather_rows(ys, dest)
    return moe_combine_rows(x2, route, y_rows, g_final, tm=ROW_TILE, final_norm=final_norm)


def _w_in_prep_kernel(w_ref, o_ref):
    gw = GROUP_WIDTH
    a_end = 7 * gw
    c_start = a_end + HEADS
    c_end = c_start + 5 * gw
    z_start = c_end + 2 * HEADS
    assert a_end % SMALL_W == LANE_FOX and c_end % SMALL_W == LANE_BETA and LANE_GDEC == LANE_BETA + HEADS
    w = w_ref[...]
    o_ref[:, 0:a_end] = w[:, 0:a_end].astype(BF16)
    o_ref[:, a_end:a_end + 5 * gw] = w[:, c_start:c_end].astype(BF16)
    o_ref[:, 12 * gw:13 * gw] = w[:, z_start:z_start + gw].astype(BF16)
    blk_fox = w[:, a_end:a_end + SMALL_W]
    blk_gdn = w[:, c_end - LANE_BETA:c_end - LANE_BETA + SMALL_W]
    lane = _iota(blk_fox.shape, 1)
    small = jnp.where(lane < LANE_BETA, blk_fox, jnp.where(lane < LANE_GDEC + HEADS, blk_gdn, 0.0))
    o_ref[:, 13 * gw:13 * gw + SMALL_W] = small.astype(BF16)
    o_ref[:, 13 * gw + SMALL_W:] = jnp.zeros((w.shape[0], SMALL_W), BF16)


def w_in_prep(w_all, layer):
    _, d, n = w_all.shape
    tr = 128
    return pl.pallas_call(
        _w_in_prep_kernel,
        grid=(d // tr,),
        in_specs=[pl.BlockSpec((None, tr, n), lambda i: (layer, i, 0))],
        out_specs=pl.BlockSpec((tr, N_PROJ), lambda i: (i, 0)),
        out_shape=jax.ShapeDtypeStruct((d, N_PROJ), BF16),
        compiler_params=_cparams("parallel"),
        name="w_in_prep",
    )(w_all)


def _block_diag(w):
    h = w.shape[0]
    eye = jnp.eye(h, dtype=w.dtype)
    return (eye[:, None, :, None] * w[:, :, None, :]).reshape(h * HEAD_DIM, h * HEAD_DIM)


def kernel(x, mem, norm_mix, w_in, hgrn_lb, fox_fb, lru_conv_w, lru_conv_b, lru_wa, lru_ba, lru_wx, lru_bx, lru_lam, gdn_conv_w, gdn_a_log, gdn_dt_bias, mix_gain, w_out, norm_mem, norm_memkv, w_mq, w_mkv, w_mo, norm_ffn, w_rg, b_rg, w_re, b_re, w_e_gu, w_e_dn, norm_final):
    b, s, d = x.shape
    depth = w_in.shape[0]
    t_tok = b * s
    mlen = mem.shape[1]
    gw = GROUP_WIDTH

    lb_all = jnp.cumsum(jax.nn.softmax(hgrn_lb.astype(F32), axis=0), axis=0)
    lb_all = lb_all - lb_all[0]

    x = x.astype(F32)
    for l in range(depth):
        proj = norm_matmul(x.reshape(t_tok, d), norm_mix[l], w_in_prep(w_in, l), tm=ROW_TILE)
        proj3 = proj.reshape(b, s, N_PROJ)
        prm = jnp.zeros((8, SMALL_W), F32)
        prm = prm.at[0, LANE_FOX:LANE_FOX + HEADS].set(fox_fb[l].astype(F32))
        prm = prm.at[0, LANE_GDEC:LANE_GDEC + HEADS].set(gdn_dt_bias[l].astype(F32))
        prm = prm.at[1, LANE_GDEC:LANE_GDEC + HEADS].set(gdn_a_log[l].astype(F32))
        sm, c_exp = small_prep(proj3, prm)
        g_row = sm[:, :, LANE_GDEC:LANE_GDEC + HEADS].reshape(b, s // GDN_CHUNK, GDN_CHUNK, HEADS)
        g_row = jnp.swapaxes(g_row, 2, 3).reshape(b, s // GDN_CHUNK, 1, gw)
        g_row = jnp.broadcast_to(g_row, (b, s // GDN_CHUNK, 8, gw))

        ya = hgrn2(proj3, lb_all[l])
        yb = fox_attention(proj3, c_exp)
        yc = rglru(proj3, lru_conv_w[l], lru_conv_b[l], _block_diag(lru_wa[l]).astype(BF16), lru_ba[l],
                   _block_diag(lru_wx[l]).astype(BF16), lru_bx[l], lru_lam[l])
        yd = gdn(proj3, sm, c_exp, g_row, gdn_conv_w[l])
        x = mix_out(x, ya, yb, yc, yd, mix_gain[l], w_out[l].astype(BF16))

        kv = norm_matmul(mem.reshape(b * mlen, d), norm_memkv[l], w_mkv[l].astype(BF16), tm=256)
        x = mem_attention(x, norm_mem[l], w_mq[l].astype(BF16), kv.reshape(b, mlen, 2 * gw), w_mo[l].astype(BF16))

        w_router = jnp.concatenate([w_re[l], w_rg[l], jnp.zeros((d, SMALL_W - N_EXPERTS - N_EXPERT_GROUPS), F32)], axis=1)
        b_router = jnp.concatenate([b_re[l], b_rg[l], jnp.zeros((SMALL_W - N_EXPERTS - N_EXPERT_GROUPS,), F32)]).reshape(1, SMALL_W)
        x = hier_moe_routed(x.reshape(t_tok, d), norm_ffn[l], w_router, b_router, w_e_gu, w_e_dn, l,
                            norm_final, final_norm=(l == depth - 1)).reshape(b, s, d)

    return x
```

```python
import functools
import math

import jax
import jax.numpy as jnp
from jax import lax
from jax.experimental import pallas as pl
from jax.experimental.pallas import tpu as pltpu
from jax.experimental.pallas import tpu_sc as plsc

F32 = jnp.float32
BF16 = jnp.bfloat16

HEAD_DIM = 64
GROUP_WIDTH = 256
HEADS = GROUP_WIDTH // HEAD_DIM
GDN_CHUNK = 64
GDN_BATCH = 2
FOX_BATCH = 2
HGRN_BATCH = 2
HGRN_CHUNK = 16
CONV_W = 4
LRU_C = 8.0
EPS = 1e-6
N_EXPERT_GROUPS = 4
EXPERTS_PER_GROUP = 8
N_EXPERTS = N_EXPERT_GROUPS * EXPERTS_PER_GROUP
D_EXPERT = 256
SC_GATHER_CHUNK = 128
EXPERT_TILE = 512
SMALL_W = 128
SEQ_TILE = 256
ROW_TILE = 512
VMEM_LIMIT = 56 * 1024 * 1024

(COL_AQ, COL_AF, COL_AI, COL_AG, COL_BQ, COL_BK, COL_BV, COL_CX, COL_CG,
 COL_DQ, COL_DK, COL_DV, COL_DZ) = range(13)
N_PROJ = 13 * GROUP_WIDTH + 2 * SMALL_W
COL_SMALL = 13 * GROUP_WIDTH // SMALL_W
LANE_FOX = 0
LANE_BETA = 4
LANE_GDEC = 8


def _cparams(*sem):
    return pltpu.CompilerParams(dimension_semantics=sem, vmem_limit_bytes=VMEM_LIMIT)


def _dot(a, b):
    return jnp.dot(a, b, preferred_element_type=F32)


def _dot_nt(a, b):
    return lax.dot_general(a, b, (((1,), (1,)), ((), ())), preferred_element_type=F32)


def _dot_tn(a, b):
    return lax.dot_general(a, b, (((0,), (0,)), ((), ())), preferred_element_type=F32)


def _split3(x):
    h = x.astype(BF16)
    r = x - h.astype(F32)
    m = r.astype(BF16)
    l = (r - m.astype(F32)).astype(BF16)
    return h, m, l


def _dot_exact_rhs(x, w_bf16):
    h, m, l = _split3(x)
    return _dot(h, w_bf16) + _dot(m, w_bf16) + _dot(l, w_bf16)


def _dot_exact_lhs(w_bf16, x):
    h, m, l = _split3(x)
    return _dot(w_bf16, h) + _dot(w_bf16, m) + _dot(w_bf16, l)


def _iota(shape, dim):
    return lax.broadcasted_iota(jnp.int32, shape, dim)


def _head_ones(n=GROUP_WIDTH):
    r = _iota((n, n), 0) // HEAD_DIM
    c = _iota((n, n), 1) // HEAD_DIM
    return r == c


def _sigmoid(x):
    return 1.0 / (1.0 + jnp.exp(-x))


def _silu(x):
    return x * _sigmoid(x)


def _log_sigmoid(x):
    return jnp.minimum(x, 0.0) - jnp.log1p(jnp.exp(-jnp.abs(x)))


def _softplus(x):
    return jnp.maximum(x, 0.0) + jnp.log1p(jnp.exp(-jnp.abs(x)))


def _gelu_tanh(x):
    return 0.5 * x * (1.0 + jnp.tanh(math.sqrt(2.0 / math.pi) * (x + 0.044715 * (x * x * x))))


def _pack_bf16_pairs(x):
    n = x.shape[1] // 2
    u = lax.bitcast_convert_type(x.astype(BF16).astype(F32), jnp.uint32)
    return u[:, :n] | (u[:, n:] >> 16)


def _unpack_bf16_pairs(p):
    hi = lax.bitcast_convert_type(p & jnp.uint32(0xFFFF0000), F32)
    lo = lax.bitcast_convert_type(p << 16, F32)
    return jnp.concatenate([hi, lo], axis=1)


def _head_mean_sq(x, ones_bf16):
    return _dot_exact_rhs(x * x, ones_bf16) * (1.0 / HEAD_DIM)


def _head_rms(x, ones_bf16):
    return x * lax.rsqrt(_head_mean_sq(x, ones_bf16) + EPS)


def _stack_heads(x):
    lane_head = _iota(x.shape, 1) // HEAD_DIM
    parts = []
    for h in range(HEADS):
        parts.append(jnp.where(lane_head == h, x, 0.0))
    return jnp.concatenate(parts, axis=0)


def _unstack_heads(xs, rows):
    out = xs[0:rows]
    for h in range(1, HEADS):
        out = out + xs[h * rows:(h + 1) * rows]
    return out


def _causal_conv4(x, prev8, w):
    r = x.shape[0]
    row8 = _iota((8, x.shape[1]), 0)
    acc = x * w[CONV_W - 1:CONV_W, :]
    for k in range(1, CONV_W):
        xs = pltpu.roll(x, k, 0)
        ps = pltpu.roll(prev8, k, 0)
        top = jnp.where(row8 < k, ps, xs[0:8])
        xs = jnp.concatenate([top, xs[8:r]], axis=0)
        acc = acc + xs * w[CONV_W - 1 - k:CONV_W - k, :]
    return acc


def _norm_matmul_kernel(x_ref, g_ref, w_ref, o_ref):
    x = x_ref[...]
    ms = jnp.mean(x * x, axis=-1, keepdims=True)
    h = (x * lax.rsqrt(ms + EPS)) * g_ref[...]
    o_ref[...] = _dot(h.astype(BF16), w_ref[...]).astype(o_ref.dtype)


def norm_matmul(x, g, w_bf16, tm, out_dtype=F32):
    m, d = x.shape
    n = w_bf16.shape[1]
    return pl.pallas_call(
        _norm_matmul_kernel,
        grid=(m // tm,),
        in_specs=[pl.BlockSpec((tm, d), lambda i: (i, 0)),
                  pl.BlockSpec((1, d), lambda i: (0, 0)),
                  pl.BlockSpec((d, n), lambda i: (0, 0))],
        out_specs=pl.BlockSpec((tm, n), lambda i: (i, 0)),
        out_shape=jax.ShapeDtypeStruct((m, n), out_dtype),
        compiler_params=_cparams("parallel"),
        name="norm_matmul",
    )(x, g.reshape(1, d), w_bf16)


def _small_prep_kernel(s_ref, p_ref, o_ref, x_ref):
    S = s_ref.shape[1]
    W = s_ref.shape[2]
    blk = SEQ_TILE
    lane = _iota((blk, W), 1)
    er = _iota((W, 2 * HEADS * W), 0)
    ec = _iota((W, 2 * HEADS * W), 1) // W
    expand = jnp.where(er == jnp.where(ec < HEADS, LANE_FOX + ec, LANE_GDEC + ec - HEADS), 1.0, 0.0).astype(BF16)
    is_fox = (lane >= LANE_FOX) & (lane < LANE_FOX + HEADS)
    is_beta = (lane >= LANE_BETA) & (lane < LANE_BETA + HEADS)
    is_gdec = (lane >= LANE_GDEC) & (lane < LANE_GDEC + HEADS)
    r = _iota((blk, blk), 0)
    c = _iota((blk, blk), 1)
    tril_all = jnp.where(r >= c, 1.0, 0.0).astype(BF16)
    tril_chunk = jnp.where((r >= c) & (r // GDN_CHUNK == c // GDN_CHUNK), 1.0, 0.0).astype(BF16)
    neg_exp_a = -jnp.exp(p_ref[1:2, :])
    carry = jnp.zeros((1, W), F32)
    for i in range(S // blk):
        sl = pl.ds(i * blk, blk)
        z = s_ref[0, sl, :] + p_ref[0:1, :]
        fox = jnp.where(is_fox, _log_sigmoid(z), 0.0)
        beta = jnp.where(is_beta, _sigmoid(z), 0.0)
        gdec = jnp.where(is_gdec, neg_exp_a * _softplus(z), 0.0)
        cf = _dot_exact_lhs(tril_all, fox) + carry
        cg = _dot_exact_lhs(tril_chunk, gdec)
        carry = cf[blk - 1:blk, :]
        o_ref[0, sl, :] = cf + cg + beta
        x_ref[0, sl, :] = _dot_exact_rhs(cf + cg, expand)


def small_prep(proj3, params):
    b, s, _ = proj3.shape
    w = SMALL_W
    return pl.pallas_call(
        _small_prep_kernel,
        grid=(b,),
        in_specs=[pl.BlockSpec((1, s, w), lambda i: (i, 0, COL_SMALL)),
                  pl.BlockSpec((8, w), lambda i: (0, 0))],
        out_specs=[pl.BlockSpec((1, s, w), lambda i: (i, 0, 0)),
                   pl.BlockSpec((1, s, 2 * HEADS * w), lambda i: (i, 0, 0))],
        out_shape=[jax.ShapeDtypeStruct((b, s, w), F32), jax.ShapeDtypeStruct((b, s, 2 * HEADS * w), F32)],
        compiler_params=_cparams("parallel"),
        name="small_prep",
    )(proj3, params)


def _hgrn2_kernel(q_ref, f_ref, i_ref, g_ref, lb_ref, o_ref, st_ref, qs, ks, vs, bs, os_):
    nb = q_ref.shape[0]
    T = q_ref.shape[1]
    C = HGRN_CHUNK
    seqs = range(nb)

    @pl.when(pl.program_id(1) == 0)
    def _():
        st_ref[...] = jnp.zeros_like(st_ref)

    same_head = _head_ones()
    ones_bf = jnp.where(same_head, 1.0, 0.0).astype(BF16)
    lb = lb_ref[...]
    log_lb = jnp.log(lb)
    r = _iota((T, T), 0)
    c = _iota((T, T), 1)
    tril_chunk = jnp.where((r >= c) & (r // C == c // C), 1.0, 0.0).astype(BF16)
    for bb in seqs:
        fl = f_ref[bb]
        c2 = jnp.log1p(-lb) + _log_sigmoid(fl)
        mx = jnp.maximum(log_lb, c2)
        log_f = mx + jnp.log1p(jnp.exp(-jnp.abs(log_lb - c2)))
        bs[bb] = _dot_exact_lhs(tril_chunk, log_f)
        qs[bb] = _silu(q_ref[bb])
        ks[bb] = (1.0 - lb) * _sigmoid(-fl)
        vs[bb] = i_ref[bb]

    trow = _iota((C, GROUP_WIDTH), 0)

    def chunk(ci, carry):
        r0 = pl.multiple_of(ci * C, C)
        qc = [qs[bb, pl.ds(r0, C), :] for bb in seqs]
        kc = [ks[bb, pl.ds(r0, C), :] for bb in seqs]
        vc = [vs[bb, pl.ds(r0, C), :] for bb in seqs]
        bc = [bs[bb, pl.ds(r0, C), :] for bb in seqs]
        st = [st_ref[bb] for bb in seqs]
        a = []
        for bb in seqs:
            parts = []
            for s in range(C):
                parts.append(jnp.where(trow >= s, jnp.exp(bc[bb] - bc[bb][s:s + 1, :]) * (qc[bb] * kc[bb][s:s + 1, :]), 0.0))
            a.append(jnp.concatenate(parts, axis=0).astype(BF16))
        sc = [_dot(a[bb], ones_bf) for bb in seqs]
        o = [_dot_nt((qc[bb] * jnp.exp(bc[bb])).astype(BF16), st[bb].astype(BF16)) for bb in seqs]
        b_last = [bc[bb][C - 1:C, :] for bb in seqs]
        upd = [_dot_tn(vc[bb].astype(BF16), (kc[bb] * jnp.exp(b_last[bb] - bc[bb])).astype(BF16)) for bb in seqs]
        for bb in seqs:
            acc = o[bb]
            for s in range(C):
                acc = acc + sc[bb][s * C:(s + 1) * C, :] * vc[bb][s:s + 1, :]
            os_[bb, pl.ds(r0, C), :] = acc
            st_ref[bb] = st[bb] * jnp.exp(b_last[bb]) + jnp.where(same_head, upd[bb], 0.0)
        return carry

    lax.fori_loop(0, T // C, chunk, 0)
    for bb in seqs:
        o_ref[bb] = _head_rms(os_[bb], ones_bf) * _silu(g_ref[bb])


def hgrn2(proj3, lb):
    b, s, _ = proj3.shape
    t = SEQ_TILE
    gw = GROUP_WIDTH
    nb = HGRN_BATCH
    col = lambda cidx: pl.BlockSpec((nb, t, gw), lambda i, j: (i, j, cidx))
    return pl.pallas_call(
        _hgrn2_kernel,
        grid=(b // nb, s // t),
        in_specs=[col(COL_AQ), col(COL_AF), col(COL_AI), col(COL_AG),
                  pl.BlockSpec((1, gw), lambda i, j: (0, 0))],
        out_specs=pl.BlockSpec((nb, t, gw), lambda i, j: (i, j, 0)),
        out_shape=jax.ShapeDtypeStruct((b, s, gw), F32),
        scratch_shapes=[pltpu.VMEM((nb, gw, gw), F32)] + [pltpu.VMEM((nb, t, gw), F32)] * 5,
        compiler_params=_cparams("parallel", "arbitrary"),
        name="hgrn2",
    )(proj3, proj3, proj3, proj3, lb.reshape(1, gw))


def _fox_kernel(q_ref, k_ref, v_ref, cx_ref, o_ref, kb_ref, vt_ref, acc_ref):
    nb = q_ref.shape[0]
    tq = q_ref.shape[1]
    tk = tq
    S = k_ref.shape[1]
    qi = pl.program_id(1)
    seqs = range(nb)

    @pl.when(qi == 0)
    def _():
        for bb in seqs:
            for i in range(S // tk):
                sl = pl.ds(i * tk, tk)
                kb_ref[bb, sl, :] = k_ref[bb, sl, :].astype(BF16)
                vt_ref[bb, :, sl] = v_ref[bb, sl, :].T.astype(BF16)

    lane_head = _iota((tq, GROUP_WIDTH), 1) // HEAD_DIM
    qh = []
    for bb in seqs:
        qsc = q_ref[bb] * (HEAD_DIM ** -0.5)
        qh.append([jnp.where(lane_head == h, qsc, 0.0).astype(BF16) for h in range(HEADS)])
    acc_ref[...] = jnp.zeros_like(acc_ref)
    reps = tq // SMALL_W
    causal = _iota((tk, tq), 0) <= _iota((tk, tq), 1)

    def step(kb, stats, masked):
        k0 = pl.multiple_of(kb * tk, tk)
        new_stats = [[None] * HEADS for _ in seqs]
        strips = [(h, bb) for h in range(HEADS) for bb in seqs]

        def scores(h, bb):
            return _dot_nt(kb_ref[bb, pl.ds(k0, tk), :], qh[bb][h])

        ahead = 3
        queue = [scores(*strips[j]) for j in range(ahead)]
        for i, (h, bb) in enumerate(strips):
            st = queue.pop(0)
            if i + ahead < len(strips):
                queue.append(scores(*strips[i + ahead]))
            rows = slice(h * HEAD_DIM, (h + 1) * HEAD_DIM)
            cx = cx_ref[bb, pl.ds(k0, tk), h * SMALL_W:(h + 1) * SMALL_W]
            st = st - jnp.concatenate([cx] * reps, axis=1)
            if masked:
                st = jnp.where(causal, st, -jnp.inf)
            m_old, l_old = stats[bb][h]
            m_new = jnp.maximum(m_old, jnp.max(st, axis=0, keepdims=True))
            alpha = jnp.exp(m_old - m_new)
            p = jnp.exp(st - m_new)
            l_new = alpha * l_old + jnp.sum(p, axis=0, keepdims=True)
            pv = _dot(vt_ref[bb, rows, pl.ds(k0, tk)], p.astype(BF16))
            acc_ref[bb, rows, :] = alpha * acc_ref[bb, rows, :] + pv
            new_stats[bb][h] = (m_new, l_new)
        return tuple(tuple(new_stats[bb]) for bb in seqs)

    init = tuple(tuple((jnp.full((1, tq), -jnp.inf, F32), jnp.zeros((1, tq), F32)) for _ in range(HEADS)) for _ in seqs)
    stats = lax.fori_loop(0, qi, lambda kb, c: step(kb, c, False), init)
    stats = step(qi, stats, True)

    ones_bf = jnp.where(_head_ones(), 1.0, 0.0).astype(BF16)
    for bb in seqs:
        inv_l = jnp.concatenate([jnp.broadcast_to(1.0 / stats[bb][h][1], (HEAD_DIM, tq)) for h in range(HEADS)], axis=0)
        o_ref[bb] = _head_rms((acc_ref[bb] * inv_l).T, ones_bf)


def fox_attention(proj3, c_exp):
    b, s, _ = proj3.shape
    tq = SEQ_TILE
    gw = GROUP_WIDTH
    nb = FOX_BATCH
    return pl.pallas_call(
        _fox_kernel,
        grid=(b // nb, s // tq),
        in_specs=[pl.BlockSpec((nb, tq, gw), lambda i, j: (i, j, COL_BQ)),
                  pl.BlockSpec((nb, s, gw), lambda i, j: (i, 0, COL_BK)),
                  pl.BlockSpec((nb, s, gw), lambda i, j: (i, 0, COL_BV)),
                  pl.BlockSpec((nb, s, HEADS * SMALL_W), lambda i, j: (i, 0, 0))],
        out_specs=pl.BlockSpec((nb, tq, gw), lambda i, j: (i, j, 0)),
        out_shape=jax.ShapeDtypeStruct((b, s, gw), F32),
        scratch_shapes=[pltpu.VMEM((nb, s, gw), BF16),
                        pltpu.VMEM((nb, gw, s), BF16),
                        pltpu.VMEM((nb, gw, tq), F32)],
        compiler_params=_cparams("parallel", "arbitrary"),
        name="fox_attention",
    )(proj3, proj3, proj3, c_exp)


def _rglru_kernel(x_ref, g_ref, cw_ref, cb_ref, wa_ref, ba_ref, wx_ref, bx_ref, lam_ref, o_ref, prev_ref, h_ref):
    @pl.when(pl.program_id(1) == 0)
    def _():
        prev_ref[...] = jnp.zeros_like(prev_ref)
        h_ref[...] = jnp.zeros_like(h_ref)

    x = x_ref[0]
    t = x.shape[0]
    xc = _causal_conv4(x, prev_ref[...], cw_ref[...]) + cb_ref[...]
    prev_ref[...] = x[t - 8:t, :]
    xb = xc.astype(BF16)
    r = _sigmoid(_dot(xb, wa_ref[...]) + ba_ref[...])
    ig = _sigmoid(_dot(xb, wx_ref[...]) + bx_ref[...])
    log_a = (-LRU_C * r) * _softplus(-lam_ref[...])
    a = jnp.exp(log_a)
    u = jnp.sqrt(1.0 - jnp.exp(2.0 * log_a)) * (ig * xc)
    row = _iota(a.shape, 0)
    d = 1
    while d < t:
        valid = row >= d
        u = jnp.where(valid, a * pltpu.roll(u, d, 0) + u, u)
        a = jnp.where(valid, a * pltpu.roll(a, d, 0), a)
        d *= 2
    h = a * h_ref[...] + u
    h_ref[...] = h[t - 1:t, :]
    ones_bf = jnp.where(_head_ones(), 1.0, 0.0).astype(BF16)
    o_ref[0] = _head_rms(h * _gelu_tanh(g_ref[0]), ones_bf)


def rglru(proj3, conv_w, conv_b, wa_bd, ba, wx_bd, bx, lam):
    b, s, _ = proj3.shape
    t = ROW_TILE
    gw = GROUP_WIDTH
    row = lambda: pl.BlockSpec((1, gw), lambda i, j: (0, 0))
    return pl.pallas_call(
        _rglru_kernel,
        grid=(b, s // t),
        in_specs=[pl.BlockSpec((1, t, gw), lambda i, j: (i, j, COL_CX)),
                  pl.BlockSpec((1, t, gw), lambda i, j: (i, j, COL_CG)),
                  pl.BlockSpec((CONV_W, gw), lambda i, j: (0, 0)), row(),
                  pl.BlockSpec((gw, gw), lambda i, j: (0, 0)), row(),
                  pl.BlockSpec((gw, gw), lambda i, j: (0, 0)), row(), row()],
        out_specs=pl.BlockSpec((1, t, gw), lambda i, j: (i, j, 0)),
        out_shape=jax.ShapeDtypeStruct((b, s, gw), F32),
        scratch_shapes=[pltpu.VMEM((8, gw), F32), pltpu.VMEM((1, gw), F32)],
        compiler_params=_cparams("parallel", "arbitrary"),
        name="rglru",
    )(proj3, proj3, conv_w, conv_b.reshape(1, gw), wa_bd, ba.reshape(1, gw), wx_bd, bx.reshape(1, gw), lam.reshape(1, gw))


def _gdn_kernel(q_ref, k_ref, v_ref, z_ref, sm_ref, ge_ref, gr_ref, cw_ref, o_ref, st_ref, pq_ref, pk_ref, pv_ref):
    nb = q_ref.shape[0]
    T = q_ref.shape[1]
    C = GDN_CHUNK

    @pl.when(pl.program_id(1) == 0)
    def _():
        st_ref[...] = jnp.zeros_like(st_ref)
        pq_ref[...] = jnp.zeros_like(pq_ref)
        pk_ref[...] = jnp.zeros_like(pk_ref)
        pv_ref[...] = jnp.zeros_like(pv_ref)

    same_head = _head_ones()
    ones_bf = jnp.where(same_head, 1.0, 0.0).astype(BF16)
    cw = cw_ref[...]
    gw = GROUP_WIDTH
    er = _iota((SMALL_W, gw), 0)
    ec = _iota((SMALL_W, gw), 1) // HEAD_DIM
    exp_beta = jnp.where(er == LANE_BETA + ec, 1.0, 0.0).astype(BF16)
    exp_g = jnp.where(er == LANE_GDEC + ec, 1.0, 0.0).astype(BF16)
    n = HEADS * C
    rr = _iota((n, n), 0)
    cc = _iota((n, n), 1)
    incl = same_head & (rr % C >= cc % C)
    strict = same_head & (rr % C > cc % C)

    prep = []
    for bb in range(nb):
        xq, xk, xv = q_ref[bb], k_ref[bb], v_ref[bb]
        q = _silu(_causal_conv4(xq, pq_ref[bb], cw[:, 0:gw]))
        k = _silu(_causal_conv4(xk, pk_ref[bb], cw[:, gw:2 * gw]))
        v = _silu(_causal_conv4(xv, pv_ref[bb], cw[:, 2 * gw:3 * gw]))
        pq_ref[bb] = xq[T - 8:T, :]
        pk_ref[bb] = xk[T - 8:T, :]
        pv_ref[bb] = xv[T - 8:T, :]
        q = q * lax.rsqrt(_dot_exact_rhs(q * q, ones_bf) + EPS) * (HEAD_DIM ** -0.5)
        k = k * lax.rsqrt(_dot_exact_rhs(k * k, ones_bf) + EPS)
        sm = sm_ref[bb]
        betax = _dot_exact_rhs(sm, exp_beta)
        gx = _dot_exact_rhs(sm, exp_g)
        prep.append((q, k, v, betax, gx))

    reps = gw // SMALL_W
    n_chunks = T // C
    eye = jnp.where(rr == cc, 1.0, 0.0)

    chains = []
    for ci in range(n_chunks):
        sl = slice(ci * C, (ci + 1) * C)
        for bb in range(nb):
            q, k, v, betax, gx = prep[bb]
            qc, kc, vc, bx, gc = q[sl], k[sl], v[sl], betax[sl], gx[sl]
            grow = gr_ref[bb, ci, 0:1, :]
            ge = ge_ref[bb, sl, :]
            gcol = jnp.concatenate(
                [jnp.concatenate([ge[:, h * SMALL_W:(h + 1) * SMALL_W]] * reps, axis=1) for h in range(HEADS)], axis=0)
            eg = jnp.exp(gc)
            kb = kc * bx
            g_last = gc[C - 1:C, :]
            gamma = jnp.where(incl, jnp.exp(jnp.where(incl, gcol - grow, 0.0)), 0.0)
            ks_bf = _stack_heads(kc).astype(BF16)
            m = jnp.where(strict, _dot_nt(_stack_heads(kb).astype(BF16), ks_bf) * gamma, 0.0)
            chains.append(dict(
                x=eye - m, y=m,
                a_qk=(_dot_nt(_stack_heads(qc).astype(BF16), ks_bf) * gamma).astype(BF16),
                rhs_u=_stack_heads(vc * bx).astype(BF16),
                rhs_w=_stack_heads(kb * eg).astype(BF16),
                q_dec=_stack_heads(qc * eg).astype(BF16),
                k_dec=_stack_heads(kc * jnp.exp(g_last - gc)).astype(BF16),
                decay=jnp.exp(g_last)))
    for _ in range(5):
        for ch in chains:
            yb = ch["y"].astype(BF16)
            ch["y"] = _dot(yb, yb)
        for ch in chains:
            ch["x"] = ch["x"] + _dot(ch["x"].astype(BF16), ch["y"].astype(BF16))
    for ch in chains:
        t_inv = ch["x"].astype(BF16)
        ch["u"] = _dot(t_inv, ch["rhs_u"])
        ch["wk"] = _dot(t_inv, ch["rhs_w"]).astype(BF16)

    outs = [[] for _ in range(nb)]
    for ci in range(n_chunks):
        for bb in range(nb):
            ch = chains[ci * nb + bb]
            st = st_ref[bb]
            st_bf = st.astype(BF16)
            v_new = (ch["u"] - _dot(ch["wk"], st_bf)).astype(BF16)
            o = _dot(ch["q_dec"], st_bf) + _dot(ch["a_qk"], v_new)
            outs[bb].append(_unstack_heads(o, C))
            st_ref[bb] = st * ch["decay"] + _dot_tn(ch["k_dec"], v_new)
    for bb in range(nb):
        o = jnp.concatenate(outs[bb], axis=0)
        o_ref[bb] = _head_rms(o, ones_bf) * _silu(z_ref[bb])


def gdn(proj3, sm, c_exp, g_row, conv_w):
    b, s, _ = proj3.shape
    t = SEQ_TILE
    gw = GROUP_WIDTH
    nb = GDN_BATCH
    col = lambda cidx: pl.BlockSpec((nb, t, gw), lambda i, j: (i, j, cidx))
    return pl.pallas_call(
        _gdn_kernel,
        grid=(b // nb, s // t),
        in_specs=[col(COL_DQ), col(COL_DK), col(COL_DV), col(COL_DZ),
                  pl.BlockSpec((nb, t, SMALL_W), lambda i, j: (i, j, 0)),
                  pl.BlockSpec((nb, t, HEADS * SMALL_W), lambda i, j: (i, j, 1)),
                  pl.BlockSpec((nb, t // GDN_CHUNK, 8, gw), lambda i, j: (i, j, 0, 0)),
                  pl.BlockSpec((CONV_W, 3 * gw), lambda i, j: (0, 0))],
        out_specs=pl.BlockSpec((nb, t, gw), lambda i, j: (i, j, 0)),
        out_shape=jax.ShapeDtypeStruct((b, s, gw), F32),
        scratch_shapes=[pltpu.VMEM((nb, gw, gw), F32)] + [pltpu.VMEM((nb, 8, gw), F32)] * 3,
        compiler_params=_cparams("parallel", "arbitrary"),
        name="gdn",
    )(proj3, proj3, proj3, proj3, sm, c_exp, g_row, conv_w)


def _mix_out_kernel(x_ref, ya_ref, yb_ref, yc_ref, yd_ref, gain_ref, w_ref, o_ref):
    gw = GROUP_WIDTH
    gain = gain_ref[...]
    acc = x_ref[0]
    for i, y_ref in enumerate((ya_ref, yb_ref, yc_ref, yd_ref)):
        yg = (y_ref[0] * gain[:, i * gw:(i + 1) * gw]).astype(BF16)
        acc = acc + _dot(yg, w_ref[i * gw:(i + 1) * gw, :])
    o_ref[0] = acc


def mix_out(x3, ya, yb, yc, yd, gain, w_out_bf16):
    b, s, d = x3.shape
    t = ROW_TILE
    gw = GROUP_WIDTH
    grp = lambda: pl.BlockSpec((1, t, gw), lambda i, j: (i, j, 0))
    return pl.pallas_call(
        _mix_out_kernel,
        grid=(b, s // t),
        in_specs=[pl.BlockSpec((1, t, d), lambda i, j: (i, j, 0)),
                  grp(), grp(), grp(), grp(),
                  pl.BlockSpec((1, 4 * gw), lambda i, j: (0, 0)),
                  pl.BlockSpec((4 * gw, d), lambda i, j: (0, 0))],
        out_specs=pl.BlockSpec((1, t, d), lambda i, j: (i, j, 0)),
        out_shape=jax.ShapeDtypeStruct((b, s, d), F32),
        compiler_params=_cparams("parallel", "parallel"),
        name="mix_out",
    )(x3, ya, yb, yc, yd, gain.reshape(1, 4 * gw), w_out_bf16)


def _mem_attn_kernel(x_ref, g_ref, wq_ref, kv_ref, wo_ref, o_ref):
    x = x_ref[0]
    t = x.shape[0]
    gw = GROUP_WIDTH
    ms = jnp.mean(x * x, axis=-1, keepdims=True)
    h = ((x * lax.rsqrt(ms + EPS)) * g_ref[...]).astype(BF16)
    q = _dot(h, wq_ref[...])
    kv = kv_ref[0]
    k = kv[:, 0:gw].astype(BF16)
    v = kv[:, gw:2 * gw].astype(BF16)
    s = _dot_nt(k, _stack_heads(q).astype(BF16)) * (HEAD_DIM ** -0.5)
    s = s - jnp.max(s, axis=0, keepdims=True)
    p = jnp.exp(s)
    p = p * (1.0 / jnp.sum(p, axis=0, keepdims=True))
    ot = _dot_tn(v, p.astype(BF16))
    oc = jnp.concatenate([ot[hh * HEAD_DIM:(hh + 1) * HEAD_DIM, hh * t:(hh + 1) * t] for hh in range(HEADS)], axis=0).T
    o_ref[0] = x + _dot(oc.astype(BF16), wo_ref[...])


def mem_attention(x3, g, wq_bf16, kv3, wo_bf16):
    b, s, d = x3.shape
    t = ROW_TILE
    m = kv3.shape[1]
    return pl.pallas_call(
        _mem_attn_kernel,
        grid=(b, s // t),
        in_specs=[pl.BlockSpec((1, t, d), lambda i, j: (i, j, 0)),
                  pl.BlockSpec((1, d), lambda i, j: (0, 0)),
                  pl.BlockSpec((d, GROUP_WIDTH), lambda i, j: (0, 0)),
                  pl.BlockSpec((1, m, 2 * GROUP_WIDTH), lambda i, j: (i, 0, 0)),
                  pl.BlockSpec((GROUP_WIDTH, d), lambda i, j: (0, 0))],
        out_specs=pl.BlockSpec((1, t, d), lambda i, j: (i, j, 0)),
        out_shape=jax.ShapeDtypeStruct((b, s, d), F32),
        compiler_params=_cparams("parallel", "parallel"),
        name="mem_attention",
    )(x3, g.reshape(1, d), wq_bf16, kv3, wo_bf16)


def _router_kernel(x_ref, g_ref, w_ref, b_ref, h_ref, r_ref, n_ref, cnt_ref):
    @pl.when(pl.program_id(0) == 0)
    def _():
        cnt_ref[...] = jnp.zeros_like(cnt_ref)

    x = x_ref[...]
    tm = x.shape[0]
    ms = jnp.mean(x * x, axis=-1, keepdims=True)
    h = (x * lax.rsqrt(ms + EPS)) * g_ref[...]
    h_ref[...] = _pack_bf16_pairs(h)
    hh, hm, _ = _split3(h)
    w = w_ref[...]
    wh, wm, _ = _split3(w)
    logits = _dot(hh, wh) + _dot(hh, wm) + _dot(hm, wh) + b_ref[...]
    lane = _iota(logits.shape, 1)
    big = jnp.int32(1 << 30)
    neg = -jnp.inf
    is_grp = (lane >= N_EXPERTS) & (lane < N_EXPERTS + N_EXPERT_GROUPS)
    gl = jnp.where(is_grp, logits, neg)
    gmax = jnp.max(gl, axis=-1, keepdims=True)
    p_grp = 1.0 / jnp.sum(jnp.exp(gl - gmax), axis=-1, keepdims=True)
    g_sel = jnp.min(jnp.where(gl == gmax, lane, big), axis=-1, keepdims=True) - N_EXPERTS
    in_grp = (lane < N_EXPERTS) & (lane // EXPERTS_PER_GROUP == g_sel)
    el = jnp.where(in_grp, logits, neg)
    m1 = jnp.max(el, axis=-1, keepdims=True)
    i1 = jnp.min(jnp.where(el == m1, lane, big), axis=-1, keepdims=True)
    el2 = jnp.where(lane == i1, neg, el)
    m2 = jnp.max(el2, axis=-1, keepdims=True)
    i2 = jnp.min(jnp.where(el2 == m2, lane, big), axis=-1, keepdims=True)
    e21 = jnp.exp(m2 - m1)
    w1 = 1.0 / (1.0 + e21)
    w2 = e21 / (1.0 + e21)
    oh1 = lane == i1
    oh2 = lane == i2
    onehot = jnp.where(oh1 | oh2, 1.0, 0.0)
    strict = jnp.where(_iota((tm, tm), 0) > _iota((tm, tm), 1), 1.0, 0.0).astype(BF16)
    before = _dot(strict, onehot.astype(BF16)) + cnt_ref[...]
    rank1 = jnp.sum(jnp.where(oh1, before, 0.0), axis=-1, keepdims=True)
    rank2 = jnp.sum(jnp.where(oh2, before, 0.0), axis=-1, keepdims=True)
    cnt = before[tm - 1:tm, :] + onehot[tm - 1:tm, :]
    cnt_ref[...] = cnt
    n_ref[...] = jnp.broadcast_to(cnt, n_ref.shape)
    cols = (i1.astype(F32), i2.astype(F32), p_grp * w1, p_grp * w2, rank1, rank2)
    route = jnp.zeros(logits.shape, F32)
    for j, col in enumerate(cols):
        route = jnp.where(lane == j, col, route)
    r_ref[...] = route


def router(x2, g, w_router, b_router, tm):
    m, d = x2.shape
    return pl.pallas_call(
        _router_kernel,
        grid=(m // tm,),
        in_specs=[pl.BlockSpec((tm, d), lambda i: (i, 0)),
                  pl.BlockSpec((1, d), lambda i: (0, 0)),
                  pl.BlockSpec((d, SMALL_W), lambda i: (0, 0)),
                  pl.BlockSpec((1, SMALL_W), lambda i: (0, 0))],
        out_specs=[pl.BlockSpec((tm, d // 2), lambda i: (i, 0)),
                   pl.BlockSpec((tm, SMALL_W), lambda i: (i, 0)),
                   pl.BlockSpec((8, SMALL_W), lambda i: (0, 0))],
        out_shape=[jax.ShapeDtypeStruct((m, d // 2), jnp.uint32), jax.ShapeDtypeStruct((m, SMALL_W), F32),
                   jax.ShapeDtypeStruct((8, SMALL_W), F32)],
        scratch_shapes=[pltpu.VMEM((1, SMALL_W), F32)],
        compiler_params=_cparams("arbitrary"),
        name="router",
    )(x2, g.reshape(1, d), w_router, b_router)


def _experts_kernel(te_ref, nu_ref, xs_ref, wgu_ref, wdn_ref, y_ref, wgu_bf, wdn_bf):
    i = pl.program_id(0)

    @pl.when(i >= nu_ref[0])
    def _():
        y_ref[...] = jnp.zeros_like(y_ref)

    @pl.when(i < nu_ref[0])
    def _():
        prev = te_ref[jnp.maximum(i - 1, 0)]

        @pl.when((i == 0) | (te_ref[i] != prev))
        def _():
            wgu_bf[...] = wgu_ref[0, 0].astype(BF16)
            wdn_bf[...] = wdn_ref[0, 0].astype(BF16)

        gu = _dot(_unpack_bf16_pairs(xs_ref[...]).astype(BF16), wgu_bf[...])
        act = _silu(gu[:, 0:D_EXPERT]) * gu[:, D_EXPERT:2 * D_EXPERT]
        y_ref[...] = _pack_bf16_pairs(_dot(act.astype(BF16), wdn_bf[...]))


def moe_experts(xs, tile_expert, n_used, w_gu, w_dn, layer, tm):
    n_rows, dp = xs.shape
    d = 2 * dp
    tile = lambda i, te, nu: (jnp.minimum(i, nu[0] - 1), 0)
    wsel = lambda i, te, nu: (layer, te[jnp.minimum(i, nu[0] - 1)], 0, 0)
    return pl.pallas_call(
        _experts_kernel,
        grid_spec=pltpu.PrefetchScalarGridSpec(
            num_scalar_prefetch=2,
            grid=(n_rows // tm,),
            in_specs=[pl.BlockSpec((tm, dp), tile),
                      pl.BlockSpec((1, 1, d, 2 * D_EXPERT), wsel),
                      pl.BlockSpec((1, 1, D_EXPERT, d), wsel)],
            out_specs=pl.BlockSpec((tm, dp), lambda i, te, nu: (i, 0)),
            scratch_shapes=[pltpu.VMEM((d, 2 * D_EXPERT), BF16), pltpu.VMEM((D_EXPERT, d), BF16)]),
        out_shape=jax.ShapeDtypeStruct((n_rows, dp), jnp.uint32),
        compiler_params=_cparams("arbitrary"),
        name="moe_experts",
    )(tile_expert, n_used, xs, w_gu, w_dn)


def sc_gather_rows(table, idx):
    n_idx = idx.shape[0]
    width = table.shape[1]
    info = plsc.get_sparse_core_info()
    n_workers = info.num_cores * info.num_subcores
    per_worker = n_idx // n_workers
    chunk = SC_GATHER_CHUNK
    assert n_idx % (n_workers * chunk) == 0 and width % info.num_lanes == 0
    mesh = plsc.VectorSubcoreMesh(core_axis_name="c", subcore_axis_name="s")

    def body(table_hbm, idx_hbm, out_hbm, idx_v, rows_v, sem):
        wid = lax.axis_index("s") * info.num_cores + lax.axis_index("c")
        base = wid * per_worker

        @pl.loop(0, per_worker // chunk)
        def _(j):
            off = pl.multiple_of(base + j * chunk, chunk)
            pltpu.sync_copy(idx_hbm.at[pl.ds(off, chunk)], idx_v)
            pltpu.async_copy(table_hbm.at[idx_v], rows_v, sem).wait()
            pltpu.sync_copy(rows_v, out_hbm.at[pl.ds(off, chunk)])

    return pl.kernel(
        body,
        out_type=jax.ShapeDtypeStruct((n_idx, width), table.dtype),
        mesh=mesh,
        scratch_types=[pltpu.VMEM((chunk,), jnp.int32), pltpu.VMEM((chunk, width), table.dtype),
                       pltpu.SemaphoreType.DMA],
        name="sc_gather_rows",
    )(table, idx)


def sc_dispatch_rows(rows, dest, pad_idx, zero_rows):
    m, width = rows.shape
    n_pad = pad_idx.shape[0]
    info = plsc.get_sparse_core_info()
    n_workers = info.num_cores * info.num_subcores
    chunk = SC_GATHER_CHUNK
    assert m % (n_workers * chunk) == 0 and n_pad % (n_workers * chunk) == 0 and zero_rows.shape == (chunk, width)
    rows_per_worker = m // n_workers
    pad_per_worker = n_pad // n_workers
    mesh = plsc.VectorSubcoreMesh(core_axis_name="c", subcore_axis_name="s")

    def body(rows_hbm, dest_hbm, pad_hbm, zero_hbm, out_hbm, idx_v, rows_v, sem):
        wid = lax.axis_index("s") * info.num_cores + lax.axis_index("c")

        @pl.loop(0, rows_per_worker // chunk)
        def _(j):
            off = pl.multiple_of(wid * rows_per_worker + j * chunk, chunk)
            pltpu.sync_copy(rows_hbm.at[pl.ds(off, chunk)], rows_v)
            for slot in range(2):
                pltpu.sync_copy(dest_hbm.at[pl.ds(slot * m + off, chunk)], idx_v)
                pltpu.async_copy(rows_v, out_hbm.at[idx_v], sem).wait()

        pltpu.sync_copy(zero_hbm, rows_v)

        @pl.loop(0, pad_per_worker // chunk)
        def _(j):
            off = pl.multiple_of(wid * pad_per_worker + j * chunk, chunk)
            pltpu.sync_copy(pad_hbm.at[pl.ds(off, chunk)], idx_v)
            pltpu.async_copy(rows_v, out_hbm.at[idx_v], sem).wait()

    return pl.kernel(
        body,
        out_type=jax.ShapeDtypeStruct((2 * m + n_pad, width), rows.dtype),
        mesh=mesh,
        scratch_types=[pltpu.VMEM((chunk,), jnp.int32), pltpu.VMEM((chunk, width), rows.dtype),
                       pltpu.SemaphoreType.DMA],
        name="sc_dispatch_rows",
    )(rows, dest, pad_idx, zero_rows)


def _combine_rows_kernel(x_ref, r_ref, g_ref, y1_ref, y2_ref, o_ref, *, final_norm):
    r = r_ref[...]
    out = x_ref[...] + r[:, 2:3] * _unpack_bf16_pairs(y1_ref[...]) + r[:, 3:4] * _unpack_bf16_pairs(y2_ref[...])
    if final_norm:
        ms = jnp.mean(out * out, axis=-1, keepdims=True)
        out = (out * lax.rsqrt(ms + EPS)) * g_ref[...]
    o_ref[...] = out


def moe_combine_rows(x2, route, y_rows, g_final, tm, final_norm):
    m, d = x2.shape
    steps = m // tm
    return pl.pallas_call(
        functools.partial(_combine_rows_kernel, final_norm=final_norm),
        grid=(steps,),
        in_specs=[pl.BlockSpec((tm, d), lambda i: (i, 0)),
                  pl.BlockSpec((tm, SMALL_W), lambda i: (i, 0)),
                  pl.BlockSpec((1, d), lambda i: (0, 0)),
                  pl.BlockSpec((tm, d // 2), lambda i: (i, 0)),
                  pl.BlockSpec((tm, d // 2), lambda i: (i + steps, 0))],
        out_specs=pl.BlockSpec((tm, d), lambda i: (i, 0)),
        out_shape=jax.ShapeDtypeStruct((m, d), F32),
        compiler_params=_cparams("parallel"),
        name="moe_combine_rows",
    )(x2, route, g_final.reshape(1, d), y_rows, y_rows)


def hier_moe_routed(x2, g_ffn, w_router, b_router, w_gu, w_dn, layer, g_final, final_norm):
    m, d = x2.shape
    te_rows = EXPERT_TILE
    n_rows = 2 * m + N_EXPERTS * te_rows
    h, route, counts = router(x2, g_ffn, w_router, b_router, tm=ROW_TILE)
    cnt = counts[0, 0:N_EXPERTS].astype(jnp.int32)
    padded = (cnt + te_rows - 1) // te_rows * te_rows
    seg_end = jnp.cumsum(padded)
    seg_start = seg_end - padded
    idx = route[:, 0:2].astype(jnp.int32)
    onehot = idx[:, :, None] == jnp.arange(N_EXPERTS, dtype=jnp.int32)
    dest = (jnp.sum(jnp.where(onehot, seg_start, 0), axis=-1) + route[:, 4:6].astype(jnp.int32)).T.reshape(2 * m)
    tile_start = jnp.arange(n_rows // te_rows, dtype=jnp.int32) * te_rows
    tile_expert = jnp.minimum(jnp.sum((tile_start[:, None] >= seg_end[None, :]).astype(jnp.int32), axis=1), N_EXPERTS - 1)
    n_used = (seg_end[N_EXPERTS - 1] // te_rows).reshape(1)
    n_pad = n_rows - 2 * m
    experts = jnp.arange(N_EXPERTS, dtype=jnp.int32)
    pad_end = jnp.cumsum(padded - cnt)
    k = jnp.arange(n_pad, dtype=jnp.int32)
    k_expert = jnp.sum((k[:, None] >= pad_end[None, :]).astype(jnp.int32), axis=1)
    first_pad = seg_start + cnt - (pad_end - (padded - cnt))
    in_segment = jnp.sum(jnp.where(k_expert[:, None] == experts[None, :], first_pad[None, :], 0), axis=1) + k
    pad_idx = jnp.where(k_expert < N_EXPERTS, in_segment, seg_end[N_EXPERTS - 1] + k - pad_end[N_EXPERTS - 1])
    xs = sc_dispatch_rows(h, dest, pad_idx, jnp.zeros((SC_GATHER_CHUNK, d // 2), h.dtype))
    ys = moe_experts(xs, tile_expert, n_used, w_gu, w_dn, layer, tm=te_rows)
    y_rows = sc_gather_rows(ys, dest)
    return moe_combine_rows(x2, route, y_rows, g_final, tm=ROW_TILE, final_norm=final_norm)


def _w_in_prep_kernel(w_ref, o_ref):
    gw = GROUP_WIDTH
    a_end = 7 * gw
    c_start = a_end + HEADS
    c_end = c_start + 5 * gw
    z_start = c_end + 2 * HEADS
    assert a_end % SMALL_W == LANE_FOX and c_end % SMALL_W == LANE_BETA and LANE_GDEC == LANE_BETA + HEADS
    w = w_ref[...]
    o_ref[:, 0:a_end] = w[:, 0:a_end].astype(BF16)
    o_ref[:, a_end:a_end + 5 * gw] = w[:, c_start:c_end].astype(BF16)
    o_ref[:, 12 * gw:13 * gw] = w[:, z_start:z_start + gw].astype(BF16)
    blk_fox = w[:, a_end:a_end + SMALL_W]
    blk_gdn = w[:, c_end - LANE_BETA:c_end - LANE_BETA + SMALL_W]
    lane = _iota(blk_fox.shape, 1)
    small = jnp.where(lane < LANE_BETA, blk_fox, jnp.where(lane < LANE_GDEC + HEADS, blk_gdn, 0.0))
    o_ref[:, 13 * gw:13 * gw + SMALL_W] = small.astype(BF16)
    o_ref[:, 13 * gw + SMALL_W:] = jnp.zeros((w.shape[0], SMALL_W), BF16)


def w_in_prep(w_all, layer):
    _, d, n = w_all.shape
    tr = 128
    return pl.pallas_call(
        _w_in_prep_kernel,
        grid=(d // tr,),
        in_specs=[pl.BlockSpec((None, tr, n), lambda i: (layer, i, 0))],
        out_specs=pl.BlockSpec((tr, N_PROJ), lambda i: (i, 0)),
        out_shape=jax.ShapeDtypeStruct((d, N_PROJ), BF16),
        compiler_params=_cparams("parallel"),
        name="w_in_prep",
    )(w_all)


def _block_diag(w):
    h = w.shape[0]
    eye = jnp.eye(h, dtype=w.dtype)
    return (eye[:, None, :, None] * w[:, :, None, :]).reshape(h * HEAD_DIM, h * HEAD_DIM)


def kernel(x, mem, norm_mix, w_in, hgrn_lb, fox_fb, lru_conv_w, lru_conv_b, lru_wa, lru_ba, lru_wx, lru_bx, lru_lam, gdn_conv_w, gdn_a_log, gdn_dt_bias, mix_gain, w_out, norm_mem, norm_memkv, w_mq, w_mkv, w_mo, norm_ffn, w_rg, b_rg, w_re, b_re, w_e_gu, w_e_dn, norm_final):
    b, s, d = x.shape
    depth = w_in.shape[0]
    t_tok = b * s
    mlen = mem.shape[1]
    gw = GROUP_WIDTH

    lb_all = jnp.cumsum(jax.nn.softmax(hgrn_lb.astype(F32), axis=0), axis=0)
    lb_all = lb_all - lb_all[0]

    x = x.astype(F32)
    for l in range(depth):
        proj = norm_matmul(x.reshape(t_tok, d), norm_mix[l], w_in_prep(w_in, l), tm=ROW_TILE)
        proj3 = proj.reshape(b, s, N_PROJ)
        prm = jnp.zeros((8, SMALL_W), F32)
        prm = prm.at[0, LANE_FOX:LANE_FOX + HEADS].set(fox_fb[l].astype(F32))
        prm = prm.at[0, LANE_GDEC:LANE_GDEC + HEADS].set(gdn_dt_bias[l].astype(F32))
        prm = prm.at[1, LANE_GDEC:LANE_GDEC + HEADS].set(gdn_a_log[l].astype(F32))
        sm, c_exp = small_prep(proj3, prm)
        g_row = sm[:, :, LANE_GDEC:LANE_GDEC + HEADS].reshape(b, s // GDN_CHUNK, GDN_CHUNK, HEADS)
        g_row = jnp.swapaxes(g_row, 2, 3).reshape(b, s // GDN_CHUNK, 1, gw)
        g_row = jnp.broadcast_to(g_row, (b, s // GDN_CHUNK, 8, gw))

        ya = hgrn2(proj3, lb_all[l])
        yb = fox_attention(proj3, c_exp)
        yc = rglru(proj3, lru_conv_w[l], lru_conv_b[l], _block_diag(lru_wa[l]).astype(BF16), lru_ba[l],
                   _block_diag(lru_wx[l]).astype(BF16), lru_bx[l], lru_lam[l])
        yd = gdn(proj3, sm, c_exp, g_row, gdn_conv_w[l])
        x = mix_out(x, ya, yb, yc, yd, mix_gain[l], w_out[l].astype(BF16))

        kv = norm_matmul(mem.reshape(b * mlen, d), norm_memkv[l], w_mkv[l].astype(BF16), tm=256)
        x = mem_attention(x, norm_mem[l], w_mq[l].astype(BF16), kv.reshape(b, mlen, 2 * gw), w_mo[l].astype(BF16))

        w_router = jnp.concatenate([w_re[l], w_rg[l], jnp.zeros((d, SMALL_W - N_EXPERTS - N_EXPERT_GROUPS), F32)], axis=1)
        b_router = jnp.concatenate([b_re[l], b_rg[l], jnp.zeros((SMALL_W - N_EXPERTS - N_EXPERT_GROUPS,), F32)]).reshape(1, SMALL_W)
        x = hier_moe_routed(x.reshape(t_tok, d), norm_ffn[l], w_router, b_router, w_e_gu, w_e_dn, l,
                            norm_final, final_norm=(l == depth - 1)).reshape(b, s, d)

    return x
```

```python
import functools
import math

import jax
import jax.numpy as jnp
from jax import lax
from jax.experimental import pallas as pl
from jax.experimental.pallas import tpu as pltpu
from jax.experimental.pallas import tpu_sc as plsc

F32 = jnp.float32
BF16 = jnp.bfloat16

HEAD_DIM = 64
GROUP_WIDTH = 256
HEADS = GROUP_WIDTH // HEAD_DIM
GDN_CHUNK = 64
GDN_BATCH = 2
FOX_BATCH = 2
HGRN_BATCH = 2
HGRN_CHUNK = 16
CONV_W = 4
LRU_C = 8.0
EPS = 1e-6
N_EXPERT_GROUPS = 4
EXPERTS_PER_GROUP = 8
N_EXPERTS = N_EXPERT_GROUPS * EXPERTS_PER_GROUP
D_EXPERT = 256
SC_GATHER_CHUNK = 128
EXPERT_TILE = 512
SMALL_W = 128
SEQ_TILE = 256
ROW_TILE = 512
VMEM_LIMIT = 56 * 1024 * 1024

(COL_AQ, COL_AF, COL_AI, COL_AG, COL_BQ, COL_BK, COL_BV, COL_CX, COL_CG,
 COL_DQ, COL_DK, COL_DV, COL_DZ) = range(13)
N_PROJ = 13 * GROUP_WIDTH + 2 * SMALL_W
COL_SMALL = 13 * GROUP_WIDTH // SMALL_W
LANE_FOX = 0
LANE_BETA = 4
LANE_GDEC = 8


def _cparams(*sem):
    return pltpu.CompilerParams(dimension_semantics=sem, vmem_limit_bytes=VMEM_LIMIT)


def _dot(a, b):
    return jnp.dot(a, b, preferred_element_type=F32)


def _dot_nt(a, b):
    return lax.dot_general(a, b, (((1,), (1,)), ((), ())), preferred_element_type=F32)


def _dot_tn(a, b):
    return lax.dot_general(a, b, (((0,), (0,)), ((), ())), preferred_element_type=F32)


def _split3(x):
    h = x.astype(BF16)
    r = x - h.astype(F32)
    m = r.astype(BF16)
    l = (r - m.astype(F32)).astype(BF16)
    return h, m, l


def _dot_exact_rhs(x, w_bf16):
    h, m, l = _split3(x)
    return _dot(h, w_bf16) + _dot(m, w_bf16) + _dot(l, w_bf16)


def _dot_exact_lhs(w_bf16, x):
    h, m, l = _split3(x)
    return _dot(w_bf16, h) + _dot(w_bf16, m) + _dot(w_bf16, l)


def _iota(shape, dim):
    return lax.broadcasted_iota(jnp.int32, shape, dim)


def _head_ones(n=GROUP_WIDTH):
    r = _iota((n, n), 0) // HEAD_DIM
    c = _iota((n, n), 1) // HEAD_DIM
    return r == c


def _sigmoid(x):
    return 1.0 / (1.0 + jnp.exp(-x))


def _silu(x):
    return x * _sigmoid(x)


def _log_sigmoid(x):
    return jnp.minimum(x, 0.0) - jnp.log1p(jnp.exp(-jnp.abs(x)))


def _softplus(x):
    return jnp.maximum(x, 0.0) + jnp.log1p(jnp.exp(-jnp.abs(x)))


def _gelu_tanh(x):
    return 0.5 * x * (1.0 + jnp.tanh(math.sqrt(2.0 / math.pi) * (x + 0.044715 * (x * x * x))))


def _pack_bf16_pairs(x):
    n = x.shape[1] // 2
    u = lax.bitcast_convert_type(x.astype(BF16).astype(F32), jnp.uint32)
    return u[:, :n] | (u[:, n:] >> 16)


def _unpack_bf16_pairs(p):
    hi = lax.bitcast_convert_type(p & jnp.uint32(0xFFFF0000), F32)
    lo = lax.bitcast_convert_type(p << 16, F32)
    return jnp.concatenate([hi, lo], axis=1)


def _head_mean_sq(x, ones_bf16):
    return _dot_exact_rhs(x * x, ones_bf16) * (1.0 / HEAD_DIM)


def _head_rms(x, ones_bf16):
    return x * lax.rsqrt(_head_mean_sq(x, ones_bf16) + EPS)


def _stack_heads(x):
    lane_head = _iota(x.shape, 1) // HEAD_DIM
    parts = []
    for h in range(HEADS):
        parts.append(jnp.where(lane_head == h, x, 0.0))
    return jnp.concatenate(parts, axis=0)


def _unstack_heads(xs, rows):
    out = xs[0:rows]
    for h in range(1, HEADS):
        out = out + xs[h * rows:(h + 1) * rows]
    return out


def _causal_conv4(x, prev8, w):
    r = x.shape[0]
    row8 = _iota((8, x.shape[1]), 0)
    acc = x * w[CONV_W - 1:CONV_W, :]
    for k in range(1, CONV_W):
        xs = pltpu.roll(x, k, 0)
        ps = pltpu.roll(prev8, k, 0)
        top = jnp.where(row8 < k, ps, xs[0:8])
        xs = jnp.concatenate([top, xs[8:r]], axis=0)
        acc = acc + xs * w[CONV_W - 1 - k:CONV_W - k, :]
    return acc


def _norm_matmul_kernel(x_ref, g_ref, w_ref, o_ref):
    x = x_ref[...]
    ms = jnp.mean(x * x, axis=-1, keepdims=True)
    h = (x * lax.rsqrt(ms + EPS)) * g_ref[...]
    o_ref[...] = _dot(h.astype(BF16), w_ref[...]).astype(o_ref.dtype)


def norm_matmul(x, g, w_bf16, tm, out_dtype=F32):
    m, d = x.shape
    n = w_bf16.shape[1]
    return pl.pallas_call(
        _norm_matmul_kernel,
        grid=(m // tm,),
        in_specs=[pl.BlockSpec((tm, d), lambda i: (i, 0)),
                  pl.BlockSpec((1, d), lambda i: (0, 0)),
                  pl.BlockSpec((d, n), lambda i: (0, 0))],
        out_specs=pl.BlockSpec((tm, n), lambda i: (i, 0)),
        out_shape=jax.ShapeDtypeStruct((m, n), out_dtype),
        compiler_params=_cparams("parallel"),
        name="norm_matmul",
    )(x, g.reshape(1, d), w_bf16)


def _small_prep_kernel(s_ref, p_ref, o_ref, x_ref):
    S = s_ref.shape[1]
    W = s_ref.shape[2]
    blk = SEQ_TILE
    lane = _iota((blk, W), 1)
    is_fox = (lane >= LANE_FOX) & (lane < LANE_FOX + HEADS)
    is_beta = (lane >= LANE_BETA) & (lane < LANE_BETA + HEADS)
    is_gdec = (lane >= LANE_GDEC) & (lane < LANE_GDEC + HEADS)
    r = _iota((blk, blk), 0)
    c = _iota((blk, blk), 1)
    tril_all = jnp.where(r >= c, 1.0, 0.0).astype(BF16)
    tril_chunk = jnp.where((r >= c) & (r // GDN_CHUNK == c // GDN_CHUNK), 1.0, 0.0).astype(BF16)
    neg_exp_a = -jnp.exp(p_ref[1:2, :])
    carry = jnp.zeros((1, W), F32)
    for i in range(S // blk):
        sl = pl.ds(i * blk, blk)
        z = s_ref[0, sl, :] + p_ref[0:1, :]
        fox = jnp.where(is_fox, _log_sigmoid(z), 0.0)
        beta = jnp.where(is_beta, _sigmoid(z), 0.0)
        gdec = jnp.where(is_gdec, neg_exp_a * _softplus(z), 0.0)
        cf = _dot_exact_lhs(tril_all, fox) + carry
        cg = _dot_exact_lhs(tril_chunk, gdec)
        carry = cf[blk - 1:blk, :]
        o_ref[0, sl, :] = cf + cg + beta
        both = cf + cg
        for j, src_lane in enumerate(list(range(LANE_FOX, LANE_FOX + HEADS)) + list(range(LANE_GDEC, LANE_GDEC + HEADS))):
            x_ref[0, sl, j * W:(j + 1) * W] = jnp.broadcast_to(both[:, src_lane:src_lane + 1], (blk, W))


def small_prep(proj3, params):
    b, s, _ = proj3.shape
    w = SMALL_W
    return pl.pallas_call(
        _small_prep_kernel,
        grid=(b,),
        in_specs=[pl.BlockSpec((1, s, w), lambda i: (i, 0, COL_SMALL)),
                  pl.BlockSpec((8, w), lambda i: (0, 0))],
        out_specs=[pl.BlockSpec((1, s, w), lambda i: (i, 0, 0)),
                   pl.BlockSpec((1, s, 2 * HEADS * w), lambda i: (i, 0, 0))],
        out_shape=[jax.ShapeDtypeStruct((b, s, w), F32), jax.ShapeDtypeStruct((b, s, 2 * HEADS * w), F32)],
        compiler_params=_cparams("parallel"),
        name="small_prep",
    )(proj3, params)


def _hgrn2_kernel(q_ref, f_ref, i_ref, g_ref, lb_ref, o_ref, st_ref, qs, ks, vs, bs, os_):
    nb = q_ref.shape[0]
    T = q_ref.shape[1]
    C = HGRN_CHUNK
    seqs = range(nb)

    @pl.when(pl.program_id(1) == 0)
    def _():
        st_ref[...] = jnp.zeros_like(st_ref)

    same_head = _head_ones()
    ones_bf = jnp.where(same_head, 1.0, 0.0).astype(BF16)
    lb = lb_ref[...]
    log_lb = jnp.log(lb)
    r = _iota((T, T), 0)
    c = _iota((T, T), 1)
    tril_chunk = jnp.where((r >= c) & (r // C == c // C), 1.0, 0.0).astype(BF16)
    for bb in seqs:
        fl = f_ref[bb]
        c2 = jnp.log1p(-lb) + _log_sigmoid(fl)
        mx = jnp.maximum(log_lb, c2)
        log_f = mx + jnp.log1p(jnp.exp(-jnp.abs(log_lb - c2)))
        bs[bb] = _dot_exact_lhs(tril_chunk, log_f)
        qs[bb] = _silu(q_ref[bb])
        ks[bb] = (1.0 - lb) * _sigmoid(-fl)
        vs[bb] = i_ref[bb]

    H8 = C // 2
    trow8 = _iota((H8, GROUP_WIDTH), 0)

    def chunk(ci, carry):
        r0 = pl.multiple_of(ci * C, C)
        qc = [qs[bb, pl.ds(r0, C), :] for bb in seqs]
        kc = [ks[bb, pl.ds(r0, C), :] for bb in seqs]
        vc = [vs[bb, pl.ds(r0, C), :] for bb in seqs]
        bc = [bs[bb, pl.ds(r0, C), :] for bb in seqs]
        st = [st_ref[bb] for bb in seqs]
        a = []
        for bb in seqs:
            parts = []
            for half, s_range in ((0, range(H8)), (1, range(C))):
                b_half = bc[bb][half * H8:(half + 1) * H8, :]
                q_half = qc[bb][half * H8:(half + 1) * H8, :]
                for s in s_range:
                    blk = jnp.exp(b_half - bc[bb][s:s + 1, :]) * (q_half * kc[bb][s:s + 1, :])
                    if s >= half * H8:
                        blk = jnp.where(trow8 >= s - half * H8, blk, 0.0)
                    parts.append(blk)
            a.append(jnp.concatenate(parts, axis=0).astype(BF16))
        sc = [_dot(a[bb], ones_bf) for bb in seqs]
        o = [_dot_nt((qc[bb] * jnp.exp(bc[bb])).astype(BF16), st[bb].astype(BF16)) for bb in seqs]
        b_last = [bc[bb][C - 1:C, :] for bb in seqs]
        upd = [_dot_tn(vc[bb].astype(BF16), (kc[bb] * jnp.exp(b_last[bb] - bc[bb])).astype(BF16)) for bb in seqs]
        for bb in seqs:
            top = o[bb][0:H8, :]
            bot = o[bb][H8:C, :]
            for s in range(H8):
                top = top + sc[bb][s * H8:(s + 1) * H8, :] * vc[bb][s:s + 1, :]
            for s in range(C):
                bot = bot + sc[bb][(H8 + s) * H8:(H8 + s + 1) * H8, :] * vc[bb][s:s + 1, :]
            os_[bb, pl.ds(r0, C), :] = jnp.concatenate([top, bot], axis=0)
            st_ref[bb] = st[bb] * jnp.exp(b_last[bb]) + jnp.where(same_head, upd[bb], 0.0)
        return carry

    lax.fori_loop(0, T // C, chunk, 0)
    for bb in seqs:
        o_ref[bb] = _head_rms(os_[bb], ones_bf) * _silu(g_ref[bb])


def hgrn2(proj3, lb):
    b, s, _ = proj3.shape
    t = SEQ_TILE
    gw = GROUP_WIDTH
    nb = HGRN_BATCH
    col = lambda cidx: pl.BlockSpec((nb, t, gw), lambda i, j: (i, j, cidx))
    return pl.pallas_call(
        _hgrn2_kernel,
        grid=(b // nb, s // t),
        in_specs=[col(COL_AQ), col(COL_AF), col(COL_AI), col(COL_AG),
                  pl.BlockSpec((1, gw), lambda i, j: (0, 0))],
        out_specs=pl.BlockSpec((nb, t, gw), lambda i, j: (i, j, 0)),
        out_shape=jax.ShapeDtypeStruct((b, s, gw), F32),
        scratch_shapes=[pltpu.VMEM((nb, gw, gw), F32)] + [pltpu.VMEM((nb, t, gw), F32)] * 5,
        compiler_params=_cparams("parallel", "arbitrary"),
        name="hgrn2",
    )(proj3, proj3, proj3, proj3, lb.reshape(1, gw))


def _fox_kernel(q_ref, k_ref, v_ref, cx_ref, o_ref, kb_ref, vt_ref, acc_ref):
    nb = q_ref.shape[0]
    tq = q_ref.shape[1]
    tk = tq
    S = k_ref.shape[1]
    qi = pl.program_id(1)
    seqs = range(nb)

    @pl.when(qi == 0)
    def _():
        for bb in seqs:
            for i in range(S // tk):
                sl = pl.ds(i * tk, tk)
                kb_ref[bb, sl, :] = k_ref[bb, sl, :].astype(BF16)
                vt_ref[bb, :, sl] = v_ref[bb, sl, :].T.astype(BF16)

    lane_head = _iota((tq, GROUP_WIDTH), 1) // HEAD_DIM
    qh = []
    for bb in seqs:
        qsc = q_ref[bb] * (HEAD_DIM ** -0.5)
        qh.append([jnp.where(lane_head == h, qsc, 0.0).astype(BF16) for h in range(HEADS)])
    acc_ref[...] = jnp.zeros_like(acc_ref)
    reps = tq // SMALL_W
    causal = _iota((tk, tq), 0) <= _iota((tk, tq), 1)

    def step(kb, stats, masked):
        k0 = pl.multiple_of(kb * tk, tk)
        new_stats = [[None] * HEADS for _ in seqs]
        strips = [(h, bb) for h in range(HEADS) for bb in seqs]

        def scores(h, bb):
            return _dot_nt(kb_ref[bb, pl.ds(k0, tk), :], qh[bb][h])

        ahead = 3
        queue = [scores(*strips[j]) for j in range(ahead)]
        for i, (h, bb) in enumerate(strips):
            st = queue.pop(0)
            if i + ahead < len(strips):
                queue.append(scores(*strips[i + ahead]))
            rows = slice(h * HEAD_DIM, (h + 1) * HEAD_DIM)
            cx = cx_ref[bb, pl.ds(k0, tk), h * SMALL_W:(h + 1) * SMALL_W]
            st = st - jnp.concatenate([cx] * reps, axis=1)
            if masked:
                st = jnp.where(causal, st, -jnp.inf)
            m_old, l_old = stats[bb][h]
            m_new = jnp.maximum(m_old, jnp.max(st, axis=0, keepdims=True))
            alpha = jnp.exp(m_old - m_new)
            p = jnp.exp(st - m_new)
            l_new = alpha * l_old + jnp.sum(p, axis=0, keepdims=True)
            pv = _dot(vt_ref[bb, rows, pl.ds(k0, tk)], p.astype(BF16))
            acc_ref[bb, rows, :] = alpha * acc_ref[bb, rows, :] + pv
            new_stats[bb][h] = (m_new, l_new)
        return tuple(tuple(new_stats[bb]) for bb in seqs)

    init = tuple(tuple((jnp.full((1, tq), -jnp.inf, F32), jnp.zeros((1, tq), F32)) for _ in range(HEADS)) for _ in seqs)
    stats = lax.fori_loop(0, qi, lambda kb, c: step(kb, c, False), init)
    stats = step(qi, stats, True)

    ones_bf = jnp.where(_head_ones(), 1.0, 0.0).astype(BF16)
    for bb in seqs:
        inv_l = jnp.concatenate([jnp.broadcast_to(1.0 / stats[bb][h][1], (HEAD_DIM, tq)) for h in range(HEADS)], axis=0)
        o_ref[bb] = _head_rms((acc_ref[bb] * inv_l).T, ones_bf)


def fox_attention(proj3, c_exp):
    b, s, _ = proj3.shape
    tq = SEQ_TILE
    gw = GROUP_WIDTH
    nb = FOX_BATCH
    return pl.pallas_call(
        _fox_kernel,
        grid=(b // nb, s // tq),
        in_specs=[pl.BlockSpec((nb, tq, gw), lambda i, j: (i, j, COL_BQ)),
                  pl.BlockSpec((nb, s, gw), lambda i, j: (i, 0, COL_BK)),
                  pl.BlockSpec((nb, s, gw), lambda i, j: (i, 0, COL_BV)),
                  pl.BlockSpec((nb, s, HEADS * SMALL_W), lambda i, j: (i, 0, 0))],
        out_specs=pl.BlockSpec((nb, tq, gw), lambda i, j: (i, j, 0)),
        out_shape=jax.ShapeDtypeStruct((b, s, gw), F32),
        scratch_shapes=[pltpu.VMEM((nb, s, gw), BF16),
                        pltpu.VMEM((nb, gw, s), BF16),
                        pltpu.VMEM((nb, gw, tq), F32)],
        compiler_params=_cparams("parallel", "arbitrary"),
        name="fox_attention",
    )(proj3, proj3, proj3, c_exp)


def _rglru_kernel(x_ref, g_ref, cw_ref, cb_ref, wa_ref, ba_ref, wx_ref, bx_ref, lam_ref, o_ref, prev_ref, h_ref):
    @pl.when(pl.program_id(1) == 0)
    def _():
        prev_ref[...] = jnp.zeros_like(prev_ref)
        h_ref[...] = jnp.zeros_like(h_ref)

    x = x_ref[0]
    t = x.shape[0]
    xc = _causal_conv4(x, prev_ref[...], cw_ref[...]) + cb_ref[...]
    prev_ref[...] = x[t - 8:t, :]
    xb = xc.astype(BF16)
    r = _sigmoid(_dot(xb, wa_ref[...]) + ba_ref[...])
    ig = _sigmoid(_dot(xb, wx_ref[...]) + bx_ref[...])
    log_a = (-LRU_C * r) * _softplus(-lam_ref[...])
    a = jnp.exp(log_a)
    u = jnp.sqrt(1.0 - jnp.exp(2.0 * log_a)) * (ig * xc)
    row = _iota(a.shape, 0)
    d = 1
    while d < t:
        valid = row >= d
        u = jnp.where(valid, a * pltpu.roll(u, d, 0) + u, u)
        a = jnp.where(valid, a * pltpu.roll(a, d, 0), a)
        d *= 2
    h = a * h_ref[...] + u
    h_ref[...] = h[t - 1:t, :]
    ones_bf = jnp.where(_head_ones(), 1.0, 0.0).astype(BF16)
    o_ref[0] = _head_rms(h * _gelu_tanh(g_ref[0]), ones_bf)


def rglru(proj3, conv_w, conv_b, wa_bd, ba, wx_bd, bx, lam):
    b, s, _ = proj3.shape
    t = ROW_TILE
    gw = GROUP_WIDTH
    row = lambda: pl.BlockSpec((1, gw), lambda i, j: (0, 0))
    return pl.pallas_call(
        _rglru_kernel,
        grid=(b, s // t),
        in_specs=[pl.BlockSpec((1, t, gw), lambda i, j: (i, j, COL_CX)),
                  pl.BlockSpec((1, t, gw), lambda i, j: (i, j, COL_CG)),
                  pl.BlockSpec((CONV_W, gw), lambda i, j: (0, 0)), row(),
                  pl.BlockSpec((gw, gw), lambda i, j: (0, 0)), row(),
                  pl.BlockSpec((gw, gw), lambda i, j: (0, 0)), row(), row()],
        out_specs=pl.BlockSpec((1, t, gw), lambda i, j: (i, j, 0)),
        out_shape=jax.ShapeDtypeStruct((b, s, gw), F32),
        scratch_shapes=[pltpu.VMEM((8, gw), F32), pltpu.VMEM((1, gw), F32)],
        compiler_params=_cparams("parallel", "arbitrary"),
        name="rglru",
    )(proj3, proj3, conv_w, conv_b.reshape(1, gw), wa_bd, ba.reshape(1, gw), wx_bd, bx.reshape(1, gw), lam.reshape(1, gw))


def _gdn_kernel(q_ref, k_ref, v_ref, z_ref, sm_ref, ge_ref, gr_ref, cw_ref, o_ref, st_ref, pq_ref, pk_ref, pv_ref):
    nb = q_ref.shape[0]
    T = q_ref.shape[1]
    C = GDN_CHUNK

    @pl.when(pl.program_id(1) == 0)
    def _():
        st_ref[...] = jnp.zeros_like(st_ref)
        pq_ref[...] = jnp.zeros_like(pq_ref)
        pk_ref[...] = jnp.zeros_like(pk_ref)
        pv_ref[...] = jnp.zeros_like(pv_ref)

    same_head = _head_ones()
    ones_bf = jnp.where(same_head, 1.0, 0.0).astype(BF16)
    cw = cw_ref[...]
    gw = GROUP_WIDTH
    er = _iota((SMALL_W, gw), 0)
    ec = _iota((SMALL_W, gw), 1) // HEAD_DIM
    exp_beta = jnp.where(er == LANE_BETA + ec, 1.0, 0.0).astype(BF16)
    exp_g = jnp.where(er == LANE_GDEC + ec, 1.0, 0.0).astype(BF16)
    n = HEADS * C
    rr = _iota((n, n), 0)
    cc = _iota((n, n), 1)
    incl = same_head & (rr % C >= cc % C)
    strict = same_head & (rr % C > cc % C)

    prep = []
    for bb in range(nb):
        xq, xk, xv = q_ref[bb], k_ref[bb], v_ref[bb]
        q = _silu(_causal_conv4(xq, pq_ref[bb], cw[:, 0:gw]))
        k = _silu(_causal_conv4(xk, pk_ref[bb], cw[:, gw:2 * gw]))
        v = _silu(_causal_conv4(xv, pv_ref[bb], cw[:, 2 * gw:3 * gw]))
        pq_ref[bb] = xq[T - 8:T, :]
        pk_ref[bb] = xk[T - 8:T, :]
        pv_ref[bb] = xv[T - 8:T, :]
        q = q * lax.rsqrt(_dot_exact_rhs(q * q, ones_bf) + EPS) * (HEAD_DIM ** -0.5)
        k = k * lax.rsqrt(_dot_exact_rhs(k * k, ones_bf) + EPS)
        sm = sm_ref[bb]
        betax = _dot_exact_rhs(sm, exp_beta)
        gx = _dot_exact_rhs(sm, exp_g)
        prep.append((q, k, v, betax, gx))

    reps = gw // SMALL_W
    n_chunks = T // C
    eye = jnp.where(rr == cc, 1.0, 0.0)

    chains = []
    for ci in range(n_chunks):
        sl = slice(ci * C, (ci + 1) * C)
        for bb in range(nb):
            q, k, v, betax, gx = prep[bb]
            qc, kc, vc, bx, gc = q[sl], k[sl], v[sl], betax[sl], gx[sl]
            grow = gr_ref[bb, ci, 0:1, :]
            ge = ge_ref[bb, sl, :]
            gcol = jnp.concatenate(
                [jnp.concatenate([ge[:, h * SMALL_W:(h + 1) * SMALL_W]] * reps, axis=1) for h in range(HEADS)], axis=0)
            eg = jnp.exp(gc)
            kb = kc * bx
            g_last = gc[C - 1:C, :]
            gamma = jnp.where(incl, jnp.exp(jnp.where(incl, gcol - grow, 0.0)), 0.0)
            ks_bf = _stack_heads(kc).astype(BF16)
            m = jnp.where(strict, _dot_nt(_stack_heads(kb).astype(BF16), ks_bf) * gamma, 0.0)
            chains.append(dict(
                x=eye - m, y=m,
                a_qk=(_dot_nt(_stack_heads(qc).astype(BF16), ks_bf) * gamma).astype(BF16),
                rhs_u=_stack_heads(vc * bx).astype(BF16),
                rhs_w=_stack_heads(kb * eg).astype(BF16),
                q_dec=_stack_heads(qc * eg).astype(BF16),
                k_dec=_stack_heads(kc * jnp.exp(g_last - gc)).astype(BF16),
                decay=jnp.exp(g_last)))
    for _ in range(5):
        for ch in chains:
            yb = ch["y"].astype(BF16)
            ch["y"] = _dot(yb, yb)
        for ch in chains:
            ch["x"] = ch["x"] + _dot(ch["x"].astype(BF16), ch["y"].astype(BF16))
    for ch in chains:
        t_inv = ch["x"].astype(BF16)
        ch["u"] = _dot(t_inv, ch["rhs_u"])
        ch["wk"] = _dot(t_inv, ch["rhs_w"]).astype(BF16)

    outs = [[] for _ in range(nb)]
    for ci in range(n_chunks):
        for bb in range(nb):
            ch = chains[ci * nb + bb]
            st = st_ref[bb]
            st_bf = st.astype(BF16)
            v_new = (ch["u"] - _dot(ch["wk"], st_bf)).astype(BF16)
            o = _dot(ch["q_dec"], st_bf) + _dot(ch["a_qk"], v_new)
            outs[bb].append(_unstack_heads(o, C))
            st_ref[bb] = st * ch["decay"] + _dot_tn(ch["k_dec"], v_new)
    for bb in range(nb):
        o = jnp.concatenate(outs[bb], axis=0)
        o_ref[bb] = _head_rms(o, ones_bf) * _silu(z_ref[bb])


def gdn(proj3, sm, c_exp, g_row, conv_w):
    b, s, _ = proj3.shape
    t = SEQ_TILE
    gw = GROUP_WIDTH
    nb = GDN_BATCH
    col = lambda cidx: pl.BlockSpec((nb, t, gw), lambda i, j: (i, j, cidx))
    return pl.pallas_call(
        _gdn_kernel,
        grid=(b // nb, s // t),
        in_specs=[col(COL_DQ), col(COL_DK), col(COL_DV), col(COL_DZ),
                  pl.BlockSpec((nb, t, SMALL_W), lambda i, j: (i, j, 0)),
                  pl.BlockSpec((nb, t, HEADS * SMALL_W), lambda i, j: (i, j, 1)),
                  pl.BlockSpec((nb, t // GDN_CHUNK, 8, gw), lambda i, j: (i, j, 0, 0)),
                  pl.BlockSpec((CONV_W, 3 * gw), lambda i, j: (0, 0))],
        out_specs=pl.BlockSpec((nb, t, gw), lambda i, j: (i, j, 0)),
        out_shape=jax.ShapeDtypeStruct((b, s, gw), F32),
        scratch_shapes=[pltpu.VMEM((nb, gw, gw), F32)] + [pltpu.VMEM((nb, 8, gw), F32)] * 3,
        compiler_params=_cparams("parallel", "arbitrary"),
        name="gdn",
    )(proj3, proj3, proj3, proj3, sm, c_exp, g_row, conv_w)


def _mix_out(x, ys, gain, w_ref):
    gw = GROUP_WIDTH
    acc = x
    for i, y in enumerate(ys):
        yg = (y * gain[:, i * gw:(i + 1) * gw]).astype(BF16)
        acc = acc + _dot(yg, w_ref[i * gw:(i + 1) * gw, :])
    return acc


def _mem_attention(x, g, wq_ref, kv, wo_ref):
    t = x.shape[0]
    gw = GROUP_WIDTH
    ms = jnp.mean(x * x, axis=-1, keepdims=True)
    h = ((x * lax.rsqrt(ms + EPS)) * g).astype(BF16)
    q = _dot(h, wq_ref[...])
    k = kv[:, 0:gw].astype(BF16)
    v = kv[:, gw:2 * gw].astype(BF16)
    s = _dot_nt(k, _stack_heads(q).astype(BF16)) * (HEAD_DIM ** -0.5)
    s = s - jnp.max(s, axis=0, keepdims=True)
    p = jnp.exp(s)
    p = p * (1.0 / jnp.sum(p, axis=0, keepdims=True))
    ot = _dot_tn(v, p.astype(BF16))
    oc = jnp.concatenate([ot[hh * HEAD_DIM:(hh + 1) * HEAD_DIM, hh * t:(hh + 1) * t] for hh in range(HEADS)], axis=0).T
    return x + _dot(oc.astype(BF16), wo_ref[...])


def _post_mix_kernel(x_ref, ya_ref, yb_ref, yc_ref, yd_ref, gain_ref, wout_ref, gm_ref, wq_ref, kv_ref, wo_ref,
                     g_ref, w_ref, b_ref, x2_ref, h_ref, r_ref, n_ref, cnt_ref):
    @pl.when((pl.program_id(0) == 0) & (pl.program_id(1) == 0))
    def _():
        cnt_ref[...] = jnp.zeros_like(cnt_ref)

    x1 = _mix_out(x_ref[0], (ya_ref[0], yb_ref[0], yc_ref[0], yd_ref[0]), gain_ref[...], wout_ref)
    x = _mem_attention(x1, gm_ref[...], wq_ref, kv_ref[0], wo_ref)
    x2_ref[0] = x
    tm = x.shape[0]
    ms = jnp.mean(x * x, axis=-1, keepdims=True)
    h = (x * lax.rsqrt(ms + EPS)) * g_ref[...]
    h_ref[...] = _pack_bf16_pairs(h)
    hh, hm, _ = _split3(h)
    w = w_ref[...]
    wh, wm, _ = _split3(w)
    logits = _dot(hh, wh) + _dot(hh, wm) + _dot(hm, wh) + b_ref[...]
    lane = _iota(logits.shape, 1)
    big = jnp.int32(1 << 30)
    neg = -jnp.inf
    is_grp = (lane >= N_EXPERTS) & (lane < N_EXPERTS + N_EXPERT_GROUPS)
    gl = jnp.where(is_grp, logits, neg)
    gmax = jnp.max(gl, axis=-1, keepdims=True)
    p_grp = 1.0 / jnp.sum(jnp.exp(gl - gmax), axis=-1, keepdims=True)
    g_sel = jnp.min(jnp.where(gl == gmax, lane, big), axis=-1, keepdims=True) - N_EXPERTS
    in_grp = (lane < N_EXPERTS) & (lane // EXPERTS_PER_GROUP == g_sel)
    el = jnp.where(in_grp, logits, neg)
    m1 = jnp.max(el, axis=-1, keepdims=True)
    i1 = jnp.min(jnp.where(el == m1, lane, big), axis=-1, keepdims=True)
    el2 = jnp.where(lane == i1, neg, el)
    m2 = jnp.max(el2, axis=-1, keepdims=True)
    i2 = jnp.min(jnp.where(el2 == m2, lane, big), axis=-1, keepdims=True)
    e21 = jnp.exp(m2 - m1)
    w1 = 1.0 / (1.0 + e21)
    w2 = e21 / (1.0 + e21)
    oh1 = lane == i1
    oh2 = lane == i2
    onehot = jnp.where(oh1 | oh2, 1.0, 0.0)
    strict = jnp.where(_iota((tm, tm), 0) > _iota((tm, tm), 1), 1.0, 0.0).astype(BF16)
    before = _dot(strict, onehot.astype(BF16)) + cnt_ref[...]
    rank1 = jnp.sum(jnp.where(oh1, before, 0.0), axis=-1, keepdims=True)
    rank2 = jnp.sum(jnp.where(oh2, before, 0.0), axis=-1, keepdims=True)
    cnt = before[tm - 1:tm, :] + onehot[tm - 1:tm, :]
    cnt_ref[...] = cnt
    n_ref[...] = jnp.broadcast_to(cnt, n_ref.shape)
    cols = (i1.astype(F32), i2.astype(F32), p_grp * w1, p_grp * w2, rank1, rank2)
    route = jnp.zeros(logits.shape, F32)
    for j, col in enumerate(cols):
        route = jnp.where(lane == j, col, route)
    r_ref[...] = route


def post_mix(x3, ya, yb, yc, yd, gain, w_out_bf16, g_mem, wq_bf16, kv3, wo_bf16, g_ffn, w_router, b_router):
    b, s, d = x3.shape
    t = ROW_TILE
    gw = GROUP_WIDTH
    nt = s // t
    mlen = kv3.shape[1]
    tile = lambda w: pl.BlockSpec((1, t, w), lambda i, j: (i, j, 0))
    const = lambda shape: pl.BlockSpec(shape, lambda i, j: (0,) * len(shape))
    tok = lambda w: pl.BlockSpec((t, w), lambda i, j: (i * nt + j, 0))
    return pl.pallas_call(
        _post_mix_kernel,
        grid=(b, nt),
        in_specs=[tile(d), tile(gw), tile(gw), tile(gw), tile(gw),
                  const((1, 4 * gw)), const((4 * gw, d)),
                  const((1, d)), const((d, gw)), pl.BlockSpec((1, mlen, 2 * gw), lambda i, j: (i, 0, 0)), const((gw, d)),
                  const((1, d)), const((d, SMALL_W)), const((1, SMALL_W))],
        out_specs=[tile(d), tok(d // 2), tok(SMALL_W), const((8, SMALL_W))],
        out_shape=[jax.ShapeDtypeStruct((b, s, d), F32), jax.ShapeDtypeStruct((b * s, d // 2), jnp.uint32),
                   jax.ShapeDtypeStruct((b * s, SMALL_W), F32), jax.ShapeDtypeStruct((8, SMALL_W), F32)],
        scratch_shapes=[pltpu.VMEM((1, SMALL_W), F32)],
        compiler_params=_cparams("arbitrary", "arbitrary"),
        name="post_mix",
    )(x3, ya, yb, yc, yd, gain.reshape(1, 4 * gw), w_out_bf16, g_mem.reshape(1, d), wq_bf16, kv3, wo_bf16,
      g_ffn.reshape(1, d), w_router, b_router)


def _experts_kernel(te_ref, nu_ref, xs_ref, wgu_ref, wdn_ref, y_ref, wgu_bf, wdn_bf):
    i = pl.program_id(0)

    @pl.when(i >= nu_ref[0])
    def _():
        y_ref[...] = jnp.zeros_like(y_ref)

    @pl.when(i < nu_ref[0])
    def _():
        prev = te_ref[jnp.maximum(i - 1, 0)]

        @pl.when((i == 0) | (te_ref[i] != prev))
        def _():
            wgu_bf[...] = wgu_ref[0, 0].astype(BF16)
            wdn_bf[...] = wdn_ref[0, 0].astype(BF16)

        gu = _dot(_unpack_bf16_pairs(xs_ref[...]).astype(BF16), wgu_bf[...])
        act = _silu(gu[:, 0:D_EXPERT]) * gu[:, D_EXPERT:2 * D_EXPERT]
        y_ref[...] = _pack_bf16_pairs(_dot(act.astype(BF16), wdn_bf[...]))


def moe_experts(xs, tile_expert, n_used, w_gu, w_dn, layer, tm):
    n_rows, dp = xs.shape
    d = 2 * dp
    tile = lambda i, te, nu: (jnp.minimum(i, nu[0] - 1), 0)
    wsel = lambda i, te, nu: (layer, te[jnp.minimum(i, nu[0] - 1)], 0, 0)
    return pl.pallas_call(
        _experts_kernel,
        grid_spec=pltpu.PrefetchScalarGridSpec(
            num_scalar_prefetch=2,
            grid=(n_rows // tm,),
            in_specs=[pl.BlockSpec((tm, dp), tile),
                      pl.BlockSpec((1, 1, d, 2 * D_EXPERT), wsel),
                      pl.BlockSpec((1, 1, D_EXPERT, d), wsel)],
            out_specs=pl.BlockSpec((tm, dp), lambda i, te, nu: (i, 0)),
            scratch_shapes=[pltpu.VMEM((d, 2 * D_EXPERT), BF16), pltpu.VMEM((D_EXPERT, d), BF16)]),
        out_shape=jax.ShapeDtypeStruct((n_rows, dp), jnp.uint32),
        compiler_params=_cparams("arbitrary"),
        name="moe_experts",
    )(tile_expert, n_used, xs, w_gu, w_dn)


def sc_gather_rows(table, idx):
    n_idx = idx.shape[0]
    width = table.shape[1]
    info = plsc.get_sparse_core_info()
    n_workers = info.num_cores * info.num_subcores
    per_worker = n_idx // n_workers
    chunk = SC_GATHER_CHUNK
    assert n_idx % (n_workers * chunk) == 0 and width % info.num_lanes == 0
    mesh = plsc.VectorSubcoreMesh(core_axis_name="c", subcore_axis_name="s")

    def body(table_hbm, idx_hbm, out_hbm, idx_v, rows_v, sem):
        wid = lax.axis_index("s") * info.num_cores + lax.axis_index("c")
        base = wid * per_worker

        @pl.loop(0, per_worker // chunk)
        def _(j):
            off = pl.multiple_of(base + j * chunk, chunk)
            pltpu.sync_copy(idx_hbm.at[pl.ds(off, chunk)], idx_v)
            pltpu.async_copy(table_hbm.at[idx_v], rows_v, sem).wait()
            pltpu.sync_copy(rows_v, out_hbm.at[pl.ds(off, chunk)])

    return pl.kernel(
        body,
        out_type=jax.ShapeDtypeStruct((n_idx, width), table.dtype),
        mesh=mesh,
        scratch_types=[pltpu.VMEM((chunk,), jnp.int32), pltpu.VMEM((chunk, width), table.dtype),
                       pltpu.SemaphoreType.DMA],
        name="sc_gather_rows",
    )(table, idx)


def sc_dispatch_rows(rows, dest, pad_idx, zero_rows):
    m, width = rows.shape
    n_pad = pad_idx.shape[0]
    info = plsc.get_sparse_core_info()
    n_workers = info.num_cores * info.num_subcores
    chunk = SC_GATHER_CHUNK
    assert m % (n_workers * chunk) == 0 and n_pad % (n_workers * chunk) == 0 and zero_rows.shape == (chunk, width)
    rows_per_worker = m // n_workers
    pad_per_worker = n_pad // n_workers
    mesh = plsc.VectorSubcoreMesh(core_axis_name="c", subcore_axis_name="s")

    def body(rows_hbm, dest_hbm, pad_hbm, zero_hbm, out_hbm, idx_v, rows_v, sem):
        wid = lax.axis_index("s") * info.num_cores + lax.axis_index("c")

        @pl.loop(0, rows_per_worker // chunk)
        def _(j):
            off = pl.multiple_of(wid * rows_per_worker + j * chunk, chunk)
            pltpu.sync_copy(rows_hbm.at[pl.ds(off, chunk)], rows_v)
            for slot in range(2):
                pltpu.sync_copy(dest_hbm.at[pl.ds(slot * m + off, chunk)], idx_v)
                pltpu.async_copy(rows_v, out_hbm.at[idx_v], sem).wait()

        pltpu.sync_copy(zero_hbm, rows_v)

        @pl.loop(0, pad_per_worker // chunk)
        def _(j):
            off = pl.multiple_of(wid * pad_per_worker + j * chunk, chunk)
            pltpu.sync_copy(pad_hbm.at[pl.ds(off, chunk)], idx_v)
            pltpu.async_copy(rows_v, out_hbm.at[idx_v], sem).wait()

    return pl.kernel(
        body,
        out_type=jax.ShapeDtypeStruct((2 * m + n_pad, width), rows.dtype),
        mesh=mesh,
        scratch_types=[pltpu.VMEM((chunk,), jnp.int32), pltpu.VMEM((chunk, width), rows.dtype),
                       pltpu.SemaphoreType.DMA],
        name="sc_dispatch_rows",
    )(rows, dest, pad_idx, zero_rows)


def _combine_rows_kernel(x_ref, r_ref, g_ref, y1_ref, y2_ref, o_ref, *, final_norm):
    r = r_ref[...]
    out = x_ref[...] + r[:, 2:3] * _unpack_bf16_pairs(y1_ref[...]) + r[:, 3:4] * _unpack_bf16_pairs(y2_ref[...])
    if final_norm:
        ms = jnp.mean(out * out, axis=-1, keepdims=True)
        out = (out * lax.rsqrt(ms + EPS)) * g_ref[...]
    o_ref[...] = out


def moe_combine_rows(x2, route, y_rows, g_final, tm, final_norm):
    m, d = x2.shape
    steps = m // tm
    return pl.pallas_call(
        functools.partial(_combine_rows_kernel, final_norm=final_norm),
        grid=(steps,),
        in_specs=[pl.BlockSpec((tm, d), lambda i: (i, 0)),
                  pl.BlockSpec((tm, SMALL_W), lambda i: (i, 0)),
                  pl.BlockSpec((1, d), lambda i: (0, 0)),
                  pl.BlockSpec((tm, d // 2), lambda i: (i, 0)),
                  pl.BlockSpec((tm, d // 2), lambda i: (i + steps, 0))],
        out_specs=pl.BlockSpec((tm, d), lambda i: (i, 0)),
        out_shape=jax.ShapeDtypeStruct((m, d), F32),
        compiler_params=_cparams("parallel"),
        name="moe_combine_rows",
    )(x2, route, g_final.reshape(1, d), y_rows, y_rows)


def moe_routed(x2, h, route, counts, w_gu, w_dn, layer, g_final, final_norm):
    m, d = x2.shape
    te_rows = EXPERT_TILE
    n_rows = 2 * m + N_EXPERTS * te_rows
    cnt = counts[0, 0:N_EXPERTS].astype(jnp.int32)
    padded = (cnt + te_rows - 1) // te_rows * te_rows
    seg_end = jnp.cumsum(padded)
    seg_start = seg_end - padded
    idx = route[:, 0:2].astype(jnp.int32)
    onehot = idx[:, :, None] == jnp.arange(N_EXPERTS, dtype=jnp.int32)
    dest = (jnp.sum(jnp.where(onehot, seg_start, 0), axis=-1) + route[:, 4:6].astype(jnp.int32)).T.reshape(2 * m)
    tile_start = jnp.arange(n_rows // te_rows, dtype=jnp.int32) * te_rows
    tile_expert = jnp.minimum(jnp.sum((tile_start[:, None] >= seg_end[None, :]).astype(jnp.int32), axis=1), N_EXPERTS - 1)
    n_used = (seg_end[N_EXPERTS - 1] // te_rows).reshape(1)
    n_pad = n_rows - 2 * m
    experts = jnp.arange(N_EXPERTS, dtype=jnp.int32)
    pad_end = jnp.cumsum(padded - cnt)
    k = jnp.arange(n_pad, dtype=jnp.int32)
    k_expert = jnp.sum((k[:, None] >= pad_end[None, :]).astype(jnp.int32), axis=1)
    first_pad = seg_start + cnt - (pad_end - (padded - cnt))
    in_segment = jnp.sum(jnp.where(k_expert[:, None] == experts[None, :], first_pad[None, :], 0), axis=1) + k
    pad_idx = jnp.where(k_expert < N_EXPERTS, in_segment, seg_end[N_EXPERTS - 1] + k - pad_end[N_EXPERTS - 1])
    xs = sc_dispatch_rows(h, dest, pad_idx, jnp.zeros((SC_GATHER_CHUNK, d // 2), h.dtype))
    ys = moe_experts(xs, tile_expert, n_used, w_gu, w_dn, layer, tm=te_rows)
    y_rows = sc_gather_rows(ys, dest)
    return moe_combine_rows(x2, route, y_rows, g_final, tm=ROW_TILE, final_norm=final_norm)


def _w_in_prep_kernel(w_ref, o_ref):
    gw = GROUP_WIDTH
    a_end = 7 * gw
    c_start = a_end + HEADS
    c_end = c_start + 5 * gw
    z_start = c_end + 2 * HEADS
    assert a_end % SMALL_W == LANE_FOX and c_end % SMALL_W == LANE_BETA and LANE_GDEC == LANE_BETA + HEADS
    w = w_ref[...]
    o_ref[:, 0:a_end] = w[:, 0:a_end].astype(BF16)
    o_ref[:, a_end:a_end + 5 * gw] = w[:, c_start:c_end].astype(BF16)
    o_ref[:, 12 * gw:13 * gw] = w[:, z_start:z_start + gw].astype(BF16)
    blk_fox = w[:, a_end:a_end + SMALL_W]
    blk_gdn = w[:, c_end - LANE_BETA:c_end - LANE_BETA + SMALL_W]
    lane = _iota(blk_fox.shape, 1)
    small = jnp.where(lane < LANE_BETA, blk_fox, jnp.where(lane < LANE_GDEC + HEADS, blk_gdn, 0.0))
    o_ref[:, 13 * gw:13 * gw + SMALL_W] = small.astype(BF16)
    o_ref[:, 13 * gw + SMALL_W:] = jnp.zeros((w.shape[0], SMALL_W), BF16)


def w_in_prep(w_all, layer):
    _, d, n = w_all.shape
    tr = 128
    return pl.pallas_call(
        _w_in_prep_kernel,
        grid=(d // tr,),
        in_specs=[pl.BlockSpec((None, tr, n), lambda i: (layer, i, 0))],
        out_specs=pl.BlockSpec((tr, N_PROJ), lambda i: (i, 0)),
        out_shape=jax.ShapeDtypeStruct((d, N_PROJ), BF16),
        compiler_params=_cparams("parallel"),
        name="w_in_prep",
    )(w_all)


def _block_diag(w):
    h = w.shape[0]
    eye = jnp.eye(h, dtype=w.dtype)
    return (eye[:, None, :, None] * w[:, :, None, :]).reshape(h * HEAD_DIM, h * HEAD_DIM)


def kernel(x, mem, norm_mix, w_in, hgrn_lb, fox_fb, lru_conv_w, lru_conv_b, lru_wa, lru_ba, lru_wx, lru_bx, lru_lam, gdn_conv_w, gdn_a_log, gdn_dt_bias, mix_gain, w_out, norm_mem, norm_memkv, w_mq, w_mkv, w_mo, norm_ffn, w_rg, b_rg, w_re, b_re, w_e_gu, w_e_dn, norm_final):
    b, s, d = x.shape
    depth = w_in.shape[0]
    t_tok = b * s
    mlen = mem.shape[1]
    gw = GROUP_WIDTH

    lb_all = jnp.cumsum(jax.nn.softmax(hgrn_lb.astype(F32), axis=0), axis=0)
    lb_all = lb_all - lb_all[0]

    x = x.astype(F32)
    for l in range(depth):
        proj = norm_matmul(x.reshape(t_tok, d), norm_mix[l], w_in_prep(w_in, l), tm=ROW_TILE)
        proj3 = proj.reshape(b, s, N_PROJ)
        prm = jnp.zeros((8, SMALL_W), F32)
        prm = prm.at[0, LANE_FOX:LANE_FOX + HEADS].set(fox_fb[l].astype(F32))
        prm = prm.at[0, LANE_GDEC:LANE_GDEC + HEADS].set(gdn_dt_bias[l].astype(F32))
        prm = prm.at[1, LANE_GDEC:LANE_GDEC + HEADS].set(gdn_a_log[l].astype(F32))
        sm, c_exp = small_prep(proj3, prm)
        g_row = sm[:, :, LANE_GDEC:LANE_GDEC + HEADS].reshape(b, s // GDN_CHUNK, GDN_CHUNK, HEADS)
        g_row = jnp.swapaxes(g_row, 2, 3).reshape(b, s // GDN_CHUNK, 1, gw)
        g_row = jnp.broadcast_to(g_row, (b, s // GDN_CHUNK, 8, gw))

        ya = hgrn2(proj3, lb_all[l])
        yb = fox_attention(proj3, c_exp)
        yc = rglru(proj3, lru_conv_w[l], lru_conv_b[l], _block_diag(lru_wa[l]).astype(BF16), lru_ba[l],
                   _block_diag(lru_wx[l]).astype(BF16), lru_bx[l], lru_lam[l])
        yd = gdn(proj3, sm, c_exp, g_row, gdn_conv_w[l])

        kv = norm_matmul(mem.reshape(b * mlen, d), norm_memkv[l], w_mkv[l].astype(BF16), tm=256)
        w_router = jnp.concatenate([w_re[l], w_rg[l], jnp.zeros((d, SMALL_W - N_EXPERTS - N_EXPERT_GROUPS), F32)], axis=1)
        b_router = jnp.concatenate([b_re[l], b_rg[l], jnp.zeros((SMALL_W - N_EXPERTS - N_EXPERT_GROUPS,), F32)]).reshape(1, SMALL_W)
        x, h, route, counts = post_mix(x, ya, yb, yc, yd, mix_gain[l], w_out[l].astype(BF16), norm_mem[l],
                                       w_mq[l].astype(BF16), kv.reshape(b, mlen, 2 * gw), w_mo[l].astype(BF16),
                                       norm_ffn[l], w_router, b_router)

        x = moe_routed(x.reshape(t_tok, d), h, route, counts, w_e_gu, w_e_dn, l,
                       norm_final, final_norm=(l == depth - 1)).reshape(b, s, d)

    return x
```

```python
import functools
import math

import jax
import jax.numpy as jnp
from jax import lax
from jax.experimental import pallas as pl
from jax.experimental.pallas import tpu as pltpu
from jax.experimental.pallas import tpu_sc as plsc

F32 = jnp.float32
BF16 = jnp.bfloat16

HEAD_DIM = 64
GROUP_WIDTH = 256
HEADS = GROUP_WIDTH // HEAD_DIM
GDN_CHUNK = 64
GDN_BATCH = 2
FOX_BATCH = 2
HGRN_BATCH = 2
HGRN_CHUNK = 16
CONV_W = 4
LRU_C = 8.0
EPS = 1e-6
N_EXPERT_GROUPS = 4
EXPERTS_PER_GROUP = 8
N_EXPERTS = N_EXPERT_GROUPS * EXPERTS_PER_GROUP
D_EXPERT = 256
SC_GATHER_CHUNK = 128
EXPERT_TILE = 512
SMALL_W = 128
SEQ_TILE = 256
ROW_TILE = 512
VMEM_LIMIT = 56 * 1024 * 1024

(COL_AQ, COL_AF, COL_AI, COL_AG, COL_BQ, COL_BK, COL_BV, COL_CX, COL_CG,
 COL_DQ, COL_DK, COL_DV, COL_DZ) = range(13)
N_PROJ = 13 * GROUP_WIDTH + 2 * SMALL_W
COL_SMALL = 13 * GROUP_WIDTH // SMALL_W
LANE_FOX = 0
LANE_BETA = 4
LANE_GDEC = 8


def _cparams(*sem):
    return pltpu.CompilerParams(dimension_semantics=sem, vmem_limit_bytes=VMEM_LIMIT)


def _dot(a, b):
    return jnp.dot(a, b, preferred_element_type=F32)


def _dot_nt(a, b):
    return lax.dot_general(a, b, (((1,), (1,)), ((), ())), preferred_element_type=F32)


def _dot_tn(a, b):
    return lax.dot_general(a, b, (((0,), (0,)), ((), ())), preferred_element_type=F32)


def _split3(x):
    h = x.astype(BF16)
    r = x - h.astype(F32)
    m = r.astype(BF16)
    l = (r - m.astype(F32)).astype(BF16)
    return h, m, l


def _dot_exact_rhs(x, w_bf16):
    h, m, l = _split3(x)
    return _dot(h, w_bf16) + _dot(m, w_bf16) + _dot(l, w_bf16)


def _dot_exact_lhs(w_bf16, x):
    h, m, l = _split3(x)
    return _dot(w_bf16, h) + _dot(w_bf16, m) + _dot(w_bf16, l)


def _iota(shape, dim):
    return lax.broadcasted_iota(jnp.int32, shape, dim)


def _head_ones(n=GROUP_WIDTH):
    r = _iota((n, n), 0) // HEAD_DIM
    c = _iota((n, n), 1) // HEAD_DIM
    return r == c


def _sigmoid(x):
    return 1.0 / (1.0 + jnp.exp(-x))


def _silu(x):
    return x * _sigmoid(x)


def _log_sigmoid(x):
    return jnp.minimum(x, 0.0) - jnp.log1p(jnp.exp(-jnp.abs(x)))


def _softplus(x):
    return jnp.maximum(x, 0.0) + jnp.log1p(jnp.exp(-jnp.abs(x)))


def _gelu_tanh(x):
    return 0.5 * x * (1.0 + jnp.tanh(math.sqrt(2.0 / math.pi) * (x + 0.044715 * (x * x * x))))


def _pack_bf16_pairs(x):
    n = x.shape[1] // 2
    u = lax.bitcast_convert_type(x.astype(BF16).astype(F32), jnp.uint32)
    return u[:, :n] | (u[:, n:] >> 16)


def _unpack_bf16_pairs(p):
    hi = lax.bitcast_convert_type(p & jnp.uint32(0xFFFF0000), F32)
    lo = lax.bitcast_convert_type(p << 16, F32)
    return jnp.concatenate([hi, lo], axis=1)


def _head_mean_sq(x, ones_bf16):
    return _dot_exact_rhs(x * x, ones_bf16) * (1.0 / HEAD_DIM)


def _head_rms(x, ones_bf16):
    return x * lax.rsqrt(_head_mean_sq(x, ones_bf16) + EPS)


def _stack_heads(x):
    lane_head = _iota(x.shape, 1) // HEAD_DIM
    parts = []
    for h in range(HEADS):
        parts.append(jnp.where(lane_head == h, x, 0.0))
    return jnp.concatenate(parts, axis=0)


def _causal_conv4(x, prev8, w):
    r = x.shape[0]
    row8 = _iota((8, x.shape[1]), 0)
    acc = x * w[CONV_W - 1:CONV_W, :]
    for k in range(1, CONV_W):
        xs = pltpu.roll(x, k, 0)
        ps = pltpu.roll(prev8, k, 0)
        top = jnp.where(row8 < k, ps, xs[0:8])
        xs = jnp.concatenate([top, xs[8:r]], axis=0)
        acc = acc + xs * w[CONV_W - 1 - k:CONV_W - k, :]
    return acc


def _norm_matmul_kernel(x_ref, g_ref, w_ref, o_ref):
    x = x_ref[...]
    ms = jnp.mean(x * x, axis=-1, keepdims=True)
    h = (x * lax.rsqrt(ms + EPS)) * g_ref[...]
    o_ref[...] = _dot(h.astype(BF16), w_ref[...]).astype(o_ref.dtype)


def norm_matmul(x, g, w_bf16, tm, out_dtype=F32):
    m, d = x.shape
    n = w_bf16.shape[1]
    return pl.pallas_call(
        _norm_matmul_kernel,
        grid=(m // tm,),
        in_specs=[pl.BlockSpec((tm, d), lambda i: (i, 0)),
                  pl.BlockSpec((1, d), lambda i: (0, 0)),
                  pl.BlockSpec((d, n), lambda i: (0, 0))],
        out_specs=pl.BlockSpec((tm, n), lambda i: (i, 0)),
        out_shape=jax.ShapeDtypeStruct((m, n), out_dtype),
        compiler_params=_cparams("parallel"),
        name="norm_matmul",
    )(x, g.reshape(1, d), w_bf16)


def _small_prep_kernel(s_ref, p_ref, o_ref, x_ref):
    S = s_ref.shape[1]
    W = s_ref.shape[2]
    blk = SEQ_TILE
    lane = _iota((blk, W), 1)
    is_fox = (lane >= LANE_FOX) & (lane < LANE_FOX + HEADS)
    is_beta = (lane >= LANE_BETA) & (lane < LANE_BETA + HEADS)
    is_gdec = (lane >= LANE_GDEC) & (lane < LANE_GDEC + HEADS)
    r = _iota((blk, blk), 0)
    c = _iota((blk, blk), 1)
    tril_all = jnp.where(r >= c, 1.0, 0.0).astype(BF16)
    tril_chunk = jnp.where((r >= c) & (r // GDN_CHUNK == c // GDN_CHUNK), 1.0, 0.0).astype(BF16)
    neg_exp_a = -jnp.exp(p_ref[1:2, :])
    carry = jnp.zeros((1, W), F32)
    for i in range(S // blk):
        sl = pl.ds(i * blk, blk)
        z = s_ref[0, sl, :] + p_ref[0:1, :]
        fox = jnp.where(is_fox, _log_sigmoid(z), 0.0)
        beta = jnp.where(is_beta, _sigmoid(z), 0.0)
        gdec = jnp.where(is_gdec, neg_exp_a * _softplus(z), 0.0)
        cf = _dot_exact_lhs(tril_all, fox) + carry
        cg = _dot_exact_lhs(tril_chunk, gdec)
        carry = cf[blk - 1:blk, :]
        o_ref[0, sl, :] = cf + cg + beta
        both = cf + cg
        for j, src_lane in enumerate(range(LANE_FOX, LANE_FOX + HEADS)):
            x_ref[0, sl, j * W:(j + 1) * W] = jnp.broadcast_to(both[:, src_lane:src_lane + 1], (blk, W))


def small_prep(proj3, params):
    b, s, _ = proj3.shape
    w = SMALL_W
    return pl.pallas_call(
        _small_prep_kernel,
        grid=(b,),
        in_specs=[pl.BlockSpec((1, s, w), lambda i: (i, 0, COL_SMALL)),
                  pl.BlockSpec((8, w), lambda i: (0, 0))],
        out_specs=[pl.BlockSpec((1, s, w), lambda i: (i, 0, 0)),
                   pl.BlockSpec((1, s, HEADS * w), lambda i: (i, 0, 0))],
        out_shape=[jax.ShapeDtypeStruct((b, s, w), F32), jax.ShapeDtypeStruct((b, s, HEADS * w), F32)],
        compiler_params=_cparams("parallel"),
        name="small_prep",
    )(proj3, params)


def _hgrn2_kernel(q_ref, f_ref, i_ref, g_ref, lb_ref, o_ref, st_ref, qs, ks, vs, bs, os_):
    nb = q_ref.shape[0]
    T = q_ref.shape[1]
    C = HGRN_CHUNK
    seqs = range(nb)

    @pl.when(pl.program_id(1) == 0)
    def _():
        st_ref[...] = jnp.zeros_like(st_ref)

    same_head = _head_ones()
    ones_bf = jnp.where(same_head, 1.0, 0.0).astype(BF16)
    lb = lb_ref[...]
    log_lb = jnp.log(lb)
    r = _iota((T, T), 0)
    c = _iota((T, T), 1)
    tril_chunk = jnp.where((r >= c) & (r // C == c // C), 1.0, 0.0).astype(BF16)
    for bb in seqs:
        fl = f_ref[bb]
        c2 = jnp.log1p(-lb) + _log_sigmoid(fl)
        mx = jnp.maximum(log_lb, c2)
        log_f = mx + jnp.log1p(jnp.exp(-jnp.abs(log_lb - c2)))
        bs[bb] = _dot_exact_lhs(tril_chunk, log_f)
        qs[bb] = _silu(q_ref[bb])
        ks[bb] = (1.0 - lb) * _sigmoid(-fl)
        vs[bb] = i_ref[bb]

    H8 = C // 2
    trow8 = _iota((H8, GROUP_WIDTH), 0)

    def chunk(ci, carry):
        r0 = pl.multiple_of(ci * C, C)
        qc = [qs[bb, pl.ds(r0, C), :] for bb in seqs]
        kc = [ks[bb, pl.ds(r0, C), :] for bb in seqs]
        vc = [vs[bb, pl.ds(r0, C), :] for bb in seqs]
        bc = [bs[bb, pl.ds(r0, C), :] for bb in seqs]
        st = [st_ref[bb] for bb in seqs]
        a = []
        for bb in seqs:
            parts = []
            for half, s_range in ((0, range(H8)), (1, range(C))):
                b_half = bc[bb][half * H8:(half + 1) * H8, :]
                q_half = qc[bb][half * H8:(half + 1) * H8, :]
                for s in s_range:
                    blk = jnp.exp(b_half - bc[bb][s:s + 1, :]) * (q_half * kc[bb][s:s + 1, :])
                    if s >= half * H8:
                        blk = jnp.where(trow8 >= s - half * H8, blk, 0.0)
                    parts.append(blk)
            a.append(jnp.concatenate(parts, axis=0).astype(BF16))
        sc = [_dot(a[bb], ones_bf) for bb in seqs]
        o = [_dot_nt((qc[bb] * jnp.exp(bc[bb])).astype(BF16), st[bb].astype(BF16)) for bb in seqs]
        b_last = [bc[bb][C - 1:C, :] for bb in seqs]
        upd = [_dot_tn(vc[bb].astype(BF16), (kc[bb] * jnp.exp(b_last[bb] - bc[bb])).astype(BF16)) for bb in seqs]
        for bb in seqs:
            top = o[bb][0:H8, :]
            bot = o[bb][H8:C, :]
            for s in range(H8):
                top = top + sc[bb][s * H8:(s + 1) * H8, :] * vc[bb][s:s + 1, :]
            for s in range(C):
                bot = bot + sc[bb][(H8 + s) * H8:(H8 + s + 1) * H8, :] * vc[bb][s:s + 1, :]
            os_[bb, pl.ds(r0, C), :] = jnp.concatenate([top, bot], axis=0)
            st_ref[bb] = st[bb] * jnp.exp(b_last[bb]) + jnp.where(same_head, upd[bb], 0.0)
        return carry

    lax.fori_loop(0, T // C, chunk, 0)
    for bb in seqs:
        o_ref[bb] = _head_rms(os_[bb], ones_bf) * _silu(g_ref[bb])


def hgrn2(proj3, lb):
    b, s, _ = proj3.shape
    t = SEQ_TILE
    gw = GROUP_WIDTH
    nb = HGRN_BATCH
    col = lambda cidx: pl.BlockSpec((nb, t, gw), lambda i, j: (i, j, cidx))
    return pl.pallas_call(
        _hgrn2_kernel,
        grid=(b // nb, s // t),
        in_specs=[col(COL_AQ), col(COL_AF), col(COL_AI), col(COL_AG),
                  pl.BlockSpec((1, gw), lambda i, j: (0, 0))],
        out_specs=pl.BlockSpec((nb, t, gw), lambda i, j: (i, j, 0)),
        out_shape=jax.ShapeDtypeStruct((b, s, gw), F32),
        scratch_shapes=[pltpu.VMEM((nb, gw, gw), F32)] + [pltpu.VMEM((nb, t, gw), F32)] * 5,
        compiler_params=_cparams("parallel", "arbitrary"),
        name="hgrn2",
    )(proj3, proj3, proj3, proj3, lb.reshape(1, gw))


def _fox_kernel(q_ref, k_ref, v_ref, cx_ref, o_ref, kb_ref, vt_ref, acc_ref):
    nb = q_ref.shape[0]
    tq = q_ref.shape[1]
    tk = tq
    S = k_ref.shape[1]
    qi = pl.program_id(1)
    seqs = range(nb)

    @pl.when(qi == 0)
    def _():
        for bb in seqs:
            for i in range(S // tk):
                sl = pl.ds(i * tk, tk)
                kb_ref[bb, sl, :] = k_ref[bb, sl, :].astype(BF16)
                vt_ref[bb, :, sl] = v_ref[bb, sl, :].T.astype(BF16)

    lane_head = _iota((tq, GROUP_WIDTH), 1) // HEAD_DIM
    qh = []
    for bb in seqs:
        qsc = q_ref[bb] * (HEAD_DIM ** -0.5)
        qh.append([jnp.where(lane_head == h, qsc, 0.0).astype(BF16) for h in range(HEADS)])
    acc_ref[...] = jnp.zeros_like(acc_ref)
    reps = tq // SMALL_W
    causal = _iota((tk, tq), 0) <= _iota((tk, tq), 1)

    def step(kb, stats, masked):
        k0 = pl.multiple_of(kb * tk, tk)
        new_stats = [[None] * HEADS for _ in seqs]
        strips = [(h, bb) for h in range(HEADS) for bb in seqs]

        def scores(h, bb):
            return _dot_nt(kb_ref[bb, pl.ds(k0, tk), :], qh[bb][h])

        ahead = 3
        queue = [scores(*strips[j]) for j in range(ahead)]
        for i, (h, bb) in enumerate(strips):
            st = queue.pop(0)
            if i + ahead < len(strips):
                queue.append(scores(*strips[i + ahead]))
            rows = slice(h * HEAD_DIM, (h + 1) * HEAD_DIM)
            cx = cx_ref[bb, pl.ds(k0, tk), h * SMALL_W:(h + 1) * SMALL_W]
            st = st - jnp.concatenate([cx] * reps, axis=1)
            if masked:
                st = jnp.where(causal, st, -jnp.inf)
            m_old, l_old = stats[bb][h]
            m_new = jnp.maximum(m_old, jnp.max(st, axis=0, keepdims=True))
            alpha = jnp.exp(m_old - m_new)
            p = jnp.exp(st - m_new)
            l_new = alpha * l_old + jnp.sum(p, axis=0, keepdims=True)
            pv = _dot(vt_ref[bb, rows, pl.ds(k0, tk)], p.astype(BF16))
            acc_ref[bb, rows, :] = alpha * acc_ref[bb, rows, :] + pv
            new_stats[bb][h] = (m_new, l_new)
        return tuple(tuple(new_stats[bb]) for bb in seqs)

    init = tuple(tuple((jnp.full((1, tq), -jnp.inf, F32), jnp.zeros((1, tq), F32)) for _ in range(HEADS)) for _ in seqs)
    stats = lax.fori_loop(0, qi, lambda kb, c: step(kb, c, False), init)
    stats = step(qi, stats, True)

    ones_bf = jnp.where(_head_ones(), 1.0, 0.0).astype(BF16)
    for bb in seqs:
        inv_l = jnp.concatenate([jnp.broadcast_to(1.0 / stats[bb][h][1], (HEAD_DIM, tq)) for h in range(HEADS)], axis=0)
        o_ref[bb] = _head_rms((acc_ref[bb] * inv_l).T, ones_bf)


def fox_attention(proj3, c_exp):
    b, s, _ = proj3.shape
    tq = SEQ_TILE
    gw = GROUP_WIDTH
    nb = FOX_BATCH
    return pl.pallas_call(
        _fox_kernel,
        grid=(b // nb, s // tq),
        in_specs=[pl.BlockSpec((nb, tq, gw), lambda i, j: (i, j, COL_BQ)),
                  pl.BlockSpec((nb, s, gw), lambda i, j: (i, 0, COL_BK)),
                  pl.BlockSpec((nb, s, gw), lambda i, j: (i, 0, COL_BV)),
                  pl.BlockSpec((nb, s, HEADS * SMALL_W), lambda i, j: (i, 0, 0))],
        out_specs=pl.BlockSpec((nb, tq, gw), lambda i, j: (i, j, 0)),
        out_shape=jax.ShapeDtypeStruct((b, s, gw), F32),
        scratch_shapes=[pltpu.VMEM((nb, s, gw), BF16),
                        pltpu.VMEM((nb, gw, s), BF16),
                        pltpu.VMEM((nb, gw, tq), F32)],
        compiler_params=_cparams("parallel", "arbitrary"),
        name="fox_attention",
    )(proj3, proj3, proj3, c_exp)


def _rglru_kernel(x_ref, g_ref, cw_ref, cb_ref, wa_ref, ba_ref, wx_ref, bx_ref, lam_ref, o_ref, prev_ref, h_ref):
    @pl.when(pl.program_id(1) == 0)
    def _():
        prev_ref[...] = jnp.zeros_like(prev_ref)
        h_ref[...] = jnp.zeros_like(h_ref)

    x = x_ref[0]
    t = x.shape[0]
    xc = _causal_conv4(x, prev_ref[...], cw_ref[...]) + cb_ref[...]
    prev_ref[...] = x[t - 8:t, :]
    xb = xc.astype(BF16)
    r = _sigmoid(_dot(xb, wa_ref[...]) + ba_ref[...])
    ig = _sigmoid(_dot(xb, wx_ref[...]) + bx_ref[...])
    log_a = (-LRU_C * r) * _softplus(-lam_ref[...])
    a = jnp.exp(log_a)
    u = jnp.sqrt(1.0 - jnp.exp(2.0 * log_a)) * (ig * xc)
    row = _iota(a.shape, 0)
    d = 1
    while d < t:
        valid = row >= d
        u = jnp.where(valid, a * pltpu.roll(u, d, 0) + u, u)
        a = jnp.where(valid, a * pltpu.roll(a, d, 0), a)
        d *= 2
    h = a * h_ref[...] + u
    h_ref[...] = h[t - 1:t, :]
    ones_bf = jnp.where(_head_ones(), 1.0, 0.0).astype(BF16)
    o_ref[0] = _head_rms(h * _gelu_tanh(g_ref[0]), ones_bf)


def rglru(proj3, conv_w, conv_b, wa_bd, ba, wx_bd, bx, lam):
    b, s, _ = proj3.shape
    t = ROW_TILE
    gw = GROUP_WIDTH
    row = lambda: pl.BlockSpec((1, gw), lambda i, j: (0, 0))
    return pl.pallas_call(
        _rglru_kernel,
        grid=(b, s // t),
        in_specs=[pl.BlockSpec((1, t, gw), lambda i, j: (i, j, COL_CX)),
                  pl.BlockSpec((1, t, gw), lambda i, j: (i, j, COL_CG)),
                  pl.BlockSpec((CONV_W, gw), lambda i, j: (0, 0)), row(),
                  pl.BlockSpec((gw, gw), lambda i, j: (0, 0)), row(),
                  pl.BlockSpec((gw, gw), lambda i, j: (0, 0)), row(), row()],
        out_specs=pl.BlockSpec((1, t, gw), lambda i, j: (i, j, 0)),
        out_shape=jax.ShapeDtypeStruct((b, s, gw), F32),
        scratch_shapes=[pltpu.VMEM((8, gw), F32), pltpu.VMEM((1, gw), F32)],
        compiler_params=_cparams("parallel", "arbitrary"),
        name="rglru",
    )(proj3, proj3, conv_w, conv_b.reshape(1, gw), wa_bd, ba.reshape(1, gw), wx_bd, bx.reshape(1, gw), lam.reshape(1, gw))


def _gdn_kernel(q_ref, k_ref, v_ref, z_ref, sm_ref, gr_ref, cw_ref, o_ref, st_ref, pq_ref, pk_ref, pv_ref):
    nb = q_ref.shape[0]
    T = q_ref.shape[1]
    C = GDN_CHUNK

    @pl.when(pl.program_id(1) == 0)
    def _():
        st_ref[...] = jnp.zeros_like(st_ref)
        pq_ref[...] = jnp.zeros_like(pq_ref)
        pk_ref[...] = jnp.zeros_like(pk_ref)
        pv_ref[...] = jnp.zeros_like(pv_ref)

    same_head = _head_ones()
    ones_bf = jnp.where(same_head, 1.0, 0.0).astype(BF16)
    cw = cw_ref[...]
    gw = GROUP_WIDTH
    er = _iota((SMALL_W, gw), 0)
    ec = _iota((SMALL_W, gw), 1) // HEAD_DIM
    exp_beta = jnp.where(er == LANE_BETA + ec, 1.0, 0.0).astype(BF16)
    exp_g = jnp.where(er == LANE_GDEC + ec, 1.0, 0.0).astype(BF16)

    prep = []
    for bb in range(nb):
        xq, xk, xv = q_ref[bb], k_ref[bb], v_ref[bb]
        q = _silu(_causal_conv4(xq, pq_ref[bb], cw[:, 0:gw]))
        k = _silu(_causal_conv4(xk, pk_ref[bb], cw[:, gw:2 * gw]))
        v = _silu(_causal_conv4(xv, pv_ref[bb], cw[:, 2 * gw:3 * gw]))
        pq_ref[bb] = xq[T - 8:T, :]
        pk_ref[bb] = xk[T - 8:T, :]
        pv_ref[bb] = xv[T - 8:T, :]
        q = q * lax.rsqrt(_dot_exact_rhs(q * q, ones_bf) + EPS) * (HEAD_DIM ** -0.5)
        k = k * lax.rsqrt(_dot_exact_rhs(k * k, ones_bf) + EPS)
        sm = sm_ref[bb]
        betax = _dot_exact_rhs(sm, exp_beta)
        gx = _dot_exact_rhs(sm, exp_g)
        prep.append((q, k, v, betax, gx))

    n_chunks = T // C
    trow = _iota((C, gw), 0)
    scol = _iota((C, gw), 1) % HEAD_DIM
    incl = scol <= trow
    strict = scol < trow
    eye = jnp.where(scol == trow, 1.0, 0.0)

    def bdiag(x_cat):
        return _stack_heads(x_cat).astype(BF16)

    chains = []
    for ci in range(n_chunks):
        sl = slice(ci * C, (ci + 1) * C)
        for bb in range(nb):
            q, k, v, betax, gx = prep[bb]
            qc, kc, vc, bx, gc = q[sl], k[sl], v[sl], betax[sl], gx[sl]
            grow = gr_ref[bb, ci, 0:1, :]
            eg = jnp.exp(gc)
            kb = kc * bx
            g_last = gc[C - 1:C, :]
            gamma = jnp.where(incl, jnp.exp(jnp.where(incl, gc - grow, 0.0)), 0.0)
            ks_bf = bdiag(kc)
            m = jnp.where(strict, _dot_nt(kb.astype(BF16), ks_bf) * gamma, 0.0)
            chains.append(dict(
                x=eye - m, y=m,
                a_qk=(_dot_nt(qc.astype(BF16), ks_bf) * gamma).astype(BF16),
                rhs_u=bdiag(vc * bx),
                rhs_w=bdiag(kb * eg),
                q_dec=(qc * eg).astype(BF16),
                k_dec=(kc * jnp.exp(g_last - gc)).astype(BF16),
                decay=jnp.exp(g_last)))
    for ch in chains:
        ch["ybd"] = bdiag(ch["y"])
    for _ in range(5):
        for ch in chains:
            ch["y"] = _dot(ch["y"].astype(BF16), ch["ybd"])
            ch["ybd"] = bdiag(ch["y"])
        for ch in chains:
            ch["x"] = ch["x"] + _dot(ch["x"].astype(BF16), ch["ybd"])
    for ch in chains:
        t_inv = ch["x"].astype(BF16)
        ch["u"] = _dot(t_inv, ch["rhs_u"])
        ch["wk"] = _dot(t_inv, ch["rhs_w"]).astype(BF16)

    outs = [[] for _ in range(nb)]
    for ci in range(n_chunks):
        for bb in range(nb):
            ch = chains[ci * nb + bb]
            st = st_ref[bb]
            st_bf = st.astype(BF16)
            v_new = ch["u"] - _dot(ch["wk"], st_bf)
            outs[bb].append(_dot(ch["q_dec"], st_bf) + _dot(ch["a_qk"], bdiag(v_new)))
            upd = _dot_tn(ch["k_dec"], v_new.astype(BF16))
            st_ref[bb] = st * ch["decay"] + jnp.where(same_head, upd, 0.0)
    for bb in range(nb):
        o = jnp.concatenate(outs[bb], axis=0)
        o_ref[bb] = _head_rms(o, ones_bf) * _silu(z_ref[bb])


def gdn(proj3, sm, g_row, conv_w):
    b, s, _ = proj3.shape
    t = SEQ_TILE
    gw = GROUP_WIDTH
    nb = GDN_BATCH
    col = lambda cidx: pl.BlockSpec((nb, t, gw), lambda i, j: (i, j, cidx))
    return pl.pallas_call(
        _gdn_kernel,
        grid=(b // nb, s // t),
        in_specs=[col(COL_DQ), col(COL_DK), col(COL_DV), col(COL_DZ),
                  pl.BlockSpec((nb, t, SMALL_W), lambda i, j: (i, j, 0)),
                  pl.BlockSpec((nb, t // GDN_CHUNK, 8, gw), lambda i, j: (i, j, 0, 0)),
                  pl.BlockSpec((CONV_W, 3 * gw), lambda i, j: (0, 0))],
        out_specs=pl.BlockSpec((nb, t, gw), lambda i, j: (i, j, 0)),
        out_shape=jax.ShapeDtypeStruct((b, s, gw), F32),
        scratch_shapes=[pltpu.VMEM((nb, gw, gw), F32)] + [pltpu.VMEM((nb, 8, gw), F32)] * 3,
        compiler_params=_cparams("parallel", "arbitrary"),
        name="gdn",
    )(proj3, proj3, proj3, proj3, sm, g_row, conv_w)


def _mix_out(x, ys, gain, w_ref):
    gw = GROUP_WIDTH
    acc = x
    for i, y in enumerate(ys):
        yg = (y * gain[:, i * gw:(i + 1) * gw]).astype(BF16)
        acc = acc + _dot(yg, w_ref[i * gw:(i + 1) * gw, :])
    return acc


def _mem_attention(x, g, wq_ref, kv, wo_ref):
    t = x.shape[0]
    gw = GROUP_WIDTH
    ms = jnp.mean(x * x, axis=-1, keepdims=True)
    h = ((x * lax.rsqrt(ms + EPS)) * g).astype(BF16)
    q = _dot(h, wq_ref[...])
    k = kv[:, 0:gw].astype(BF16)
    v = kv[:, gw:2 * gw].astype(BF16)
    s = _dot_nt(k, _stack_heads(q).astype(BF16)) * (HEAD_DIM ** -0.5)
    s = s - jnp.max(s, axis=0, keepdims=True)
    p = jnp.exp(s)
    p = p * (1.0 / jnp.sum(p, axis=0, keepdims=True))
    ot = _dot_tn(v, p.astype(BF16))
    oc = jnp.concatenate([ot[hh * HEAD_DIM:(hh + 1) * HEAD_DIM, hh * t:(hh + 1) * t] for hh in range(HEADS)], axis=0).T
    return x + _dot(oc.astype(BF16), wo_ref[...])


def _post_mix_kernel(x_ref, ya_ref, yb_ref, yc_ref, yd_ref, gain_ref, wout_ref, gm_ref, wq_ref, kv_ref, wo_ref,
                     g_ref, w_ref, b_ref, x2_ref, h_ref, r_ref, n_ref, cnt_ref):
    @pl.when((pl.program_id(0) == 0) & (pl.program_id(1) == 0))
    def _():
        cnt_ref[...] = jnp.zeros_like(cnt_ref)

    x1 = _mix_out(x_ref[0], (ya_ref[0], yb_ref[0], yc_ref[0], yd_ref[0]), gain_ref[...], wout_ref)
    x = _mem_attention(x1, gm_ref[...], wq_ref, kv_ref[0], wo_ref)
    x2_ref[0] = x
    tm = x.shape[0]
    ms = jnp.mean(x * x, axis=-1, keepdims=True)
    h = (x * lax.rsqrt(ms + EPS)) * g_ref[...]
    h_ref[...] = _pack_bf16_pairs(h)
    hh, hm, _ = _split3(h)
    w = w_ref[...]
    wh, wm, _ = _split3(w)
    logits = _dot(hh, wh) + _dot(hh, wm) + _dot(hm, wh) + b_ref[...]
    lane = _iota(logits.shape, 1)
    big = jnp.int32(1 << 30)
    neg = -jnp.inf
    is_grp = (lane >= N_EXPERTS) & (lane < N_EXPERTS + N_EXPERT_GROUPS)
    gl = jnp.where(is_grp, logits, neg)
    gmax = jnp.max(gl, axis=-1, keepdims=True)
    p_grp = 1.0 / jnp.sum(jnp.exp(gl - gmax), axis=-1, keepdims=True)
    g_sel = jnp.min(jnp.where(gl == gmax, lane, big), axis=-1, keepdims=True) - N_EXPERTS
    in_grp = (lane < N_EXPERTS) & (lane // EXPERTS_PER_GROUP == g_sel)
    el = jnp.where(in_grp, logits, neg)
    m1 = jnp.max(el, axis=-1, keepdims=True)
    i1 = jnp.min(jnp.where(el == m1, lane, big), axis=-1, keepdims=True)
    el2 = jnp.where(lane == i1, neg, el)
    m2 = jnp.max(el2, axis=-1, keepdims=True)
    i2 = jnp.min(jnp.where(el2 == m2, lane, big), axis=-1, keepdims=True)
    e21 = jnp.exp(m2 - m1)
    w1 = 1.0 / (1.0 + e21)
    w2 = e21 / (1.0 + e21)
    oh1 = lane == i1
    oh2 = lane == i2
    onehot = jnp.where(oh1 | oh2, 1.0, 0.0)
    strict = jnp.where(_iota((tm, tm), 0) > _iota((tm, tm), 1), 1.0, 0.0).astype(BF16)
    before = _dot(strict, onehot.astype(BF16)) + cnt_ref[...]
    rank1 = jnp.sum(jnp.where(oh1, before, 0.0), axis=-1, keepdims=True)
    rank2 = jnp.sum(jnp.where(oh2, before, 0.0), axis=-1, keepdims=True)
    cnt = before[tm - 1:tm, :] + onehot[tm - 1:tm, :]
    cnt_ref[...] = cnt
    n_ref[...] = jnp.broadcast_to(cnt, n_ref.shape)
    cols = (i1.astype(F32), i2.astype(F32), p_grp * w1, p_grp * w2, rank1, rank2)
    route = jnp.zeros(logits.shape, F32)
    for j, col in enumerate(cols):
        route = jnp.where(lane == j, col, route)
    r_ref[...] = route


def post_mix(x3, ya, yb, yc, yd, gain, w_out_bf16, g_mem, wq_bf16, kv3, wo_bf16, g_ffn, w_router, b_router):
    b, s, d = x3.shape
    t = ROW_TILE
    gw = GROUP_WIDTH
    nt = s // t
    mlen = kv3.shape[1]
    tile = lambda w: pl.BlockSpec((1, t, w), lambda i, j: (i, j, 0))
    const = lambda shape: pl.BlockSpec(shape, lambda i, j: (0,) * len(shape))
    tok = lambda w: pl.BlockSpec((t, w), lambda i, j: (i * nt + j, 0))
    return pl.pallas_call(
        _post_mix_kernel,
        grid=(b, nt),
        in_specs=[tile(d), tile(gw), tile(gw), tile(gw), tile(gw),
                  const((1, 4 * gw)), const((4 * gw, d)),
                  const((1, d)), const((d, gw)), pl.BlockSpec((1, mlen, 2 * gw), lambda i, j: (i, 0, 0)), const((gw, d)),
                  const((1, d)), const((d, SMALL_W)), const((1, SMALL_W))],
        out_specs=[tile(d), tok(d // 2), tok(SMALL_W), const((8, SMALL_W))],
        out_shape=[jax.ShapeDtypeStruct((b, s, d), F32), jax.ShapeDtypeStruct((b * s, d // 2), jnp.uint32),
                   jax.ShapeDtypeStruct((b * s, SMALL_W), F32), jax.ShapeDtypeStruct((8, SMALL_W), F32)],
        scratch_shapes=[pltpu.VMEM((1, SMALL_W), F32)],
        compiler_params=_cparams("arbitrary", "arbitrary"),
        name="post_mix",
    )(x3, ya, yb, yc, yd, gain.reshape(1, 4 * gw), w_out_bf16, g_mem.reshape(1, d), wq_bf16, kv3, wo_bf16,
      g_ffn.reshape(1, d), w_router, b_router)


def _experts_kernel(te_ref, nu_ref, xs_ref, wgu_ref, wdn_ref, y_ref, wgu_bf, wdn_bf):
    i = pl.program_id(0)

    @pl.when(i >= nu_ref[0])
    def _():
        y_ref[...] = jnp.zeros_like(y_ref)

    @pl.when(i < nu_ref[0])
    def _():
        prev = te_ref[jnp.maximum(i - 1, 0)]

        @pl.when((i == 0) | (te_ref[i] != prev))
        def _():
            wgu_bf[...] = wgu_ref[0, 0].astype(BF16)
            wdn_bf[...] = wdn_ref[0, 0].astype(BF16)

        gu = _dot(_unpack_bf16_pairs(xs_ref[...]).astype(BF16), wgu_bf[...])
        act = _silu(gu[:, 0:D_EXPERT]) * gu[:, D_EXPERT:2 * D_EXPERT]
        y_ref[...] = _pack_bf16_pairs(_dot(act.astype(BF16), wdn_bf[...]))


def moe_experts(xs, tile_expert, n_used, w_gu, w_dn, layer, tm):
    n_rows, dp = xs.shape
    d = 2 * dp
    tile = lambda i, te, nu: (jnp.minimum(i, nu[0] - 1), 0)
    wsel = lambda i, te, nu: (layer, te[jnp.minimum(i, nu[0] - 1)], 0, 0)
    return pl.pallas_call(
        _experts_kernel,
        grid_spec=pltpu.PrefetchScalarGridSpec(
            num_scalar_prefetch=2,
            grid=(n_rows // tm,),
            in_specs=[pl.BlockSpec((tm, dp), tile),
                      pl.BlockSpec((1, 1, d, 2 * D_EXPERT), wsel),
                      pl.BlockSpec((1, 1, D_EXPERT, d), wsel)],
            out_specs=pl.BlockSpec((tm, dp), lambda i, te, nu: (i, 0)),
            scratch_shapes=[pltpu.VMEM((d, 2 * D_EXPERT), BF16), pltpu.VMEM((D_EXPERT, d), BF16)]),
        out_shape=jax.ShapeDtypeStruct((n_rows, dp), jnp.uint32),
        compiler_params=_cparams("arbitrary"),
        name="moe_experts",
    )(tile_expert, n_used, xs, w_gu, w_dn)


def sc_gather_rows(table, idx):
    n_idx = idx.shape[0]
    width = table.shape[1]
    info = plsc.get_sparse_core_info()
    n_workers = info.num_cores * info.num_subcores
    per_worker = n_idx // n_workers
    chunk = SC_GATHER_CHUNK
    assert n_idx % (n_workers * chunk) == 0 and width % info.num_lanes == 0
    mesh = plsc.VectorSubcoreMesh(core_axis_name="c", subcore_axis_name="s")

    def body(table_hbm, idx_hbm, out_hbm, idx_v, rows_v, sem):
        wid = lax.axis_index("s") * info.num_cores + lax.axis_index("c")
        base = wid * per_worker

        @pl.loop(0, per_worker // chunk)
        def _(j):
            off = pl.multiple_of(base + j * chunk, chunk)
            pltpu.sync_copy(idx_hbm.at[pl.ds(off, chunk)], idx_v)
            pltpu.async_copy(table_hbm.at[idx_v], rows_v, sem).wait()
            pltpu.sync_copy(rows_v, out_hbm.at[pl.ds(off, chunk)])

    return pl.kernel(
        body,
        out_type=jax.ShapeDtypeStruct((n_idx, width), table.dtype),
        mesh=mesh,
        scratch_types=[pltpu.VMEM((chunk,), jnp.int32), pltpu.VMEM((chunk, width), table.dtype),
                       pltpu.SemaphoreType.DMA],
        name="sc_gather_rows",
    )(table, idx)


def sc_dispatch_rows(rows, dest, pad_idx, zero_rows):
    m, width = rows.shape
    n_pad = pad_idx.shape[0]
    info = plsc.get_sparse_core_info()
    n_workers = info.num_cores * info.num_subcores
    chunk = SC_GATHER_CHUNK
    assert m % (n_workers * chunk) == 0 and n_pad % (n_workers * chunk) == 0 and zero_rows.shape == (chunk, width)
    rows_per_worker = m // n_workers
    pad_per_worker = n_pad // n_workers
    mesh = plsc.VectorSubcoreMesh(core_axis_name="c", subcore_axis_name="s")

    def body(rows_hbm, dest_hbm, pad_hbm, zero_hbm, out_hbm, idx_v, rows_v, sem):
        wid = lax.axis_index("s") * info.num_cores + lax.axis_index("c")

        @pl.loop(0, rows_per_worker // chunk)
        def _(j):
            off = pl.multiple_of(wid * rows_per_worker + j * chunk, chunk)
            pltpu.sync_copy(rows_hbm.at[pl.ds(off, chunk)], rows_v)
            for slot in range(2):
                pltpu.sync_copy(dest_hbm.at[pl.ds(slot * m + off, chunk)], idx_v)
                pltpu.async_copy(rows_v, out_hbm.at[idx_v], sem).wait()

        pltpu.sync_copy(zero_hbm, rows_v)

        @pl.loop(0, pad_per_worker // chunk)
        def _(j):
            off = pl.multiple_of(wid * pad_per_worker + j * chunk, chunk)
            pltpu.sync_copy(pad_hbm.at[pl.ds(off, chunk)], idx_v)
            pltpu.async_copy(rows_v, out_hbm.at[idx_v], sem).wait()

    return pl.kernel(
        body,
        out_type=jax.ShapeDtypeStruct((2 * m + n_pad, width), rows.dtype),
        mesh=mesh,
        scratch_types=[pltpu.VMEM((chunk,), jnp.int32), pltpu.VMEM((chunk, width), rows.dtype),
                       pltpu.SemaphoreType.DMA],
        name="sc_dispatch_rows",
    )(rows, dest, pad_idx, zero_rows)


def _combine_rows_kernel(x_ref, r_ref, g_ref, y1_ref, y2_ref, o_ref, *, final_norm):
    r = r_ref[...]
    out = x_ref[...] + r[:, 2:3] * _unpack_bf16_pairs(y1_ref[...]) + r[:, 3:4] * _unpack_bf16_pairs(y2_ref[...])
    if final_norm:
        ms = jnp.mean(out * out, axis=-1, keepdims=True)
        out = (out * lax.rsqrt(ms + EPS)) * g_ref[...]
    o_ref[...] = out


def moe_combine_rows(x2, route, y_rows, g_final, tm, final_norm):
    m, d = x2.shape
    steps = m // tm
    return pl.pallas_call(
        functools.partial(_combine_rows_kernel, final_norm=final_norm),
        grid=(steps,),
        in_specs=[pl.BlockSpec((tm, d), lambda i: (i, 0)),
                  pl.BlockSpec((tm, SMALL_W), lambda i: (i, 0)),
                  pl.BlockSpec((1, d), lambda i: (0, 0)),
                  pl.BlockSpec((tm, d // 2), lambda i: (i, 0)),
                  pl.BlockSpec((tm, d // 2), lambda i: (i + steps, 0))],
        out_specs=pl.BlockSpec((tm, d), lambda i: (i, 0)),
        out_shape=jax.ShapeDtypeStruct((m, d), F32),
        compiler_params=_cparams("parallel"),
        name="moe_combine_rows",
    )(x2, route, g_final.reshape(1, d), y_rows, y_rows)


def moe_routed(x2, h, route, counts, w_gu, w_dn, layer, g_final, final_norm):
    m, d = x2.shape
    te_rows = EXPERT_TILE
    n_rows = 2 * m + N_EXPERTS * te_rows
    cnt = counts[0, 0:N_EXPERTS].astype(jnp.int32)
    padded = (cnt + te_rows - 1) // te_rows * te_rows
    seg_end = jnp.cumsum(padded)
    seg_start = seg_end - padded
    idx = route[:, 0:2].astype(jnp.int32)
    onehot = idx[:, :, None] == jnp.arange(N_EXPERTS, dtype=jnp.int32)
    dest = (jnp.sum(jnp.where(onehot, seg_start, 0), axis=-1) + route[:, 4:6].astype(jnp.int32)).T.reshape(2 * m)
    tile_start = jnp.arange(n_rows // te_rows, dtype=jnp.int32) * te_rows
    tile_expert = jnp.minimum(jnp.sum((tile_start[:, None] >= seg_end[None, :]).astype(jnp.int32), axis=1), N_EXPERTS - 1)
    n_used = (seg_end[N_EXPERTS - 1] // te_rows).reshape(1)
    n_pad = n_rows - 2 * m
    experts = jnp.arange(N_EXPERTS, dtype=jnp.int32)
    pad_end = jnp.cumsum(padded - cnt)
    k = jnp.arange(n_pad, dtype=jnp.int32)
    k_expert = jnp.sum((k[:, None] >= pad_end[None, :]).astype(jnp.int32), axis=1)
    first_pad = seg_start + cnt - (pad_end - (padded - cnt))
    in_segment = jnp.sum(jnp.where(k_expert[:, None] == experts[None, :], first_pad[None, :], 0), axis=1) + k
    pad_idx = jnp.where(k_expert < N_EXPERTS, in_segment, seg_end[N_EXPERTS - 1] + k - pad_end[N_EXPERTS - 1])
    xs = sc_dispatch_rows(h, dest, pad_idx, jnp.zeros((SC_GATHER_CHUNK, d // 2), h.dtype))
    ys = moe_experts(xs, tile_expert, n_used, w_gu, w_dn, layer, tm=te_rows)
    y_rows = sc_gather_rows(ys, dest)
    return moe_combine_rows(x2, route, y_rows, g_final, tm=ROW_TILE, final_norm=final_norm)


def _w_in_prep_kernel(w_ref, o_ref):
    gw = GROUP_WIDTH
    a_end = 7 * gw
    c_start = a_end + HEADS
    c_end = c_start + 5 * gw
    z_start = c_end + 2 * HEADS
    assert a_end % SMALL_W == LANE_FOX and c_end % SMALL_W == LANE_BETA and LANE_GDEC == LANE_BETA + HEADS
    w = w_ref[...]
    o_ref[:, 0:a_end] = w[:, 0:a_end].astype(BF16)
    o_ref[:, a_end:a_end + 5 * gw] = w[:, c_start:c_end].astype(BF16)
    o_ref[:, 12 * gw:13 * gw] = w[:, z_start:z_start + gw].astype(BF16)
    blk_fox = w[:, a_end:a_end + SMALL_W]
    blk_gdn = w[:, c_end - LANE_BETA:c_end - LANE_BETA + SMALL_W]
    lane = _iota(blk_fox.shape, 1)
    small = jnp.where(lane < LANE_BETA, blk_fox, jnp.where(lane < LANE_GDEC + HEADS, blk_gdn, 0.0))
    o_ref[:, 13 * gw:13 * gw + SMALL_W] = small.astype(BF16)
    o_ref[:, 13 * gw + SMALL_W:] = jnp.zeros((w.shape[0], SMALL_W), BF16)


def w_in_prep(w_all, layer):
    _, d, n = w_all.shape
    tr = 128
    return pl.pallas_call(
        _w_in_prep_kernel,
        grid=(d // tr,),
        in_specs=[pl.BlockSpec((None, tr, n), lambda i: (layer, i, 0))],
        out_specs=pl.BlockSpec((tr, N_PROJ), lambda i: (i, 0)),
        out_shape=jax.ShapeDtypeStruct((d, N_PROJ), BF16),
        compiler_params=_cparams("parallel"),
        name="w_in_prep",
    )(w_all)


def _block_diag(w):
    h = w.shape[0]
    eye = jnp.eye(h, dtype=w.dtype)
    return (eye[:, None, :, None] * w[:, :, None, :]).reshape(h * HEAD_DIM, h * HEAD_DIM)


def kernel(x, mem, norm_mix, w_in, hgrn_lb, fox_fb, lru_conv_w, lru_conv_b, lru_wa, lru_ba, lru_wx, lru_bx, lru_lam, gdn_conv_w, gdn_a_log, gdn_dt_bias, mix_gain, w_out, norm_mem, norm_memkv, w_mq, w_mkv, w_mo, norm_ffn, w_rg, b_rg, w_re, b_re, w_e_gu, w_e_dn, norm_final):
    b, s, d = x.shape
    depth = w_in.shape[0]
    t_tok = b * s
    mlen = mem.shape[1]
    gw = GROUP_WIDTH

    lb_all = jnp.cumsum(jax.nn.softmax(hgrn_lb.astype(F32), axis=0), axis=0)
    lb_all = lb_all - lb_all[0]

    x = x.astype(F32)
    for l in range(depth):
        proj = norm_matmul(x.reshape(t_tok, d), norm_mix[l], w_in_prep(w_in, l), tm=ROW_TILE)
        proj3 = proj.reshape(b, s, N_PROJ)
        prm = jnp.zeros((8, SMALL_W), F32)
        prm = prm.at[0, LANE_FOX:LANE_FOX + HEADS].set(fox_fb[l].astype(F32))
        prm = prm.at[0, LANE_GDEC:LANE_GDEC + HEADS].set(gdn_dt_bias[l].astype(F32))
        prm = prm.at[1, LANE_GDEC:LANE_GDEC + HEADS].set(gdn_a_log[l].astype(F32))
        sm, c_exp = small_prep(proj3, prm)
        g_row = sm[:, :, LANE_GDEC:LANE_GDEC + HEADS].reshape(b, s // GDN_CHUNK, GDN_CHUNK, HEADS)
        g_row = jnp.swapaxes(g_row, 2, 3).reshape(b, s // GDN_CHUNK, 1, gw)
        g_row = jnp.broadcast_to(g_row, (b, s // GDN_CHUNK, 8, gw))

        ya = hgrn2(proj3, lb_all[l])
        yb = fox_attention(proj3, c_exp)
        yc = rglru(proj3, lru_conv_w[l], lru_conv_b[l], _block_diag(lru_wa[l]).astype(BF16), lru_ba[l],
                   _block_diag(lru_wx[l]).astype(BF16), lru_bx[l], lru_lam[l])
        yd = gdn(proj3, sm, g_row, gdn_conv_w[l])

        kv = norm_matmul(mem.reshape(b * mlen, d), norm_memkv[l], w_mkv[l].astype(BF16), tm=256)
        w_router = jnp.concatenate([w_re[l], w_rg[l], jnp.zeros((d, SMALL_W - N_EXPERTS - N_EXPERT_GROUPS), F32)], axis=1)
        b_router = jnp.concatenate([b_re[l], b_rg[l], jnp.zeros((SMALL_W - N_EXPERTS - N_EXPERT_GROUPS,), F32)]).reshape(1, SMALL_W)
        x, h, route, counts = post_mix(x, ya, yb, yc, yd, mix_gain[l], w_out[l].astype(BF16), norm_mem[l],
                                       w_mq[l].astype(BF16), kv.reshape(b, mlen, 2 * gw), w_mo[l].astype(BF16),
                                       norm_ffn[l], w_router, b_router)

        x = moe_routed(x.reshape(t_tok, d), h, route, counts, w_e_gu, w_e_dn, l,
                       norm_final, final_norm=(l == depth - 1)).reshape(b, s, d)

    return x
```

```python
import functools
import math

import jax
import jax.numpy as jnp
from jax import lax
from jax.experimental import pallas as pl
from jax.experimental.pallas import tpu as pltpu
from jax.experimental.pallas import tpu_sc as plsc

F32 = jnp.float32
BF16 = jnp.bfloat16

HEAD_DIM = 64
GROUP_WIDTH = 256
HEADS = GROUP_WIDTH // HEAD_DIM
GDN_CHUNK = 64
GDN_BATCH = 4
FOX_BATCH = 2
HGRN_BATCH = 4
HGRN_CHUNK = 16
CONV_W = 4
LRU_C = 8.0
EPS = 1e-6
N_EXPERT_GROUPS = 4
EXPERTS_PER_GROUP = 8
N_EXPERTS = N_EXPERT_GROUPS * EXPERTS_PER_GROUP
D_EXPERT = 256
SC_GATHER_CHUNK = 128
EXPERT_TILE = 512
SMALL_W = 128
SEQ_TILE = 256
ROW_TILE = 512
VMEM_LIMIT = 56 * 1024 * 1024

(COL_AQ, COL_AF, COL_AI, COL_AG, COL_BQ, COL_BK, COL_BV, COL_CX, COL_CG,
 COL_DQ, COL_DK, COL_DV, COL_DZ) = range(13)
N_PROJ = 13 * GROUP_WIDTH + 2 * SMALL_W
COL_SMALL = 13 * GROUP_WIDTH // SMALL_W
LANE_FOX = 0
LANE_BETA = 4
LANE_GDEC = 8


def _cparams(*sem):
    return pltpu.CompilerParams(dimension_semantics=sem, vmem_limit_bytes=VMEM_LIMIT)


def _dot(a, b):
    return jnp.dot(a, b, preferred_element_type=F32)


def _dot_nt(a, b):
    return lax.dot_general(a, b, (((1,), (1,)), ((), ())), preferred_element_type=F32)


def _dot_tn(a, b):
    return lax.dot_general(a, b, (((0,), (0,)), ((), ())), preferred_element_type=F32)


def _split3(x):
    h = x.astype(BF16)
    r = x - h.astype(F32)
    m = r.astype(BF16)
    l = (r - m.astype(F32)).astype(BF16)
    return h, m, l


def _dot_exact_rhs(x, w_bf16):
    h, m, l = _split3(x)
    return _dot(h, w_bf16) + _dot(m, w_bf16) + _dot(l, w_bf16)


def _dot_exact_lhs(w_bf16, x):
    h, m, l = _split3(x)
    return _dot(w_bf16, h) + _dot(w_bf16, m) + _dot(w_bf16, l)


def _iota(shape, dim):
    return lax.broadcasted_iota(jnp.int32, shape, dim)


def _head_ones(n=GROUP_WIDTH):
    r = _iota((n, n), 0) // HEAD_DIM
    c = _iota((n, n), 1) // HEAD_DIM
    return r == c


def _sigmoid(x):
    return 1.0 / (1.0 + jnp.exp(-x))


def _silu(x):
    return x * _sigmoid(x)


def _log_sigmoid(x):
    return jnp.minimum(x, 0.0) - jnp.log1p(jnp.exp(-jnp.abs(x)))


def _softplus(x):
    return jnp.maximum(x, 0.0) + jnp.log1p(jnp.exp(-jnp.abs(x)))


def _gelu_tanh(x):
    return 0.5 * x * (1.0 + jnp.tanh(math.sqrt(2.0 / math.pi) * (x + 0.044715 * (x * x * x))))


def _pack_bf16_pairs(x):
    n = x.shape[1] // 2
    u = lax.bitcast_convert_type(x.astype(BF16).astype(F32), jnp.uint32)
    return u[:, :n] | (u[:, n:] >> 16)


def _unpack_bf16_pairs(p):
    hi = lax.bitcast_convert_type(p & jnp.uint32(0xFFFF0000), F32)
    lo = lax.bitcast_convert_type(p << 16, F32)
    return jnp.concatenate([hi, lo], axis=1)


def _head_mean_sq(x, ones_bf16):
    return _dot_exact_rhs(x * x, ones_bf16) * (1.0 / HEAD_DIM)


def _head_rms(x, ones_bf16):
    return x * lax.rsqrt(_head_mean_sq(x, ones_bf16) + EPS)


def _stack_heads(x):
    lane_head = _iota(x.shape, 1) // HEAD_DIM
    parts = []
    for h in range(HEADS):
        parts.append(jnp.where(lane_head == h, x, 0.0))
    return jnp.concatenate(parts, axis=0)


def _causal_conv4(x, prev8, w):
    r = x.shape[0]
    row8 = _iota((8, x.shape[1]), 0)
    acc = x * w[CONV_W - 1:CONV_W, :]
    for k in range(1, CONV_W):
        xs = pltpu.roll(x, k, 0)
        ps = pltpu.roll(prev8, k, 0)
        top = jnp.where(row8 < k, ps, xs[0:8])
        xs = jnp.concatenate([top, xs[8:r]], axis=0)
        acc = acc + xs * w[CONV_W - 1 - k:CONV_W - k, :]
    return acc


def _norm_matmul_kernel(x_ref, g_ref, w_ref, o_ref):
    x = x_ref[...]
    ms = jnp.mean(x * x, axis=-1, keepdims=True)
    h = (x * lax.rsqrt(ms + EPS)) * g_ref[...]
    o_ref[...] = _dot(h.astype(BF16), w_ref[...]).astype(o_ref.dtype)


def norm_matmul(x, g, w_bf16, tm, out_dtype=F32):
    m, d = x.shape
    n = w_bf16.shape[1]
    return pl.pallas_call(
        _norm_matmul_kernel,
        grid=(m // tm,),
        in_specs=[pl.BlockSpec((tm, d), lambda i: (i, 0)),
                  pl.BlockSpec((1, d), lambda i: (0, 0)),
                  pl.BlockSpec((d, n), lambda i: (0, 0))],
        out_specs=pl.BlockSpec((tm, n), lambda i: (i, 0)),
        out_shape=jax.ShapeDtypeStruct((m, n), out_dtype),
        compiler_params=_cparams("parallel"),
        name="norm_matmul",
    )(x, g.reshape(1, d), w_bf16)


def _small_prep_kernel(s_ref, p_ref, o_ref, x_ref):
    S = s_ref.shape[1]
    W = s_ref.shape[2]
    blk = SEQ_TILE
    lane = _iota((blk, W), 1)
    is_fox = (lane >= LANE_FOX) & (lane < LANE_FOX + HEADS)
    is_beta = (lane >= LANE_BETA) & (lane < LANE_BETA + HEADS)
    is_gdec = (lane >= LANE_GDEC) & (lane < LANE_GDEC + HEADS)
    r = _iota((blk, blk), 0)
    c = _iota((blk, blk), 1)
    tril_all = jnp.where(r >= c, 1.0, 0.0).astype(BF16)
    tril_chunk = jnp.where((r >= c) & (r // GDN_CHUNK == c // GDN_CHUNK), 1.0, 0.0).astype(BF16)
    neg_exp_a = -jnp.exp(p_ref[1:2, :])
    carry = jnp.zeros((1, W), F32)
    for i in range(S // blk):
        sl = pl.ds(i * blk, blk)
        z = s_ref[0, sl, :] + p_ref[0:1, :]
        fox = jnp.where(is_fox, _log_sigmoid(z), 0.0)
        beta = jnp.where(is_beta, _sigmoid(z), 0.0)
        gdec = jnp.where(is_gdec, neg_exp_a * _softplus(z), 0.0)
        cf = _dot_exact_lhs(tril_all, fox) + carry
        cg = _dot_exact_lhs(tril_chunk, gdec)
        carry = cf[blk - 1:blk, :]
        o_ref[0, sl, :] = cf + cg + beta
        both = cf + cg
        for j, src_lane in enumerate(range(LANE_FOX, LANE_FOX + HEADS)):
            x_ref[0, sl, j * W:(j + 1) * W] = jnp.broadcast_to(both[:, src_lane:src_lane + 1], (blk, W))


def small_prep(proj3, params):
    b, s, _ = proj3.shape
    w = SMALL_W
    return pl.pallas_call(
        _small_prep_kernel,
        grid=(b,),
        in_specs=[pl.BlockSpec((1, s, w), lambda i: (i, 0, COL_SMALL)),
                  pl.BlockSpec((8, w), lambda i: (0, 0))],
        out_specs=[pl.BlockSpec((1, s, w), lambda i: (i, 0, 0)),
                   pl.BlockSpec((1, s, HEADS * w), lambda i: (i, 0, 0))],
        out_shape=[jax.ShapeDtypeStruct((b, s, w), F32), jax.ShapeDtypeStruct((b, s, HEADS * w), F32)],
        compiler_params=_cparams("parallel"),
        name="small_prep",
    )(proj3, params)


def _hgrn2_kernel(q_ref, f_ref, i_ref, g_ref, lb_ref, o_ref, st_ref, qs, ks, vs, bs, os_):
    nb = q_ref.shape[0]
    T = q_ref.shape[1]
    C = HGRN_CHUNK
    seqs = range(nb)

    @pl.when(pl.program_id(1) == 0)
    def _():
        st_ref[...] = jnp.zeros_like(st_ref)

    same_head = _head_ones()
    ones_bf = jnp.where(same_head, 1.0, 0.0).astype(BF16)
    lb = lb_ref[...]
    log_lb = jnp.log(lb)
    r = _iota((T, T), 0)
    c = _iota((T, T), 1)
    tril_chunk = jnp.where((r >= c) & (r // C == c // C), 1.0, 0.0).astype(BF16)
    for bb in seqs:
        fl = f_ref[bb]
        c2 = jnp.log1p(-lb) + _log_sigmoid(fl)
        mx = jnp.maximum(log_lb, c2)
        log_f = mx + jnp.log1p(jnp.exp(-jnp.abs(log_lb - c2)))
        bs[bb] = _dot_exact_lhs(tril_chunk, log_f)
        qs[bb] = _silu(q_ref[bb])
        ks[bb] = (1.0 - lb) * _sigmoid(-fl)
        vs[bb] = i_ref[bb]

    H8 = C // 2
    trow8 = _iota((H8, GROUP_WIDTH), 0)

    def chunk(ci, carry):
        r0 = pl.multiple_of(ci * C, C)
        qc = [qs[bb, pl.ds(r0, C), :] for bb in seqs]
        kc = [ks[bb, pl.ds(r0, C), :] for bb in seqs]
        vc = [vs[bb, pl.ds(r0, C), :] for bb in seqs]
        bc = [bs[bb, pl.ds(r0, C), :] for bb in seqs]
        st = [st_ref[bb] for bb in seqs]
        a = []
        for bb in seqs:
            parts = []
            for half, s_range in ((0, range(H8)), (1, range(C))):
                b_half = bc[bb][half * H8:(half + 1) * H8, :]
                q_half = qc[bb][half * H8:(half + 1) * H8, :]
                for s in s_range:
                    blk = jnp.exp(b_half - bc[bb][s:s + 1, :]) * (q_half * kc[bb][s:s + 1, :])
                    if s >= half * H8:
                        blk = jnp.where(trow8 >= s - half * H8, blk, 0.0)
                    parts.append(blk)
            a.append(jnp.concatenate(parts, axis=0).astype(BF16))
        sc = [_dot(a[bb], ones_bf) for bb in seqs]
        o = [_dot_nt((qc[bb] * jnp.exp(bc[bb])).astype(BF16), st[bb].astype(BF16)) for bb in seqs]
        b_last = [bc[bb][C - 1:C, :] for bb in seqs]
        upd = [_dot_tn(vc[bb].astype(BF16), (kc[bb] * jnp.exp(b_last[bb] - bc[bb])).astype(BF16)) for bb in seqs]
        for bb in seqs:
            top = o[bb][0:H8, :]
            bot = o[bb][H8:C, :]
            for s in range(H8):
                top = top + sc[bb][s * H8:(s + 1) * H8, :] * vc[bb][s:s + 1, :]
            for s in range(C):
                bot = bot + sc[bb][(H8 + s) * H8:(H8 + s + 1) * H8, :] * vc[bb][s:s + 1, :]
            os_[bb, pl.ds(r0, C), :] = jnp.concatenate([top, bot], axis=0)
            st_ref[bb] = st[bb] * jnp.exp(b_last[bb]) + jnp.where(same_head, upd[bb], 0.0)
        return carry

    lax.fori_loop(0, T // C, chunk, 0)
    for bb in seqs:
        o_ref[bb] = _head_rms(os_[bb], ones_bf) * _silu(g_ref[bb])


def hgrn2(proj3, lb):
    b, s, _ = proj3.shape
    t = SEQ_TILE
    gw = GROUP_WIDTH
    nb = HGRN_BATCH
    col = lambda cidx: pl.BlockSpec((nb, t, gw), lambda i, j: (i, j, cidx))
    return pl.pallas_call(
        _hgrn2_kernel,
        grid=(b // nb, s // t),
        in_specs=[col(COL_AQ), col(COL_AF), col(COL_AI), col(COL_AG),
                  pl.BlockSpec((1, gw), lambda i, j: (0, 0))],
        out_specs=pl.BlockSpec((nb, t, gw), lambda i, j: (i, j, 0)),
        out_shape=jax.ShapeDtypeStruct((b, s, gw), F32),
        scratch_shapes=[pltpu.VMEM((nb, gw, gw), F32)] + [pltpu.VMEM((nb, t, gw), F32)] * 5,
        compiler_params=_cparams("parallel", "arbitrary"),
        name="hgrn2",
    )(proj3, proj3, proj3, proj3, lb.reshape(1, gw))


def _fox_kernel(q_ref, k_ref, v_ref, cx_ref, o_ref, kb_ref, vt_ref, acc_ref):
    nb = q_ref.shape[0]
    tq = q_ref.shape[1]
    tk = tq
    S = k_ref.shape[1]
    qi = pl.program_id(1)
    seqs = range(nb)

    @pl.when(qi == 0)
    def _():
        for bb in seqs:
            for i in range(S // tk):
                sl = pl.ds(i * tk, tk)
                kb_ref[bb, sl, :] = k_ref[bb, sl, :].astype(BF16)
                vt_ref[bb, :, sl] = v_ref[bb, sl, :].T.astype(BF16)

    lane_head = _iota((tq, GROUP_WIDTH), 1) // HEAD_DIM
    qh = []
    for bb in seqs:
        qsc = q_ref[bb] * (HEAD_DIM ** -0.5)
        qh.append([jnp.where(lane_head == h, qsc, 0.0).astype(BF16) for h in range(HEADS)])
    acc_ref[...] = jnp.zeros_like(acc_ref)
    reps = tq // SMALL_W
    causal = _iota((tk, tq), 0) <= _iota((tk, tq), 1)

    def step(kb, stats, masked):
        k0 = pl.multiple_of(kb * tk, tk)
        new_stats = [[None] * HEADS for _ in seqs]
        strips = [(h, bb) for h in range(HEADS) for bb in seqs]

        def scores(h, bb):
            return _dot_nt(kb_ref[bb, pl.ds(k0, tk), :], qh[bb][h])

        ahead = 3
        queue = [scores(*strips[j]) for j in range(ahead)]
        for i, (h, bb) in enumerate(strips):
            st = queue.pop(0)
            if i + ahead < len(strips):
                queue.append(scores(*strips[i + ahead]))
            rows = slice(h * HEAD_DIM, (h + 1) * HEAD_DIM)
            cx = cx_ref[bb, pl.ds(k0, tk), h * SMALL_W:(h + 1) * SMALL_W]
            st = st - jnp.concatenate([cx] * reps, axis=1)
            if masked:
                st = jnp.where(causal, st, -jnp.inf)
            m_old, l_old = stats[bb][h]
            m_new = jnp.maximum(m_old, jnp.max(st, axis=0, keepdims=True))
            alpha = jnp.exp(m_old - m_new)
            p = jnp.exp(st - m_new)
            l_new = alpha * l_old + jnp.sum(p, axis=0, keepdims=True)
            pv = _dot(vt_ref[bb, rows, pl.ds(k0, tk)], p.astype(BF16))
            acc_ref[bb, rows, :] = alpha * acc_ref[bb, rows, :] + pv
            new_stats[bb][h] = (m_new, l_new)
        return tuple(tuple(new_stats[bb]) for bb in seqs)

    init = tuple(tuple((jnp.full((1, tq), -jnp.inf, F32), jnp.zeros((1, tq), F32)) for _ in range(HEADS)) for _ in seqs)
    stats = lax.fori_loop(0, qi, lambda kb, c: step(kb, c, False), init)
    stats = step(qi, stats, True)

    ones_bf = jnp.where(_head_ones(), 1.0, 0.0).astype(BF16)
    for bb in seqs:
        inv_l = jnp.concatenate([jnp.broadcast_to(1.0 / stats[bb][h][1], (HEAD_DIM, tq)) for h in range(HEADS)], axis=0)
        o_ref[bb] = _head_rms((acc_ref[bb] * inv_l).T, ones_bf)


def fox_attention(proj3, c_exp):
    b, s, _ = proj3.shape
    tq = SEQ_TILE
    gw = GROUP_WIDTH
    nb = FOX_BATCH
    return pl.pallas_call(
        _fox_kernel,
        grid=(b // nb, s // tq),
        in_specs=[pl.BlockSpec((nb, tq, gw), lambda i, j: (i, j, COL_BQ)),
                  pl.BlockSpec((nb, s, gw), lambda i, j: (i, 0, COL_BK)),
                  pl.BlockSpec((nb, s, gw), lambda i, j: (i, 0, COL_BV)),
                  pl.BlockSpec((nb, s, HEADS * SMALL_W), lambda i, j: (i, 0, 0))],
        out_specs=pl.BlockSpec((nb, tq, gw), lambda i, j: (i, j, 0)),
        out_shape=jax.ShapeDtypeStruct((b, s, gw), F32),
        scratch_shapes=[pltpu.VMEM((nb, s, gw), BF16),
                        pltpu.VMEM((nb, gw, s), BF16),
                        pltpu.VMEM((nb, gw, tq), F32)],
        compiler_params=_cparams("parallel", "arbitrary"),
        name="fox_attention",
    )(proj3, proj3, proj3, c_exp)


def _rglru_kernel(x_ref, g_ref, cw_ref, cb_ref, wa_ref, ba_ref, wx_ref, bx_ref, lam_ref, o_ref, prev_ref, h_ref):
    @pl.when(pl.program_id(1) == 0)
    def _():
        prev_ref[...] = jnp.zeros_like(prev_ref)
        h_ref[...] = jnp.zeros_like(h_ref)

    x = x_ref[0]
    t = x.shape[0]
    xc = _causal_conv4(x, prev_ref[...], cw_ref[...]) + cb_ref[...]
    prev_ref[...] = x[t - 8:t, :]
    xb = xc.astype(BF16)
    r = _sigmoid(_dot(xb, wa_ref[...]) + ba_ref[...])
    ig = _sigmoid(_dot(xb, wx_ref[...]) + bx_ref[...])
    log_a = (-LRU_C * r) * _softplus(-lam_ref[...])
    a = jnp.exp(log_a)
    u = jnp.sqrt(1.0 - jnp.exp(2.0 * log_a)) * (ig * xc)
    row = _iota(a.shape, 0)
    d = 1
    while d < t:
        valid = row >= d
        u = jnp.where(valid, a * pltpu.roll(u, d, 0) + u, u)
        a = jnp.where(valid, a * pltpu.roll(a, d, 0), a)
        d *= 2
    h = a * h_ref[...] + u
    h_ref[...] = h[t - 1:t, :]
    ones_bf = jnp.where(_head_ones(), 1.0, 0.0).astype(BF16)
    o_ref[0] = _head_rms(h * _gelu_tanh(g_ref[0]), ones_bf)


def rglru(proj3, conv_w, conv_b, wa_bd, ba, wx_bd, bx, lam):
    b, s, _ = proj3.shape
    t = ROW_TILE
    gw = GROUP_WIDTH
    row = lambda: pl.BlockSpec((1, gw), lambda i, j: (0, 0))
    return pl.pallas_call(
        _rglru_kernel,
        grid=(b, s // t),
        in_specs=[pl.BlockSpec((1, t, gw), lambda i, j: (i, j, COL_CX)),
                  pl.BlockSpec((1, t, gw), lambda i, j: (i, j, COL_CG)),
                  pl.BlockSpec((CONV_W, gw), lambda i, j: (0, 0)), row(),
                  pl.BlockSpec((gw, gw), lambda i, j: (0, 0)), row(),
                  pl.BlockSpec((gw, gw), lambda i, j: (0, 0)), row(), row()],
        out_specs=pl.BlockSpec((1, t, gw), lambda i, j: (i, j, 0)),
        out_shape=jax.ShapeDtypeStruct((b, s, gw), F32),
        scratch_shapes=[pltpu.VMEM((8, gw), F32), pltpu.VMEM((1, gw), F32)],
        compiler_params=_cparams("parallel", "arbitrary"),
        name="rglru",
    )(proj3, proj3, conv_w, conv_b.reshape(1, gw), wa_bd, ba.reshape(1, gw), wx_bd, bx.reshape(1, gw), lam.reshape(1, gw))


def _gdn_kernel(q_ref, k_ref, v_ref, z_ref, sm_ref, gr_ref, cw_ref, o_ref, st_ref, pq_ref, pk_ref, pv_ref):
    nb = q_ref.shape[0]
    T = q_ref.shape[1]
    C = GDN_CHUNK

    @pl.when(pl.program_id(1) == 0)
    def _():
        st_ref[...] = jnp.zeros_like(st_ref)
        pq_ref[...] = jnp.zeros_like(pq_ref)
        pk_ref[...] = jnp.zeros_like(pk_ref)
        pv_ref[...] = jnp.zeros_like(pv_ref)

    same_head = _head_ones()
    ones_bf = jnp.where(same_head, 1.0, 0.0).astype(BF16)
    cw = cw_ref[...]
    gw = GROUP_WIDTH
    er = _iota((SMALL_W, gw), 0)
    ec = _iota((SMALL_W, gw), 1) // HEAD_DIM
    exp_beta = jnp.where(er == LANE_BETA + ec, 1.0, 0.0).astype(BF16)
    exp_g = jnp.where(er == LANE_GDEC + ec, 1.0, 0.0).astype(BF16)

    prep = []
    for bb in range(nb):
        xq, xk, xv = q_ref[bb], k_ref[bb], v_ref[bb]
        q = _silu(_causal_conv4(xq, pq_ref[bb], cw[:, 0:gw]))
        k = _silu(_causal_conv4(xk, pk_ref[bb], cw[:, gw:2 * gw]))
        v = _silu(_causal_conv4(xv, pv_ref[bb], cw[:, 2 * gw:3 * gw]))
        pq_ref[bb] = xq[T - 8:T, :]
        pk_ref[bb] = xk[T - 8:T, :]
        pv_ref[bb] = xv[T - 8:T, :]
        q = q * lax.rsqrt(_dot_exact_rhs(q * q, ones_bf) + EPS) * (HEAD_DIM ** -0.5)
        k = k * lax.rsqrt(_dot_exact_rhs(k * k, ones_bf) + EPS)
        sm = sm_ref[bb]
        betax = _dot_exact_rhs(sm, exp_beta)
        gx = _dot_exact_rhs(sm, exp_g)
        prep.append((q, k, v, betax, gx))

    n_chunks = T // C
    trow = _iota((C, gw), 0)
    scol = _iota((C, gw), 1) % HEAD_DIM
    incl = scol <= trow
    strict = scol < trow
    eye = jnp.where(scol == trow, 1.0, 0.0)

    def bdiag(x_cat):
        return _stack_heads(x_cat).astype(BF16)

    chains = []
    for ci in range(n_chunks):
        sl = slice(ci * C, (ci + 1) * C)
        for bb in range(nb):
            q, k, v, betax, gx = prep[bb]
            qc, kc, vc, bx, gc = q[sl], k[sl], v[sl], betax[sl], gx[sl]
            grow = gr_ref[bb, ci, 0:1, :]
            eg = jnp.exp(gc)
            kb = kc * bx
            g_last = gc[C - 1:C, :]
            gamma = jnp.where(incl, jnp.exp(jnp.where(incl, gc - grow, 0.0)), 0.0)
            ks_bf = bdiag(kc)
            m = jnp.where(strict, _dot_nt(kb.astype(BF16), ks_bf) * gamma, 0.0)
            chains.append(dict(
                x=eye - m, y=m,
                a_qk=(_dot_nt(qc.astype(BF16), ks_bf) * gamma).astype(BF16),
                rhs_u=bdiag(vc * bx),
                rhs_w=bdiag(kb * eg),
                q_dec=(qc * eg).astype(BF16),
                k_dec=(kc * jnp.exp(g_last - gc)).astype(BF16),
                decay=jnp.exp(g_last)))
    for ch in chains:
        ch["ybd"] = bdiag(ch["y"])
    for _ in range(5):
        for ch in chains:
            ch["y"] = _dot(ch["y"].astype(BF16), ch["ybd"])
            ch["ybd"] = bdiag(ch["y"])
        for ch in chains:
            ch["x"] = ch["x"] + _dot(ch["x"].astype(BF16), ch["ybd"])
    for ch in chains:
        t_inv = ch["x"].astype(BF16)
        ch["u"] = _dot(t_inv, ch["rhs_u"])
        ch["wk"] = _dot(t_inv, ch["rhs_w"]).astype(BF16)

    outs = [[] for _ in range(nb)]
    for ci in range(n_chunks):
        for bb in range(nb):
            ch = chains[ci * nb + bb]
            st = st_ref[bb]
            st_bf = st.astype(BF16)
            v_new = ch["u"] - _dot(ch["wk"], st_bf)
            outs[bb].append(_dot(ch["q_dec"], st_bf) + _dot(ch["a_qk"], bdiag(v_new)))
            upd = _dot_tn(ch["k_dec"], v_new.astype(BF16))
            st_ref[bb] = st * ch["decay"] + jnp.where(same_head, upd, 0.0)
    for bb in range(nb):
        o = jnp.concatenate(outs[bb], axis=0)
        o_ref[bb] = _head_rms(o, ones_bf) * _silu(z_ref[bb])


def gdn(proj3, sm, g_row, conv_w):
    b, s, _ = proj3.shape
    t = SEQ_TILE
    gw = GROUP_WIDTH
    nb = GDN_BATCH
    col = lambda cidx: pl.BlockSpec((nb, t, gw), lambda i, j: (i, j, cidx))
    return pl.pallas_call(
        _gdn_kernel,
        grid=(b // nb, s // t),
        in_specs=[col(COL_DQ), col(COL_DK), col(COL_DV), col(COL_DZ),
                  pl.BlockSpec((nb, t, SMALL_W), lambda i, j: (i, j, 0)),
                  pl.BlockSpec((nb, t // GDN_CHUNK, 8, gw), lambda i, j: (i, j, 0, 0)),
                  pl.BlockSpec((CONV_W, 3 * gw), lambda i, j: (0, 0))],
        out_specs=pl.BlockSpec((nb, t, gw), lambda i, j: (i, j, 0)),
        out_shape=jax.ShapeDtypeStruct((b, s, gw), F32),
        scratch_shapes=[pltpu.VMEM((nb, gw, gw), F32)] + [pltpu.VMEM((nb, 8, gw), F32)] * 3,
        compiler_params=_cparams("parallel", "arbitrary"),
        name="gdn",
    )(proj3, proj3, proj3, proj3, sm, g_row, conv_w)


def _mix_out(x, ys, gain, w_ref):
    gw = GROUP_WIDTH
    acc = x
    for i, y in enumerate(ys):
        yg = (y * gain[:, i * gw:(i + 1) * gw]).astype(BF16)
        acc = acc + _dot(yg, w_ref[i * gw:(i + 1) * gw, :])
    return acc


def _mem_attention(x, g, wq_ref, kv, wo_ref):
    t = x.shape[0]
    gw = GROUP_WIDTH
    ms = jnp.mean(x * x, axis=-1, keepdims=True)
    h = ((x * lax.rsqrt(ms + EPS)) * g).astype(BF16)
    q = _dot(h, wq_ref[...])
    k = kv[:, 0:gw].astype(BF16)
    v = kv[:, gw:2 * gw].astype(BF16)
    s = _dot_nt(k, _stack_heads(q).astype(BF16)) * (HEAD_DIM ** -0.5)
    s = s - jnp.max(s, axis=0, keepdims=True)
    p = jnp.exp(s)
    p = p * (1.0 / jnp.sum(p, axis=0, keepdims=True))
    ot = _dot_tn(v, p.astype(BF16))
    oc = jnp.concatenate([ot[hh * HEAD_DIM:(hh + 1) * HEAD_DIM, hh * t:(hh + 1) * t] for hh in range(HEADS)], axis=0).T
    return x + _dot(oc.astype(BF16), wo_ref[...])


def _post_mix_kernel(x_ref, ya_ref, yb_ref, yc_ref, yd_ref, gain_ref, wout_ref, gm_ref, wq_ref, kv_ref, wo_ref,
                     g_ref, w_ref, b_ref, x2_ref, h_ref, r_ref, n_ref, cnt_ref):
    @pl.when((pl.program_id(0) == 0) & (pl.program_id(1) == 0))
    def _():
        cnt_ref[...] = jnp.zeros_like(cnt_ref)

    x1 = _mix_out(x_ref[0], (ya_ref[0], yb_ref[0], yc_ref[0], yd_ref[0]), gain_ref[...], wout_ref)
    x = _mem_attention(x1, gm_ref[...], wq_ref, kv_ref[0], wo_ref)
    x2_ref[0] = x
    tm = x.shape[0]
    ms = jnp.mean(x * x, axis=-1, keepdims=True)
    h = (x * lax.rsqrt(ms + EPS)) * g_ref[...]
    h_ref[...] = _pack_bf16_pairs(h)
    hh, hm, _ = _split3(h)
    w = w_ref[...]
    wh, wm, _ = _split3(w)
    logits = _dot(hh, wh) + _dot(hh, wm) + _dot(hm, wh) + b_ref[...]
    lane = _iota(logits.shape, 1)
    big = jnp.int32(1 << 30)
    neg = -jnp.inf
    is_grp = (lane >= N_EXPERTS) & (lane < N_EXPERTS + N_EXPERT_GROUPS)
    gl = jnp.where(is_grp, logits, neg)
    gmax = jnp.max(gl, axis=-1, keepdims=True)
    p_grp = 1.0 / jnp.sum(jnp.exp(gl - gmax), axis=-1, keepdims=True)
    g_sel = jnp.min(jnp.where(gl == gmax, lane, big), axis=-1, keepdims=True) - N_EXPERTS
    in_grp = (lane < N_EXPERTS) & (lane // EXPERTS_PER_GROUP == g_sel)
    el = jnp.where(in_grp, logits, neg)
    m1 = jnp.max(el, axis=-1, keepdims=True)
    i1 = jnp.min(jnp.where(el == m1, lane, big), axis=-1, keepdims=True)
    el2 = jnp.where(lane == i1, neg, el)
    m2 = jnp.max(el2, axis=-1, keepdims=True)
    i2 = jnp.min(jnp.where(el2 == m2, lane, big), axis=-1, keepdims=True)
    e21 = jnp.exp(m2 - m1)
    w1 = 1.0 / (1.0 + e21)
    w2 = e21 / (1.0 + e21)
    oh1 = lane == i1
    oh2 = lane == i2
    onehot = jnp.where(oh1 | oh2, 1.0, 0.0)
    strict = jnp.where(_iota((tm, tm), 0) > _iota((tm, tm), 1), 1.0, 0.0).astype(BF16)
    before = _dot(strict, onehot.astype(BF16)) + cnt_ref[...]
    rank1 = jnp.sum(jnp.where(oh1, before, 0.0), axis=-1, keepdims=True)
    rank2 = jnp.sum(jnp.where(oh2, before, 0.0), axis=-1, keepdims=True)
    cnt = before[tm - 1:tm, :] + onehot[tm - 1:tm, :]
    cnt_ref[...] = cnt
    n_ref[...] = jnp.broadcast_to(cnt, n_ref.shape)
    cols = (i1.astype(F32), i2.astype(F32), p_grp * w1, p_grp * w2, rank1, rank2)
    route = jnp.zeros(logits.shape, F32)
    for j, col in enumerate(cols):
        route = jnp.where(lane == j, col, route)
    r_ref[...] = route


def post_mix(x3, ya, yb, yc, yd, gain, w_out_bf16, g_mem, wq_bf16, kv3, wo_bf16, g_ffn, w_router, b_router):
    b, s, d = x3.shape
    t = ROW_TILE
    gw = GROUP_WIDTH
    nt = s // t
    mlen = kv3.shape[1]
    tile = lambda w: pl.BlockSpec((1, t, w), lambda i, j: (i, j, 0))
    const = lambda shape: pl.BlockSpec(shape, lambda i, j: (0,) * len(shape))
    tok = lambda w: pl.BlockSpec((t, w), lambda i, j: (i * nt + j, 0))
    return pl.pallas_call(
        _post_mix_kernel,
        grid=(b, nt),
        in_specs=[tile(d), tile(gw), tile(gw), tile(gw), tile(gw),
                  const((1, 4 * gw)), const((4 * gw, d)),
                  const((1, d)), const((d, gw)), pl.BlockSpec((1, mlen, 2 * gw), lambda i, j: (i, 0, 0)), const((gw, d)),
                  const((1, d)), const((d, SMALL_W)), const((1, SMALL_W))],
        out_specs=[tile(d), tok(d // 2), tok(SMALL_W), const((8, SMALL_W))],
        out_shape=[jax.ShapeDtypeStruct((b, s, d), F32), jax.ShapeDtypeStruct((b * s, d // 2), jnp.uint32),
                   jax.ShapeDtypeStruct((b * s, SMALL_W), F32), jax.ShapeDtypeStruct((8, SMALL_W), F32)],
        scratch_shapes=[pltpu.VMEM((1, SMALL_W), F32)],
        compiler_params=_cparams("arbitrary", "arbitrary"),
        name="post_mix",
    )(x3, ya, yb, yc, yd, gain.reshape(1, 4 * gw), w_out_bf16, g_mem.reshape(1, d), wq_bf16, kv3, wo_bf16,
      g_ffn.reshape(1, d), w_router, b_router)


def _experts_kernel(te_ref, nu_ref, xs_ref, wgu_ref, wdn_ref, y_ref, wgu_bf, wdn_bf):
    i = pl.program_id(0)

    @pl.when(i >= nu_ref[0])
    def _():
        y_ref[...] = jnp.zeros_like(y_ref)

    @pl.when(i < nu_ref[0])
    def _():
        prev = te_ref[jnp.maximum(i - 1, 0)]

        @pl.when((i == 0) | (te_ref[i] != prev))
        def _():
            wgu_bf[...] = wgu_ref[0, 0].astype(BF16)
            wdn_bf[...] = wdn_ref[0, 0].astype(BF16)

        gu = _dot(_unpack_bf16_pairs(xs_ref[...]).astype(BF16), wgu_bf[...])
        act = _silu(gu[:, 0:D_EXPERT]) * gu[:, D_EXPERT:2 * D_EXPERT]
        y_ref[...] = _pack_bf16_pairs(_dot(act.astype(BF16), wdn_bf[...]))


def moe_experts(xs, tile_expert, n_used, w_gu, w_dn, layer, tm):
    n_rows, dp = xs.shape
    d = 2 * dp
    tile = lambda i, te, nu: (jnp.minimum(i, nu[0] - 1), 0)
    wsel = lambda i, te, nu: (layer, te[jnp.minimum(i, nu[0] - 1)], 0, 0)
    return pl.pallas_call(
        _experts_kernel,
        grid_spec=pltpu.PrefetchScalarGridSpec(
            num_scalar_prefetch=2,
            grid=(n_rows // tm,),
            in_specs=[pl.BlockSpec((tm, dp), tile),
                      pl.BlockSpec((1, 1, d, 2 * D_EXPERT), wsel),
                      pl.BlockSpec((1, 1, D_EXPERT, d), wsel)],
            out_specs=pl.BlockSpec((tm, dp), lambda i, te, nu: (i, 0)),
            scratch_shapes=[pltpu.VMEM((d, 2 * D_EXPERT), BF16), pltpu.VMEM((D_EXPERT, d), BF16)]),
        out_shape=jax.ShapeDtypeStruct((n_rows, dp), jnp.uint32),
        compiler_params=_cparams("arbitrary"),
        name="moe_experts",
    )(tile_expert, n_used, xs, w_gu, w_dn)


def sc_gather_rows(table, idx):
    n_idx = idx.shape[0]
    width = table.shape[1]
    info = plsc.get_sparse_core_info()
    n_workers = info.num_cores * info.num_subcores
    per_worker = n_idx // n_workers
    chunk = SC_GATHER_CHUNK
    assert n_idx % (n_workers * chunk) == 0 and width % info.num_lanes == 0
    mesh = plsc.VectorSubcoreMesh(core_axis_name="c", subcore_axis_name="s")

    def body(table_hbm, idx_hbm, out_hbm, idx_v, rows_v, sem):
        wid = lax.axis_index("s") * info.num_cores + lax.axis_index("c")
        base = wid * per_worker

        @pl.loop(0, per_worker // chunk)
        def _(j):
            off = pl.multiple_of(base + j * chunk, chunk)
            pltpu.sync_copy(idx_hbm.at[pl.ds(off, chunk)], idx_v)
            pltpu.async_copy(table_hbm.at[idx_v], rows_v, sem).wait()
            pltpu.sync_copy(rows_v, out_hbm.at[pl.ds(off, chunk)])

    return pl.kernel(
        body,
        out_type=jax.ShapeDtypeStruct((n_idx, width), table.dtype),
        mesh=mesh,
        scratch_types=[pltpu.VMEM((chunk,), jnp.int32), pltpu.VMEM((chunk, width), table.dtype),
                       pltpu.SemaphoreType.DMA],
        name="sc_gather_rows",
    )(table, idx)


def sc_dispatch_rows(rows, dest, pad_idx, zero_rows):
    m, width = rows.shape
    n_pad = pad_idx.shape[0]
    info = plsc.get_sparse_core_info()
    n_workers = info.num_cores * info.num_subcores
    chunk = SC_GATHER_CHUNK
    assert m % (n_workers * chunk) == 0 and n_pad % (n_workers * chunk) == 0 and zero_rows.shape == (chunk, width)
    rows_per_worker = m // n_workers
    pad_per_worker = n_pad // n_workers
    mesh = plsc.VectorSubcoreMesh(core_axis_name="c", subcore_axis_name="s")

    def body(rows_hbm, dest_hbm, pad_hbm, zero_hbm, out_hbm, idx_v, rows_v, sem):
        wid = lax.axis_index("s") * info.num_cores + lax.axis_index("c")

        @pl.loop(0, rows_per_worker // chunk)
        def _(j):
            off = pl.multiple_of(wid * rows_per_worker + j * chunk, chunk)
            pltpu.sync_copy(rows_hbm.at[pl.ds(off, chunk)], rows_v)
            for slot in range(2):
                pltpu.sync_copy(dest_hbm.at[pl.ds(slot * m + off, chunk)], idx_v)
                pltpu.async_copy(rows_v, out_hbm.at[idx_v], sem).wait()

        pltpu.sync_copy(zero_hbm, rows_v)

        @pl.loop(0, pad_per_worker // chunk)
        def _(j):
            off = pl.multiple_of(wid * pad_per_worker + j * chunk, chunk)
            pltpu.sync_copy(pad_hbm.at[pl.ds(off, chunk)], idx_v)
            pltpu.async_copy(rows_v, out_hbm.at[idx_v], sem).wait()

    return pl.kernel(
        body,
        out_type=jax.ShapeDtypeStruct((2 * m + n_pad, width), rows.dtype),
        mesh=mesh,
        scratch_types=[pltpu.VMEM((chunk,), jnp.int32), pltpu.VMEM((chunk, width), rows.dtype),
                       pltpu.SemaphoreType.DMA],
        name="sc_dispatch_rows",
    )(rows, dest, pad_idx, zero_rows)


def _combine_rows_kernel(x_ref, r_ref, g_ref, y1_ref, y2_ref, o_ref, *, final_norm):
    r = r_ref[...]
    out = x_ref[...] + r[:, 2:3] * _unpack_bf16_pairs(y1_ref[...]) + r[:, 3:4] * _unpack_bf16_pairs(y2_ref[...])
    if final_norm:
        ms = jnp.mean(out * out, axis=-1, keepdims=True)
        out = (out * lax.rsqrt(ms + EPS)) * g_ref[...]
    o_ref[...] = out


def moe_combine_rows(x2, route, y_rows, g_final, tm, final_norm):
    m, d = x2.shape
    steps = m // tm
    return pl.pallas_call(
        functools.partial(_combine_rows_kernel, final_norm=final_norm),
        grid=(steps,),
        in_specs=[pl.BlockSpec((tm, d), lambda i: (i, 0)),
                  pl.BlockSpec((tm, SMALL_W), lambda i: (i, 0)),
                  pl.BlockSpec((1, d), lambda i: (0, 0)),
                  pl.BlockSpec((tm, d // 2), lambda i: (i, 0)),
                  pl.BlockSpec((tm, d // 2), lambda i: (i + steps, 0))],
        out_specs=pl.BlockSpec((tm, d), lambda i: (i, 0)),
        out_shape=jax.ShapeDtypeStruct((m, d), F32),
        compiler_params=_cparams("parallel"),
        name="moe_combine_rows",
    )(x2, route, g_final.reshape(1, d), y_rows, y_rows)


def moe_routed(x2, h, route, counts, w_gu, w_dn, layer, g_final, final_norm):
    m, d = x2.shape
    te_rows = EXPERT_TILE
    n_rows = 2 * m + N_EXPERTS * te_rows
    cnt = counts[0, 0:N_EXPERTS].astype(jnp.int32)
    padded = (cnt + te_rows - 1) // te_rows * te_rows
    seg_end = jnp.cumsum(padded)
    seg_start = seg_end - padded
    idx = route[:, 0:2].astype(jnp.int32)
    onehot = idx[:, :, None] == jnp.arange(N_EXPERTS, dtype=jnp.int32)
    dest = (jnp.sum(jnp.where(onehot, seg_start, 0), axis=-1) + route[:, 4:6].astype(jnp.int32)).T.reshape(2 * m)
    tile_start = jnp.arange(n_rows // te_rows, dtype=jnp.int32) * te_rows
    tile_expert = jnp.minimum(jnp.sum((tile_start[:, None] >= seg_end[None, :]).astype(jnp.int32), axis=1), N_EXPERTS - 1)
    n_used = (seg_end[N_EXPERTS - 1] // te_rows).reshape(1)
    n_pad = n_rows - 2 * m
    experts = jnp.arange(N_EXPERTS, dtype=jnp.int32)
    pad_end = jnp.cumsum(padded - cnt)
    k = jnp.arange(n_pad, dtype=jnp.int32)
    k_expert = jnp.sum((k[:, None] >= pad_end[None, :]).astype(jnp.int32), axis=1)
    first_pad = seg_start + cnt - (pad_end - (padded - cnt))
    in_segment = jnp.sum(jnp.where(k_expert[:, None] == experts[None, :], first_pad[None, :], 0), axis=1) + k
    pad_idx = jnp.where(k_expert < N_EXPERTS, in_segment, seg_end[N_EXPERTS - 1] + k - pad_end[N_EXPERTS - 1])
    xs = sc_dispatch_rows(h, dest, pad_idx, jnp.zeros((SC_GATHER_CHUNK, d // 2), h.dtype))
    ys = moe_experts(xs, tile_expert, n_used, w_gu, w_dn, layer, tm=te_rows)
    y_rows = sc_gather_rows(ys, dest)
    return moe_combine_rows(x2, route, y_rows, g_final, tm=ROW_TILE, final_norm=final_norm)


def _w_in_prep_kernel(w_ref, o_ref):
    gw = GROUP_WIDTH
    a_end = 7 * gw
    c_start = a_end + HEADS
    c_end = c_start + 5 * gw
    z_start = c_end + 2 * HEADS
    assert a_end % SMALL_W == LANE_FOX and c_end % SMALL_W == LANE_BETA and LANE_GDEC == LANE_BETA + HEADS
    w = w_ref[...]
    o_ref[:, 0:a_end] = w[:, 0:a_end].astype(BF16)
    o_ref[:, a_end:a_end + 5 * gw] = w[:, c_start:c_end].astype(BF16)
    o_ref[:, 12 * gw:13 * gw] = w[:, z_start:z_start + gw].astype(BF16)
    blk_fox = w[:, a_end:a_end + SMALL_W]
    blk_gdn = w[:, c_end - LANE_BETA:c_end - LANE_BETA + SMALL_W]
    lane = _iota(blk_fox.shape, 1)
    small = jnp.where(lane < LANE_BETA, blk_fox, jnp.where(lane < LANE_GDEC + HEADS, blk_gdn, 0.0))
    o_ref[:, 13 * gw:13 * gw + SMALL_W] = small.astype(BF16)
    o_ref[:, 13 * gw + SMALL_W:] = jnp.zeros((w.shape[0], SMALL_W), BF16)


def w_in_prep(w_all, layer):
    _, d, n = w_all.shape
    tr = 128
    return pl.pallas_call(
        _w_in_prep_kernel,
        grid=(d // tr,),
        in_specs=[pl.BlockSpec((None, tr, n), lambda i: (layer, i, 0))],
        out_specs=pl.BlockSpec((tr, N_PROJ), lambda i: (i, 0)),
        out_shape=jax.ShapeDtypeStruct((d, N_PROJ), BF16),
        compiler_params=_cparams("parallel"),
        name="w_in_prep",
    )(w_all)


def _block_diag(w):
    h = w.shape[0]
    eye = jnp.eye(h, dtype=w.dtype)
    return (eye[:, None, :, None] * w[:, :, None, :]).reshape(h * HEAD_DIM, h * HEAD_DIM)


def kernel(x, mem, norm_mix, w_in, hgrn_lb, fox_fb, lru_conv_w, lru_conv_b, lru_wa, lru_ba, lru_wx, lru_bx, lru_lam, gdn_conv_w, gdn_a_log, gdn_dt_bias, mix_gain, w_out, norm_mem, norm_memkv, w_mq, w_mkv, w_mo, norm_ffn, w_rg, b_rg, w_re, b_re, w_e_gu, w_e_dn, norm_final):
    b, s, d = x.shape
    depth = w_in.shape[0]
    t_tok = b * s
    mlen = mem.shape[1]
    gw = GROUP_WIDTH

    lb_all = jnp.cumsum(jax.nn.softmax(hgrn_lb.astype(F32), axis=0), axis=0)
    lb_all = lb_all - lb_all[0]

    x = x.astype(F32)
    for l in range(depth):
        proj = norm_matmul(x.reshape(t_tok, d), norm_mix[l], w_in_prep(w_in, l), tm=ROW_TILE)
        proj3 = proj.reshape(b, s, N_PROJ)
        prm = jnp.zeros((8, SMALL_W), F32)
        prm = prm.at[0, LANE_FOX:LANE_FOX + HEADS].set(fox_fb[l].astype(F32))
        prm = prm.at[0, LANE_GDEC:LANE_GDEC + HEADS].set(gdn_dt_bias[l].astype(F32))
        prm = prm.at[1, LANE_GDEC:LANE_GDEC + HEADS].set(gdn_a_log[l].astype(F32))
        sm, c_exp = small_prep(proj3, prm)
        g_row = sm[:, :, LANE_GDEC:LANE_GDEC + HEADS].reshape(b, s // GDN_CHUNK, GDN_CHUNK, HEADS)
        g_row = jnp.swapaxes(g_row, 2, 3).reshape(b, s // GDN_CHUNK, 1, gw)
        g_row = jnp.broadcast_to(g_row, (b, s // GDN_CHUNK, 8, gw))

        ya = hgrn2(proj3, lb_all[l])
        yb = fox_attention(proj3, c_exp)
        yc = rglru(proj3, lru_conv_w[l], lru_conv_b[l], _block_diag(lru_wa[l]).astype(BF16), lru_ba[l],
                   _block_diag(lru_wx[l]).astype(BF16), lru_bx[l], lru_lam[l])
        yd = gdn(proj3, sm, g_row, gdn_conv_w[l])

        kv = norm_matmul(mem.reshape(b * mlen, d), norm_memkv[l], w_mkv[l].astype(BF16), tm=256)
        w_router = jnp.concatenate([w_re[l], w_rg[l], jnp.zeros((d, SMALL_W - N_EXPERTS - N_EXPERT_GROUPS), F32)], axis=1)
        b_router = jnp.concatenate([b_re[l], b_rg[l], jnp.zeros((SMALL_W - N_EXPERTS - N_EXPERT_GROUPS,), F32)]).reshape(1, SMALL_W)
        x, h, route, counts = post_mix(x, ya, yb, yc, yd, mix_gain[l], w_out[l].astype(BF16), norm_mem[l],
                                       w_mq[l].astype(BF16), kv.reshape(b, mlen, 2 * gw), w_mo[l].astype(BF16),
                                       norm_ffn[l], w_router, b_router)

        x = moe_routed(x.reshape(t_tok, d), h, route, counts, w_e_gu, w_e_dn, l,
                       norm_final, final_norm=(l == depth - 1)).reshape(b, s, d)

    return x
```

```python
import functools
import math

import jax
import jax.numpy as jnp
from jax import lax
from jax.experimental import pallas as pl
from jax.experimental.pallas import tpu as pltpu
from jax.experimental.pallas import tpu_sc as plsc

F32 = jnp.float32
BF16 = jnp.bfloat16

HEAD_DIM = 64
GROUP_WIDTH = 256
HEADS = GROUP_WIDTH // HEAD_DIM
GDN_CHUNK = 64
GDN_BATCH = 4
FOX_BATCH = 2
HGRN_BATCH = 4
HGRN_CHUNK = 16
CONV_W = 4
LRU_C = 8.0
EPS = 1e-6
N_EXPERT_GROUPS = 4
EXPERTS_PER_GROUP = 8
N_EXPERTS = N_EXPERT_GROUPS * EXPERTS_PER_GROUP
D_EXPERT = 256
SC_GATHER_CHUNK = 128
EXPERT_TILE = 512
SMALL_W = 128
SEQ_TILE = 256
ROW_TILE = 512
VMEM_LIMIT = 56 * 1024 * 1024

(COL_AQ, COL_AF, COL_AI, COL_AG, COL_BQ, COL_BK, COL_BV, COL_CX, COL_CG,
 COL_DQ, COL_DK, COL_DV, COL_DZ) = range(13)
N_PROJ = 13 * GROUP_WIDTH + 2 * SMALL_W
COL_SMALL = 13 * GROUP_WIDTH // SMALL_W
LANE_FOX = 0
LANE_BETA = 4
LANE_GDEC = 8


def _cparams(*sem):
    return pltpu.CompilerParams(dimension_semantics=sem, vmem_limit_bytes=VMEM_LIMIT)


def _dot(a, b):
    return jnp.dot(a, b, preferred_element_type=F32)


def _dot_nt(a, b):
    return lax.dot_general(a, b, (((1,), (1,)), ((), ())), preferred_element_type=F32)


def _dot_tn(a, b):
    return lax.dot_general(a, b, (((0,), (0,)), ((), ())), preferred_element_type=F32)


def _split3(x):
    h = x.astype(BF16)
    r = x - h.astype(F32)
    m = r.astype(BF16)
    l = (r - m.astype(F32)).astype(BF16)
    return h, m, l


def _dot_exact_rhs(x, w_bf16):
    h, m, l = _split3(x)
    return _dot(h, w_bf16) + _dot(m, w_bf16) + _dot(l, w_bf16)


def _dot_exact_lhs(w_bf16, x):
    h, m, l = _split3(x)
    return _dot(w_bf16, h) + _dot(w_bf16, m) + _dot(w_bf16, l)


def _iota(shape, dim):
    return lax.broadcasted_iota(jnp.int32, shape, dim)


def _head_ones(n=GROUP_WIDTH):
    r = _iota((n, n), 0) // HEAD_DIM
    c = _iota((n, n), 1) // HEAD_DIM
    return r == c


def _sigmoid(x):
    return 1.0 / (1.0 + jnp.exp(-x))


def _silu(x):
    return x * _sigmoid(x)


def _log_sigmoid(x):
    return jnp.minimum(x, 0.0) - jnp.log1p(jnp.exp(-jnp.abs(x)))


def _softplus(x):
    return jnp.maximum(x, 0.0) + jnp.log1p(jnp.exp(-jnp.abs(x)))


def _gelu_tanh(x):
    return 0.5 * x * (1.0 + jnp.tanh(math.sqrt(2.0 / math.pi) * (x + 0.044715 * (x * x * x))))


def _pack_bf16_pairs(x):
    n = x.shape[1] // 2
    u = lax.bitcast_convert_type(x.astype(BF16).astype(F32), jnp.uint32)
    return u[:, :n] | (u[:, n:] >> 16)


def _unpack_bf16_pairs(p):
    hi = lax.bitcast_convert_type(p & jnp.uint32(0xFFFF0000), F32)
    lo = lax.bitcast_convert_type(p << 16, F32)
    return jnp.concatenate([hi, lo], axis=1)


def _head_mean_sq(x, ones_bf16):
    return _dot_exact_rhs(x * x, ones_bf16) * (1.0 / HEAD_DIM)


def _head_rms(x, ones_bf16):
    return x * lax.rsqrt(_head_mean_sq(x, ones_bf16) + EPS)


def _stack_heads(x):
    lane_head = _iota(x.shape, 1) // HEAD_DIM
    parts = []
    for h in range(HEADS):
        parts.append(jnp.where(lane_head == h, x, 0.0))
    return jnp.concatenate(parts, axis=0)


def _causal_conv4(x, pad_ref, w):
    r = x.shape[0]
    pad_ref[8:8 + r, :] = x
    acc = x * w[CONV_W - 1:CONV_W, :]
    for k in range(1, CONV_W):
        acc = acc + pad_ref[8 - k:8 - k + r, :] * w[CONV_W - 1 - k:CONV_W - k, :]
    pad_ref[0:8, :] = x[r - 8:r, :]
    return acc


def _norm_matmul_kernel(x_ref, g_ref, w_ref, o_ref):
    x = x_ref[...]
    ms = jnp.mean(x * x, axis=-1, keepdims=True)
    h = (x * lax.rsqrt(ms + EPS)) * g_ref[...]
    o_ref[...] = _dot(h.astype(BF16), w_ref[...]).astype(o_ref.dtype)


def norm_matmul(x, g, w_bf16, tm, out_dtype=F32):
    m, d = x.shape
    n = w_bf16.shape[1]
    return pl.pallas_call(
        _norm_matmul_kernel,
        grid=(m // tm,),
        in_specs=[pl.BlockSpec((tm, d), lambda i: (i, 0)),
                  pl.BlockSpec((1, d), lambda i: (0, 0)),
                  pl.BlockSpec((d, n), lambda i: (0, 0))],
        out_specs=pl.BlockSpec((tm, n), lambda i: (i, 0)),
        out_shape=jax.ShapeDtypeStruct((m, n), out_dtype),
        compiler_params=_cparams("parallel"),
        name="norm_matmul",
    )(x, g.reshape(1, d), w_bf16)


def _small_prep_kernel(s_ref, p_ref, o_ref, x_ref):
    S = s_ref.shape[1]
    W = s_ref.shape[2]
    blk = SEQ_TILE
    lane = _iota((blk, W), 1)
    is_fox = (lane >= LANE_FOX) & (lane < LANE_FOX + HEADS)
    is_beta = (lane >= LANE_BETA) & (lane < LANE_BETA + HEADS)
    is_gdec = (lane >= LANE_GDEC) & (lane < LANE_GDEC + HEADS)
    r = _iota((blk, blk), 0)
    c = _iota((blk, blk), 1)
    tril_all = jnp.where(r >= c, 1.0, 0.0).astype(BF16)
    tril_chunk = jnp.where((r >= c) & (r // GDN_CHUNK == c // GDN_CHUNK), 1.0, 0.0).astype(BF16)
    neg_exp_a = -jnp.exp(p_ref[1:2, :])
    carry = jnp.zeros((1, W), F32)
    for i in range(S // blk):
        sl = pl.ds(i * blk, blk)
        z = s_ref[0, sl, :] + p_ref[0:1, :]
        fox = jnp.where(is_fox, _log_sigmoid(z), 0.0)
        beta = jnp.where(is_beta, _sigmoid(z), 0.0)
        gdec = jnp.where(is_gdec, neg_exp_a * _softplus(z), 0.0)
        cf = _dot_exact_lhs(tril_all, fox) + carry
        cg = _dot_exact_lhs(tril_chunk, gdec)
        carry = cf[blk - 1:blk, :]
        o_ref[0, sl, :] = cf + cg + beta
        both = cf + cg
        for j, src_lane in enumerate(range(LANE_FOX, LANE_FOX + HEADS)):
            x_ref[0, sl, j * W:(j + 1) * W] = jnp.broadcast_to(both[:, src_lane:src_lane + 1], (blk, W))


def small_prep(proj3, params):
    b, s, _ = proj3.shape
    w = SMALL_W
    return pl.pallas_call(
        _small_prep_kernel,
        grid=(b,),
        in_specs=[pl.BlockSpec((1, s, w), lambda i: (i, 0, COL_SMALL)),
                  pl.BlockSpec((8, w), lambda i: (0, 0))],
        out_specs=[pl.BlockSpec((1, s, w), lambda i: (i, 0, 0)),
                   pl.BlockSpec((1, s, HEADS * w), lambda i: (i, 0, 0))],
        out_shape=[jax.ShapeDtypeStruct((b, s, w), F32), jax.ShapeDtypeStruct((b, s, HEADS * w), F32)],
        compiler_params=_cparams("parallel"),
        name="small_prep",
    )(proj3, params)


def _hgrn2_kernel(q_ref, f_ref, i_ref, g_ref, lb_ref, o_ref, st_ref, qs, ks, vs, bs, os_):
    nb = q_ref.shape[0]
    T = q_ref.shape[1]
    C = HGRN_CHUNK
    seqs = range(nb)

    @pl.when(pl.program_id(1) == 0)
    def _():
        st_ref[...] = jnp.zeros_like(st_ref)

    same_head = _head_ones()
    ones_bf = jnp.where(same_head, 1.0, 0.0).astype(BF16)
    lb = lb_ref[...]
    log_lb = jnp.log(lb)
    r = _iota((T, T), 0)
    c = _iota((T, T), 1)
    tril_chunk = jnp.where((r >= c) & (r // C == c // C), 1.0, 0.0).astype(BF16)
    for bb in seqs:
        fl = f_ref[bb]
        c2 = jnp.log1p(-lb) + _log_sigmoid(fl)
        mx = jnp.maximum(log_lb, c2)
        log_f = mx + jnp.log1p(jnp.exp(-jnp.abs(log_lb - c2)))
        bs[bb] = _dot_exact_lhs(tril_chunk, log_f)
        qs[bb] = _silu(q_ref[bb])
        ks[bb] = (1.0 - lb) * _sigmoid(-fl)
        vs[bb] = i_ref[bb]

    H8 = C // 2
    trow8 = _iota((H8, GROUP_WIDTH), 0)

    def chunk(ci, carry):
        r0 = pl.multiple_of(ci * C, C)
        qc = [qs[bb, pl.ds(r0, C), :] for bb in seqs]
        kc = [ks[bb, pl.ds(r0, C), :] for bb in seqs]
        vc = [vs[bb, pl.ds(r0, C), :] for bb in seqs]
        bc = [bs[bb, pl.ds(r0, C), :] for bb in seqs]
        st = [st_ref[bb] for bb in seqs]
        a = []
        for bb in seqs:
            parts = []
            for half, s_range in ((0, range(H8)), (1, range(C))):
                b_half = bc[bb][half * H8:(half + 1) * H8, :]
                q_half = qc[bb][half * H8:(half + 1) * H8, :]
                for s in s_range:
                    blk = jnp.exp(b_half - bc[bb][s:s + 1, :]) * (q_half * kc[bb][s:s + 1, :])
                    if s >= half * H8:
                        blk = jnp.where(trow8 >= s - half * H8, blk, 0.0)
                    parts.append(blk)
            a.append(jnp.concatenate(parts, axis=0).astype(BF16))
        sc = [_dot(a[bb], ones_bf) for bb in seqs]
        o = [_dot_nt((qc[bb] * jnp.exp(bc[bb])).astype(BF16), st[bb].astype(BF16)) for bb in seqs]
        b_last = [bc[bb][C - 1:C, :] for bb in seqs]
        upd = [_dot_tn(vc[bb].astype(BF16), (kc[bb] * jnp.exp(b_last[bb] - bc[bb])).astype(BF16)) for bb in seqs]
        for bb in seqs:
            top = o[bb][0:H8, :]
            bot = o[bb][H8:C, :]
            for s in range(H8):
                top = top + sc[bb][s * H8:(s + 1) * H8, :] * vc[bb][s:s + 1, :]
            for s in range(C):
                bot = bot + sc[bb][(H8 + s) * H8:(H8 + s + 1) * H8, :] * vc[bb][s:s + 1, :]
            os_[bb, pl.ds(r0, C), :] = jnp.concatenate([top, bot], axis=0)
            st_ref[bb] = st[bb] * jnp.exp(b_last[bb]) + jnp.where(same_head, upd[bb], 0.0)
        return carry

    lax.fori_loop(0, T // C, chunk, 0)
    for bb in seqs:
        o_ref[bb] = _head_rms(os_[bb], ones_bf) * _silu(g_ref[bb])


def hgrn2(proj3, lb):
    b, s, _ = proj3.shape
    t = SEQ_TILE
    gw = GROUP_WIDTH
    nb = HGRN_BATCH
    col = lambda cidx: pl.BlockSpec((nb, t, gw), lambda i, j: (i, j, cidx))
    return pl.pallas_call(
        _hgrn2_kernel,
        grid=(b // nb, s // t),
        in_specs=[col(COL_AQ), col(COL_AF), col(COL_AI), col(COL_AG),
                  pl.BlockSpec((1, gw), lambda i, j: (0, 0))],
        out_specs=pl.BlockSpec((nb, t, gw), lambda i, j: (i, j, 0)),
        out_shape=jax.ShapeDtypeStruct((b, s, gw), F32),
        scratch_shapes=[pltpu.VMEM((nb, gw, gw), F32)] + [pltpu.VMEM((nb, t, gw), F32)] * 5,
        compiler_params=_cparams("parallel", "arbitrary"),
        name="hgrn2",
    )(proj3, proj3, proj3, proj3, lb.reshape(1, gw))


def _fox_kernel(q_ref, k_ref, v_ref, cx_ref, o_ref, kb_ref, vt_ref, acc_ref):
    nb = q_ref.shape[0]
    tq = q_ref.shape[1]
    tk = tq
    S = k_ref.shape[1]
    qi = pl.program_id(1)
    seqs = range(nb)

    @pl.when(qi == 0)
    def _():
        for bb in seqs:
            for i in range(S // tk):
                sl = pl.ds(i * tk, tk)
                kb_ref[bb, sl, :] = k_ref[bb, sl, :].astype(BF16)
                vt_ref[bb, :, sl] = v_ref[bb, sl, :].T.astype(BF16)

    lane_head = _iota((tq, GROUP_WIDTH), 1) // HEAD_DIM
    qh = []
    for bb in seqs:
        qsc = q_ref[bb] * (HEAD_DIM ** -0.5)
        qh.append([jnp.where(lane_head == h, qsc, 0.0).astype(BF16) for h in range(HEADS)])
    acc_ref[...] = jnp.zeros_like(acc_ref)
    reps = tq // SMALL_W
    causal = _iota((tk, tq), 0) <= _iota((tk, tq), 1)

    def step(kb, stats, masked):
        k0 = pl.multiple_of(kb * tk, tk)
        new_stats = [[None] * HEADS for _ in seqs]
        strips = [(h, bb) for h in range(HEADS) for bb in seqs]

        def scores(h, bb):
            return _dot_nt(kb_ref[bb, pl.ds(k0, tk), :], qh[bb][h])

        ahead = 3
        queue = [scores(*strips[j]) for j in range(ahead)]
        for i, (h, bb) in enumerate(strips):
            st = queue.pop(0)
            if i + ahead < len(strips):
                queue.append(scores(*strips[i + ahead]))
            rows = slice(h * HEAD_DIM, (h + 1) * HEAD_DIM)
            cx = cx_ref[bb, pl.ds(k0, tk), h * SMALL_W:(h + 1) * SMALL_W]
            st = st - jnp.concatenate([cx] * reps, axis=1)
            if masked:
                st = jnp.where(causal, st, -jnp.inf)
            m_old, l_old = stats[bb][h]
            m_new = jnp.maximum(m_old, jnp.max(st, axis=0, keepdims=True))
            alpha = jnp.exp(m_old - m_new)
            p = jnp.exp(st - m_new)
            l_new = alpha * l_old + jnp.sum(p, axis=0, keepdims=True)
            pv = _dot(vt_ref[bb, rows, pl.ds(k0, tk)], p.astype(BF16))
            acc_ref[bb, rows, :] = alpha * acc_ref[bb, rows, :] + pv
            new_stats[bb][h] = (m_new, l_new)
        return tuple(tuple(new_stats[bb]) for bb in seqs)

    init = tuple(tuple((jnp.full((1, tq), -jnp.inf, F32), jnp.zeros((1, tq), F32)) for _ in range(HEADS)) for _ in seqs)
    stats = lax.fori_loop(0, qi, lambda kb, c: step(kb, c, False), init)
    stats = step(qi, stats, True)

    ones_bf = jnp.where(_head_ones(), 1.0, 0.0).astype(BF16)
    for bb in seqs:
        inv_l = jnp.concatenate([jnp.broadcast_to(1.0 / stats[bb][h][1], (HEAD_DIM, tq)) for h in range(HEADS)], axis=0)
        o_ref[bb] = _head_rms((acc_ref[bb] * inv_l).T, ones_bf)


def fox_attention(proj3, c_exp):
    b, s, _ = proj3.shape
    tq = SEQ_TILE
    gw = GROUP_WIDTH
    nb = FOX_BATCH
    return pl.pallas_call(
        _fox_kernel,
        grid=(b // nb, s // tq),
        in_specs=[pl.BlockSpec((nb, tq, gw), lambda i, j: (i, j, COL_BQ)),
                  pl.BlockSpec((nb, s, gw), lambda i, j: (i, 0, COL_BK)),
                  pl.BlockSpec((nb, s, gw), lambda i, j: (i, 0, COL_BV)),
                  pl.BlockSpec((nb, s, HEADS * SMALL_W), lambda i, j: (i, 0, 0))],
        out_specs=pl.BlockSpec((nb, tq, gw), lambda i, j: (i, j, 0)),
        out_shape=jax.ShapeDtypeStruct((b, s, gw), F32),
        scratch_shapes=[pltpu.VMEM((nb, s, gw), BF16),
                        pltpu.VMEM((nb, gw, s), BF16),
                        pltpu.VMEM((nb, gw, tq), F32)],
        compiler_params=_cparams("parallel", "arbitrary"),
        name="fox_attention",
    )(proj3, proj3, proj3, c_exp)


def _rglru_kernel(x_ref, g_ref, cw_ref, cb_ref, wa_ref, ba_ref, wx_ref, bx_ref, lam_ref, o_ref, prev_ref, h_ref):
    @pl.when(pl.program_id(1) == 0)
    def _():
        prev_ref[0:8, :] = jnp.zeros((8, prev_ref.shape[1]), F32)
        h_ref[...] = jnp.zeros_like(h_ref)

    x = x_ref[0]
    t = x.shape[0]
    xc = _causal_conv4(x, prev_ref, cw_ref[...]) + cb_ref[...]
    xb = xc.astype(BF16)
    r = _sigmoid(_dot(xb, wa_ref[...]) + ba_ref[...])
    ig = _sigmoid(_dot(xb, wx_ref[...]) + bx_ref[...])
    log_a = (-LRU_C * r) * _softplus(-lam_ref[...])
    a = jnp.exp(log_a)
    u = jnp.sqrt(1.0 - jnp.exp(2.0 * log_a)) * (ig * xc)
    row = _iota(a.shape, 0)
    d = 1
    while d < t:
        valid = row >= d
        u = jnp.where(valid, a * pltpu.roll(u, d, 0) + u, u)
        a = jnp.where(valid, a * pltpu.roll(a, d, 0), a)
        d *= 2
    h = a * h_ref[...] + u
    h_ref[...] = h[t - 1:t, :]
    ones_bf = jnp.where(_head_ones(), 1.0, 0.0).astype(BF16)
    o_ref[0] = _head_rms(h * _gelu_tanh(g_ref[0]), ones_bf)


def rglru(proj3, conv_w, conv_b, wa_bd, ba, wx_bd, bx, lam):
    b, s, _ = proj3.shape
    t = ROW_TILE
    gw = GROUP_WIDTH
    row = lambda: pl.BlockSpec((1, gw), lambda i, j: (0, 0))
    return pl.pallas_call(
        _rglru_kernel,
        grid=(b, s // t),
        in_specs=[pl.BlockSpec((1, t, gw), lambda i, j: (i, j, COL_CX)),
                  pl.BlockSpec((1, t, gw), lambda i, j: (i, j, COL_CG)),
                  pl.BlockSpec((CONV_W, gw), lambda i, j: (0, 0)), row(),
                  pl.BlockSpec((gw, gw), lambda i, j: (0, 0)), row(),
                  pl.BlockSpec((gw, gw), lambda i, j: (0, 0)), row(), row()],
        out_specs=pl.BlockSpec((1, t, gw), lambda i, j: (i, j, 0)),
        out_shape=jax.ShapeDtypeStruct((b, s, gw), F32),
        scratch_shapes=[pltpu.VMEM((8 + t, gw), F32), pltpu.VMEM((1, gw), F32)],
        compiler_params=_cparams("parallel", "arbitrary"),
        name="rglru",
    )(proj3, proj3, conv_w, conv_b.reshape(1, gw), wa_bd, ba.reshape(1, gw), wx_bd, bx.reshape(1, gw), lam.reshape(1, gw))


def _gdn_kernel(q_ref, k_ref, v_ref, z_ref, sm_ref, gr_ref, cw_ref, o_ref, st_ref, pq_ref, pk_ref, pv_ref):
    nb = q_ref.shape[0]
    T = q_ref.shape[1]
    C = GDN_CHUNK

    @pl.when(pl.program_id(1) == 0)
    def _():
        st_ref[...] = jnp.zeros_like(st_ref)
        for pad_ref in (pq_ref, pk_ref, pv_ref):
            pad_ref[:, 0:8, :] = jnp.zeros((nb, 8, pad_ref.shape[2]), F32)

    same_head = _head_ones()
    ones_bf = jnp.where(same_head, 1.0, 0.0).astype(BF16)
    cw = cw_ref[...]
    gw = GROUP_WIDTH
    er = _iota((SMALL_W, gw), 0)
    ec = _iota((SMALL_W, gw), 1) // HEAD_DIM
    exp_beta = jnp.where(er == LANE_BETA + ec, 1.0, 0.0).astype(BF16)
    exp_g = jnp.where(er == LANE_GDEC + ec, 1.0, 0.0).astype(BF16)

    prep = []
    for bb in range(nb):
        xq, xk, xv = q_ref[bb], k_ref[bb], v_ref[bb]
        q = _silu(_causal_conv4(xq, pq_ref.at[bb], cw[:, 0:gw]))
        k = _silu(_causal_conv4(xk, pk_ref.at[bb], cw[:, gw:2 * gw]))
        v = _silu(_causal_conv4(xv, pv_ref.at[bb], cw[:, 2 * gw:3 * gw]))
        q = q * lax.rsqrt(_dot_exact_rhs(q * q, ones_bf) + EPS) * (HEAD_DIM ** -0.5)
        k = k * lax.rsqrt(_dot_exact_rhs(k * k, ones_bf) + EPS)
        sm = sm_ref[bb]
        betax = _dot_exact_rhs(sm, exp_beta)
        gx = _dot_exact_rhs(sm, exp_g)
        prep.append((q, k, v, betax, gx))

    n_chunks = T // C
    trow = _iota((C, gw), 0)
    scol = _iota((C, gw), 1) % HEAD_DIM
    incl = scol <= trow
    strict = scol < trow
    eye = jnp.where(scol == trow, 1.0, 0.0)

    def bdiag(x_cat):
        return _stack_heads(x_cat).astype(BF16)

    chains = []
    for ci in range(n_chunks):
        sl = slice(ci * C, (ci + 1) * C)
        for bb in range(nb):
            q, k, v, betax, gx = prep[bb]
            qc, kc, vc, bx, gc = q[sl], k[sl], v[sl], betax[sl], gx[sl]
            grow = gr_ref[bb, ci, 0:1, :]
            eg = jnp.exp(gc)
            kb = kc * bx
            g_last = gc[C - 1:C, :]
            gamma = jnp.where(incl, jnp.exp(jnp.where(incl, gc - grow, 0.0)), 0.0)
            ks_bf = bdiag(kc)
            m = jnp.where(strict, _dot_nt(kb.astype(BF16), ks_bf) * gamma, 0.0)
            chains.append(dict(
                x=eye - m, y=m,
                a_qk=(_dot_nt(qc.astype(BF16), ks_bf) * gamma).astype(BF16),
                rhs_u=bdiag(vc * bx),
                rhs_w=bdiag(kb * eg),
                q_dec=(qc * eg).astype(BF16),
                k_dec=(kc * jnp.exp(g_last - gc)).astype(BF16),
                decay=jnp.exp(g_last)))
    for ch in chains:
        ch["ybd"] = bdiag(ch["y"])
    for _ in range(5):
        for ch in chains:
            ch["y"] = _dot(ch["y"].astype(BF16), ch["ybd"])
            ch["ybd"] = bdiag(ch["y"])
        for ch in chains:
            ch["x"] = ch["x"] + _dot(ch["x"].astype(BF16), ch["ybd"])
    for ch in chains:
        t_inv = ch["x"].astype(BF16)
        ch["u"] = _dot(t_inv, ch["rhs_u"])
        ch["wk"] = _dot(t_inv, ch["rhs_w"]).astype(BF16)

    outs = [[] for _ in range(nb)]
    for ci in range(n_chunks):
        for bb in range(nb):
            ch = chains[ci * nb + bb]
            st = st_ref[bb]
            st_bf = st.astype(BF16)
            v_new = ch["u"] - _dot(ch["wk"], st_bf)
            outs[bb].append(_dot(ch["q_dec"], st_bf) + _dot(ch["a_qk"], bdiag(v_new)))
            upd = _dot_tn(ch["k_dec"], v_new.astype(BF16))
            st_ref[bb] = st * ch["decay"] + jnp.where(same_head, upd, 0.0)
    for bb in range(nb):
        o = jnp.concatenate(outs[bb], axis=0)
        o_ref[bb] = _head_rms(o, ones_bf) * _silu(z_ref[bb])


def gdn(proj3, sm, g_row, conv_w):
    b, s, _ = proj3.shape
    t = SEQ_TILE
    gw = GROUP_WIDTH
    nb = GDN_BATCH
    col = lambda cidx: pl.BlockSpec((nb, t, gw), lambda i, j: (i, j, cidx))
    return pl.pallas_call(
        _gdn_kernel,
        grid=(b // nb, s // t),
        in_specs=[col(COL_DQ), col(COL_DK), col(COL_DV), col(COL_DZ),
                  pl.BlockSpec((nb, t, SMALL_W), lambda i, j: (i, j, 0)),
                  pl.BlockSpec((nb, t // GDN_CHUNK, 8, gw), lambda i, j: (i, j, 0, 0)),
                  pl.BlockSpec((CONV_W, 3 * gw), lambda i, j: (0, 0))],
        out_specs=pl.BlockSpec((nb, t, gw), lambda i, j: (i, j, 0)),
        out_shape=jax.ShapeDtypeStruct((b, s, gw), F32),
        scratch_shapes=[pltpu.VMEM((nb, gw, gw), F32)] + [pltpu.VMEM((nb, 8 + t, gw), F32)] * 3,
        compiler_params=_cparams("parallel", "arbitrary"),
        name="gdn",
    )(proj3, proj3, proj3, proj3, sm, g_row, conv_w)


def _mix_out(x, ys, gain, w_ref):
    gw = GROUP_WIDTH
    acc = x
    for i, y in enumerate(ys):
        yg = (y * gain[:, i * gw:(i + 1) * gw]).astype(BF16)
        acc = acc + _dot(yg, w_ref[i * gw:(i + 1) * gw, :])
    return acc


def _mem_attention(x, g, wq_ref, kv, wo_ref):
    t = x.shape[0]
    gw = GROUP_WIDTH
    ms = jnp.mean(x * x, axis=-1, keepdims=True)
    h = ((x * lax.rsqrt(ms + EPS)) * g).astype(BF16)
    q = _dot(h, wq_ref[...])
    k = kv[:, 0:gw].astype(BF16)
    v = kv[:, gw:2 * gw].astype(BF16)
    s = _dot_nt(k, _stack_heads(q).astype(BF16)) * (HEAD_DIM ** -0.5)
    s = s - jnp.max(s, axis=0, keepdims=True)
    p = jnp.exp(s)
    p = p * (1.0 / jnp.sum(p, axis=0, keepdims=True))
    ot = _dot_tn(v, p.astype(BF16))
    oc = jnp.concatenate([ot[hh * HEAD_DIM:(hh + 1) * HEAD_DIM, hh * t:(hh + 1) * t] for hh in range(HEADS)], axis=0).T
    return x + _dot(oc.astype(BF16), wo_ref[...])


def _post_mix_kernel(x_ref, ya_ref, yb_ref, yc_ref, yd_ref, gain_ref, wout_ref, gm_ref, wq_ref, kv_ref, wo_ref,
                     g_ref, w_ref, b_ref, x2_ref, h_ref, r_ref, n_ref, cnt_ref):
    @pl.when((pl.program_id(0) == 0) & (pl.program_id(1) == 0))
    def _():
        cnt_ref[...] = jnp.zeros_like(cnt_ref)

    x1 = _mix_out(x_ref[0], (ya_ref[0], yb_ref[0], yc_ref[0], yd_ref[0]), gain_ref[...], wout_ref)
    x = _mem_attention(x1, gm_ref[...], wq_ref, kv_ref[0], wo_ref)
    x2_ref[0] = x
    tm = x.shape[0]
    ms = jnp.mean(x * x, axis=-1, keepdims=True)
    h = (x * lax.rsqrt(ms + EPS)) * g_ref[...]
    h_ref[...] = _pack_bf16_pairs(h)
    hh, hm, _ = _split3(h)
    w = w_ref[...]
    wh, wm, _ = _split3(w)
    logits = _dot(hh, wh) + _dot(hh, wm) + _dot(hm, wh) + b_ref[...]
    lane = _iota(logits.shape, 1)
    big = jnp.int32(1 << 30)
    neg = -jnp.inf
    is_grp = (lane >= N_EXPERTS) & (lane < N_EXPERTS + N_EXPERT_GROUPS)
    gl = jnp.where(is_grp, logits, neg)
    gmax = jnp.max(gl, axis=-1, keepdims=True)
    p_grp = 1.0 / jnp.sum(jnp.exp(gl - gmax), axis=-1, keepdims=True)
    g_sel = jnp.min(jnp.where(gl == gmax, lane, big), axis=-1, keepdims=True) - N_EXPERTS
    in_grp = (lane < N_EXPERTS) & (lane // EXPERTS_PER_GROUP == g_sel)
    el = jnp.where(in_grp, logits, neg)
    m1 = jnp.max(el, axis=-1, keepdims=True)
    i1 = jnp.min(jnp.where(el == m1, lane, big), axis=-1, keepdims=True)
    el2 = jnp.where(lane == i1, neg, el)
    m2 = jnp.max(el2, axis=-1, keepdims=True)
    i2 = jnp.min(jnp.where(el2 == m2, lane, big), axis=-1, keepdims=True)
    e21 = jnp.exp(m2 - m1)
    w1 = 1.0 / (1.0 + e21)
    w2 = e21 / (1.0 + e21)
    oh1 = lane == i1
    oh2 = lane == i2
    onehot = jnp.where(oh1 | oh2, 1.0, 0.0)
    strict = jnp.where(_iota((tm, tm), 0) > _iota((tm, tm), 1), 1.0, 0.0).astype(BF16)
    before = _dot(strict, onehot.astype(BF16)) + cnt_ref[...]
    rank1 = jnp.sum(jnp.where(oh1, before, 0.0), axis=-1, keepdims=True)
    rank2 = jnp.sum(jnp.where(oh2, before, 0.0), axis=-1, keepdims=True)
    cnt = before[tm - 1:tm, :] + onehot[tm - 1:tm, :]
    cnt_ref[...] = cnt
    n_ref[...] = jnp.broadcast_to(cnt, n_ref.shape)
    cols = (i1.astype(F32), i2.astype(F32), p_grp * w1, p_grp * w2, rank1, rank2)
    route = jnp.zeros(logits.shape, F32)
    for j, col in enumerate(cols):
        route = jnp.where(lane == j, col, route)
    r_ref[...] = route


def post_mix(x3, ya, yb, yc, yd, gain, w_out_bf16, g_mem, wq_bf16, kv3, wo_bf16, g_ffn, w_router, b_router):
    b, s, d = x3.shape
    t = ROW_TILE
    gw = GROUP_WIDTH
    nt = s // t
    mlen = kv3.shape[1]
    tile = lambda w: pl.BlockSpec((1, t, w), lambda i, j: (i, j, 0))
    const = lambda shape: pl.BlockSpec(shape, lambda i, j: (0,) * len(shape))
    tok = lambda w: pl.BlockSpec((t, w), lambda i, j: (i * nt + j, 0))
    return pl.pallas_call(
        _post_mix_kernel,
        grid=(b, nt),
        in_specs=[tile(d), tile(gw), tile(gw), tile(gw), tile(gw),
                  const((1, 4 * gw)), const((4 * gw, d)),
                  const((1, d)), const((d, gw)), pl.BlockSpec((1, mlen, 2 * gw), lambda i, j: (i, 0, 0)), const((gw, d)),
                  const((1, d)), const((d, SMALL_W)), const((1, SMALL_W))],
        out_specs=[tile(d), tok(d // 2), tok(SMALL_W), const((8, SMALL_W))],
        out_shape=[jax.ShapeDtypeStruct((b, s, d), F32), jax.ShapeDtypeStruct((b * s, d // 2), jnp.uint32),
                   jax.ShapeDtypeStruct((b * s, SMALL_W), F32), jax.ShapeDtypeStruct((8, SMALL_W), F32)],
        scratch_shapes=[pltpu.VMEM((1, SMALL_W), F32)],
        compiler_params=_cparams("arbitrary", "arbitrary"),
        name="post_mix",
    )(x3, ya, yb, yc, yd, gain.reshape(1, 4 * gw), w_out_bf16, g_mem.reshape(1, d), wq_bf16, kv3, wo_bf16,
      g_ffn.reshape(1, d), w_router, b_router)


def _experts_kernel(te_ref, nu_ref, run_ref, nxt_ref, xs_ref, wgu_hbm, wdn_hbm, y_ref,
                    wgu_buf, wdn_buf, wgu_bf, wdn_bf, sem, *, layer):
    i = pl.program_id(0)

    def weight_copies(expert, slot):
        return (pltpu.make_async_copy(wgu_hbm.at[layer, expert], wgu_buf.at[slot], sem.at[slot, 0]),
                pltpu.make_async_copy(wdn_hbm.at[layer, expert], wdn_buf.at[slot], sem.at[slot, 1]))

    @pl.when(i >= nu_ref[0])
    def _():
        y_ref[...] = jnp.zeros_like(y_ref)

    @pl.when(i < nu_ref[0])
    def _():
        expert = te_ref[i]
        slot = run_ref[i] % 2

        @pl.when(i == 0)
        def _():
            for c in weight_copies(expert, slot):
                c.start()

        @pl.when((i == 0) | (te_ref[jnp.maximum(i - 1, 0)] != expert))
        def _():
            for c in weight_copies(expert, slot):
                c.wait()
            nxt = nxt_ref[i]

            @pl.when(nxt >= 0)
            def _():
                for c in weight_copies(nxt, 1 - slot):
                    c.start()

            wgu_bf[...] = wgu_buf[slot].astype(BF16)
            wdn_bf[...] = wdn_buf[slot].astype(BF16)

        gu = _dot(_unpack_bf16_pairs(xs_ref[...]).astype(BF16), wgu_bf[...])
        act = _silu(gu[:, 0:D_EXPERT]) * gu[:, D_EXPERT:2 * D_EXPERT]
        y_ref[...] = _pack_bf16_pairs(_dot(act.astype(BF16), wdn_bf[...]))


def moe_experts(xs, tile_expert, n_used, run_index, next_expert, w_gu, w_dn, layer, tm):
    n_rows, dp = xs.shape
    d = 2 * dp
    return pl.pallas_call(
        functools.partial(_experts_kernel, layer=layer),
        grid_spec=pltpu.PrefetchScalarGridSpec(
            num_scalar_prefetch=4,
            grid=(n_rows // tm,),
            in_specs=[pl.BlockSpec((tm, dp), lambda i, te, nu, run, nxt: (jnp.minimum(i, nu[0] - 1), 0)),
                      pl.BlockSpec(memory_space=pl.ANY),
                      pl.BlockSpec(memory_space=pl.ANY)],
            out_specs=pl.BlockSpec((tm, dp), lambda i, te, nu, run, nxt: (i, 0)),
            scratch_shapes=[pltpu.VMEM((2, d, 2 * D_EXPERT), F32), pltpu.VMEM((2, D_EXPERT, d), F32),
                            pltpu.VMEM((d, 2 * D_EXPERT), BF16), pltpu.VMEM((D_EXPERT, d), BF16),
                            pltpu.SemaphoreType.DMA((2, 2))]),
        out_shape=jax.ShapeDtypeStruct((n_rows, dp), jnp.uint32),
        compiler_params=_cparams("arbitrary"),
        name="moe_experts",
    )(tile_expert, n_used, run_index, next_expert, xs, w_gu, w_dn)


def sc_gather_rows(table, idx):
    n_idx = idx.shape[0]
    width = table.shape[1]
    info = plsc.get_sparse_core_info()
    n_workers = info.num_cores * info.num_subcores
    per_worker = n_idx // n_workers
    chunk = SC_GATHER_CHUNK
    assert n_idx % (n_workers * chunk) == 0 and width % info.num_lanes == 0
    mesh = plsc.VectorSubcoreMesh(core_axis_name="c", subcore_axis_name="s")

    def body(table_hbm, idx_hbm, out_hbm, idx_v, rows_v, sem):
        wid = lax.axis_index("s") * info.num_cores + lax.axis_index("c")
        base = wid * per_worker

        @pl.loop(0, per_worker // chunk)
        def _(j):
            off = pl.multiple_of(base + j * chunk, chunk)
            pltpu.sync_copy(idx_hbm.at[pl.ds(off, chunk)], idx_v)
            pltpu.async_copy(table_hbm.at[idx_v], rows_v, sem).wait()
            pltpu.sync_copy(rows_v, out_hbm.at[pl.ds(off, chunk)])

    return pl.kernel(
        body,
        out_type=jax.ShapeDtypeStruct((n_idx, width), table.dtype),
        mesh=mesh,
        scratch_types=[pltpu.VMEM((chunk,), jnp.int32), pltpu.VMEM((chunk, width), table.dtype),
                       pltpu.SemaphoreType.DMA],
        name="sc_gather_rows",
    )(table, idx)


def sc_dispatch_rows(rows, dest, pad_idx, zero_rows):
    m, width = rows.shape
    n_pad = pad_idx.shape[0]
    info = plsc.get_sparse_core_info()
    n_workers = info.num_cores * info.num_subcores
    chunk = SC_GATHER_CHUNK
    assert m % (n_workers * chunk) == 0 and n_pad % (n_workers * chunk) == 0 and zero_rows.shape == (chunk, width)
    rows_per_worker = m // n_workers
    pad_per_worker = n_pad // n_workers
    mesh = plsc.VectorSubcoreMesh(core_axis_name="c", subcore_axis_name="s")

    def body(rows_hbm, dest_hbm, pad_hbm, zero_hbm, out_hbm, idx_v, rows_v, sem):
        wid = lax.axis_index("s") * info.num_cores + lax.axis_index("c")

        @pl.loop(0, rows_per_worker // chunk)
        def _(j):
            off = pl.multiple_of(wid * rows_per_worker + j * chunk, chunk)
            pltpu.sync_copy(rows_hbm.at[pl.ds(off, chunk)], rows_v)
            for slot in range(2):
                pltpu.sync_copy(dest_hbm.at[pl.ds(slot * m + off, chunk)], idx_v)
                pltpu.async_copy(rows_v, out_hbm.at[idx_v], sem).wait()

        pltpu.sync_copy(zero_hbm, rows_v)

        @pl.loop(0, pad_per_worker // chunk)
        def _(j):
            off = pl.multiple_of(wid * pad_per_worker + j * chunk, chunk)
            pltpu.sync_copy(pad_hbm.at[pl.ds(off, chunk)], idx_v)
            pltpu.async_copy(rows_v, out_hbm.at[idx_v], sem).wait()

    return pl.kernel(
        body,
        out_type=jax.ShapeDtypeStruct((2 * m + n_pad, width), rows.dtype),
        mesh=mesh,
        scratch_types=[pltpu.VMEM((chunk,), jnp.int32), pltpu.VMEM((chunk, width), rows.dtype),
                       pltpu.SemaphoreType.DMA],
        name="sc_dispatch_rows",
    )(rows, dest, pad_idx, zero_rows)


def _combine_rows_kernel(x_ref, r_ref, g_ref, y1_ref, y2_ref, o_ref, *, final_norm):
    r = r_ref[...]
    out = x_ref[...] + r[:, 2:3] * _unpack_bf16_pairs(y1_ref[...]) + r[:, 3:4] * _unpack_bf16_pairs(y2_ref[...])
    if final_norm:
        ms = jnp.mean(out * out, axis=-1, keepdims=True)
        out = (out * lax.rsqrt(ms + EPS)) * g_ref[...]
    o_ref[...] = out


def moe_combine_rows(x2, route, y_rows, g_final, tm, final_norm):
    m, d = x2.shape
    steps = m // tm
    return pl.pallas_call(
        functools.partial(_combine_rows_kernel, final_norm=final_norm),
        grid=(steps,),
        in_specs=[pl.BlockSpec((tm, d), lambda i: (i, 0)),
                  pl.BlockSpec((tm, SMALL_W), lambda i: (i, 0)),
                  pl.BlockSpec((1, d), lambda i: (0, 0)),
                  pl.BlockSpec((tm, d // 2), lambda i: (i, 0)),
                  pl.BlockSpec((tm, d // 2), lambda i: (i + steps, 0))],
        out_specs=pl.BlockSpec((tm, d), lambda i: (i, 0)),
        out_shape=jax.ShapeDtypeStruct((m, d), F32),
        compiler_params=_cparams("parallel"),
        name="moe_combine_rows",
    )(x2, route, g_final.reshape(1, d), y_rows, y_rows)


def moe_routed(x2, h, route, counts, w_gu, w_dn, layer, g_final, final_norm):
    m, d = x2.shape
    te_rows = EXPERT_TILE
    n_rows = 2 * m + N_EXPERTS * te_rows
    cnt = counts[0, 0:N_EXPERTS].astype(jnp.int32)
    padded = (cnt + te_rows - 1) // te_rows * te_rows
    seg_end = jnp.cumsum(padded)
    seg_start = seg_end - padded
    idx = route[:, 0:2].astype(jnp.int32)
    onehot = idx[:, :, None] == jnp.arange(N_EXPERTS, dtype=jnp.int32)
    dest = (jnp.sum(jnp.where(onehot, seg_start, 0), axis=-1) + route[:, 4:6].astype(jnp.int32)).T.reshape(2 * m)
    tile_start = jnp.arange(n_rows // te_rows, dtype=jnp.int32) * te_rows
    tile_expert = jnp.minimum(jnp.sum((tile_start[:, None] >= seg_end[None, :]).astype(jnp.int32), axis=1), N_EXPERTS - 1)
    n_used = (seg_end[N_EXPERTS - 1] // te_rows).reshape(1)
    n_pad = n_rows - 2 * m
    experts = jnp.arange(N_EXPERTS, dtype=jnp.int32)
    pad_end = jnp.cumsum(padded - cnt)
    k = jnp.arange(n_pad, dtype=jnp.int32)
    k_expert = jnp.sum((k[:, None] >= pad_end[None, :]).astype(jnp.int32), axis=1)
    first_pad = seg_start + cnt - (pad_end - (padded - cnt))
    in_segment = jnp.sum(jnp.where(k_expert[:, None] == experts[None, :], first_pad[None, :], 0), axis=1) + k
    pad_idx = jnp.where(k_expert < N_EXPERTS, in_segment, seg_end[N_EXPERTS - 1] + k - pad_end[N_EXPERTS - 1])
    xs = sc_dispatch_rows(h, dest, pad_idx, jnp.zeros((SC_GATHER_CHUNK, d // 2), h.dtype))
    n_tiles = n_rows // te_rows
    tile_ids = jnp.arange(n_tiles, dtype=jnp.int32)
    prev_expert = jnp.concatenate([jnp.full((1,), -1, jnp.int32), tile_expert[:-1]])
    run_index = jnp.maximum(jnp.cumsum(((tile_expert != prev_expert) & (tile_ids < n_used[0])).astype(jnp.int32)) - 1, 0)
    next_tile = jnp.take(seg_end, tile_expert) // te_rows
    next_expert = jnp.where(next_tile < n_used[0], jnp.take(tile_expert, jnp.minimum(next_tile, n_tiles - 1)), -1)
    ys = moe_experts(xs, tile_expert, n_used, run_index, next_expert, w_gu, w_dn, layer, tm=te_rows)
    y_rows = sc_gather_rows(ys, dest)
    return moe_combine_rows(x2, route, y_rows, g_final, tm=ROW_TILE, final_norm=final_norm)


def _w_in_prep_kernel(w_ref, o_ref):
    gw = GROUP_WIDTH
    a_end = 7 * gw
    c_start = a_end + HEADS
    c_end = c_start + 5 * gw
    z_start = c_end + 2 * HEADS
    assert a_end % SMALL_W == LANE_FOX and c_end % SMALL_W == LANE_BETA and LANE_GDEC == LANE_BETA + HEADS
    w = w_ref[...]
    o_ref[:, 0:a_end] = w[:, 0:a_end].astype(BF16)
    o_ref[:, a_end:a_end + 5 * gw] = w[:, c_start:c_end].astype(BF16)
    o_ref[:, 12 * gw:13 * gw] = w[:, z_start:z_start + gw].astype(BF16)
    blk_fox = w[:, a_end:a_end + SMALL_W]
    blk_gdn = w[:, c_end - LANE_BETA:c_end - LANE_BETA + SMALL_W]
    lane = _iota(blk_fox.shape, 1)
    small = jnp.where(lane < LANE_BETA, blk_fox, jnp.where(lane < LANE_GDEC + HEADS, blk_gdn, 0.0))
    o_ref[:, 13 * gw:13 * gw + SMALL_W] = small.astype(BF16)
    o_ref[:, 13 * gw + SMALL_W:] = jnp.zeros((w.shape[0], SMALL_W), BF16)


def w_in_prep(w_all, layer):
    _, d, n = w_all.shape
    tr = 128
    return pl.pallas_call(
        _w_in_prep_kernel,
        grid=(d // tr,),
        in_specs=[pl.BlockSpec((None, tr, n), lambda i: (layer, i, 0))],
        out_specs=pl.BlockSpec((tr, N_PROJ), lambda i: (i, 0)),
        out_shape=jax.ShapeDtypeStruct((d, N_PROJ), BF16),
        compiler_params=_cparams("parallel"),
        name="w_in_prep",
    )(w_all)


def _block_diag(w):
    h = w.shape[0]
    eye = jnp.eye(h, dtype=w.dtype)
    return (eye[:, None, :, None] * w[:, :, None, :]).reshape(h * HEAD_DIM, h * HEAD_DIM)


def kernel(x, mem, norm_mix, w_in, hgrn_lb, fox_fb, lru_conv_w, lru_conv_b, lru_wa, lru_ba, lru_wx, lru_bx, lru_lam, gdn_conv_w, gdn_a_log, gdn_dt_bias, mix_gain, w_out, norm_mem, norm_memkv, w_mq, w_mkv, w_mo, norm_ffn, w_rg, b_rg, w_re, b_re, w_e_gu, w_e_dn, norm_final):
    b, s, d = x.shape
    depth = w_in.shape[0]
    t_tok = b * s
    mlen = mem.shape[1]
    gw = GROUP_WIDTH

    lb_all = jnp.cumsum(jax.nn.softmax(hgrn_lb.astype(F32), axis=0), axis=0)
    lb_all = lb_all - lb_all[0]

    x = x.astype(F32)
    for l in range(depth):
        proj = norm_matmul(x.reshape(t_tok, d), norm_mix[l], w_in_prep(w_in, l), tm=ROW_TILE)
        proj3 = proj.reshape(b, s, N_PROJ)
        prm = jnp.zeros((8, SMALL_W), F32)
        prm = prm.at[0, LANE_FOX:LANE_FOX + HEADS].set(fox_fb[l].astype(F32))
        prm = prm.at[0, LANE_GDEC:LANE_GDEC + HEADS].set(gdn_dt_bias[l].astype(F32))
        prm = prm.at[1, LANE_GDEC:LANE_GDEC + HEADS].set(gdn_a_log[l].astype(F32))
        sm, c_exp = small_prep(proj3, prm)
        g_row = sm[:, :, LANE_GDEC:LANE_GDEC + HEADS].reshape(b, s // GDN_CHUNK, GDN_CHUNK, HEADS)
        g_row = jnp.swapaxes(g_row, 2, 3).reshape(b, s // GDN_CHUNK, 1, gw)
        g_row = jnp.broadcast_to(g_row, (b, s // GDN_CHUNK, 8, gw))

        ya = hgrn2(proj3, lb_all[l])
        yb = fox_attention(proj3, c_exp)
        yc = rglru(proj3, lru_conv_w[l], lru_conv_b[l], _block_diag(lru_wa[l]).astype(BF16), lru_ba[l],
                   _block_diag(lru_wx[l]).astype(BF16), lru_bx[l], lru_lam[l])
        yd = gdn(proj3, sm, g_row, gdn_conv_w[l])

        kv = norm_matmul(mem.reshape(b * mlen, d), norm_memkv[l], w_mkv[l].astype(BF16), tm=256)
        w_router = jnp.concatenate([w_re[l], w_rg[l], jnp.zeros((d, SMALL_W - N_EXPERTS - N_EXPERT_GROUPS), F32)], axis=1)
        b_router = jnp.concatenate([b_re[l], b_rg[l], jnp.zeros((SMALL_W - N_EXPERTS - N_EXPERT_GROUPS,), F32)]).reshape(1, SMALL_W)
        x, h, route, counts = post_mix(x, ya, yb, yc, yd, mix_gain[l], w_out[l].astype(BF16), norm_mem[l],
                                       w_mq[l].astype(BF16), kv.reshape(b, mlen, 2 * gw), w_mo[l].astype(BF16),
                                       norm_ffn[l], w_router, b_router)

        x = moe_routed(x.reshape(t_tok, d), h, route, counts, w_e_gu, w_e_dn, l,
                       norm_final, final_norm=(l == depth - 1)).reshape(b, s, d)

    return x
```

```python
import functools
import math

import jax
import jax.numpy as jnp
from jax import lax
from jax.experimental import pallas as pl
from jax.experimental.pallas import tpu as pltpu
from jax.experimental.pallas import tpu_sc as plsc

F32 = jnp.float32
BF16 = jnp.bfloat16

HEAD_DIM = 64
GROUP_WIDTH = 256
HEADS = GROUP_WIDTH // HEAD_DIM
GDN_CHUNK = 64
GDN_TILE = 128
GDN_BATCH = 4
FOX_BATCH = 2
HGRN_BATCH = 4
HGRN_CHUNK = 16
CONV_W = 4
LRU_C = 8.0
EPS = 1e-6
N_EXPERT_GROUPS = 4
EXPERTS_PER_GROUP = 8
N_EXPERTS = N_EXPERT_GROUPS * EXPERTS_PER_GROUP
D_EXPERT = 256
SC_GATHER_CHUNK = 128
EXPERT_TILE = 512
SMALL_W = 128
SEQ_TILE = 256
ROW_TILE = 512
VMEM_LIMIT = 56 * 1024 * 1024

(COL_AQ, COL_AF, COL_AI, COL_AG, COL_BQ, COL_BK, COL_BV, COL_CX, COL_CG,
 COL_DQ, COL_DK, COL_DV, COL_DZ) = range(13)
N_PROJ = 13 * GROUP_WIDTH + 2 * SMALL_W
COL_SMALL = 13 * GROUP_WIDTH // SMALL_W
LANE_FOX = 0
LANE_BETA = 4
LANE_GDEC = 8


def _cparams(*sem):
    return pltpu.CompilerParams(dimension_semantics=sem, vmem_limit_bytes=VMEM_LIMIT)


def _dot(a, b):
    return jnp.dot(a, b, preferred_element_type=F32)


def _dot_nt(a, b):
    return lax.dot_general(a, b, (((1,), (1,)), ((), ())), preferred_element_type=F32)


def _dot_tn(a, b):
    return lax.dot_general(a, b, (((0,), (0,)), ((), ())), preferred_element_type=F32)


def _split3(x):
    h = x.astype(BF16)
    r = x - h.astype(F32)
    m = r.astype(BF16)
    l = (r - m.astype(F32)).astype(BF16)
    return h, m, l


def _dot_exact_rhs(x, w_bf16):
    h, m, l = _split3(x)
    return _dot(h, w_bf16) + _dot(m, w_bf16) + _dot(l, w_bf16)


def _dot_exact_lhs(w_bf16, x):
    h, m, l = _split3(x)
    return _dot(w_bf16, h) + _dot(w_bf16, m) + _dot(w_bf16, l)


def _iota(shape, dim):
    return lax.broadcasted_iota(jnp.int32, shape, dim)


def _head_ones(n=GROUP_WIDTH):
    r = _iota((n, n), 0) // HEAD_DIM
    c = _iota((n, n), 1) // HEAD_DIM
    return r == c


def _sigmoid(x):
    return 1.0 / (1.0 + jnp.exp(-x))


def _silu(x):
    return x * _sigmoid(x)


def _log_sigmoid(x):
    return jnp.minimum(x, 0.0) - jnp.log1p(jnp.exp(-jnp.abs(x)))


def _softplus(x):
    return jnp.maximum(x, 0.0) + jnp.log1p(jnp.exp(-jnp.abs(x)))


def _gelu_tanh(x):
    return 0.5 * x * (1.0 + jnp.tanh(math.sqrt(2.0 / math.pi) * (x + 0.044715 * (x * x * x))))


def _pack_bf16_pairs(x):
    n = x.shape[1] // 2
    u = lax.bitcast_convert_type(x.astype(BF16).astype(F32), jnp.uint32)
    return u[:, :n] | (u[:, n:] >> 16)


def _unpack_bf16_pairs(p):
    hi = lax.bitcast_convert_type(p & jnp.uint32(0xFFFF0000), F32)
    lo = lax.bitcast_convert_type(p << 16, F32)
    return jnp.concatenate([hi, lo], axis=1)


def _head_mean_sq(x, ones_bf16):
    return _dot_exact_rhs(x * x, ones_bf16) * (1.0 / HEAD_DIM)


def _head_rms(x, ones_bf16):
    return x * lax.rsqrt(_head_mean_sq(x, ones_bf16) + EPS)


def _stack_heads(x):
    lane_head = _iota(x.shape, 1) // HEAD_DIM
    parts = []
    for h in range(HEADS):
        parts.append(jnp.where(lane_head == h, x, 0.0))
    return jnp.concatenate(parts, axis=0)


def _causal_conv4(x, pad_ref, w):
    r = x.shape[0]
    pad_ref[8:8 + r, :] = x
    acc = x * w[CONV_W - 1:CONV_W, :]
    for k in range(1, CONV_W):
        acc = acc + pad_ref[8 - k:8 - k + r, :] * w[CONV_W - 1 - k:CONV_W - k, :]
    pad_ref[0:8, :] = x[r - 8:r, :]
    return acc


def _norm_matmul_kernel(x_ref, g_ref, w_ref, o_ref):
    x = x_ref[...]
    ms = jnp.mean(x * x, axis=-1, keepdims=True)
    h = (x * lax.rsqrt(ms + EPS)) * g_ref[...]
    o_ref[...] = _dot(h.astype(BF16), w_ref[...]).astype(o_ref.dtype)


def norm_matmul(x, g, w_bf16, tm, out_dtype=F32):
    m, d = x.shape
    n = w_bf16.shape[1]
    return pl.pallas_call(
        _norm_matmul_kernel,
        grid=(m // tm,),
        in_specs=[pl.BlockSpec((tm, d), lambda i: (i, 0)),
                  pl.BlockSpec((1, d), lambda i: (0, 0)),
                  pl.BlockSpec((d, n), lambda i: (0, 0))],
        out_specs=pl.BlockSpec((tm, n), lambda i: (i, 0)),
        out_shape=jax.ShapeDtypeStruct((m, n), out_dtype),
        compiler_params=_cparams("parallel"),
        name="norm_matmul",
    )(x, g.reshape(1, d), w_bf16)


def _small_prep_kernel(s_ref, p_ref, o_ref, x_ref):
    S = s_ref.shape[1]
    W = s_ref.shape[2]
    blk = SEQ_TILE
    lane = _iota((blk, W), 1)
    is_fox = (lane >= LANE_FOX) & (lane < LANE_FOX + HEADS)
    is_beta = (lane >= LANE_BETA) & (lane < LANE_BETA + HEADS)
    is_gdec = (lane >= LANE_GDEC) & (lane < LANE_GDEC + HEADS)
    r = _iota((blk, blk), 0)
    c = _iota((blk, blk), 1)
    tril_all = jnp.where(r >= c, 1.0, 0.0).astype(BF16)
    tril_chunk = jnp.where((r >= c) & (r // GDN_CHUNK == c // GDN_CHUNK), 1.0, 0.0).astype(BF16)
    neg_exp_a = -jnp.exp(p_ref[1:2, :])
    carry = jnp.zeros((1, W), F32)
    for i in range(S // blk):
        sl = pl.ds(i * blk, blk)
        z = s_ref[0, sl, :] + p_ref[0:1, :]
        fox = jnp.where(is_fox, _log_sigmoid(z), 0.0)
        beta = jnp.where(is_beta, _sigmoid(z), 0.0)
        gdec = jnp.where(is_gdec, neg_exp_a * _softplus(z), 0.0)
        cf = _dot_exact_lhs(tril_all, fox) + carry
        cg = _dot_exact_lhs(tril_chunk, gdec)
        carry = cf[blk - 1:blk, :]
        o_ref[0, sl, :] = cf + cg + beta
        both = cf + cg
        for j, src_lane in enumerate(range(LANE_FOX, LANE_FOX + HEADS)):
            x_ref[0, sl, j * W:(j + 1) * W] = jnp.broadcast_to(both[:, src_lane:src_lane + 1], (blk, W))


def small_prep(proj3, params):
    b, s, _ = proj3.shape
    w = SMALL_W
    return pl.pallas_call(
        _small_prep_kernel,
        grid=(b,),
        in_specs=[pl.BlockSpec((1, s, w), lambda i: (i, 0, COL_SMALL)),
                  pl.BlockSpec((8, w), lambda i: (0, 0))],
        out_specs=[pl.BlockSpec((1, s, w), lambda i: (i, 0, 0)),
                   pl.BlockSpec((1, s, HEADS * w), lambda i: (i, 0, 0))],
        out_shape=[jax.ShapeDtypeStruct((b, s, w), F32), jax.ShapeDtypeStruct((b, s, HEADS * w), F32)],
        compiler_params=_cparams("parallel"),
        name="small_prep",
    )(proj3, params)


def _hgrn2_kernel(q_ref, f_ref, i_ref, g_ref, lb_ref, o_ref, st_ref, qs, ks, vs, bs, os_):
    nb = q_ref.shape[0]
    T = q_ref.shape[1]
    C = HGRN_CHUNK
    seqs = range(nb)

    @pl.when(pl.program_id(1) == 0)
    def _():
        st_ref[...] = jnp.zeros_like(st_ref)

    same_head = _head_ones()
    ones_bf = jnp.where(same_head, 1.0, 0.0).astype(BF16)
    lb = lb_ref[...]
    log_lb = jnp.log(lb)
    r = _iota((T, T), 0)
    c = _iota((T, T), 1)
    tril_chunk = jnp.where((r >= c) & (r // C == c // C), 1.0, 0.0).astype(BF16)
    for bb in seqs:
        fl = f_ref[bb]
        c2 = jnp.log1p(-lb) + _log_sigmoid(fl)
        mx = jnp.maximum(log_lb, c2)
        log_f = mx + jnp.log1p(jnp.exp(-jnp.abs(log_lb - c2)))
        bs[bb] = _dot_exact_lhs(tril_chunk, log_f)
        qs[bb] = _silu(q_ref[bb])
        ks[bb] = (1.0 - lb) * _sigmoid(-fl)
        vs[bb] = i_ref[bb]

    H8 = C // 2
    trow8 = _iota((H8, GROUP_WIDTH), 0)

    def chunk(ci, carry):
        r0 = pl.multiple_of(ci * C, C)
        qc = [qs[bb, pl.ds(r0, C), :] for bb in seqs]
        kc = [ks[bb, pl.ds(r0, C), :] for bb in seqs]
        vc = [vs[bb, pl.ds(r0, C), :] for bb in seqs]
        bc = [bs[bb, pl.ds(r0, C), :] for bb in seqs]
        st = [st_ref[bb] for bb in seqs]
        a = []
        for bb in seqs:
            parts = []
            for half, s_range in ((0, range(H8)), (1, range(C))):
                b_half = bc[bb][half * H8:(half + 1) * H8, :]
                q_half = qc[bb][half * H8:(half + 1) * H8, :]
                for s in s_range:
                    blk = jnp.exp(b_half - bc[bb][s:s + 1, :]) * (q_half * kc[bb][s:s + 1, :])
                    if s >= half * H8:
                        blk = jnp.where(trow8 >= s - half * H8, blk, 0.0)
                    parts.append(blk)
            a.append(jnp.concatenate(parts, axis=0).astype(BF16))
        sc = [_dot(a[bb], ones_bf) for bb in seqs]
        o = [_dot_nt((qc[bb] * jnp.exp(bc[bb])).astype(BF16), st[bb].astype(BF16)) for bb in seqs]
        b_last = [bc[bb][C - 1:C, :] for bb in seqs]
        upd = [_dot_tn(vc[bb].astype(BF16), (kc[bb] * jnp.exp(b_last[bb] - bc[bb])).astype(BF16)) for bb in seqs]
        for bb in seqs:
            top = o[bb][0:H8, :]
            bot = o[bb][H8:C, :]
            for s in range(H8):
                top = top + sc[bb][s * H8:(s + 1) * H8, :] * vc[bb][s:s + 1, :]
            for s in range(C):
                bot = bot + sc[bb][(H8 + s) * H8:(H8 + s + 1) * H8, :] * vc[bb][s:s + 1, :]
            os_[bb, pl.ds(r0, C), :] = jnp.concatenate([top, bot], axis=0)
            st_ref[bb] = st[bb] * jnp.exp(b_last[bb]) + jnp.where(same_head, upd[bb], 0.0)
        return carry

    lax.fori_loop(0, T // C, chunk, 0)
    for bb in seqs:
        o_ref[bb] = _head_rms(os_[bb], ones_bf) * _silu(g_ref[bb])


def hgrn2(proj3, lb):
    b, s, _ = proj3.shape
    t = SEQ_TILE
    gw = GROUP_WIDTH
    nb = HGRN_BATCH
    col = lambda cidx: pl.BlockSpec((nb, t, gw), lambda i, j: (i, j, cidx))
    return pl.pallas_call(
        _hgrn2_kernel,
        grid=(b // nb, s // t),
        in_specs=[col(COL_AQ), col(COL_AF), col(COL_AI), col(COL_AG),
                  pl.BlockSpec((1, gw), lambda i, j: (0, 0))],
        out_specs=pl.BlockSpec((nb, t, gw), lambda i, j: (i, j, 0)),
        out_shape=jax.ShapeDtypeStruct((b, s, gw), F32),
        scratch_shapes=[pltpu.VMEM((nb, gw, gw), F32)] + [pltpu.VMEM((nb, t, gw), F32)] * 5,
        compiler_params=_cparams("parallel", "arbitrary"),
        name="hgrn2",
    )(proj3, proj3, proj3, proj3, lb.reshape(1, gw))


def _fox_kernel(q_ref, k_ref, v_ref, cx_ref, o_ref, kb_ref, vt_ref, acc_ref):
    nb = q_ref.shape[0]
    tq = q_ref.shape[1]
    tk = tq
    S = k_ref.shape[1]
    qi = pl.program_id(1)
    seqs = range(nb)

    @pl.when(qi == 0)
    def _():
        for bb in seqs:
            for i in range(S // tk):
                sl = pl.ds(i * tk, tk)
                kb_ref[bb, sl, :] = k_ref[bb, sl, :].astype(BF16)
                vt_ref[bb, :, sl] = v_ref[bb, sl, :].T.astype(BF16)

    lane_head = _iota((tq, GROUP_WIDTH), 1) // HEAD_DIM
    qh = []
    for bb in seqs:
        qsc = q_ref[bb] * (HEAD_DIM ** -0.5)
        qh.append([jnp.where(lane_head == h, qsc, 0.0).astype(BF16) for h in range(HEADS)])
    acc_ref[...] = jnp.zeros_like(acc_ref)
    reps = tq // SMALL_W
    causal = _iota((tk, tq), 0) <= _iota((tk, tq), 1)

    def step(kb, stats, masked):
        k0 = pl.multiple_of(kb * tk, tk)
        new_stats = [[None] * HEADS for _ in seqs]
        strips = [(h, bb) for h in range(HEADS) for bb in seqs]

        def scores(h, bb):
            return _dot_nt(kb_ref[bb, pl.ds(k0, tk), :], qh[bb][h])

        ahead = 3
        queue = [scores(*strips[j]) for j in range(ahead)]
        for i, (h, bb) in enumerate(strips):
            st = queue.pop(0)
            if i + ahead < len(strips):
                queue.append(scores(*strips[i + ahead]))
            rows = slice(h * HEAD_DIM, (h + 1) * HEAD_DIM)
            cx = cx_ref[bb, pl.ds(k0, tk), h * SMALL_W:(h + 1) * SMALL_W]
            st = st - jnp.concatenate([cx] * reps, axis=1)
            if masked:
                st = jnp.where(causal, st, -jnp.inf)
            m_old, l_old = stats[bb][h]
            m_new = jnp.maximum(m_old, jnp.max(st, axis=0, keepdims=True))
            alpha = jnp.exp(m_old - m_new)
            p = jnp.exp(st - m_new)
            l_new = alpha * l_old + jnp.sum(p, axis=0, keepdims=True)
            pv = _dot(vt_ref[bb, rows, pl.ds(k0, tk)], p.astype(BF16))
            acc_ref[bb, rows, :] = alpha * acc_ref[bb, rows, :] + pv
            new_stats[bb][h] = (m_new, l_new)
        return tuple(tuple(new_stats[bb]) for bb in seqs)

    init = tuple(tuple((jnp.full((1, tq), -jnp.inf, F32), jnp.zeros((1, tq), F32)) for _ in range(HEADS)) for _ in seqs)
    stats = lax.fori_loop(0, qi, lambda kb, c: step(kb, c, False), init)
    stats = step(qi, stats, True)

    ones_bf = jnp.where(_head_ones(), 1.0, 0.0).astype(BF16)
    for bb in seqs:
        inv_l = jnp.concatenate([jnp.broadcast_to(1.0 / stats[bb][h][1], (HEAD_DIM, tq)) for h in range(HEADS)], axis=0)
        o_ref[bb] = _head_rms((acc_ref[bb] * inv_l).T, ones_bf)


def fox_attention(proj3, c_exp):
    b, s, _ = proj3.shape
    tq = SEQ_TILE
    gw = GROUP_WIDTH
    nb = FOX_BATCH
    return pl.pallas_call(
        _fox_kernel,
        grid=(b // nb, s // tq),
        in_specs=[pl.BlockSpec((nb, tq, gw), lambda i, j: (i, j, COL_BQ)),
                  pl.BlockSpec((nb, s, gw), lambda i, j: (i, 0, COL_BK)),
                  pl.BlockSpec((nb, s, gw), lambda i, j: (i, 0, COL_BV)),
                  pl.BlockSpec((nb, s, HEADS * SMALL_W), lambda i, j: (i, 0, 0))],
        out_specs=pl.BlockSpec((nb, tq, gw), lambda i, j: (i, j, 0)),
        out_shape=jax.ShapeDtypeStruct((b, s, gw), F32),
        scratch_shapes=[pltpu.VMEM((nb, s, gw), BF16),
                        pltpu.VMEM((nb, gw, s), BF16),
                        pltpu.VMEM((nb, gw, tq), F32)],
        compiler_params=_cparams("parallel", "arbitrary"),
        name="fox_attention",
    )(proj3, proj3, proj3, c_exp)


def _rglru_kernel(x_ref, g_ref, cw_ref, cb_ref, wa_ref, ba_ref, wx_ref, bx_ref, lam_ref, o_ref, prev_ref, h_ref):
    @pl.when(pl.program_id(1) == 0)
    def _():
        prev_ref[0:8, :] = jnp.zeros((8, prev_ref.shape[1]), F32)
        h_ref[...] = jnp.zeros_like(h_ref)

    x = x_ref[0]
    t = x.shape[0]
    xc = _causal_conv4(x, prev_ref, cw_ref[...]) + cb_ref[...]
    xb = xc.astype(BF16)
    r = _sigmoid(_dot(xb, wa_ref[...]) + ba_ref[...])
    ig = _sigmoid(_dot(xb, wx_ref[...]) + bx_ref[...])
    log_a = (-LRU_C * r) * _softplus(-lam_ref[...])
    a = jnp.exp(log_a)
    u = jnp.sqrt(1.0 - jnp.exp(2.0 * log_a)) * (ig * xc)
    row = _iota(a.shape, 0)
    d = 1
    while d < t:
        valid = row >= d
        u = jnp.where(valid, a * pltpu.roll(u, d, 0) + u, u)
        a = jnp.where(valid, a * pltpu.roll(a, d, 0), a)
        d *= 2
    h = a * h_ref[...] + u
    h_ref[...] = h[t - 1:t, :]
    ones_bf = jnp.where(_head_ones(), 1.0, 0.0).astype(BF16)
    o_ref[0] = _head_rms(h * _gelu_tanh(g_ref[0]), ones_bf)


def rglru(proj3, conv_w, conv_b, wa_bd, ba, wx_bd, bx, lam):
    b, s, _ = proj3.shape
    t = ROW_TILE
    gw = GROUP_WIDTH
    row = lambda: pl.BlockSpec((1, gw), lambda i, j: (0, 0))
    return pl.pallas_call(
        _rglru_kernel,
        grid=(b, s // t),
        in_specs=[pl.BlockSpec((1, t, gw), lambda i, j: (i, j, COL_CX)),
                  pl.BlockSpec((1, t, gw), lambda i, j: (i, j, COL_CG)),
                  pl.BlockSpec((CONV_W, gw), lambda i, j: (0, 0)), row(),
                  pl.BlockSpec((gw, gw), lambda i, j: (0, 0)), row(),
                  pl.BlockSpec((gw, gw), lambda i, j: (0, 0)), row(), row()],
        out_specs=pl.BlockSpec((1, t, gw), lambda i, j: (i, j, 0)),
        out_shape=jax.ShapeDtypeStruct((b, s, gw), F32),
        scratch_shapes=[pltpu.VMEM((8 + t, gw), F32), pltpu.VMEM((1, gw), F32)],
        compiler_params=_cparams("parallel", "arbitrary"),
        name="rglru",
    )(proj3, proj3, conv_w, conv_b.reshape(1, gw), wa_bd, ba.reshape(1, gw), wx_bd, bx.reshape(1, gw), lam.reshape(1, gw))


def _gdn_kernel(q_ref, k_ref, v_ref, z_ref, sm_ref, gr_ref, cw_ref, o_ref, st_ref, pq_ref, pk_ref, pv_ref):
    nb = q_ref.shape[0]
    T = q_ref.shape[1]
    C = GDN_CHUNK

    @pl.when(pl.program_id(1) == 0)
    def _():
        st_ref[...] = jnp.zeros_like(st_ref)
        for pad_ref in (pq_ref, pk_ref, pv_ref):
            pad_ref[:, 0:8, :] = jnp.zeros((nb, 8, pad_ref.shape[2]), F32)

    same_head = _head_ones()
    ones_bf = jnp.where(same_head, 1.0, 0.0).astype(BF16)
    cw = cw_ref[...]
    gw = GROUP_WIDTH
    er = _iota((SMALL_W, gw), 0)
    ec = _iota((SMALL_W, gw), 1) // HEAD_DIM
    exp_beta = jnp.where(er == LANE_BETA + ec, 1.0, 0.0).astype(BF16)
    exp_g = jnp.where(er == LANE_GDEC + ec, 1.0, 0.0).astype(BF16)

    prep = []
    for bb in range(nb):
        xq, xk, xv = q_ref[bb], k_ref[bb], v_ref[bb]
        q = _silu(_causal_conv4(xq, pq_ref.at[bb], cw[:, 0:gw]))
        k = _silu(_causal_conv4(xk, pk_ref.at[bb], cw[:, gw:2 * gw]))
        v = _silu(_causal_conv4(xv, pv_ref.at[bb], cw[:, 2 * gw:3 * gw]))
        q = q * lax.rsqrt(_dot_exact_rhs(q * q, ones_bf) + EPS) * (HEAD_DIM ** -0.5)
        k = k * lax.rsqrt(_dot_exact_rhs(k * k, ones_bf) + EPS)
        sm = sm_ref[bb]
        betax = _dot_exact_rhs(sm, exp_beta)
        gx = _dot_exact_rhs(sm, exp_g)
        prep.append((q, k, v, betax, gx))

    n_chunks = T // C
    trow = _iota((C, gw), 0)
    scol = _iota((C, gw), 1) % HEAD_DIM
    incl = scol <= trow
    strict = scol < trow
    eye = jnp.where(scol == trow, 1.0, 0.0)

    def bdiag(x_cat):
        return _stack_heads(x_cat).astype(BF16)

    chains = []
    for ci in range(n_chunks):
        sl = slice(ci * C, (ci + 1) * C)
        for bb in range(nb):
            q, k, v, betax, gx = prep[bb]
            qc, kc, vc, bx, gc = q[sl], k[sl], v[sl], betax[sl], gx[sl]
            grow = gr_ref[bb, ci, 0:1, :]
            eg = jnp.exp(gc)
            kb = kc * bx
            g_last = gc[C - 1:C, :]
            gamma = jnp.where(incl, jnp.exp(jnp.where(incl, gc - grow, 0.0)), 0.0)
            ks_bf = bdiag(kc)
            m = jnp.where(strict, _dot_nt(kb.astype(BF16), ks_bf) * gamma, 0.0)
            chains.append(dict(
                x=eye - m, y=m,
                a_qk=(_dot_nt(qc.astype(BF16), ks_bf) * gamma).astype(BF16),
                rhs_u=bdiag(vc * bx),
                rhs_w=bdiag(kb * eg),
                q_dec=(qc * eg).astype(BF16),
                k_dec=(kc * jnp.exp(g_last - gc)).astype(BF16),
                decay=jnp.exp(g_last)))
    for ch in chains:
        ch["ybd"] = bdiag(ch["y"])
    for _ in range(5):
        for ch in chains:
            ch["y"] = _dot(ch["y"].astype(BF16), ch["ybd"])
            ch["ybd"] = bdiag(ch["y"])
        for ch in chains:
            ch["x"] = ch["x"] + _dot(ch["x"].astype(BF16), ch["ybd"])
    for ch in chains:
        t_inv = ch["x"].astype(BF16)
        ch["u"] = _dot(t_inv, ch["rhs_u"])
        ch["wk"] = _dot(t_inv, ch["rhs_w"]).astype(BF16)

    outs = [[] for _ in range(nb)]
    for ci in range(n_chunks):
        for bb in range(nb):
            ch = chains[ci * nb + bb]
            st = st_ref[bb]
            st_bf = st.astype(BF16)
            v_new = ch["u"] - _dot(ch["wk"], st_bf)
            outs[bb].append(_dot(ch["q_dec"], st_bf) + _dot(ch["a_qk"], bdiag(v_new)))
            upd = _dot_tn(ch["k_dec"], v_new.astype(BF16))
            st_ref[bb] = st * ch["decay"] + jnp.where(same_head, upd, 0.0)
    for bb in range(nb):
        o = jnp.concatenate(outs[bb], axis=0)
        o_ref[bb] = _head_rms(o, ones_bf) * _silu(z_ref[bb])


def gdn(proj3, sm, g_row, conv_w):
    b, s, _ = proj3.shape
    t = GDN_TILE
    gw = GROUP_WIDTH
    nb = GDN_BATCH
    col = lambda cidx: pl.BlockSpec((nb, t, gw), lambda i, j: (i, j, cidx))
    return pl.pallas_call(
        _gdn_kernel,
        grid=(b // nb, s // t),
        in_specs=[col(COL_DQ), col(COL_DK), col(COL_DV), col(COL_DZ),
                  pl.BlockSpec((nb, t, SMALL_W), lambda i, j: (i, j, 0)),
                  pl.BlockSpec((nb, t // GDN_CHUNK, 8, gw), lambda i, j: (i, j, 0, 0)),
                  pl.BlockSpec((CONV_W, 3 * gw), lambda i, j: (0, 0))],
        out_specs=pl.BlockSpec((nb, t, gw), lambda i, j: (i, j, 0)),
        out_shape=jax.ShapeDtypeStruct((b, s, gw), F32),
        scratch_shapes=[pltpu.VMEM((nb, gw, gw), F32)] + [pltpu.VMEM((nb, 8 + t, gw), F32)] * 3,
        compiler_params=_cparams("parallel", "arbitrary"),
        name="gdn",
    )(proj3, proj3, proj3, proj3, sm, g_row, conv_w)


def _mix_out(x, ys, gain, w_ref):
    gw = GROUP_WIDTH
    acc = x
    for i, y in enumerate(ys):
        yg = (y * gain[:, i * gw:(i + 1) * gw]).astype(BF16)
        acc = acc + _dot(yg, w_ref[i * gw:(i + 1) * gw, :])
    return acc


def _mem_attention(x, g, wq_ref, kv, wo_ref):
    t = x.shape[0]
    gw = GROUP_WIDTH
    ms = jnp.mean(x * x, axis=-1, keepdims=True)
    h = ((x * lax.rsqrt(ms + EPS)) * g).astype(BF16)
    q = _dot(h, wq_ref[...])
    k = kv[:, 0:gw].astype(BF16)
    v = kv[:, gw:2 * gw].astype(BF16)
    s = _dot_nt(k, _stack_heads(q).astype(BF16)) * (HEAD_DIM ** -0.5)
    s = s - jnp.max(s, axis=0, keepdims=True)
    p = jnp.exp(s)
    p = p * (1.0 / jnp.sum(p, axis=0, keepdims=True))
    ot = _dot_tn(v, p.astype(BF16))
    oc = jnp.concatenate([ot[hh * HEAD_DIM:(hh + 1) * HEAD_DIM, hh * t:(hh + 1) * t] for hh in range(HEADS)], axis=0).T
    return x + _dot(oc.astype(BF16), wo_ref[...])


def _post_mix_kernel(x_ref, ya_ref, yb_ref, yc_ref, yd_ref, gain_ref, wout_ref, gm_ref, wq_ref, kv_ref, wo_ref,
                     g_ref, w_ref, b_ref, x2_ref, h_ref, r_ref, n_ref, cnt_ref):
    @pl.when((pl.program_id(0) == 0) & (pl.program_id(1) == 0))
    def _():
        cnt_ref[...] = jnp.zeros_like(cnt_ref)

    x1 = _mix_out(x_ref[0], (ya_ref[0], yb_ref[0], yc_ref[0], yd_ref[0]), gain_ref[...], wout_ref)
    x = _mem_attention(x1, gm_ref[...], wq_ref, kv_ref[0], wo_ref)
    x2_ref[0] = x
    tm = x.shape[0]
    ms = jnp.mean(x * x, axis=-1, keepdims=True)
    h = (x * lax.rsqrt(ms + EPS)) * g_ref[...]
    h_ref[...] = _pack_bf16_pairs(h)
    hh, hm, _ = _split3(h)
    w = w_ref[...]
    wh, wm, _ = _split3(w)
    logits = _dot(hh, wh) + _dot(hh, wm) + _dot(hm, wh) + b_ref[...]
    lane = _iota(logits.shape, 1)
    big = jnp.int32(1 << 30)
    neg = -jnp.inf
    is_grp = (lane >= N_EXPERTS) & (lane < N_EXPERTS + N_EXPERT_GROUPS)
    gl = jnp.where(is_grp, logits, neg)
    gmax = jnp.max(gl, axis=-1, keepdims=True)
    p_grp = 1.0 / jnp.sum(jnp.exp(gl - gmax), axis=-1, keepdims=True)
    g_sel = jnp.min(jnp.where(gl == gmax, lane, big), axis=-1, keepdims=True) - N_EXPERTS
    in_grp = (lane < N_EXPERTS) & (lane // EXPERTS_PER_GROUP == g_sel)
    el = jnp.where(in_grp, logits, neg)
    m1 = jnp.max(el, axis=-1, keepdims=True)
    i1 = jnp.min(jnp.where(el == m1, lane, big), axis=-1, keepdims=True)
    el2 = jnp.where(lane == i1, neg, el)
    m2 = jnp.max(el2, axis=-1, keepdims=True)
    i2 = jnp.min(jnp.where(el2 == m2, lane, big), axis=-1, keepdims=True)
    e21 = jnp.exp(m2 - m1)
    w1 = 1.0 / (1.0 + e21)
    w2 = e21 / (1.0 + e21)
    oh1 = lane == i1
    oh2 = lane == i2
    onehot = jnp.where(oh1 | oh2, 1.0, 0.0)
    strict = jnp.where(_iota((tm, tm), 0) > _iota((tm, tm), 1), 1.0, 0.0).astype(BF16)
    before = _dot(strict, onehot.astype(BF16)) + cnt_ref[...]
    rank1 = jnp.sum(jnp.where(oh1, before, 0.0), axis=-1, keepdims=True)
    rank2 = jnp.sum(jnp.where(oh2, before, 0.0), axis=-1, keepdims=True)
    cnt = before[tm - 1:tm, :] + onehot[tm - 1:tm, :]
    cnt_ref[...] = cnt
    n_ref[...] = jnp.broadcast_to(cnt, n_ref.shape)
    cols = (i1.astype(F32), i2.astype(F32), p_grp * w1, p_grp * w2, rank1, rank2)
    route = jnp.zeros(logits.shape, F32)
    for j, col in enumerate(cols):
        route = jnp.where(lane == j, col, route)
    r_ref[...] = route


def post_mix(x3, ya, yb, yc, yd, gain, w_out_bf16, g_mem, wq_bf16, kv3, wo_bf16, g_ffn, w_router, b_router):
    b, s, d = x3.shape
    t = ROW_TILE
    gw = GROUP_WIDTH
    nt = s // t
    mlen = kv3.shape[1]
    tile = lambda w: pl.BlockSpec((1, t, w), lambda i, j: (i, j, 0))
    const = lambda shape: pl.BlockSpec(shape, lambda i, j: (0,) * len(shape))
    tok = lambda w: pl.BlockSpec((t, w), lambda i, j: (i * nt + j, 0))
    return pl.pallas_call(
        _post_mix_kernel,
        grid=(b, nt),
        in_specs=[tile(d), tile(gw), tile(gw), tile(gw), tile(gw),
                  const((1, 4 * gw)), const((4 * gw, d)),
                  const((1, d)), const((d, gw)), pl.BlockSpec((1, mlen, 2 * gw), lambda i, j: (i, 0, 0)), const((gw, d)),
                  const((1, d)), const((d, SMALL_W)), const((1, SMALL_W))],
        out_specs=[tile(d), tok(d // 2), tok(SMALL_W), const((8, SMALL_W))],
        out_shape=[jax.ShapeDtypeStruct((b, s, d), F32), jax.ShapeDtypeStruct((b * s, d // 2), jnp.uint32),
                   jax.ShapeDtypeStruct((b * s, SMALL_W), F32), jax.ShapeDtypeStruct((8, SMALL_W), F32)],
        scratch_shapes=[pltpu.VMEM((1, SMALL_W), F32)],
        compiler_params=_cparams("arbitrary", "arbitrary"),
        name="post_mix",
    )(x3, ya, yb, yc, yd, gain.reshape(1, 4 * gw), w_out_bf16, g_mem.reshape(1, d), wq_bf16, kv3, wo_bf16,
      g_ffn.reshape(1, d), w_router, b_router)


def _experts_kernel(te_ref, nu_ref, run_ref, nxt_ref, xs_ref, wgu_hbm, wdn_hbm, y_ref,
                    wgu_buf, wdn_buf, wgu_bf, wdn_bf, sem, *, layer):
    i = pl.program_id(0)

    def weight_copies(expert, slot):
        return (pltpu.make_async_copy(wgu_hbm.at[layer, expert], wgu_buf.at[slot], sem.at[slot, 0]),
                pltpu.make_async_copy(wdn_hbm.at[layer, expert], wdn_buf.at[slot], sem.at[slot, 1]))

    @pl.when(i >= nu_ref[0])
    def _():
        y_ref[...] = jnp.zeros_like(y_ref)

    @pl.when(i < nu_ref[0])
    def _():
        expert = te_ref[i]
        slot = run_ref[i] % 2

        @pl.when(i == 0)
        def _():
            for c in weight_copies(expert, slot):
                c.start()

        @pl.when((i == 0) | (te_ref[jnp.maximum(i - 1, 0)] != expert))
        def _():
            for c in weight_copies(expert, slot):
                c.wait()
            nxt = nxt_ref[i]

            @pl.when(nxt >= 0)
            def _():
                for c in weight_copies(nxt, 1 - slot):
                    c.start()

            wgu_bf[...] = wgu_buf[slot].astype(BF16)
            wdn_bf[...] = wdn_buf[slot].astype(BF16)

        gu = _dot(_unpack_bf16_pairs(xs_ref[...]).astype(BF16), wgu_bf[...])
        act = _silu(gu[:, 0:D_EXPERT]) * gu[:, D_EXPERT:2 * D_EXPERT]
        y_ref[...] = _pack_bf16_pairs(_dot(act.astype(BF16), wdn_bf[...]))


def moe_experts(xs, tile_expert, n_used, run_index, next_expert, w_gu, w_dn, layer, tm):
    n_rows, dp = xs.shape
    d = 2 * dp
    return pl.pallas_call(
        functools.partial(_experts_kernel, layer=layer),
        grid_spec=pltpu.PrefetchScalarGridSpec(
            num_scalar_prefetch=4,
            grid=(n_rows // tm,),
            in_specs=[pl.BlockSpec((tm, dp), lambda i, te, nu, run, nxt: (jnp.minimum(i, nu[0] - 1), 0)),
                      pl.BlockSpec(memory_space=pl.ANY),
                      pl.BlockSpec(memory_space=pl.ANY)],
            out_specs=pl.BlockSpec((tm, dp), lambda i, te, nu, run, nxt: (i, 0)),
            scratch_shapes=[pltpu.VMEM((2, d, 2 * D_EXPERT), F32), pltpu.VMEM((2, D_EXPERT, d), F32),
                            pltpu.VMEM((d, 2 * D_EXPERT), BF16), pltpu.VMEM((D_EXPERT, d), BF16),
                            pltpu.SemaphoreType.DMA((2, 2))]),
        out_shape=jax.ShapeDtypeStruct((n_rows, dp), jnp.uint32),
        compiler_params=_cparams("arbitrary"),
        name="moe_experts",
    )(tile_expert, n_used, run_index, next_expert, xs, w_gu, w_dn)


def sc_gather_rows(table, idx):
    n_idx = idx.shape[0]
    width = table.shape[1]
    info = plsc.get_sparse_core_info()
    n_workers = info.num_cores * info.num_subcores
    per_worker = n_idx // n_workers
    chunk = SC_GATHER_CHUNK
    assert n_idx % (n_workers * chunk) == 0 and width % info.num_lanes == 0
    mesh = plsc.VectorSubcoreMesh(core_axis_name="c", subcore_axis_name="s")

    def body(table_hbm, idx_hbm, out_hbm, idx_v, rows_v, sem):
        wid = lax.axis_index("s") * info.num_cores + lax.axis_index("c")
        base = wid * per_worker

        @pl.loop(0, per_worker // chunk)
        def _(j):
            off = pl.multiple_of(base + j * chunk, chunk)
            pltpu.sync_copy(idx_hbm.at[pl.ds(off, chunk)], idx_v)
            pltpu.async_copy(table_hbm.at[idx_v], rows_v, sem).wait()
            pltpu.sync_copy(rows_v, out_hbm.at[pl.ds(off, chunk)])

    return pl.kernel(
        body,
        out_type=jax.ShapeDtypeStruct((n_idx, width), table.dtype),
        mesh=mesh,
        scratch_types=[pltpu.VMEM((chunk,), jnp.int32), pltpu.VMEM((chunk, width), table.dtype),
                       pltpu.SemaphoreType.DMA],
        name="sc_gather_rows",
    )(table, idx)


def sc_dispatch_rows(rows, dest, pad_idx, zero_rows):
    m, width = rows.shape
    n_pad = pad_idx.shape[0]
    info = plsc.get_sparse_core_info()
    n_workers = info.num_cores * info.num_subcores
    chunk = SC_GATHER_CHUNK
    assert m % (n_workers * chunk) == 0 and n_pad % (n_workers * chunk) == 0 and zero_rows.shape == (chunk, width)
    rows_per_worker = m // n_workers
    pad_per_worker = n_pad // n_workers
    mesh = plsc.VectorSubcoreMesh(core_axis_name="c", subcore_axis_name="s")

    def body(rows_hbm, dest_hbm, pad_hbm, zero_hbm, out_hbm, idx_v, rows_v, sem):
        wid = lax.axis_index("s") * info.num_cores + lax.axis_index("c")

        @pl.loop(0, rows_per_worker // chunk)
        def _(j):
            off = pl.multiple_of(wid * rows_per_worker + j * chunk, chunk)
            pltpu.sync_copy(rows_hbm.at[pl.ds(off, chunk)], rows_v)
            for slot in range(2):
                pltpu.sync_copy(dest_hbm.at[pl.ds(slot * m + off, chunk)], idx_v)
                pltpu.async_copy(rows_v, out_hbm.at[idx_v], sem).wait()

        pltpu.sync_copy(zero_hbm, rows_v)

        @pl.loop(0, pad_per_worker // chunk)
        def _(j):
            off = pl.multiple_of(wid * pad_per_worker + j * chunk, chunk)
            pltpu.sync_copy(pad_hbm.at[pl.ds(off, chunk)], idx_v)
            pltpu.async_copy(rows_v, out_hbm.at[idx_v], sem).wait()

    return pl.kernel(
        body,
        out_type=jax.ShapeDtypeStruct((2 * m + n_pad, width), rows.dtype),
        mesh=mesh,
        scratch_types=[pltpu.VMEM((chunk,), jnp.int32), pltpu.VMEM((chunk, width), rows.dtype),
                       pltpu.SemaphoreType.DMA],
        name="sc_dispatch_rows",
    )(rows, dest, pad_idx, zero_rows)


def _combine_rows_kernel(x_ref, r_ref, g_ref, y1_ref, y2_ref, o_ref, *, final_norm):
    r = r_ref[...]
    out = x_ref[...] + r[:, 2:3] * _unpack_bf16_pairs(y1_ref[...]) + r[:, 3:4] * _unpack_bf16_pairs(y2_ref[...])
    if final_norm:
        ms = jnp.mean(out * out, axis=-1, keepdims=True)
        out = (out * lax.rsqrt(ms + EPS)) * g_ref[...]
    o_ref[...] = out


def moe_combine_rows(x2, route, y_rows, g_final, tm, final_norm):
    m, d = x2.shape
    steps = m // tm
    return pl.pallas_call(
        functools.partial(_combine_rows_kernel, final_norm=final_norm),
        grid=(steps,),
        in_specs=[pl.BlockSpec((tm, d), lambda i: (i, 0)),
                  pl.BlockSpec((tm, SMALL_W), lambda i: (i, 0)),
                  pl.BlockSpec((1, d), lambda i: (0, 0)),
                  pl.BlockSpec((tm, d // 2), lambda i: (i, 0)),
                  pl.BlockSpec((tm, d // 2), lambda i: (i + steps, 0))],
        out_specs=pl.BlockSpec((tm, d), lambda i: (i, 0)),
        out_shape=jax.ShapeDtypeStruct((m, d), F32),
        compiler_params=_cparams("parallel"),
        name="moe_combine_rows",
    )(x2, route, g_final.reshape(1, d), y_rows, y_rows)


def moe_routed(x2, h, route, counts, w_gu, w_dn, layer, g_final, final_norm):
    m, d = x2.shape
    te_rows = EXPERT_TILE
    n_rows = 2 * m + N_EXPERTS * te_rows
    cnt = counts[0, 0:N_EXPERTS].astype(jnp.int32)
    padded = (cnt + te_rows - 1) // te_rows * te_rows
    seg_end = jnp.cumsum(padded)
    seg_start = seg_end - padded
    idx = route[:, 0:2].astype(jnp.int32)
    onehot = idx[:, :, None] == jnp.arange(N_EXPERTS, dtype=jnp.int32)
    dest = (jnp.sum(jnp.where(onehot, seg_start, 0), axis=-1) + route[:, 4:6].astype(jnp.int32)).T.reshape(2 * m)
    tile_start = jnp.arange(n_rows // te_rows, dtype=jnp.int32) * te_rows
    tile_expert = jnp.minimum(jnp.sum((tile_start[:, None] >= seg_end[None, :]).astype(jnp.int32), axis=1), N_EXPERTS - 1)
    n_used = (seg_end[N_EXPERTS - 1] // te_rows).reshape(1)
    n_pad = n_rows - 2 * m
    experts = jnp.arange(N_EXPERTS, dtype=jnp.int32)
    pad_end = jnp.cumsum(padded - cnt)
    k = jnp.arange(n_pad, dtype=jnp.int32)
    k_expert = jnp.sum((k[:, None] >= pad_end[None, :]).astype(jnp.int32), axis=1)
    first_pad = seg_start + cnt - (pad_end - (padded - cnt))
    in_segment = jnp.sum(jnp.where(k_expert[:, None] == experts[None, :], first_pad[None, :], 0), axis=1) + k
    pad_idx = jnp.where(k_expert < N_EXPERTS, in_segment, seg_end[N_EXPERTS - 1] + k - pad_end[N_EXPERTS - 1])
    xs = sc_dispatch_rows(h, dest, pad_idx, jnp.zeros((SC_GATHER_CHUNK, d // 2), h.dtype))
    n_tiles = n_rows // te_rows
    tile_ids = jnp.arange(n_tiles, dtype=jnp.int32)
    prev_expert = jnp.concatenate([jnp.full((1,), -1, jnp.int32), tile_expert[:-1]])
    run_index = jnp.maximum(jnp.cumsum(((tile_expert != prev_expert) & (tile_ids < n_used[0])).astype(jnp.int32)) - 1, 0)
    next_tile = jnp.take(seg_end, tile_expert) // te_rows
    next_expert = jnp.where(next_tile < n_used[0], jnp.take(tile_expert, jnp.minimum(next_tile, n_tiles - 1)), -1)
    ys = moe_experts(xs, tile_expert, n_used, run_index, next_expert, w_gu, w_dn, layer, tm=te_rows)
    y_rows = sc_gather_rows(ys, dest)
    return moe_combine_rows(x2, route, y_rows, g_final, tm=ROW_TILE, final_norm=final_norm)


def _w_in_prep_kernel(w_ref, o_ref):
    gw = GROUP_WIDTH
    a_end = 7 * gw
    c_start = a_end + HEADS
    c_end = c_start + 5 * gw
    z_start = c_end + 2 * HEADS
    assert a_end % SMALL_W == LANE_FOX and c_end % SMALL_W == LANE_BETA and LANE_GDEC == LANE_BETA + HEADS
    w = w_ref[...]
    o_ref[:, 0:a_end] = w[:, 0:a_end].astype(BF16)
    o_ref[:, a_end:a_end + 5 * gw] = w[:, c_start:c_end].astype(BF16)
    o_ref[:, 12 * gw:13 * gw] = w[:, z_start:z_start + gw].astype(BF16)
    blk_fox = w[:, a_end:a_end + SMALL_W]
    blk_gdn = w[:, c_end - LANE_BETA:c_end - LANE_BETA + SMALL_W]
    lane = _iota(blk_fox.shape, 1)
    small = jnp.where(lane < LANE_BETA, blk_fox, jnp.where(lane < LANE_GDEC + HEADS, blk_gdn, 0.0))
    o_ref[:, 13 * gw:13 * gw + SMALL_W] = small.astype(BF16)
    o_ref[:, 13 * gw + SMALL_W:] = jnp.zeros((w.shape[0], SMALL_W), BF16)


def w_in_prep(w_all, layer):
    _, d, n = w_all.shape
    tr = 128
    return pl.pallas_call(
        _w_in_prep_kernel,
        grid=(d // tr,),
        in_specs=[pl.BlockSpec((None, tr, n), lambda i: (layer, i, 0))],
        out_specs=pl.BlockSpec((tr, N_PROJ), lambda i: (i, 0)),
        out_shape=jax.ShapeDtypeStruct((d, N_PROJ), BF16),
        compiler_params=_cparams("parallel"),
        name="w_in_prep",
    )(w_all)


def _block_diag(w):
    h = w.shape[0]
    eye = jnp.eye(h, dtype=w.dtype)
    return (eye[:, None, :, None] * w[:, :, None, :]).reshape(h * HEAD_DIM, h * HEAD_DIM)


def kernel(x, mem, norm_mix, w_in, hgrn_lb, fox_fb, lru_conv_w, lru_conv_b, lru_wa, lru_ba, lru_wx, lru_bx, lru_lam, gdn_conv_w, gdn_a_log, gdn_dt_bias, mix_gain, w_out, norm_mem, norm_memkv, w_mq, w_mkv, w_mo, norm_ffn, w_rg, b_rg, w_re, b_re, w_e_gu, w_e_dn, norm_final):
    b, s, d = x.shape
    depth = w_in.shape[0]
    t_tok = b * s
    mlen = mem.shape[1]
    gw = GROUP_WIDTH

    lb_all = jnp.cumsum(jax.nn.softmax(hgrn_lb.astype(F32), axis=0), axis=0)
    lb_all = lb_all - lb_all[0]

    x = x.astype(F32)
    for l in range(depth):
        proj = norm_matmul(x.reshape(t_tok, d), norm_mix[l], w_in_prep(w_in, l), tm=ROW_TILE)
        proj3 = proj.reshape(b, s, N_PROJ)
        prm = jnp.zeros((8, SMALL_W), F32)
        prm = prm.at[0, LANE_FOX:LANE_FOX + HEADS].set(fox_fb[l].astype(F32))
        prm = prm.at[0, LANE_GDEC:LANE_GDEC + HEADS].set(gdn_dt_bias[l].astype(F32))
        prm = prm.at[1, LANE_GDEC:LANE_GDEC + HEADS].set(gdn_a_log[l].astype(F32))
        sm, c_exp = small_prep(proj3, prm)
        g_row = sm[:, :, LANE_GDEC:LANE_GDEC + HEADS].reshape(b, s // GDN_CHUNK, GDN_CHUNK, HEADS)
        g_row = jnp.swapaxes(g_row, 2, 3).reshape(b, s // GDN_CHUNK, 1, gw)
        g_row = jnp.broadcast_to(g_row, (b, s // GDN_CHUNK, 8, gw))

        ya = hgrn2(proj3, lb_all[l])
        yb = fox_attention(proj3, c_exp)
        yc = rglru(proj3, lru_conv_w[l], lru_conv_b[l], _block_diag(lru_wa[l]).astype(BF16), lru_ba[l],
                   _block_diag(lru_wx[l]).astype(BF16), lru_bx[l], lru_lam[l])
        yd = gdn(proj3, sm, g_row, gdn_conv_w[l])

        kv = norm_matmul(mem.reshape(b * mlen, d), norm_memkv[l], w_mkv[l].astype(BF16), tm=256)
        w_router = jnp.concatenate([w_re[l], w_rg[l], jnp.zeros((d, SMALL_W - N_EXPERTS - N_EXPERT_GROUPS), F32)], axis=1)
        b_router = jnp.concatenate([b_re[l], b_rg[l], jnp.zeros((SMALL_W - N_EXPERTS - N_EXPERT_GROUPS,), F32)]).reshape(1, SMALL_W)
        x, h, route, counts = post_mix(x, ya, yb, yc, yd, mix_gain[l], w_out[l].astype(BF16), norm_mem[l],
                                       w_mq[l].astype(BF16), kv.reshape(b, mlen, 2 * gw), w_mo[l].astype(BF16),
                                       norm_ffn[l], w_router, b_router)

        x = moe_routed(x.reshape(t_tok, d), h, route, counts, w_e_gu, w_e_dn, l,
                       norm_final, final_norm=(l == depth - 1)).reshape(b, s, d)

    return x
```

```python
import functools
import math

import jax
import jax.numpy as jnp
from jax import lax
from jax.experimental import pallas as pl
from jax.experimental.pallas import tpu as pltpu
from jax.experimental.pallas import tpu_sc as plsc

F32 = jnp.float32
BF16 = jnp.bfloat16

HEAD_DIM = 64
GROUP_WIDTH = 256
HEADS = GROUP_WIDTH // HEAD_DIM
GDN_CHUNK = 64
GDN_TILE = 128
GDN_BATCH = 4
FOX_BATCH = 2
HGRN_BATCH = 4
HGRN_CHUNK = 16
CONV_W = 4
LRU_C = 8.0
EPS = 1e-6
N_EXPERT_GROUPS = 4
EXPERTS_PER_GROUP = 8
N_EXPERTS = N_EXPERT_GROUPS * EXPERTS_PER_GROUP
D_EXPERT = 256
SC_GATHER_CHUNK = 128
EXPERT_TILE = 512
SMALL_W = 128
SEQ_TILE = 256
ROW_TILE = 512
VMEM_LIMIT = 56 * 1024 * 1024

(COL_AQ, COL_AF, COL_AI, COL_AG, COL_BQ, COL_BK, COL_BV, COL_CX, COL_CG,
 COL_DQ, COL_DK, COL_DV, COL_DZ) = range(13)
N_PROJ = 13 * GROUP_WIDTH + 2 * SMALL_W
COL_SMALL = 13 * GROUP_WIDTH // SMALL_W
LANE_FOX = 0
LANE_BETA = 4
LANE_GDEC = 8


def _cparams(*sem):
    return pltpu.CompilerParams(dimension_semantics=sem, vmem_limit_bytes=VMEM_LIMIT)


def _dot(a, b):
    return jnp.dot(a, b, preferred_element_type=F32)


def _dot_nt(a, b):
    return lax.dot_general(a, b, (((1,), (1,)), ((), ())), preferred_element_type=F32)


def _dot_tn(a, b):
    return lax.dot_general(a, b, (((0,), (0,)), ((), ())), preferred_element_type=F32)


def _split3(x):
    h = x.astype(BF16)
    r = x - h.astype(F32)
    m = r.astype(BF16)
    l = (r - m.astype(F32)).astype(BF16)
    return h, m, l


def _dot_exact_rhs(x, w_bf16):
    h, m, l = _split3(x)
    return _dot(h, w_bf16) + _dot(m, w_bf16) + _dot(l, w_bf16)


def _dot_exact_lhs(w_bf16, x):
    h, m, l = _split3(x)
    return _dot(w_bf16, h) + _dot(w_bf16, m) + _dot(w_bf16, l)


def _iota(shape, dim):
    return lax.broadcasted_iota(jnp.int32, shape, dim)


def _head_ones(n=GROUP_WIDTH):
    r = _iota((n, n), 0) // HEAD_DIM
    c = _iota((n, n), 1) // HEAD_DIM
    return r == c


def _sigmoid(x):
    return 1.0 / (1.0 + jnp.exp(-x))


def _silu(x):
    return x * _sigmoid(x)


def _log_sigmoid(x):
    return jnp.minimum(x, 0.0) - jnp.log1p(jnp.exp(-jnp.abs(x)))


def _softplus(x):
    return jnp.maximum(x, 0.0) + jnp.log1p(jnp.exp(-jnp.abs(x)))


def _gelu_tanh(x):
    return 0.5 * x * (1.0 + jnp.tanh(math.sqrt(2.0 / math.pi) * (x + 0.044715 * (x * x * x))))


def _pack_bf16_pairs(x):
    n = x.shape[1] // 2
    u = lax.bitcast_convert_type(x.astype(BF16).astype(F32), jnp.uint32)
    return u[:, :n] | (u[:, n:] >> 16)


def _unpack_bf16_pairs(p):
    hi = lax.bitcast_convert_type(p & jnp.uint32(0xFFFF0000), F32)
    lo = lax.bitcast_convert_type(p << 16, F32)
    return jnp.concatenate([hi, lo], axis=1)


def _head_mean_sq(x, ones_bf16):
    return _dot_exact_rhs(x * x, ones_bf16) * (1.0 / HEAD_DIM)


def _head_rms(x, ones_bf16):
    return x * lax.rsqrt(_head_mean_sq(x, ones_bf16) + EPS)


def _stack_heads(x):
    lane_head = _iota(x.shape, 1) // HEAD_DIM
    parts = []
    for h in range(HEADS):
        parts.append(jnp.where(lane_head == h, x, 0.0))
    return jnp.concatenate(parts, axis=0)


def _causal_conv4(x, pad_ref, w):
    r = x.shape[0]
    pad_ref[8:8 + r, :] = x
    acc = x * w[CONV_W - 1:CONV_W, :]
    for k in range(1, CONV_W):
        acc = acc + pad_ref[8 - k:8 - k + r, :] * w[CONV_W - 1 - k:CONV_W - k, :]
    pad_ref[0:8, :] = x[r - 8:r, :]
    return acc


def _norm_matmul_kernel(x_ref, g_ref, w_ref, o_ref):
    x = x_ref[...]
    ms = jnp.mean(x * x, axis=-1, keepdims=True)
    h = (x * lax.rsqrt(ms + EPS)) * g_ref[...]
    o_ref[...] = _dot(h.astype(BF16), w_ref[...]).astype(o_ref.dtype)


def norm_matmul(x, g, w_bf16, tm, out_dtype=F32):
    m, d = x.shape
    n = w_bf16.shape[1]
    return pl.pallas_call(
        _norm_matmul_kernel,
        grid=(m // tm,),
        in_specs=[pl.BlockSpec((tm, d), lambda i: (i, 0)),
                  pl.BlockSpec((1, d), lambda i: (0, 0)),
                  pl.BlockSpec((d, n), lambda i: (0, 0))],
        out_specs=pl.BlockSpec((tm, n), lambda i: (i, 0)),
        out_shape=jax.ShapeDtypeStruct((m, n), out_dtype),
        compiler_params=_cparams("parallel"),
        name="norm_matmul",
    )(x, g.reshape(1, d), w_bf16)


def _small_prep_kernel(s_ref, p_ref, o_ref, x_ref):
    S = s_ref.shape[1]
    W = s_ref.shape[2]
    blk = SEQ_TILE
    lane = _iota((blk, W), 1)
    is_fox = (lane >= LANE_FOX) & (lane < LANE_FOX + HEADS)
    is_beta = (lane >= LANE_BETA) & (lane < LANE_BETA + HEADS)
    is_gdec = (lane >= LANE_GDEC) & (lane < LANE_GDEC + HEADS)
    r = _iota((blk, blk), 0)
    c = _iota((blk, blk), 1)
    tril_all = jnp.where(r >= c, 1.0, 0.0).astype(BF16)
    tril_chunk = jnp.where((r >= c) & (r // GDN_CHUNK == c // GDN_CHUNK), 1.0, 0.0).astype(BF16)
    neg_exp_a = -jnp.exp(p_ref[1:2, :])
    carry = jnp.zeros((1, W), F32)
    for i in range(S // blk):
        sl = pl.ds(i * blk, blk)
        z = s_ref[0, sl, :] + p_ref[0:1, :]
        fox = jnp.where(is_fox, _log_sigmoid(z), 0.0)
        beta = jnp.where(is_beta, _sigmoid(z), 0.0)
        gdec = jnp.where(is_gdec, neg_exp_a * _softplus(z), 0.0)
        cf = _dot_exact_lhs(tril_all, fox) + carry
        cg = _dot_exact_lhs(tril_chunk, gdec)
        carry = cf[blk - 1:blk, :]
        o_ref[0, sl, :] = cf + cg + beta
        both = cf + cg
        for j, src_lane in enumerate(range(LANE_FOX, LANE_FOX + HEADS)):
            x_ref[0, sl, j * W:(j + 1) * W] = jnp.broadcast_to(both[:, src_lane:src_lane + 1], (blk, W))


def small_prep(proj3, params):
    b, s, _ = proj3.shape
    w = SMALL_W
    return pl.pallas_call(
        _small_prep_kernel,
        grid=(b,),
        in_specs=[pl.BlockSpec((1, s, w), lambda i: (i, 0, COL_SMALL)),
                  pl.BlockSpec((8, w), lambda i: (0, 0))],
        out_specs=[pl.BlockSpec((1, s, w), lambda i: (i, 0, 0)),
                   pl.BlockSpec((1, s, HEADS * w), lambda i: (i, 0, 0))],
        out_shape=[jax.ShapeDtypeStruct((b, s, w), F32), jax.ShapeDtypeStruct((b, s, HEADS * w), F32)],
        compiler_params=_cparams("parallel"),
        name="small_prep",
    )(proj3, params)


def _hgrn2_kernel(q_ref, f_ref, i_ref, g_ref, lb_ref, o_ref, st_ref, qs, ks, vs, bs, os_):
    nb = q_ref.shape[0]
    T = q_ref.shape[1]
    C = HGRN_CHUNK
    seqs = range(nb)

    @pl.when(pl.program_id(1) == 0)
    def _():
        st_ref[...] = jnp.zeros_like(st_ref)

    same_head = _head_ones()
    ones_bf = jnp.where(same_head, 1.0, 0.0).astype(BF16)
    lb = lb_ref[...]
    log_lb = jnp.log(lb)
    r = _iota((T, T), 0)
    c = _iota((T, T), 1)
    tril_chunk = jnp.where((r >= c) & (r // C == c // C), 1.0, 0.0).astype(BF16)
    for bb in seqs:
        fl = f_ref[bb]
        c2 = jnp.log1p(-lb) + _log_sigmoid(fl)
        mx = jnp.maximum(log_lb, c2)
        log_f = mx + jnp.log1p(jnp.exp(-jnp.abs(log_lb - c2)))
        bs[bb] = _dot_exact_lhs(tril_chunk, log_f)
        qs[bb] = _silu(q_ref[bb])
        ks[bb] = (1.0 - lb) * _sigmoid(-fl)
        vs[bb] = i_ref[bb]

    H8 = C // 2
    trow8 = _iota((H8, GROUP_WIDTH), 0)

    def chunk(ci, carry):
        r0 = pl.multiple_of(ci * C, C)
        qc = [qs[bb, pl.ds(r0, C), :] for bb in seqs]
        kc = [ks[bb, pl.ds(r0, C), :] for bb in seqs]
        vc = [vs[bb, pl.ds(r0, C), :] for bb in seqs]
        bc = [bs[bb, pl.ds(r0, C), :] for bb in seqs]
        st = [st_ref[bb] for bb in seqs]
        a = []
        for bb in seqs:
            parts = []
            for half, s_range in ((0, range(H8)), (1, range(C))):
                b_half = bc[bb][half * H8:(half + 1) * H8, :]
                q_half = qc[bb][half * H8:(half + 1) * H8, :]
                for s in s_range:
                    blk = jnp.exp(b_half - bc[bb][s:s + 1, :]) * (q_half * kc[bb][s:s + 1, :])
                    if s >= half * H8:
                        blk = jnp.where(trow8 >= s - half * H8, blk, 0.0)
                    parts.append(blk)
            a.append(jnp.concatenate(parts, axis=0).astype(BF16))
        sc = [_dot(a[bb], ones_bf) for bb in seqs]
        o = [_dot_nt((qc[bb] * jnp.exp(bc[bb])).astype(BF16), st[bb].astype(BF16)) for bb in seqs]
        b_last = [bc[bb][C - 1:C, :] for bb in seqs]
        upd = [_dot_tn(vc[bb].astype(BF16), (kc[bb] * jnp.exp(b_last[bb] - bc[bb])).astype(BF16)) for bb in seqs]
        for bb in seqs:
            top = o[bb][0:H8, :]
            bot = o[bb][H8:C, :]
            for s in range(H8):
                top = top + sc[bb][s * H8:(s + 1) * H8, :] * vc[bb][s:s + 1, :]
            for s in range(C):
                bot = bot + sc[bb][(H8 + s) * H8:(H8 + s + 1) * H8, :] * vc[bb][s:s + 1, :]
            os_[bb, pl.ds(r0, C), :] = jnp.concatenate([top, bot], axis=0)
            st_ref[bb] = st[bb] * jnp.exp(b_last[bb]) + jnp.where(same_head, upd[bb], 0.0)
        return carry

    lax.fori_loop(0, T // C, chunk, 0)
    for bb in seqs:
        o_ref[bb] = _head_rms(os_[bb], ones_bf) * _silu(g_ref[bb])


def hgrn2(proj3, lb):
    b, s, _ = proj3.shape
    t = SEQ_TILE
    gw = GROUP_WIDTH
    nb = HGRN_BATCH
    col = lambda cidx: pl.BlockSpec((nb, t, gw), lambda i, j: (i, j, cidx))
    return pl.pallas_call(
        _hgrn2_kernel,
        grid=(b // nb, s // t),
        in_specs=[col(COL_AQ), col(COL_AF), col(COL_AI), col(COL_AG),
                  pl.BlockSpec((1, gw), lambda i, j: (0, 0))],
        out_specs=pl.BlockSpec((nb, t, gw), lambda i, j: (i, j, 0)),
        out_shape=jax.ShapeDtypeStruct((b, s, gw), F32),
        scratch_shapes=[pltpu.VMEM((nb, gw, gw), F32)] + [pltpu.VMEM((nb, t, gw), F32)] * 5,
        compiler_params=_cparams("parallel", "arbitrary"),
        name="hgrn2",
    )(proj3, proj3, proj3, proj3, lb.reshape(1, gw))


def _fox_kernel(q_ref, k_ref, v_ref, cx_ref, o_ref, kb_ref, vt_ref, acc_ref):
    nb = q_ref.shape[0]
    tq = q_ref.shape[1]
    tk = tq
    S = k_ref.shape[1]
    qi = pl.program_id(1)
    seqs = range(nb)

    @pl.when(qi == 0)
    def _():
        for bb in seqs:
            for i in range(S // tk):
                sl = pl.ds(i * tk, tk)
                kb_ref[bb, sl, :] = k_ref[bb, sl, :].astype(BF16)
                vt_ref[bb, :, sl] = v_ref[bb, sl, :].T.astype(BF16)

    lane_head = _iota((tq, GROUP_WIDTH), 1) // HEAD_DIM
    qh = []
    for bb in seqs:
        qsc = q_ref[bb] * (HEAD_DIM ** -0.5)
        qh.append([jnp.where(lane_head == h, qsc, 0.0).astype(BF16) for h in range(HEADS)])
    acc_ref[...] = jnp.zeros_like(acc_ref)
    reps = tq // SMALL_W
    causal = _iota((tk, tq), 0) <= _iota((tk, tq), 1)

    def step(kb, stats, masked):
        k0 = pl.multiple_of(kb * tk, tk)
        new_stats = [[None] * HEADS for _ in seqs]
        strips = [(h, bb) for h in range(HEADS) for bb in seqs]

        def scores(h, bb):
            return _dot_nt(kb_ref[bb, pl.ds(k0, tk), :], qh[bb][h])

        ahead = 3
        queue = [scores(*strips[j]) for j in range(ahead)]
        for i, (h, bb) in enumerate(strips):
            st = queue.pop(0)
            if i + ahead < len(strips):
                queue.append(scores(*strips[i + ahead]))
            rows = slice(h * HEAD_DIM, (h + 1) * HEAD_DIM)
            cx = cx_ref[bb, pl.ds(k0, tk), h * SMALL_W:(h + 1) * SMALL_W]
            st = st - jnp.concatenate([cx] * reps, axis=1)
            if masked:
                st = jnp.where(causal, st, -jnp.inf)
            m_old, l_old = stats[bb][h]
            m_new = jnp.maximum(m_old, jnp.max(st, axis=0, keepdims=True))
            alpha = jnp.exp(m_old - m_new)
            p = jnp.exp(st - m_new)
            l_new = alpha * l_old + jnp.sum(p, axis=0, keepdims=True)
            pv = _dot(vt_ref[bb, rows, pl.ds(k0, tk)], p.astype(BF16))
            acc_ref[bb, rows, :] = alpha * acc_ref[bb, rows, :] + pv
            new_stats[bb][h] = (m_new, l_new)
        return tuple(tuple(new_stats[bb]) for bb in seqs)

    init = tuple(tuple((jnp.full((1, tq), -jnp.inf, F32), jnp.zeros((1, tq), F32)) for _ in range(HEADS)) for _ in seqs)
    stats = lax.fori_loop(0, qi, lambda kb, c: step(kb, c, False), init)
    stats = step(qi, stats, True)

    ones_bf = jnp.where(_head_ones(), 1.0, 0.0).astype(BF16)
    for bb in seqs:
        inv_l = jnp.concatenate([jnp.broadcast_to(1.0 / stats[bb][h][1], (HEAD_DIM, tq)) for h in range(HEADS)], axis=0)
        o_ref[bb] = _head_rms((acc_ref[bb] * inv_l).T, ones_bf)


def fox_attention(proj3, c_exp):
    b, s, _ = proj3.shape
    tq = SEQ_TILE
    gw = GROUP_WIDTH
    nb = FOX_BATCH
    return pl.pallas_call(
        _fox_kernel,
        grid=(b // nb, s // tq),
        in_specs=[pl.BlockSpec((nb, tq, gw), lambda i, j: (i, j, COL_BQ)),
                  pl.BlockSpec((nb, s, gw), lambda i, j: (i, 0, COL_BK)),
                  pl.BlockSpec((nb, s, gw), lambda i, j: (i, 0, COL_BV)),
                  pl.BlockSpec((nb, s, HEADS * SMALL_W), lambda i, j: (i, 0, 0))],
        out_specs=pl.BlockSpec((nb, tq, gw), lambda i, j: (i, j, 0)),
        out_shape=jax.ShapeDtypeStruct((b, s, gw), F32),
        scratch_shapes=[pltpu.VMEM((nb, s, gw), BF16),
                        pltpu.VMEM((nb, gw, s), BF16),
                        pltpu.VMEM((nb, gw, tq), F32)],
        compiler_params=_cparams("parallel", "arbitrary"),
        name="fox_attention",
    )(proj3, proj3, proj3, c_exp)


def _rglru_kernel(x_ref, g_ref, cw_ref, cb_ref, wa_ref, ba_ref, wx_ref, bx_ref, lam_ref, o_ref, prev_ref, h_ref):
    @pl.when(pl.program_id(1) == 0)
    def _():
        prev_ref[0:8, :] = jnp.zeros((8, prev_ref.shape[1]), F32)
        h_ref[...] = jnp.zeros_like(h_ref)

    x = x_ref[0]
    t = x.shape[0]
    xc = _causal_conv4(x, prev_ref, cw_ref[...]) + cb_ref[...]
    xb = xc.astype(BF16)
    r = _sigmoid(_dot(xb, wa_ref[...]) + ba_ref[...])
    ig = _sigmoid(_dot(xb, wx_ref[...]) + bx_ref[...])
    log_a = (-LRU_C * r) * _softplus(-lam_ref[...])
    a = jnp.exp(log_a)
    u = jnp.sqrt(1.0 - jnp.exp(2.0 * log_a)) * (ig * xc)
    row = _iota(a.shape, 0)
    d = 1
    while d < t:
        valid = row >= d
        u = jnp.where(valid, a * pltpu.roll(u, d, 0) + u, u)
        a = jnp.where(valid, a * pltpu.roll(a, d, 0), a)
        d *= 2
    h = a * h_ref[...] + u
    h_ref[...] = h[t - 1:t, :]
    ones_bf = jnp.where(_head_ones(), 1.0, 0.0).astype(BF16)
    o_ref[0] = _head_rms(h * _gelu_tanh(g_ref[0]), ones_bf)


def rglru(proj3, conv_w, conv_b, wa_bd, ba, wx_bd, bx, lam):
    b, s, _ = proj3.shape
    t = ROW_TILE
    gw = GROUP_WIDTH
    row = lambda: pl.BlockSpec((1, gw), lambda i, j: (0, 0))
    return pl.pallas_call(
        _rglru_kernel,
        grid=(b, s // t),
        in_specs=[pl.BlockSpec((1, t, gw), lambda i, j: (i, j, COL_CX)),
                  pl.BlockSpec((1, t, gw), lambda i, j: (i, j, COL_CG)),
                  pl.BlockSpec((CONV_W, gw), lambda i, j: (0, 0)), row(),
                  pl.BlockSpec((gw, gw), lambda i, j: (0, 0)), row(),
                  pl.BlockSpec((gw, gw), lambda i, j: (0, 0)), row(), row()],
        out_specs=pl.BlockSpec((1, t, gw), lambda i, j: (i, j, 0)),
        out_shape=jax.ShapeDtypeStruct((b, s, gw), F32),
        scratch_shapes=[pltpu.VMEM((8 + t, gw), F32), pltpu.VMEM((1, gw), F32)],
        compiler_params=_cparams("parallel", "arbitrary"),
        name="rglru",
    )(proj3, proj3, conv_w, conv_b.reshape(1, gw), wa_bd, ba.reshape(1, gw), wx_bd, bx.reshape(1, gw), lam.reshape(1, gw))


def _gdn_kernel(q_ref, k_ref, v_ref, z_ref, sm_ref, gr_ref, cw_ref, o_ref, st_ref, pq_ref, pk_ref, pv_ref):
    nb = q_ref.shape[0]
    T = q_ref.shape[1]
    C = GDN_CHUNK

    @pl.when(pl.program_id(1) == 0)
    def _():
        st_ref[...] = jnp.zeros_like(st_ref)
        for pad_ref in (pq_ref, pk_ref, pv_ref):
            pad_ref[:, 0:8, :] = jnp.zeros((nb, 8, pad_ref.shape[2]), F32)

    same_head = _head_ones()
    ones_bf = jnp.where(same_head, 1.0, 0.0).astype(BF16)
    cw = cw_ref[...]
    gw = GROUP_WIDTH
    er = _iota((SMALL_W, gw), 0)
    ec = _iota((SMALL_W, gw), 1) // HEAD_DIM
    exp_beta = jnp.where(er == LANE_BETA + ec, 1.0, 0.0).astype(BF16)
    exp_g = jnp.where(er == LANE_GDEC + ec, 1.0, 0.0).astype(BF16)

    prep = []
    for bb in range(nb):
        xq, xk, xv = q_ref[bb], k_ref[bb], v_ref[bb]
        q = _silu(_causal_conv4(xq, pq_ref.at[bb], cw[:, 0:gw]))
        k = _silu(_causal_conv4(xk, pk_ref.at[bb], cw[:, gw:2 * gw]))
        v = _silu(_causal_conv4(xv, pv_ref.at[bb], cw[:, 2 * gw:3 * gw]))
        q = q * lax.rsqrt(_dot_exact_rhs(q * q, ones_bf) + EPS) * (HEAD_DIM ** -0.5)
        k = k * lax.rsqrt(_dot_exact_rhs(k * k, ones_bf) + EPS)
        sm = sm_ref[bb]
        betax = _dot_exact_rhs(sm, exp_beta)
        gx = _dot_exact_rhs(sm, exp_g)
        prep.append((q, k, v, betax, gx))

    n_chunks = T // C
    trow = _iota((C, gw), 0)
    scol = _iota((C, gw), 1) % HEAD_DIM
    incl = scol <= trow
    strict = scol < trow
    eye = jnp.where(scol == trow, 1.0, 0.0)

    def bdiag(x_cat):
        return _stack_heads(x_cat).astype(BF16)

    chains = []
    for ci in range(n_chunks):
        sl = slice(ci * C, (ci + 1) * C)
        for bb in range(nb):
            q, k, v, betax, gx = prep[bb]
            qc, kc, vc, bx, gc = q[sl], k[sl], v[sl], betax[sl], gx[sl]
            grow = gr_ref[bb, ci, 0:1, :]
            eg = jnp.exp(gc)
            kb = kc * bx
            g_last = gc[C - 1:C, :]
            gamma = jnp.where(incl, jnp.exp(jnp.where(incl, gc - grow, 0.0)), 0.0)
            ks_bf = bdiag(kc)
            m = jnp.where(strict, _dot_nt(kb.astype(BF16), ks_bf) * gamma, 0.0)
            chains.append(dict(
                x=eye - m, y=m,
                a_qk=(_dot_nt(qc.astype(BF16), ks_bf) * gamma).astype(BF16),
                rhs_u=bdiag(vc * bx),
                rhs_w=bdiag(kb * eg),
                q_dec=(qc * eg).astype(BF16),
                k_dec=(kc * jnp.exp(g_last - gc)).astype(BF16),
                decay=jnp.exp(g_last)))
    for ch in chains:
        ch["ybd"] = bdiag(ch["y"])
    for _ in range(5):
        for ch in chains:
            ch["y"] = _dot(ch["y"].astype(BF16), ch["ybd"])
            ch["ybd"] = bdiag(ch["y"])
        for ch in chains:
            ch["x"] = ch["x"] + _dot(ch["x"].astype(BF16), ch["ybd"])
    for ch in chains:
        t_inv = ch["x"].astype(BF16)
        ch["u"] = _dot(t_inv, ch["rhs_u"])
        ch["wk"] = _dot(t_inv, ch["rhs_w"]).astype(BF16)

    outs = [[] for _ in range(nb)]
    for ci in range(n_chunks):
        for bb in range(nb):
            ch = chains[ci * nb + bb]
            st = st_ref[bb]
            st_bf = st.astype(BF16)
            v_new = ch["u"] - _dot(ch["wk"], st_bf)
            outs[bb].append(_dot(ch["q_dec"], st_bf) + _dot(ch["a_qk"], bdiag(v_new)))
            upd = _dot_tn(ch["k_dec"], v_new.astype(BF16))
            st_ref[bb] = st * ch["decay"] + jnp.where(same_head, upd, 0.0)
    for bb in range(nb):
        o = jnp.concatenate(outs[bb], axis=0)
        o_ref[bb] = _head_rms(o, ones_bf) * _silu(z_ref[bb])


def gdn(proj3, sm, g_row, conv_w):
    b, s, _ = proj3.shape
    t = GDN_TILE
    gw = GROUP_WIDTH
    nb = GDN_BATCH
    col = lambda cidx: pl.BlockSpec((nb, t, gw), lambda i, j: (i, j, cidx))
    return pl.pallas_call(
        _gdn_kernel,
        grid=(b // nb, s // t),
        in_specs=[col(COL_DQ), col(COL_DK), col(COL_DV), col(COL_DZ),
                  pl.BlockSpec((nb, t, SMALL_W), lambda i, j: (i, j, 0)),
                  pl.BlockSpec((nb, t // GDN_CHUNK, 8, gw), lambda i, j: (i, j, 0, 0)),
                  pl.BlockSpec((CONV_W, 3 * gw), lambda i, j: (0, 0))],
        out_specs=pl.BlockSpec((nb, t, gw), lambda i, j: (i, j, 0)),
        out_shape=jax.ShapeDtypeStruct((b, s, gw), F32),
        scratch_shapes=[pltpu.VMEM((nb, gw, gw), F32)] + [pltpu.VMEM((nb, 8 + t, gw), F32)] * 3,
        compiler_params=_cparams("parallel", "arbitrary"),
        name="gdn",
    )(proj3, proj3, proj3, proj3, sm, g_row, conv_w)


def _mix_out(x, ys, gain, w_ref):
    gw = GROUP_WIDTH
    acc = x
    for i, y in enumerate(ys):
        yg = (y * gain[:, i * gw:(i + 1) * gw]).astype(BF16)
        acc = acc + _dot(yg, w_ref[i * gw:(i + 1) * gw, :])
    return acc


def _mem_attention(x, g, wq_ref, kv, wo_ref):
    t = x.shape[0]
    gw = GROUP_WIDTH
    ms = jnp.mean(x * x, axis=-1, keepdims=True)
    h = ((x * lax.rsqrt(ms + EPS)) * g).astype(BF16)
    q = _dot(h, wq_ref[...])
    k = kv[:, 0:gw].astype(BF16)
    v = kv[:, gw:2 * gw].astype(BF16)
    s = _dot_nt(k, _stack_heads(q).astype(BF16)) * (HEAD_DIM ** -0.5)
    s = s - jnp.max(s, axis=0, keepdims=True)
    p = jnp.exp(s)
    p = p * (1.0 / jnp.sum(p, axis=0, keepdims=True))
    ot = _dot_tn(v, p.astype(BF16))
    oc = jnp.concatenate([ot[hh * HEAD_DIM:(hh + 1) * HEAD_DIM, hh * t:(hh + 1) * t] for hh in range(HEADS)], axis=0).T
    return x + _dot(oc.astype(BF16), wo_ref[...])


def _post_mix_kernel(x_ref, ya_ref, yb_ref, yc_ref, yd_ref, gain_ref, wout_ref, gm_ref, wq_ref, kv_ref, wo_ref,
                     g_ref, w_ref, b_ref, x2_ref, h_ref, r_ref, n_ref, cnt_ref):
    @pl.when((pl.program_id(0) == 0) & (pl.program_id(1) == 0))
    def _():
        cnt_ref[...] = jnp.zeros_like(cnt_ref)

    x1 = _mix_out(x_ref[0], (ya_ref[0], yb_ref[0], yc_ref[0], yd_ref[0]), gain_ref[...], wout_ref)
    x = _mem_attention(x1, gm_ref[...], wq_ref, kv_ref[0], wo_ref)
    x2_ref[0] = x
    tm = x.shape[0]
    ms = jnp.mean(x * x, axis=-1, keepdims=True)
    h = (x * lax.rsqrt(ms + EPS)) * g_ref[...]
    h_ref[...] = _pack_bf16_pairs(h)
    hh, hm, _ = _split3(h)
    w = w_ref[...]
    wh, wm, _ = _split3(w)
    logits = _dot(hh, wh) + _dot(hh, wm) + _dot(hm, wh) + b_ref[...]
    lane_i = _iota(logits.shape, 1)
    lane = lane_i.astype(F32)
    lane_grp = (lane_i // EXPERTS_PER_GROUP).astype(F32)
    big = jnp.float32(1e9)
    neg = -jnp.inf
    is_grp = (lane_i >= N_EXPERTS) & (lane_i < N_EXPERTS + N_EXPERT_GROUPS)
    gl = jnp.where(is_grp, logits, neg)
    gmax = jnp.max(gl, axis=-1, keepdims=True)
    p_grp = 1.0 / jnp.sum(jnp.exp(gl - gmax), axis=-1, keepdims=True)
    g_sel = jnp.min(jnp.where(gl == gmax, lane, big), axis=-1, keepdims=True) - N_EXPERTS
    in_grp = (lane_i < N_EXPERTS) & (lane_grp == g_sel)
    el = jnp.where(in_grp, logits, neg)
    m1 = jnp.max(el, axis=-1, keepdims=True)
    i1 = jnp.min(jnp.where(el == m1, lane, big), axis=-1, keepdims=True)
    el2 = jnp.where(lane == i1, neg, el)
    m2 = jnp.max(el2, axis=-1, keepdims=True)
    i2 = jnp.min(jnp.where(el2 == m2, lane, big), axis=-1, keepdims=True)
    e21 = jnp.exp(m2 - m1)
    w1 = 1.0 / (1.0 + e21)
    w2 = e21 / (1.0 + e21)
    oh1 = lane == i1
    oh2 = lane == i2
    onehot = jnp.where(oh1 | oh2, 1.0, 0.0)
    strict = jnp.where(_iota((tm, tm), 0) > _iota((tm, tm), 1), 1.0, 0.0).astype(BF16)
    before = _dot(strict, onehot.astype(BF16)) + cnt_ref[...]
    rank1 = jnp.sum(jnp.where(oh1, before, 0.0), axis=-1, keepdims=True)
    rank2 = jnp.sum(jnp.where(oh2, before, 0.0), axis=-1, keepdims=True)
    cnt = before[tm - 1:tm, :] + onehot[tm - 1:tm, :]
    cnt_ref[...] = cnt
    n_ref[...] = jnp.broadcast_to(cnt, n_ref.shape)
    cols = (i1, i2, p_grp * w1, p_grp * w2, rank1, rank2)
    route = jnp.zeros(logits.shape, F32)
    for j, col in enumerate(cols):
        route = jnp.where(lane_i == j, col, route)
    r_ref[...] = route


def post_mix(x3, ya, yb, yc, yd, gain, w_out_bf16, g_mem, wq_bf16, kv3, wo_bf16, g_ffn, w_router, b_router):
    b, s, d = x3.shape
    t = ROW_TILE
    gw = GROUP_WIDTH
    nt = s // t
    mlen = kv3.shape[1]
    tile = lambda w: pl.BlockSpec((1, t, w), lambda i, j: (i, j, 0))
    const = lambda shape: pl.BlockSpec(shape, lambda i, j: (0,) * len(shape))
    tok = lambda w: pl.BlockSpec((t, w), lambda i, j: (i * nt + j, 0))
    return pl.pallas_call(
        _post_mix_kernel,
        grid=(b, nt),
        in_specs=[tile(d), tile(gw), tile(gw), tile(gw), tile(gw),
                  const((1, 4 * gw)), const((4 * gw, d)),
                  const((1, d)), const((d, gw)), pl.BlockSpec((1, mlen, 2 * gw), lambda i, j: (i, 0, 0)), const((gw, d)),
                  const((1, d)), const((d, SMALL_W)), const((1, SMALL_W))],
        out_specs=[tile(d), tok(d // 2), tok(SMALL_W), const((8, SMALL_W))],
        out_shape=[jax.ShapeDtypeStruct((b, s, d), F32), jax.ShapeDtypeStruct((b * s, d // 2), jnp.uint32),
                   jax.ShapeDtypeStruct((b * s, SMALL_W), F32), jax.ShapeDtypeStruct((8, SMALL_W), F32)],
        scratch_shapes=[pltpu.VMEM((1, SMALL_W), F32)],
        compiler_params=_cparams("arbitrary", "arbitrary"),
        name="post_mix",
    )(x3, ya, yb, yc, yd, gain.reshape(1, 4 * gw), w_out_bf16, g_mem.reshape(1, d), wq_bf16, kv3, wo_bf16,
      g_ffn.reshape(1, d), w_router, b_router)


def _experts_kernel(te_ref, nu_ref, run_ref, nxt_ref, xs_ref, wgu_hbm, wdn_hbm, y_ref,
                    wgu_buf, wdn_buf, wgu_bf, wdn_bf, sem, *, layer):
    i = pl.program_id(0)

    def weight_copies(expert, slot):
        return (pltpu.make_async_copy(wgu_hbm.at[layer, expert], wgu_buf.at[slot], sem.at[slot, 0]),
                pltpu.make_async_copy(wdn_hbm.at[layer, expert], wdn_buf.at[slot], sem.at[slot, 1]))

    @pl.when(i >= nu_ref[0])
    def _():
        y_ref[...] = jnp.zeros_like(y_ref)

    @pl.when(i < nu_ref[0])
    def _():
        expert = te_ref[i]
        slot = run_ref[i] % 2

        @pl.when(i == 0)
        def _():
            for c in weight_copies(expert, slot):
                c.start()

        @pl.when((i == 0) | (te_ref[jnp.maximum(i - 1, 0)] != expert))
        def _():
            for c in weight_copies(expert, slot):
                c.wait()
            nxt = nxt_ref[i]

            @pl.when(nxt >= 0)
            def _():
                for c in weight_copies(nxt, 1 - slot):
                    c.start()

            wgu_bf[...] = wgu_buf[slot].astype(BF16)
            wdn_bf[...] = wdn_buf[slot].astype(BF16)

        gu = _dot(_unpack_bf16_pairs(xs_ref[...]).astype(BF16), wgu_bf[...])
        act = _silu(gu[:, 0:D_EXPERT]) * gu[:, D_EXPERT:2 * D_EXPERT]
        y_ref[...] = _pack_bf16_pairs(_dot(act.astype(BF16), wdn_bf[...]))


def moe_experts(xs, tile_expert, n_used, run_index, next_expert, w_gu, w_dn, layer, tm):
    n_rows, dp = xs.shape
    d = 2 * dp
    return pl.pallas_call(
        functools.partial(_experts_kernel, layer=layer),
        grid_spec=pltpu.PrefetchScalarGridSpec(
            num_scalar_prefetch=4,
            grid=(n_rows // tm,),
            in_specs=[pl.BlockSpec((tm, dp), lambda i, te, nu, run, nxt: (jnp.minimum(i, nu[0] - 1), 0)),
                      pl.BlockSpec(memory_space=pl.ANY),
                      pl.BlockSpec(memory_space=pl.ANY)],
            out_specs=pl.BlockSpec((tm, dp), lambda i, te, nu, run, nxt: (i, 0)),
            scratch_shapes=[pltpu.VMEM((2, d, 2 * D_EXPERT), F32), pltpu.VMEM((2, D_EXPERT, d), F32),
                            pltpu.VMEM((d, 2 * D_EXPERT), BF16), pltpu.VMEM((D_EXPERT, d), BF16),
                            pltpu.SemaphoreType.DMA((2, 2))]),
        out_shape=jax.ShapeDtypeStruct((n_rows, dp), jnp.uint32),
        compiler_params=_cparams("arbitrary"),
        name="moe_experts",
    )(tile_expert, n_used, run_index, next_expert, xs, w_gu, w_dn)


def sc_gather_rows(table, idx):
    n_idx = idx.shape[0]
    width = table.shape[1]
    info = plsc.get_sparse_core_info()
    n_workers = info.num_cores * info.num_subcores
    per_worker = n_idx // n_workers
    chunk = SC_GATHER_CHUNK
    assert n_idx % (n_workers * chunk) == 0 and width % info.num_lanes == 0
    mesh = plsc.VectorSubcoreMesh(core_axis_name="c", subcore_axis_name="s")

    def body(table_hbm, idx_hbm, out_hbm, idx_v, rows_v, sem):
        wid = lax.axis_index("s") * info.num_cores + lax.axis_index("c")
        base = wid * per_worker

        @pl.loop(0, per_worker // chunk)
        def _(j):
            off = pl.multiple_of(base + j * chunk, chunk)
            pltpu.sync_copy(idx_hbm.at[pl.ds(off, chunk)], idx_v)
            pltpu.async_copy(table_hbm.at[idx_v], rows_v, sem).wait()
            pltpu.sync_copy(rows_v, out_hbm.at[pl.ds(off, chunk)])

    return pl.kernel(
        body,
        out_type=jax.ShapeDtypeStruct((n_idx, width), table.dtype),
        mesh=mesh,
        scratch_types=[pltpu.VMEM((chunk,), jnp.int32), pltpu.VMEM((chunk, width), table.dtype),
                       pltpu.SemaphoreType.DMA],
        name="sc_gather_rows",
    )(table, idx)


def sc_dispatch_rows(rows, dest, pad_idx, zero_rows):
    m, width = rows.shape
    n_pad = pad_idx.shape[0]
    info = plsc.get_sparse_core_info()
    n_workers = info.num_cores * info.num_subcores
    chunk = SC_GATHER_CHUNK
    assert m % (n_workers * chunk) == 0 and n_pad % (n_workers * chunk) == 0 and zero_rows.shape == (chunk, width)
    rows_per_worker = m // n_workers
    pad_per_worker = n_pad // n_workers
    mesh = plsc.VectorSubcoreMesh(core_axis_name="c", subcore_axis_name="s")

    def body(rows_hbm, dest_hbm, pad_hbm, zero_hbm, out_hbm, idx_v, rows_v, sem):
        wid = lax.axis_index("s") * info.num_cores + lax.axis_index("c")

        @pl.loop(0, rows_per_worker // chunk)
        def _(j):
            off = pl.multiple_of(wid * rows_per_worker + j * chunk, chunk)
            pltpu.sync_copy(rows_hbm.at[pl.ds(off, chunk)], rows_v)
            for slot in range(2):
                pltpu.sync_copy(dest_hbm.at[pl.ds(slot * m + off, chunk)], idx_v)
                pltpu.async_copy(rows_v, out_hbm.at[idx_v], sem).wait()

        pltpu.sync_copy(zero_hbm, rows_v)

        @pl.loop(0, pad_per_worker // chunk)
        def _(j):
            off = pl.multiple_of(wid * pad_per_worker + j * chunk, chunk)
            pltpu.sync_copy(pad_hbm.at[pl.ds(off, chunk)], idx_v)
            pltpu.async_copy(rows_v, out_hbm.at[idx_v], sem).wait()

    return pl.kernel(
        body,
        out_type=jax.ShapeDtypeStruct((2 * m + n_pad, width), rows.dtype),
        mesh=mesh,
        scratch_types=[pltpu.VMEM((chunk,), jnp.int32), pltpu.VMEM((chunk, width), rows.dtype),
                       pltpu.SemaphoreType.DMA],
        name="sc_dispatch_rows",
    )(rows, dest, pad_idx, zero_rows)


def _combine_rows_kernel(x_ref, r_ref, g_ref, y1_ref, y2_ref, o_ref, *, final_norm):
    r = r_ref[...]
    out = x_ref[...] + r[:, 2:3] * _unpack_bf16_pairs(y1_ref[...]) + r[:, 3:4] * _unpack_bf16_pairs(y2_ref[...])
    if final_norm:
        ms = jnp.mean(out * out, axis=-1, keepdims=True)
        out = (out * lax.rsqrt(ms + EPS)) * g_ref[...]
    o_ref[...] = out


def moe_combine_rows(x2, route, y_rows, g_final, tm, final_norm):
    m, d = x2.shape
    steps = m // tm
    return pl.pallas_call(
        functools.partial(_combine_rows_kernel, final_norm=final_norm),
        grid=(steps,),
        in_specs=[pl.BlockSpec((tm, d), lambda i: (i, 0)),
                  pl.BlockSpec((tm, SMALL_W), lambda i: (i, 0)),
                  pl.BlockSpec((1, d), lambda i: (0, 0)),
                  pl.BlockSpec((tm, d // 2), lambda i: (i, 0)),
                  pl.BlockSpec((tm, d // 2), lambda i: (i + steps, 0))],
        out_specs=pl.BlockSpec((tm, d), lambda i: (i, 0)),
        out_shape=jax.ShapeDtypeStruct((m, d), F32),
        compiler_params=_cparams("parallel"),
        name="moe_combine_rows",
    )(x2, route, g_final.reshape(1, d), y_rows, y_rows)


def moe_routed(x2, h, route, counts, w_gu, w_dn, layer, g_final, final_norm):
    m, d = x2.shape
    te_rows = EXPERT_TILE
    n_rows = 2 * m + N_EXPERTS * te_rows
    cnt = counts[0, 0:N_EXPERTS].astype(jnp.int32)
    padded = (cnt + te_rows - 1) // te_rows * te_rows
    seg_end = jnp.cumsum(padded)
    seg_start = seg_end - padded
    idx = route[:, 0:2].astype(jnp.int32)
    onehot = idx[:, :, None] == jnp.arange(N_EXPERTS, dtype=jnp.int32)
    dest = (jnp.sum(jnp.where(onehot, seg_start, 0), axis=-1) + route[:, 4:6].astype(jnp.int32)).T.reshape(2 * m)
    tile_start = jnp.arange(n_rows // te_rows, dtype=jnp.int32) * te_rows
    tile_expert = jnp.minimum(jnp.sum((tile_start[:, None] >= seg_end[None, :]).astype(jnp.int32), axis=1), N_EXPERTS - 1)
    n_used = (seg_end[N_EXPERTS - 1] // te_rows).reshape(1)
    n_pad = n_rows - 2 * m
    experts = jnp.arange(N_EXPERTS, dtype=jnp.int32)
    pad_end = jnp.cumsum(padded - cnt)
    k = jnp.arange(n_pad, dtype=jnp.int32)
    k_expert = jnp.sum((k[:, None] >= pad_end[None, :]).astype(jnp.int32), axis=1)
    first_pad = seg_start + cnt - (pad_end - (padded - cnt))
    in_segment = jnp.sum(jnp.where(k_expert[:, None] == experts[None, :], first_pad[None, :], 0), axis=1) + k
    pad_idx = jnp.where(k_expert < N_EXPERTS, in_segment, seg_end[N_EXPERTS - 1] + k - pad_end[N_EXPERTS - 1])
    xs = sc_dispatch_rows(h, dest, pad_idx, jnp.zeros((SC_GATHER_CHUNK, d // 2), h.dtype))
    n_tiles = n_rows // te_rows
    tile_ids = jnp.arange(n_tiles, dtype=jnp.int32)
    prev_expert = jnp.concatenate([jnp.full((1,), -1, jnp.int32), tile_expert[:-1]])
    run_index = jnp.maximum(jnp.cumsum(((tile_expert != prev_expert) & (tile_ids < n_used[0])).astype(jnp.int32)) - 1, 0)
    next_tile = jnp.take(seg_end, tile_expert) // te_rows
    next_expert = jnp.where(next_tile < n_used[0], jnp.take(tile_expert, jnp.minimum(next_tile, n_tiles - 1)), -1)
    ys = moe_experts(xs, tile_expert, n_used, run_index, next_expert, w_gu, w_dn, layer, tm=te_rows)
    y_rows = sc_gather_rows(ys, dest)
    return moe_combine_rows(x2, route, y_rows, g_final, tm=ROW_TILE, final_norm=final_norm)


def _w_in_prep_kernel(w_ref, o_ref):
    gw = GROUP_WIDTH
    a_end = 7 * gw
    c_start = a_end + HEADS
    c_end = c_start + 5 * gw
    z_start = c_end + 2 * HEADS
    assert a_end % SMALL_W == LANE_FOX and c_end % SMALL_W == LANE_BETA and LANE_GDEC == LANE_BETA + HEADS
    w = w_ref[...]
    o_ref[:, 0:a_end] = w[:, 0:a_end].astype(BF16)
    o_ref[:, a_end:a_end + 5 * gw] = w[:, c_start:c_end].astype(BF16)
    o_ref[:, 12 * gw:13 * gw] = w[:, z_start:z_start + gw].astype(BF16)
    blk_fox = w[:, a_end:a_end + SMALL_W]
    blk_gdn = w[:, c_end - LANE_BETA:c_end - LANE_BETA + SMALL_W]
    lane = _iota(blk_fox.shape, 1)
    small = jnp.where(lane < LANE_BETA, blk_fox, jnp.where(lane < LANE_GDEC + HEADS, blk_gdn, 0.0))
    o_ref[:, 13 * gw:13 * gw + SMALL_W] = small.astype(BF16)
    o_ref[:, 13 * gw + SMALL_W:] = jnp.zeros((w.shape[0], SMALL_W), BF16)


def w_in_prep(w_all, layer):
    _, d, n = w_all.shape
    tr = 128
    return pl.pallas_call(
        _w_in_prep_kernel,
        grid=(d // tr,),
        in_specs=[pl.BlockSpec((None, tr, n), lambda i: (layer, i, 0))],
        out_specs=pl.BlockSpec((tr, N_PROJ), lambda i: (i, 0)),
        out_shape=jax.ShapeDtypeStruct((d, N_PROJ), BF16),
        compiler_params=_cparams("parallel"),
        name="w_in_prep",
    )(w_all)


def _block_diag(w):
    h = w.shape[0]
    eye = jnp.eye(h, dtype=w.dtype)
    return (eye[:, None, :, None] * w[:, :, None, :]).reshape(h * HEAD_DIM, h * HEAD_DIM)


def kernel(x, mem, norm_mix, w_in, hgrn_lb, fox_fb, lru_conv_w, lru_conv_b, lru_wa, lru_ba, lru_wx, lru_bx, lru_lam, gdn_conv_w, gdn_a_log, gdn_dt_bias, mix_gain, w_out, norm_mem, norm_memkv, w_mq, w_mkv, w_mo, norm_ffn, w_rg, b_rg, w_re, b_re, w_e_gu, w_e_dn, norm_final):
    b, s, d = x.shape
    depth = w_in.shape[0]
    t_tok = b * s
    mlen = mem.shape[1]
    gw = GROUP_WIDTH

    lb_all = jnp.cumsum(jax.nn.softmax(hgrn_lb.astype(F32), axis=0), axis=0)
    lb_all = lb_all - lb_all[0]

    x = x.astype(F32)
    for l in range(depth):
        proj = norm_matmul(x.reshape(t_tok, d), norm_mix[l], w_in_prep(w_in, l), tm=ROW_TILE)
        proj3 = proj.reshape(b, s, N_PROJ)
        prm = jnp.zeros((8, SMALL_W), F32)
        prm = prm.at[0, LANE_FOX:LANE_FOX + HEADS].set(fox_fb[l].astype(F32))
        prm = prm.at[0, LANE_GDEC:LANE_GDEC + HEADS].set(gdn_dt_bias[l].astype(F32))
        prm = prm.at[1, LANE_GDEC:LANE_GDEC + HEADS].set(gdn_a_log[l].astype(F32))
        sm, c_exp = small_prep(proj3, prm)
        g_row = sm[:, :, LANE_GDEC:LANE_GDEC + HEADS].reshape(b, s // GDN_CHUNK, GDN_CHUNK, HEADS)
        g_row = jnp.swapaxes(g_row, 2, 3).reshape(b, s // GDN_CHUNK, 1, gw)
        g_row = jnp.broadcast_to(g_row, (b, s // GDN_CHUNK, 8, gw))

        ya = hgrn2(proj3, lb_all[l])
        yb = fox_attention(proj3, c_exp)
        yc = rglru(proj3, lru_conv_w[l], lru_conv_b[l], _block_diag(lru_wa[l]).astype(BF16), lru_ba[l],
                   _block_diag(lru_wx[l]).astype(BF16), lru_bx[l], lru_lam[l])
        yd = gdn(proj3, sm, g_row, gdn_conv_w[l])

        kv = norm_matmul(mem.reshape(b * mlen, d), norm_memkv[l], w_mkv[l].astype(BF16), tm=256)
        w_router = jnp.concatenate([w_re[l], w_rg[l], jnp.zeros((d, SMALL_W - N_EXPERTS - N_EXPERT_GROUPS), F32)], axis=1)
        b_router = jnp.concatenate([b_re[l], b_rg[l], jnp.zeros((SMALL_W - N_EXPERTS - N_EXPERT_GROUPS,), F32)]).reshape(1, SMALL_W)
        x, h, route, counts = post_mix(x, ya, yb, yc, yd, mix_gain[l], w_out[l].astype(BF16), norm_mem[l],
                                       w_mq[l].astype(BF16), kv.reshape(b, mlen, 2 * gw), w_mo[l].astype(BF16),
                                       norm_ffn[l], w_router, b_router)

        x = moe_routed(x.reshape(t_tok, d), h, route, counts, w_e_gu, w_e_dn, l,
                       norm_final, final_norm=(l == depth - 1)).reshape(b, s, d)

    return x
```

```python
import functools
import math

import jax
import jax.numpy as jnp
from jax import lax
from jax.experimental import pallas as pl
from jax.experimental.pallas import tpu as pltpu
from jax.experimental.pallas import tpu_sc as plsc

F32 = jnp.float32
BF16 = jnp.bfloat16

HEAD_DIM = 64
GROUP_WIDTH = 256
HEADS = GROUP_WIDTH // HEAD_DIM
GDN_CHUNK = 64
GDN_TILE = 128
GDN_BATCH = 4
FOX_BATCH = 2
HGRN_BATCH = 4
HGRN_CHUNK = 16
CONV_W = 4
LRU_C = 8.0
EPS = 1e-6
N_EXPERT_GROUPS = 4
EXPERTS_PER_GROUP = 8
N_EXPERTS = N_EXPERT_GROUPS * EXPERTS_PER_GROUP
D_EXPERT = 256
SC_GATHER_CHUNK = 128
EXPERT_TILE = 512
SMALL_W = 128
SEQ_TILE = 256
ROW_TILE = 512
VMEM_LIMIT = 56 * 1024 * 1024

(COL_AQ, COL_AF, COL_AI, COL_AG, COL_BQ, COL_BK, COL_BV, COL_CX, COL_CG,
 COL_DQ, COL_DK, COL_DV, COL_DZ) = range(13)
N_PROJ = 13 * GROUP_WIDTH + 2 * SMALL_W
COL_SMALL = 13 * GROUP_WIDTH // SMALL_W
LANE_FOX = 0
LANE_BETA = 4
LANE_GDEC = 8


def _cparams(*sem):
    return pltpu.CompilerParams(dimension_semantics=sem, vmem_limit_bytes=VMEM_LIMIT)


def _dot(a, b):
    return jnp.dot(a, b, preferred_element_type=F32)


def _dot_nt(a, b):
    return lax.dot_general(a, b, (((1,), (1,)), ((), ())), preferred_element_type=F32)


def _dot_tn(a, b):
    return lax.dot_general(a, b, (((0,), (0,)), ((), ())), preferred_element_type=F32)


def _split3(x):
    h = x.astype(BF16)
    r = x - h.astype(F32)
    m = r.astype(BF16)
    l = (r - m.astype(F32)).astype(BF16)
    return h, m, l


def _dot_exact_rhs(x, w_bf16):
    h, m, l = _split3(x)
    return _dot(h, w_bf16) + _dot(m, w_bf16) + _dot(l, w_bf16)


def _dot_exact_lhs(w_bf16, x):
    h, m, l = _split3(x)
    return _dot(w_bf16, h) + _dot(w_bf16, m) + _dot(w_bf16, l)


def _iota(shape, dim):
    return lax.broadcasted_iota(jnp.int32, shape, dim)


def _head_ones(n=GROUP_WIDTH):
    r = _iota((n, n), 0) // HEAD_DIM
    c = _iota((n, n), 1) // HEAD_DIM
    return r == c


def _sigmoid(x):
    return 1.0 / (1.0 + jnp.exp(-x))


def _silu(x):
    return x * _sigmoid(x)


def _log_sigmoid(x):
    return jnp.minimum(x, 0.0) - jnp.log1p(jnp.exp(-jnp.abs(x)))


def _softplus(x):
    return jnp.maximum(x, 0.0) + jnp.log1p(jnp.exp(-jnp.abs(x)))


def _gelu_tanh(x):
    return 0.5 * x * (1.0 + jnp.tanh(math.sqrt(2.0 / math.pi) * (x + 0.044715 * (x * x * x))))


def _pack_bf16_pairs(x):
    n = x.shape[1] // 2
    u = lax.bitcast_convert_type(x.astype(BF16).astype(F32), jnp.uint32)
    return u[:, :n] | (u[:, n:] >> 16)


def _unpack_bf16_pairs(p):
    hi = lax.bitcast_convert_type(p & jnp.uint32(0xFFFF0000), F32)
    lo = lax.bitcast_convert_type(p << 16, F32)
    return jnp.concatenate([hi, lo], axis=1)


def _head_mean_sq(x, ones_bf16):
    return _dot_exact_rhs(x * x, ones_bf16) * (1.0 / HEAD_DIM)


def _head_rms(x, ones_bf16):
    return x * lax.rsqrt(_head_mean_sq(x, ones_bf16) + EPS)


def _stack_heads(x):
    lane_head = _iota(x.shape, 1) // HEAD_DIM
    parts = []
    for h in range(HEADS):
        parts.append(jnp.where(lane_head == h, x, 0.0))
    return jnp.concatenate(parts, axis=0)


def _causal_conv4(x, pad_ref, w):
    r = x.shape[0]
    pad_ref[8:8 + r, :] = x
    acc = x * w[CONV_W - 1:CONV_W, :]
    for k in range(1, CONV_W):
        acc = acc + pad_ref[8 - k:8 - k + r, :] * w[CONV_W - 1 - k:CONV_W - k, :]
    pad_ref[0:8, :] = x[r - 8:r, :]
    return acc


def _norm_matmul_kernel(x_ref, g_ref, w_ref, o_ref):
    x = x_ref[...]
    ms = jnp.mean(x * x, axis=-1, keepdims=True)
    h = (x * lax.rsqrt(ms + EPS)) * g_ref[...]
    o_ref[...] = _dot(h.astype(BF16), w_ref[...]).astype(o_ref.dtype)


def norm_matmul(x, g, w_bf16, tm, out_dtype=F32):
    m, d = x.shape
    n = w_bf16.shape[1]
    return pl.pallas_call(
        _norm_matmul_kernel,
        grid=(m // tm,),
        in_specs=[pl.BlockSpec((tm, d), lambda i: (i, 0)),
                  pl.BlockSpec((1, d), lambda i: (0, 0)),
                  pl.BlockSpec((d, n), lambda i: (0, 0))],
        out_specs=pl.BlockSpec((tm, n), lambda i: (i, 0)),
        out_shape=jax.ShapeDtypeStruct((m, n), out_dtype),
        compiler_params=_cparams("parallel"),
        name="norm_matmul",
    )(x, g.reshape(1, d), w_bf16)


def _small_prep_kernel(s_ref, p_ref, o_ref, x_ref):
    S = s_ref.shape[1]
    W = s_ref.shape[2]
    blk = SEQ_TILE
    lane = _iota((blk, W), 1)
    is_fox = (lane >= LANE_FOX) & (lane < LANE_FOX + HEADS)
    is_beta = (lane >= LANE_BETA) & (lane < LANE_BETA + HEADS)
    is_gdec = (lane >= LANE_GDEC) & (lane < LANE_GDEC + HEADS)
    r = _iota((blk, blk), 0)
    c = _iota((blk, blk), 1)
    tril_all = jnp.where(r >= c, 1.0, 0.0).astype(BF16)
    tril_chunk = jnp.where((r >= c) & (r // GDN_CHUNK == c // GDN_CHUNK), 1.0, 0.0).astype(BF16)
    neg_exp_a = -jnp.exp(p_ref[1:2, :])
    carry = jnp.zeros((1, W), F32)
    for i in range(S // blk):
        sl = pl.ds(i * blk, blk)
        z = s_ref[0, sl, :] + p_ref[0:1, :]
        fox = jnp.where(is_fox, _log_sigmoid(z), 0.0)
        beta = jnp.where(is_beta, _sigmoid(z), 0.0)
        gdec = jnp.where(is_gdec, neg_exp_a * _softplus(z), 0.0)
        cf = _dot_exact_lhs(tril_all, fox) + carry
        cg = _dot_exact_lhs(tril_chunk, gdec)
        carry = cf[blk - 1:blk, :]
        o_ref[0, sl, :] = cf + cg + beta
        both = cf + cg
        for j, src_lane in enumerate(range(LANE_FOX, LANE_FOX + HEADS)):
            x_ref[0, sl, j * W:(j + 1) * W] = jnp.broadcast_to(both[:, src_lane:src_lane + 1], (blk, W))


def small_prep(proj3, params):
    b, s, _ = proj3.shape
    w = SMALL_W
    return pl.pallas_call(
        _small_prep_kernel,
        grid=(b,),
        in_specs=[pl.BlockSpec((1, s, w), lambda i: (i, 0, COL_SMALL)),
                  pl.BlockSpec((8, w), lambda i: (0, 0))],
        out_specs=[pl.BlockSpec((1, s, w), lambda i: (i, 0, 0)),
                   pl.BlockSpec((1, s, HEADS * w), lambda i: (i, 0, 0))],
        out_shape=[jax.ShapeDtypeStruct((b, s, w), F32), jax.ShapeDtypeStruct((b, s, HEADS * w), F32)],
        compiler_params=_cparams("parallel"),
        name="small_prep",
    )(proj3, params)


def _hgrn2_kernel(q_ref, f_ref, i_ref, g_ref, lb_ref, o_ref, st_ref, qs, ks, vs, bs, os_):
    nb = q_ref.shape[0]
    T = q_ref.shape[1]
    C = HGRN_CHUNK
    seqs = range(nb)

    @pl.when(pl.program_id(1) == 0)
    def _():
        st_ref[...] = jnp.zeros_like(st_ref)

    same_head = _head_ones()
    ones_bf = jnp.where(same_head, 1.0, 0.0).astype(BF16)
    lb = lb_ref[...]
    log_lb = jnp.log(lb)
    r = _iota((T, T), 0)
    c = _iota((T, T), 1)
    tril_chunk = jnp.where((r >= c) & (r // C == c // C), 1.0, 0.0).astype(BF16)
    for bb in seqs:
        fl = f_ref[bb]
        c2 = jnp.log1p(-lb) + _log_sigmoid(fl)
        mx = jnp.maximum(log_lb, c2)
        log_f = mx + jnp.log1p(jnp.exp(-jnp.abs(log_lb - c2)))
        bs[bb] = _dot_exact_lhs(tril_chunk, log_f)
        qs[bb] = _silu(q_ref[bb])
        ks[bb] = (1.0 - lb) * _sigmoid(-fl)
        vs[bb] = i_ref[bb]

    H8 = C // 2
    trow8 = _iota((H8, GROUP_WIDTH), 0)

    def chunk(ci, carry):
        r0 = pl.multiple_of(ci * C, C)
        qc = [qs[bb, pl.ds(r0, C), :] for bb in seqs]
        kc = [ks[bb, pl.ds(r0, C), :] for bb in seqs]
        vc = [vs[bb, pl.ds(r0, C), :] for bb in seqs]
        bc = [bs[bb, pl.ds(r0, C), :] for bb in seqs]
        st = [st_ref[bb] for bb in seqs]
        a = []
        for bb in seqs:
            parts = []
            for half, s_range in ((0, range(H8)), (1, range(C))):
                b_half = bc[bb][half * H8:(half + 1) * H8, :]
                q_half = qc[bb][half * H8:(half + 1) * H8, :]
                for s in s_range:
                    blk = jnp.exp(b_half - bc[bb][s:s + 1, :]) * (q_half * kc[bb][s:s + 1, :])
                    if s >= half * H8:
                        blk = jnp.where(trow8 >= s - half * H8, blk, 0.0)
                    parts.append(blk)
            a.append(jnp.concatenate(parts, axis=0).astype(BF16))
        sc = [_dot(a[bb], ones_bf) for bb in seqs]
        o = [_dot_nt((qc[bb] * jnp.exp(bc[bb])).astype(BF16), st[bb].astype(BF16)) for bb in seqs]
        b_last = [bc[bb][C - 1:C, :] for bb in seqs]
        upd = [_dot_tn(vc[bb].astype(BF16), (kc[bb] * jnp.exp(b_last[bb] - bc[bb])).astype(BF16)) for bb in seqs]
        for bb in seqs:
            top = o[bb][0:H8, :]
            bot = o[bb][H8:C, :]
            for s in range(H8):
                top = top + sc[bb][s * H8:(s + 1) * H8, :] * vc[bb][s:s + 1, :]
            for s in range(C):
                bot = bot + sc[bb][(H8 + s) * H8:(H8 + s + 1) * H8, :] * vc[bb][s:s + 1, :]
            os_[bb, pl.ds(r0, C), :] = jnp.concatenate([top, bot], axis=0)
            st_ref[bb] = st[bb] * jnp.exp(b_last[bb]) + jnp.where(same_head, upd[bb], 0.0)
        return carry

    lax.fori_loop(0, T // C, chunk, 0)
    for bb in seqs:
        o_ref[bb] = _head_rms(os_[bb], ones_bf) * _silu(g_ref[bb])


def hgrn2(proj3, lb):
    b, s, _ = proj3.shape
    t = SEQ_TILE
    gw = GROUP_WIDTH
    nb = HGRN_BATCH
    col = lambda cidx: pl.BlockSpec((nb, t, gw), lambda i, j: (i, j, cidx))
    return pl.pallas_call(
        _hgrn2_kernel,
        grid=(b // nb, s // t),
        in_specs=[col(COL_AQ), col(COL_AF), col(COL_AI), col(COL_AG),
                  pl.BlockSpec((1, gw), lambda i, j: (0, 0))],
        out_specs=pl.BlockSpec((nb, t, gw), lambda i, j: (i, j, 0)),
        out_shape=jax.ShapeDtypeStruct((b, s, gw), F32),
        scratch_shapes=[pltpu.VMEM((nb, gw, gw), F32)] + [pltpu.VMEM((nb, t, gw), F32)] * 5,
        compiler_params=_cparams("parallel", "arbitrary"),
        name="hgrn2",
    )(proj3, proj3, proj3, proj3, lb.reshape(1, gw))


def _fox_kernel(q_ref, k_ref, v_ref, cx_ref, o_ref, kb_ref, vt_ref, acc_ref):
    nb = q_ref.shape[0]
    tq = q_ref.shape[1]
    tk = tq
    S = k_ref.shape[1]
    qi = pl.program_id(1)
    seqs = range(nb)

    @pl.when(qi == 0)
    def _():
        for bb in seqs:
            for i in range(S // tk):
                sl = pl.ds(i * tk, tk)
                kb_ref[bb, sl, :] = k_ref[bb, sl, :].astype(BF16)
                vt_ref[bb, :, sl] = v_ref[bb, sl, :].T.astype(BF16)

    lane_head = _iota((tq, GROUP_WIDTH), 1) // HEAD_DIM
    qh = []
    for bb in seqs:
        qsc = q_ref[bb] * (HEAD_DIM ** -0.5)
        qh.append([jnp.where(lane_head == h, qsc, 0.0).astype(BF16) for h in range(HEADS)])
    acc_ref[...] = jnp.zeros_like(acc_ref)
    reps = tq // SMALL_W
    causal = _iota((tk, tq), 0) <= _iota((tk, tq), 1)

    def step(kb, stats, masked):
        k0 = pl.multiple_of(kb * tk, tk)
        new_stats = [[None] * HEADS for _ in seqs]
        strips = [(h, bb) for h in range(HEADS) for bb in seqs]

        def scores(h, bb):
            return _dot_nt(kb_ref[bb, pl.ds(k0, tk), :], qh[bb][h])

        ahead = 3
        queue = [scores(*strips[j]) for j in range(ahead)]
        for i, (h, bb) in enumerate(strips):
            st = queue.pop(0)
            if i + ahead < len(strips):
                queue.append(scores(*strips[i + ahead]))
            rows = slice(h * HEAD_DIM, (h + 1) * HEAD_DIM)
            cx = cx_ref[bb, pl.ds(k0, tk), h * SMALL_W:(h + 1) * SMALL_W]
            st = st - jnp.concatenate([cx] * reps, axis=1)
            if masked:
                st = jnp.where(causal, st, -jnp.inf)
            m_old, l_old = stats[bb][h]
            m_new = jnp.maximum(m_old, jnp.max(st, axis=0, keepdims=True))
            alpha = jnp.exp(m_old - m_new)
            p = jnp.exp(st - m_new)
            l_new = alpha * l_old + jnp.sum(p, axis=0, keepdims=True)
            pv = _dot(vt_ref[bb, rows, pl.ds(k0, tk)], p.astype(BF16))
            acc_ref[bb, rows, :] = alpha * acc_ref[bb, rows, :] + pv
            new_stats[bb][h] = (m_new, l_new)
        return tuple(tuple(new_stats[bb]) for bb in seqs)

    init = tuple(tuple((jnp.full((1, tq), -jnp.inf, F32), jnp.zeros((1, tq), F32)) for _ in range(HEADS)) for _ in seqs)
    stats = lax.fori_loop(0, qi, lambda kb, c: step(kb, c, False), init)
    stats = step(qi, stats, True)

    ones_bf = jnp.where(_head_ones(), 1.0, 0.0).astype(BF16)
    for bb in seqs:
        inv_l = jnp.concatenate([jnp.broadcast_to(1.0 / stats[bb][h][1], (HEAD_DIM, tq)) for h in range(HEADS)], axis=0)
        o_ref[bb] = _head_rms((acc_ref[bb] * inv_l).T, ones_bf)


def fox_attention(proj3, c_exp):
    b, s, _ = proj3.shape
    tq = SEQ_TILE
    gw = GROUP_WIDTH
    nb = FOX_BATCH
    return pl.pallas_call(
        _fox_kernel,
        grid=(b // nb, s // tq),
        in_specs=[pl.BlockSpec((nb, tq, gw), lambda i, j: (i, j, COL_BQ)),
                  pl.BlockSpec((nb, s, gw), lambda i, j: (i, 0, COL_BK)),
                  pl.BlockSpec((nb, s, gw), lambda i, j: (i, 0, COL_BV)),
                  pl.BlockSpec((nb, s, HEADS * SMALL_W), lambda i, j: (i, 0, 0))],
        out_specs=pl.BlockSpec((nb, tq, gw), lambda i, j: (i, j, 0)),
        out_shape=jax.ShapeDtypeStruct((b, s, gw), F32),
        scratch_shapes=[pltpu.VMEM((nb, s, gw), BF16),
                        pltpu.VMEM((nb, gw, s), BF16),
                        pltpu.VMEM((nb, gw, tq), F32)],
        compiler_params=_cparams("parallel", "arbitrary"),
        name="fox_attention",
    )(proj3, proj3, proj3, c_exp)


def _rglru_kernel(x_ref, g_ref, cw_ref, cb_ref, wa_ref, ba_ref, wx_ref, bx_ref, lam_ref, o_ref, prev_ref, h_ref):
    @pl.when(pl.program_id(1) == 0)
    def _():
        prev_ref[0:8, :] = jnp.zeros((8, prev_ref.shape[1]), F32)
        h_ref[...] = jnp.zeros_like(h_ref)

    x = x_ref[0]
    t = x.shape[0]
    xc = _causal_conv4(x, prev_ref, cw_ref[...]) + cb_ref[...]
    xb = xc.astype(BF16)
    r = _sigmoid(_dot(xb, wa_ref[...]) + ba_ref[...])
    ig = _sigmoid(_dot(xb, wx_ref[...]) + bx_ref[...])
    log_a = (-LRU_C * r) * _softplus(-lam_ref[...])
    a = jnp.exp(log_a)
    u = jnp.sqrt(1.0 - jnp.exp(2.0 * log_a)) * (ig * xc)
    row = _iota(a.shape, 0)
    d = 1
    while d < t:
        valid = row >= d
        u = jnp.where(valid, a * pltpu.roll(u, d, 0) + u, u)
        a = jnp.where(valid, a * pltpu.roll(a, d, 0), a)
        d *= 2
    h = a * h_ref[...] + u
    h_ref[...] = h[t - 1:t, :]
    ones_bf = jnp.where(_head_ones(), 1.0, 0.0).astype(BF16)
    o_ref[0] = _head_rms(h * _gelu_tanh(g_ref[0]), ones_bf)


def rglru(proj3, conv_w, conv_b, wa_bd, ba, wx_bd, bx, lam):
    b, s, _ = proj3.shape
    t = ROW_TILE
    gw = GROUP_WIDTH
    row = lambda: pl.BlockSpec((1, gw), lambda i, j: (0, 0))
    return pl.pallas_call(
        _rglru_kernel,
        grid=(b, s // t),
        in_specs=[pl.BlockSpec((1, t, gw), lambda i, j: (i, j, COL_CX)),
                  pl.BlockSpec((1, t, gw), lambda i, j: (i, j, COL_CG)),
                  pl.BlockSpec((CONV_W, gw), lambda i, j: (0, 0)), row(),
                  pl.BlockSpec((gw, gw), lambda i, j: (0, 0)), row(),
                  pl.BlockSpec((gw, gw), lambda i, j: (0, 0)), row(), row()],
        out_specs=pl.BlockSpec((1, t, gw), lambda i, j: (i, j, 0)),
        out_shape=jax.ShapeDtypeStruct((b, s, gw), F32),
        scratch_shapes=[pltpu.VMEM((8 + t, gw), F32), pltpu.VMEM((1, gw), F32)],
        compiler_params=_cparams("parallel", "arbitrary"),
        name="rglru",
    )(proj3, proj3, conv_w, conv_b.reshape(1, gw), wa_bd, ba.reshape(1, gw), wx_bd, bx.reshape(1, gw), lam.reshape(1, gw))


def _gdn_kernel(q_ref, k_ref, v_ref, z_ref, sm_ref, cw_ref, o_ref, st_ref, pq_ref, pk_ref, pv_ref):
    nb = q_ref.shape[0]
    T = q_ref.shape[1]
    C = GDN_CHUNK

    @pl.when(pl.program_id(1) == 0)
    def _():
        st_ref[...] = jnp.zeros_like(st_ref)
        for pad_ref in (pq_ref, pk_ref, pv_ref):
            pad_ref[:, 0:8, :] = jnp.zeros((nb, 8, pad_ref.shape[2]), F32)

    same_head = _head_ones()
    ones_bf = jnp.where(same_head, 1.0, 0.0).astype(BF16)
    cw = cw_ref[...]
    gw = GROUP_WIDTH
    er = _iota((SMALL_W, gw), 0)
    ec = _iota((SMALL_W, gw), 1) // HEAD_DIM
    exp_beta = jnp.where(er == LANE_BETA + ec, 1.0, 0.0).astype(BF16)
    exp_g = jnp.where(er == LANE_GDEC + ec, 1.0, 0.0).astype(BF16)

    prep = []
    for bb in range(nb):
        xq, xk, xv = q_ref[bb], k_ref[bb], v_ref[bb]
        q = _silu(_causal_conv4(xq, pq_ref.at[bb], cw[:, 0:gw]))
        k = _silu(_causal_conv4(xk, pk_ref.at[bb], cw[:, gw:2 * gw]))
        v = _silu(_causal_conv4(xv, pv_ref.at[bb], cw[:, 2 * gw:3 * gw]))
        q = q * lax.rsqrt(_dot_exact_rhs(q * q, ones_bf) + EPS) * (HEAD_DIM ** -0.5)
        k = k * lax.rsqrt(_dot_exact_rhs(k * k, ones_bf) + EPS)
        sm = sm_ref[bb]
        betax = _dot_exact_rhs(sm, exp_beta)
        gx = _dot_exact_rhs(sm, exp_g)
        prep.append((q, k, v, betax, gx))

    n_chunks = T // C
    trow = _iota((C, gw), 0)
    scol = _iota((C, gw), 1) % HEAD_DIM
    incl = scol <= trow
    strict = scol < trow
    eye = jnp.where(scol == trow, 1.0, 0.0)

    def bdiag(x_cat):
        return _stack_heads(x_cat).astype(BF16)

    chains = []
    for ci in range(n_chunks):
        sl = slice(ci * C, (ci + 1) * C)
        for bb in range(nb):
            q, k, v, betax, gx = prep[bb]
            qc, kc, vc, bx, gc = q[sl], k[sl], v[sl], betax[sl], gx[sl]
            grow = jnp.sum(jnp.where(scol == trow, gc, 0.0), axis=0, keepdims=True)
            eg = jnp.exp(gc)
            kb = kc * bx
            g_last = gc[C - 1:C, :]
            gamma = jnp.where(incl, jnp.exp(jnp.where(incl, gc - grow, 0.0)), 0.0)
            ks_bf = bdiag(kc)
            m = jnp.where(strict, _dot_nt(kb.astype(BF16), ks_bf) * gamma, 0.0)
            chains.append(dict(
                x=eye - m, y=m,
                a_qk=(_dot_nt(qc.astype(BF16), ks_bf) * gamma).astype(BF16),
                rhs_u=bdiag(vc * bx),
                rhs_w=bdiag(kb * eg),
                q_dec=(qc * eg).astype(BF16),
                k_dec=(kc * jnp.exp(g_last - gc)).astype(BF16),
                decay=jnp.exp(g_last)))
    for ch in chains:
        ch["ybd"] = bdiag(ch["y"])
    for _ in range(5):
        for ch in chains:
            ch["y"] = _dot(ch["y"].astype(BF16), ch["ybd"])
            ch["ybd"] = bdiag(ch["y"])
        for ch in chains:
            ch["x"] = ch["x"] + _dot(ch["x"].astype(BF16), ch["ybd"])
    for ch in chains:
        t_inv = ch["x"].astype(BF16)
        ch["u"] = _dot(t_inv, ch["rhs_u"])
        ch["wk"] = _dot(t_inv, ch["rhs_w"]).astype(BF16)

    outs = [[] for _ in range(nb)]
    for ci in range(n_chunks):
        for bb in range(nb):
            ch = chains[ci * nb + bb]
            st = st_ref[bb]
            st_bf = st.astype(BF16)
            v_new = ch["u"] - _dot(ch["wk"], st_bf)
            outs[bb].append(_dot(ch["q_dec"], st_bf) + _dot(ch["a_qk"], bdiag(v_new)))
            upd = _dot_tn(ch["k_dec"], v_new.astype(BF16))
            st_ref[bb] = st * ch["decay"] + jnp.where(same_head, upd, 0.0)
    for bb in range(nb):
        o = jnp.concatenate(outs[bb], axis=0)
        o_ref[bb] = _head_rms(o, ones_bf) * _silu(z_ref[bb])


def gdn(proj3, sm, conv_w):
    b, s, _ = proj3.shape
    t = GDN_TILE
    gw = GROUP_WIDTH
    nb = GDN_BATCH
    col = lambda cidx: pl.BlockSpec((nb, t, gw), lambda i, j: (i, j, cidx))
    return pl.pallas_call(
        _gdn_kernel,
        grid=(b // nb, s // t),
        in_specs=[col(COL_DQ), col(COL_DK), col(COL_DV), col(COL_DZ),
                  pl.BlockSpec((nb, t, SMALL_W), lambda i, j: (i, j, 0)),
                  pl.BlockSpec((CONV_W, 3 * gw), lambda i, j: (0, 0))],
        out_specs=pl.BlockSpec((nb, t, gw), lambda i, j: (i, j, 0)),
        out_shape=jax.ShapeDtypeStruct((b, s, gw), F32),
        scratch_shapes=[pltpu.VMEM((nb, gw, gw), F32)] + [pltpu.VMEM((nb, 8 + t, gw), F32)] * 3,
        compiler_params=_cparams("parallel", "arbitrary"),
        name="gdn",
    )(proj3, proj3, proj3, proj3, sm, conv_w)


def _mix_out(x, ys, gain, w_ref):
    gw = GROUP_WIDTH
    acc = x
    for i, y in enumerate(ys):
        yg = (y * gain[:, i * gw:(i + 1) * gw]).astype(BF16)
        acc = acc + _dot(yg, w_ref[i * gw:(i + 1) * gw, :])
    return acc


def _mem_attention(x, g, wq_ref, kv, wo_ref):
    t = x.shape[0]
    gw = GROUP_WIDTH
    ms = jnp.mean(x * x, axis=-1, keepdims=True)
    h = ((x * lax.rsqrt(ms + EPS)) * g).astype(BF16)
    q = _dot(h, wq_ref[...])
    k = kv[:, 0:gw].astype(BF16)
    v = kv[:, gw:2 * gw].astype(BF16)
    s = _dot_nt(k, _stack_heads(q).astype(BF16)) * (HEAD_DIM ** -0.5)
    s = s - jnp.max(s, axis=0, keepdims=True)
    p = jnp.exp(s)
    p = p * (1.0 / jnp.sum(p, axis=0, keepdims=True))
    ot = _dot_tn(v, p.astype(BF16))
    oc = jnp.concatenate([ot[hh * HEAD_DIM:(hh + 1) * HEAD_DIM, hh * t:(hh + 1) * t] for hh in range(HEADS)], axis=0).T
    return x + _dot(oc.astype(BF16), wo_ref[...])


def _post_mix_kernel(x_ref, ya_ref, yb_ref, yc_ref, yd_ref, gain_ref, wout_ref, gm_ref, wq_ref, kv_ref, wo_ref,
                     g_ref, w_ref, b_ref, x2_ref, h_ref, r_ref, n_ref, cnt_ref):
    @pl.when((pl.program_id(0) == 0) & (pl.program_id(1) == 0))
    def _():
        cnt_ref[...] = jnp.zeros_like(cnt_ref)

    x1 = _mix_out(x_ref[0], (ya_ref[0], yb_ref[0], yc_ref[0], yd_ref[0]), gain_ref[...], wout_ref)
    x = _mem_attention(x1, gm_ref[...], wq_ref, kv_ref[0], wo_ref)
    x2_ref[0] = x
    tm = x.shape[0]
    ms = jnp.mean(x * x, axis=-1, keepdims=True)
    h = (x * lax.rsqrt(ms + EPS)) * g_ref[...]
    h_ref[...] = _pack_bf16_pairs(h)
    hh, hm, _ = _split3(h)
    w = w_ref[...]
    wh, wm, _ = _split3(w)
    logits = _dot(hh, wh) + _dot(hh, wm) + _dot(hm, wh) + b_ref[...]
    lane_i = _iota(logits.shape, 1)
    lane = lane_i.astype(F32)
    lane_grp = (lane_i // EXPERTS_PER_GROUP).astype(F32)
    big = jnp.float32(1e9)
    neg = -jnp.inf
    is_grp = (lane_i >= N_EXPERTS) & (lane_i < N_EXPERTS + N_EXPERT_GROUPS)
    gl = jnp.where(is_grp, logits, neg)
    gmax = jnp.max(gl, axis=-1, keepdims=True)
    p_grp = 1.0 / jnp.sum(jnp.exp(gl - gmax), axis=-1, keepdims=True)
    g_sel = jnp.min(jnp.where(gl == gmax, lane, big), axis=-1, keepdims=True) - N_EXPERTS
    in_grp = (lane_i < N_EXPERTS) & (lane_grp == g_sel)
    el = jnp.where(in_grp, logits, neg)
    m1 = jnp.max(el, axis=-1, keepdims=True)
    i1 = jnp.min(jnp.where(el == m1, lane, big), axis=-1, keepdims=True)
    el2 = jnp.where(lane == i1, neg, el)
    m2 = jnp.max(el2, axis=-1, keepdims=True)
    i2 = jnp.min(jnp.where(el2 == m2, lane, big), axis=-1, keepdims=True)
    e21 = jnp.exp(m2 - m1)
    w1 = 1.0 / (1.0 + e21)
    w2 = e21 / (1.0 + e21)
    oh1 = lane == i1
    oh2 = lane == i2
    onehot = jnp.where(oh1 | oh2, 1.0, 0.0)
    strict = jnp.where(_iota((tm, tm), 0) > _iota((tm, tm), 1), 1.0, 0.0).astype(BF16)
    before = _dot(strict, onehot.astype(BF16)) + cnt_ref[...]
    rank1 = jnp.sum(jnp.where(oh1, before, 0.0), axis=-1, keepdims=True)
    rank2 = jnp.sum(jnp.where(oh2, before, 0.0), axis=-1, keepdims=True)
    cnt = before[tm - 1:tm, :] + onehot[tm - 1:tm, :]
    cnt_ref[...] = cnt
    n_ref[...] = jnp.broadcast_to(cnt, n_ref.shape)
    cols = (i1, i2, p_grp * w1, p_grp * w2, rank1, rank2)
    route = jnp.zeros(logits.shape, F32)
    for j, col in enumerate(cols):
        route = jnp.where(lane_i == j, col, route)
    r_ref[...] = route


def post_mix(x3, ya, yb, yc, yd, gain, w_out_bf16, g_mem, wq_bf16, kv3, wo_bf16, g_ffn, w_router, b_router):
    b, s, d = x3.shape
    t = ROW_TILE
    gw = GROUP_WIDTH
    nt = s // t
    mlen = kv3.shape[1]
    tile = lambda w: pl.BlockSpec((1, t, w), lambda i, j: (i, j, 0))
    const = lambda shape: pl.BlockSpec(shape, lambda i, j: (0,) * len(shape))
    tok = lambda w: pl.BlockSpec((t, w), lambda i, j: (i * nt + j, 0))
    return pl.pallas_call(
        _post_mix_kernel,
        grid=(b, nt),
        in_specs=[tile(d), tile(gw), tile(gw), tile(gw), tile(gw),
                  const((1, 4 * gw)), const((4 * gw, d)),
                  const((1, d)), const((d, gw)), pl.BlockSpec((1, mlen, 2 * gw), lambda i, j: (i, 0, 0)), const((gw, d)),
                  const((1, d)), const((d, SMALL_W)), const((1, SMALL_W))],
        out_specs=[tile(d), tok(d // 2), tok(SMALL_W), const((8, SMALL_W))],
        out_shape=[jax.ShapeDtypeStruct((b, s, d), F32), jax.ShapeDtypeStruct((b * s, d // 2), jnp.uint32),
                   jax.ShapeDtypeStruct((b * s, SMALL_W), F32), jax.ShapeDtypeStruct((8, SMALL_W), F32)],
        scratch_shapes=[pltpu.VMEM((1, SMALL_W), F32)],
        compiler_params=_cparams("arbitrary", "arbitrary"),
        name="post_mix",
    )(x3, ya, yb, yc, yd, gain.reshape(1, 4 * gw), w_out_bf16, g_mem.reshape(1, d), wq_bf16, kv3, wo_bf16,
      g_ffn.reshape(1, d), w_router, b_router)


def _experts_kernel(te_ref, nu_ref, run_ref, nxt_ref, xs_ref, wgu_hbm, wdn_hbm, y_ref,
                    wgu_buf, wdn_buf, wgu_bf, wdn_bf, sem, *, layer):
    i = pl.program_id(0)

    def weight_copies(expert, slot):
        return (pltpu.make_async_copy(wgu_hbm.at[layer, expert], wgu_buf.at[slot], sem.at[slot, 0]),
                pltpu.make_async_copy(wdn_hbm.at[layer, expert], wdn_buf.at[slot], sem.at[slot, 1]))

    @pl.when(i >= nu_ref[0])
    def _():
        y_ref[...] = jnp.zeros_like(y_ref)

    @pl.when(i < nu_ref[0])
    def _():
        expert = te_ref[i]
        slot = run_ref[i] % 2

        @pl.when(i == 0)
        def _():
            for c in weight_copies(expert, slot):
                c.start()

        @pl.when((i == 0) | (te_ref[jnp.maximum(i - 1, 0)] != expert))
        def _():
            for c in weight_copies(expert, slot):
                c.wait()
            nxt = nxt_ref[i]

            @pl.when(nxt >= 0)
            def _():
                for c in weight_copies(nxt, 1 - slot):
                    c.start()

            wgu_bf[...] = wgu_buf[slot].astype(BF16)
            wdn_bf[...] = wdn_buf[slot].astype(BF16)

        gu = _dot(_unpack_bf16_pairs(xs_ref[...]).astype(BF16), wgu_bf[...])
        act = _silu(gu[:, 0:D_EXPERT]) * gu[:, D_EXPERT:2 * D_EXPERT]
        y_ref[...] = _pack_bf16_pairs(_dot(act.astype(BF16), wdn_bf[...]))


def moe_experts(xs, tile_expert, n_used, run_index, next_expert, w_gu, w_dn, layer, tm):
    n_rows, dp = xs.shape
    d = 2 * dp
    return pl.pallas_call(
        functools.partial(_experts_kernel, layer=layer),
        grid_spec=pltpu.PrefetchScalarGridSpec(
            num_scalar_prefetch=4,
            grid=(n_rows // tm,),
            in_specs=[pl.BlockSpec((tm, dp), lambda i, te, nu, run, nxt: (jnp.minimum(i, nu[0] - 1), 0)),
                      pl.BlockSpec(memory_space=pl.ANY),
                      pl.BlockSpec(memory_space=pl.ANY)],
            out_specs=pl.BlockSpec((tm, dp), lambda i, te, nu, run, nxt: (i, 0)),
            scratch_shapes=[pltpu.VMEM((2, d, 2 * D_EXPERT), F32), pltpu.VMEM((2, D_EXPERT, d), F32),
                            pltpu.VMEM((d, 2 * D_EXPERT), BF16), pltpu.VMEM((D_EXPERT, d), BF16),
                            pltpu.SemaphoreType.DMA((2, 2))]),
        out_shape=jax.ShapeDtypeStruct((n_rows, dp), jnp.uint32),
        compiler_params=_cparams("arbitrary"),
        name="moe_experts",
    )(tile_expert, n_used, run_index, next_expert, xs, w_gu, w_dn)


def sc_gather_rows(table, idx):
    n_idx = idx.shape[0]
    width = table.shape[1]
    info = plsc.get_sparse_core_info()
    n_workers = info.num_cores * info.num_subcores
    per_worker = n_idx // n_workers
    chunk = SC_GATHER_CHUNK
    assert n_idx % (n_workers * chunk) == 0 and width % info.num_lanes == 0
    mesh = plsc.VectorSubcoreMesh(core_axis_name="c", subcore_axis_name="s")

    def body(table_hbm, idx_hbm, out_hbm, idx_v, rows_v, sem):
        wid = lax.axis_index("s") * info.num_cores + lax.axis_index("c")
        base = wid * per_worker

        @pl.loop(0, per_worker // chunk)
        def _(j):
            off = pl.multiple_of(base + j * chunk, chunk)
            pltpu.sync_copy(idx_hbm.at[pl.ds(off, chunk)], idx_v)
            pltpu.async_copy(table_hbm.at[idx_v], rows_v, sem).wait()
            pltpu.sync_copy(rows_v, out_hbm.at[pl.ds(off, chunk)])

    return pl.kernel(
        body,
        out_type=jax.ShapeDtypeStruct((n_idx, width), table.dtype),
        mesh=mesh,
        scratch_types=[pltpu.VMEM((chunk,), jnp.int32), pltpu.VMEM((chunk, width), table.dtype),
                       pltpu.SemaphoreType.DMA],
        name="sc_gather_rows",
    )(table, idx)


def sc_dispatch_rows(rows, dest, pad_idx, zero_rows):
    m, width = rows.shape
    n_pad = pad_idx.shape[0]
    info = plsc.get_sparse_core_info()
    n_workers = info.num_cores * info.num_subcores
    chunk = SC_GATHER_CHUNK
    assert m % (n_workers * chunk) == 0 and n_pad % (n_workers * chunk) == 0 and zero_rows.shape == (chunk, width)
    rows_per_worker = m // n_workers
    pad_per_worker = n_pad // n_workers
    mesh = plsc.VectorSubcoreMesh(core_axis_name="c", subcore_axis_name="s")

    def body(rows_hbm, dest_hbm, pad_hbm, zero_hbm, out_hbm, idx_v, rows_v, sem):
        wid = lax.axis_index("s") * info.num_cores + lax.axis_index("c")

        @pl.loop(0, rows_per_worker // chunk)
        def _(j):
            off = pl.multiple_of(wid * rows_per_worker + j * chunk, chunk)
            pltpu.sync_copy(rows_hbm.at[pl.ds(off, chunk)], rows_v)
            for slot in range(2):
                pltpu.sync_copy(dest_hbm.at[pl.ds(slot * m + off, chunk)], idx_v)
                pltpu.async_copy(rows_v, out_hbm.at[idx_v], sem).wait()

        pltpu.sync_copy(zero_hbm, rows_v)

        @pl.loop(0, pad_per_worker // chunk)
        def _(j):
            off = pl.multiple_of(wid * pad_per_worker + j * chunk, chunk)
            pltpu.sync_copy(pad_hbm.at[pl.ds(off, chunk)], idx_v)
            pltpu.async_copy(rows_v, out_hbm.at[idx_v], sem).wait()

    return pl.kernel(
        body,
        out_type=jax.ShapeDtypeStruct((2 * m + n_pad, width), rows.dtype),
        mesh=mesh,
        scratch_types=[pltpu.VMEM((chunk,), jnp.int32), pltpu.VMEM((chunk, width), rows.dtype),
                       pltpu.SemaphoreType.DMA],
        name="sc_dispatch_rows",
    )(rows, dest, pad_idx, zero_rows)


def _combine_rows_kernel(x_ref, r_ref, g_ref, y1_ref, y2_ref, o_ref, *, final_norm):
    r = r_ref[...]
    out = x_ref[...] + r[:, 2:3] * _unpack_bf16_pairs(y1_ref[...]) + r[:, 3:4] * _unpack_bf16_pairs(y2_ref[...])
    if final_norm:
        ms = jnp.mean(out * out, axis=-1, keepdims=True)
        out = (out * lax.rsqrt(ms + EPS)) * g_ref[...]
    o_ref[...] = out


def moe_combine_rows(x2, route, y_rows, g_final, tm, final_norm):
    m, d = x2.shape
    steps = m // tm
    return pl.pallas_call(
        functools.partial(_combine_rows_kernel, final_norm=final_norm),
        grid=(steps,),
        in_specs=[pl.BlockSpec((tm, d), lambda i: (i, 0)),
                  pl.BlockSpec((tm, SMALL_W), lambda i: (i, 0)),
                  pl.BlockSpec((1, d), lambda i: (0, 0)),
                  pl.BlockSpec((tm, d // 2), lambda i: (i, 0)),
                  pl.BlockSpec((tm, d // 2), lambda i: (i + steps, 0))],
        out_specs=pl.BlockSpec((tm, d), lambda i: (i, 0)),
        out_shape=jax.ShapeDtypeStruct((m, d), F32),
        compiler_params=_cparams("parallel"),
        name="moe_combine_rows",
    )(x2, route, g_final.reshape(1, d), y_rows, y_rows)


def moe_routed(x2, h, route, counts, w_gu, w_dn, layer, g_final, final_norm):
    m, d = x2.shape
    te_rows = EXPERT_TILE
    n_rows = 2 * m + N_EXPERTS * te_rows
    cnt = counts[0, 0:N_EXPERTS].astype(jnp.int32)
    padded = (cnt + te_rows - 1) // te_rows * te_rows
    seg_end = jnp.cumsum(padded)
    seg_start = seg_end - padded
    idx = route[:, 0:2].astype(jnp.int32)
    onehot = idx[:, :, None] == jnp.arange(N_EXPERTS, dtype=jnp.int32)
    dest = (jnp.sum(jnp.where(onehot, seg_start, 0), axis=-1) + route[:, 4:6].astype(jnp.int32)).T.reshape(2 * m)
    tile_start = jnp.arange(n_rows // te_rows, dtype=jnp.int32) * te_rows
    tile_expert = jnp.minimum(jnp.sum((tile_start[:, None] >= seg_end[None, :]).astype(jnp.int32), axis=1), N_EXPERTS - 1)
    n_used = (seg_end[N_EXPERTS - 1] // te_rows).reshape(1)
    n_pad = n_rows - 2 * m
    experts = jnp.arange(N_EXPERTS, dtype=jnp.int32)
    pad_end = jnp.cumsum(padded - cnt)
    k = jnp.arange(n_pad, dtype=jnp.int32)
    k_expert = jnp.sum((k[:, None] >= pad_end[None, :]).astype(jnp.int32), axis=1)
    first_pad = seg_start + cnt - (pad_end - (padded - cnt))
    in_segment = jnp.sum(jnp.where(k_expert[:, None] == experts[None, :], first_pad[None, :], 0), axis=1) + k
    pad_idx = jnp.where(k_expert < N_EXPERTS, in_segment, seg_end[N_EXPERTS - 1] + k - pad_end[N_EXPERTS - 1])
    xs = sc_dispatch_rows(h, dest, pad_idx, jnp.zeros((SC_GATHER_CHUNK, d // 2), h.dtype))
    n_tiles = n_rows // te_rows
    tile_ids = jnp.arange(n_tiles, dtype=jnp.int32)
    prev_expert = jnp.concatenate([jnp.full((1,), -1, jnp.int32), tile_expert[:-1]])
    run_index = jnp.maximum(jnp.cumsum(((tile_expert != prev_expert) & (tile_ids < n_used[0])).astype(jnp.int32)) - 1, 0)
    next_tile = jnp.take(seg_end, tile_expert) // te_rows
    next_expert = jnp.where(next_tile < n_used[0], jnp.take(tile_expert, jnp.minimum(next_tile, n_tiles - 1)), -1)
    ys = moe_experts(xs, tile_expert, n_used, run_index, next_expert, w_gu, w_dn, layer, tm=te_rows)
    y_rows = sc_gather_rows(ys, dest)
    return moe_combine_rows(x2, route, y_rows, g_final, tm=ROW_TILE, final_norm=final_norm)


def _w_in_prep_kernel(w_ref, o_ref):
    gw = GROUP_WIDTH
    a_end = 7 * gw
    c_start = a_end + HEADS
    c_end = c_start + 5 * gw
    z_start = c_end + 2 * HEADS
    assert a_end % SMALL_W == LANE_FOX and c_end % SMALL_W == LANE_BETA and LANE_GDEC == LANE_BETA + HEADS
    w = w_ref[...]
    o_ref[:, 0:a_end] = w[:, 0:a_end].astype(BF16)
    o_ref[:, a_end:a_end + 5 * gw] = w[:, c_start:c_end].astype(BF16)
    o_ref[:, 12 * gw:13 * gw] = w[:, z_start:z_start + gw].astype(BF16)
    blk_fox = w[:, a_end:a_end + SMALL_W]
    blk_gdn = w[:, c_end - LANE_BETA:c_end - LANE_BETA + SMALL_W]
    lane = _iota(blk_fox.shape, 1)
    small = jnp.where(lane < LANE_BETA, blk_fox, jnp.where(lane < LANE_GDEC + HEADS, blk_gdn, 0.0))
    o_ref[:, 13 * gw:13 * gw + SMALL_W] = small.astype(BF16)
    o_ref[:, 13 * gw + SMALL_W:] = jnp.zeros((w.shape[0], SMALL_W), BF16)


def w_in_prep(w_all, layer):
    _, d, n = w_all.shape
    tr = 128
    return pl.pallas_call(
        _w_in_prep_kernel,
        grid=(d // tr,),
        in_specs=[pl.BlockSpec((None, tr, n), lambda i: (layer, i, 0))],
        out_specs=pl.BlockSpec((tr, N_PROJ), lambda i: (i, 0)),
        out_shape=jax.ShapeDtypeStruct((d, N_PROJ), BF16),
        compiler_params=_cparams("parallel"),
        name="w_in_prep",
    )(w_all)


def _block_diag(w):
    h = w.shape[0]
    eye = jnp.eye(h, dtype=w.dtype)
    return (eye[:, None, :, None] * w[:, :, None, :]).reshape(h * HEAD_DIM, h * HEAD_DIM)


def kernel(x, mem, norm_mix, w_in, hgrn_lb, fox_fb, lru_conv_w, lru_conv_b, lru_wa, lru_ba, lru_wx, lru_bx, lru_lam, gdn_conv_w, gdn_a_log, gdn_dt_bias, mix_gain, w_out, norm_mem, norm_memkv, w_mq, w_mkv, w_mo, norm_ffn, w_rg, b_rg, w_re, b_re, w_e_gu, w_e_dn, norm_final):
    b, s, d = x.shape
    depth = w_in.shape[0]
    t_tok = b * s
    mlen = mem.shape[1]
    gw = GROUP_WIDTH

    lb_all = jnp.cumsum(jax.nn.softmax(hgrn_lb.astype(F32), axis=0), axis=0)
    lb_all = lb_all - lb_all[0]

    x = x.astype(F32)
    for l in range(depth):
        proj = norm_matmul(x.reshape(t_tok, d), norm_mix[l], w_in_prep(w_in, l), tm=ROW_TILE)
        proj3 = proj.reshape(b, s, N_PROJ)
        prm = jnp.zeros((8, SMALL_W), F32)
        prm = prm.at[0, LANE_FOX:LANE_FOX + HEADS].set(fox_fb[l].astype(F32))
        prm = prm.at[0, LANE_GDEC:LANE_GDEC + HEADS].set(gdn_dt_bias[l].astype(F32))
        prm = prm.at[1, LANE_GDEC:LANE_GDEC + HEADS].set(gdn_a_log[l].astype(F32))
        sm, c_exp = small_prep(proj3, prm)

        ya = hgrn2(proj3, lb_all[l])
        yb = fox_attention(proj3, c_exp)
        yc = rglru(proj3, lru_conv_w[l], lru_conv_b[l], _block_diag(lru_wa[l]).astype(BF16), lru_ba[l],
                   _block_diag(lru_wx[l]).astype(BF16), lru_bx[l], lru_lam[l])
        yd = gdn(proj3, sm, gdn_conv_w[l])

        kv = norm_matmul(mem.reshape(b * mlen, d), norm_memkv[l], w_mkv[l].astype(BF16), tm=256)
        w_router = jnp.concatenate([w_re[l], w_rg[l], jnp.zeros((d, SMALL_W - N_EXPERTS - N_EXPERT_GROUPS), F32)], axis=1)
        b_router = jnp.concatenate([b_re[l], b_rg[l], jnp.zeros((SMALL_W - N_EXPERTS - N_EXPERT_GROUPS,), F32)]).reshape(1, SMALL_W)
        x, h, route, counts = post_mix(x, ya, yb, yc, yd, mix_gain[l], w_out[l].astype(BF16), norm_mem[l],
                                       w_mq[l].astype(BF16), kv.reshape(b, mlen, 2 * gw), w_mo[l].astype(BF16),
                                       norm_ffn[l], w_router, b_router)

        x = moe_routed(x.reshape(t_tok, d), h, route, counts, w_e_gu, w_e_dn, l,
                       norm_final, final_norm=(l == depth - 1)).reshape(b, s, d)

    return x
```

```python
import functools
import math

import jax
import jax.numpy as jnp
from jax import lax
from jax.experimental import pallas as pl
from jax.experimental.pallas import tpu as pltpu
from jax.experimental.pallas import tpu_sc as plsc

F32 = jnp.float32
BF16 = jnp.bfloat16

HEAD_DIM = 64
GROUP_WIDTH = 256
HEADS = GROUP_WIDTH // HEAD_DIM
GDN_CHUNK = 64
GDN_TILE = 128
GDN_BATCH = 4
FOX_BATCH = 2
HGRN_BATCH = 4
HGRN_CHUNK = 16
CONV_W = 4
LRU_C = 8.0
EPS = 1e-6
N_EXPERT_GROUPS = 4
EXPERTS_PER_GROUP = 8
N_EXPERTS = N_EXPERT_GROUPS * EXPERTS_PER_GROUP
D_EXPERT = 256
SC_GATHER_CHUNK = 128
EXPERT_TILE = 512
SMALL_W = 128
SEQ_TILE = 256
ROW_TILE = 512
VMEM_LIMIT = 56 * 1024 * 1024

(COL_AQ, COL_AF, COL_AI, COL_AG, COL_BQ, COL_BK, COL_BV, COL_CX, COL_CG,
 COL_DQ, COL_DK, COL_DV, COL_DZ) = range(13)
N_PROJ = 13 * GROUP_WIDTH + 2 * SMALL_W
COL_SMALL = 13 * GROUP_WIDTH // SMALL_W
LANE_FOX = 0
LANE_BETA = 4
LANE_GDEC = 8


def _cparams(*sem):
    return pltpu.CompilerParams(dimension_semantics=sem, vmem_limit_bytes=VMEM_LIMIT)


def _dot(a, b):
    return jnp.dot(a, b, preferred_element_type=F32)


def _dot_nt(a, b):
    return lax.dot_general(a, b, (((1,), (1,)), ((), ())), preferred_element_type=F32)


def _dot_tn(a, b):
    return lax.dot_general(a, b, (((0,), (0,)), ((), ())), preferred_element_type=F32)


def _split3(x):
    h = x.astype(BF16)
    r = x - h.astype(F32)
    m = r.astype(BF16)
    l = (r - m.astype(F32)).astype(BF16)
    return h, m, l


def _dot_exact_rhs(x, w_bf16):
    h, m, l = _split3(x)
    return _dot(h, w_bf16) + _dot(m, w_bf16) + _dot(l, w_bf16)


def _dot_exact_lhs(w_bf16, x):
    h, m, l = _split3(x)
    return _dot(w_bf16, h) + _dot(w_bf16, m) + _dot(w_bf16, l)


def _iota(shape, dim):
    return lax.broadcasted_iota(jnp.int32, shape, dim)


def _head_ones(n=GROUP_WIDTH):
    r = _iota((n, n), 0) // HEAD_DIM
    c = _iota((n, n), 1) // HEAD_DIM
    return r == c


def _sigmoid(x):
    return 1.0 / (1.0 + jnp.exp(-x))


def _silu(x):
    return x * _sigmoid(x)


def _log_sigmoid(x):
    return jnp.minimum(x, 0.0) - jnp.log1p(jnp.exp(-jnp.abs(x)))


def _softplus(x):
    return jnp.maximum(x, 0.0) + jnp.log1p(jnp.exp(-jnp.abs(x)))


def _gelu_tanh(x):
    return 0.5 * x * (1.0 + jnp.tanh(math.sqrt(2.0 / math.pi) * (x + 0.044715 * (x * x * x))))


def _pack_bf16_pairs(x):
    n = x.shape[1] // 2
    u = lax.bitcast_convert_type(x.astype(BF16).astype(F32), jnp.uint32)
    return u[:, :n] | (u[:, n:] >> 16)


def _unpack_bf16_pairs(p):
    hi = lax.bitcast_convert_type(p & jnp.uint32(0xFFFF0000), F32)
    lo = lax.bitcast_convert_type(p << 16, F32)
    return jnp.concatenate([hi, lo], axis=1)


def _head_mean_sq(x, ones_bf16):
    return _dot_exact_rhs(x * x, ones_bf16) * (1.0 / HEAD_DIM)


def _head_rms(x, ones_bf16):
    return x * lax.rsqrt(_head_mean_sq(x, ones_bf16) + EPS)


def _stack_heads(x):
    lane_head = _iota(x.shape, 1) // HEAD_DIM
    parts = []
    for h in range(HEADS):
        parts.append(jnp.where(lane_head == h, x, 0.0))
    return jnp.concatenate(parts, axis=0)


def _causal_conv4(x, pad_ref, w):
    r = x.shape[0]
    pad_ref[8:8 + r, :] = x
    acc = x * w[CONV_W - 1:CONV_W, :]
    for k in range(1, CONV_W):
        acc = acc + pad_ref[8 - k:8 - k + r, :] * w[CONV_W - 1 - k:CONV_W - k, :]
    pad_ref[0:8, :] = x[r - 8:r, :]
    return acc


def _norm_matmul_kernel(x_ref, g_ref, w_ref, o_ref):
    x = x_ref[...]
    ms = jnp.mean(x * x, axis=-1, keepdims=True)
    h = (x * lax.rsqrt(ms + EPS)) * g_ref[...]
    o_ref[...] = _dot(h.astype(BF16), w_ref[...]).astype(o_ref.dtype)


def norm_matmul(x, g, w_bf16, tm, out_dtype=F32):
    m, d = x.shape
    n = w_bf16.shape[1]
    return pl.pallas_call(
        _norm_matmul_kernel,
        grid=(m // tm,),
        in_specs=[pl.BlockSpec((tm, d), lambda i: (i, 0)),
                  pl.BlockSpec((1, d), lambda i: (0, 0)),
                  pl.BlockSpec((d, n), lambda i: (0, 0))],
        out_specs=pl.BlockSpec((tm, n), lambda i: (i, 0)),
        out_shape=jax.ShapeDtypeStruct((m, n), out_dtype),
        compiler_params=_cparams("parallel"),
        name="norm_matmul",
    )(x, g.reshape(1, d), w_bf16)


def _small_prep_kernel(s_ref, p_ref, o_ref, x_ref):
    S = s_ref.shape[1]
    W = s_ref.shape[2]
    blk = SEQ_TILE
    lane = _iota((blk, W), 1)
    is_fox = (lane >= LANE_FOX) & (lane < LANE_FOX + HEADS)
    is_beta = (lane >= LANE_BETA) & (lane < LANE_BETA + HEADS)
    is_gdec = (lane >= LANE_GDEC) & (lane < LANE_GDEC + HEADS)
    r = _iota((blk, blk), 0)
    c = _iota((blk, blk), 1)
    tril_all = jnp.where(r >= c, 1.0, 0.0).astype(BF16)
    tril_chunk = jnp.where((r >= c) & (r // GDN_CHUNK == c // GDN_CHUNK), 1.0, 0.0).astype(BF16)
    neg_exp_a = -jnp.exp(p_ref[1:2, :])
    carry = jnp.zeros((1, W), F32)
    for i in range(S // blk):
        sl = pl.ds(i * blk, blk)
        z = s_ref[0, sl, :] + p_ref[0:1, :]
        fox = jnp.where(is_fox, _log_sigmoid(z), 0.0)
        beta = jnp.where(is_beta, _sigmoid(z), 0.0)
        gdec = jnp.where(is_gdec, neg_exp_a * _softplus(z), 0.0)
        cf = _dot_exact_lhs(tril_all, fox) + carry
        cg = _dot_exact_lhs(tril_chunk, gdec)
        carry = cf[blk - 1:blk, :]
        o_ref[0, sl, :] = cf + cg + beta
        both = cf + cg
        for j, src_lane in enumerate(range(LANE_FOX, LANE_FOX + HEADS)):
            x_ref[0, sl, j * W:(j + 1) * W] = jnp.broadcast_to(both[:, src_lane:src_lane + 1], (blk, W))


def small_prep(proj3, params):
    b, s, _ = proj3.shape
    w = SMALL_W
    return pl.pallas_call(
        _small_prep_kernel,
        grid=(b,),
        in_specs=[pl.BlockSpec((1, s, w), lambda i: (i, 0, COL_SMALL)),
                  pl.BlockSpec((8, w), lambda i: (0, 0))],
        out_specs=[pl.BlockSpec((1, s, w), lambda i: (i, 0, 0)),
                   pl.BlockSpec((1, s, HEADS * w), lambda i: (i, 0, 0))],
        out_shape=[jax.ShapeDtypeStruct((b, s, w), F32), jax.ShapeDtypeStruct((b, s, HEADS * w), F32)],
        compiler_params=_cparams("parallel"),
        name="small_prep",
    )(proj3, params)


def _hgrn2_kernel(q_ref, f_ref, i_ref, g_ref, lb_ref, o_ref, st_ref, qs, ks, vs, bs, os_):
    nb = q_ref.shape[0]
    T = q_ref.shape[1]
    C = HGRN_CHUNK
    seqs = range(nb)

    @pl.when(pl.program_id(1) == 0)
    def _():
        st_ref[...] = jnp.zeros_like(st_ref)

    same_head = _head_ones()
    ones_bf = jnp.where(same_head, 1.0, 0.0).astype(BF16)
    lb = lb_ref[...]
    log_lb = jnp.log(lb)
    r = _iota((T, T), 0)
    c = _iota((T, T), 1)
    tril_chunk = jnp.where((r >= c) & (r // C == c // C), 1.0, 0.0).astype(BF16)
    for bb in seqs:
        fl = f_ref[bb]
        c2 = jnp.log1p(-lb) + _log_sigmoid(fl)
        mx = jnp.maximum(log_lb, c2)
        log_f = mx + jnp.log1p(jnp.exp(-jnp.abs(log_lb - c2)))
        bs[bb] = _dot_exact_lhs(tril_chunk, log_f)
        qs[bb] = _silu(q_ref[bb])
        ks[bb] = (1.0 - lb) * _sigmoid(-fl)
        vs[bb] = i_ref[bb]

    H8 = C // 2
    trow8 = _iota((H8, GROUP_WIDTH), 0)

    def chunk(ci, carry):
        r0 = pl.multiple_of(ci * C, C)
        qc = [qs[bb, pl.ds(r0, C), :] for bb in seqs]
        kc = [ks[bb, pl.ds(r0, C), :] for bb in seqs]
        vc = [vs[bb, pl.ds(r0, C), :] for bb in seqs]
        bc = [bs[bb, pl.ds(r0, C), :] for bb in seqs]
        st = [st_ref[bb] for bb in seqs]
        a = []
        for bb in seqs:
            parts = []
            for half, s_range in ((0, range(H8)), (1, range(C))):
                b_half = bc[bb][half * H8:(half + 1) * H8, :]
                q_half = qc[bb][half * H8:(half + 1) * H8, :]
                for s in s_range:
                    blk = jnp.exp(b_half - bc[bb][s:s + 1, :]) * (q_half * kc[bb][s:s + 1, :])
                    if s >= half * H8:
                        blk = jnp.where(trow8 >= s - half * H8, blk, 0.0)
                    parts.append(blk)
            a.append(jnp.concatenate(parts, axis=0).astype(BF16))
        sc = [_dot(a[bb], ones_bf) for bb in seqs]
        o = [_dot_nt((qc[bb] * jnp.exp(bc[bb])).astype(BF16), st[bb].astype(BF16)) for bb in seqs]
        b_last = [bc[bb][C - 1:C, :] for bb in seqs]
        upd = [_dot_tn(vc[bb].astype(BF16), (kc[bb] * jnp.exp(b_last[bb] - bc[bb])).astype(BF16)) for bb in seqs]
        for bb in seqs:
            top = o[bb][0:H8, :]
            bot = o[bb][H8:C, :]
            for s in range(H8):
                top = top + sc[bb][s * H8:(s + 1) * H8, :] * vc[bb][s:s + 1, :]
            for s in range(C):
                bot = bot + sc[bb][(H8 + s) * H8:(H8 + s + 1) * H8, :] * vc[bb][s:s + 1, :]
            os_[bb, pl.ds(r0, C), :] = jnp.concatenate([top, bot], axis=0)
            st_ref[bb] = st[bb] * jnp.exp(b_last[bb]) + jnp.where(same_head, upd[bb], 0.0)
        return carry

    lax.fori_loop(0, T // C, chunk, 0)
    for bb in seqs:
        o_ref[bb] = _head_rms(os_[bb], ones_bf) * _silu(g_ref[bb])


def hgrn2(proj3, lb):
    b, s, _ = proj3.shape
    t = SEQ_TILE
    gw = GROUP_WIDTH
    nb = HGRN_BATCH
    col = lambda cidx: pl.BlockSpec((nb, t, gw), lambda i, j: (i, j, cidx))
    return pl.pallas_call(
        _hgrn2_kernel,
        grid=(b // nb, s // t),
        in_specs=[col(COL_AQ), col(COL_AF), col(COL_AI), col(COL_AG),
                  pl.BlockSpec((1, gw), lambda i, j: (0, 0))],
        out_specs=pl.BlockSpec((nb, t, gw), lambda i, j: (i, j, 0)),
        out_shape=jax.ShapeDtypeStruct((b, s, gw), F32),
        scratch_shapes=[pltpu.VMEM((nb, gw, gw), F32)] + [pltpu.VMEM((nb, t, gw), F32)] * 5,
        compiler_params=_cparams("parallel", "arbitrary"),
        name="hgrn2",
    )(proj3, proj3, proj3, proj3, lb.reshape(1, gw))


def _fox_kernel(q_ref, k_ref, v_ref, cx_ref, o_ref, kb_ref, vt_ref, acc_ref):
    nb = q_ref.shape[0]
    tq = q_ref.shape[1]
    tk = tq
    S = k_ref.shape[1]
    qi = pl.program_id(1)
    seqs = range(nb)

    @pl.when(qi == 0)
    def _():
        for bb in seqs:
            for i in range(S // tk):
                sl = pl.ds(i * tk, tk)
                kb_ref[bb, sl, :] = k_ref[bb, sl, :].astype(BF16)
                vt_ref[bb, :, sl] = v_ref[bb, sl, :].T.astype(BF16)

    lane_head = _iota((tq, GROUP_WIDTH), 1) // HEAD_DIM
    qh = []
    for bb in seqs:
        qsc = q_ref[bb] * (HEAD_DIM ** -0.5)
        qh.append([jnp.where(lane_head == h, qsc, 0.0).astype(BF16) for h in range(HEADS)])
    acc_ref[...] = jnp.zeros_like(acc_ref)
    reps = tq // SMALL_W
    causal = _iota((tk, tq), 0) <= _iota((tk, tq), 1)

    def step(kb, stats, masked):
        k0 = pl.multiple_of(kb * tk, tk)
        new_stats = [[None] * HEADS for _ in seqs]
        strips = [(h, bb) for h in range(HEADS) for bb in seqs]

        def scores(h, bb):
            return _dot_nt(kb_ref[bb, pl.ds(k0, tk), :], qh[bb][h])

        ahead = 3
        queue = [scores(*strips[j]) for j in range(ahead)]
        for i, (h, bb) in enumerate(strips):
            st = queue.pop(0)
            if i + ahead < len(strips):
                queue.append(scores(*strips[i + ahead]))
            rows = slice(h * HEAD_DIM, (h + 1) * HEAD_DIM)
            cx = cx_ref[bb, pl.ds(k0, tk), h * SMALL_W:(h + 1) * SMALL_W]
            st = st - jnp.concatenate([cx] * reps, axis=1)
            if masked:
                st = jnp.where(causal, st, -jnp.inf)
            m_old, l_old = stats[bb][h]
            m_new = jnp.maximum(m_old, jnp.max(st, axis=0, keepdims=True))
            alpha = jnp.exp(m_old - m_new)
            p = jnp.exp(st - m_new)
            l_new = alpha * l_old + jnp.sum(p, axis=0, keepdims=True)
            pv = _dot(vt_ref[bb, rows, pl.ds(k0, tk)], p.astype(BF16))
            acc_ref[bb, rows, :] = alpha * acc_ref[bb, rows, :] + pv
            new_stats[bb][h] = (m_new, l_new)
        return tuple(tuple(new_stats[bb]) for bb in seqs)

    init = tuple(tuple((jnp.full((1, tq), -jnp.inf, F32), jnp.zeros((1, tq), F32)) for _ in range(HEADS)) for _ in seqs)
    stats = lax.fori_loop(0, qi, lambda kb, c: step(kb, c, False), init)
    stats = step(qi, stats, True)

    ones_bf = jnp.where(_head_ones(), 1.0, 0.0).astype(BF16)
    for bb in seqs:
        inv_l = jnp.concatenate([jnp.broadcast_to(1.0 / stats[bb][h][1], (HEAD_DIM, tq)) for h in range(HEADS)], axis=0)
        o_ref[bb] = _head_rms((acc_ref[bb] * inv_l).T, ones_bf)


def fox_attention(proj3, c_exp):
    b, s, _ = proj3.shape
    tq = SEQ_TILE
    gw = GROUP_WIDTH
    nb = FOX_BATCH
    return pl.pallas_call(
        _fox_kernel,
        grid=(b // nb, s // tq),
        in_specs=[pl.BlockSpec((nb, tq, gw), lambda i, j: (i, j, COL_BQ)),
                  pl.BlockSpec((nb, s, gw), lambda i, j: (i, 0, COL_BK)),
                  pl.BlockSpec((nb, s, gw), lambda i, j: (i, 0, COL_BV)),
                  pl.BlockSpec((nb, s, HEADS * SMALL_W), lambda i, j: (i, 0, 0))],
        out_specs=pl.BlockSpec((nb, tq, gw), lambda i, j: (i, j, 0)),
        out_shape=jax.ShapeDtypeStruct((b, s, gw), F32),
        scratch_shapes=[pltpu.VMEM((nb, s, gw), BF16),
                        pltpu.VMEM((nb, gw, s), BF16),
                        pltpu.VMEM((nb, gw, tq), F32)],
        compiler_params=_cparams("parallel", "arbitrary"),
        name="fox_attention",
    )(proj3, proj3, proj3, c_exp)


def _rglru_kernel(x_ref, g_ref, cw_ref, cb_ref, wa_ref, ba_ref, wx_ref, bx_ref, lam_ref, o_ref, prev_ref, h_ref):
    @pl.when(pl.program_id(1) == 0)
    def _():
        prev_ref[0:8, :] = jnp.zeros((8, prev_ref.shape[1]), F32)
        h_ref[...] = jnp.zeros_like(h_ref)

    x = x_ref[0]
    t = x.shape[0]
    xc = _causal_conv4(x, prev_ref, cw_ref[...]) + cb_ref[...]
    xb = xc.astype(BF16)
    r = _sigmoid(_dot(xb, wa_ref[...]) + ba_ref[...])
    ig = _sigmoid(_dot(xb, wx_ref[...]) + bx_ref[...])
    log_a = (-LRU_C * r) * _softplus(-lam_ref[...])
    a = jnp.exp(log_a)
    u = jnp.sqrt(1.0 - jnp.exp(2.0 * log_a)) * (ig * xc)
    row = _iota(a.shape, 0)
    d = 1
    while d < t:
        valid = row >= d
        u = jnp.where(valid, a * pltpu.roll(u, d, 0) + u, u)
        a = jnp.where(valid, a * pltpu.roll(a, d, 0), a)
        d *= 2
    h = a * h_ref[...] + u
    h_ref[...] = h[t - 1:t, :]
    ones_bf = jnp.where(_head_ones(), 1.0, 0.0).astype(BF16)
    o_ref[0] = _head_rms(h * _gelu_tanh(g_ref[0]), ones_bf)


def rglru(proj3, conv_w, conv_b, wa_bd, ba, wx_bd, bx, lam):
    b, s, _ = proj3.shape
    t = ROW_TILE
    gw = GROUP_WIDTH
    row = lambda: pl.BlockSpec((1, gw), lambda i, j: (0, 0))
    return pl.pallas_call(
        _rglru_kernel,
        grid=(b, s // t),
        in_specs=[pl.BlockSpec((1, t, gw), lambda i, j: (i, j, COL_CX)),
                  pl.BlockSpec((1, t, gw), lambda i, j: (i, j, COL_CG)),
                  pl.BlockSpec((CONV_W, gw), lambda i, j: (0, 0)), row(),
                  pl.BlockSpec((gw, gw), lambda i, j: (0, 0)), row(),
                  pl.BlockSpec((gw, gw), lambda i, j: (0, 0)), row(), row()],
        out_specs=pl.BlockSpec((1, t, gw), lambda i, j: (i, j, 0)),
        out_shape=jax.ShapeDtypeStruct((b, s, gw), F32),
        scratch_shapes=[pltpu.VMEM((8 + t, gw), F32), pltpu.VMEM((1, gw), F32)],
        compiler_params=_cparams("parallel", "arbitrary"),
        name="rglru",
    )(proj3, proj3, conv_w, conv_b.reshape(1, gw), wa_bd, ba.reshape(1, gw), wx_bd, bx.reshape(1, gw), lam.reshape(1, gw))


def _gdn_kernel(q_ref, k_ref, v_ref, z_ref, sm_ref, cw_ref, o_ref, st_ref, pq_ref, pk_ref, pv_ref):
    nb = q_ref.shape[0]
    T = q_ref.shape[1]
    C = GDN_CHUNK

    @pl.when(pl.program_id(1) == 0)
    def _():
        st_ref[...] = jnp.zeros_like(st_ref)
        for pad_ref in (pq_ref, pk_ref, pv_ref):
            pad_ref[:, 0:8, :] = jnp.zeros((nb, 8, pad_ref.shape[2]), F32)

    same_head = _head_ones()
    ones_bf = jnp.where(same_head, 1.0, 0.0).astype(BF16)
    cw = cw_ref[...]
    gw = GROUP_WIDTH
    er = _iota((SMALL_W, gw), 0)
    ec = _iota((SMALL_W, gw), 1) // HEAD_DIM
    exp_beta = jnp.where(er == LANE_BETA + ec, 1.0, 0.0).astype(BF16)
    exp_g = jnp.where(er == LANE_GDEC + ec, 1.0, 0.0).astype(BF16)

    prep = []
    for bb in range(nb):
        xq, xk, xv = q_ref[bb], k_ref[bb], v_ref[bb]
        q = _silu(_causal_conv4(xq, pq_ref.at[bb], cw[:, 0:gw]))
        k = _silu(_causal_conv4(xk, pk_ref.at[bb], cw[:, gw:2 * gw]))
        v = _silu(_causal_conv4(xv, pv_ref.at[bb], cw[:, 2 * gw:3 * gw]))
        q = q * lax.rsqrt(_dot_exact_rhs(q * q, ones_bf) + EPS) * (HEAD_DIM ** -0.5)
        k = k * lax.rsqrt(_dot_exact_rhs(k * k, ones_bf) + EPS)
        sm = sm_ref[bb]
        betax = _dot_exact_rhs(sm, exp_beta)
        gx = _dot_exact_rhs(sm, exp_g)
        prep.append((q, k, v, betax, gx))

    n_chunks = T // C
    trow = _iota((C, gw), 0)
    scol = _iota((C, gw), 1) % HEAD_DIM
    incl = scol <= trow
    strict = scol < trow
    eye = jnp.where(scol == trow, 1.0, 0.0)

    def bdiag(x_cat):
        return _stack_heads(x_cat).astype(BF16)

    chains = []
    for ci in range(n_chunks):
        sl = slice(ci * C, (ci + 1) * C)
        for bb in range(nb):
            q, k, v, betax, gx = prep[bb]
            qc, kc, vc, bx, gc = q[sl], k[sl], v[sl], betax[sl], gx[sl]
            grow = jnp.sum(jnp.where(scol == trow, gc, 0.0), axis=0, keepdims=True)
            eg = jnp.exp(gc)
            kb = kc * bx
            g_last = gc[C - 1:C, :]
            gamma = jnp.where(incl, jnp.exp(jnp.where(incl, gc - grow, 0.0)), 0.0)
            ks_bf = bdiag(kc)
            m = jnp.where(strict, _dot_nt(kb.astype(BF16), ks_bf) * gamma, 0.0)
            chains.append(dict(
                x=eye - m, y=m,
                a_qk=(_dot_nt(qc.astype(BF16), ks_bf) * gamma).astype(BF16),
                rhs_u=bdiag(vc * bx),
                rhs_w=bdiag(kb * eg),
                q_dec=(qc * eg).astype(BF16),
                k_dec=(kc * jnp.exp(g_last - gc)).astype(BF16),
                decay=jnp.exp(g_last)))
    for ch in chains:
        ch["ybd"] = bdiag(ch["y"])
    for _ in range(5):
        for ch in chains:
            ch["y"] = _dot(ch["y"].astype(BF16), ch["ybd"])
            ch["ybd"] = bdiag(ch["y"])
        for ch in chains:
            ch["x"] = ch["x"] + _dot(ch["x"].astype(BF16), ch["ybd"])
    for ch in chains:
        t_inv = ch["x"].astype(BF16)
        ch["u"] = _dot(t_inv, ch["rhs_u"])
        ch["wk"] = _dot(t_inv, ch["rhs_w"]).astype(BF16)

    outs = [[] for _ in range(nb)]
    for ci in range(n_chunks):
        for bb in range(nb):
            ch = chains[ci * nb + bb]
            st = st_ref[bb]
            st_bf = st.astype(BF16)
            v_new = ch["u"] - _dot(ch["wk"], st_bf)
            outs[bb].append(_dot(ch["q_dec"], st_bf) + _dot(ch["a_qk"], bdiag(v_new)))
            upd = _dot_tn(ch["k_dec"], v_new.astype(BF16))
            st_ref[bb] = st * ch["decay"] + jnp.where(same_head, upd, 0.0)
    for bb in range(nb):
        o = jnp.concatenate(outs[bb], axis=0)
        o_ref[bb] = _head_rms(o, ones_bf) * _silu(z_ref[bb])


def gdn(proj3, sm, conv_w):
    b, s, _ = proj3.shape
    t = GDN_TILE
    gw = GROUP_WIDTH
    nb = GDN_BATCH
    col = lambda cidx: pl.BlockSpec((nb, t, gw), lambda i, j: (i, j, cidx))
    return pl.pallas_call(
        _gdn_kernel,
        grid=(b // nb, s // t),
        in_specs=[col(COL_DQ), col(COL_DK), col(COL_DV), col(COL_DZ),
                  pl.BlockSpec((nb, t, SMALL_W), lambda i, j: (i, j, 0)),
                  pl.BlockSpec((CONV_W, 3 * gw), lambda i, j: (0, 0))],
        out_specs=pl.BlockSpec((nb, t, gw), lambda i, j: (i, j, 0)),
        out_shape=jax.ShapeDtypeStruct((b, s, gw), F32),
        scratch_shapes=[pltpu.VMEM((nb, gw, gw), F32)] + [pltpu.VMEM((nb, 8 + t, gw), F32)] * 3,
        compiler_params=_cparams("parallel", "arbitrary"),
        name="gdn",
    )(proj3, proj3, proj3, proj3, sm, conv_w)


def _mix_out(x, ys, gain, w_ref):
    gw = GROUP_WIDTH
    acc = x
    for i, y in enumerate(ys):
        yg = (y * gain[:, i * gw:(i + 1) * gw]).astype(BF16)
        acc = acc + _dot(yg, w_ref[i * gw:(i + 1) * gw, :])
    return acc


def _mem_attention(x, g, wq_ref, kv, wo_ref):
    t = x.shape[0]
    gw = GROUP_WIDTH
    ms = jnp.mean(x * x, axis=-1, keepdims=True)
    h = ((x * lax.rsqrt(ms + EPS)) * g).astype(BF16)
    q = _dot(h, wq_ref[...])
    k = kv[:, 0:gw].astype(BF16)
    v = kv[:, gw:2 * gw].astype(BF16)
    s = _dot_nt(k, _stack_heads(q).astype(BF16)) * (HEAD_DIM ** -0.5)
    s = s - jnp.max(s, axis=0, keepdims=True)
    p = jnp.exp(s)
    p = p * (1.0 / jnp.sum(p, axis=0, keepdims=True))
    ot = _dot_tn(v, p.astype(BF16))
    oc = jnp.concatenate([ot[hh * HEAD_DIM:(hh + 1) * HEAD_DIM, hh * t:(hh + 1) * t] for hh in range(HEADS)], axis=0).T
    return x + _dot(oc.astype(BF16), wo_ref[...])


def _post_mix_kernel(x_ref, ya_ref, yb_ref, yc_ref, yd_ref, gain_ref, wout_ref, gm_ref, wq_ref, kv_ref, wo_ref,
                     g_ref, w_ref, b_ref, x2_ref, h_ref, r_ref, rt_ref, n_ref, cnt_ref):
    @pl.when((pl.program_id(0) == 0) & (pl.program_id(1) == 0))
    def _():
        cnt_ref[...] = jnp.zeros_like(cnt_ref)

    x1 = _mix_out(x_ref[0], (ya_ref[0], yb_ref[0], yc_ref[0], yd_ref[0]), gain_ref[...], wout_ref)
    x = _mem_attention(x1, gm_ref[...], wq_ref, kv_ref[0], wo_ref)
    x2_ref[0] = x
    tm = x.shape[0]
    ms = jnp.mean(x * x, axis=-1, keepdims=True)
    h = (x * lax.rsqrt(ms + EPS)) * g_ref[...]
    h_ref[...] = _pack_bf16_pairs(h)
    hh, hm, _ = _split3(h)
    w = w_ref[...]
    wh, wm, _ = _split3(w)
    logits = _dot(hh, wh) + _dot(hh, wm) + _dot(hm, wh) + b_ref[...]
    lane_i = _iota(logits.shape, 1)
    lane = lane_i.astype(F32)
    lane_grp = (lane_i // EXPERTS_PER_GROUP).astype(F32)
    big = jnp.float32(1e9)
    neg = -jnp.inf
    is_grp = (lane_i >= N_EXPERTS) & (lane_i < N_EXPERTS + N_EXPERT_GROUPS)
    gl = jnp.where(is_grp, logits, neg)
    gmax = jnp.max(gl, axis=-1, keepdims=True)
    p_grp = 1.0 / jnp.sum(jnp.exp(gl - gmax), axis=-1, keepdims=True)
    g_sel = jnp.min(jnp.where(gl == gmax, lane, big), axis=-1, keepdims=True) - N_EXPERTS
    in_grp = (lane_i < N_EXPERTS) & (lane_grp == g_sel)
    el = jnp.where(in_grp, logits, neg)
    m1 = jnp.max(el, axis=-1, keepdims=True)
    i1 = jnp.min(jnp.where(el == m1, lane, big), axis=-1, keepdims=True)
    el2 = jnp.where(lane == i1, neg, el)
    m2 = jnp.max(el2, axis=-1, keepdims=True)
    i2 = jnp.min(jnp.where(el2 == m2, lane, big), axis=-1, keepdims=True)
    e21 = jnp.exp(m2 - m1)
    w1 = 1.0 / (1.0 + e21)
    w2 = e21 / (1.0 + e21)
    oh1 = lane == i1
    oh2 = lane == i2
    onehot = jnp.where(oh1 | oh2, 1.0, 0.0)
    strict = jnp.where(_iota((tm, tm), 0) > _iota((tm, tm), 1), 1.0, 0.0).astype(BF16)
    before = _dot(strict, onehot.astype(BF16)) + cnt_ref[...]
    rank1 = jnp.sum(jnp.where(oh1, before, 0.0), axis=-1, keepdims=True)
    rank2 = jnp.sum(jnp.where(oh2, before, 0.0), axis=-1, keepdims=True)
    cnt = before[tm - 1:tm, :] + onehot[tm - 1:tm, :]
    cnt_ref[...] = cnt
    n_ref[...] = jnp.broadcast_to(cnt, n_ref.shape)
    cols = (i1, i2, p_grp * w1, p_grp * w2, rank1, rank2)
    route = jnp.zeros(logits.shape, F32)
    for j, col in enumerate(cols):
        route = jnp.where(lane_i == j, col, route)
    r_ref[...] = route
    rt_ref[...] = route.T[0:8, :]


def post_mix(x3, ya, yb, yc, yd, gain, w_out_bf16, g_mem, wq_bf16, kv3, wo_bf16, g_ffn, w_router, b_router):
    b, s, d = x3.shape
    t = ROW_TILE
    gw = GROUP_WIDTH
    nt = s // t
    mlen = kv3.shape[1]
    tile = lambda w: pl.BlockSpec((1, t, w), lambda i, j: (i, j, 0))
    const = lambda shape: pl.BlockSpec(shape, lambda i, j: (0,) * len(shape))
    tok = lambda w: pl.BlockSpec((t, w), lambda i, j: (i * nt + j, 0))
    return pl.pallas_call(
        _post_mix_kernel,
        grid=(b, nt),
        in_specs=[tile(d), tile(gw), tile(gw), tile(gw), tile(gw),
                  const((1, 4 * gw)), const((4 * gw, d)),
                  const((1, d)), const((d, gw)), pl.BlockSpec((1, mlen, 2 * gw), lambda i, j: (i, 0, 0)), const((gw, d)),
                  const((1, d)), const((d, SMALL_W)), const((1, SMALL_W))],
        out_specs=[tile(d), tok(d // 2), tok(SMALL_W), pl.BlockSpec((8, t), lambda i, j: (0, i * nt + j)),
                   const((8, SMALL_W))],
        out_shape=[jax.ShapeDtypeStruct((b, s, d), F32), jax.ShapeDtypeStruct((b * s, d // 2), jnp.uint32),
                   jax.ShapeDtypeStruct((b * s, SMALL_W), F32), jax.ShapeDtypeStruct((8, b * s), F32),
                   jax.ShapeDtypeStruct((8, SMALL_W), F32)],
        scratch_shapes=[pltpu.VMEM((1, SMALL_W), F32)],
        compiler_params=_cparams("arbitrary", "arbitrary"),
        name="post_mix",
    )(x3, ya, yb, yc, yd, gain.reshape(1, 4 * gw), w_out_bf16, g_mem.reshape(1, d), wq_bf16, kv3, wo_bf16,
      g_ffn.reshape(1, d), w_router, b_router)


def _experts_kernel(te_ref, nu_ref, run_ref, nxt_ref, xs_ref, wgu_hbm, wdn_hbm, y_ref,
                    wgu_buf, wdn_buf, wgu_bf, wdn_bf, sem, *, layer):
    i = pl.program_id(0)

    def weight_copies(expert, slot):
        return (pltpu.make_async_copy(wgu_hbm.at[layer, expert], wgu_buf.at[slot], sem.at[slot, 0]),
                pltpu.make_async_copy(wdn_hbm.at[layer, expert], wdn_buf.at[slot], sem.at[slot, 1]))

    @pl.when(i >= nu_ref[0])
    def _():
        y_ref[...] = jnp.zeros_like(y_ref)

    @pl.when(i < nu_ref[0])
    def _():
        expert = te_ref[i]
        slot = run_ref[i] % 2

        @pl.when(i == 0)
        def _():
            for c in weight_copies(expert, slot):
                c.start()

        @pl.when((i == 0) | (te_ref[jnp.maximum(i - 1, 0)] != expert))
        def _():
            for c in weight_copies(expert, slot):
                c.wait()
            nxt = nxt_ref[i]

            @pl.when(nxt >= 0)
            def _():
                for c in weight_copies(nxt, 1 - slot):
                    c.start()

            wgu_bf[...] = wgu_buf[slot].astype(BF16)
            wdn_bf[...] = wdn_buf[slot].astype(BF16)

        gu = _dot(_unpack_bf16_pairs(xs_ref[...]).astype(BF16), wgu_bf[...])
        act = _silu(gu[:, 0:D_EXPERT]) * gu[:, D_EXPERT:2 * D_EXPERT]
        y_ref[...] = _pack_bf16_pairs(_dot(act.astype(BF16), wdn_bf[...]))


def moe_experts(xs, tile_expert, n_used, run_index, next_expert, w_gu, w_dn, layer, tm):
    n_rows, dp = xs.shape
    d = 2 * dp
    return pl.pallas_call(
        functools.partial(_experts_kernel, layer=layer),
        grid_spec=pltpu.PrefetchScalarGridSpec(
            num_scalar_prefetch=4,
            grid=(n_rows // tm,),
            in_specs=[pl.BlockSpec((tm, dp), lambda i, te, nu, run, nxt: (jnp.minimum(i, nu[0] - 1), 0)),
                      pl.BlockSpec(memory_space=pl.ANY),
                      pl.BlockSpec(memory_space=pl.ANY)],
            out_specs=pl.BlockSpec((tm, dp), lambda i, te, nu, run, nxt: (i, 0)),
            scratch_shapes=[pltpu.VMEM((2, d, 2 * D_EXPERT), F32), pltpu.VMEM((2, D_EXPERT, d), F32),
                            pltpu.VMEM((d, 2 * D_EXPERT), BF16), pltpu.VMEM((D_EXPERT, d), BF16),
                            pltpu.SemaphoreType.DMA((2, 2))]),
        out_shape=jax.ShapeDtypeStruct((n_rows, dp), jnp.uint32),
        compiler_params=_cparams("arbitrary"),
        name="moe_experts",
    )(tile_expert, n_used, run_index, next_expert, xs, w_gu, w_dn)


def sc_gather_rows(table, idx):
    n_idx = idx.shape[0]
    width = table.shape[1]
    info = plsc.get_sparse_core_info()
    n_workers = info.num_cores * info.num_subcores
    per_worker = n_idx // n_workers
    chunk = SC_GATHER_CHUNK
    assert n_idx % (n_workers * chunk) == 0 and width % info.num_lanes == 0
    mesh = plsc.VectorSubcoreMesh(core_axis_name="c", subcore_axis_name="s")

    def body(table_hbm, idx_hbm, out_hbm, idx_v, rows_v, sem):
        wid = lax.axis_index("s") * info.num_cores + lax.axis_index("c")
        base = wid * per_worker

        @pl.loop(0, per_worker // chunk)
        def _(j):
            off = pl.multiple_of(base + j * chunk, chunk)
            pltpu.sync_copy(idx_hbm.at[pl.ds(off, chunk)], idx_v)
            pltpu.async_copy(table_hbm.at[idx_v], rows_v, sem).wait()
            pltpu.sync_copy(rows_v, out_hbm.at[pl.ds(off, chunk)])

    return pl.kernel(
        body,
        out_type=jax.ShapeDtypeStruct((n_idx, width), table.dtype),
        mesh=mesh,
        scratch_types=[pltpu.VMEM((chunk,), jnp.int32), pltpu.VMEM((chunk, width), table.dtype),
                       pltpu.SemaphoreType.DMA],
        name="sc_gather_rows",
    )(table, idx)


def sc_dispatch_rows(rows, dest, pad_idx, zero_rows):
    m, width = rows.shape
    n_pad = pad_idx.shape[0]
    info = plsc.get_sparse_core_info()
    n_workers = info.num_cores * info.num_subcores
    chunk = SC_GATHER_CHUNK
    assert m % (n_workers * chunk) == 0 and n_pad % (n_workers * chunk) == 0 and zero_rows.shape == (chunk, width)
    rows_per_worker = m // n_workers
    pad_per_worker = n_pad // n_workers
    mesh = plsc.VectorSubcoreMesh(core_axis_name="c", subcore_axis_name="s")

    def body(rows_hbm, dest_hbm, pad_hbm, zero_hbm, out_hbm, idx_v, rows_v, sem):
        wid = lax.axis_index("s") * info.num_cores + lax.axis_index("c")

        @pl.loop(0, rows_per_worker // chunk)
        def _(j):
            off = pl.multiple_of(wid * rows_per_worker + j * chunk, chunk)
            pltpu.sync_copy(rows_hbm.at[pl.ds(off, chunk)], rows_v)
            for slot in range(2):
                pltpu.sync_copy(dest_hbm.at[pl.ds(slot * m + off, chunk)], idx_v)
                pltpu.async_copy(rows_v, out_hbm.at[idx_v], sem).wait()

        pltpu.sync_copy(zero_hbm, rows_v)

        @pl.loop(0, pad_per_worker // chunk)
        def _(j):
            off = pl.multiple_of(wid * pad_per_worker + j * chunk, chunk)
            pltpu.sync_copy(pad_hbm.at[pl.ds(off, chunk)], idx_v)
            pltpu.async_copy(rows_v, out_hbm.at[idx_v], sem).wait()

    return pl.kernel(
        body,
        out_type=jax.ShapeDtypeStruct((2 * m + n_pad, width), rows.dtype),
        mesh=mesh,
        scratch_types=[pltpu.VMEM((chunk,), jnp.int32), pltpu.VMEM((chunk, width), rows.dtype),
                       pltpu.SemaphoreType.DMA],
        name="sc_dispatch_rows",
    )(rows, dest, pad_idx, zero_rows)


def _combine_rows_kernel(x_ref, r_ref, g_ref, y1_ref, y2_ref, o_ref, *, final_norm):
    r = r_ref[...]
    out = x_ref[...] + r[:, 2:3] * _unpack_bf16_pairs(y1_ref[...]) + r[:, 3:4] * _unpack_bf16_pairs(y2_ref[...])
    if final_norm:
        ms = jnp.mean(out * out, axis=-1, keepdims=True)
        out = (out * lax.rsqrt(ms + EPS)) * g_ref[...]
    o_ref[...] = out


def moe_combine_rows(x2, route, y_rows, g_final, tm, final_norm):
    m, d = x2.shape
    steps = m // tm
    return pl.pallas_call(
        functools.partial(_combine_rows_kernel, final_norm=final_norm),
        grid=(steps,),
        in_specs=[pl.BlockSpec((tm, d), lambda i: (i, 0)),
                  pl.BlockSpec((tm, SMALL_W), lambda i: (i, 0)),
                  pl.BlockSpec((1, d), lambda i: (0, 0)),
                  pl.BlockSpec((tm, d // 2), lambda i: (i, 0)),
                  pl.BlockSpec((tm, d // 2), lambda i: (i + steps, 0))],
        out_specs=pl.BlockSpec((tm, d), lambda i: (i, 0)),
        out_shape=jax.ShapeDtypeStruct((m, d), F32),
        compiler_params=_cparams("parallel"),
        name="moe_combine_rows",
    )(x2, route, g_final.reshape(1, d), y_rows, y_rows)


def moe_routed(x2, h, route, route_t, counts, w_gu, w_dn, layer, g_final, final_norm):
    m, d = x2.shape
    te_rows = EXPERT_TILE
    n_rows = 2 * m + N_EXPERTS * te_rows
    cnt = counts[0, 0:N_EXPERTS].astype(jnp.int32)
    padded = (cnt + te_rows - 1) // te_rows * te_rows
    seg_end = jnp.cumsum(padded)
    seg_start = seg_end - padded
    idx = route_t[0:2].astype(jnp.int32)
    onehot = idx[:, :, None] == jnp.arange(N_EXPERTS, dtype=jnp.int32)
    dest = (jnp.sum(jnp.where(onehot, seg_start, 0), axis=-1) + route_t[4:6].astype(jnp.int32)).reshape(2 * m)
    tile_start = jnp.arange(n_rows // te_rows, dtype=jnp.int32) * te_rows
    tile_expert = jnp.minimum(jnp.sum((tile_start[:, None] >= seg_end[None, :]).astype(jnp.int32), axis=1), N_EXPERTS - 1)
    n_used = (seg_end[N_EXPERTS - 1] // te_rows).reshape(1)
    n_pad = n_rows - 2 * m
    experts = jnp.arange(N_EXPERTS, dtype=jnp.int32)
    pad_end = jnp.cumsum(padded - cnt)
    k = jnp.arange(n_pad, dtype=jnp.int32)
    k_expert = jnp.sum((k[:, None] >= pad_end[None, :]).astype(jnp.int32), axis=1)
    first_pad = seg_start + cnt - (pad_end - (padded - cnt))
    in_segment = jnp.sum(jnp.where(k_expert[:, None] == experts[None, :], first_pad[None, :], 0), axis=1) + k
    pad_idx = jnp.where(k_expert < N_EXPERTS, in_segment, seg_end[N_EXPERTS - 1] + k - pad_end[N_EXPERTS - 1])
    xs = sc_dispatch_rows(h, dest, pad_idx, jnp.zeros((SC_GATHER_CHUNK, d // 2), h.dtype))
    n_tiles = n_rows // te_rows
    tile_ids = jnp.arange(n_tiles, dtype=jnp.int32)
    prev_expert = jnp.concatenate([jnp.full((1,), -1, jnp.int32), tile_expert[:-1]])
    run_index = jnp.maximum(jnp.cumsum(((tile_expert != prev_expert) & (tile_ids < n_used[0])).astype(jnp.int32)) - 1, 0)
    next_tile = jnp.take(seg_end, tile_expert) // te_rows
    next_expert = jnp.where(next_tile < n_used[0], jnp.take(tile_expert, jnp.minimum(next_tile, n_tiles - 1)), -1)
    ys = moe_experts(xs, tile_expert, n_used, run_index, next_expert, w_gu, w_dn, layer, tm=te_rows)
    y_rows = sc_gather_rows(ys, dest)
    return moe_combine_rows(x2, route, y_rows, g_final, tm=ROW_TILE, final_norm=final_norm)


def _w_in_prep_kernel(w_ref, o_ref):
    gw = GROUP_WIDTH
    a_end = 7 * gw
    c_start = a_end + HEADS
    c_end = c_start + 5 * gw
    z_start = c_end + 2 * HEADS
    assert a_end % SMALL_W == LANE_FOX and c_end % SMALL_W == LANE_BETA and LANE_GDEC == LANE_BETA + HEADS
    w = w_ref[...]
    o_ref[:, 0:a_end] = w[:, 0:a_end].astype(BF16)
    o_ref[:, a_end:a_end + 5 * gw] = w[:, c_start:c_end].astype(BF16)
    o_ref[:, 12 * gw:13 * gw] = w[:, z_start:z_start + gw].astype(BF16)
    blk_fox = w[:, a_end:a_end + SMALL_W]
    blk_gdn = w[:, c_end - LANE_BETA:c_end - LANE_BETA + SMALL_W]
    lane = _iota(blk_fox.shape, 1)
    small = jnp.where(lane < LANE_BETA, blk_fox, jnp.where(lane < LANE_GDEC + HEADS, blk_gdn, 0.0))
    o_ref[:, 13 * gw:13 * gw + SMALL_W] = small.astype(BF16)
    o_ref[:, 13 * gw + SMALL_W:] = jnp.zeros((w.shape[0], SMALL_W), BF16)


def w_in_prep(w_all, layer):
    _, d, n = w_all.shape
    tr = 128
    return pl.pallas_call(
        _w_in_prep_kernel,
        grid=(d // tr,),
        in_specs=[pl.BlockSpec((None, tr, n), lambda i: (layer, i, 0))],
        out_specs=pl.BlockSpec((tr, N_PROJ), lambda i: (i, 0)),
        out_shape=jax.ShapeDtypeStruct((d, N_PROJ), BF16),
        compiler_params=_cparams("parallel"),
        name="w_in_prep",
    )(w_all)


def _block_diag(w):
    h = w.shape[0]
    eye = jnp.eye(h, dtype=w.dtype)
    return (eye[:, None, :, None] * w[:, :, None, :]).reshape(h * HEAD_DIM, h * HEAD_DIM)


def kernel(x, mem, norm_mix, w_in, hgrn_lb, fox_fb, lru_conv_w, lru_conv_b, lru_wa, lru_ba, lru_wx, lru_bx, lru_lam, gdn_conv_w, gdn_a_log, gdn_dt_bias, mix_gain, w_out, norm_mem, norm_memkv, w_mq, w_mkv, w_mo, norm_ffn, w_rg, b_rg, w_re, b_re, w_e_gu, w_e_dn, norm_final):
    b, s, d = x.shape
    depth = w_in.shape[0]
    t_tok = b * s
    mlen = mem.shape[1]
    gw = GROUP_WIDTH

    lb_all = jnp.cumsum(jax.nn.softmax(hgrn_lb.astype(F32), axis=0), axis=0)
    lb_all = lb_all - lb_all[0]

    x = x.astype(F32)
    for l in range(depth):
        proj = norm_matmul(x.reshape(t_tok, d), norm_mix[l], w_in_prep(w_in, l), tm=ROW_TILE)
        proj3 = proj.reshape(b, s, N_PROJ)
        prm = jnp.zeros((8, SMALL_W), F32)
        prm = prm.at[0, LANE_FOX:LANE_FOX + HEADS].set(fox_fb[l].astype(F32))
        prm = prm.at[0, LANE_GDEC:LANE_GDEC + HEADS].set(gdn_dt_bias[l].astype(F32))
        prm = prm.at[1, LANE_GDEC:LANE_GDEC + HEADS].set(gdn_a_log[l].astype(F32))
        sm, c_exp = small_prep(proj3, prm)

        ya = hgrn2(proj3, lb_all[l])
        yb = fox_attention(proj3, c_exp)
        yc = rglru(proj3, lru_conv_w[l], lru_conv_b[l], _block_diag(lru_wa[l]).astype(BF16), lru_ba[l],
                   _block_diag(lru_wx[l]).astype(BF16), lru_bx[l], lru_lam[l])
        yd = gdn(proj3, sm, gdn_conv_w[l])

        kv = norm_matmul(mem.reshape(b * mlen, d), norm_memkv[l], w_mkv[l].astype(BF16), tm=256)
        w_router = jnp.concatenate([w_re[l], w_rg[l], jnp.zeros((d, SMALL_W - N_EXPERTS - N_EXPERT_GROUPS), F32)], axis=1)
        b_router = jnp.concatenate([b_re[l], b_rg[l], jnp.zeros((SMALL_W - N_EXPERTS - N_EXPERT_GROUPS,), F32)]).reshape(1, SMALL_W)
        x, h, route, route_t, counts = post_mix(x, ya, yb, yc, yd, mix_gain[l], w_out[l].astype(BF16), norm_mem[l],
                                                w_mq[l].astype(BF16), kv.reshape(b, mlen, 2 * gw), w_mo[l].astype(BF16),
                                                norm_ffn[l], w_router, b_router)

        x = moe_routed(x.reshape(t_tok, d), h, route, route_t, counts, w_e_gu, w_e_dn, l,
                       norm_final, final_norm=(l == depth - 1)).reshape(b, s, d)

    return x
```

```python
import functools
import math

import jax
import jax.numpy as jnp
from jax import lax
from jax.experimental import pallas as pl
from jax.experimental.pallas import tpu as pltpu
from jax.experimental.pallas import tpu_sc as plsc

F32 = jnp.float32
BF16 = jnp.bfloat16

HEAD_DIM = 64
GROUP_WIDTH = 256
HEADS = GROUP_WIDTH // HEAD_DIM
GDN_CHUNK = 64
GDN_TILE = 128
GDN_BATCH = 4
FOX_BATCH = 2
HGRN_BATCH = 4
HGRN_CHUNK = 16
CONV_W = 4
LRU_C = 8.0
EPS = 1e-6
N_EXPERT_GROUPS = 4
EXPERTS_PER_GROUP = 8
N_EXPERTS = N_EXPERT_GROUPS * EXPERTS_PER_GROUP
D_EXPERT = 256
SC_GATHER_CHUNK = 128
EXPERT_TILE = 256
SMALL_W = 128
SEQ_TILE = 256
ROW_TILE = 512
VMEM_LIMIT = 56 * 1024 * 1024

(COL_AQ, COL_AF, COL_AI, COL_AG, COL_BQ, COL_BK, COL_BV, COL_CX, COL_CG,
 COL_DQ, COL_DK, COL_DV, COL_DZ) = range(13)
N_PROJ = 13 * GROUP_WIDTH + 2 * SMALL_W
COL_SMALL = 13 * GROUP_WIDTH // SMALL_W
LANE_FOX = 0
LANE_BETA = 4
LANE_GDEC = 8


def _cparams(*sem):
    return pltpu.CompilerParams(dimension_semantics=sem, vmem_limit_bytes=VMEM_LIMIT)


def _dot(a, b):
    return jnp.dot(a, b, preferred_element_type=F32)


def _dot_nt(a, b):
    return lax.dot_general(a, b, (((1,), (1,)), ((), ())), preferred_element_type=F32)


def _dot_tn(a, b):
    return lax.dot_general(a, b, (((0,), (0,)), ((), ())), preferred_element_type=F32)


def _split3(x):
    h = x.astype(BF16)
    r = x - h.astype(F32)
    m = r.astype(BF16)
    l = (r - m.astype(F32)).astype(BF16)
    return h, m, l


def _dot_exact_rhs(x, w_bf16):
    h, m, l = _split3(x)
    return _dot(h, w_bf16) + _dot(m, w_bf16) + _dot(l, w_bf16)


def _dot_exact_lhs(w_bf16, x):
    h, m, l = _split3(x)
    return _dot(w_bf16, h) + _dot(w_bf16, m) + _dot(w_bf16, l)


def _iota(shape, dim):
    return lax.broadcasted_iota(jnp.int32, shape, dim)


def _head_ones(n=GROUP_WIDTH):
    r = _iota((n, n), 0) // HEAD_DIM
    c = _iota((n, n), 1) // HEAD_DIM
    return r == c


def _sigmoid(x):
    return 1.0 / (1.0 + jnp.exp(-x))


def _silu(x):
    return x * _sigmoid(x)


def _log_sigmoid(x):
    return jnp.minimum(x, 0.0) - jnp.log1p(jnp.exp(-jnp.abs(x)))


def _softplus(x):
    return jnp.maximum(x, 0.0) + jnp.log1p(jnp.exp(-jnp.abs(x)))


def _gelu_tanh(x):
    return 0.5 * x * (1.0 + jnp.tanh(math.sqrt(2.0 / math.pi) * (x + 0.044715 * (x * x * x))))


def _pack_bf16_pairs(x):
    n = x.shape[1] // 2
    u = lax.bitcast_convert_type(x.astype(BF16).astype(F32), jnp.uint32)
    return u[:, :n] | (u[:, n:] >> 16)


def _unpack_bf16_pairs(p):
    hi = lax.bitcast_convert_type(p & jnp.uint32(0xFFFF0000), F32)
    lo = lax.bitcast_convert_type(p << 16, F32)
    return jnp.concatenate([hi, lo], axis=1)


def _head_mean_sq(x, ones_bf16):
    return _dot_exact_rhs(x * x, ones_bf16) * (1.0 / HEAD_DIM)


def _head_rms(x, ones_bf16):
    return x * lax.rsqrt(_head_mean_sq(x, ones_bf16) + EPS)


def _stack_heads(x):
    lane_head = _iota(x.shape, 1) // HEAD_DIM
    parts = []
    for h in range(HEADS):
        parts.append(jnp.where(lane_head == h, x, 0.0))
    return jnp.concatenate(parts, axis=0)


def _causal_conv4(x, pad_ref, w):
    r = x.shape[0]
    pad_ref[8:8 + r, :] = x
    acc = x * w[CONV_W - 1:CONV_W, :]
    for k in range(1, CONV_W):
        acc = acc + pad_ref[8 - k:8 - k + r, :] * w[CONV_W - 1 - k:CONV_W - k, :]
    pad_ref[0:8, :] = x[r - 8:r, :]
    return acc


def _norm_matmul_kernel(x_ref, g_ref, w_ref, o_ref):
    x = x_ref[...]
    ms = jnp.mean(x * x, axis=-1, keepdims=True)
    h = (x * lax.rsqrt(ms + EPS)) * g_ref[...]
    o_ref[...] = _dot(h.astype(BF16), w_ref[...]).astype(o_ref.dtype)


def norm_matmul(x, g, w_bf16, tm, out_dtype=F32):
    m, d = x.shape
    n = w_bf16.shape[1]
    return pl.pallas_call(
        _norm_matmul_kernel,
        grid=(m // tm,),
        in_specs=[pl.BlockSpec((tm, d), lambda i: (i, 0)),
                  pl.BlockSpec((1, d), lambda i: (0, 0)),
                  pl.BlockSpec((d, n), lambda i: (0, 0))],
        out_specs=pl.BlockSpec((tm, n), lambda i: (i, 0)),
        out_shape=jax.ShapeDtypeStruct((m, n), out_dtype),
        compiler_params=_cparams("parallel"),
        name="norm_matmul",
    )(x, g.reshape(1, d), w_bf16)


def _small_prep_kernel(s_ref, p_ref, o_ref, x_ref):
    S = s_ref.shape[1]
    W = s_ref.shape[2]
    blk = SEQ_TILE
    lane = _iota((blk, W), 1)
    is_fox = (lane >= LANE_FOX) & (lane < LANE_FOX + HEADS)
    is_beta = (lane >= LANE_BETA) & (lane < LANE_BETA + HEADS)
    is_gdec = (lane >= LANE_GDEC) & (lane < LANE_GDEC + HEADS)
    r = _iota((blk, blk), 0)
    c = _iota((blk, blk), 1)
    tril_all = jnp.where(r >= c, 1.0, 0.0).astype(BF16)
    tril_chunk = jnp.where((r >= c) & (r // GDN_CHUNK == c // GDN_CHUNK), 1.0, 0.0).astype(BF16)
    neg_exp_a = -jnp.exp(p_ref[1:2, :])
    carry = jnp.zeros((1, W), F32)
    for i in range(S // blk):
        sl = pl.ds(i * blk, blk)
        z = s_ref[0, sl, :] + p_ref[0:1, :]
        fox = jnp.where(is_fox, _log_sigmoid(z), 0.0)
        beta = jnp.where(is_beta, _sigmoid(z), 0.0)
        gdec = jnp.where(is_gdec, neg_exp_a * _softplus(z), 0.0)
        cf = _dot_exact_lhs(tril_all, fox) + carry
        cg = _dot_exact_lhs(tril_chunk, gdec)
        carry = cf[blk - 1:blk, :]
        o_ref[0, sl, :] = cf + cg + beta
        both = cf + cg
        for j, src_lane in enumerate(range(LANE_FOX, LANE_FOX + HEADS)):
            x_ref[0, sl, j * W:(j + 1) * W] = jnp.broadcast_to(both[:, src_lane:src_lane + 1], (blk, W))


def small_prep(proj3, params):
    b, s, _ = proj3.shape
    w = SMALL_W
    return pl.pallas_call(
        _small_prep_kernel,
        grid=(b,),
        in_specs=[pl.BlockSpec((1, s, w), lambda i: (i, 0, COL_SMALL)),
                  pl.BlockSpec((8, w), lambda i: (0, 0))],
        out_specs=[pl.BlockSpec((1, s, w), lambda i: (i, 0, 0)),
                   pl.BlockSpec((1, s, HEADS * w), lambda i: (i, 0, 0))],
        out_shape=[jax.ShapeDtypeStruct((b, s, w), F32), jax.ShapeDtypeStruct((b, s, HEADS * w), F32)],
        compiler_params=_cparams("parallel"),
        name="small_prep",
    )(proj3, params)


def _hgrn2_kernel(q_ref, f_ref, i_ref, g_ref, lb_ref, o_ref, st_ref, qs, ks, vs, bs, os_):
    nb = q_ref.shape[0]
    T = q_ref.shape[1]
    C = HGRN_CHUNK
    seqs = range(nb)

    @pl.when(pl.program_id(1) == 0)
    def _():
        st_ref[...] = jnp.zeros_like(st_ref)

    same_head = _head_ones()
    ones_bf = jnp.where(same_head, 1.0, 0.0).astype(BF16)
    lb = lb_ref[...]
    log_lb = jnp.log(lb)
    r = _iota((T, T), 0)
    c = _iota((T, T), 1)
    tril_chunk = jnp.where((r >= c) & (r // C == c // C), 1.0, 0.0).astype(BF16)
    for bb in seqs:
        fl = f_ref[bb]
        c2 = jnp.log1p(-lb) + _log_sigmoid(fl)
        mx = jnp.maximum(log_lb, c2)
        log_f = mx + jnp.log1p(jnp.exp(-jnp.abs(log_lb - c2)))
        bs[bb] = _dot_exact_lhs(tril_chunk, log_f)
        qs[bb] = _silu(q_ref[bb])
        ks[bb] = (1.0 - lb) * _sigmoid(-fl)
        vs[bb] = i_ref[bb]

    H8 = C // 2
    trow8 = _iota((H8, GROUP_WIDTH), 0)

    def chunk(ci, carry):
        r0 = pl.multiple_of(ci * C, C)
        qc = [qs[bb, pl.ds(r0, C), :] for bb in seqs]
        kc = [ks[bb, pl.ds(r0, C), :] for bb in seqs]
        vc = [vs[bb, pl.ds(r0, C), :] for bb in seqs]
        bc = [bs[bb, pl.ds(r0, C), :] for bb in seqs]
        st = [st_ref[bb] for bb in seqs]
        a = []
        for bb in seqs:
            parts = []
            for half, s_range in ((0, range(H8)), (1, range(C))):
                b_half = bc[bb][half * H8:(half + 1) * H8, :]
                q_half = qc[bb][half * H8:(half + 1) * H8, :]
                for s in s_range:
                    blk = jnp.exp(b_half - bc[bb][s:s + 1, :]) * (q_half * kc[bb][s:s + 1, :])
                    if s >= half * H8:
                        blk = jnp.where(trow8 >= s - half * H8, blk, 0.0)
                    parts.append(blk)
            a.append(jnp.concatenate(parts, axis=0).astype(BF16))
        sc = [_dot(a[bb], ones_bf) for bb in seqs]
        o = [_dot_nt((qc[bb] * jnp.exp(bc[bb])).astype(BF16), st[bb].astype(BF16)) for bb in seqs]
        b_last = [bc[bb][C - 1:C, :] for bb in seqs]
        upd = [_dot_tn(vc[bb].astype(BF16), (kc[bb] * jnp.exp(b_last[bb] - bc[bb])).astype(BF16)) for bb in seqs]
        for bb in seqs:
            top = o[bb][0:H8, :]
            bot = o[bb][H8:C, :]
            for s in range(H8):
                top = top + sc[bb][s * H8:(s + 1) * H8, :] * vc[bb][s:s + 1, :]
            for s in range(C):
                bot = bot + sc[bb][(H8 + s) * H8:(H8 + s + 1) * H8, :] * vc[bb][s:s + 1, :]
            os_[bb, pl.ds(r0, C), :] = jnp.concatenate([top, bot], axis=0)
            st_ref[bb] = st[bb] * jnp.exp(b_last[bb]) + jnp.where(same_head, upd[bb], 0.0)
        return carry

    lax.fori_loop(0, T // C, chunk, 0)
    for bb in seqs:
        o_ref[bb] = _head_rms(os_[bb], ones_bf) * _silu(g_ref[bb])


def hgrn2(proj3, lb):
    b, s, _ = proj3.shape
    t = SEQ_TILE
    gw = GROUP_WIDTH
    nb = HGRN_BATCH
    col = lambda cidx: pl.BlockSpec((nb, t, gw), lambda i, j: (i, j, cidx))
    return pl.pallas_call(
        _hgrn2_kernel,
        grid=(b // nb, s // t),
        in_specs=[col(COL_AQ), col(COL_AF), col(COL_AI), col(COL_AG),
                  pl.BlockSpec((1, gw), lambda i, j: (0, 0))],
        out_specs=pl.BlockSpec((nb, t, gw), lambda i, j: (i, j, 0)),
        out_shape=jax.ShapeDtypeStruct((b, s, gw), F32),
        scratch_shapes=[pltpu.VMEM((nb, gw, gw), F32)] + [pltpu.VMEM((nb, t, gw), F32)] * 5,
        compiler_params=_cparams("parallel", "arbitrary"),
        name="hgrn2",
    )(proj3, proj3, proj3, proj3, lb.reshape(1, gw))


def _fox_kernel(q_ref, k_ref, v_ref, cx_ref, o_ref, kb_ref, vt_ref, acc_ref):
    nb = q_ref.shape[0]
    tq = q_ref.shape[1]
    tk = tq
    S = k_ref.shape[1]
    qi = pl.program_id(1)
    seqs = range(nb)

    @pl.when(qi == 0)
    def _():
        for bb in seqs:
            for i in range(S // tk):
                sl = pl.ds(i * tk, tk)
                kb_ref[bb, sl, :] = k_ref[bb, sl, :].astype(BF16)
                vt_ref[bb, :, sl] = v_ref[bb, sl, :].T.astype(BF16)

    lane_head = _iota((tq, GROUP_WIDTH), 1) // HEAD_DIM
    qh = []
    for bb in seqs:
        qsc = q_ref[bb] * (HEAD_DIM ** -0.5)
        qh.append([jnp.where(lane_head == h, qsc, 0.0).astype(BF16) for h in range(HEADS)])
    acc_ref[...] = jnp.zeros_like(acc_ref)
    reps = tq // SMALL_W
    causal = _iota((tk, tq), 0) <= _iota((tk, tq), 1)

    def step(kb, stats, masked):
        k0 = pl.multiple_of(kb * tk, tk)
        new_stats = [[None] * HEADS for _ in seqs]
        strips = [(h, bb) for h in range(HEADS) for bb in seqs]

        def scores(h, bb):
            return _dot_nt(kb_ref[bb, pl.ds(k0, tk), :], qh[bb][h])

        ahead = 3
        queue = [scores(*strips[j]) for j in range(ahead)]
        for i, (h, bb) in enumerate(strips):
            st = queue.pop(0)
            if i + ahead < len(strips):
                queue.append(scores(*strips[i + ahead]))
            rows = slice(h * HEAD_DIM, (h + 1) * HEAD_DIM)
            cx = cx_ref[bb, pl.ds(k0, tk), h * SMALL_W:(h + 1) * SMALL_W]
            st = st - jnp.concatenate([cx] * reps, axis=1)
            if masked:
                st = jnp.where(causal, st, -jnp.inf)
            m_old, l_old = stats[bb][h]
            m_new = jnp.maximum(m_old, jnp.max(st, axis=0, keepdims=True))
            alpha = jnp.exp(m_old - m_new)
            p = jnp.exp(st - m_new)
            l_new = alpha * l_old + jnp.sum(p, axis=0, keepdims=True)
            pv = _dot(vt_ref[bb, rows, pl.ds(k0, tk)], p.astype(BF16))
            acc_ref[bb, rows, :] = alpha * acc_ref[bb, rows, :] + pv
            new_stats[bb][h] = (m_new, l_new)
        return tuple(tuple(new_stats[bb]) for bb in seqs)

    init = tuple(tuple((jnp.full((1, tq), -jnp.inf, F32), jnp.zeros((1, tq), F32)) for _ in range(HEADS)) for _ in seqs)
    stats = lax.fori_loop(0, qi, lambda kb, c: step(kb, c, False), init)
    stats = step(qi, stats, True)

    ones_bf = jnp.where(_head_ones(), 1.0, 0.0).astype(BF16)
    for bb in seqs:
        inv_l = jnp.concatenate([jnp.broadcast_to(1.0 / stats[bb][h][1], (HEAD_DIM, tq)) for h in range(HEADS)], axis=0)
        o_ref[bb] = _head_rms((acc_ref[bb] * inv_l).T, ones_bf)


def fox_attention(proj3, c_exp):
    b, s, _ = proj3.shape
    tq = SEQ_TILE
    gw = GROUP_WIDTH
    nb = FOX_BATCH
    return pl.pallas_call(
        _fox_kernel,
        grid=(b // nb, s // tq),
        in_specs=[pl.BlockSpec((nb, tq, gw), lambda i, j: (i, j, COL_BQ)),
                  pl.BlockSpec((nb, s, gw), lambda i, j: (i, 0, COL_BK)),
                  pl.BlockSpec((nb, s, gw), lambda i, j: (i, 0, COL_BV)),
                  pl.BlockSpec((nb, s, HEADS * SMALL_W), lambda i, j: (i, 0, 0))],
        out_specs=pl.BlockSpec((nb, tq, gw), lambda i, j: (i, j, 0)),
        out_shape=jax.ShapeDtypeStruct((b, s, gw), F32),
        scratch_shapes=[pltpu.VMEM((nb, s, gw), BF16),
                        pltpu.VMEM((nb, gw, s), BF16),
                        pltpu.VMEM((nb, gw, tq), F32)],
        compiler_params=_cparams("parallel", "arbitrary"),
        name="fox_attention",
    )(proj3, proj3, proj3, c_exp)


def _rglru_kernel(x_ref, g_ref, cw_ref, cb_ref, wa_ref, ba_ref, wx_ref, bx_ref, lam_ref, o_ref, prev_ref, h_ref):
    @pl.when(pl.program_id(1) == 0)
    def _():
        prev_ref[0:8, :] = jnp.zeros((8, prev_ref.shape[1]), F32)
        h_ref[...] = jnp.zeros_like(h_ref)

    x = x_ref[0]
    t = x.shape[0]
    xc = _causal_conv4(x, prev_ref, cw_ref[...]) + cb_ref[...]
    xb = xc.astype(BF16)
    r = _sigmoid(_dot(xb, wa_ref[...]) + ba_ref[...])
    ig = _sigmoid(_dot(xb, wx_ref[...]) + bx_ref[...])
    log_a = (-LRU_C * r) * _softplus(-lam_ref[...])
    a = jnp.exp(log_a)
    u = jnp.sqrt(1.0 - jnp.exp(2.0 * log_a)) * (ig * xc)
    row = _iota(a.shape, 0)
    d = 1
    while d < t:
        valid = row >= d
        u = jnp.where(valid, a * pltpu.roll(u, d, 0) + u, u)
        a = jnp.where(valid, a * pltpu.roll(a, d, 0), a)
        d *= 2
    h = a * h_ref[...] + u
    h_ref[...] = h[t - 1:t, :]
    ones_bf = jnp.where(_head_ones(), 1.0, 0.0).astype(BF16)
    o_ref[0] = _head_rms(h * _gelu_tanh(g_ref[0]), ones_bf)


def rglru(proj3, conv_w, conv_b, wa_bd, ba, wx_bd, bx, lam):
    b, s, _ = proj3.shape
    t = ROW_TILE
    gw = GROUP_WIDTH
    row = lambda: pl.BlockSpec((1, gw), lambda i, j: (0, 0))
    return pl.pallas_call(
        _rglru_kernel,
        grid=(b, s // t),
        in_specs=[pl.BlockSpec((1, t, gw), lambda i, j: (i, j, COL_CX)),
                  pl.BlockSpec((1, t, gw), lambda i, j: (i, j, COL_CG)),
                  pl.BlockSpec((CONV_W, gw), lambda i, j: (0, 0)), row(),
                  pl.BlockSpec((gw, gw), lambda i, j: (0, 0)), row(),
                  pl.BlockSpec((gw, gw), lambda i, j: (0, 0)), row(), row()],
        out_specs=pl.BlockSpec((1, t, gw), lambda i, j: (i, j, 0)),
        out_shape=jax.ShapeDtypeStruct((b, s, gw), F32),
        scratch_shapes=[pltpu.VMEM((8 + t, gw), F32), pltpu.VMEM((1, gw), F32)],
        compiler_params=_cparams("parallel", "arbitrary"),
        name="rglru",
    )(proj3, proj3, conv_w, conv_b.reshape(1, gw), wa_bd, ba.reshape(1, gw), wx_bd, bx.reshape(1, gw), lam.reshape(1, gw))


def _gdn_kernel(q_ref, k_ref, v_ref, z_ref, sm_ref, cw_ref, o_ref, st_ref, pq_ref, pk_ref, pv_ref):
    nb = q_ref.shape[0]
    T = q_ref.shape[1]
    C = GDN_CHUNK

    @pl.when(pl.program_id(1) == 0)
    def _():
        st_ref[...] = jnp.zeros_like(st_ref)
        for pad_ref in (pq_ref, pk_ref, pv_ref):
            pad_ref[:, 0:8, :] = jnp.zeros((nb, 8, pad_ref.shape[2]), F32)

    same_head = _head_ones()
    ones_bf = jnp.where(same_head, 1.0, 0.0).astype(BF16)
    cw = cw_ref[...]
    gw = GROUP_WIDTH
    er = _iota((SMALL_W, gw), 0)
    ec = _iota((SMALL_W, gw), 1) // HEAD_DIM
    exp_beta = jnp.where(er == LANE_BETA + ec, 1.0, 0.0).astype(BF16)
    exp_g = jnp.where(er == LANE_GDEC + ec, 1.0, 0.0).astype(BF16)

    prep = []
    for bb in range(nb):
        xq, xk, xv = q_ref[bb], k_ref[bb], v_ref[bb]
        q = _silu(_causal_conv4(xq, pq_ref.at[bb], cw[:, 0:gw]))
        k = _silu(_causal_conv4(xk, pk_ref.at[bb], cw[:, gw:2 * gw]))
        v = _silu(_causal_conv4(xv, pv_ref.at[bb], cw[:, 2 * gw:3 * gw]))
        q = q * lax.rsqrt(_dot_exact_rhs(q * q, ones_bf) + EPS) * (HEAD_DIM ** -0.5)
        k = k * lax.rsqrt(_dot_exact_rhs(k * k, ones_bf) + EPS)
        sm = sm_ref[bb]
        betax = _dot_exact_rhs(sm, exp_beta)
        gx = _dot_exact_rhs(sm, exp_g)
        prep.append((q, k, v, betax, gx))

    n_chunks = T // C
    trow = _iota((C, gw), 0)
    scol = _iota((C, gw), 1) % HEAD_DIM
    incl = scol <= trow
    strict = scol < trow
    eye = jnp.where(scol == trow, 1.0, 0.0)

    def bdiag(x_cat):
        return _stack_heads(x_cat).astype(BF16)

    chains = []
    for ci in range(n_chunks):
        sl = slice(ci * C, (ci + 1) * C)
        for bb in range(nb):
            q, k, v, betax, gx = prep[bb]
            qc, kc, vc, bx, gc = q[sl], k[sl], v[sl], betax[sl], gx[sl]
            grow = jnp.sum(jnp.where(scol == trow, gc, 0.0), axis=0, keepdims=True)
            eg = jnp.exp(gc)
            kb = kc * bx
            g_last = gc[C - 1:C, :]
            gamma = jnp.where(incl, jnp.exp(jnp.where(incl, gc - grow, 0.0)), 0.0)
            ks_bf = bdiag(kc)
            m = jnp.where(strict, _dot_nt(kb.astype(BF16), ks_bf) * gamma, 0.0)
            chains.append(dict(
                x=eye - m, y=m,
                a_qk=(_dot_nt(qc.astype(BF16), ks_bf) * gamma).astype(BF16),
                rhs_u=bdiag(vc * bx),
                rhs_w=bdiag(kb * eg),
                q_dec=(qc * eg).astype(BF16),
                k_dec=(kc * jnp.exp(g_last - gc)).astype(BF16),
                decay=jnp.exp(g_last)))
    for ch in chains:
        ch["ybd"] = bdiag(ch["y"])
    for _ in range(5):
        for ch in chains:
            ch["y"] = _dot(ch["y"].astype(BF16), ch["ybd"])
            ch["ybd"] = bdiag(ch["y"])
        for ch in chains:
            ch["x"] = ch["x"] + _dot(ch["x"].astype(BF16), ch["ybd"])
    for ch in chains:
        t_inv = ch["x"].astype(BF16)
        ch["u"] = _dot(t_inv, ch["rhs_u"])
        ch["wk"] = _dot(t_inv, ch["rhs_w"]).astype(BF16)

    outs = [[] for _ in range(nb)]
    for ci in range(n_chunks):
        for bb in range(nb):
            ch = chains[ci * nb + bb]
            st = st_ref[bb]
            st_bf = st.astype(BF16)
            v_new = ch["u"] - _dot(ch["wk"], st_bf)
            outs[bb].append(_dot(ch["q_dec"], st_bf) + _dot(ch["a_qk"], bdiag(v_new)))
            upd = _dot_tn(ch["k_dec"], v_new.astype(BF16))
            st_ref[bb] = st * ch["decay"] + jnp.where(same_head, upd, 0.0)
    for bb in range(nb):
        o = jnp.concatenate(outs[bb], axis=0)
        o_ref[bb] = _head_rms(o, ones_bf) * _silu(z_ref[bb])


def gdn(proj3, sm, conv_w):
    b, s, _ = proj3.shape
    t = GDN_TILE
    gw = GROUP_WIDTH
    nb = GDN_BATCH
    col = lambda cidx: pl.BlockSpec((nb, t, gw), lambda i, j: (i, j, cidx))
    return pl.pallas_call(
        _gdn_kernel,
        grid=(b // nb, s // t),
        in_specs=[col(COL_DQ), col(COL_DK), col(COL_DV), col(COL_DZ),
                  pl.BlockSpec((nb, t, SMALL_W), lambda i, j: (i, j, 0)),
                  pl.BlockSpec((CONV_W, 3 * gw), lambda i, j: (0, 0))],
        out_specs=pl.BlockSpec((nb, t, gw), lambda i, j: (i, j, 0)),
        out_shape=jax.ShapeDtypeStruct((b, s, gw), F32),
        scratch_shapes=[pltpu.VMEM((nb, gw, gw), F32)] + [pltpu.VMEM((nb, 8 + t, gw), F32)] * 3,
        compiler_params=_cparams("parallel", "arbitrary"),
        name="gdn",
    )(proj3, proj3, proj3, proj3, sm, conv_w)


def _mix_out(x, ys, gain, w_ref):
    gw = GROUP_WIDTH
    acc = x
    for i, y in enumerate(ys):
        yg = (y * gain[:, i * gw:(i + 1) * gw]).astype(BF16)
        acc = acc + _dot(yg, w_ref[i * gw:(i + 1) * gw, :])
    return acc


def _mem_attention(x, g, wq_ref, kv, wo_ref):
    t = x.shape[0]
    gw = GROUP_WIDTH
    ms = jnp.mean(x * x, axis=-1, keepdims=True)
    h = ((x * lax.rsqrt(ms + EPS)) * g).astype(BF16)
    q = _dot(h, wq_ref[...])
    k = kv[:, 0:gw].astype(BF16)
    v = kv[:, gw:2 * gw].astype(BF16)
    s = _dot_nt(k, _stack_heads(q).astype(BF16)) * (HEAD_DIM ** -0.5)
    s = s - jnp.max(s, axis=0, keepdims=True)
    p = jnp.exp(s)
    p = p * (1.0 / jnp.sum(p, axis=0, keepdims=True))
    ot = _dot_tn(v, p.astype(BF16))
    oc = jnp.concatenate([ot[hh * HEAD_DIM:(hh + 1) * HEAD_DIM, hh * t:(hh + 1) * t] for hh in range(HEADS)], axis=0).T
    return x + _dot(oc.astype(BF16), wo_ref[...])


def _post_mix_kernel(x_ref, ya_ref, yb_ref, yc_ref, yd_ref, gain_ref, wout_ref, gm_ref, wq_ref, kv_ref, wo_ref,
                     g_ref, w_ref, b_ref, x2_ref, h_ref, r_ref, rt_ref, n_ref, cnt_ref):
    @pl.when((pl.program_id(0) == 0) & (pl.program_id(1) == 0))
    def _():
        cnt_ref[...] = jnp.zeros_like(cnt_ref)

    x1 = _mix_out(x_ref[0], (ya_ref[0], yb_ref[0], yc_ref[0], yd_ref[0]), gain_ref[...], wout_ref)
    x = _mem_attention(x1, gm_ref[...], wq_ref, kv_ref[0], wo_ref)
    x2_ref[0] = x
    tm = x.shape[0]
    ms = jnp.mean(x * x, axis=-1, keepdims=True)
    h = (x * lax.rsqrt(ms + EPS)) * g_ref[...]
    h_ref[...] = _pack_bf16_pairs(h)
    hh, hm, _ = _split3(h)
    w = w_ref[...]
    wh, wm, _ = _split3(w)
    logits = _dot(hh, wh) + _dot(hh, wm) + _dot(hm, wh) + b_ref[...]
    lane_i = _iota(logits.shape, 1)
    lane = lane_i.astype(F32)
    lane_grp = (lane_i // EXPERTS_PER_GROUP).astype(F32)
    big = jnp.float32(1e9)
    neg = -jnp.inf
    is_grp = (lane_i >= N_EXPERTS) & (lane_i < N_EXPERTS + N_EXPERT_GROUPS)
    gl = jnp.where(is_grp, logits, neg)
    gmax = jnp.max(gl, axis=-1, keepdims=True)
    p_grp = 1.0 / jnp.sum(jnp.exp(gl - gmax), axis=-1, keepdims=True)
    g_sel = jnp.min(jnp.where(gl == gmax, lane, big), axis=-1, keepdims=True) - N_EXPERTS
    in_grp = (lane_i < N_EXPERTS) & (lane_grp == g_sel)
    el = jnp.where(in_grp, logits, neg)
    m1 = jnp.max(el, axis=-1, keepdims=True)
    i1 = jnp.min(jnp.where(el == m1, lane, big), axis=-1, keepdims=True)
    el2 = jnp.where(lane == i1, neg, el)
    m2 = jnp.max(el2, axis=-1, keepdims=True)
    i2 = jnp.min(jnp.where(el2 == m2, lane, big), axis=-1, keepdims=True)
    e21 = jnp.exp(m2 - m1)
    w1 = 1.0 / (1.0 + e21)
    w2 = e21 / (1.0 + e21)
    oh1 = lane == i1
    oh2 = lane == i2
    onehot = jnp.where(oh1 | oh2, 1.0, 0.0)
    strict = jnp.where(_iota((tm, tm), 0) > _iota((tm, tm), 1), 1.0, 0.0).astype(BF16)
    before = _dot(strict, onehot.astype(BF16)) + cnt_ref[...]
    rank1 = jnp.sum(jnp.where(oh1, before, 0.0), axis=-1, keepdims=True)
    rank2 = jnp.sum(jnp.where(oh2, before, 0.0), axis=-1, keepdims=True)
    cnt = before[tm - 1:tm, :] + onehot[tm - 1:tm, :]
    cnt_ref[...] = cnt
    n_ref[...] = jnp.broadcast_to(cnt, n_ref.shape)
    cols = (i1, i2, p_grp * w1, p_grp * w2, rank1, rank2)
    route = jnp.zeros(logits.shape, F32)
    for j, col in enumerate(cols):
        route = jnp.where(lane_i == j, col, route)
    r_ref[...] = route
    rt_ref[...] = route.T[0:8, :]


def post_mix(x3, ya, yb, yc, yd, gain, w_out_bf16, g_mem, wq_bf16, kv3, wo_bf16, g_ffn, w_router, b_router):
    b, s, d = x3.shape
    t = ROW_TILE
    gw = GROUP_WIDTH
    nt = s // t
    mlen = kv3.shape[1]
    tile = lambda w: pl.BlockSpec((1, t, w), lambda i, j: (i, j, 0))
    const = lambda shape: pl.BlockSpec(shape, lambda i, j: (0,) * len(shape))
    tok = lambda w: pl.BlockSpec((t, w), lambda i, j: (i * nt + j, 0))
    return pl.pallas_call(
        _post_mix_kernel,
        grid=(b, nt),
        in_specs=[tile(d), tile(gw), tile(gw), tile(gw), tile(gw),
                  const((1, 4 * gw)), const((4 * gw, d)),
                  const((1, d)), const((d, gw)), pl.BlockSpec((1, mlen, 2 * gw), lambda i, j: (i, 0, 0)), const((gw, d)),
                  const((1, d)), const((d, SMALL_W)), const((1, SMALL_W))],
        out_specs=[tile(d), tok(d // 2), tok(SMALL_W), pl.BlockSpec((8, t), lambda i, j: (0, i * nt + j)),
                   const((8, SMALL_W))],
        out_shape=[jax.ShapeDtypeStruct((b, s, d), F32), jax.ShapeDtypeStruct((b * s, d // 2), jnp.uint32),
                   jax.ShapeDtypeStruct((b * s, SMALL_W), F32), jax.ShapeDtypeStruct((8, b * s), F32),
                   jax.ShapeDtypeStruct((8, SMALL_W), F32)],
        scratch_shapes=[pltpu.VMEM((1, SMALL_W), F32)],
        compiler_params=_cparams("arbitrary", "arbitrary"),
        name="post_mix",
    )(x3, ya, yb, yc, yd, gain.reshape(1, 4 * gw), w_out_bf16, g_mem.reshape(1, d), wq_bf16, kv3, wo_bf16,
      g_ffn.reshape(1, d), w_router, b_router)


def _experts_kernel(te_ref, nu_ref, run_ref, nxt_ref, xs_ref, wgu_hbm, wdn_hbm, y_ref,
                    wgu_buf, wdn_buf, wgu_bf, wdn_bf, sem, *, layer):
    i = pl.program_id(0)

    def weight_copies(expert, slot):
        return (pltpu.make_async_copy(wgu_hbm.at[layer, expert], wgu_buf.at[slot], sem.at[slot, 0]),
                pltpu.make_async_copy(wdn_hbm.at[layer, expert], wdn_buf.at[slot], sem.at[slot, 1]))

    @pl.when(i >= nu_ref[0])
    def _():
        y_ref[...] = jnp.zeros_like(y_ref)

    @pl.when(i < nu_ref[0])
    def _():
        expert = te_ref[i]
        slot = run_ref[i] % 2

        @pl.when(i == 0)
        def _():
            for c in weight_copies(expert, slot):
                c.start()

        @pl.when((i == 0) | (te_ref[jnp.maximum(i - 1, 0)] != expert))
        def _():
            for c in weight_copies(expert, slot):
                c.wait()
            nxt = nxt_ref[i]

            @pl.when(nxt >= 0)
            def _():
                for c in weight_copies(nxt, 1 - slot):
                    c.start()

            wgu_bf[...] = wgu_buf[slot].astype(BF16)
            wdn_bf[...] = wdn_buf[slot].astype(BF16)

        gu = _dot(_unpack_bf16_pairs(xs_ref[...]).astype(BF16), wgu_bf[...])
        act = _silu(gu[:, 0:D_EXPERT]) * gu[:, D_EXPERT:2 * D_EXPERT]
        y_ref[...] = _pack_bf16_pairs(_dot(act.astype(BF16), wdn_bf[...]))


def moe_experts(xs, tile_expert, n_used, run_index, next_expert, w_gu, w_dn, layer, tm):
    n_rows, dp = xs.shape
    d = 2 * dp
    return pl.pallas_call(
        functools.partial(_experts_kernel, layer=layer),
        grid_spec=pltpu.PrefetchScalarGridSpec(
            num_scalar_prefetch=4,
            grid=(n_rows // tm,),
            in_specs=[pl.BlockSpec((tm, dp), lambda i, te, nu, run, nxt: (jnp.minimum(i, nu[0] - 1), 0)),
                      pl.BlockSpec(memory_space=pl.ANY),
                      pl.BlockSpec(memory_space=pl.ANY)],
            out_specs=pl.BlockSpec((tm, dp), lambda i, te, nu, run, nxt: (i, 0)),
            scratch_shapes=[pltpu.VMEM((2, d, 2 * D_EXPERT), F32), pltpu.VMEM((2, D_EXPERT, d), F32),
                            pltpu.VMEM((d, 2 * D_EXPERT), BF16), pltpu.VMEM((D_EXPERT, d), BF16),
                            pltpu.SemaphoreType.DMA((2, 2))]),
        out_shape=jax.ShapeDtypeStruct((n_rows, dp), jnp.uint32),
        compiler_params=_cparams("arbitrary"),
        name="moe_experts",
    )(tile_expert, n_used, run_index, next_expert, xs, w_gu, w_dn)


def sc_gather_rows(table, idx):
    n_idx = idx.shape[0]
    width = table.shape[1]
    info = plsc.get_sparse_core_info()
    n_workers = info.num_cores * info.num_subcores
    per_worker = n_idx // n_workers
    chunk = SC_GATHER_CHUNK
    assert n_idx % (n_workers * chunk) == 0 and width % info.num_lanes == 0
    mesh = plsc.VectorSubcoreMesh(core_axis_name="c", subcore_axis_name="s")

    def body(table_hbm, idx_hbm, out_hbm, idx_v, rows_v, sem):
        wid = lax.axis_index("s") * info.num_cores + lax.axis_index("c")
        base = wid * per_worker

        @pl.loop(0, per_worker // chunk)
        def _(j):
            off = pl.multiple_of(base + j * chunk, chunk)
            pltpu.sync_copy(idx_hbm.at[pl.ds(off, chunk)], idx_v)
            pltpu.async_copy(table_hbm.at[idx_v], rows_v, sem).wait()
            pltpu.sync_copy(rows_v, out_hbm.at[pl.ds(off, chunk)])

    return pl.kernel(
        body,
        out_type=jax.ShapeDtypeStruct((n_idx, width), table.dtype),
        mesh=mesh,
        scratch_types=[pltpu.VMEM((chunk,), jnp.int32), pltpu.VMEM((chunk, width), table.dtype),
                       pltpu.SemaphoreType.DMA],
        name="sc_gather_rows",
    )(table, idx)


def sc_dispatch_rows(rows, dest, pad_idx, zero_rows):
    m, width = rows.shape
    n_pad = pad_idx.shape[0]
    info = plsc.get_sparse_core_info()
    n_workers = info.num_cores * info.num_subcores
    chunk = SC_GATHER_CHUNK
    assert m % (n_workers * chunk) == 0 and n_pad % (n_workers * chunk) == 0 and zero_rows.shape == (chunk, width)
    rows_per_worker = m // n_workers
    pad_per_worker = n_pad // n_workers
    mesh = plsc.VectorSubcoreMesh(core_axis_name="c", subcore_axis_name="s")

    def body(rows_hbm, dest_hbm, pad_hbm, zero_hbm, out_hbm, idx_v, rows_v, sem):
        wid = lax.axis_index("s") * info.num_cores + lax.axis_index("c")

        @pl.loop(0, rows_per_worker // chunk)
        def _(j):
            off = pl.multiple_of(wid * rows_per_worker + j * chunk, chunk)
            pltpu.sync_copy(rows_hbm.at[pl.ds(off, chunk)], rows_v)
            for slot in range(2):
                pltpu.sync_copy(dest_hbm.at[pl.ds(slot * m + off, chunk)], idx_v)
                pltpu.async_copy(rows_v, out_hbm.at[idx_v], sem).wait()

        pltpu.sync_copy(zero_hbm, rows_v)

        @pl.loop(0, pad_per_worker // chunk)
        def _(j):
            off = pl.multiple_of(wid * pad_per_worker + j * chunk, chunk)
            pltpu.sync_copy(pad_hbm.at[pl.ds(off, chunk)], idx_v)
            pltpu.async_copy(rows_v, out_hbm.at[idx_v], sem).wait()

    return pl.kernel(
        body,
        out_type=jax.ShapeDtypeStruct((2 * m + n_pad, width), rows.dtype),
        mesh=mesh,
        scratch_types=[pltpu.VMEM((chunk,), jnp.int32), pltpu.VMEM((chunk, width), rows.dtype),
                       pltpu.SemaphoreType.DMA],
        name="sc_dispatch_rows",
    )(rows, dest, pad_idx, zero_rows)


def _combine_rows_kernel(x_ref, r_ref, g_ref, y1_ref, y2_ref, o_ref, *, final_norm):
    r = r_ref[...]
    out = x_ref[...] + r[:, 2:3] * _unpack_bf16_pairs(y1_ref[...]) + r[:, 3:4] * _unpack_bf16_pairs(y2_ref[...])
    if final_norm:
        ms = jnp.mean(out * out, axis=-1, keepdims=True)
        out = (out * lax.rsqrt(ms + EPS)) * g_ref[...]
    o_ref[...] = out


def moe_combine_rows(x2, route, y_rows, g_final, tm, final_norm):
    m, d = x2.shape
    steps = m // tm
    return pl.pallas_call(
        functools.partial(_combine_rows_kernel, final_norm=final_norm),
        grid=(steps,),
        in_specs=[pl.BlockSpec((tm, d), lambda i: (i, 0)),
                  pl.BlockSpec((tm, SMALL_W), lambda i: (i, 0)),
                  pl.BlockSpec((1, d), lambda i: (0, 0)),
                  pl.BlockSpec((tm, d // 2), lambda i: (i, 0)),
                  pl.BlockSpec((tm, d // 2), lambda i: (i + steps, 0))],
        out_specs=pl.BlockSpec((tm, d), lambda i: (i, 0)),
        out_shape=jax.ShapeDtypeStruct((m, d), F32),
        compiler_params=_cparams("parallel"),
        name="moe_combine_rows",
    )(x2, route, g_final.reshape(1, d), y_rows, y_rows)


def moe_routed(x2, h, route, route_t, counts, w_gu, w_dn, layer, g_final, final_norm):
    m, d = x2.shape
    te_rows = EXPERT_TILE
    n_rows = 2 * m + N_EXPERTS * te_rows
    cnt = counts[0, 0:N_EXPERTS].astype(jnp.int32)
    padded = (cnt + te_rows - 1) // te_rows * te_rows
    seg_end = jnp.cumsum(padded)
    seg_start = seg_end - padded
    idx = route_t[0:2].astype(jnp.int32)
    onehot = idx[:, :, None] == jnp.arange(N_EXPERTS, dtype=jnp.int32)
    dest = (jnp.sum(jnp.where(onehot, seg_start, 0), axis=-1) + route_t[4:6].astype(jnp.int32)).reshape(2 * m)
    tile_start = jnp.arange(n_rows // te_rows, dtype=jnp.int32) * te_rows
    tile_expert = jnp.minimum(jnp.sum((tile_start[:, None] >= seg_end[None, :]).astype(jnp.int32), axis=1), N_EXPERTS - 1)
    n_used = (seg_end[N_EXPERTS - 1] // te_rows).reshape(1)
    n_pad = n_rows - 2 * m
    experts = jnp.arange(N_EXPERTS, dtype=jnp.int32)
    pad_end = jnp.cumsum(padded - cnt)
    k = jnp.arange(n_pad, dtype=jnp.int32)
    k_expert = jnp.sum((k[:, None] >= pad_end[None, :]).astype(jnp.int32), axis=1)
    first_pad = seg_start + cnt - (pad_end - (padded - cnt))
    in_segment = jnp.sum(jnp.where(k_expert[:, None] == experts[None, :], first_pad[None, :], 0), axis=1) + k
    pad_idx = jnp.where(k_expert < N_EXPERTS, in_segment, seg_end[N_EXPERTS - 1] + k - pad_end[N_EXPERTS - 1])
    xs = sc_dispatch_rows(h, dest, pad_idx, jnp.zeros((SC_GATHER_CHUNK, d // 2), h.dtype))
    n_tiles = n_rows // te_rows
    tile_ids = jnp.arange(n_tiles, dtype=jnp.int32)
    prev_expert = jnp.concatenate([jnp.full((1,), -1, jnp.int32), tile_expert[:-1]])
    run_index = jnp.maximum(jnp.cumsum(((tile_expert != prev_expert) & (tile_ids < n_used[0])).astype(jnp.int32)) - 1, 0)
    next_tile = jnp.take(seg_end, tile_expert) // te_rows
    next_expert = jnp.where(next_tile < n_used[0], jnp.take(tile_expert, jnp.minimum(next_tile, n_tiles - 1)), -1)
    ys = moe_experts(xs, tile_expert, n_used, run_index, next_expert, w_gu, w_dn, layer, tm=te_rows)
    y_rows = sc_gather_rows(ys, dest)
    return moe_combine_rows(x2, route, y_rows, g_final, tm=ROW_TILE, final_norm=final_norm)


def _w_in_prep_kernel(w_ref, o_ref):
    gw = GROUP_WIDTH
    a_end = 7 * gw
    c_start = a_end + HEADS
    c_end = c_start + 5 * gw
    z_start = c_end + 2 * HEADS
    assert a_end % SMALL_W == LANE_FOX and c_end % SMALL_W == LANE_BETA and LANE_GDEC == LANE_BETA + HEADS
    w = w_ref[...]
    o_ref[:, 0:a_end] = w[:, 0:a_end].astype(BF16)
    o_ref[:, a_end:a_end + 5 * gw] = w[:, c_start:c_end].astype(BF16)
    o_ref[:, 12 * gw:13 * gw] = w[:, z_start:z_start + gw].astype(BF16)
    blk_fox = w[:, a_end:a_end + SMALL_W]
    blk_gdn = w[:, c_end - LANE_BETA:c_end - LANE_BETA + SMALL_W]
    lane = _iota(blk_fox.shape, 1)
    small = jnp.where(lane < LANE_BETA, blk_fox, jnp.where(lane < LANE_GDEC + HEADS, blk_gdn, 0.0))
    o_ref[:, 13 * gw:13 * gw + SMALL_W] = small.astype(BF16)
    o_ref[:, 13 * gw + SMALL_W:] = jnp.zeros((w.shape[0], SMALL_W), BF16)


def w_in_prep(w_all, layer):
    _, d, n = w_all.shape
    tr = 128
    return pl.pallas_call(
        _w_in_prep_kernel,
        grid=(d // tr,),
        in_specs=[pl.BlockSpec((None, tr, n), lambda i: (layer, i, 0))],
        out_specs=pl.BlockSpec((tr, N_PROJ), lambda i: (i, 0)),
        out_shape=jax.ShapeDtypeStruct((d, N_PROJ), BF16),
        compiler_params=_cparams("parallel"),
        name="w_in_prep",
    )(w_all)


def _block_diag(w):
    h = w.shape[0]
    eye = jnp.eye(h, dtype=w.dtype)
    return (eye[:, None, :, None] * w[:, :, None, :]).reshape(h * HEAD_DIM, h * HEAD_DIM)


def kernel(x, mem, norm_mix, w_in, hgrn_lb, fox_fb, lru_conv_w, lru_conv_b, lru_wa, lru_ba, lru_wx, lru_bx, lru_lam, gdn_conv_w, gdn_a_log, gdn_dt_bias, mix_gain, w_out, norm_mem, norm_memkv, w_mq, w_mkv, w_mo, norm_ffn, w_rg, b_rg, w_re, b_re, w_e_gu, w_e_dn, norm_final):
    b, s, d = x.shape
    depth = w_in.shape[0]
    t_tok = b * s
    mlen = mem.shape[1]
    gw = GROUP_WIDTH

    lb_all = jnp.cumsum(jax.nn.softmax(hgrn_lb.astype(F32), axis=0), axis=0)
    lb_all = lb_all - lb_all[0]

    x = x.astype(F32)
    for l in range(depth):
        proj = norm_matmul(x.reshape(t_tok, d), norm_mix[l], w_in_prep(w_in, l), tm=ROW_TILE)
        proj3 = proj.reshape(b, s, N_PROJ)
        prm = jnp.zeros((8, SMALL_W), F32)
        prm = prm.at[0, LANE_FOX:LANE_FOX + HEADS].set(fox_fb[l].astype(F32))
        prm = prm.at[0, LANE_GDEC:LANE_GDEC + HEADS].set(gdn_dt_bias[l].astype(F32))
        prm = prm.at[1, LANE_GDEC:LANE_GDEC + HEADS].set(gdn_a_log[l].astype(F32))
        sm, c_exp = small_prep(proj3, prm)

        ya = hgrn2(proj3, lb_all[l])
        yb = fox_attention(proj3, c_exp)
        yc = rglru(proj3, lru_conv_w[l], lru_conv_b[l], _block_diag(lru_wa[l]).astype(BF16), lru_ba[l],
                   _block_diag(lru_wx[l]).astype(BF16), lru_bx[l], lru_lam[l])
        yd = gdn(proj3, sm, gdn_conv_w[l])

        kv = norm_matmul(mem.reshape(b * mlen, d), norm_memkv[l], w_mkv[l].astype(BF16), tm=256)
        w_router = jnp.concatenate([w_re[l], w_rg[l], jnp.zeros((d, SMALL_W - N_EXPERTS - N_EXPERT_GROUPS), F32)], axis=1)
        b_router = jnp.concatenate([b_re[l], b_rg[l], jnp.zeros((SMALL_W - N_EXPERTS - N_EXPERT_GROUPS,), F32)]).reshape(1, SMALL_W)
        x, h, route, route_t, counts = post_mix(x, ya, yb, yc, yd, mix_gain[l], w_out[l].astype(BF16), norm_mem[l],
                                                w_mq[l].astype(BF16), kv.reshape(b, mlen, 2 * gw), w_mo[l].astype(BF16),
                                                norm_ffn[l], w_router, b_router)

        x = moe_routed(x.reshape(t_tok, d), h, route, route_t, counts, w_e_gu, w_e_dn, l,
                       norm_final, final_norm=(l == depth - 1)).reshape(b, s, d)

    return x
```

```python
import functools
import math

import jax
import jax.numpy as jnp
from jax import lax
from jax.experimental import pallas as pl
from jax.experimental.pallas import tpu as pltpu
from jax.experimental.pallas import tpu_sc as plsc

F32 = jnp.float32
BF16 = jnp.bfloat16

HEAD_DIM = 64
GROUP_WIDTH = 256
HEADS = GROUP_WIDTH // HEAD_DIM
GDN_CHUNK = 64
GDN_TILE = 128
GDN_BATCH = 4
FOX_BATCH = 2
HGRN_BATCH = 4
HGRN_CHUNK = 16
CONV_W = 4
LRU_C = 8.0
EPS = 1e-6
N_EXPERT_GROUPS = 4
EXPERTS_PER_GROUP = 8
N_EXPERTS = N_EXPERT_GROUPS * EXPERTS_PER_GROUP
D_EXPERT = 256
SC_GATHER_CHUNK = 128
EXPERT_TILE = 512
SMALL_W = 128
SEQ_TILE = 256
ROW_TILE = 512
VMEM_LIMIT = 56 * 1024 * 1024

(COL_AQ, COL_AF, COL_AI, COL_AG, COL_BQ, COL_BK, COL_BV, COL_CX, COL_CG,
 COL_DQ, COL_DK, COL_DV, COL_DZ) = range(13)
N_PROJ = 13 * GROUP_WIDTH + 2 * SMALL_W
COL_SMALL = 13 * GROUP_WIDTH // SMALL_W
LANE_FOX = 0
LANE_BETA = 4
LANE_GDEC = 8


def _cparams(*sem):
    return pltpu.CompilerParams(dimension_semantics=sem, vmem_limit_bytes=VMEM_LIMIT)


def _dot(a, b):
    return jnp.dot(a, b, preferred_element_type=F32)


def _dot_nt(a, b):
    return lax.dot_general(a, b, (((1,), (1,)), ((), ())), preferred_element_type=F32)


def _dot_tn(a, b):
    return lax.dot_general(a, b, (((0,), (0,)), ((), ())), preferred_element_type=F32)


def _split3(x):
    h = x.astype(BF16)
    r = x - h.astype(F32)
    m = r.astype(BF16)
    l = (r - m.astype(F32)).astype(BF16)
    return h, m, l


def _dot_exact_rhs(x, w_bf16):
    h, m, l = _split3(x)
    return _dot(h, w_bf16) + _dot(m, w_bf16) + _dot(l, w_bf16)


def _dot_exact_lhs(w_bf16, x):
    h, m, l = _split3(x)
    return _dot(w_bf16, h) + _dot(w_bf16, m) + _dot(w_bf16, l)


def _iota(shape, dim):
    return lax.broadcasted_iota(jnp.int32, shape, dim)


def _head_ones(n=GROUP_WIDTH):
    r = _iota((n, n), 0) // HEAD_DIM
    c = _iota((n, n), 1) // HEAD_DIM
    return r == c


def _sigmoid(x):
    return 1.0 / (1.0 + jnp.exp(-x))


def _silu(x):
    return x * _sigmoid(x)


def _log_sigmoid(x):
    return jnp.minimum(x, 0.0) - jnp.log1p(jnp.exp(-jnp.abs(x)))


def _softplus(x):
    return jnp.maximum(x, 0.0) + jnp.log1p(jnp.exp(-jnp.abs(x)))


def _gelu_tanh(x):
    return 0.5 * x * (1.0 + jnp.tanh(math.sqrt(2.0 / math.pi) * (x + 0.044715 * (x * x * x))))


def _pack_bf16_pairs(x):
    n = x.shape[1] // 2
    u = lax.bitcast_convert_type(x.astype(BF16).astype(F32), jnp.uint32)
    return u[:, :n] | (u[:, n:] >> 16)


def _unpack_bf16_pairs(p):
    hi = lax.bitcast_convert_type(p & jnp.uint32(0xFFFF0000), F32)
    lo = lax.bitcast_convert_type(p << 16, F32)
    return jnp.concatenate([hi, lo], axis=1)


def _head_mean_sq(x, ones_bf16):
    return _dot_exact_rhs(x * x, ones_bf16) * (1.0 / HEAD_DIM)


def _head_rms(x, ones_bf16):
    return x * lax.rsqrt(_head_mean_sq(x, ones_bf16) + EPS)


def _stack_heads(x):
    lane_head = _iota(x.shape, 1) // HEAD_DIM
    parts = []
    for h in range(HEADS):
        parts.append(jnp.where(lane_head == h, x, 0.0))
    return jnp.concatenate(parts, axis=0)


def _causal_conv4(x, pad_ref, w):
    r = x.shape[0]
    pad_ref[8:8 + r, :] = x
    acc = x * w[CONV_W - 1:CONV_W, :]
    for k in range(1, CONV_W):
        acc = acc + pad_ref[8 - k:8 - k + r, :] * w[CONV_W - 1 - k:CONV_W - k, :]
    pad_ref[0:8, :] = x[r - 8:r, :]
    return acc


def _norm_matmul_kernel(x_ref, g_ref, w_ref, o_ref):
    x = x_ref[...]
    ms = jnp.mean(x * x, axis=-1, keepdims=True)
    h = (x * lax.rsqrt(ms + EPS)) * g_ref[...]
    o_ref[...] = _dot(h.astype(BF16), w_ref[...]).astype(o_ref.dtype)


def norm_matmul(x, g, w_bf16, tm, out_dtype=F32):
    m, d = x.shape
    n = w_bf16.shape[1]
    return pl.pallas_call(
        _norm_matmul_kernel,
        grid=(m // tm,),
        in_specs=[pl.BlockSpec((tm, d), lambda i: (i, 0)),
                  pl.BlockSpec((1, d), lambda i: (0, 0)),
                  pl.BlockSpec((d, n), lambda i: (0, 0))],
        out_specs=pl.BlockSpec((tm, n), lambda i: (i, 0)),
        out_shape=jax.ShapeDtypeStruct((m, n), out_dtype),
        compiler_params=_cparams("parallel"),
        name="norm_matmul",
    )(x, g.reshape(1, d), w_bf16)


def _small_prep_kernel(s_ref, p_ref, o_ref, x_ref):
    S = s_ref.shape[1]
    W = s_ref.shape[2]
    blk = SEQ_TILE
    lane = _iota((blk, W), 1)
    is_fox = (lane >= LANE_FOX) & (lane < LANE_FOX + HEADS)
    is_beta = (lane >= LANE_BETA) & (lane < LANE_BETA + HEADS)
    is_gdec = (lane >= LANE_GDEC) & (lane < LANE_GDEC + HEADS)
    r = _iota((blk, blk), 0)
    c = _iota((blk, blk), 1)
    tril_all = jnp.where(r >= c, 1.0, 0.0).astype(BF16)
    tril_chunk = jnp.where((r >= c) & (r // GDN_CHUNK == c // GDN_CHUNK), 1.0, 0.0).astype(BF16)
    neg_exp_a = -jnp.exp(p_ref[1:2, :])
    carry = jnp.zeros((1, W), F32)
    for i in range(S // blk):
        sl = pl.ds(i * blk, blk)
        z = s_ref[0, sl, :] + p_ref[0:1, :]
        fox = jnp.where(is_fox, _log_sigmoid(z), 0.0)
        beta = jnp.where(is_beta, _sigmoid(z), 0.0)
        gdec = jnp.where(is_gdec, neg_exp_a * _softplus(z), 0.0)
        cf = _dot_exact_lhs(tril_all, fox) + carry
        cg = _dot_exact_lhs(tril_chunk, gdec)
        carry = cf[blk - 1:blk, :]
        o_ref[0, sl, :] = cf + cg + beta
        both = cf + cg
        for j, src_lane in enumerate(range(LANE_FOX, LANE_FOX + HEADS)):
            x_ref[0, sl, j * W:(j + 1) * W] = jnp.broadcast_to(both[:, src_lane:src_lane + 1], (blk, W))


def small_prep(proj3, params):
    b, s, _ = proj3.shape
    w = SMALL_W
    return pl.pallas_call(
        _small_prep_kernel,
        grid=(b,),
        in_specs=[pl.BlockSpec((1, s, w), lambda i: (i, 0, COL_SMALL)),
                  pl.BlockSpec((8, w), lambda i: (0, 0))],
        out_specs=[pl.BlockSpec((1, s, w), lambda i: (i, 0, 0)),
                   pl.BlockSpec((1, s, HEADS * w), lambda i: (i, 0, 0))],
        out_shape=[jax.ShapeDtypeStruct((b, s, w), F32), jax.ShapeDtypeStruct((b, s, HEADS * w), F32)],
        compiler_params=_cparams("parallel"),
        name="small_prep",
    )(proj3, params)


def _hgrn2_kernel(q_ref, f_ref, i_ref, g_ref, lb_ref, o_ref, st_ref, qs, ks, vs, bs, os_):
    nb = q_ref.shape[0]
    T = q_ref.shape[1]
    C = HGRN_CHUNK
    seqs = range(nb)

    @pl.when(pl.program_id(1) == 0)
    def _():
        st_ref[...] = jnp.zeros_like(st_ref)

    same_head = _head_ones()
    ones_bf = jnp.where(same_head, 1.0, 0.0).astype(BF16)
    lb = lb_ref[...]
    log_lb = jnp.log(lb)
    r = _iota((T, T), 0)
    c = _iota((T, T), 1)
    tril_chunk = jnp.where((r >= c) & (r // C == c // C), 1.0, 0.0).astype(BF16)
    for bb in seqs:
        fl = f_ref[bb]
        c2 = jnp.log1p(-lb) + _log_sigmoid(fl)
        mx = jnp.maximum(log_lb, c2)
        log_f = mx + jnp.log1p(jnp.exp(-jnp.abs(log_lb - c2)))
        bs[bb] = _dot_exact_lhs(tril_chunk, log_f)
        qs[bb] = _silu(q_ref[bb])
        ks[bb] = (1.0 - lb) * _sigmoid(-fl)
        vs[bb] = i_ref[bb]

    H8 = C // 2
    trow8 = _iota((H8, GROUP_WIDTH), 0)

    def chunk(ci, carry):
        r0 = pl.multiple_of(ci * C, C)
        qc = [qs[bb, pl.ds(r0, C), :] for bb in seqs]
        kc = [ks[bb, pl.ds(r0, C), :] for bb in seqs]
        vc = [vs[bb, pl.ds(r0, C), :] for bb in seqs]
        bc = [bs[bb, pl.ds(r0, C), :] for bb in seqs]
        st = [st_ref[bb] for bb in seqs]
        a = []
        for bb in seqs:
            parts = []
            for half, s_range in ((0, range(H8)), (1, range(C))):
                b_half = bc[bb][half * H8:(half + 1) * H8, :]
                q_half = qc[bb][half * H8:(half + 1) * H8, :]
                for s in s_range:
                    blk = jnp.exp(b_half - bc[bb][s:s + 1, :]) * (q_half * kc[bb][s:s + 1, :])
                    if s >= half * H8:
                        blk = jnp.where(trow8 >= s - half * H8, blk, 0.0)
                    parts.append(blk)
            a.append(jnp.concatenate(parts, axis=0).astype(BF16))
        sc = [_dot(a[bb], ones_bf) for bb in seqs]
        o = [_dot_nt((qc[bb] * jnp.exp(bc[bb])).astype(BF16), st[bb].astype(BF16)) for bb in seqs]
        b_last = [bc[bb][C - 1:C, :] for bb in seqs]
        upd = [_dot_tn(vc[bb].astype(BF16), (kc[bb] * jnp.exp(b_last[bb] - bc[bb])).astype(BF16)) for bb in seqs]
        for bb in seqs:
            top = o[bb][0:H8, :]
            bot = o[bb][H8:C, :]
            for s in range(H8):
                top = top + sc[bb][s * H8:(s + 1) * H8, :] * vc[bb][s:s + 1, :]
            for s in range(C):
                bot = bot + sc[bb][(H8 + s) * H8:(H8 + s + 1) * H8, :] * vc[bb][s:s + 1, :]
            os_[bb, pl.ds(r0, C), :] = jnp.concatenate([top, bot], axis=0)
            st_ref[bb] = st[bb] * jnp.exp(b_last[bb]) + jnp.where(same_head, upd[bb], 0.0)
        return carry

    lax.fori_loop(0, T // C, chunk, 0)
    for bb in seqs:
        o_ref[bb] = _head_rms(os_[bb], ones_bf) * _silu(g_ref[bb])


def hgrn2(proj3, lb):
    b, s, _ = proj3.shape
    t = SEQ_TILE
    gw = GROUP_WIDTH
    nb = HGRN_BATCH
    col = lambda cidx: pl.BlockSpec((nb, t, gw), lambda i, j: (i, j, cidx))
    return pl.pallas_call(
        _hgrn2_kernel,
        grid=(b // nb, s // t),
        in_specs=[col(COL_AQ), col(COL_AF), col(COL_AI), col(COL_AG),
                  pl.BlockSpec((1, gw), lambda i, j: (0, 0))],
        out_specs=pl.BlockSpec((nb, t, gw), lambda i, j: (i, j, 0)),
        out_shape=jax.ShapeDtypeStruct((b, s, gw), F32),
        scratch_shapes=[pltpu.VMEM((nb, gw, gw), F32)] + [pltpu.VMEM((nb, t, gw), F32)] * 5,
        compiler_params=_cparams("parallel", "arbitrary"),
        name="hgrn2",
    )(proj3, proj3, proj3, proj3, lb.reshape(1, gw))


def _fox_kernel(q_ref, k_ref, v_ref, cx_ref, o_ref, kb_ref, vt_ref, acc_ref):
    nb = q_ref.shape[0]
    tq = q_ref.shape[1]
    tk = tq
    S = k_ref.shape[1]
    qi = pl.program_id(1)
    seqs = range(nb)

    @pl.when(qi == 0)
    def _():
        for bb in seqs:
            for i in range(S // tk):
                sl = pl.ds(i * tk, tk)
                kb_ref[bb, sl, :] = k_ref[bb, sl, :].astype(BF16)
                vt_ref[bb, :, sl] = v_ref[bb, sl, :].T.astype(BF16)

    lane_head = _iota((tq, GROUP_WIDTH), 1) // HEAD_DIM
    qh = []
    for bb in seqs:
        qsc = q_ref[bb] * (HEAD_DIM ** -0.5)
        qh.append([jnp.where(lane_head == h, qsc, 0.0).astype(BF16) for h in range(HEADS)])
    acc_ref[...] = jnp.zeros_like(acc_ref)
    reps = tq // SMALL_W
    causal = _iota((tk, tq), 0) <= _iota((tk, tq), 1)

    def step(kb, stats, masked):
        k0 = pl.multiple_of(kb * tk, tk)
        new_stats = [[None] * HEADS for _ in seqs]
        strips = [(h, bb) for h in range(HEADS) for bb in seqs]

        def scores(h, bb):
            return _dot_nt(kb_ref[bb, pl.ds(k0, tk), :], qh[bb][h])

        ahead = 4
        queue = [scores(*strips[j]) for j in range(ahead)]
        for i, (h, bb) in enumerate(strips):
            st = queue.pop(0)
            if i + ahead < len(strips):
                queue.append(scores(*strips[i + ahead]))
            rows = slice(h * HEAD_DIM, (h + 1) * HEAD_DIM)
            cx = cx_ref[bb, pl.ds(k0, tk), h * SMALL_W:(h + 1) * SMALL_W]
            st = st - jnp.concatenate([cx] * reps, axis=1)
            if masked:
                st = jnp.where(causal, st, -jnp.inf)
            m_old, l_old = stats[bb][h]
            m_new = jnp.maximum(m_old, jnp.max(st, axis=0, keepdims=True))
            alpha = jnp.exp(m_old - m_new)
            p = jnp.exp(st - m_new)
            l_new = alpha * l_old + jnp.sum(p, axis=0, keepdims=True)
            pv = _dot(vt_ref[bb, rows, pl.ds(k0, tk)], p.astype(BF16))
            acc_ref[bb, rows, :] = alpha * acc_ref[bb, rows, :] + pv
            new_stats[bb][h] = (m_new, l_new)
        return tuple(tuple(new_stats[bb]) for bb in seqs)

    init = tuple(tuple((jnp.full((1, tq), -jnp.inf, F32), jnp.zeros((1, tq), F32)) for _ in range(HEADS)) for _ in seqs)
    stats = lax.fori_loop(0, qi, lambda kb, c: step(kb, c, False), init)
    stats = step(qi, stats, True)

    ones_bf = jnp.where(_head_ones(), 1.0, 0.0).astype(BF16)
    for bb in seqs:
        inv_l = jnp.concatenate([jnp.broadcast_to(1.0 / stats[bb][h][1], (HEAD_DIM, tq)) for h in range(HEADS)], axis=0)
        o_ref[bb] = _head_rms((acc_ref[bb] * inv_l).T, ones_bf)


def fox_attention(proj3, c_exp):
    b, s, _ = proj3.shape
    tq = SEQ_TILE
    gw = GROUP_WIDTH
    nb = FOX_BATCH
    return pl.pallas_call(
        _fox_kernel,
        grid=(b // nb, s // tq),
        in_specs=[pl.BlockSpec((nb, tq, gw), lambda i, j: (i, j, COL_BQ)),
                  pl.BlockSpec((nb, s, gw), lambda i, j: (i, 0, COL_BK)),
                  pl.BlockSpec((nb, s, gw), lambda i, j: (i, 0, COL_BV)),
                  pl.BlockSpec((nb, s, HEADS * SMALL_W), lambda i, j: (i, 0, 0))],
        out_specs=pl.BlockSpec((nb, tq, gw), lambda i, j: (i, j, 0)),
        out_shape=jax.ShapeDtypeStruct((b, s, gw), F32),
        scratch_shapes=[pltpu.VMEM((nb, s, gw), BF16),
                        pltpu.VMEM((nb, gw, s), BF16),
                        pltpu.VMEM((nb, gw, tq), F32)],
        compiler_params=_cparams("parallel", "arbitrary"),
        name="fox_attention",
    )(proj3, proj3, proj3, c_exp)


def _rglru_kernel(x_ref, g_ref, cw_ref, cb_ref, wa_ref, ba_ref, wx_ref, bx_ref, lam_ref, o_ref, prev_ref, h_ref):
    @pl.when(pl.program_id(1) == 0)
    def _():
        prev_ref[0:8, :] = jnp.zeros((8, prev_ref.shape[1]), F32)
        h_ref[...] = jnp.zeros_like(h_ref)

    x = x_ref[0]
    t = x.shape[0]
    xc = _causal_conv4(x, prev_ref, cw_ref[...]) + cb_ref[...]
    xb = xc.astype(BF16)
    r = _sigmoid(_dot(xb, wa_ref[...]) + ba_ref[...])
    ig = _sigmoid(_dot(xb, wx_ref[...]) + bx_ref[...])
    log_a = (-LRU_C * r) * _softplus(-lam_ref[...])
    a = jnp.exp(log_a)
    u = jnp.sqrt(1.0 - jnp.exp(2.0 * log_a)) * (ig * xc)
    row = _iota(a.shape, 0)
    d = 1
    while d < t:
        valid = row >= d
        u = jnp.where(valid, a * pltpu.roll(u, d, 0) + u, u)
        a = jnp.where(valid, a * pltpu.roll(a, d, 0), a)
        d *= 2
    h = a * h_ref[...] + u
    h_ref[...] = h[t - 1:t, :]
    ones_bf = jnp.where(_head_ones(), 1.0, 0.0).astype(BF16)
    o_ref[0] = _head_rms(h * _gelu_tanh(g_ref[0]), ones_bf)


def rglru(proj3, conv_w, conv_b, wa_bd, ba, wx_bd, bx, lam):
    b, s, _ = proj3.shape
    t = ROW_TILE
    gw = GROUP_WIDTH
    row = lambda: pl.BlockSpec((1, gw), lambda i, j: (0, 0))
    return pl.pallas_call(
        _rglru_kernel,
        grid=(b, s // t),
        in_specs=[pl.BlockSpec((1, t, gw), lambda i, j: (i, j, COL_CX)),
                  pl.BlockSpec((1, t, gw), lambda i, j: (i, j, COL_CG)),
                  pl.BlockSpec((CONV_W, gw), lambda i, j: (0, 0)), row(),
                  pl.BlockSpec((gw, gw), lambda i, j: (0, 0)), row(),
                  pl.BlockSpec((gw, gw), lambda i, j: (0, 0)), row(), row()],
        out_specs=pl.BlockSpec((1, t, gw), lambda i, j: (i, j, 0)),
        out_shape=jax.ShapeDtypeStruct((b, s, gw), F32),
        scratch_shapes=[pltpu.VMEM((8 + t, gw), F32), pltpu.VMEM((1, gw), F32)],
        compiler_params=_cparams("parallel", "arbitrary"),
        name="rglru",
    )(proj3, proj3, conv_w, conv_b.reshape(1, gw), wa_bd, ba.reshape(1, gw), wx_bd, bx.reshape(1, gw), lam.reshape(1, gw))


def _gdn_kernel(q_ref, k_ref, v_ref, z_ref, sm_ref, cw_ref, o_ref, st_ref, pq_ref, pk_ref, pv_ref):
    nb = q_ref.shape[0]
    T = q_ref.shape[1]
    C = GDN_CHUNK

    @pl.when(pl.program_id(1) == 0)
    def _():
        st_ref[...] = jnp.zeros_like(st_ref)
        for pad_ref in (pq_ref, pk_ref, pv_ref):
            pad_ref[:, 0:8, :] = jnp.zeros((nb, 8, pad_ref.shape[2]), F32)

    same_head = _head_ones()
    ones_bf = jnp.where(same_head, 1.0, 0.0).astype(BF16)
    cw = cw_ref[...]
    gw = GROUP_WIDTH
    er = _iota((SMALL_W, gw), 0)
    ec = _iota((SMALL_W, gw), 1) // HEAD_DIM
    exp_beta = jnp.where(er == LANE_BETA + ec, 1.0, 0.0).astype(BF16)
    exp_g = jnp.where(er == LANE_GDEC + ec, 1.0, 0.0).astype(BF16)

    prep = []
    for bb in range(nb):
        xq, xk, xv = q_ref[bb], k_ref[bb], v_ref[bb]
        q = _silu(_causal_conv4(xq, pq_ref.at[bb], cw[:, 0:gw]))
        k = _silu(_causal_conv4(xk, pk_ref.at[bb], cw[:, gw:2 * gw]))
        v = _silu(_causal_conv4(xv, pv_ref.at[bb], cw[:, 2 * gw:3 * gw]))
        q = q * lax.rsqrt(_dot_exact_rhs(q * q, ones_bf) + EPS) * (HEAD_DIM ** -0.5)
        k = k * lax.rsqrt(_dot_exact_rhs(k * k, ones_bf) + EPS)
        sm = sm_ref[bb]
        betax = _dot_exact_rhs(sm, exp_beta)
        gx = _dot_exact_rhs(sm, exp_g)
        prep.append((q, k, v, betax, gx))

    n_chunks = T // C
    trow = _iota((C, gw), 0)
    scol = _iota((C, gw), 1) % HEAD_DIM
    incl = scol <= trow
    strict = scol < trow
    eye = jnp.where(scol == trow, 1.0, 0.0)

    def bdiag(x_cat):
        return _stack_heads(x_cat).astype(BF16)

    chains = []
    for ci in range(n_chunks):
        sl = slice(ci * C, (ci + 1) * C)
        for bb in range(nb):
            q, k, v, betax, gx = prep[bb]
            qc, kc, vc, bx, gc = q[sl], k[sl], v[sl], betax[sl], gx[sl]
            grow = jnp.sum(jnp.where(scol == trow, gc, 0.0), axis=0, keepdims=True)
            eg = jnp.exp(gc)
            kb = kc * bx
            g_last = gc[C - 1:C, :]
            gamma = jnp.where(incl, jnp.exp(jnp.where(incl, gc - grow, 0.0)), 0.0)
            ks_bf = bdiag(kc)
            m = jnp.where(strict, _dot_nt(kb.astype(BF16), ks_bf) * gamma, 0.0)
            chains.append(dict(
                x=eye - m, y=m,
                a_qk=(_dot_nt(qc.astype(BF16), ks_bf) * gamma).astype(BF16),
                rhs_u=bdiag(vc * bx),
                rhs_w=bdiag(kb * eg),
                q_dec=(qc * eg).astype(BF16),
                k_dec=(kc * jnp.exp(g_last - gc)).astype(BF16),
                decay=jnp.exp(g_last)))
    for ch in chains:
        ch["ybd"] = bdiag(ch["y"])
    for _ in range(5):
        for ch in chains:
            ch["y"] = _dot(ch["y"].astype(BF16), ch["ybd"])
            ch["ybd"] = bdiag(ch["y"])
        for ch in chains:
            ch["x"] = ch["x"] + _dot(ch["x"].astype(BF16), ch["ybd"])
    for ch in chains:
        t_inv = ch["x"].astype(BF16)
        ch["u"] = _dot(t_inv, ch["rhs_u"])
        ch["wk"] = _dot(t_inv, ch["rhs_w"]).astype(BF16)

    outs = [[] for _ in range(nb)]
    for ci in range(n_chunks):
        for bb in range(nb):
            ch = chains[ci * nb + bb]
            st = st_ref[bb]
            st_bf = st.astype(BF16)
            v_new = ch["u"] - _dot(ch["wk"], st_bf)
            outs[bb].append(_dot(ch["q_dec"], st_bf) + _dot(ch["a_qk"], bdiag(v_new)))
            upd = _dot_tn(ch["k_dec"], v_new.astype(BF16))
            st_ref[bb] = st * ch["decay"] + jnp.where(same_head, upd, 0.0)
    for bb in range(nb):
        o = jnp.concatenate(outs[bb], axis=0)
        o_ref[bb] = _head_rms(o, ones_bf) * _silu(z_ref[bb])


def gdn(proj3, sm, conv_w):
    b, s, _ = proj3.shape
    t = GDN_TILE
    gw = GROUP_WIDTH
    nb = GDN_BATCH
    col = lambda cidx: pl.BlockSpec((nb, t, gw), lambda i, j: (i, j, cidx))
    return pl.pallas_call(
        _gdn_kernel,
        grid=(b // nb, s // t),
        in_specs=[col(COL_DQ), col(COL_DK), col(COL_DV), col(COL_DZ),
                  pl.BlockSpec((nb, t, SMALL_W), lambda i, j: (i, j, 0)),
                  pl.BlockSpec((CONV_W, 3 * gw), lambda i, j: (0, 0))],
        out_specs=pl.BlockSpec((nb, t, gw), lambda i, j: (i, j, 0)),
        out_shape=jax.ShapeDtypeStruct((b, s, gw), F32),
        scratch_shapes=[pltpu.VMEM((nb, gw, gw), F32)] + [pltpu.VMEM((nb, 8 + t, gw), F32)] * 3,
        compiler_params=_cparams("parallel", "arbitrary"),
        name="gdn",
    )(proj3, proj3, proj3, proj3, sm, conv_w)


def _mix_out(x, ys, gain, w_ref):
    gw = GROUP_WIDTH
    acc = x
    for i, y in enumerate(ys):
        yg = (y * gain[:, i * gw:(i + 1) * gw]).astype(BF16)
        acc = acc + _dot(yg, w_ref[i * gw:(i + 1) * gw, :])
    return acc


def _mem_attention(x, g, wq_ref, kv, wo_ref):
    t = x.shape[0]
    gw = GROUP_WIDTH
    ms = jnp.mean(x * x, axis=-1, keepdims=True)
    h = ((x * lax.rsqrt(ms + EPS)) * g).astype(BF16)
    q = _dot(h, wq_ref[...])
    k = kv[:, 0:gw].astype(BF16)
    v = kv[:, gw:2 * gw].astype(BF16)
    s = _dot_nt(k, _stack_heads(q).astype(BF16)) * (HEAD_DIM ** -0.5)
    s = s - jnp.max(s, axis=0, keepdims=True)
    p = jnp.exp(s)
    p = p * (1.0 / jnp.sum(p, axis=0, keepdims=True))
    ot = _dot_tn(v, p.astype(BF16))
    oc = jnp.concatenate([ot[hh * HEAD_DIM:(hh + 1) * HEAD_DIM, hh * t:(hh + 1) * t] for hh in range(HEADS)], axis=0).T
    return x + _dot(oc.astype(BF16), wo_ref[...])


def _post_mix_kernel(x_ref, ya_ref, yb_ref, yc_ref, yd_ref, gain_ref, wout_ref, gm_ref, wq_ref, kv_ref, wo_ref,
                     g_ref, w_ref, b_ref, x2_ref, h_ref, r_ref, rt_ref, n_ref, cnt_ref):
    @pl.when((pl.program_id(0) == 0) & (pl.program_id(1) == 0))
    def _():
        cnt_ref[...] = jnp.zeros_like(cnt_ref)

    x1 = _mix_out(x_ref[0], (ya_ref[0], yb_ref[0], yc_ref[0], yd_ref[0]), gain_ref[...], wout_ref)
    x = _mem_attention(x1, gm_ref[...], wq_ref, kv_ref[0], wo_ref)
    x2_ref[0] = x
    tm = x.shape[0]
    ms = jnp.mean(x * x, axis=-1, keepdims=True)
    h = (x * lax.rsqrt(ms + EPS)) * g_ref[...]
    h_ref[...] = _pack_bf16_pairs(h)
    hh, hm, _ = _split3(h)
    w = w_ref[...]
    wh, wm, _ = _split3(w)
    logits = _dot(hh, wh) + _dot(hh, wm) + _dot(hm, wh) + b_ref[...]
    lane_i = _iota(logits.shape, 1)
    lane = lane_i.astype(F32)
    lane_grp = (lane_i // EXPERTS_PER_GROUP).astype(F32)
    big = jnp.float32(1e9)
    neg = -jnp.inf
    is_grp = (lane_i >= N_EXPERTS) & (lane_i < N_EXPERTS + N_EXPERT_GROUPS)
    gl = jnp.where(is_grp, logits, neg)
    gmax = jnp.max(gl, axis=-1, keepdims=True)
    p_grp = 1.0 / jnp.sum(jnp.exp(gl - gmax), axis=-1, keepdims=True)
    g_sel = jnp.min(jnp.where(gl == gmax, lane, big), axis=-1, keepdims=True) - N_EXPERTS
    in_grp = (lane_i < N_EXPERTS) & (lane_grp == g_sel)
    el = jnp.where(in_grp, logits, neg)
    m1 = jnp.max(el, axis=-1, keepdims=True)
    i1 = jnp.min(jnp.where(el == m1, lane, big), axis=-1, keepdims=True)
    el2 = jnp.where(lane == i1, neg, el)
    m2 = jnp.max(el2, axis=-1, keepdims=True)
    i2 = jnp.min(jnp.where(el2 == m2, lane, big), axis=-1, keepdims=True)
    e21 = jnp.exp(m2 - m1)
    w1 = 1.0 / (1.0 + e21)
    w2 = e21 / (1.0 + e21)
    oh1 = lane == i1
    oh2 = lane == i2
    onehot = jnp.where(oh1 | oh2, 1.0, 0.0)
    strict = jnp.where(_iota((tm, tm), 0) > _iota((tm, tm), 1), 1.0, 0.0).astype(BF16)
    before = _dot(strict, onehot.astype(BF16)) + cnt_ref[...]
    rank1 = jnp.sum(jnp.where(oh1, before, 0.0), axis=-1, keepdims=True)
    rank2 = jnp.sum(jnp.where(oh2, before, 0.0), axis=-1, keepdims=True)
    cnt = before[tm - 1:tm, :] + onehot[tm - 1:tm, :]
    cnt_ref[...] = cnt
    n_ref[...] = jnp.broadcast_to(cnt, n_ref.shape)
    cols = (i1, i2, p_grp * w1, p_grp * w2, rank1, rank2)
    route = jnp.zeros(logits.shape, F32)
    for j, col in enumerate(cols):
        route = jnp.where(lane_i == j, col, route)
    r_ref[...] = route
    rt_ref[...] = route.T[0:8, :]


def post_mix(x3, ya, yb, yc, yd, gain, w_out_bf16, g_mem, wq_bf16, kv3, wo_bf16, g_ffn, w_router, b_router):
    b, s, d = x3.shape
    t = ROW_TILE
    gw = GROUP_WIDTH
    nt = s // t
    mlen = kv3.shape[1]
    tile = lambda w: pl.BlockSpec((1, t, w), lambda i, j: (i, j, 0))
    const = lambda shape: pl.BlockSpec(shape, lambda i, j: (0,) * len(shape))
    tok = lambda w: pl.BlockSpec((t, w), lambda i, j: (i * nt + j, 0))
    return pl.pallas_call(
        _post_mix_kernel,
        grid=(b, nt),
        in_specs=[tile(d), tile(gw), tile(gw), tile(gw), tile(gw),
                  const((1, 4 * gw)), const((4 * gw, d)),
                  const((1, d)), const((d, gw)), pl.BlockSpec((1, mlen, 2 * gw), lambda i, j: (i, 0, 0)), const((gw, d)),
                  const((1, d)), const((d, SMALL_W)), const((1, SMALL_W))],
        out_specs=[tile(d), tok(d // 2), tok(SMALL_W), pl.BlockSpec((8, t), lambda i, j: (0, i * nt + j)),
                   const((8, SMALL_W))],
        out_shape=[jax.ShapeDtypeStruct((b, s, d), F32), jax.ShapeDtypeStruct((b * s, d // 2), jnp.uint32),
                   jax.ShapeDtypeStruct((b * s, SMALL_W), F32), jax.ShapeDtypeStruct((8, b * s), F32),
                   jax.ShapeDtypeStruct((8, SMALL_W), F32)],
        scratch_shapes=[pltpu.VMEM((1, SMALL_W), F32)],
        compiler_params=_cparams("arbitrary", "arbitrary"),
        name="post_mix",
    )(x3, ya, yb, yc, yd, gain.reshape(1, 4 * gw), w_out_bf16, g_mem.reshape(1, d), wq_bf16, kv3, wo_bf16,
      g_ffn.reshape(1, d), w_router, b_router)


def _experts_kernel(te_ref, nu_ref, run_ref, nxt_ref, xs_ref, wgu_hbm, wdn_hbm, y_ref,
                    wgu_buf, wdn_buf, wgu_bf, wdn_bf, sem, *, layer):
    i = pl.program_id(0)

    def weight_copies(expert, slot):
        return (pltpu.make_async_copy(wgu_hbm.at[layer, expert], wgu_buf.at[slot], sem.at[slot, 0]),
                pltpu.make_async_copy(wdn_hbm.at[layer, expert], wdn_buf.at[slot], sem.at[slot, 1]))

    @pl.when(i >= nu_ref[0])
    def _():
        y_ref[...] = jnp.zeros_like(y_ref)

    @pl.when(i < nu_ref[0])
    def _():
        expert = te_ref[i]
        slot = run_ref[i] % 2

        @pl.when(i == 0)
        def _():
            for c in weight_copies(expert, slot):
                c.start()

        @pl.when((i == 0) | (te_ref[jnp.maximum(i - 1, 0)] != expert))
        def _():
            for c in weight_copies(expert, slot):
                c.wait()
            nxt = nxt_ref[i]

            @pl.when(nxt >= 0)
            def _():
                for c in weight_copies(nxt, 1 - slot):
                    c.start()

            wgu_bf[...] = wgu_buf[slot].astype(BF16)
            wdn_bf[...] = wdn_buf[slot].astype(BF16)

        gu = _dot(_unpack_bf16_pairs(xs_ref[...]).astype(BF16), wgu_bf[...])
        act = _silu(gu[:, 0:D_EXPERT]) * gu[:, D_EXPERT:2 * D_EXPERT]
        y_ref[...] = _pack_bf16_pairs(_dot(act.astype(BF16), wdn_bf[...]))


def moe_experts(xs, tile_expert, n_used, run_index, next_expert, w_gu, w_dn, layer, tm):
    n_rows, dp = xs.shape
    d = 2 * dp
    return pl.pallas_call(
        functools.partial(_experts_kernel, layer=layer),
        grid_spec=pltpu.PrefetchScalarGridSpec(
            num_scalar_prefetch=4,
            grid=(n_rows // tm,),
            in_specs=[pl.BlockSpec((tm, dp), lambda i, te, nu, run, nxt: (jnp.minimum(i, nu[0] - 1), 0)),
                      pl.BlockSpec(memory_space=pl.ANY),
                      pl.BlockSpec(memory_space=pl.ANY)],
            out_specs=pl.BlockSpec((tm, dp), lambda i, te, nu, run, nxt: (i, 0)),
            scratch_shapes=[pltpu.VMEM((2, d, 2 * D_EXPERT), F32), pltpu.VMEM((2, D_EXPERT, d), F32),
                            pltpu.VMEM((d, 2 * D_EXPERT), BF16), pltpu.VMEM((D_EXPERT, d), BF16),
                            pltpu.SemaphoreType.DMA((2, 2))]),
        out_shape=jax.ShapeDtypeStruct((n_rows, dp), jnp.uint32),
        compiler_params=_cparams("arbitrary"),
        name="moe_experts",
    )(tile_expert, n_used, run_index, next_expert, xs, w_gu, w_dn)


def sc_gather_rows(table, idx):
    n_idx = idx.shape[0]
    width = table.shape[1]
    info = plsc.get_sparse_core_info()
    n_workers = info.num_cores * info.num_subcores
    per_worker = n_idx // n_workers
    chunk = SC_GATHER_CHUNK
    assert n_idx % (n_workers * chunk) == 0 and width % info.num_lanes == 0
    mesh = plsc.VectorSubcoreMesh(core_axis_name="c", subcore_axis_name="s")

    def body(table_hbm, idx_hbm, out_hbm, idx_v, rows_v, sem):
        wid = lax.axis_index("s") * info.num_cores + lax.axis_index("c")
        base = wid * per_worker

        @pl.loop(0, per_worker // chunk)
        def _(j):
            off = pl.multiple_of(base + j * chunk, chunk)
            pltpu.sync_copy(idx_hbm.at[pl.ds(off, chunk)], idx_v)
            pltpu.async_copy(table_hbm.at[idx_v], rows_v, sem).wait()
            pltpu.sync_copy(rows_v, out_hbm.at[pl.ds(off, chunk)])

    return pl.kernel(
        body,
        out_type=jax.ShapeDtypeStruct((n_idx, width), table.dtype),
        mesh=mesh,
        scratch_types=[pltpu.VMEM((chunk,), jnp.int32), pltpu.VMEM((chunk, width), table.dtype),
                       pltpu.SemaphoreType.DMA],
        name="sc_gather_rows",
    )(table, idx)


def sc_dispatch_rows(rows, dest, pad_idx, zero_rows):
    m, width = rows.shape
    n_pad = pad_idx.shape[0]
    info = plsc.get_sparse_core_info()
    n_workers = info.num_cores * info.num_subcores
    chunk = SC_GATHER_CHUNK
    assert m % (n_workers * chunk) == 0 and n_pad % (n_workers * chunk) == 0 and zero_rows.shape == (chunk, width)
    rows_per_worker = m // n_workers
    pad_per_worker = n_pad // n_workers
    mesh = plsc.VectorSubcoreMesh(core_axis_name="c", subcore_axis_name="s")

    def body(rows_hbm, dest_hbm, pad_hbm, zero_hbm, out_hbm, idx_v, rows_v, sem):
        wid = lax.axis_index("s") * info.num_cores + lax.axis_index("c")

        @pl.loop(0, rows_per_worker // chunk)
        def _(j):
            off = pl.multiple_of(wid * rows_per_worker + j * chunk, chunk)
            pltpu.sync_copy(rows_hbm.at[pl.ds(off, chunk)], rows_v)
            for slot in range(2):
                pltpu.sync_copy(dest_hbm.at[pl.ds(slot * m + off, chunk)], idx_v)
                pltpu.async_copy(rows_v, out_hbm.at[idx_v], sem).wait()

        pltpu.sync_copy(zero_hbm, rows_v)

        @pl.loop(0, pad_per_worker // chunk)
        def _(j):
            off = pl.multiple_of(wid * pad_per_worker + j * chunk, chunk)
            pltpu.sync_copy(pad_hbm.at[pl.ds(off, chunk)], idx_v)
            pltpu.async_copy(rows_v, out_hbm.at[idx_v], sem).wait()

    return pl.kernel(
        body,
        out_type=jax.ShapeDtypeStruct((2 * m + n_pad, width), rows.dtype),
        mesh=mesh,
        scratch_types=[pltpu.VMEM((chunk,), jnp.int32), pltpu.VMEM((chunk, width), rows.dtype),
                       pltpu.SemaphoreType.DMA],
        name="sc_dispatch_rows",
    )(rows, dest, pad_idx, zero_rows)


def _combine_rows_kernel(x_ref, r_ref, g_ref, y1_ref, y2_ref, o_ref, *, final_norm):
    r = r_ref[...]
    out = x_ref[...] + r[:, 2:3] * _unpack_bf16_pairs(y1_ref[...]) + r[:, 3:4] * _unpack_bf16_pairs(y2_ref[...])
    if final_norm:
        ms = jnp.mean(out * out, axis=-1, keepdims=True)
        out = (out * lax.rsqrt(ms + EPS)) * g_ref[...]
    o_ref[...] = out


def moe_combine_rows(x2, route, y_rows, g_final, tm, final_norm):
    m, d = x2.shape
    steps = m // tm
    return pl.pallas_call(
        functools.partial(_combine_rows_kernel, final_norm=final_norm),
        grid=(steps,),
        in_specs=[pl.BlockSpec((tm, d), lambda i: (i, 0)),
                  pl.BlockSpec((tm, SMALL_W), lambda i: (i, 0)),
                  pl.BlockSpec((1, d), lambda i: (0, 0)),
                  pl.BlockSpec((tm, d // 2), lambda i: (i, 0)),
                  pl.BlockSpec((tm, d // 2), lambda i: (i + steps, 0))],
        out_specs=pl.BlockSpec((tm, d), lambda i: (i, 0)),
        out_shape=jax.ShapeDtypeStruct((m, d), F32),
        compiler_params=_cparams("parallel"),
        name="moe_combine_rows",
    )(x2, route, g_final.reshape(1, d), y_rows, y_rows)


def moe_routed(x2, h, route, route_t, counts, w_gu, w_dn, layer, g_final, final_norm):
    m, d = x2.shape
    te_rows = EXPERT_TILE
    n_rows = 2 * m + N_EXPERTS * te_rows
    cnt = counts[0, 0:N_EXPERTS].astype(jnp.int32)
    padded = (cnt + te_rows - 1) // te_rows * te_rows
    seg_end = jnp.cumsum(padded)
    seg_start = seg_end - padded
    idx = route_t[0:2].astype(jnp.int32)
    onehot = idx[:, :, None] == jnp.arange(N_EXPERTS, dtype=jnp.int32)
    dest = (jnp.sum(jnp.where(onehot, seg_start, 0), axis=-1) + route_t[4:6].astype(jnp.int32)).reshape(2 * m)
    tile_start = jnp.arange(n_rows // te_rows, dtype=jnp.int32) * te_rows
    tile_expert = jnp.minimum(jnp.sum((tile_start[:, None] >= seg_end[None, :]).astype(jnp.int32), axis=1), N_EXPERTS - 1)
    n_used = (seg_end[N_EXPERTS - 1] // te_rows).reshape(1)
    n_pad = n_rows - 2 * m
    experts = jnp.arange(N_EXPERTS, dtype=jnp.int32)
    pad_end = jnp.cumsum(padded - cnt)
    k = jnp.arange(n_pad, dtype=jnp.int32)
    k_expert = jnp.sum((k[:, None] >= pad_end[None, :]).astype(jnp.int32), axis=1)
    first_pad = seg_start + cnt - (pad_end - (padded - cnt))
    in_segment = jnp.sum(jnp.where(k_expert[:, None] == experts[None, :], first_pad[None, :], 0), axis=1) + k
    pad_idx = jnp.where(k_expert < N_EXPERTS, in_segment, seg_end[N_EXPERTS - 1] + k - pad_end[N_EXPERTS - 1])
    xs = sc_dispatch_rows(h, dest, pad_idx, jnp.zeros((SC_GATHER_CHUNK, d // 2), h.dtype))
    n_tiles = n_rows // te_rows
    tile_ids = jnp.arange(n_tiles, dtype=jnp.int32)
    prev_expert = jnp.concatenate([jnp.full((1,), -1, jnp.int32), tile_expert[:-1]])
    run_index = jnp.maximum(jnp.cumsum(((tile_expert != prev_expert) & (tile_ids < n_used[0])).astype(jnp.int32)) - 1, 0)
    next_tile = jnp.take(seg_end, tile_expert) // te_rows
    next_expert = jnp.where(next_tile < n_used[0], jnp.take(tile_expert, jnp.minimum(next_tile, n_tiles - 1)), -1)
    ys = moe_experts(xs, tile_expert, n_used, run_index, next_expert, w_gu, w_dn, layer, tm=te_rows)
    y_rows = sc_gather_rows(ys, dest)
    return moe_combine_rows(x2, route, y_rows, g_final, tm=ROW_TILE, final_norm=final_norm)


def _w_in_prep_kernel(w_ref, o_ref):
    gw = GROUP_WIDTH
    a_end = 7 * gw
    c_start = a_end + HEADS
    c_end = c_start + 5 * gw
    z_start = c_end + 2 * HEADS
    assert a_end % SMALL_W == LANE_FOX and c_end % SMALL_W == LANE_BETA and LANE_GDEC == LANE_BETA + HEADS
    w = w_ref[...]
    o_ref[:, 0:a_end] = w[:, 0:a_end].astype(BF16)
    o_ref[:, a_end:a_end + 5 * gw] = w[:, c_start:c_end].astype(BF16)
    o_ref[:, 12 * gw:13 * gw] = w[:, z_start:z_start + gw].astype(BF16)
    blk_fox = w[:, a_end:a_end + SMALL_W]
    blk_gdn = w[:, c_end - LANE_BETA:c_end - LANE_BETA + SMALL_W]
    lane = _iota(blk_fox.shape, 1)
    small = jnp.where(lane < LANE_BETA, blk_fox, jnp.where(lane < LANE_GDEC + HEADS, blk_gdn, 0.0))
    o_ref[:, 13 * gw:13 * gw + SMALL_W] = small.astype(BF16)
    o_ref[:, 13 * gw + SMALL_W:] = jnp.zeros((w.shape[0], SMALL_W), BF16)


def w_in_prep(w_all, layer):
    _, d, n = w_all.shape
    tr = 128
    return pl.pallas_call(
        _w_in_prep_kernel,
        grid=(d // tr,),
        in_specs=[pl.BlockSpec((None, tr, n), lambda i: (layer, i, 0))],
        out_specs=pl.BlockSpec((tr, N_PROJ), lambda i: (i, 0)),
        out_shape=jax.ShapeDtypeStruct((d, N_PROJ), BF16),
        compiler_params=_cparams("parallel"),
        name="w_in_prep",
    )(w_all)


def _block_diag(w):
    h = w.shape[0]
    eye = jnp.eye(h, dtype=w.dtype)
    return (eye[:, None, :, None] * w[:, :, None, :]).reshape(h * HEAD_DIM, h * HEAD_DIM)


def kernel(x, mem, norm_mix, w_in, hgrn_lb, fox_fb, lru_conv_w, lru_conv_b, lru_wa, lru_ba, lru_wx, lru_bx, lru_lam, gdn_conv_w, gdn_a_log, gdn_dt_bias, mix_gain, w_out, norm_mem, norm_memkv, w_mq, w_mkv, w_mo, norm_ffn, w_rg, b_rg, w_re, b_re, w_e_gu, w_e_dn, norm_final):
    b, s, d = x.shape
    depth = w_in.shape[0]
    t_tok = b * s
    mlen = mem.shape[1]
    gw = GROUP_WIDTH

    lb_all = jnp.cumsum(jax.nn.softmax(hgrn_lb.astype(F32), axis=0), axis=0)
    lb_all = lb_all - lb_all[0]

    x = x.astype(F32)
    for l in range(depth):
        proj = norm_matmul(x.reshape(t_tok, d), norm_mix[l], w_in_prep(w_in, l), tm=ROW_TILE)
        proj3 = proj.reshape(b, s, N_PROJ)
        prm = jnp.zeros((8, SMALL_W), F32)
        prm = prm.at[0, LANE_FOX:LANE_FOX + HEADS].set(fox_fb[l].astype(F32))
        prm = prm.at[0, LANE_GDEC:LANE_GDEC + HEADS].set(gdn_dt_bias[l].astype(F32))
        prm = prm.at[1, LANE_GDEC:LANE_GDEC + HEADS].set(gdn_a_log[l].astype(F32))
        sm, c_exp = small_prep(proj3, prm)

        ya = hgrn2(proj3, lb_all[l])
        yb = fox_attention(proj3, c_exp)
        yc = rglru(proj3, lru_conv_w[l], lru_conv_b[l], _block_diag(lru_wa[l]).astype(BF16), lru_ba[l],
                   _block_diag(lru_wx[l]).astype(BF16), lru_bx[l], lru_lam[l])
        yd = gdn(proj3, sm, gdn_conv_w[l])

        kv = norm_matmul(mem.reshape(b * mlen, d), norm_memkv[l], w_mkv[l].astype(BF16), tm=256)
        w_router = jnp.concatenate([w_re[l], w_rg[l], jnp.zeros((d, SMALL_W - N_EXPERTS - N_EXPERT_GROUPS), F32)], axis=1)
        b_router = jnp.concatenate([b_re[l], b_rg[l], jnp.zeros((SMALL_W - N_EXPERTS - N_EXPERT_GROUPS,), F32)]).reshape(1, SMALL_W)
        x, h, route, route_t, counts = post_mix(x, ya, yb, yc, yd, mix_gain[l], w_out[l].astype(BF16), norm_mem[l],
                                                w_mq[l].astype(BF16), kv.reshape(b, mlen, 2 * gw), w_mo[l].astype(BF16),
                                                norm_ffn[l], w_router, b_router)

        x = moe_routed(x.reshape(t_tok, d), h, route, route_t, counts, w_e_gu, w_e_dn, l,
                       norm_final, final_norm=(l == depth - 1)).reshape(b, s, d)

    return x
```

```python
import functools
import math

import jax
import jax.numpy as jnp
from jax import lax
from jax.experimental import pallas as pl
from jax.experimental.pallas import tpu as pltpu
from jax.experimental.pallas import tpu_sc as plsc

F32 = jnp.float32
BF16 = jnp.bfloat16

HEAD_DIM = 64
GROUP_WIDTH = 256
HEADS = GROUP_WIDTH // HEAD_DIM
GDN_CHUNK = 64
GDN_TILE = 128
GDN_BATCH = 4
FOX_BATCH = 2
HGRN_BATCH = 4
HGRN_CHUNK = 16
CONV_W = 4
LRU_C = 8.0
EPS = 1e-6
N_EXPERT_GROUPS = 4
EXPERTS_PER_GROUP = 8
N_EXPERTS = N_EXPERT_GROUPS * EXPERTS_PER_GROUP
D_EXPERT = 256
SC_GATHER_CHUNK = 128
EXPERT_TILE = 512
SMALL_W = 128
SEQ_TILE = 256
ROW_TILE = 512
VMEM_LIMIT = 56 * 1024 * 1024

(COL_AQ, COL_AF, COL_AI, COL_AG, COL_BQ, COL_BK, COL_BV, COL_CX, COL_CG,
 COL_DQ, COL_DK, COL_DV, COL_DZ) = range(13)
N_PROJ = 13 * GROUP_WIDTH + 2 * SMALL_W
COL_SMALL = 13 * GROUP_WIDTH // SMALL_W
LANE_FOX = 0
LANE_BETA = 4
LANE_GDEC = 8


def _cparams(*sem):
    return pltpu.CompilerParams(dimension_semantics=sem, vmem_limit_bytes=VMEM_LIMIT)


def _dot(a, b):
    return jnp.dot(a, b, preferred_element_type=F32)


def _dot_nt(a, b):
    return lax.dot_general(a, b, (((1,), (1,)), ((), ())), preferred_element_type=F32)


def _dot_tn(a, b):
    return lax.dot_general(a, b, (((0,), (0,)), ((), ())), preferred_element_type=F32)


def _split3(x):
    h = x.astype(BF16)
    r = x - h.astype(F32)
    m = r.astype(BF16)
    l = (r - m.astype(F32)).astype(BF16)
    return h, m, l


def _dot_exact_rhs(x, w_bf16):
    h, m, l = _split3(x)
    return _dot(h, w_bf16) + _dot(m, w_bf16) + _dot(l, w_bf16)


def _dot_exact_lhs(w_bf16, x):
    h, m, l = _split3(x)
    return _dot(w_bf16, h) + _dot(w_bf16, m) + _dot(w_bf16, l)


def _iota(shape, dim):
    return lax.broadcasted_iota(jnp.int32, shape, dim)


def _head_ones(n=GROUP_WIDTH):
    r = _iota((n, n), 0) // HEAD_DIM
    c = _iota((n, n), 1) // HEAD_DIM
    return r == c


def _sigmoid(x):
    return 1.0 / (1.0 + jnp.exp(-x))


def _silu(x):
    return x * _sigmoid(x)


def _log_sigmoid(x):
    return jnp.minimum(x, 0.0) - jnp.log1p(jnp.exp(-jnp.abs(x)))


def _softplus(x):
    return jnp.maximum(x, 0.0) + jnp.log1p(jnp.exp(-jnp.abs(x)))


def _gelu_tanh(x):
    return 0.5 * x * (1.0 + jnp.tanh(math.sqrt(2.0 / math.pi) * (x + 0.044715 * (x * x * x))))


def _pack_bf16_pairs(x):
    n = x.shape[1] // 2
    u = lax.bitcast_convert_type(x.astype(BF16).astype(F32), jnp.uint32)
    return u[:, :n] | (u[:, n:] >> 16)


def _unpack_bf16_pairs(p):
    hi = lax.bitcast_convert_type(p & jnp.uint32(0xFFFF0000), F32)
    lo = lax.bitcast_convert_type(p << 16, F32)
    return jnp.concatenate([hi, lo], axis=1)


def _head_mean_sq(x, ones_bf16):
    return _dot_exact_rhs(x * x, ones_bf16) * (1.0 / HEAD_DIM)


def _head_rms(x, ones_bf16):
    return x * lax.rsqrt(_head_mean_sq(x, ones_bf16) + EPS)


def _stack_heads(x):
    lane_head = _iota(x.shape, 1) // HEAD_DIM
    parts = []
    for h in range(HEADS):
        parts.append(jnp.where(lane_head == h, x, 0.0))
    return jnp.concatenate(parts, axis=0)


def _causal_conv4(x, pad_ref, w):
    r = x.shape[0]
    pad_ref[8:8 + r, :] = x
    acc = x * w[CONV_W - 1:CONV_W, :]
    for k in range(1, CONV_W):
        acc = acc + pad_ref[8 - k:8 - k + r, :] * w[CONV_W - 1 - k:CONV_W - k, :]
    pad_ref[0:8, :] = x[r - 8:r, :]
    return acc


def _norm_matmul_kernel(x_ref, g_ref, w_ref, o_ref):
    x = x_ref[...]
    ms = jnp.mean(x * x, axis=-1, keepdims=True)
    h = (x * lax.rsqrt(ms + EPS)) * g_ref[...]
    o_ref[...] = _dot(h.astype(BF16), w_ref[...]).astype(o_ref.dtype)


def norm_matmul(x, g, w_bf16, tm, out_dtype=F32):
    m, d = x.shape
    n = w_bf16.shape[1]
    return pl.pallas_call(
        _norm_matmul_kernel,
        grid=(m // tm,),
        in_specs=[pl.BlockSpec((tm, d), lambda i: (i, 0)),
                  pl.BlockSpec((1, d), lambda i: (0, 0)),
                  pl.BlockSpec((d, n), lambda i: (0, 0))],
        out_specs=pl.BlockSpec((tm, n), lambda i: (i, 0)),
        out_shape=jax.ShapeDtypeStruct((m, n), out_dtype),
        compiler_params=_cparams("parallel"),
        name="norm_matmul",
    )(x, g.reshape(1, d), w_bf16)


def _in_proj_kernel(x_ref, g_ref, w_ref, p_ref, o_ref, sm_ref, cx_ref, carry_ref, *, tiles_per_seq):
    @pl.when(pl.program_id(0) % tiles_per_seq == 0)
    def _():
        carry_ref[...] = jnp.zeros_like(carry_ref)

    x = x_ref[...]
    tm = x.shape[0]
    ms = jnp.mean(x * x, axis=-1, keepdims=True)
    h = (x * lax.rsqrt(ms + EPS)) * g_ref[...]
    o_ref[...] = _dot(h.astype(BF16), w_ref[...])

    W = SMALL_W
    blk = SEQ_TILE
    lane = _iota((blk, W), 1)
    is_fox = (lane >= LANE_FOX) & (lane < LANE_FOX + HEADS)
    is_beta = (lane >= LANE_BETA) & (lane < LANE_BETA + HEADS)
    is_gdec = (lane >= LANE_GDEC) & (lane < LANE_GDEC + HEADS)
    r = _iota((blk, blk), 0)
    c = _iota((blk, blk), 1)
    tril_all = jnp.where(r >= c, 1.0, 0.0).astype(BF16)
    tril_chunk = jnp.where((r >= c) & (r // GDN_CHUNK == c // GDN_CHUNK), 1.0, 0.0).astype(BF16)
    neg_exp_a = -jnp.exp(p_ref[1:2, :])
    carry = carry_ref[...]
    for i in range(tm // blk):
        sl = pl.ds(i * blk, blk)
        z = o_ref[sl, COL_SMALL * W:(COL_SMALL + 1) * W] + p_ref[0:1, :]
        fox = jnp.where(is_fox, _log_sigmoid(z), 0.0)
        beta = jnp.where(is_beta, _sigmoid(z), 0.0)
        gdec = jnp.where(is_gdec, neg_exp_a * _softplus(z), 0.0)
        cf = _dot_exact_lhs(tril_all, fox) + carry
        cg = _dot_exact_lhs(tril_chunk, gdec)
        carry = cf[blk - 1:blk, :]
        sm_ref[sl, :] = cf + cg + beta
        both = cf + cg
        for j, src_lane in enumerate(range(LANE_FOX, LANE_FOX + HEADS)):
            cx_ref[sl, j * W:(j + 1) * W] = jnp.broadcast_to(both[:, src_lane:src_lane + 1], (blk, W))
    carry_ref[...] = carry


def in_proj(x, g, w_bf16, params, seq_len, tm):
    m, d = x.shape
    n = w_bf16.shape[1]
    w = SMALL_W
    assert seq_len % tm == 0 and tm % SEQ_TILE == 0
    return pl.pallas_call(
        functools.partial(_in_proj_kernel, tiles_per_seq=seq_len // tm),
        grid=(m // tm,),
        in_specs=[pl.BlockSpec((tm, d), lambda i: (i, 0)),
                  pl.BlockSpec((1, d), lambda i: (0, 0)),
                  pl.BlockSpec((d, n), lambda i: (0, 0)),
                  pl.BlockSpec((8, w), lambda i: (0, 0))],
        out_specs=[pl.BlockSpec((tm, n), lambda i: (i, 0)),
                   pl.BlockSpec((tm, w), lambda i: (i, 0)),
                   pl.BlockSpec((tm, HEADS * w), lambda i: (i, 0))],
        out_shape=[jax.ShapeDtypeStruct((m, n), F32), jax.ShapeDtypeStruct((m, w), F32),
                   jax.ShapeDtypeStruct((m, HEADS * w), F32)],
        scratch_shapes=[pltpu.VMEM((1, w), F32)],
        compiler_params=_cparams("arbitrary"),
        name="in_proj",
    )(x, g.reshape(1, d), w_bf16, params)


def _hgrn2_kernel(q_ref, f_ref, i_ref, g_ref, lb_ref, o_ref, st_ref, qs, ks, vs, bs, os_):
    nb = q_ref.shape[0]
    T = q_ref.shape[1]
    C = HGRN_CHUNK
    seqs = range(nb)

    @pl.when(pl.program_id(1) == 0)
    def _():
        st_ref[...] = jnp.zeros_like(st_ref)

    same_head = _head_ones()
    ones_bf = jnp.where(same_head, 1.0, 0.0).astype(BF16)
    lb = lb_ref[...]
    log_lb = jnp.log(lb)
    r = _iota((T, T), 0)
    c = _iota((T, T), 1)
    tril_chunk = jnp.where((r >= c) & (r // C == c // C), 1.0, 0.0).astype(BF16)
    for bb in seqs:
        fl = f_ref[bb]
        c2 = jnp.log1p(-lb) + _log_sigmoid(fl)
        mx = jnp.maximum(log_lb, c2)
        log_f = mx + jnp.log1p(jnp.exp(-jnp.abs(log_lb - c2)))
        bs[bb] = _dot_exact_lhs(tril_chunk, log_f)
        qs[bb] = _silu(q_ref[bb])
        ks[bb] = (1.0 - lb) * _sigmoid(-fl)
        vs[bb] = i_ref[bb]

    H8 = C // 2
    trow8 = _iota((H8, GROUP_WIDTH), 0)

    def chunk(ci, carry):
        r0 = pl.multiple_of(ci * C, C)
        qc = [qs[bb, pl.ds(r0, C), :] for bb in seqs]
        kc = [ks[bb, pl.ds(r0, C), :] for bb in seqs]
        vc = [vs[bb, pl.ds(r0, C), :] for bb in seqs]
        bc = [bs[bb, pl.ds(r0, C), :] for bb in seqs]
        st = [st_ref[bb] for bb in seqs]
        a = []
        for bb in seqs:
            parts = []
            for half, s_range in ((0, range(H8)), (1, range(C))):
                b_half = bc[bb][half * H8:(half + 1) * H8, :]
                q_half = qc[bb][half * H8:(half + 1) * H8, :]
                for s in s_range:
                    blk = jnp.exp(b_half - bc[bb][s:s + 1, :]) * (q_half * kc[bb][s:s + 1, :])
                    if s >= half * H8:
                        blk = jnp.where(trow8 >= s - half * H8, blk, 0.0)
                    parts.append(blk)
            a.append(jnp.concatenate(parts, axis=0).astype(BF16))
        sc = [_dot(a[bb], ones_bf) for bb in seqs]
        o = [_dot_nt((qc[bb] * jnp.exp(bc[bb])).astype(BF16), st[bb].astype(BF16)) for bb in seqs]
        b_last = [bc[bb][C - 1:C, :] for bb in seqs]
        upd = [_dot_tn(vc[bb].astype(BF16), (kc[bb] * jnp.exp(b_last[bb] - bc[bb])).astype(BF16)) for bb in seqs]
        for bb in seqs:
            top = o[bb][0:H8, :]
            bot = o[bb][H8:C, :]
            for s in range(H8):
                top = top + sc[bb][s * H8:(s + 1) * H8, :] * vc[bb][s:s + 1, :]
            for s in range(C):
                bot = bot + sc[bb][(H8 + s) * H8:(H8 + s + 1) * H8, :] * vc[bb][s:s + 1, :]
            os_[bb, pl.ds(r0, C), :] = jnp.concatenate([top, bot], axis=0)
            st_ref[bb] = st[bb] * jnp.exp(b_last[bb]) + jnp.where(same_head, upd[bb], 0.0)
        return carry

    lax.fori_loop(0, T // C, chunk, 0)
    for bb in seqs:
        o_ref[bb] = _head_rms(os_[bb], ones_bf) * _silu(g_ref[bb])


def hgrn2(proj3, lb):
    b, s, _ = proj3.shape
    t = SEQ_TILE
    gw = GROUP_WIDTH
    nb = HGRN_BATCH
    col = lambda cidx: pl.BlockSpec((nb, t, gw), lambda i, j: (i, j, cidx))
    return pl.pallas_call(
        _hgrn2_kernel,
        grid=(b // nb, s // t),
        in_specs=[col(COL_AQ), col(COL_AF), col(COL_AI), col(COL_AG),
                  pl.BlockSpec((1, gw), lambda i, j: (0, 0))],
        out_specs=pl.BlockSpec((nb, t, gw), lambda i, j: (i, j, 0)),
        out_shape=jax.ShapeDtypeStruct((b, s, gw), F32),
        scratch_shapes=[pltpu.VMEM((nb, gw, gw), F32)] + [pltpu.VMEM((nb, t, gw), F32)] * 5,
        compiler_params=_cparams("parallel", "arbitrary"),
        name="hgrn2",
    )(proj3, proj3, proj3, proj3, lb.reshape(1, gw))


def _fox_kernel(q_ref, k_ref, v_ref, cx_ref, o_ref, kb_ref, vt_ref, acc_ref):
    nb = q_ref.shape[0]
    tq = q_ref.shape[1]
    tk = tq
    S = k_ref.shape[1]
    qi = pl.program_id(1)
    seqs = range(nb)

    @pl.when(qi == 0)
    def _():
        for bb in seqs:
            for i in range(S // tk):
                sl = pl.ds(i * tk, tk)
                kb_ref[bb, sl, :] = k_ref[bb, sl, :].astype(BF16)
                vt_ref[bb, :, sl] = v_ref[bb, sl, :].T.astype(BF16)

    lane_head = _iota((tq, GROUP_WIDTH), 1) // HEAD_DIM
    qh = []
    for bb in seqs:
        qsc = q_ref[bb] * (HEAD_DIM ** -0.5)
        qh.append([jnp.where(lane_head == h, qsc, 0.0).astype(BF16) for h in range(HEADS)])
    acc_ref[...] = jnp.zeros_like(acc_ref)
    reps = tq // SMALL_W
    causal = _iota((tk, tq), 0) <= _iota((tk, tq), 1)

    def step(kb, stats, masked):
        k0 = pl.multiple_of(kb * tk, tk)
        new_stats = [[None] * HEADS for _ in seqs]
        strips = [(h, bb) for h in range(HEADS) for bb in seqs]

        def scores(h, bb):
            return _dot_nt(kb_ref[bb, pl.ds(k0, tk), :], qh[bb][h])

        ahead = 4
        queue = [scores(*strips[j]) for j in range(ahead)]
        for i, (h, bb) in enumerate(strips):
            st = queue.pop(0)
            if i + ahead < len(strips):
                queue.append(scores(*strips[i + ahead]))
            rows = slice(h * HEAD_DIM, (h + 1) * HEAD_DIM)
            cx = cx_ref[bb, pl.ds(k0, tk), h * SMALL_W:(h + 1) * SMALL_W]
            st = st - jnp.concatenate([cx] * reps, axis=1)
            if masked:
                st = jnp.where(causal, st, -jnp.inf)
            m_old, l_old = stats[bb][h]
            m_new = jnp.maximum(m_old, jnp.max(st, axis=0, keepdims=True))
            alpha = jnp.exp(m_old - m_new)
            p = jnp.exp(st - m_new)
            l_new = alpha * l_old + jnp.sum(p, axis=0, keepdims=True)
            pv = _dot(vt_ref[bb, rows, pl.ds(k0, tk)], p.astype(BF16))
            acc_ref[bb, rows, :] = alpha * acc_ref[bb, rows, :] + pv
            new_stats[bb][h] = (m_new, l_new)
        return tuple(tuple(new_stats[bb]) for bb in seqs)

    init = tuple(tuple((jnp.full((1, tq), -jnp.inf, F32), jnp.zeros((1, tq), F32)) for _ in range(HEADS)) for _ in seqs)
    stats = lax.fori_loop(0, qi, lambda kb, c: step(kb, c, False), init)
    stats = step(qi, stats, True)

    ones_bf = jnp.where(_head_ones(), 1.0, 0.0).astype(BF16)
    for bb in seqs:
        inv_l = jnp.concatenate([jnp.broadcast_to(1.0 / stats[bb][h][1], (HEAD_DIM, tq)) for h in range(HEADS)], axis=0)
        o_ref[bb] = _head_rms((acc_ref[bb] * inv_l).T, ones_bf)


def fox_attention(proj3, c_exp):
    b, s, _ = proj3.shape
    tq = SEQ_TILE
    gw = GROUP_WIDTH
    nb = FOX_BATCH
    return pl.pallas_call(
        _fox_kernel,
        grid=(b // nb, s // tq),
        in_specs=[pl.BlockSpec((nb, tq, gw), lambda i, j: (i, j, COL_BQ)),
                  pl.BlockSpec((nb, s, gw), lambda i, j: (i, 0, COL_BK)),
                  pl.BlockSpec((nb, s, gw), lambda i, j: (i, 0, COL_BV)),
                  pl.BlockSpec((nb, s, HEADS * SMALL_W), lambda i, j: (i, 0, 0))],
        out_specs=pl.BlockSpec((nb, tq, gw), lambda i, j: (i, j, 0)),
        out_shape=jax.ShapeDtypeStruct((b, s, gw), F32),
        scratch_shapes=[pltpu.VMEM((nb, s, gw), BF16),
                        pltpu.VMEM((nb, gw, s), BF16),
                        pltpu.VMEM((nb, gw, tq), F32)],
        compiler_params=_cparams("parallel", "arbitrary"),
        name="fox_attention",
    )(proj3, proj3, proj3, c_exp)


def _rglru_kernel(x_ref, g_ref, cw_ref, cb_ref, wa_ref, ba_ref, wx_ref, bx_ref, lam_ref, o_ref, prev_ref, h_ref):
    @pl.when(pl.program_id(1) == 0)
    def _():
        prev_ref[0:8, :] = jnp.zeros((8, prev_ref.shape[1]), F32)
        h_ref[...] = jnp.zeros_like(h_ref)

    x = x_ref[0]
    t = x.shape[0]
    xc = _causal_conv4(x, prev_ref, cw_ref[...]) + cb_ref[...]
    xb = xc.astype(BF16)
    r = _sigmoid(_dot(xb, wa_ref[...]) + ba_ref[...])
    ig = _sigmoid(_dot(xb, wx_ref[...]) + bx_ref[...])
    log_a = (-LRU_C * r) * _softplus(-lam_ref[...])
    a = jnp.exp(log_a)
    u = jnp.sqrt(1.0 - jnp.exp(2.0 * log_a)) * (ig * xc)
    row = _iota(a.shape, 0)
    d = 1
    while d < t:
        valid = row >= d
        u = jnp.where(valid, a * pltpu.roll(u, d, 0) + u, u)
        a = jnp.where(valid, a * pltpu.roll(a, d, 0), a)
        d *= 2
    h = a * h_ref[...] + u
    h_ref[...] = h[t - 1:t, :]
    ones_bf = jnp.where(_head_ones(), 1.0, 0.0).astype(BF16)
    o_ref[0] = _head_rms(h * _gelu_tanh(g_ref[0]), ones_bf)


def rglru(proj3, conv_w, conv_b, wa_bd, ba, wx_bd, bx, lam):
    b, s, _ = proj3.shape
    t = ROW_TILE
    gw = GROUP_WIDTH
    row = lambda: pl.BlockSpec((1, gw), lambda i, j: (0, 0))
    return pl.pallas_call(
        _rglru_kernel,
        grid=(b, s // t),
        in_specs=[pl.BlockSpec((1, t, gw), lambda i, j: (i, j, COL_CX)),
                  pl.BlockSpec((1, t, gw), lambda i, j: (i, j, COL_CG)),
                  pl.BlockSpec((CONV_W, gw), lambda i, j: (0, 0)), row(),
                  pl.BlockSpec((gw, gw), lambda i, j: (0, 0)), row(),
                  pl.BlockSpec((gw, gw), lambda i, j: (0, 0)), row(), row()],
        out_specs=pl.BlockSpec((1, t, gw), lambda i, j: (i, j, 0)),
        out_shape=jax.ShapeDtypeStruct((b, s, gw), F32),
        scratch_shapes=[pltpu.VMEM((8 + t, gw), F32), pltpu.VMEM((1, gw), F32)],
        compiler_params=_cparams("parallel", "arbitrary"),
        name="rglru",
    )(proj3, proj3, conv_w, conv_b.reshape(1, gw), wa_bd, ba.reshape(1, gw), wx_bd, bx.reshape(1, gw), lam.reshape(1, gw))


def _gdn_kernel(q_ref, k_ref, v_ref, z_ref, sm_ref, cw_ref, o_ref, st_ref, pq_ref, pk_ref, pv_ref):
    nb = q_ref.shape[0]
    T = q_ref.shape[1]
    C = GDN_CHUNK

    @pl.when(pl.program_id(1) == 0)
    def _():
        st_ref[...] = jnp.zeros_like(st_ref)
        for pad_ref in (pq_ref, pk_ref, pv_ref):
            pad_ref[:, 0:8, :] = jnp.zeros((nb, 8, pad_ref.shape[2]), F32)

    same_head = _head_ones()
    ones_bf = jnp.where(same_head, 1.0, 0.0).astype(BF16)
    cw = cw_ref[...]
    gw = GROUP_WIDTH
    er = _iota((SMALL_W, gw), 0)
    ec = _iota((SMALL_W, gw), 1) // HEAD_DIM
    exp_beta = jnp.where(er == LANE_BETA + ec, 1.0, 0.0).astype(BF16)
    exp_g = jnp.where(er == LANE_GDEC + ec, 1.0, 0.0).astype(BF16)

    prep = []
    for bb in range(nb):
        xq, xk, xv = q_ref[bb], k_ref[bb], v_ref[bb]
        q = _silu(_causal_conv4(xq, pq_ref.at[bb], cw[:, 0:gw]))
        k = _silu(_causal_conv4(xk, pk_ref.at[bb], cw[:, gw:2 * gw]))
        v = _silu(_causal_conv4(xv, pv_ref.at[bb], cw[:, 2 * gw:3 * gw]))
        q = q * lax.rsqrt(_dot_exact_rhs(q * q, ones_bf) + EPS) * (HEAD_DIM ** -0.5)
        k = k * lax.rsqrt(_dot_exact_rhs(k * k, ones_bf) + EPS)
        sm = sm_ref[bb]
        betax = _dot_exact_rhs(sm, exp_beta)
        gx = _dot_exact_rhs(sm, exp_g)
        prep.append((q, k, v, betax, gx))

    n_chunks = T // C
    trow = _iota((C, gw), 0)
    scol = _iota((C, gw), 1) % HEAD_DIM
    incl = scol <= trow
    strict = scol < trow
    eye = jnp.where(scol == trow, 1.0, 0.0)

    def bdiag(x_cat):
        return _stack_heads(x_cat).astype(BF16)

    chains = []
    for ci in range(n_chunks):
        sl = slice(ci * C, (ci + 1) * C)
        for bb in range(nb):
            q, k, v, betax, gx = prep[bb]
            qc, kc, vc, bx, gc = q[sl], k[sl], v[sl], betax[sl], gx[sl]
            grow = jnp.sum(jnp.where(scol == trow, gc, 0.0), axis=0, keepdims=True)
            eg = jnp.exp(gc)
            kb = kc * bx
            g_last = gc[C - 1:C, :]
            gamma = jnp.where(incl, jnp.exp(jnp.where(incl, gc - grow, 0.0)), 0.0)
            ks_bf = bdiag(kc)
            m = jnp.where(strict, _dot_nt(kb.astype(BF16), ks_bf) * gamma, 0.0)
            chains.append(dict(
                x=eye - m, y=m,
                a_qk=(_dot_nt(qc.astype(BF16), ks_bf) * gamma).astype(BF16),
                rhs_u=bdiag(vc * bx),
                rhs_w=bdiag(kb * eg),
                q_dec=(qc * eg).astype(BF16),
                k_dec=(kc * jnp.exp(g_last - gc)).astype(BF16),
                decay=jnp.exp(g_last)))
    for ch in chains:
        ch["ybd"] = bdiag(ch["y"])
    for _ in range(5):
        for ch in chains:
            ch["y"] = _dot(ch["y"].astype(BF16), ch["ybd"])
            ch["ybd"] = bdiag(ch["y"])
        for ch in chains:
            ch["x"] = ch["x"] + _dot(ch["x"].astype(BF16), ch["ybd"])
    for ch in chains:
        t_inv = ch["x"].astype(BF16)
        ch["u"] = _dot(t_inv, ch["rhs_u"])
        ch["wk"] = _dot(t_inv, ch["rhs_w"]).astype(BF16)

    outs = [[] for _ in range(nb)]
    for ci in range(n_chunks):
        for bb in range(nb):
            ch = chains[ci * nb + bb]
            st = st_ref[bb]
            st_bf = st.astype(BF16)
            v_new = ch["u"] - _dot(ch["wk"], st_bf)
            outs[bb].append(_dot(ch["q_dec"], st_bf) + _dot(ch["a_qk"], bdiag(v_new)))
            upd = _dot_tn(ch["k_dec"], v_new.astype(BF16))
            st_ref[bb] = st * ch["decay"] + jnp.where(same_head, upd, 0.0)
    for bb in range(nb):
        o = jnp.concatenate(outs[bb], axis=0)
        o_ref[bb] = _head_rms(o, ones_bf) * _silu(z_ref[bb])


def gdn(proj3, sm, conv_w):
    b, s, _ = proj3.shape
    t = GDN_TILE
    gw = GROUP_WIDTH
    nb = GDN_BATCH
    col = lambda cidx: pl.BlockSpec((nb, t, gw), lambda i, j: (i, j, cidx))
    return pl.pallas_call(
        _gdn_kernel,
        grid=(b // nb, s // t),
        in_specs=[col(COL_DQ), col(COL_DK), col(COL_DV), col(COL_DZ),
                  pl.BlockSpec((nb, t, SMALL_W), lambda i, j: (i, j, 0)),
                  pl.BlockSpec((CONV_W, 3 * gw), lambda i, j: (0, 0))],
        out_specs=pl.BlockSpec((nb, t, gw), lambda i, j: (i, j, 0)),
        out_shape=jax.ShapeDtypeStruct((b, s, gw), F32),
        scratch_shapes=[pltpu.VMEM((nb, gw, gw), F32)] + [pltpu.VMEM((nb, 8 + t, gw), F32)] * 3,
        compiler_params=_cparams("parallel", "arbitrary"),
        name="gdn",
    )(proj3, proj3, proj3, proj3, sm, conv_w)


def _mix_out(x, ys, gain, w_ref):
    gw = GROUP_WIDTH
    acc = x
    for i, y in enumerate(ys):
        yg = (y * gain[:, i * gw:(i + 1) * gw]).astype(BF16)
        acc = acc + _dot(yg, w_ref[i * gw:(i + 1) * gw, :])
    return acc


def _mem_attention(x, g, wq_ref, kv, wo_ref):
    t = x.shape[0]
    gw = GROUP_WIDTH
    ms = jnp.mean(x * x, axis=-1, keepdims=True)
    h = ((x * lax.rsqrt(ms + EPS)) * g).astype(BF16)
    q = _dot(h, wq_ref[...])
    k = kv[:, 0:gw].astype(BF16)
    v = kv[:, gw:2 * gw].astype(BF16)
    s = _dot_nt(k, _stack_heads(q).astype(BF16)) * (HEAD_DIM ** -0.5)
    s = s - jnp.max(s, axis=0, keepdims=True)
    p = jnp.exp(s)
    p = p * (1.0 / jnp.sum(p, axis=0, keepdims=True))
    ot = _dot_tn(v, p.astype(BF16))
    oc = jnp.concatenate([ot[hh * HEAD_DIM:(hh + 1) * HEAD_DIM, hh * t:(hh + 1) * t] for hh in range(HEADS)], axis=0).T
    return x + _dot(oc.astype(BF16), wo_ref[...])


def _post_mix_kernel(x_ref, ya_ref, yb_ref, yc_ref, yd_ref, gain_ref, wout_ref, gm_ref, wq_ref, kv_ref, wo_ref,
                     g_ref, w_ref, b_ref, x2_ref, h_ref, r_ref, rt_ref, n_ref, cnt_ref):
    @pl.when((pl.program_id(0) == 0) & (pl.program_id(1) == 0))
    def _():
        cnt_ref[...] = jnp.zeros_like(cnt_ref)

    x1 = _mix_out(x_ref[0], (ya_ref[0], yb_ref[0], yc_ref[0], yd_ref[0]), gain_ref[...], wout_ref)
    x = _mem_attention(x1, gm_ref[...], wq_ref, kv_ref[0], wo_ref)
    x2_ref[0] = x
    tm = x.shape[0]
    ms = jnp.mean(x * x, axis=-1, keepdims=True)
    h = (x * lax.rsqrt(ms + EPS)) * g_ref[...]
    h_ref[...] = _pack_bf16_pairs(h)
    hh, hm, _ = _split3(h)
    w = w_ref[...]
    wh, wm, _ = _split3(w)
    logits = _dot(hh, wh) + _dot(hh, wm) + _dot(hm, wh) + b_ref[...]
    lane_i = _iota(logits.shape, 1)
    lane = lane_i.astype(F32)
    lane_grp = (lane_i // EXPERTS_PER_GROUP).astype(F32)
    big = jnp.float32(1e9)
    neg = -jnp.inf
    is_grp = (lane_i >= N_EXPERTS) & (lane_i < N_EXPERTS + N_EXPERT_GROUPS)
    gl = jnp.where(is_grp, logits, neg)
    gmax = jnp.max(gl, axis=-1, keepdims=True)
    p_grp = 1.0 / jnp.sum(jnp.exp(gl - gmax), axis=-1, keepdims=True)
    g_sel = jnp.min(jnp.where(gl == gmax, lane, big), axis=-1, keepdims=True) - N_EXPERTS
    in_grp = (lane_i < N_EXPERTS) & (lane_grp == g_sel)
    el = jnp.where(in_grp, logits, neg)
    m1 = jnp.max(el, axis=-1, keepdims=True)
    i1 = jnp.min(jnp.where(el == m1, lane, big), axis=-1, keepdims=True)
    el2 = jnp.where(lane == i1, neg, el)
    m2 = jnp.max(el2, axis=-1, keepdims=True)
    i2 = jnp.min(jnp.where(el2 == m2, lane, big), axis=-1, keepdims=True)
    e21 = jnp.exp(m2 - m1)
    w1 = 1.0 / (1.0 + e21)
    w2 = e21 / (1.0 + e21)
    oh1 = lane == i1
    oh2 = lane == i2
    onehot = jnp.where(oh1 | oh2, 1.0, 0.0)
    strict = jnp.where(_iota((tm, tm), 0) > _iota((tm, tm), 1), 1.0, 0.0).astype(BF16)
    before = _dot(strict, onehot.astype(BF16)) + cnt_ref[...]
    rank1 = jnp.sum(jnp.where(oh1, before, 0.0), axis=-1, keepdims=True)
    rank2 = jnp.sum(jnp.where(oh2, before, 0.0), axis=-1, keepdims=True)
    cnt = before[tm - 1:tm, :] + onehot[tm - 1:tm, :]
    cnt_ref[...] = cnt
    n_ref[...] = jnp.broadcast_to(cnt, n_ref.shape)
    cols = (i1, i2, p_grp * w1, p_grp * w2, rank1, rank2)
    route = jnp.zeros(logits.shape, F32)
    for j, col in enumerate(cols):
        route = jnp.where(lane_i == j, col, route)
    r_ref[...] = route
    rt_ref[...] = route.T[0:8, :]


def post_mix(x3, ya, yb, yc, yd, gain, w_out_bf16, g_mem, wq_bf16, kv3, wo_bf16, g_ffn, w_router, b_router):
    b, s, d = x3.shape
    t = ROW_TILE
    gw = GROUP_WIDTH
    nt = s // t
    mlen = kv3.shape[1]
    tile = lambda w: pl.BlockSpec((1, t, w), lambda i, j: (i, j, 0))
    const = lambda shape: pl.BlockSpec(shape, lambda i, j: (0,) * len(shape))
    tok = lambda w: pl.BlockSpec((t, w), lambda i, j: (i * nt + j, 0))
    return pl.pallas_call(
        _post_mix_kernel,
        grid=(b, nt),
        in_specs=[tile(d), tile(gw), tile(gw), tile(gw), tile(gw),
                  const((1, 4 * gw)), const((4 * gw, d)),
                  const((1, d)), const((d, gw)), pl.BlockSpec((1, mlen, 2 * gw), lambda i, j: (i, 0, 0)), const((gw, d)),
                  const((1, d)), const((d, SMALL_W)), const((1, SMALL_W))],
        out_specs=[tile(d), tok(d // 2), tok(SMALL_W), pl.BlockSpec((8, t), lambda i, j: (0, i * nt + j)),
                   const((8, SMALL_W))],
        out_shape=[jax.ShapeDtypeStruct((b, s, d), F32), jax.ShapeDtypeStruct((b * s, d // 2), jnp.uint32),
                   jax.ShapeDtypeStruct((b * s, SMALL_W), F32), jax.ShapeDtypeStruct((8, b * s), F32),
                   jax.ShapeDtypeStruct((8, SMALL_W), F32)],
        scratch_shapes=[pltpu.VMEM((1, SMALL_W), F32)],
        compiler_params=_cparams("arbitrary", "arbitrary"),
        name="post_mix",
    )(x3, ya, yb, yc, yd, gain.reshape(1, 4 * gw), w_out_bf16, g_mem.reshape(1, d), wq_bf16, kv3, wo_bf16,
      g_ffn.reshape(1, d), w_router, b_router)


def _experts_kernel(te_ref, nu_ref, run_ref, nxt_ref, xs_ref, wgu_hbm, wdn_hbm, y_ref,
                    wgu_buf, wdn_buf, wgu_bf, wdn_bf, sem, *, layer):
    i = pl.program_id(0)

    def weight_copies(expert, slot):
        return (pltpu.make_async_copy(wgu_hbm.at[layer, expert], wgu_buf.at[slot], sem.at[slot, 0]),
                pltpu.make_async_copy(wdn_hbm.at[layer, expert], wdn_buf.at[slot], sem.at[slot, 1]))

    @pl.when(i >= nu_ref[0])
    def _():
        y_ref[...] = jnp.zeros_like(y_ref)

    @pl.when(i < nu_ref[0])
    def _():
        expert = te_ref[i]
        slot = run_ref[i] % 2

        @pl.when(i == 0)
        def _():
            for c in weight_copies(expert, slot):
                c.start()

        @pl.when((i == 0) | (te_ref[jnp.maximum(i - 1, 0)] != expert))
        def _():
            for c in weight_copies(expert, slot):
                c.wait()
            nxt = nxt_ref[i]

            @pl.when(nxt >= 0)
            def _():
                for c in weight_copies(nxt, 1 - slot):
                    c.start()

            wgu_bf[...] = wgu_buf[slot].astype(BF16)
            wdn_bf[...] = wdn_buf[slot].astype(BF16)

        gu = _dot(_unpack_bf16_pairs(xs_ref[...]).astype(BF16), wgu_bf[...])
        act = _silu(gu[:, 0:D_EXPERT]) * gu[:, D_EXPERT:2 * D_EXPERT]
        y_ref[...] = _pack_bf16_pairs(_dot(act.astype(BF16), wdn_bf[...]))


def moe_experts(xs, tile_expert, n_used, run_index, next_expert, w_gu, w_dn, layer, tm):
    n_rows, dp = xs.shape
    d = 2 * dp
    return pl.pallas_call(
        functools.partial(_experts_kernel, layer=layer),
        grid_spec=pltpu.PrefetchScalarGridSpec(
            num_scalar_prefetch=4,
            grid=(n_rows // tm,),
            in_specs=[pl.BlockSpec((tm, dp), lambda i, te, nu, run, nxt: (jnp.minimum(i, nu[0] - 1), 0)),
                      pl.BlockSpec(memory_space=pl.ANY),
                      pl.BlockSpec(memory_space=pl.ANY)],
            out_specs=pl.BlockSpec((tm, dp), lambda i, te, nu, run, nxt: (i, 0)),
            scratch_shapes=[pltpu.VMEM((2, d, 2 * D_EXPERT), F32), pltpu.VMEM((2, D_EXPERT, d), F32),
                            pltpu.VMEM((d, 2 * D_EXPERT), BF16), pltpu.VMEM((D_EXPERT, d), BF16),
                            pltpu.SemaphoreType.DMA((2, 2))]),
        out_shape=jax.ShapeDtypeStruct((n_rows, dp), jnp.uint32),
        compiler_params=_cparams("arbitrary"),
        name="moe_experts",
    )(tile_expert, n_used, run_index, next_expert, xs, w_gu, w_dn)


def sc_gather_rows(table, idx):
    n_idx = idx.shape[0]
    width = table.shape[1]
    info = plsc.get_sparse_core_info()
    n_workers = info.num_cores * info.num_subcores
    per_worker = n_idx // n_workers
    chunk = SC_GATHER_CHUNK
    assert n_idx % (n_workers * chunk) == 0 and width % info.num_lanes == 0
    mesh = plsc.VectorSubcoreMesh(core_axis_name="c", subcore_axis_name="s")

    def body(table_hbm, idx_hbm, out_hbm, idx_v, rows_v, sem):
        wid = lax.axis_index("s") * info.num_cores + lax.axis_index("c")
        base = wid * per_worker

        @pl.loop(0, per_worker // chunk)
        def _(j):
            off = pl.multiple_of(base + j * chunk, chunk)
            pltpu.sync_copy(idx_hbm.at[pl.ds(off, chunk)], idx_v)
            pltpu.async_copy(table_hbm.at[idx_v], rows_v, sem).wait()
            pltpu.sync_copy(rows_v, out_hbm.at[pl.ds(off, chunk)])

    return pl.kernel(
        body,
        out_type=jax.ShapeDtypeStruct((n_idx, width), table.dtype),
        mesh=mesh,
        scratch_types=[pltpu.VMEM((chunk,), jnp.int32), pltpu.VMEM((chunk, width), table.dtype),
                       pltpu.SemaphoreType.DMA],
        name="sc_gather_rows",
    )(table, idx)


def sc_dispatch_rows(rows, dest, pad_idx, zero_rows):
    m, width = rows.shape
    n_pad = pad_idx.shape[0]
    info = plsc.get_sparse_core_info()
    n_workers = info.num_cores * info.num_subcores
    chunk = SC_GATHER_CHUNK
    assert m % (n_workers * chunk) == 0 and n_pad % (n_workers * chunk) == 0 and zero_rows.shape == (chunk, width)
    rows_per_worker = m // n_workers
    pad_per_worker = n_pad // n_workers
    mesh = plsc.VectorSubcoreMesh(core_axis_name="c", subcore_axis_name="s")

    def body(rows_hbm, dest_hbm, pad_hbm, zero_hbm, out_hbm, idx_v, rows_v, sem):
        wid = lax.axis_index("s") * info.num_cores + lax.axis_index("c")

        @pl.loop(0, rows_per_worker // chunk)
        def _(j):
            off = pl.multiple_of(wid * rows_per_worker + j * chunk, chunk)
            pltpu.sync_copy(rows_hbm.at[pl.ds(off, chunk)], rows_v)
            for slot in range(2):
                pltpu.sync_copy(dest_hbm.at[pl.ds(slot * m + off, chunk)], idx_v)
                pltpu.async_copy(rows_v, out_hbm.at[idx_v], sem).wait()

        pltpu.sync_copy(zero_hbm, rows_v)

        @pl.loop(0, pad_per_worker // chunk)
        def _(j):
            off = pl.multiple_of(wid * pad_per_worker + j * chunk, chunk)
            pltpu.sync_copy(pad_hbm.at[pl.ds(off, chunk)], idx_v)
            pltpu.async_copy(rows_v, out_hbm.at[idx_v], sem).wait()

    return pl.kernel(
        body,
        out_type=jax.ShapeDtypeStruct((2 * m + n_pad, width), rows.dtype),
        mesh=mesh,
        scratch_types=[pltpu.VMEM((chunk,), jnp.int32), pltpu.VMEM((chunk, width), rows.dtype),
                       pltpu.SemaphoreType.DMA],
        name="sc_dispatch_rows",
    )(rows, dest, pad_idx, zero_rows)


def _combine_rows_kernel(x_ref, r_ref, g_ref, y1_ref, y2_ref, o_ref, *, final_norm):
    r = r_ref[...]
    out = x_ref[...] + r[:, 2:3] * _unpack_bf16_pairs(y1_ref[...]) + r[:, 3:4] * _unpack_bf16_pairs(y2_ref[...])
    if final_norm:
        ms = jnp.mean(out * out, axis=-1, keepdims=True)
        out = (out * lax.rsqrt(ms + EPS)) * g_ref[...]
    o_ref[...] = out


def moe_combine_rows(x2, route, y_rows, g_final, tm, final_norm):
    m, d = x2.shape
    steps = m // tm
    return pl.pallas_call(
        functools.partial(_combine_rows_kernel, final_norm=final_norm),
        grid=(steps,),
        in_specs=[pl.BlockSpec((tm, d), lambda i: (i, 0)),
                  pl.BlockSpec((tm, SMALL_W), lambda i: (i, 0)),
                  pl.BlockSpec((1, d), lambda i: (0, 0)),
                  pl.BlockSpec((tm, d // 2), lambda i: (i, 0)),
                  pl.BlockSpec((tm, d // 2), lambda i: (i + steps, 0))],
        out_specs=pl.BlockSpec((tm, d), lambda i: (i, 0)),
        out_shape=jax.ShapeDtypeStruct((m, d), F32),
        compiler_params=_cparams("parallel"),
        name="moe_combine_rows",
    )(x2, route, g_final.reshape(1, d), y_rows, y_rows)


def moe_routed(x2, h, route, route_t, counts, w_gu, w_dn, layer, g_final, final_norm):
    m, d = x2.shape
    te_rows = EXPERT_TILE
    n_rows = 2 * m + N_EXPERTS * te_rows
    cnt = counts[0, 0:N_EXPERTS].astype(jnp.int32)
    padded = (cnt + te_rows - 1) // te_rows * te_rows
    seg_end = jnp.cumsum(padded)
    seg_start = seg_end - padded
    idx = route_t[0:2].astype(jnp.int32)
    onehot = idx[:, :, None] == jnp.arange(N_EXPERTS, dtype=jnp.int32)
    dest = (jnp.sum(jnp.where(onehot, seg_start, 0), axis=-1) + route_t[4:6].astype(jnp.int32)).reshape(2 * m)
    tile_start = jnp.arange(n_rows // te_rows, dtype=jnp.int32) * te_rows
    tile_expert = jnp.minimum(jnp.sum((tile_start[:, None] >= seg_end[None, :]).astype(jnp.int32), axis=1), N_EXPERTS - 1)
    n_used = (seg_end[N_EXPERTS - 1] // te_rows).reshape(1)
    n_pad = n_rows - 2 * m
    experts = jnp.arange(N_EXPERTS, dtype=jnp.int32)
    pad_end = jnp.cumsum(padded - cnt)
    k = jnp.arange(n_pad, dtype=jnp.int32)
    k_expert = jnp.sum((k[:, None] >= pad_end[None, :]).astype(jnp.int32), axis=1)
    first_pad = seg_start + cnt - (pad_end - (padded - cnt))
    in_segment = jnp.sum(jnp.where(k_expert[:, None] == experts[None, :], first_pad[None, :], 0), axis=1) + k
    pad_idx = jnp.where(k_expert < N_EXPERTS, in_segment, seg_end[N_EXPERTS - 1] + k - pad_end[N_EXPERTS - 1])
    xs = sc_dispatch_rows(h, dest, pad_idx, jnp.zeros((SC_GATHER_CHUNK, d // 2), h.dtype))
    n_tiles = n_rows // te_rows
    tile_ids = jnp.arange(n_tiles, dtype=jnp.int32)
    prev_expert = jnp.concatenate([jnp.full((1,), -1, jnp.int32), tile_expert[:-1]])
    run_index = jnp.maximum(jnp.cumsum(((tile_expert != prev_expert) & (tile_ids < n_used[0])).astype(jnp.int32)) - 1, 0)
    next_tile = jnp.take(seg_end, tile_expert) // te_rows
    next_expert = jnp.where(next_tile < n_used[0], jnp.take(tile_expert, jnp.minimum(next_tile, n_tiles - 1)), -1)
    ys = moe_experts(xs, tile_expert, n_used, run_index, next_expert, w_gu, w_dn, layer, tm=te_rows)
    y_rows = sc_gather_rows(ys, dest)
    return moe_combine_rows(x2, route, y_rows, g_final, tm=ROW_TILE, final_norm=final_norm)


def _w_in_prep_kernel(w_ref, o_ref):
    gw = GROUP_WIDTH
    a_end = 7 * gw
    c_start = a_end + HEADS
    c_end = c_start + 5 * gw
    z_start = c_end + 2 * HEADS
    assert a_end % SMALL_W == LANE_FOX and c_end % SMALL_W == LANE_BETA and LANE_GDEC == LANE_BETA + HEADS
    w = w_ref[...]
    o_ref[:, 0:a_end] = w[:, 0:a_end].astype(BF16)
    o_ref[:, a_end:a_end + 5 * gw] = w[:, c_start:c_end].astype(BF16)
    o_ref[:, 12 * gw:13 * gw] = w[:, z_start:z_start + gw].astype(BF16)
    blk_fox = w[:, a_end:a_end + SMALL_W]
    blk_gdn = w[:, c_end - LANE_BETA:c_end - LANE_BETA + SMALL_W]
    lane = _iota(blk_fox.shape, 1)
    small = jnp.where(lane < LANE_BETA, blk_fox, jnp.where(lane < LANE_GDEC + HEADS, blk_gdn, 0.0))
    o_ref[:, 13 * gw:13 * gw + SMALL_W] = small.astype(BF16)
    o_ref[:, 13 * gw + SMALL_W:] = jnp.zeros((w.shape[0], SMALL_W), BF16)


def w_in_prep(w_all, layer):
    _, d, n = w_all.shape
    tr = 128
    return pl.pallas_call(
        _w_in_prep_kernel,
        grid=(d // tr,),
        in_specs=[pl.BlockSpec((None, tr, n), lambda i: (layer, i, 0))],
        out_specs=pl.BlockSpec((tr, N_PROJ), lambda i: (i, 0)),
        out_shape=jax.ShapeDtypeStruct((d, N_PROJ), BF16),
        compiler_params=_cparams("parallel"),
        name="w_in_prep",
    )(w_all)


def _block_diag(w):
    h = w.shape[0]
    eye = jnp.eye(h, dtype=w.dtype)
    return (eye[:, None, :, None] * w[:, :, None, :]).reshape(h * HEAD_DIM, h * HEAD_DIM)


def kernel(x, mem, norm_mix, w_in, hgrn_lb, fox_fb, lru_conv_w, lru_conv_b, lru_wa, lru_ba, lru_wx, lru_bx, lru_lam, gdn_conv_w, gdn_a_log, gdn_dt_bias, mix_gain, w_out, norm_mem, norm_memkv, w_mq, w_mkv, w_mo, norm_ffn, w_rg, b_rg, w_re, b_re, w_e_gu, w_e_dn, norm_final):
    b, s, d = x.shape
    depth = w_in.shape[0]
    t_tok = b * s
    mlen = mem.shape[1]
    gw = GROUP_WIDTH

    lb_all = jnp.cumsum(jax.nn.softmax(hgrn_lb.astype(F32), axis=0), axis=0)
    lb_all = lb_all - lb_all[0]

    x = x.astype(F32)
    for l in range(depth):
        prm = jnp.zeros((8, SMALL_W), F32)
        prm = prm.at[0, LANE_FOX:LANE_FOX + HEADS].set(fox_fb[l].astype(F32))
        prm = prm.at[0, LANE_GDEC:LANE_GDEC + HEADS].set(gdn_dt_bias[l].astype(F32))
        prm = prm.at[1, LANE_GDEC:LANE_GDEC + HEADS].set(gdn_a_log[l].astype(F32))
        proj, sm, c_exp = in_proj(x.reshape(t_tok, d), norm_mix[l], w_in_prep(w_in, l), prm, s, tm=ROW_TILE)
        proj3 = proj.reshape(b, s, N_PROJ)
        sm = sm.reshape(b, s, SMALL_W)
        c_exp = c_exp.reshape(b, s, HEADS * SMALL_W)

        ya = hgrn2(proj3, lb_all[l])
        yb = fox_attention(proj3, c_exp)
        yc = rglru(proj3, lru_conv_w[l], lru_conv_b[l], _block_diag(lru_wa[l]).astype(BF16), lru_ba[l],
                   _block_diag(lru_wx[l]).astype(BF16), lru_bx[l], lru_lam[l])
        yd = gdn(proj3, sm, gdn_conv_w[l])

        kv = norm_matmul(mem.reshape(b * mlen, d), norm_memkv[l], w_mkv[l].astype(BF16), tm=256)
        w_router = jnp.concatenate([w_re[l], w_rg[l], jnp.zeros((d, SMALL_W - N_EXPERTS - N_EXPERT_GROUPS), F32)], axis=1)
        b_router = jnp.concatenate([b_re[l], b_rg[l], jnp.zeros((SMALL_W - N_EXPERTS - N_EXPERT_GROUPS,), F32)]).reshape(1, SMALL_W)
        x, h, route, route_t, counts = post_mix(x, ya, yb, yc, yd, mix_gain[l], w_out[l].astype(BF16), norm_mem[l],
                                                w_mq[l].astype(BF16), kv.reshape(b, mlen, 2 * gw), w_mo[l].astype(BF16),
                                                norm_ffn[l], w_router, b_router)

        x = moe_routed(x.reshape(t_tok, d), h, route, route_t, counts, w_e_gu, w_e_dn, l,
                       norm_final, final_norm=(l == depth - 1)).reshape(b, s, d)

    return x
```

```python
import functools
import math

import jax
import jax.numpy as jnp
from jax import lax
from jax.experimental import pallas as pl
from jax.experimental.pallas import tpu as pltpu
from jax.experimental.pallas import tpu_sc as plsc

F32 = jnp.float32
BF16 = jnp.bfloat16

HEAD_DIM = 64
GROUP_WIDTH = 256
HEADS = GROUP_WIDTH // HEAD_DIM
GDN_CHUNK = 64
GDN_TILE = 128
GDN_BATCH = 4
FOX_BATCH = 2
HGRN_BATCH = 4
HGRN_CHUNK = 16
CONV_W = 4
LRU_C = 8.0
EPS = 1e-6
N_EXPERT_GROUPS = 4
EXPERTS_PER_GROUP = 8
N_EXPERTS = N_EXPERT_GROUPS * EXPERTS_PER_GROUP
D_EXPERT = 256
SC_GATHER_CHUNK = 128
EXPERT_TILE = 512
SMALL_W = 128
SEQ_TILE = 256
ROW_TILE = 512
VMEM_LIMIT = 56 * 1024 * 1024

(COL_AQ, COL_AF, COL_AI, COL_AG, COL_BQ, COL_BK, COL_BV, COL_CX, COL_CG,
 COL_DQ, COL_DK, COL_DV, COL_DZ) = range(13)
N_PROJ = 13 * GROUP_WIDTH + 2 * SMALL_W
COL_SMALL = 13 * GROUP_WIDTH // SMALL_W
LANE_FOX = 0
LANE_BETA = 4
LANE_GDEC = 8


def _cparams(*sem):
    return pltpu.CompilerParams(dimension_semantics=sem, vmem_limit_bytes=VMEM_LIMIT)


def _dot(a, b):
    return jnp.dot(a, b, preferred_element_type=F32)


def _dot_nt(a, b):
    return lax.dot_general(a, b, (((1,), (1,)), ((), ())), preferred_element_type=F32)


def _dot_tn(a, b):
    return lax.dot_general(a, b, (((0,), (0,)), ((), ())), preferred_element_type=F32)


def _split3(x):
    h = x.astype(BF16)
    r = x - h.astype(F32)
    m = r.astype(BF16)
    l = (r - m.astype(F32)).astype(BF16)
    return h, m, l


def _dot_exact_rhs(x, w_bf16):
    h, m, l = _split3(x)
    return _dot(h, w_bf16) + _dot(m, w_bf16) + _dot(l, w_bf16)


def _dot_exact_lhs(w_bf16, x):
    h, m, l = _split3(x)
    return _dot(w_bf16, h) + _dot(w_bf16, m) + _dot(w_bf16, l)


def _iota(shape, dim):
    return lax.broadcasted_iota(jnp.int32, shape, dim)


def _head_ones(n=GROUP_WIDTH):
    r = _iota((n, n), 0) // HEAD_DIM
    c = _iota((n, n), 1) // HEAD_DIM
    return r == c


def _sigmoid(x):
    return 1.0 / (1.0 + jnp.exp(-x))


def _silu(x):
    return x * _sigmoid(x)


def _log_sigmoid(x):
    return jnp.minimum(x, 0.0) - jnp.log1p(jnp.exp(-jnp.abs(x)))


def _softplus(x):
    return jnp.maximum(x, 0.0) + jnp.log1p(jnp.exp(-jnp.abs(x)))


def _gelu_tanh(x):
    return 0.5 * x * (1.0 + jnp.tanh(math.sqrt(2.0 / math.pi) * (x + 0.044715 * (x * x * x))))


def _pack_bf16_pairs(x):
    n = x.shape[1] // 2
    u = lax.bitcast_convert_type(x.astype(BF16).astype(F32), jnp.uint32)
    return u[:, :n] | (u[:, n:] >> 16)


def _unpack_bf16_pairs(p):
    hi = lax.bitcast_convert_type(p & jnp.uint32(0xFFFF0000), F32)
    lo = lax.bitcast_convert_type(p << 16, F32)
    return jnp.concatenate([hi, lo], axis=1)


def _head_mean_sq(x, ones_bf16):
    return _dot_exact_rhs(x * x, ones_bf16) * (1.0 / HEAD_DIM)


def _head_rms(x, ones_bf16):
    return x * lax.rsqrt(_head_mean_sq(x, ones_bf16) + EPS)


def _stack_heads(x):
    lane_head = _iota(x.shape, 1) // HEAD_DIM
    parts = []
    for h in range(HEADS):
        parts.append(jnp.where(lane_head == h, x, 0.0))
    return jnp.concatenate(parts, axis=0)


def _causal_conv4(x, pad_ref, w):
    r = x.shape[0]
    pad_ref[8:8 + r, :] = x
    acc = x * w[CONV_W - 1:CONV_W, :]
    for k in range(1, CONV_W):
        acc = acc + pad_ref[8 - k:8 - k + r, :] * w[CONV_W - 1 - k:CONV_W - k, :]
    pad_ref[0:8, :] = x[r - 8:r, :]
    return acc


def _norm_matmul_kernel(x_ref, g_ref, w_ref, o_ref):
    x = x_ref[...]
    ms = jnp.mean(x * x, axis=-1, keepdims=True)
    h = (x * lax.rsqrt(ms + EPS)) * g_ref[...]
    o_ref[...] = _dot(h.astype(BF16), w_ref[...]).astype(o_ref.dtype)


def norm_matmul(x, g, w_bf16, tm, out_dtype=F32):
    m, d = x.shape
    n = w_bf16.shape[1]
    return pl.pallas_call(
        _norm_matmul_kernel,
        grid=(m // tm,),
        in_specs=[pl.BlockSpec((tm, d), lambda i: (i, 0)),
                  pl.BlockSpec((1, d), lambda i: (0, 0)),
                  pl.BlockSpec((d, n), lambda i: (0, 0))],
        out_specs=pl.BlockSpec((tm, n), lambda i: (i, 0)),
        out_shape=jax.ShapeDtypeStruct((m, n), out_dtype),
        compiler_params=_cparams("parallel"),
        name="norm_matmul",
    )(x, g.reshape(1, d), w_bf16)


def _small_prep_kernel(s_ref, p_ref, o_ref, x_ref):
    S = s_ref.shape[1]
    W = s_ref.shape[2]
    blk = SEQ_TILE
    lane = _iota((blk, W), 1)
    is_fox = (lane >= LANE_FOX) & (lane < LANE_FOX + HEADS)
    is_beta = (lane >= LANE_BETA) & (lane < LANE_BETA + HEADS)
    is_gdec = (lane >= LANE_GDEC) & (lane < LANE_GDEC + HEADS)
    r = _iota((blk, blk), 0)
    c = _iota((blk, blk), 1)
    tril_all = jnp.where(r >= c, 1.0, 0.0).astype(BF16)
    tril_chunk = jnp.where((r >= c) & (r // GDN_CHUNK == c // GDN_CHUNK), 1.0, 0.0).astype(BF16)
    neg_exp_a = -jnp.exp(p_ref[1:2, :])
    carry = jnp.zeros((1, W), F32)
    for i in range(S // blk):
        sl = pl.ds(i * blk, blk)
        z = s_ref[0, sl, :] + p_ref[0:1, :]
        fox = jnp.where(is_fox, _log_sigmoid(z), 0.0)
        beta = jnp.where(is_beta, _sigmoid(z), 0.0)
        gdec = jnp.where(is_gdec, neg_exp_a * _softplus(z), 0.0)
        cf = _dot_exact_lhs(tril_all, fox) + carry
        cg = _dot_exact_lhs(tril_chunk, gdec)
        carry = cf[blk - 1:blk, :]
        o_ref[0, sl, :] = cf + cg + beta
        both = cf + cg
        for j, src_lane in enumerate(range(LANE_FOX, LANE_FOX + HEADS)):
            x_ref[0, sl, j * W:(j + 1) * W] = jnp.broadcast_to(both[:, src_lane:src_lane + 1], (blk, W))


def small_prep(proj3, params):
    b, s, _ = proj3.shape
    w = SMALL_W
    return pl.pallas_call(
        _small_prep_kernel,
        grid=(b,),
        in_specs=[pl.BlockSpec((1, s, w), lambda i: (i, 0, COL_SMALL)),
                  pl.BlockSpec((8, w), lambda i: (0, 0))],
        out_specs=[pl.BlockSpec((1, s, w), lambda i: (i, 0, 0)),
                   pl.BlockSpec((1, s, HEADS * w), lambda i: (i, 0, 0))],
        out_shape=[jax.ShapeDtypeStruct((b, s, w), F32), jax.ShapeDtypeStruct((b, s, HEADS * w), F32)],
        compiler_params=_cparams("parallel"),
        name="small_prep",
    )(proj3, params)


def _hgrn2_kernel(q_ref, f_ref, i_ref, g_ref, lb_ref, o_ref, st_ref, qs, ks, vs, bs, os_):
    nb = q_ref.shape[0]
    T = q_ref.shape[1]
    C = HGRN_CHUNK
    seqs = range(nb)

    @pl.when(pl.program_id(1) == 0)
    def _():
        st_ref[...] = jnp.zeros_like(st_ref)

    same_head = _head_ones()
    ones_bf = jnp.where(same_head, 1.0, 0.0).astype(BF16)
    lb = lb_ref[...]
    log_lb = jnp.log(lb)
    r = _iota((T, T), 0)
    c = _iota((T, T), 1)
    tril_chunk = jnp.where((r >= c) & (r // C == c // C), 1.0, 0.0).astype(BF16)
    for bb in seqs:
        fl = f_ref[bb]
        c2 = jnp.log1p(-lb) + _log_sigmoid(fl)
        mx = jnp.maximum(log_lb, c2)
        log_f = mx + jnp.log1p(jnp.exp(-jnp.abs(log_lb - c2)))
        bs[bb] = _dot_exact_lhs(tril_chunk, log_f)
        qs[bb] = _silu(q_ref[bb])
        ks[bb] = (1.0 - lb) * _sigmoid(-fl)
        vs[bb] = i_ref[bb]

    H8 = C // 2
    trow8 = _iota((H8, GROUP_WIDTH), 0)

    def chunk(ci, carry):
        r0 = pl.multiple_of(ci * C, C)
        qc = [qs[bb, pl.ds(r0, C), :] for bb in seqs]
        kc = [ks[bb, pl.ds(r0, C), :] for bb in seqs]
        vc = [vs[bb, pl.ds(r0, C), :] for bb in seqs]
        bc = [bs[bb, pl.ds(r0, C), :] for bb in seqs]
        st = [st_ref[bb] for bb in seqs]
        a = []
        for bb in seqs:
            parts = []
            for half, s_range in ((0, range(H8)), (1, range(C))):
                b_half = bc[bb][half * H8:(half + 1) * H8, :]
                q_half = qc[bb][half * H8:(half + 1) * H8, :]
                for s in s_range:
                    blk = jnp.exp(b_half - bc[bb][s:s + 1, :]) * (q_half * kc[bb][s:s + 1, :])
                    if s >= half * H8:
                        blk = jnp.where(trow8 >= s - half * H8, blk, 0.0)
                    parts.append(blk)
            a.append(jnp.concatenate(parts, axis=0).astype(BF16))
        sc = [_dot(a[bb], ones_bf) for bb in seqs]
        o = [_dot_nt((qc[bb] * jnp.exp(bc[bb])).astype(BF16), st[bb].astype(BF16)) for bb in seqs]
        b_last = [bc[bb][C - 1:C, :] for bb in seqs]
        upd = [_dot_tn(vc[bb].astype(BF16), (kc[bb] * jnp.exp(b_last[bb] - bc[bb])).astype(BF16)) for bb in seqs]
        for bb in seqs:
            top = o[bb][0:H8, :]
            bot = o[bb][H8:C, :]
            for s in range(H8):
                top = top + sc[bb][s * H8:(s + 1) * H8, :] * vc[bb][s:s + 1, :]
            for s in range(C):
                bot = bot + sc[bb][(H8 + s) * H8:(H8 + s + 1) * H8, :] * vc[bb][s:s + 1, :]
            os_[bb, pl.ds(r0, C), :] = jnp.concatenate([top, bot], axis=0)
            st_ref[bb] = st[bb] * jnp.exp(b_last[bb]) + jnp.where(same_head, upd[bb], 0.0)
        return carry

    lax.fori_loop(0, T // C, chunk, 0)
    for bb in seqs:
        o_ref[bb] = _head_rms(os_[bb], ones_bf) * _silu(g_ref[bb])


def hgrn2(proj3, lb):
    b, s, _ = proj3.shape
    t = SEQ_TILE
    gw = GROUP_WIDTH
    nb = HGRN_BATCH
    col = lambda cidx: pl.BlockSpec((nb, t, gw), lambda i, j: (i, j, cidx))
    return pl.pallas_call(
        _hgrn2_kernel,
        grid=(b // nb, s // t),
        in_specs=[col(COL_AQ), col(COL_AF), col(COL_AI), col(COL_AG),
                  pl.BlockSpec((1, gw), lambda i, j: (0, 0))],
        out_specs=pl.BlockSpec((nb, t, gw), lambda i, j: (i, j, 0)),
        out_shape=jax.ShapeDtypeStruct((b, s, gw), F32),
        scratch_shapes=[pltpu.VMEM((nb, gw, gw), F32)] + [pltpu.VMEM((nb, t, gw), F32)] * 5,
        compiler_params=_cparams("parallel", "arbitrary"),
        name="hgrn2",
    )(proj3, proj3, proj3, proj3, lb.reshape(1, gw))


def _fox_kernel(q_ref, k_ref, v_ref, cx_ref, o_ref, kb_ref, vt_ref, acc_ref):
    nb = q_ref.shape[0]
    tq = q_ref.shape[1]
    tk = tq
    S = k_ref.shape[1]
    qi = pl.program_id(1)
    seqs = range(nb)

    @pl.when(qi == 0)
    def _():
        for bb in seqs:
            for i in range(S // tk):
                sl = pl.ds(i * tk, tk)
                kb_ref[bb, sl, :] = k_ref[bb, sl, :].astype(BF16)
                vt_ref[bb, :, sl] = v_ref[bb, sl, :].T.astype(BF16)

    lane_head = _iota((tq, GROUP_WIDTH), 1) // HEAD_DIM
    qh = []
    for bb in seqs:
        qsc = q_ref[bb] * (HEAD_DIM ** -0.5)
        qh.append([jnp.where(lane_head == h, qsc, 0.0).astype(BF16) for h in range(HEADS)])
    acc_ref[...] = jnp.zeros_like(acc_ref)
    reps = tq // SMALL_W
    causal = _iota((tk, tq), 0) <= _iota((tk, tq), 1)

    def step(kb, stats, masked):
        k0 = pl.multiple_of(kb * tk, tk)
        new_stats = [[None] * HEADS for _ in seqs]
        strips = [(h, bb) for h in range(HEADS) for bb in seqs]

        def scores(h, bb):
            return _dot_nt(kb_ref[bb, pl.ds(k0, tk), :], qh[bb][h])

        ahead = 4
        queue = [scores(*strips[j]) for j in range(ahead)]
        for i, (h, bb) in enumerate(strips):
            st = queue.pop(0)
            if i + ahead < len(strips):
                queue.append(scores(*strips[i + ahead]))
            rows = slice(h * HEAD_DIM, (h + 1) * HEAD_DIM)
            cx = cx_ref[bb, pl.ds(k0, tk), h * SMALL_W:(h + 1) * SMALL_W]
            st = st - jnp.concatenate([cx] * reps, axis=1)
            if masked:
                st = jnp.where(causal, st, -jnp.inf)
            m_old, l_old = stats[bb][h]
            m_new = jnp.maximum(m_old, jnp.max(st, axis=0, keepdims=True))
            alpha = jnp.exp(m_old - m_new)
            p = jnp.exp(st - m_new)
            l_new = alpha * l_old + jnp.sum(p, axis=0, keepdims=True)
            pv = _dot(vt_ref[bb, rows, pl.ds(k0, tk)], p.astype(BF16))
            acc_ref[bb, rows, :] = alpha * acc_ref[bb, rows, :] + pv
            new_stats[bb][h] = (m_new, l_new)
        return tuple(tuple(new_stats[bb]) for bb in seqs)

    init = tuple(tuple((jnp.full((1, tq), -jnp.inf, F32), jnp.zeros((1, tq), F32)) for _ in range(HEADS)) for _ in seqs)
    stats = lax.fori_loop(0, qi, lambda kb, c: step(kb, c, False), init)
    stats = step(qi, stats, True)

    ones_bf = jnp.where(_head_ones(), 1.0, 0.0).astype(BF16)
    for bb in seqs:
        inv_l = jnp.concatenate([jnp.broadcast_to(1.0 / stats[bb][h][1], (HEAD_DIM, tq)) for h in range(HEADS)], axis=0)
        o_ref[bb] = _head_rms((acc_ref[bb] * inv_l).T, ones_bf)


def fox_attention(proj3, c_exp):
    b, s, _ = proj3.shape
    tq = SEQ_TILE
    gw = GROUP_WIDTH
    nb = FOX_BATCH
    return pl.pallas_call(
        _fox_kernel,
        grid=(b // nb, s // tq),
        in_specs=[pl.BlockSpec((nb, tq, gw), lambda i, j: (i, j, COL_BQ)),
                  pl.BlockSpec((nb, s, gw), lambda i, j: (i, 0, COL_BK)),
                  pl.BlockSpec((nb, s, gw), lambda i, j: (i, 0, COL_BV)),
                  pl.BlockSpec((nb, s, HEADS * SMALL_W), lambda i, j: (i, 0, 0))],
        out_specs=pl.BlockSpec((nb, tq, gw), lambda i, j: (i, j, 0)),
        out_shape=jax.ShapeDtypeStruct((b, s, gw), F32),
        scratch_shapes=[pltpu.VMEM((nb, s, gw), BF16),
                        pltpu.VMEM((nb, gw, s), BF16),
                        pltpu.VMEM((nb, gw, tq), F32)],
        compiler_params=_cparams("parallel", "arbitrary"),
        name="fox_attention",
    )(proj3, proj3, proj3, c_exp)


def _rglru_kernel(x_ref, g_ref, cw_ref, cb_ref, wa_ref, ba_ref, wx_ref, bx_ref, lam_ref, o_ref, prev_ref, h_ref):
    @pl.when(pl.program_id(1) == 0)
    def _():
        prev_ref[0:8, :] = jnp.zeros((8, prev_ref.shape[1]), F32)
        h_ref[...] = jnp.zeros_like(h_ref)

    x = x_ref[0]
    t = x.shape[0]
    xc = _causal_conv4(x, prev_ref, cw_ref[...]) + cb_ref[...]
    xb = xc.astype(BF16)
    r = _sigmoid(_dot(xb, wa_ref[...]) + ba_ref[...])
    ig = _sigmoid(_dot(xb, wx_ref[...]) + bx_ref[...])
    log_a = (-LRU_C * r) * _softplus(-lam_ref[...])
    a = jnp.exp(log_a)
    u = jnp.sqrt(1.0 - jnp.exp(2.0 * log_a)) * (ig * xc)
    row = _iota(a.shape, 0)
    d = 1
    while d < t:
        valid = row >= d
        u = jnp.where(valid, a * pltpu.roll(u, d, 0) + u, u)
        a = jnp.where(valid, a * pltpu.roll(a, d, 0), a)
        d *= 2
    h = a * h_ref[...] + u
    h_ref[...] = h[t - 1:t, :]
    ones_bf = jnp.where(_head_ones(), 1.0, 0.0).astype(BF16)
    o_ref[0] = _head_rms(h * _gelu_tanh(g_ref[0]), ones_bf)


def rglru(proj3, conv_w, conv_b, wa_bd, ba, wx_bd, bx, lam):
    b, s, _ = proj3.shape
    t = ROW_TILE
    gw = GROUP_WIDTH
    row = lambda: pl.BlockSpec((1, gw), lambda i, j: (0, 0))
    return pl.pallas_call(
        _rglru_kernel,
        grid=(b, s // t),
        in_specs=[pl.BlockSpec((1, t, gw), lambda i, j: (i, j, COL_CX)),
                  pl.BlockSpec((1, t, gw), lambda i, j: (i, j, COL_CG)),
                  pl.BlockSpec((CONV_W, gw), lambda i, j: (0, 0)), row(),
                  pl.BlockSpec((gw, gw), lambda i, j: (0, 0)), row(),
                  pl.BlockSpec((gw, gw), lambda i, j: (0, 0)), row(), row()],
        out_specs=pl.BlockSpec((1, t, gw), lambda i, j: (i, j, 0)),
        out_shape=jax.ShapeDtypeStruct((b, s, gw), F32),
        scratch_shapes=[pltpu.VMEM((8 + t, gw), F32), pltpu.VMEM((1, gw), F32)],
        compiler_params=_cparams("parallel", "arbitrary"),
        name="rglru",
    )(proj3, proj3, conv_w, conv_b.reshape(1, gw), wa_bd, ba.reshape(1, gw), wx_bd, bx.reshape(1, gw), lam.reshape(1, gw))


def _gdn_kernel(q_ref, k_ref, v_ref, z_ref, sm_ref, cw_ref, o_ref, st_ref, pq_ref, pk_ref, pv_ref):
    nb = q_ref.shape[0]
    T = q_ref.shape[1]
    C = GDN_CHUNK

    @pl.when(pl.program_id(1) == 0)
    def _():
        st_ref[...] = jnp.zeros_like(st_ref)
        for pad_ref in (pq_ref, pk_ref, pv_ref):
            pad_ref[:, 0:8, :] = jnp.zeros((nb, 8, pad_ref.shape[2]), F32)

    same_head = _head_ones()
    ones_bf = jnp.where(same_head, 1.0, 0.0).astype(BF16)
    cw = cw_ref[...]
    gw = GROUP_WIDTH
    er = _iota((SMALL_W, gw), 0)
    ec = _iota((SMALL_W, gw), 1) // HEAD_DIM
    exp_beta = jnp.where(er == LANE_BETA + ec, 1.0, 0.0).astype(BF16)
    exp_g = jnp.where(er == LANE_GDEC + ec, 1.0, 0.0).astype(BF16)

    prep = []
    for bb in range(nb):
        xq, xk, xv = q_ref[bb], k_ref[bb], v_ref[bb]
        q = _silu(_causal_conv4(xq, pq_ref.at[bb], cw[:, 0:gw]))
        k = _silu(_causal_conv4(xk, pk_ref.at[bb], cw[:, gw:2 * gw]))
        v = _silu(_causal_conv4(xv, pv_ref.at[bb], cw[:, 2 * gw:3 * gw]))
        q = q * lax.rsqrt(_dot_exact_rhs(q * q, ones_bf) + EPS) * (HEAD_DIM ** -0.5)
        k = k * lax.rsqrt(_dot_exact_rhs(k * k, ones_bf) + EPS)
        sm = sm_ref[bb]
        betax = _dot_exact_rhs(sm, exp_beta)
        gx = _dot_exact_rhs(sm, exp_g)
        prep.append((q, k, v, betax, gx))

    n_chunks = T // C
    trow = _iota((C, gw), 0)
    scol = _iota((C, gw), 1) % HEAD_DIM
    incl = scol <= trow
    strict = scol < trow
    eye = jnp.where(scol == trow, 1.0, 0.0)

    def bdiag(x_cat):
        return _stack_heads(x_cat).astype(BF16)

    chains = []
    for ci in range(n_chunks):
        sl = slice(ci * C, (ci + 1) * C)
        for bb in range(nb):
            q, k, v, betax, gx = prep[bb]
            qc, kc, vc, bx, gc = q[sl], k[sl], v[sl], betax[sl], gx[sl]
            grow = jnp.sum(jnp.where(scol == trow, gc, 0.0), axis=0, keepdims=True)
            eg = jnp.exp(gc)
            kb = kc * bx
            g_last = gc[C - 1:C, :]
            gamma = jnp.where(incl, jnp.exp(jnp.where(incl, gc - grow, 0.0)), 0.0)
            ks_bf = bdiag(kc)
            m = jnp.where(strict, _dot_nt(kb.astype(BF16), ks_bf) * gamma, 0.0)
            chains.append(dict(
                x=eye - m, y=m,
                a_qk=(_dot_nt(qc.astype(BF16), ks_bf) * gamma).astype(BF16),
                rhs_u=bdiag(vc * bx),
                rhs_w=bdiag(kb * eg),
                q_dec=(qc * eg).astype(BF16),
                k_dec=(kc * jnp.exp(g_last - gc)).astype(BF16),
                decay=jnp.exp(g_last)))
    for ch in chains:
        ch["ybd"] = bdiag(ch["y"])
    for _ in range(5):
        for ch in chains:
            ch["y"] = _dot(ch["y"].astype(BF16), ch["ybd"])
            ch["ybd"] = bdiag(ch["y"])
        for ch in chains:
            ch["x"] = ch["x"] + _dot(ch["x"].astype(BF16), ch["ybd"])
    for ch in chains:
        t_inv = ch["x"].astype(BF16)
        ch["u"] = _dot(t_inv, ch["rhs_u"])
        ch["wk"] = _dot(t_inv, ch["rhs_w"]).astype(BF16)

    outs = [[] for _ in range(nb)]
    seqs = range(nb)
    for ci in range(n_chunks):
        chs = [chains[ci * nb + bb] for bb in seqs]
        st = [st_ref[bb] for bb in seqs]
        st_bf = [st[bb].astype(BF16) for bb in seqs]
        v_new = [chs[bb]["u"] - _dot(chs[bb]["wk"], st_bf[bb]) for bb in seqs]
        o_st = [_dot(chs[bb]["q_dec"], st_bf[bb]) for bb in seqs]
        upd = [_dot_tn(chs[bb]["k_dec"], v_new[bb].astype(BF16)) for bb in seqs]
        for bb in seqs:
            outs[bb].append(o_st[bb] + _dot(chs[bb]["a_qk"], bdiag(v_new[bb])))
            st_ref[bb] = st[bb] * chs[bb]["decay"] + jnp.where(same_head, upd[bb], 0.0)
    for bb in range(nb):
        o = jnp.concatenate(outs[bb], axis=0)
        o_ref[bb] = _head_rms(o, ones_bf) * _silu(z_ref[bb])


def gdn(proj3, sm, conv_w):
    b, s, _ = proj3.shape
    t = GDN_TILE
    gw = GROUP_WIDTH
    nb = GDN_BATCH
    col = lambda cidx: pl.BlockSpec((nb, t, gw), lambda i, j: (i, j, cidx))
    return pl.pallas_call(
        _gdn_kernel,
        grid=(b // nb, s // t),
        in_specs=[col(COL_DQ), col(COL_DK), col(COL_DV), col(COL_DZ),
                  pl.BlockSpec((nb, t, SMALL_W), lambda i, j: (i, j, 0)),
                  pl.BlockSpec((CONV_W, 3 * gw), lambda i, j: (0, 0))],
        out_specs=pl.BlockSpec((nb, t, gw), lambda i, j: (i, j, 0)),
        out_shape=jax.ShapeDtypeStruct((b, s, gw), F32),
        scratch_shapes=[pltpu.VMEM((nb, gw, gw), F32)] + [pltpu.VMEM((nb, 8 + t, gw), F32)] * 3,
        compiler_params=_cparams("parallel", "arbitrary"),
        name="gdn",
    )(proj3, proj3, proj3, proj3, sm, conv_w)


def _mix_out(x, ys, gain, w_ref):
    gw = GROUP_WIDTH
    acc = x
    for i, y in enumerate(ys):
        yg = (y * gain[:, i * gw:(i + 1) * gw]).astype(BF16)
        acc = acc + _dot(yg, w_ref[i * gw:(i + 1) * gw, :])
    return acc


def _mem_attention(x, g, wq_ref, kv, wo_ref):
    t = x.shape[0]
    gw = GROUP_WIDTH
    ms = jnp.mean(x * x, axis=-1, keepdims=True)
    h = ((x * lax.rsqrt(ms + EPS)) * g).astype(BF16)
    q = _dot(h, wq_ref[...])
    k = kv[:, 0:gw].astype(BF16)
    v = kv[:, gw:2 * gw].astype(BF16)
    s = _dot_nt(k, _stack_heads(q).astype(BF16)) * (HEAD_DIM ** -0.5)
    s = s - jnp.max(s, axis=0, keepdims=True)
    p = jnp.exp(s)
    p = p * (1.0 / jnp.sum(p, axis=0, keepdims=True))
    ot = _dot_tn(v, p.astype(BF16))
    oc = jnp.concatenate([ot[hh * HEAD_DIM:(hh + 1) * HEAD_DIM, hh * t:(hh + 1) * t] for hh in range(HEADS)], axis=0).T
    return x + _dot(oc.astype(BF16), wo_ref[...])


def _post_mix_kernel(x_ref, ya_ref, yb_ref, yc_ref, yd_ref, gain_ref, wout_ref, gm_ref, wq_ref, kv_ref, wo_ref,
                     g_ref, w_ref, b_ref, x2_ref, h_ref, r_ref, rt_ref, n_ref, cnt_ref):
    @pl.when((pl.program_id(0) == 0) & (pl.program_id(1) == 0))
    def _():
        cnt_ref[...] = jnp.zeros_like(cnt_ref)

    x1 = _mix_out(x_ref[0], (ya_ref[0], yb_ref[0], yc_ref[0], yd_ref[0]), gain_ref[...], wout_ref)
    x = _mem_attention(x1, gm_ref[...], wq_ref, kv_ref[0], wo_ref)
    x2_ref[0] = x
    tm = x.shape[0]
    ms = jnp.mean(x * x, axis=-1, keepdims=True)
    h = (x * lax.rsqrt(ms + EPS)) * g_ref[...]
    h_ref[...] = _pack_bf16_pairs(h)
    hh, hm, _ = _split3(h)
    w = w_ref[...]
    wh, wm, _ = _split3(w)
    logits = _dot(hh, wh) + _dot(hh, wm) + _dot(hm, wh) + b_ref[...]
    lane_i = _iota(logits.shape, 1)
    lane = lane_i.astype(F32)
    lane_grp = (lane_i // EXPERTS_PER_GROUP).astype(F32)
    big = jnp.float32(1e9)
    neg = -jnp.inf
    is_grp = (lane_i >= N_EXPERTS) & (lane_i < N_EXPERTS + N_EXPERT_GROUPS)
    gl = jnp.where(is_grp, logits, neg)
    gmax = jnp.max(gl, axis=-1, keepdims=True)
    p_grp = 1.0 / jnp.sum(jnp.exp(gl - gmax), axis=-1, keepdims=True)
    g_sel = jnp.min(jnp.where(gl == gmax, lane, big), axis=-1, keepdims=True) - N_EXPERTS
    in_grp = (lane_i < N_EXPERTS) & (lane_grp == g_sel)
    el = jnp.where(in_grp, logits, neg)
    m1 = jnp.max(el, axis=-1, keepdims=True)
    i1 = jnp.min(jnp.where(el == m1, lane, big), axis=-1, keepdims=True)
    el2 = jnp.where(lane == i1, neg, el)
    m2 = jnp.max(el2, axis=-1, keepdims=True)
    i2 = jnp.min(jnp.where(el2 == m2, lane, big), axis=-1, keepdims=True)
    e21 = jnp.exp(m2 - m1)
    w1 = 1.0 / (1.0 + e21)
    w2 = e21 / (1.0 + e21)
    oh1 = lane == i1
    oh2 = lane == i2
    onehot = jnp.where(oh1 | oh2, 1.0, 0.0)
    strict = jnp.where(_iota((tm, tm), 0) > _iota((tm, tm), 1), 1.0, 0.0).astype(BF16)
    before = _dot(strict, onehot.astype(BF16)) + cnt_ref[...]
    rank1 = jnp.sum(jnp.where(oh1, before, 0.0), axis=-1, keepdims=True)
    rank2 = jnp.sum(jnp.where(oh2, before, 0.0), axis=-1, keepdims=True)
    cnt = before[tm - 1:tm, :] + onehot[tm - 1:tm, :]
    cnt_ref[...] = cnt
    n_ref[...] = jnp.broadcast_to(cnt, n_ref.shape)
    cols = (i1, i2, p_grp * w1, p_grp * w2, rank1, rank2)
    route = jnp.zeros(logits.shape, F32)
    for j, col in enumerate(cols):
        route = jnp.where(lane_i == j, col, route)
    r_ref[...] = route
    rt_ref[...] = route.T[0:8, :]


def post_mix(x3, ya, yb, yc, yd, gain, w_out_bf16, g_mem, wq_bf16, kv3, wo_bf16, g_ffn, w_router, b_router):
    b, s, d = x3.shape
    t = ROW_TILE
    gw = GROUP_WIDTH
    nt = s // t
    mlen = kv3.shape[1]
    tile = lambda w: pl.BlockSpec((1, t, w), lambda i, j: (i, j, 0))
    const = lambda shape: pl.BlockSpec(shape, lambda i, j: (0,) * len(shape))
    tok = lambda w: pl.BlockSpec((t, w), lambda i, j: (i * nt + j, 0))
    return pl.pallas_call(
        _post_mix_kernel,
        grid=(b, nt),
        in_specs=[tile(d), tile(gw), tile(gw), tile(gw), tile(gw),
                  const((1, 4 * gw)), const((4 * gw, d)),
                  const((1, d)), const((d, gw)), pl.BlockSpec((1, mlen, 2 * gw), lambda i, j: (i, 0, 0)), const((gw, d)),
                  const((1, d)), const((d, SMALL_W)), const((1, SMALL_W))],
        out_specs=[tile(d), tok(d // 2), tok(SMALL_W), pl.BlockSpec((8, t), lambda i, j: (0, i * nt + j)),
                   const((8, SMALL_W))],
        out_shape=[jax.ShapeDtypeStruct((b, s, d), F32), jax.ShapeDtypeStruct((b * s, d // 2), jnp.uint32),
                   jax.ShapeDtypeStruct((b * s, SMALL_W), F32), jax.ShapeDtypeStruct((8, b * s), F32),
                   jax.ShapeDtypeStruct((8, SMALL_W), F32)],
        scratch_shapes=[pltpu.VMEM((1, SMALL_W), F32)],
        compiler_params=_cparams("arbitrary", "arbitrary"),
        name="post_mix",
    )(x3, ya, yb, yc, yd, gain.reshape(1, 4 * gw), w_out_bf16, g_mem.reshape(1, d), wq_bf16, kv3, wo_bf16,
      g_ffn.reshape(1, d), w_router, b_router)


def _experts_kernel(te_ref, nu_ref, run_ref, nxt_ref, xs_ref, wgu_hbm, wdn_hbm, y_ref,
                    wgu_buf, wdn_buf, wgu_bf, wdn_bf, sem, *, layer):
    i = pl.program_id(0)

    def weight_copies(expert, slot):
        return (pltpu.make_async_copy(wgu_hbm.at[layer, expert], wgu_buf.at[slot], sem.at[slot, 0]),
                pltpu.make_async_copy(wdn_hbm.at[layer, expert], wdn_buf.at[slot], sem.at[slot, 1]))

    @pl.when(i >= nu_ref[0])
    def _():
        y_ref[...] = jnp.zeros_like(y_ref)

    @pl.when(i < nu_ref[0])
    def _():
        expert = te_ref[i]
        slot = run_ref[i] % 2

        @pl.when(i == 0)
        def _():
            for c in weight_copies(expert, slot):
                c.start()

        @pl.when((i == 0) | (te_ref[jnp.maximum(i - 1, 0)] != expert))
        def _():
            for c in weight_copies(expert, slot):
                c.wait()
            nxt = nxt_ref[i]

            @pl.when(nxt >= 0)
            def _():
                for c in weight_copies(nxt, 1 - slot):
                    c.start()

            wgu_bf[...] = wgu_buf[slot].astype(BF16)
            wdn_bf[...] = wdn_buf[slot].astype(BF16)

        gu = _dot(_unpack_bf16_pairs(xs_ref[...]).astype(BF16), wgu_bf[...])
        act = _silu(gu[:, 0:D_EXPERT]) * gu[:, D_EXPERT:2 * D_EXPERT]
        y_ref[...] = _pack_bf16_pairs(_dot(act.astype(BF16), wdn_bf[...]))


def moe_experts(xs, tile_expert, n_used, run_index, next_expert, w_gu, w_dn, layer, tm):
    n_rows, dp = xs.shape
    d = 2 * dp
    return pl.pallas_call(
        functools.partial(_experts_kernel, layer=layer),
        grid_spec=pltpu.PrefetchScalarGridSpec(
            num_scalar_prefetch=4,
            grid=(n_rows // tm,),
            in_specs=[pl.BlockSpec((tm, dp), lambda i, te, nu, run, nxt: (jnp.minimum(i, nu[0] - 1), 0)),
                      pl.BlockSpec(memory_space=pl.ANY),
                      pl.BlockSpec(memory_space=pl.ANY)],
            out_specs=pl.BlockSpec((tm, dp), lambda i, te, nu, run, nxt: (i, 0)),
            scratch_shapes=[pltpu.VMEM((2, d, 2 * D_EXPERT), F32), pltpu.VMEM((2, D_EXPERT, d), F32),
                            pltpu.VMEM((d, 2 * D_EXPERT), BF16), pltpu.VMEM((D_EXPERT, d), BF16),
                            pltpu.SemaphoreType.DMA((2, 2))]),
        out_shape=jax.ShapeDtypeStruct((n_rows, dp), jnp.uint32),
        compiler_params=_cparams("arbitrary"),
        name="moe_experts",
    )(tile_expert, n_used, run_index, next_expert, xs, w_gu, w_dn)


def sc_gather_rows(table, idx):
    n_idx = idx.shape[0]
    width = table.shape[1]
    info = plsc.get_sparse_core_info()
    n_workers = info.num_cores * info.num_subcores
    per_worker = n_idx // n_workers
    chunk = SC_GATHER_CHUNK
    assert n_idx % (n_workers * chunk) == 0 and width % info.num_lanes == 0
    mesh = plsc.VectorSubcoreMesh(core_axis_name="c", subcore_axis_name="s")

    def body(table_hbm, idx_hbm, out_hbm, idx_v, rows_v, sem):
        wid = lax.axis_index("s") * info.num_cores + lax.axis_index("c")
        base = wid * per_worker

        @pl.loop(0, per_worker // chunk)
        def _(j):
            off = pl.multiple_of(base + j * chunk, chunk)
            pltpu.sync_copy(idx_hbm.at[pl.ds(off, chunk)], idx_v)
            pltpu.async_copy(table_hbm.at[idx_v], rows_v, sem).wait()
            pltpu.sync_copy(rows_v, out_hbm.at[pl.ds(off, chunk)])

    return pl.kernel(
        body,
        out_type=jax.ShapeDtypeStruct((n_idx, width), table.dtype),
        mesh=mesh,
        scratch_types=[pltpu.VMEM((chunk,), jnp.int32), pltpu.VMEM((chunk, width), table.dtype),
                       pltpu.SemaphoreType.DMA],
        name="sc_gather_rows",
    )(table, idx)


def sc_dispatch_rows(rows, dest, pad_idx, zero_rows):
    m, width = rows.shape
    n_pad = pad_idx.shape[0]
    info = plsc.get_sparse_core_info()
    n_workers = info.num_cores * info.num_subcores
    chunk = SC_GATHER_CHUNK
    assert m % (n_workers * chunk) == 0 and n_pad % (n_workers * chunk) == 0 and zero_rows.shape == (chunk, width)
    rows_per_worker = m // n_workers
    pad_per_worker = n_pad // n_workers
    mesh = plsc.VectorSubcoreMesh(core_axis_name="c", subcore_axis_name="s")

    def body(rows_hbm, dest_hbm, pad_hbm, zero_hbm, out_hbm, idx_v, rows_v, sem):
        wid = lax.axis_index("s") * info.num_cores + lax.axis_index("c")

        @pl.loop(0, rows_per_worker // chunk)
        def _(j):
            off = pl.multiple_of(wid * rows_per_worker + j * chunk, chunk)
            pltpu.sync_copy(rows_hbm.at[pl.ds(off, chunk)], rows_v)
            for slot in range(2):
                pltpu.sync_copy(dest_hbm.at[pl.ds(slot * m + off, chunk)], idx_v)
                pltpu.async_copy(rows_v, out_hbm.at[idx_v], sem).wait()

        pltpu.sync_copy(zero_hbm, rows_v)

        @pl.loop(0, pad_per_worker // chunk)
        def _(j):
            off = pl.multiple_of(wid * pad_per_worker + j * chunk, chunk)
            pltpu.sync_copy(pad_hbm.at[pl.ds(off, chunk)], idx_v)
            pltpu.async_copy(rows_v, out_hbm.at[idx_v], sem).wait()

    return pl.kernel(
        body,
        out_type=jax.ShapeDtypeStruct((2 * m + n_pad, width), rows.dtype),
        mesh=mesh,
        scratch_types=[pltpu.VMEM((chunk,), jnp.int32), pltpu.VMEM((chunk, width), rows.dtype),
                       pltpu.SemaphoreType.DMA],
        name="sc_dispatch_rows",
    )(rows, dest, pad_idx, zero_rows)


def _combine_rows_kernel(x_ref, r_ref, g_ref, y1_ref, y2_ref, o_ref, *, final_norm):
    r = r_ref[...]
    out = x_ref[...] + r[:, 2:3] * _unpack_bf16_pairs(y1_ref[...]) + r[:, 3:4] * _unpack_bf16_pairs(y2_ref[...])
    if final_norm:
        ms = jnp.mean(out * out, axis=-1, keepdims=True)
        out = (out * lax.rsqrt(ms + EPS)) * g_ref[...]
    o_ref[...] = out


def moe_combine_rows(x2, route, y_rows, g_final, tm, final_norm):
    m, d = x2.shape
    steps = m // tm
    return pl.pallas_call(
        functools.partial(_combine_rows_kernel, final_norm=final_norm),
        grid=(steps,),
        in_specs=[pl.BlockSpec((tm, d), lambda i: (i, 0)),
                  pl.BlockSpec((tm, SMALL_W), lambda i: (i, 0)),
                  pl.BlockSpec((1, d), lambda i: (0, 0)),
                  pl.BlockSpec((tm, d // 2), lambda i: (i, 0)),
                  pl.BlockSpec((tm, d // 2), lambda i: (i + steps, 0))],
        out_specs=pl.BlockSpec((tm, d), lambda i: (i, 0)),
        out_shape=jax.ShapeDtypeStruct((m, d), F32),
        compiler_params=_cparams("parallel"),
        name="moe_combine_rows",
    )(x2, route, g_final.reshape(1, d), y_rows, y_rows)


def moe_routed(x2, h, route, route_t, counts, w_gu, w_dn, layer, g_final, final_norm):
    m, d = x2.shape
    te_rows = EXPERT_TILE
    n_rows = 2 * m + N_EXPERTS * te_rows
    cnt = counts[0, 0:N_EXPERTS].astype(jnp.int32)
    padded = (cnt + te_rows - 1) // te_rows * te_rows
    seg_end = jnp.cumsum(padded)
    seg_start = seg_end - padded
    idx = route_t[0:2].astype(jnp.int32)
    onehot = idx[:, :, None] == jnp.arange(N_EXPERTS, dtype=jnp.int32)
    dest = (jnp.sum(jnp.where(onehot, seg_start, 0), axis=-1) + route_t[4:6].astype(jnp.int32)).reshape(2 * m)
    tile_start = jnp.arange(n_rows // te_rows, dtype=jnp.int32) * te_rows
    tile_expert = jnp.minimum(jnp.sum((tile_start[:, None] >= seg_end[None, :]).astype(jnp.int32), axis=1), N_EXPERTS - 1)
    n_used = (seg_end[N_EXPERTS - 1] // te_rows).reshape(1)
    n_pad = n_rows - 2 * m
    experts = jnp.arange(N_EXPERTS, dtype=jnp.int32)
    pad_end = jnp.cumsum(padded - cnt)
    k = jnp.arange(n_pad, dtype=jnp.int32)
    k_expert = jnp.sum((k[:, None] >= pad_end[None, :]).astype(jnp.int32), axis=1)
    first_pad = seg_start + cnt - (pad_end - (padded - cnt))
    in_segment = jnp.sum(jnp.where(k_expert[:, None] == experts[None, :], first_pad[None, :], 0), axis=1) + k
    pad_idx = jnp.where(k_expert < N_EXPERTS, in_segment, seg_end[N_EXPERTS - 1] + k - pad_end[N_EXPERTS - 1])
    xs = sc_dispatch_rows(h, dest, pad_idx, jnp.zeros((SC_GATHER_CHUNK, d // 2), h.dtype))
    n_tiles = n_rows // te_rows
    tile_ids = jnp.arange(n_tiles, dtype=jnp.int32)
    prev_expert = jnp.concatenate([jnp.full((1,), -1, jnp.int32), tile_expert[:-1]])
    run_index = jnp.maximum(jnp.cumsum(((tile_expert != prev_expert) & (tile_ids < n_used[0])).astype(jnp.int32)) - 1, 0)
    next_tile = jnp.take(seg_end, tile_expert) // te_rows
    next_expert = jnp.where(next_tile < n_used[0], jnp.take(tile_expert, jnp.minimum(next_tile, n_tiles - 1)), -1)
    ys = moe_experts(xs, tile_expert, n_used, run_index, next_expert, w_gu, w_dn, layer, tm=te_rows)
    y_rows = sc_gather_rows(ys, dest)
    return moe_combine_rows(x2, route, y_rows, g_final, tm=ROW_TILE, final_norm=final_norm)


def _w_in_prep_kernel(w_ref, o_ref):
    gw = GROUP_WIDTH
    a_end = 7 * gw
    c_start = a_end + HEADS
    c_end = c_start + 5 * gw
    z_start = c_end + 2 * HEADS
    assert a_end % SMALL_W == LANE_FOX and c_end % SMALL_W == LANE_BETA and LANE_GDEC == LANE_BETA + HEADS
    w = w_ref[...]
    o_ref[:, 0:a_end] = w[:, 0:a_end].astype(BF16)
    o_ref[:, a_end:a_end + 5 * gw] = w[:, c_start:c_end].astype(BF16)
    o_ref[:, 12 * gw:13 * gw] = w[:, z_start:z_start + gw].astype(BF16)
    blk_fox = w[:, a_end:a_end + SMALL_W]
    blk_gdn = w[:, c_end - LANE_BETA:c_end - LANE_BETA + SMALL_W]
    lane = _iota(blk_fox.shape, 1)
    small = jnp.where(lane < LANE_BETA, blk_fox, jnp.where(lane < LANE_GDEC + HEADS, blk_gdn, 0.0))
    o_ref[:, 13 * gw:13 * gw + SMALL_W] = small.astype(BF16)
    o_ref[:, 13 * gw + SMALL_W:] = jnp.zeros((w.shape[0], SMALL_W), BF16)


def w_in_prep(w_all, layer):
    _, d, n = w_all.shape
    tr = 128
    return pl.pallas_call(
        _w_in_prep_kernel,
        grid=(d // tr,),
        in_specs=[pl.BlockSpec((None, tr, n), lambda i: (layer, i, 0))],
        out_specs=pl.BlockSpec((tr, N_PROJ), lambda i: (i, 0)),
        out_shape=jax.ShapeDtypeStruct((d, N_PROJ), BF16),
        compiler_params=_cparams("parallel"),
        name="w_in_prep",
    )(w_all)


def _block_diag(w):
    h = w.shape[0]
    eye = jnp.eye(h, dtype=w.dtype)
    return (eye[:, None, :, None] * w[:, :, None, :]).reshape(h * HEAD_DIM, h * HEAD_DIM)


def kernel(x, mem, norm_mix, w_in, hgrn_lb, fox_fb, lru_conv_w, lru_conv_b, lru_wa, lru_ba, lru_wx, lru_bx, lru_lam, gdn_conv_w, gdn_a_log, gdn_dt_bias, mix_gain, w_out, norm_mem, norm_memkv, w_mq, w_mkv, w_mo, norm_ffn, w_rg, b_rg, w_re, b_re, w_e_gu, w_e_dn, norm_final):
    b, s, d = x.shape
    depth = w_in.shape[0]
    t_tok = b * s
    mlen = mem.shape[1]
    gw = GROUP_WIDTH

    lb_all = jnp.cumsum(jax.nn.softmax(hgrn_lb.astype(F32), axis=0), axis=0)
    lb_all = lb_all - lb_all[0]

    x = x.astype(F32)
    for l in range(depth):
        proj = norm_matmul(x.reshape(t_tok, d), norm_mix[l], w_in_prep(w_in, l), tm=ROW_TILE)
        proj3 = proj.reshape(b, s, N_PROJ)
        prm = jnp.zeros((8, SMALL_W), F32)
        prm = prm.at[0, LANE_FOX:LANE_FOX + HEADS].set(fox_fb[l].astype(F32))
        prm = prm.at[0, LANE_GDEC:LANE_GDEC + HEADS].set(gdn_dt_bias[l].astype(F32))
        prm = prm.at[1, LANE_GDEC:LANE_GDEC + HEADS].set(gdn_a_log[l].astype(F32))
        sm, c_exp = small_prep(proj3, prm)

        ya = hgrn2(proj3, lb_all[l])
        yb = fox_attention(proj3, c_exp)
        yc = rglru(proj3, lru_conv_w[l], lru_conv_b[l], _block_diag(lru_wa[l]).astype(BF16), lru_ba[l],
                   _block_diag(lru_wx[l]).astype(BF16), lru_bx[l], lru_lam[l])
        yd = gdn(proj3, sm, gdn_conv_w[l])

        kv = norm_matmul(mem.reshape(b * mlen, d), norm_memkv[l], w_mkv[l].astype(BF16), tm=256)
        w_router = jnp.concatenate([w_re[l], w_rg[l], jnp.zeros((d, SMALL_W - N_EXPERTS - N_EXPERT_GROUPS), F32)], axis=1)
        b_router = jnp.concatenate([b_re[l], b_rg[l], jnp.zeros((SMALL_W - N_EXPERTS - N_EXPERT_GROUPS,), F32)]).reshape(1, SMALL_W)
        x, h, route, route_t, counts = post_mix(x, ya, yb, yc, yd, mix_gain[l], w_out[l].astype(BF16), norm_mem[l],
                                                w_mq[l].astype(BF16), kv.reshape(b, mlen, 2 * gw), w_mo[l].astype(BF16),
                                                norm_ffn[l], w_router, b_router)

        x = moe_routed(x.reshape(t_tok, d), h, route, route_t, counts, w_e_gu, w_e_dn, l,
                       norm_final, final_norm=(l == depth - 1)).reshape(b, s, d)

    return x
```

```python
import functools
import math

import jax
import jax.numpy as jnp
from jax import lax
from jax.experimental import pallas as pl
from jax.experimental.pallas import tpu as pltpu
from jax.experimental.pallas import tpu_sc as plsc

F32 = jnp.float32
BF16 = jnp.bfloat16

HEAD_DIM = 64
GROUP_WIDTH = 256
HEADS = GROUP_WIDTH // HEAD_DIM
GDN_CHUNK = 64
GDN_TILE = 128
GDN_BATCH = 4
FOX_BATCH = 2
HGRN_BATCH = 4
HGRN_CHUNK = 16
CONV_W = 4
LRU_C = 8.0
EPS = 1e-6
N_EXPERT_GROUPS = 4
EXPERTS_PER_GROUP = 8
N_EXPERTS = N_EXPERT_GROUPS * EXPERTS_PER_GROUP
D_EXPERT = 256
SC_GATHER_CHUNK = 128
EXPERT_TILE = 512
SMALL_W = 128
SEQ_TILE = 256
ROW_TILE = 512
VMEM_LIMIT = 56 * 1024 * 1024

(COL_AQ, COL_AF, COL_AI, COL_AG, COL_BQ, COL_BK, COL_BV, COL_CX, COL_CG,
 COL_DQ, COL_DK, COL_DV, COL_DZ) = range(13)
N_PROJ = 13 * GROUP_WIDTH + 2 * SMALL_W
COL_SMALL = 13 * GROUP_WIDTH // SMALL_W
LANE_FOX = 0
LANE_BETA = 4
LANE_GDEC = 8


def _cparams(*sem):
    return pltpu.CompilerParams(dimension_semantics=sem, vmem_limit_bytes=VMEM_LIMIT)


def _dot(a, b):
    return jnp.dot(a, b, preferred_element_type=F32)


def _dot_nt(a, b):
    return lax.dot_general(a, b, (((1,), (1,)), ((), ())), preferred_element_type=F32)


def _dot_tn(a, b):
    return lax.dot_general(a, b, (((0,), (0,)), ((), ())), preferred_element_type=F32)


def _split3(x):
    h = x.astype(BF16)
    r = x - h.astype(F32)
    m = r.astype(BF16)
    l = (r - m.astype(F32)).astype(BF16)
    return h, m, l


def _dot_exact_rhs(x, w_bf16):
    h, m, l = _split3(x)
    return _dot(h, w_bf16) + _dot(m, w_bf16) + _dot(l, w_bf16)


def _dot_exact_lhs(w_bf16, x):
    h, m, l = _split3(x)
    return _dot(w_bf16, h) + _dot(w_bf16, m) + _dot(w_bf16, l)


def _iota(shape, dim):
    return lax.broadcasted_iota(jnp.int32, shape, dim)


def _head_ones(n=GROUP_WIDTH):
    r = _iota((n, n), 0) // HEAD_DIM
    c = _iota((n, n), 1) // HEAD_DIM
    return r == c


def _sigmoid(x):
    return 1.0 / (1.0 + jnp.exp(-x))


def _silu(x):
    return x * _sigmoid(x)


def _log_sigmoid(x):
    return jnp.minimum(x, 0.0) - jnp.log1p(jnp.exp(-jnp.abs(x)))


def _softplus(x):
    return jnp.maximum(x, 0.0) + jnp.log1p(jnp.exp(-jnp.abs(x)))


def _gelu_tanh(x):
    return 0.5 * x * (1.0 + jnp.tanh(math.sqrt(2.0 / math.pi) * (x + 0.044715 * (x * x * x))))


def _pack_bf16_pairs(x):
    n = x.shape[1] // 2
    u = lax.bitcast_convert_type(x.astype(BF16).astype(F32), jnp.uint32)
    return u[:, :n] | (u[:, n:] >> 16)


def _unpack_bf16_pairs(p):
    hi = lax.bitcast_convert_type(p & jnp.uint32(0xFFFF0000), F32)
    lo = lax.bitcast_convert_type(p << 16, F32)
    return jnp.concatenate([hi, lo], axis=1)


def _head_mean_sq(x, ones_bf16):
    return _dot_exact_rhs(x * x, ones_bf16) * (1.0 / HEAD_DIM)


def _head_rms(x, ones_bf16):
    return x * lax.rsqrt(_head_mean_sq(x, ones_bf16) + EPS)


def _stack_heads(x):
    lane_head = _iota(x.shape, 1) // HEAD_DIM
    parts = []
    for h in range(HEADS):
        parts.append(jnp.where(lane_head == h, x, 0.0))
    return jnp.concatenate(parts, axis=0)


def _causal_conv4(x, pad_ref, w):
    r = x.shape[0]
    pad_ref[8:8 + r, :] = x
    acc = x * w[CONV_W - 1:CONV_W, :]
    for k in range(1, CONV_W):
        acc = acc + pad_ref[8 - k:8 - k + r, :] * w[CONV_W - 1 - k:CONV_W - k, :]
    pad_ref[0:8, :] = x[r - 8:r, :]
    return acc


def _norm_matmul_kernel(x_ref, g_ref, w_ref, o_ref):
    x = x_ref[...]
    ms = jnp.mean(x * x, axis=-1, keepdims=True)
    h = (x * lax.rsqrt(ms + EPS)) * g_ref[...]
    o_ref[...] = _dot(h.astype(BF16), w_ref[...]).astype(o_ref.dtype)


def norm_matmul(x, g, w_bf16, tm, out_dtype=F32):
    m, d = x.shape
    n = w_bf16.shape[1]
    return pl.pallas_call(
        _norm_matmul_kernel,
        grid=(m // tm,),
        in_specs=[pl.BlockSpec((tm, d), lambda i: (i, 0)),
                  pl.BlockSpec((1, d), lambda i: (0, 0)),
                  pl.BlockSpec((d, n), lambda i: (0, 0))],
        out_specs=pl.BlockSpec((tm, n), lambda i: (i, 0)),
        out_shape=jax.ShapeDtypeStruct((m, n), out_dtype),
        compiler_params=_cparams("parallel"),
        name="norm_matmul",
    )(x, g.reshape(1, d), w_bf16)


def _small_prep_kernel(s_ref, p_ref, o_ref, x_ref):
    S = s_ref.shape[1]
    W = s_ref.shape[2]
    blk = SEQ_TILE
    lane = _iota((blk, W), 1)
    is_fox = (lane >= LANE_FOX) & (lane < LANE_FOX + HEADS)
    is_beta = (lane >= LANE_BETA) & (lane < LANE_BETA + HEADS)
    is_gdec = (lane >= LANE_GDEC) & (lane < LANE_GDEC + HEADS)
    r = _iota((blk, blk), 0)
    c = _iota((blk, blk), 1)
    tril_all = jnp.where(r >= c, 1.0, 0.0).astype(BF16)
    tril_chunk = jnp.where((r >= c) & (r // GDN_CHUNK == c // GDN_CHUNK), 1.0, 0.0).astype(BF16)
    neg_exp_a = -jnp.exp(p_ref[1:2, :])
    carry = jnp.zeros((1, W), F32)
    for i in range(S // blk):
        sl = pl.ds(i * blk, blk)
        z = s_ref[0, sl, :] + p_ref[0:1, :]
        fox = jnp.where(is_fox, _log_sigmoid(z), 0.0)
        beta = jnp.where(is_beta, _sigmoid(z), 0.0)
        gdec = jnp.where(is_gdec, neg_exp_a * _softplus(z), 0.0)
        cf = _dot_exact_lhs(tril_all, fox) + carry
        cg = _dot_exact_lhs(tril_chunk, gdec)
        carry = cf[blk - 1:blk, :]
        o_ref[0, sl, :] = cf + cg + beta
        both = cf + cg
        for j, src_lane in enumerate(range(LANE_FOX, LANE_FOX + HEADS)):
            x_ref[0, sl, j * W:(j + 1) * W] = jnp.broadcast_to(both[:, src_lane:src_lane + 1], (blk, W))


def small_prep(proj3, params):
    b, s, _ = proj3.shape
    w = SMALL_W
    return pl.pallas_call(
        _small_prep_kernel,
        grid=(b,),
        in_specs=[pl.BlockSpec((1, s, w), lambda i: (i, 0, COL_SMALL)),
                  pl.BlockSpec((8, w), lambda i: (0, 0))],
        out_specs=[pl.BlockSpec((1, s, w), lambda i: (i, 0, 0)),
                   pl.BlockSpec((1, s, HEADS * w), lambda i: (i, 0, 0))],
        out_shape=[jax.ShapeDtypeStruct((b, s, w), F32), jax.ShapeDtypeStruct((b, s, HEADS * w), F32)],
        compiler_params=_cparams("parallel"),
        name="small_prep",
    )(proj3, params)


def _hgrn2_kernel(q_ref, f_ref, i_ref, g_ref, lb_ref, o_ref, st_ref, qs, ks, vs, bs, os_):
    nb = q_ref.shape[0]
    T = q_ref.shape[1]
    C = HGRN_CHUNK
    seqs = range(nb)

    @pl.when(pl.program_id(1) == 0)
    def _():
        st_ref[...] = jnp.zeros_like(st_ref)

    same_head = _head_ones()
    ones_bf = jnp.where(same_head, 1.0, 0.0).astype(BF16)
    lb = lb_ref[...]
    log_lb = jnp.log(lb)
    r = _iota((T, T), 0)
    c = _iota((T, T), 1)
    tril_chunk = jnp.where((r >= c) & (r // C == c // C), 1.0, 0.0).astype(BF16)
    for bb in seqs:
        fl = f_ref[bb]
        e = jnp.exp(-jnp.abs(fl))
        c2 = jnp.log1p(-lb) + (jnp.minimum(fl, 0.0) - jnp.log1p(e))
        mx = jnp.maximum(log_lb, c2)
        log_f = mx + jnp.log1p(jnp.exp(-jnp.abs(log_lb - c2)))
        bs[bb] = _dot_exact_lhs(tril_chunk, log_f)
        qs[bb] = _silu(q_ref[bb])
        ks[bb] = (1.0 - lb) * (jnp.where(fl >= 0.0, e, 1.0) / (1.0 + e))
        vs[bb] = i_ref[bb]

    H8 = C // 2
    trow8 = _iota((H8, GROUP_WIDTH), 0)

    def chunk(ci, carry):
        r0 = pl.multiple_of(ci * C, C)
        qc = [qs[bb, pl.ds(r0, C), :] for bb in seqs]
        kc = [ks[bb, pl.ds(r0, C), :] for bb in seqs]
        vc = [vs[bb, pl.ds(r0, C), :] for bb in seqs]
        bc = [bs[bb, pl.ds(r0, C), :] for bb in seqs]
        st = [st_ref[bb] for bb in seqs]
        a = []
        for bb in seqs:
            parts = []
            for half, s_range in ((0, range(H8)), (1, range(C))):
                b_half = bc[bb][half * H8:(half + 1) * H8, :]
                q_half = qc[bb][half * H8:(half + 1) * H8, :]
                for s in s_range:
                    blk = jnp.exp(b_half - bc[bb][s:s + 1, :]) * (q_half * kc[bb][s:s + 1, :])
                    if s >= half * H8:
                        blk = jnp.where(trow8 >= s - half * H8, blk, 0.0)
                    parts.append(blk)
            a.append(jnp.concatenate(parts, axis=0).astype(BF16))
        sc = [_dot(a[bb], ones_bf) for bb in seqs]
        o = [_dot_nt((qc[bb] * jnp.exp(bc[bb])).astype(BF16), st[bb].astype(BF16)) for bb in seqs]
        b_last = [bc[bb][C - 1:C, :] for bb in seqs]
        upd = [_dot_tn(vc[bb].astype(BF16), (kc[bb] * jnp.exp(b_last[bb] - bc[bb])).astype(BF16)) for bb in seqs]
        for bb in seqs:
            top = o[bb][0:H8, :]
            bot = o[bb][H8:C, :]
            for s in range(H8):
                top = top + sc[bb][s * H8:(s + 1) * H8, :] * vc[bb][s:s + 1, :]
            for s in range(C):
                bot = bot + sc[bb][(H8 + s) * H8:(H8 + s + 1) * H8, :] * vc[bb][s:s + 1, :]
            os_[bb, pl.ds(r0, C), :] = jnp.concatenate([top, bot], axis=0)
            st_ref[bb] = st[bb] * jnp.exp(b_last[bb]) + jnp.where(same_head, upd[bb], 0.0)
        return carry

    lax.fori_loop(0, T // C, chunk, 0)
    for bb in seqs:
        o_ref[bb] = _head_rms(os_[bb], ones_bf) * _silu(g_ref[bb])


def hgrn2(proj3, lb):
    b, s, _ = proj3.shape
    t = SEQ_TILE
    gw = GROUP_WIDTH
    nb = HGRN_BATCH
    col = lambda cidx: pl.BlockSpec((nb, t, gw), lambda i, j: (i, j, cidx))
    return pl.pallas_call(
        _hgrn2_kernel,
        grid=(b // nb, s // t),
        in_specs=[col(COL_AQ), col(COL_AF), col(COL_AI), col(COL_AG),
                  pl.BlockSpec((1, gw), lambda i, j: (0, 0))],
        out_specs=pl.BlockSpec((nb, t, gw), lambda i, j: (i, j, 0)),
        out_shape=jax.ShapeDtypeStruct((b, s, gw), F32),
        scratch_shapes=[pltpu.VMEM((nb, gw, gw), F32)] + [pltpu.VMEM((nb, t, gw), F32)] * 5,
        compiler_params=_cparams("parallel", "arbitrary"),
        name="hgrn2",
    )(proj3, proj3, proj3, proj3, lb.reshape(1, gw))


def _fox_kernel(q_ref, k_ref, v_ref, cx_ref, o_ref, kb_ref, vt_ref, acc_ref):
    nb = q_ref.shape[0]
    tq = q_ref.shape[1]
    tk = tq
    S = k_ref.shape[1]
    qi = pl.program_id(1)
    seqs = range(nb)

    @pl.when(qi == 0)
    def _():
        for bb in seqs:
            for i in range(S // tk):
                sl = pl.ds(i * tk, tk)
                kb_ref[bb, sl, :] = k_ref[bb, sl, :].astype(BF16)
                vt_ref[bb, :, sl] = v_ref[bb, sl, :].T.astype(BF16)

    lane_head = _iota((tq, GROUP_WIDTH), 1) // HEAD_DIM
    qh = []
    for bb in seqs:
        qsc = q_ref[bb] * (HEAD_DIM ** -0.5)
        qh.append([jnp.where(lane_head == h, qsc, 0.0).astype(BF16) for h in range(HEADS)])
    acc_ref[...] = jnp.zeros_like(acc_ref)
    reps = tq // SMALL_W
    causal = _iota((tk, tq), 0) <= _iota((tk, tq), 1)

    def step(kb, stats, masked):
        k0 = pl.multiple_of(kb * tk, tk)
        new_stats = [[None] * HEADS for _ in seqs]
        strips = [(h, bb) for h in range(HEADS) for bb in seqs]

        def scores(h, bb):
            return _dot_nt(kb_ref[bb, pl.ds(k0, tk), :], qh[bb][h])

        ahead = 4
        queue = [scores(*strips[j]) for j in range(ahead)]
        for i, (h, bb) in enumerate(strips):
            st = queue.pop(0)
            if i + ahead < len(strips):
                queue.append(scores(*strips[i + ahead]))
            rows = slice(h * HEAD_DIM, (h + 1) * HEAD_DIM)
            cx = cx_ref[bb, pl.ds(k0, tk), h * SMALL_W:(h + 1) * SMALL_W]
            st = st - jnp.concatenate([cx] * reps, axis=1)
            if masked:
                st = jnp.where(causal, st, -jnp.inf)
            m_old, l_old = stats[bb][h]
            m_new = jnp.maximum(m_old, jnp.max(st, axis=0, keepdims=True))
            alpha = jnp.exp(m_old - m_new)
            p = jnp.exp(st - m_new)
            l_new = alpha * l_old + jnp.sum(p, axis=0, keepdims=True)
            pv = _dot(vt_ref[bb, rows, pl.ds(k0, tk)], p.astype(BF16))
            acc_ref[bb, rows, :] = alpha * acc_ref[bb, rows, :] + pv
            new_stats[bb][h] = (m_new, l_new)
        return tuple(tuple(new_stats[bb]) for bb in seqs)

    init = tuple(tuple((jnp.full((1, tq), -jnp.inf, F32), jnp.zeros((1, tq), F32)) for _ in range(HEADS)) for _ in seqs)
    stats = lax.fori_loop(0, qi, lambda kb, c: step(kb, c, False), init)
    stats = step(qi, stats, True)

    ones_bf = jnp.where(_head_ones(), 1.0, 0.0).astype(BF16)
    for bb in seqs:
        inv_l = jnp.concatenate([jnp.broadcast_to(1.0 / stats[bb][h][1], (HEAD_DIM, tq)) for h in range(HEADS)], axis=0)
        o_ref[bb] = _head_rms((acc_ref[bb] * inv_l).T, ones_bf)


def fox_attention(proj3, c_exp):
    b, s, _ = proj3.shape
    tq = SEQ_TILE
    gw = GROUP_WIDTH
    nb = FOX_BATCH
    return pl.pallas_call(
        _fox_kernel,
        grid=(b // nb, s // tq),
        in_specs=[pl.BlockSpec((nb, tq, gw), lambda i, j: (i, j, COL_BQ)),
                  pl.BlockSpec((nb, s, gw), lambda i, j: (i, 0, COL_BK)),
                  pl.BlockSpec((nb, s, gw), lambda i, j: (i, 0, COL_BV)),
                  pl.BlockSpec((nb, s, HEADS * SMALL_W), lambda i, j: (i, 0, 0))],
        out_specs=pl.BlockSpec((nb, tq, gw), lambda i, j: (i, j, 0)),
        out_shape=jax.ShapeDtypeStruct((b, s, gw), F32),
        scratch_shapes=[pltpu.VMEM((nb, s, gw), BF16),
                        pltpu.VMEM((nb, gw, s), BF16),
                        pltpu.VMEM((nb, gw, tq), F32)],
        compiler_params=_cparams("parallel", "arbitrary"),
        name="fox_attention",
    )(proj3, proj3, proj3, c_exp)


def _rglru_kernel(x_ref, g_ref, cw_ref, cb_ref, wa_ref, ba_ref, wx_ref, bx_ref, lam_ref, o_ref, prev_ref, h_ref):
    @pl.when(pl.program_id(1) == 0)
    def _():
        prev_ref[0:8, :] = jnp.zeros((8, prev_ref.shape[1]), F32)
        h_ref[...] = jnp.zeros_like(h_ref)

    x = x_ref[0]
    t = x.shape[0]
    xc = _causal_conv4(x, prev_ref, cw_ref[...]) + cb_ref[...]
    xb = xc.astype(BF16)
    r = _sigmoid(_dot(xb, wa_ref[...]) + ba_ref[...])
    ig = _sigmoid(_dot(xb, wx_ref[...]) + bx_ref[...])
    log_a = (-LRU_C * r) * _softplus(-lam_ref[...])
    a = jnp.exp(log_a)
    u = jnp.sqrt(1.0 - jnp.exp(2.0 * log_a)) * (ig * xc)
    row = _iota(a.shape, 0)
    d = 1
    while d < t:
        valid = row >= d
        u = jnp.where(valid, a * pltpu.roll(u, d, 0) + u, u)
        a = jnp.where(valid, a * pltpu.roll(a, d, 0), a)
        d *= 2
    h = a * h_ref[...] + u
    h_ref[...] = h[t - 1:t, :]
    ones_bf = jnp.where(_head_ones(), 1.0, 0.0).astype(BF16)
    o_ref[0] = _head_rms(h * _gelu_tanh(g_ref[0]), ones_bf)


def rglru(proj3, conv_w, conv_b, wa_bd, ba, wx_bd, bx, lam):
    b, s, _ = proj3.shape
    t = ROW_TILE
    gw = GROUP_WIDTH
    row = lambda: pl.BlockSpec((1, gw), lambda i, j: (0, 0))
    return pl.pallas_call(
        _rglru_kernel,
        grid=(b, s // t),
        in_specs=[pl.BlockSpec((1, t, gw), lambda i, j: (i, j, COL_CX)),
                  pl.BlockSpec((1, t, gw), lambda i, j: (i, j, COL_CG)),
                  pl.BlockSpec((CONV_W, gw), lambda i, j: (0, 0)), row(),
                  pl.BlockSpec((gw, gw), lambda i, j: (0, 0)), row(),
                  pl.BlockSpec((gw, gw), lambda i, j: (0, 0)), row(), row()],
        out_specs=pl.BlockSpec((1, t, gw), lambda i, j: (i, j, 0)),
        out_shape=jax.ShapeDtypeStruct((b, s, gw), F32),
        scratch_shapes=[pltpu.VMEM((8 + t, gw), F32), pltpu.VMEM((1, gw), F32)],
        compiler_params=_cparams("parallel", "arbitrary"),
        name="rglru",
    )(proj3, proj3, conv_w, conv_b.reshape(1, gw), wa_bd, ba.reshape(1, gw), wx_bd, bx.reshape(1, gw), lam.reshape(1, gw))


def _gdn_kernel(q_ref, k_ref, v_ref, z_ref, sm_ref, cw_ref, o_ref, st_ref, pq_ref, pk_ref, pv_ref):
    nb = q_ref.shape[0]
    T = q_ref.shape[1]
    C = GDN_CHUNK

    @pl.when(pl.program_id(1) == 0)
    def _():
        st_ref[...] = jnp.zeros_like(st_ref)
        for pad_ref in (pq_ref, pk_ref, pv_ref):
            pad_ref[:, 0:8, :] = jnp.zeros((nb, 8, pad_ref.shape[2]), F32)

    same_head = _head_ones()
    ones_bf = jnp.where(same_head, 1.0, 0.0).astype(BF16)
    cw = cw_ref[...]
    gw = GROUP_WIDTH
    er = _iota((SMALL_W, gw), 0)
    ec = _iota((SMALL_W, gw), 1) // HEAD_DIM
    exp_beta = jnp.where(er == LANE_BETA + ec, 1.0, 0.0).astype(BF16)
    exp_g = jnp.where(er == LANE_GDEC + ec, 1.0, 0.0).astype(BF16)

    prep = []
    for bb in range(nb):
        xq, xk, xv = q_ref[bb], k_ref[bb], v_ref[bb]
        q = _silu(_causal_conv4(xq, pq_ref.at[bb], cw[:, 0:gw]))
        k = _silu(_causal_conv4(xk, pk_ref.at[bb], cw[:, gw:2 * gw]))
        v = _silu(_causal_conv4(xv, pv_ref.at[bb], cw[:, 2 * gw:3 * gw]))
        q = q * lax.rsqrt(_dot_exact_rhs(q * q, ones_bf) + EPS) * (HEAD_DIM ** -0.5)
        k = k * lax.rsqrt(_dot_exact_rhs(k * k, ones_bf) + EPS)
        sm = sm_ref[bb]
        betax = _dot_exact_rhs(sm, exp_beta)
        gx = _dot_exact_rhs(sm, exp_g)
        prep.append((q, k, v, betax, gx))

    n_chunks = T // C
    trow = _iota((C, gw), 0)
    scol = _iota((C, gw), 1) % HEAD_DIM
    incl = scol <= trow
    strict = scol < trow
    eye = jnp.where(scol == trow, 1.0, 0.0)

    def bdiag(x_cat):
        return _stack_heads(x_cat).astype(BF16)

    chains = []
    for ci in range(n_chunks):
        sl = slice(ci * C, (ci + 1) * C)
        for bb in range(nb):
            q, k, v, betax, gx = prep[bb]
            qc, kc, vc, bx, gc = q[sl], k[sl], v[sl], betax[sl], gx[sl]
            grow = jnp.sum(jnp.where(scol == trow, gc, 0.0), axis=0, keepdims=True)
            eg = jnp.exp(gc)
            kb = kc * bx
            g_last = gc[C - 1:C, :]
            gamma = jnp.where(incl, jnp.exp(jnp.where(incl, gc - grow, 0.0)), 0.0)
            ks_bf = bdiag(kc)
            m = jnp.where(strict, _dot_nt(kb.astype(BF16), ks_bf) * gamma, 0.0)
            chains.append(dict(
                x=eye - m, y=m,
                a_qk=(_dot_nt(qc.astype(BF16), ks_bf) * gamma).astype(BF16),
                rhs_u=bdiag(vc * bx),
                rhs_w=bdiag(kb * eg),
                q_dec=(qc * eg).astype(BF16),
                k_dec=(kc * jnp.exp(g_last - gc)).astype(BF16),
                decay=jnp.exp(g_last)))
    for ch in chains:
        ch["ybd"] = bdiag(ch["y"])
    for _ in range(5):
        for ch in chains:
            ch["y"] = _dot(ch["y"].astype(BF16), ch["ybd"])
            ch["ybd"] = bdiag(ch["y"])
        for ch in chains:
            ch["x"] = ch["x"] + _dot(ch["x"].astype(BF16), ch["ybd"])
    for ch in chains:
        t_inv = ch["x"].astype(BF16)
        ch["u"] = _dot(t_inv, ch["rhs_u"])
        ch["wk"] = _dot(t_inv, ch["rhs_w"]).astype(BF16)

    outs = [[] for _ in range(nb)]
    seqs = range(nb)
    for ci in range(n_chunks):
        chs = [chains[ci * nb + bb] for bb in seqs]
        st = [st_ref[bb] for bb in seqs]
        st_bf = [st[bb].astype(BF16) for bb in seqs]
        v_new = [chs[bb]["u"] - _dot(chs[bb]["wk"], st_bf[bb]) for bb in seqs]
        o_st = [_dot(chs[bb]["q_dec"], st_bf[bb]) for bb in seqs]
        upd = [_dot_tn(chs[bb]["k_dec"], v_new[bb].astype(BF16)) for bb in seqs]
        for bb in seqs:
            outs[bb].append(o_st[bb] + _dot(chs[bb]["a_qk"], bdiag(v_new[bb])))
            st_ref[bb] = st[bb] * chs[bb]["decay"] + jnp.where(same_head, upd[bb], 0.0)
    for bb in range(nb):
        o = jnp.concatenate(outs[bb], axis=0)
        o_ref[bb] = _head_rms(o, ones_bf) * _silu(z_ref[bb])


def gdn(proj3, sm, conv_w):
    b, s, _ = proj3.shape
    t = GDN_TILE
    gw = GROUP_WIDTH
    nb = GDN_BATCH
    col = lambda cidx: pl.BlockSpec((nb, t, gw), lambda i, j: (i, j, cidx))
    return pl.pallas_call(
        _gdn_kernel,
        grid=(b // nb, s // t),
        in_specs=[col(COL_DQ), col(COL_DK), col(COL_DV), col(COL_DZ),
                  pl.BlockSpec((nb, t, SMALL_W), lambda i, j: (i, j, 0)),
                  pl.BlockSpec((CONV_W, 3 * gw), lambda i, j: (0, 0))],
        out_specs=pl.BlockSpec((nb, t, gw), lambda i, j: (i, j, 0)),
        out_shape=jax.ShapeDtypeStruct((b, s, gw), F32),
        scratch_shapes=[pltpu.VMEM((nb, gw, gw), F32)] + [pltpu.VMEM((nb, 8 + t, gw), F32)] * 3,
        compiler_params=_cparams("parallel", "arbitrary"),
        name="gdn",
    )(proj3, proj3, proj3, proj3, sm, conv_w)


def _mix_out(x, ys, gain, w_ref):
    gw = GROUP_WIDTH
    acc = x
    for i, y in enumerate(ys):
        yg = (y * gain[:, i * gw:(i + 1) * gw]).astype(BF16)
        acc = acc + _dot(yg, w_ref[i * gw:(i + 1) * gw, :])
    return acc


def _mem_attention(x, g, wq_ref, kv, wo_ref):
    t = x.shape[0]
    gw = GROUP_WIDTH
    ms = jnp.mean(x * x, axis=-1, keepdims=True)
    h = ((x * lax.rsqrt(ms + EPS)) * g).astype(BF16)
    q = _dot(h, wq_ref[...])
    k = kv[:, 0:gw].astype(BF16)
    v = kv[:, gw:2 * gw].astype(BF16)
    s = _dot_nt(k, _stack_heads(q).astype(BF16)) * (HEAD_DIM ** -0.5)
    s = s - jnp.max(s, axis=0, keepdims=True)
    p = jnp.exp(s)
    p = p * (1.0 / jnp.sum(p, axis=0, keepdims=True))
    ot = _dot_tn(v, p.astype(BF16))
    oc = jnp.concatenate([ot[hh * HEAD_DIM:(hh + 1) * HEAD_DIM, hh * t:(hh + 1) * t] for hh in range(HEADS)], axis=0).T
    return x + _dot(oc.astype(BF16), wo_ref[...])


def _post_mix_kernel(x_ref, ya_ref, yb_ref, yc_ref, yd_ref, gain_ref, wout_ref, gm_ref, wq_ref, kv_ref, wo_ref,
                     g_ref, w_ref, b_ref, x2_ref, h_ref, r_ref, rt_ref, n_ref, cnt_ref):
    @pl.when((pl.program_id(0) == 0) & (pl.program_id(1) == 0))
    def _():
        cnt_ref[...] = jnp.zeros_like(cnt_ref)

    x1 = _mix_out(x_ref[0], (ya_ref[0], yb_ref[0], yc_ref[0], yd_ref[0]), gain_ref[...], wout_ref)
    x = _mem_attention(x1, gm_ref[...], wq_ref, kv_ref[0], wo_ref)
    x2_ref[0] = x
    tm = x.shape[0]
    ms = jnp.mean(x * x, axis=-1, keepdims=True)
    h = (x * lax.rsqrt(ms + EPS)) * g_ref[...]
    h_ref[...] = _pack_bf16_pairs(h)
    hh, hm, _ = _split3(h)
    w = w_ref[...]
    wh, wm, _ = _split3(w)
    logits = _dot(hh, wh) + _dot(hh, wm) + _dot(hm, wh) + b_ref[...]
    lane_i = _iota(logits.shape, 1)
    lane = lane_i.astype(F32)
    lane_grp = (lane_i // EXPERTS_PER_GROUP).astype(F32)
    big = jnp.float32(1e9)
    neg = -jnp.inf
    is_grp = (lane_i >= N_EXPERTS) & (lane_i < N_EXPERTS + N_EXPERT_GROUPS)
    gl = jnp.where(is_grp, logits, neg)
    gmax = jnp.max(gl, axis=-1, keepdims=True)
    p_grp = 1.0 / jnp.sum(jnp.exp(gl - gmax), axis=-1, keepdims=True)
    g_sel = jnp.min(jnp.where(gl == gmax, lane, big), axis=-1, keepdims=True) - N_EXPERTS
    in_grp = (lane_i < N_EXPERTS) & (lane_grp == g_sel)
    el = jnp.where(in_grp, logits, neg)
    m1 = jnp.max(el, axis=-1, keepdims=True)
    i1 = jnp.min(jnp.where(el == m1, lane, big), axis=-1, keepdims=True)
    el2 = jnp.where(lane == i1, neg, el)
    m2 = jnp.max(el2, axis=-1, keepdims=True)
    i2 = jnp.min(jnp.where(el2 == m2, lane, big), axis=-1, keepdims=True)
    e21 = jnp.exp(m2 - m1)
    w1 = 1.0 / (1.0 + e21)
    w2 = e21 / (1.0 + e21)
    oh1 = lane == i1
    oh2 = lane == i2
    onehot = jnp.where(oh1 | oh2, 1.0, 0.0)
    strict = jnp.where(_iota((tm, tm), 0) > _iota((tm, tm), 1), 1.0, 0.0).astype(BF16)
    before = _dot(strict, onehot.astype(BF16)) + cnt_ref[...]
    rank1 = jnp.sum(jnp.where(oh1, before, 0.0), axis=-1, keepdims=True)
    rank2 = jnp.sum(jnp.where(oh2, before, 0.0), axis=-1, keepdims=True)
    cnt = before[tm - 1:tm, :] + onehot[tm - 1:tm, :]
    cnt_ref[...] = cnt
    n_ref[...] = jnp.broadcast_to(cnt, n_ref.shape)
    cols = (i1, i2, p_grp * w1, p_grp * w2, rank1, rank2)
    route = jnp.zeros(logits.shape, F32)
    for j, col in enumerate(cols):
        route = jnp.where(lane_i == j, col, route)
    r_ref[...] = route
    rt_ref[...] = route.T[0:8, :]


def post_mix(x3, ya, yb, yc, yd, gain, w_out_bf16, g_mem, wq_bf16, kv3, wo_bf16, g_ffn, w_router, b_router):
    b, s, d = x3.shape
    t = ROW_TILE
    gw = GROUP_WIDTH
    nt = s // t
    mlen = kv3.shape[1]
    tile = lambda w: pl.BlockSpec((1, t, w), lambda i, j: (i, j, 0))
    const = lambda shape: pl.BlockSpec(shape, lambda i, j: (0,) * len(shape))
    tok = lambda w: pl.BlockSpec((t, w), lambda i, j: (i * nt + j, 0))
    return pl.pallas_call(
        _post_mix_kernel,
        grid=(b, nt),
        in_specs=[tile(d), tile(gw), tile(gw), tile(gw), tile(gw),
                  const((1, 4 * gw)), const((4 * gw, d)),
                  const((1, d)), const((d, gw)), pl.BlockSpec((1, mlen, 2 * gw), lambda i, j: (i, 0, 0)), const((gw, d)),
                  const((1, d)), const((d, SMALL_W)), const((1, SMALL_W))],
        out_specs=[tile(d), tok(d // 2), tok(SMALL_W), pl.BlockSpec((8, t), lambda i, j: (0, i * nt + j)),
                   const((8, SMALL_W))],
        out_shape=[jax.ShapeDtypeStruct((b, s, d), F32), jax.ShapeDtypeStruct((b * s, d // 2), jnp.uint32),
                   jax.ShapeDtypeStruct((b * s, SMALL_W), F32), jax.ShapeDtypeStruct((8, b * s), F32),
                   jax.ShapeDtypeStruct((8, SMALL_W), F32)],
        scratch_shapes=[pltpu.VMEM((1, SMALL_W), F32)],
        compiler_params=_cparams("arbitrary", "arbitrary"),
        name="post_mix",
    )(x3, ya, yb, yc, yd, gain.reshape(1, 4 * gw), w_out_bf16, g_mem.reshape(1, d), wq_bf16, kv3, wo_bf16,
      g_ffn.reshape(1, d), w_router, b_router)


def _experts_kernel(te_ref, nu_ref, run_ref, nxt_ref, xs_ref, wgu_hbm, wdn_hbm, y_ref,
                    wgu_buf, wdn_buf, wgu_bf, wdn_bf, sem, *, layer):
    i = pl.program_id(0)

    def weight_copies(expert, slot):
        return (pltpu.make_async_copy(wgu_hbm.at[layer, expert], wgu_buf.at[slot], sem.at[slot, 0]),
                pltpu.make_async_copy(wdn_hbm.at[layer, expert], wdn_buf.at[slot], sem.at[slot, 1]))

    @pl.when(i >= nu_ref[0])
    def _():
        y_ref[...] = jnp.zeros_like(y_ref)

    @pl.when(i < nu_ref[0])
    def _():
        expert = te_ref[i]
        slot = run_ref[i] % 2

        @pl.when(i == 0)
        def _():
            for c in weight_copies(expert, slot):
                c.start()

        @pl.when((i == 0) | (te_ref[jnp.maximum(i - 1, 0)] != expert))
        def _():
            for c in weight_copies(expert, slot):
                c.wait()
            nxt = nxt_ref[i]

            @pl.when(nxt >= 0)
            def _():
                for c in weight_copies(nxt, 1 - slot):
                    c.start()

            wgu_bf[...] = wgu_buf[slot].astype(BF16)
            wdn_bf[...] = wdn_buf[slot].astype(BF16)

        gu = _dot(_unpack_bf16_pairs(xs_ref[...]).astype(BF16), wgu_bf[...])
        act = _silu(gu[:, 0:D_EXPERT]) * gu[:, D_EXPERT:2 * D_EXPERT]
        y_ref[...] = _pack_bf16_pairs(_dot(act.astype(BF16), wdn_bf[...]))


def moe_experts(xs, tile_expert, n_used, run_index, next_expert, w_gu, w_dn, layer, tm):
    n_rows, dp = xs.shape
    d = 2 * dp
    return pl.pallas_call(
        functools.partial(_experts_kernel, layer=layer),
        grid_spec=pltpu.PrefetchScalarGridSpec(
            num_scalar_prefetch=4,
            grid=(n_rows // tm,),
            in_specs=[pl.BlockSpec((tm, dp), lambda i, te, nu, run, nxt: (jnp.minimum(i, nu[0] - 1), 0)),
                      pl.BlockSpec(memory_space=pl.ANY),
                      pl.BlockSpec(memory_space=pl.ANY)],
            out_specs=pl.BlockSpec((tm, dp), lambda i, te, nu, run, nxt: (i, 0)),
            scratch_shapes=[pltpu.VMEM((2, d, 2 * D_EXPERT), F32), pltpu.VMEM((2, D_EXPERT, d), F32),
                            pltpu.VMEM((d, 2 * D_EXPERT), BF16), pltpu.VMEM((D_EXPERT, d), BF16),
                            pltpu.SemaphoreType.DMA((2, 2))]),
        out_shape=jax.ShapeDtypeStruct((n_rows, dp), jnp.uint32),
        compiler_params=_cparams("arbitrary"),
        name="moe_experts",
    )(tile_expert, n_used, run_index, next_expert, xs, w_gu, w_dn)


def sc_gather_rows(table, idx):
    n_idx = idx.shape[0]
    width = table.shape[1]
    info = plsc.get_sparse_core_info()
    n_workers = info.num_cores * info.num_subcores
    per_worker = n_idx // n_workers
    chunk = SC_GATHER_CHUNK
    assert n_idx % (n_workers * chunk) == 0 and width % info.num_lanes == 0
    mesh = plsc.VectorSubcoreMesh(core_axis_name="c", subcore_axis_name="s")

    def body(table_hbm, idx_hbm, out_hbm, idx_v, rows_v, sem):
        wid = lax.axis_index("s") * info.num_cores + lax.axis_index("c")
        base = wid * per_worker

        @pl.loop(0, per_worker // chunk)
        def _(j):
            off = pl.multiple_of(base + j * chunk, chunk)
            pltpu.sync_copy(idx_hbm.at[pl.ds(off, chunk)], idx_v)
            pltpu.async_copy(table_hbm.at[idx_v], rows_v, sem).wait()
            pltpu.sync_copy(rows_v, out_hbm.at[pl.ds(off, chunk)])

    return pl.kernel(
        body,
        out_type=jax.ShapeDtypeStruct((n_idx, width), table.dtype),
        mesh=mesh,
        scratch_types=[pltpu.VMEM((chunk,), jnp.int32), pltpu.VMEM((chunk, width), table.dtype),
                       pltpu.SemaphoreType.DMA],
        name="sc_gather_rows",
    )(table, idx)


def sc_dispatch_rows(rows, dest, pad_idx, zero_rows):
    m, width = rows.shape
    n_pad = pad_idx.shape[0]
    info = plsc.get_sparse_core_info()
    n_workers = info.num_cores * info.num_subcores
    chunk = SC_GATHER_CHUNK
    assert m % (n_workers * chunk) == 0 and n_pad % (n_workers * chunk) == 0 and zero_rows.shape == (chunk, width)
    rows_per_worker = m // n_workers
    pad_per_worker = n_pad // n_workers
    mesh = plsc.VectorSubcoreMesh(core_axis_name="c", subcore_axis_name="s")

    def body(rows_hbm, dest_hbm, pad_hbm, zero_hbm, out_hbm, idx_v, rows_v, sem):
        wid = lax.axis_index("s") * info.num_cores + lax.axis_index("c")

        @pl.loop(0, rows_per_worker // chunk)
        def _(j):
            off = pl.multiple_of(wid * rows_per_worker + j * chunk, chunk)
            pltpu.sync_copy(rows_hbm.at[pl.ds(off, chunk)], rows_v)
            for slot in range(2):
                pltpu.sync_copy(dest_hbm.at[pl.ds(slot * m + off, chunk)], idx_v)
                pltpu.async_copy(rows_v, out_hbm.at[idx_v], sem).wait()

        pltpu.sync_copy(zero_hbm, rows_v)

        @pl.loop(0, pad_per_worker // chunk)
        def _(j):
            off = pl.multiple_of(wid * pad_per_worker + j * chunk, chunk)
            pltpu.sync_copy(pad_hbm.at[pl.ds(off, chunk)], idx_v)
            pltpu.async_copy(rows_v, out_hbm.at[idx_v], sem).wait()

    return pl.kernel(
        body,
        out_type=jax.ShapeDtypeStruct((2 * m + n_pad, width), rows.dtype),
        mesh=mesh,
        scratch_types=[pltpu.VMEM((chunk,), jnp.int32), pltpu.VMEM((chunk, width), rows.dtype),
                       pltpu.SemaphoreType.DMA],
        name="sc_dispatch_rows",
    )(rows, dest, pad_idx, zero_rows)


def _combine_rows_kernel(x_ref, r_ref, g_ref, y1_ref, y2_ref, o_ref, *, final_norm):
    r = r_ref[...]
    out = x_ref[...] + r[:, 2:3] * _unpack_bf16_pairs(y1_ref[...]) + r[:, 3:4] * _unpack_bf16_pairs(y2_ref[...])
    if final_norm:
        ms = jnp.mean(out * out, axis=-1, keepdims=True)
        out = (out * lax.rsqrt(ms + EPS)) * g_ref[...]
    o_ref[...] = out


def moe_combine_rows(x2, route, y_rows, g_final, tm, final_norm):
    m, d = x2.shape
    steps = m // tm
    return pl.pallas_call(
        functools.partial(_combine_rows_kernel, final_norm=final_norm),
        grid=(steps,),
        in_specs=[pl.BlockSpec((tm, d), lambda i: (i, 0)),
                  pl.BlockSpec((tm, SMALL_W), lambda i: (i, 0)),
                  pl.BlockSpec((1, d), lambda i: (0, 0)),
                  pl.BlockSpec((tm, d // 2), lambda i: (i, 0)),
                  pl.BlockSpec((tm, d // 2), lambda i: (i + steps, 0))],
        out_specs=pl.BlockSpec((tm, d), lambda i: (i, 0)),
        out_shape=jax.ShapeDtypeStruct((m, d), F32),
        compiler_params=_cparams("parallel"),
        name="moe_combine_rows",
    )(x2, route, g_final.reshape(1, d), y_rows, y_rows)


def moe_routed(x2, h, route, route_t, counts, w_gu, w_dn, layer, g_final, final_norm):
    m, d = x2.shape
    te_rows = EXPERT_TILE
    n_rows = 2 * m + N_EXPERTS * te_rows
    cnt = counts[0, 0:N_EXPERTS].astype(jnp.int32)
    padded = (cnt + te_rows - 1) // te_rows * te_rows
    seg_end = jnp.cumsum(padded)
    seg_start = seg_end - padded
    idx = route_t[0:2].astype(jnp.int32)
    onehot = idx[:, :, None] == jnp.arange(N_EXPERTS, dtype=jnp.int32)
    dest = (jnp.sum(jnp.where(onehot, seg_start, 0), axis=-1) + route_t[4:6].astype(jnp.int32)).reshape(2 * m)
    tile_start = jnp.arange(n_rows // te_rows, dtype=jnp.int32) * te_rows
    tile_expert = jnp.minimum(jnp.sum((tile_start[:, None] >= seg_end[None, :]).astype(jnp.int32), axis=1), N_EXPERTS - 1)
    n_used = (seg_end[N_EXPERTS - 1] // te_rows).reshape(1)
    n_pad = n_rows - 2 * m
    experts = jnp.arange(N_EXPERTS, dtype=jnp.int32)
    pad_end = jnp.cumsum(padded - cnt)
    k = jnp.arange(n_pad, dtype=jnp.int32)
    k_expert = jnp.sum((k[:, None] >= pad_end[None, :]).astype(jnp.int32), axis=1)
    first_pad = seg_start + cnt - (pad_end - (padded - cnt))
    in_segment = jnp.sum(jnp.where(k_expert[:, None] == experts[None, :], first_pad[None, :], 0), axis=1) + k
    pad_idx = jnp.where(k_expert < N_EXPERTS, in_segment, seg_end[N_EXPERTS - 1] + k - pad_end[N_EXPERTS - 1])
    xs = sc_dispatch_rows(h, dest, pad_idx, jnp.zeros((SC_GATHER_CHUNK, d // 2), h.dtype))
    n_tiles = n_rows // te_rows
    tile_ids = jnp.arange(n_tiles, dtype=jnp.int32)
    prev_expert = jnp.concatenate([jnp.full((1,), -1, jnp.int32), tile_expert[:-1]])
    run_index = jnp.maximum(jnp.cumsum(((tile_expert != prev_expert) & (tile_ids < n_used[0])).astype(jnp.int32)) - 1, 0)
    next_tile = jnp.take(seg_end, tile_expert) // te_rows
    next_expert = jnp.where(next_tile < n_used[0], jnp.take(tile_expert, jnp.minimum(next_tile, n_tiles - 1)), -1)
    ys = moe_experts(xs, tile_expert, n_used, run_index, next_expert, w_gu, w_dn, layer, tm=te_rows)
    y_rows = sc_gather_rows(ys, dest)
    return moe_combine_rows(x2, route, y_rows, g_final, tm=ROW_TILE, final_norm=final_norm)


def _w_in_prep_kernel(w_ref, o_ref):
    gw = GROUP_WIDTH
    a_end = 7 * gw
    c_start = a_end + HEADS
    c_end = c_start + 5 * gw
    z_start = c_end + 2 * HEADS
    assert a_end % SMALL_W == LANE_FOX and c_end % SMALL_W == LANE_BETA and LANE_GDEC == LANE_BETA + HEADS
    w = w_ref[...]
    o_ref[:, 0:a_end] = w[:, 0:a_end].astype(BF16)
    o_ref[:, a_end:a_end + 5 * gw] = w[:, c_start:c_end].astype(BF16)
    o_ref[:, 12 * gw:13 * gw] = w[:, z_start:z_start + gw].astype(BF16)
    blk_fox = w[:, a_end:a_end + SMALL_W]
    blk_gdn = w[:, c_end - LANE_BETA:c_end - LANE_BETA + SMALL_W]
    lane = _iota(blk_fox.shape, 1)
    small = jnp.where(lane < LANE_BETA, blk_fox, jnp.where(lane < LANE_GDEC + HEADS, blk_gdn, 0.0))
    o_ref[:, 13 * gw:13 * gw + SMALL_W] = small.astype(BF16)
    o_ref[:, 13 * gw + SMALL_W:] = jnp.zeros((w.shape[0], SMALL_W), BF16)


def w_in_prep(w_all, layer):
    _, d, n = w_all.shape
    tr = 128
    return pl.pallas_call(
        _w_in_prep_kernel,
        grid=(d // tr,),
        in_specs=[pl.BlockSpec((None, tr, n), lambda i: (layer, i, 0))],
        out_specs=pl.BlockSpec((tr, N_PROJ), lambda i: (i, 0)),
        out_shape=jax.ShapeDtypeStruct((d, N_PROJ), BF16),
        compiler_params=_cparams("parallel"),
        name="w_in_prep",
    )(w_all)


def _block_diag(w):
    h = w.shape[0]
    eye = jnp.eye(h, dtype=w.dtype)
    return (eye[:, None, :, None] * w[:, :, None, :]).reshape(h * HEAD_DIM, h * HEAD_DIM)


def kernel(x, mem, norm_mix, w_in, hgrn_lb, fox_fb, lru_conv_w, lru_conv_b, lru_wa, lru_ba, lru_wx, lru_bx, lru_lam, gdn_conv_w, gdn_a_log, gdn_dt_bias, mix_gain, w_out, norm_mem, norm_memkv, w_mq, w_mkv, w_mo, norm_ffn, w_rg, b_rg, w_re, b_re, w_e_gu, w_e_dn, norm_final):
    b, s, d = x.shape
    depth = w_in.shape[0]
    t_tok = b * s
    mlen = mem.shape[1]
    gw = GROUP_WIDTH

    lb_all = jnp.cumsum(jax.nn.softmax(hgrn_lb.astype(F32), axis=0), axis=0)
    lb_all = lb_all - lb_all[0]

    x = x.astype(F32)
    for l in range(depth):
        proj = norm_matmul(x.reshape(t_tok, d), norm_mix[l], w_in_prep(w_in, l), tm=ROW_TILE)
        proj3 = proj.reshape(b, s, N_PROJ)
        prm = jnp.zeros((8, SMALL_W), F32)
        prm = prm.at[0, LANE_FOX:LANE_FOX + HEADS].set(fox_fb[l].astype(F32))
        prm = prm.at[0, LANE_GDEC:LANE_GDEC + HEADS].set(gdn_dt_bias[l].astype(F32))
        prm = prm.at[1, LANE_GDEC:LANE_GDEC + HEADS].set(gdn_a_log[l].astype(F32))
        sm, c_exp = small_prep(proj3, prm)

        ya = hgrn2(proj3, lb_all[l])
        yb = fox_attention(proj3, c_exp)
        yc = rglru(proj3, lru_conv_w[l], lru_conv_b[l], _block_diag(lru_wa[l]).astype(BF16), lru_ba[l],
                   _block_diag(lru_wx[l]).astype(BF16), lru_bx[l], lru_lam[l])
        yd = gdn(proj3, sm, gdn_conv_w[l])

        kv = norm_matmul(mem.reshape(b * mlen, d), norm_memkv[l], w_mkv[l].astype(BF16), tm=256)
        w_router = jnp.concatenate([w_re[l], w_rg[l], jnp.zeros((d, SMALL_W - N_EXPERTS - N_EXPERT_GROUPS), F32)], axis=1)
        b_router = jnp.concatenate([b_re[l], b_rg[l], jnp.zeros((SMALL_W - N_EXPERTS - N_EXPERT_GROUPS,), F32)]).reshape(1, SMALL_W)
        x, h, route, route_t, counts = post_mix(x, ya, yb, yc, yd, mix_gain[l], w_out[l].astype(BF16), norm_mem[l],
                                                w_mq[l].astype(BF16), kv.reshape(b, mlen, 2 * gw), w_mo[l].astype(BF16),
                                                norm_ffn[l], w_router, b_router)

        x = moe_routed(x.reshape(t_tok, d), h, route, route_t, counts, w_e_gu, w_e_dn, l,
                       norm_final, final_norm=(l == depth - 1)).reshape(b, s, d)

    return x
```

```python
import functools
import math

import jax
import jax.numpy as jnp
from jax import lax
from jax.experimental import pallas as pl
from jax.experimental.pallas import tpu as pltpu
from jax.experimental.pallas import tpu_sc as plsc

F32 = jnp.float32
BF16 = jnp.bfloat16

HEAD_DIM = 64
GROUP_WIDTH = 256
HEADS = GROUP_WIDTH // HEAD_DIM
GDN_CHUNK = 64
GDN_TILE = 128
GDN_BATCH = 8
FOX_BATCH = 2
HGRN_BATCH = 4
HGRN_CHUNK = 16
CONV_W = 4
LRU_C = 8.0
EPS = 1e-6
N_EXPERT_GROUPS = 4
EXPERTS_PER_GROUP = 8
N_EXPERTS = N_EXPERT_GROUPS * EXPERTS_PER_GROUP
D_EXPERT = 256
SC_GATHER_CHUNK = 128
EXPERT_TILE = 512
SMALL_W = 128
SEQ_TILE = 256
ROW_TILE = 512
VMEM_LIMIT = 56 * 1024 * 1024

(COL_AQ, COL_AF, COL_AI, COL_AG, COL_BQ, COL_BK, COL_BV, COL_CX, COL_CG,
 COL_DQ, COL_DK, COL_DV, COL_DZ) = range(13)
N_PROJ = 13 * GROUP_WIDTH + 2 * SMALL_W
COL_SMALL = 13 * GROUP_WIDTH // SMALL_W
LANE_FOX = 0
LANE_BETA = 4
LANE_GDEC = 8


def _cparams(*sem):
    return pltpu.CompilerParams(dimension_semantics=sem, vmem_limit_bytes=VMEM_LIMIT)


def _dot(a, b):
    return jnp.dot(a, b, preferred_element_type=F32)


def _dot_nt(a, b):
    return lax.dot_general(a, b, (((1,), (1,)), ((), ())), preferred_element_type=F32)


def _dot_tn(a, b):
    return lax.dot_general(a, b, (((0,), (0,)), ((), ())), preferred_element_type=F32)


def _split3(x):
    h = x.astype(BF16)
    r = x - h.astype(F32)
    m = r.astype(BF16)
    l = (r - m.astype(F32)).astype(BF16)
    return h, m, l


def _dot_exact_rhs(x, w_bf16):
    h, m, l = _split3(x)
    return _dot(h, w_bf16) + _dot(m, w_bf16) + _dot(l, w_bf16)


def _dot_exact_lhs(w_bf16, x):
    h, m, l = _split3(x)
    return _dot(w_bf16, h) + _dot(w_bf16, m) + _dot(w_bf16, l)


def _iota(shape, dim):
    return lax.broadcasted_iota(jnp.int32, shape, dim)


def _head_ones(n=GROUP_WIDTH):
    r = _iota((n, n), 0) // HEAD_DIM
    c = _iota((n, n), 1) // HEAD_DIM
    return r == c


def _sigmoid(x):
    return 1.0 / (1.0 + jnp.exp(-x))


def _silu(x):
    return x * _sigmoid(x)


def _log_sigmoid(x):
    return jnp.minimum(x, 0.0) - jnp.log1p(jnp.exp(-jnp.abs(x)))


def _softplus(x):
    return jnp.maximum(x, 0.0) + jnp.log1p(jnp.exp(-jnp.abs(x)))


def _gelu_tanh(x):
    return 0.5 * x * (1.0 + jnp.tanh(math.sqrt(2.0 / math.pi) * (x + 0.044715 * (x * x * x))))


def _pack_bf16_pairs(x):
    n = x.shape[1] // 2
    u = lax.bitcast_convert_type(x.astype(BF16).astype(F32), jnp.uint32)
    return u[:, :n] | (u[:, n:] >> 16)


def _unpack_bf16_pairs(p):
    hi = lax.bitcast_convert_type(p & jnp.uint32(0xFFFF0000), F32)
    lo = lax.bitcast_convert_type(p << 16, F32)
    return jnp.concatenate([hi, lo], axis=1)


def _head_mean_sq(x, ones_bf16):
    return _dot_exact_rhs(x * x, ones_bf16) * (1.0 / HEAD_DIM)


def _head_rms(x, ones_bf16):
    return x * lax.rsqrt(_head_mean_sq(x, ones_bf16) + EPS)


def _stack_heads(x):
    lane_head = _iota(x.shape, 1) // HEAD_DIM
    parts = []
    for h in range(HEADS):
        parts.append(jnp.where(lane_head == h, x, 0.0))
    return jnp.concatenate(parts, axis=0)


def _causal_conv4(x, pad_ref, w):
    r = x.shape[0]
    pad_ref[8:8 + r, :] = x
    acc = x * w[CONV_W - 1:CONV_W, :]
    for k in range(1, CONV_W):
        acc = acc + pad_ref[8 - k:8 - k + r, :] * w[CONV_W - 1 - k:CONV_W - k, :]
    pad_ref[0:8, :] = x[r - 8:r, :]
    return acc


def _norm_matmul_kernel(x_ref, g_ref, w_ref, o_ref):
    x = x_ref[...]
    ms = jnp.mean(x * x, axis=-1, keepdims=True)
    h = (x * lax.rsqrt(ms + EPS)) * g_ref[...]
    o_ref[...] = _dot(h.astype(BF16), w_ref[...]).astype(o_ref.dtype)


def norm_matmul(x, g, w_bf16, tm, out_dtype=F32):
    m, d = x.shape
    n = w_bf16.shape[1]
    return pl.pallas_call(
        _norm_matmul_kernel,
        grid=(m // tm,),
        in_specs=[pl.BlockSpec((tm, d), lambda i: (i, 0)),
                  pl.BlockSpec((1, d), lambda i: (0, 0)),
                  pl.BlockSpec((d, n), lambda i: (0, 0))],
        out_specs=pl.BlockSpec((tm, n), lambda i: (i, 0)),
        out_shape=jax.ShapeDtypeStruct((m, n), out_dtype),
        compiler_params=_cparams("parallel"),
        name="norm_matmul",
    )(x, g.reshape(1, d), w_bf16)


def _small_prep_kernel(s_ref, p_ref, o_ref, x_ref):
    S = s_ref.shape[1]
    W = s_ref.shape[2]
    blk = SEQ_TILE
    lane = _iota((blk, W), 1)
    is_fox = (lane >= LANE_FOX) & (lane < LANE_FOX + HEADS)
    is_beta = (lane >= LANE_BETA) & (lane < LANE_BETA + HEADS)
    is_gdec = (lane >= LANE_GDEC) & (lane < LANE_GDEC + HEADS)
    r = _iota((blk, blk), 0)
    c = _iota((blk, blk), 1)
    tril_all = jnp.where(r >= c, 1.0, 0.0).astype(BF16)
    tril_chunk = jnp.where((r >= c) & (r // GDN_CHUNK == c // GDN_CHUNK), 1.0, 0.0).astype(BF16)
    neg_exp_a = -jnp.exp(p_ref[1:2, :])
    carry = jnp.zeros((1, W), F32)
    for i in range(S // blk):
        sl = pl.ds(i * blk, blk)
        z = s_ref[0, sl, :] + p_ref[0:1, :]
        fox = jnp.where(is_fox, _log_sigmoid(z), 0.0)
        beta = jnp.where(is_beta, _sigmoid(z), 0.0)
        gdec = jnp.where(is_gdec, neg_exp_a * _softplus(z), 0.0)
        cf = _dot_exact_lhs(tril_all, fox) + carry
        cg = _dot_exact_lhs(tril_chunk, gdec)
        carry = cf[blk - 1:blk, :]
        o_ref[0, sl, :] = cf + cg + beta
        both = cf + cg
        for j, src_lane in enumerate(range(LANE_FOX, LANE_FOX + HEADS)):
            x_ref[0, sl, j * W:(j + 1) * W] = jnp.broadcast_to(both[:, src_lane:src_lane + 1], (blk, W))


def small_prep(proj3, params):
    b, s, _ = proj3.shape
    w = SMALL_W
    return pl.pallas_call(
        _small_prep_kernel,
        grid=(b,),
        in_specs=[pl.BlockSpec((1, s, w), lambda i: (i, 0, COL_SMALL)),
                  pl.BlockSpec((8, w), lambda i: (0, 0))],
        out_specs=[pl.BlockSpec((1, s, w), lambda i: (i, 0, 0)),
                   pl.BlockSpec((1, s, HEADS * w), lambda i: (i, 0, 0))],
        out_shape=[jax.ShapeDtypeStruct((b, s, w), F32), jax.ShapeDtypeStruct((b, s, HEADS * w), F32)],
        compiler_params=_cparams("parallel"),
        name="small_prep",
    )(proj3, params)


def _hgrn2_kernel(q_ref, f_ref, i_ref, g_ref, lb_ref, o_ref, st_ref, qs, ks, vs, bs, os_):
    nb = q_ref.shape[0]
    T = q_ref.shape[1]
    C = HGRN_CHUNK
    seqs = range(nb)

    @pl.when(pl.program_id(1) == 0)
    def _():
        st_ref[...] = jnp.zeros_like(st_ref)

    same_head = _head_ones()
    ones_bf = jnp.where(same_head, 1.0, 0.0).astype(BF16)
    lb = lb_ref[...]
    log_lb = jnp.log(lb)
    r = _iota((T, T), 0)
    c = _iota((T, T), 1)
    tril_chunk = jnp.where((r >= c) & (r // C == c // C), 1.0, 0.0).astype(BF16)
    for bb in seqs:
        fl = f_ref[bb]
        e = jnp.exp(-jnp.abs(fl))
        c2 = jnp.log1p(-lb) + (jnp.minimum(fl, 0.0) - jnp.log1p(e))
        mx = jnp.maximum(log_lb, c2)
        log_f = mx + jnp.log1p(jnp.exp(-jnp.abs(log_lb - c2)))
        bs[bb] = _dot_exact_lhs(tril_chunk, log_f)
        qs[bb] = _silu(q_ref[bb])
        ks[bb] = (1.0 - lb) * (jnp.where(fl >= 0.0, e, 1.0) / (1.0 + e))
        vs[bb] = i_ref[bb]

    H8 = C // 2
    trow8 = _iota((H8, GROUP_WIDTH), 0)

    def chunk(ci, carry):
        r0 = pl.multiple_of(ci * C, C)
        qc = [qs[bb, pl.ds(r0, C), :] for bb in seqs]
        kc = [ks[bb, pl.ds(r0, C), :] for bb in seqs]
        vc = [vs[bb, pl.ds(r0, C), :] for bb in seqs]
        bc = [bs[bb, pl.ds(r0, C), :] for bb in seqs]
        st = [st_ref[bb] for bb in seqs]
        a = []
        for bb in seqs:
            parts = []
            for half, s_range in ((0, range(H8)), (1, range(C))):
                b_half = bc[bb][half * H8:(half + 1) * H8, :]
                q_half = qc[bb][half * H8:(half + 1) * H8, :]
                for s in s_range:
                    blk = jnp.exp(b_half - bc[bb][s:s + 1, :]) * (q_half * kc[bb][s:s + 1, :])
                    if s >= half * H8:
                        blk = jnp.where(trow8 >= s - half * H8, blk, 0.0)
                    parts.append(blk)
            a.append(jnp.concatenate(parts, axis=0).astype(BF16))
        sc = [_dot(a[bb], ones_bf) for bb in seqs]
        o = [_dot_nt((qc[bb] * jnp.exp(bc[bb])).astype(BF16), st[bb].astype(BF16)) for bb in seqs]
        b_last = [bc[bb][C - 1:C, :] for bb in seqs]
        upd = [_dot_tn(vc[bb].astype(BF16), (kc[bb] * jnp.exp(b_last[bb] - bc[bb])).astype(BF16)) for bb in seqs]
        for bb in seqs:
            top = o[bb][0:H8, :]
            bot = o[bb][H8:C, :]
            for s in range(H8):
                top = top + sc[bb][s * H8:(s + 1) * H8, :] * vc[bb][s:s + 1, :]
            for s in range(C):
                bot = bot + sc[bb][(H8 + s) * H8:(H8 + s + 1) * H8, :] * vc[bb][s:s + 1, :]
            os_[bb, pl.ds(r0, C), :] = jnp.concatenate([top, bot], axis=0)
            st_ref[bb] = st[bb] * jnp.exp(b_last[bb]) + jnp.where(same_head, upd[bb], 0.0)
        return carry

    lax.fori_loop(0, T // C, chunk, 0)
    for bb in seqs:
        o_ref[bb] = _head_rms(os_[bb], ones_bf) * _silu(g_ref[bb])


def hgrn2(proj3, lb):
    b, s, _ = proj3.shape
    t = SEQ_TILE
    gw = GROUP_WIDTH
    nb = HGRN_BATCH
    col = lambda cidx: pl.BlockSpec((nb, t, gw), lambda i, j: (i, j, cidx))
    return pl.pallas_call(
        _hgrn2_kernel,
        grid=(b // nb, s // t),
        in_specs=[col(COL_AQ), col(COL_AF), col(COL_AI), col(COL_AG),
                  pl.BlockSpec((1, gw), lambda i, j: (0, 0))],
        out_specs=pl.BlockSpec((nb, t, gw), lambda i, j: (i, j, 0)),
        out_shape=jax.ShapeDtypeStruct((b, s, gw), F32),
        scratch_shapes=[pltpu.VMEM((nb, gw, gw), F32)] + [pltpu.VMEM((nb, t, gw), F32)] * 5,
        compiler_params=_cparams("parallel", "arbitrary"),
        name="hgrn2",
    )(proj3, proj3, proj3, proj3, lb.reshape(1, gw))


def _fox_kernel(q_ref, k_ref, v_ref, cx_ref, o_ref, kb_ref, vt_ref, acc_ref):
    nb = q_ref.shape[0]
    tq = q_ref.shape[1]
    tk = tq
    S = k_ref.shape[1]
    qi = pl.program_id(1)
    seqs = range(nb)

    @pl.when(qi == 0)
    def _():
        for bb in seqs:
            for i in range(S // tk):
                sl = pl.ds(i * tk, tk)
                kb_ref[bb, sl, :] = k_ref[bb, sl, :].astype(BF16)
                vt_ref[bb, :, sl] = v_ref[bb, sl, :].T.astype(BF16)

    lane_head = _iota((tq, GROUP_WIDTH), 1) // HEAD_DIM
    qh = []
    for bb in seqs:
        qsc = q_ref[bb] * (HEAD_DIM ** -0.5)
        qh.append([jnp.where(lane_head == h, qsc, 0.0).astype(BF16) for h in range(HEADS)])
    acc_ref[...] = jnp.zeros_like(acc_ref)
    reps = tq // SMALL_W
    causal = _iota((tk, tq), 0) <= _iota((tk, tq), 1)

    def step(kb, stats, masked):
        k0 = pl.multiple_of(kb * tk, tk)
        new_stats = [[None] * HEADS for _ in seqs]
        strips = [(h, bb) for h in range(HEADS) for bb in seqs]

        def scores(h, bb):
            return _dot_nt(kb_ref[bb, pl.ds(k0, tk), :], qh[bb][h])

        ahead = 4
        queue = [scores(*strips[j]) for j in range(ahead)]
        for i, (h, bb) in enumerate(strips):
            st = queue.pop(0)
            if i + ahead < len(strips):
                queue.append(scores(*strips[i + ahead]))
            rows = slice(h * HEAD_DIM, (h + 1) * HEAD_DIM)
            cx = cx_ref[bb, pl.ds(k0, tk), h * SMALL_W:(h + 1) * SMALL_W]
            st = st - jnp.concatenate([cx] * reps, axis=1)
            if masked:
                st = jnp.where(causal, st, -jnp.inf)
            m_old, l_old = stats[bb][h]
            m_new = jnp.maximum(m_old, jnp.max(st, axis=0, keepdims=True))
            alpha = jnp.exp(m_old - m_new)
            p = jnp.exp(st - m_new)
            l_new = alpha * l_old + jnp.sum(p, axis=0, keepdims=True)
            pv = _dot(vt_ref[bb, rows, pl.ds(k0, tk)], p.astype(BF16))
            acc_ref[bb, rows, :] = alpha * acc_ref[bb, rows, :] + pv
            new_stats[bb][h] = (m_new, l_new)
        return tuple(tuple(new_stats[bb]) for bb in seqs)

    init = tuple(tuple((jnp.full((1, tq), -jnp.inf, F32), jnp.zeros((1, tq), F32)) for _ in range(HEADS)) for _ in seqs)
    stats = lax.fori_loop(0, qi, lambda kb, c: step(kb, c, False), init)
    stats = step(qi, stats, True)

    ones_bf = jnp.where(_head_ones(), 1.0, 0.0).astype(BF16)
    for bb in seqs:
        inv_l = jnp.concatenate([jnp.broadcast_to(1.0 / stats[bb][h][1], (HEAD_DIM, tq)) for h in range(HEADS)], axis=0)
        o_ref[bb] = _head_rms((acc_ref[bb] * inv_l).T, ones_bf)


def fox_attention(proj3, c_exp):
    b, s, _ = proj3.shape
    tq = SEQ_TILE
    gw = GROUP_WIDTH
    nb = FOX_BATCH
    return pl.pallas_call(
        _fox_kernel,
        grid=(b // nb, s // tq),
        in_specs=[pl.BlockSpec((nb, tq, gw), lambda i, j: (i, j, COL_BQ)),
                  pl.BlockSpec((nb, s, gw), lambda i, j: (i, 0, COL_BK)),
                  pl.BlockSpec((nb, s, gw), lambda i, j: (i, 0, COL_BV)),
                  pl.BlockSpec((nb, s, HEADS * SMALL_W), lambda i, j: (i, 0, 0))],
        out_specs=pl.BlockSpec((nb, tq, gw), lambda i, j: (i, j, 0)),
        out_shape=jax.ShapeDtypeStruct((b, s, gw), F32),
        scratch_shapes=[pltpu.VMEM((nb, s, gw), BF16),
                        pltpu.VMEM((nb, gw, s), BF16),
                        pltpu.VMEM((nb, gw, tq), F32)],
        compiler_params=_cparams("parallel", "arbitrary"),
        name="fox_attention",
    )(proj3, proj3, proj3, c_exp)


def _rglru_kernel(x_ref, g_ref, cw_ref, cb_ref, wa_ref, ba_ref, wx_ref, bx_ref, lam_ref, o_ref, prev_ref, h_ref):
    @pl.when(pl.program_id(1) == 0)
    def _():
        prev_ref[0:8, :] = jnp.zeros((8, prev_ref.shape[1]), F32)
        h_ref[...] = jnp.zeros_like(h_ref)

    x = x_ref[0]
    t = x.shape[0]
    xc = _causal_conv4(x, prev_ref, cw_ref[...]) + cb_ref[...]
    xb = xc.astype(BF16)
    r = _sigmoid(_dot(xb, wa_ref[...]) + ba_ref[...])
    ig = _sigmoid(_dot(xb, wx_ref[...]) + bx_ref[...])
    log_a = (-LRU_C * r) * _softplus(-lam_ref[...])
    a = jnp.exp(log_a)
    u = jnp.sqrt(1.0 - jnp.exp(2.0 * log_a)) * (ig * xc)
    row = _iota(a.shape, 0)
    d = 1
    while d < t:
        valid = row >= d
        u = jnp.where(valid, a * pltpu.roll(u, d, 0) + u, u)
        a = jnp.where(valid, a * pltpu.roll(a, d, 0), a)
        d *= 2
    h = a * h_ref[...] + u
    h_ref[...] = h[t - 1:t, :]
    ones_bf = jnp.where(_head_ones(), 1.0, 0.0).astype(BF16)
    o_ref[0] = _head_rms(h * _gelu_tanh(g_ref[0]), ones_bf)


def rglru(proj3, conv_w, conv_b, wa_bd, ba, wx_bd, bx, lam):
    b, s, _ = proj3.shape
    t = ROW_TILE
    gw = GROUP_WIDTH
    row = lambda: pl.BlockSpec((1, gw), lambda i, j: (0, 0))
    return pl.pallas_call(
        _rglru_kernel,
        grid=(b, s // t),
        in_specs=[pl.BlockSpec((1, t, gw), lambda i, j: (i, j, COL_CX)),
                  pl.BlockSpec((1, t, gw), lambda i, j: (i, j, COL_CG)),
                  pl.BlockSpec((CONV_W, gw), lambda i, j: (0, 0)), row(),
                  pl.BlockSpec((gw, gw), lambda i, j: (0, 0)), row(),
                  pl.BlockSpec((gw, gw), lambda i, j: (0, 0)), row(), row()],
        out_specs=pl.BlockSpec((1, t, gw), lambda i, j: (i, j, 0)),
        out_shape=jax.ShapeDtypeStruct((b, s, gw), F32),
        scratch_shapes=[pltpu.VMEM((8 + t, gw), F32), pltpu.VMEM((1, gw), F32)],
        compiler_params=_cparams("parallel", "arbitrary"),
        name="rglru",
    )(proj3, proj3, conv_w, conv_b.reshape(1, gw), wa_bd, ba.reshape(1, gw), wx_bd, bx.reshape(1, gw), lam.reshape(1, gw))


def _gdn_kernel(q_ref, k_ref, v_ref, z_ref, sm_ref, cw_ref, o_ref, st_ref, pq_ref, pk_ref, pv_ref):
    nb = q_ref.shape[0]
    T = q_ref.shape[1]
    C = GDN_CHUNK

    @pl.when(pl.program_id(1) == 0)
    def _():
        st_ref[...] = jnp.zeros_like(st_ref)
        for pad_ref in (pq_ref, pk_ref, pv_ref):
            pad_ref[:, 0:8, :] = jnp.zeros((nb, 8, pad_ref.shape[2]), F32)

    same_head = _head_ones()
    ones_bf = jnp.where(same_head, 1.0, 0.0).astype(BF16)
    cw = cw_ref[...]
    gw = GROUP_WIDTH
    er = _iota((SMALL_W, gw), 0)
    ec = _iota((SMALL_W, gw), 1) // HEAD_DIM
    exp_beta = jnp.where(er == LANE_BETA + ec, 1.0, 0.0).astype(BF16)
    exp_g = jnp.where(er == LANE_GDEC + ec, 1.0, 0.0).astype(BF16)

    prep = []
    for bb in range(nb):
        xq, xk, xv = q_ref[bb], k_ref[bb], v_ref[bb]
        q = _silu(_causal_conv4(xq, pq_ref.at[bb], cw[:, 0:gw]))
        k = _silu(_causal_conv4(xk, pk_ref.at[bb], cw[:, gw:2 * gw]))
        v = _silu(_causal_conv4(xv, pv_ref.at[bb], cw[:, 2 * gw:3 * gw]))
        q = q * lax.rsqrt(_dot_exact_rhs(q * q, ones_bf) + EPS) * (HEAD_DIM ** -0.5)
        k = k * lax.rsqrt(_dot_exact_rhs(k * k, ones_bf) + EPS)
        sm = sm_ref[bb]
        betax = _dot_exact_rhs(sm, exp_beta)
        gx = _dot_exact_rhs(sm, exp_g)
        prep.append((q, k, v, betax, gx))

    n_chunks = T // C
    trow = _iota((C, gw), 0)
    scol = _iota((C, gw), 1) % HEAD_DIM
    incl = scol <= trow
    strict = scol < trow
    eye = jnp.where(scol == trow, 1.0, 0.0)

    def bdiag(x_cat):
        return _stack_heads(x_cat).astype(BF16)

    chains = []
    for ci in range(n_chunks):
        sl = slice(ci * C, (ci + 1) * C)
        for bb in range(nb):
            q, k, v, betax, gx = prep[bb]
            qc, kc, vc, bx, gc = q[sl], k[sl], v[sl], betax[sl], gx[sl]
            grow = jnp.sum(jnp.where(scol == trow, gc, 0.0), axis=0, keepdims=True)
            eg = jnp.exp(gc)
            kb = kc * bx
            g_last = gc[C - 1:C, :]
            gamma = jnp.where(incl, jnp.exp(jnp.where(incl, gc - grow, 0.0)), 0.0)
            ks_bf = bdiag(kc)
            m = jnp.where(strict, _dot_nt(kb.astype(BF16), ks_bf) * gamma, 0.0)
            chains.append(dict(
                x=eye - m, y=m,
                a_qk=(_dot_nt(qc.astype(BF16), ks_bf) * gamma).astype(BF16),
                rhs_u=bdiag(vc * bx),
                rhs_w=bdiag(kb * eg),
                q_dec=(qc * eg).astype(BF16),
                k_dec=(kc * jnp.exp(g_last - gc)).astype(BF16),
                decay=jnp.exp(g_last)))
    for ch in chains:
        ch["ybd"] = bdiag(ch["y"])
    for _ in range(5):
        for ch in chains:
            ch["y"] = _dot(ch["y"].astype(BF16), ch["ybd"])
            ch["ybd"] = bdiag(ch["y"])
        for ch in chains:
            ch["x"] = ch["x"] + _dot(ch["x"].astype(BF16), ch["ybd"])
    for ch in chains:
        t_inv = ch["x"].astype(BF16)
        ch["u"] = _dot(t_inv, ch["rhs_u"])
        ch["wk"] = _dot(t_inv, ch["rhs_w"]).astype(BF16)

    outs = [[] for _ in range(nb)]
    seqs = range(nb)
    for ci in range(n_chunks):
        chs = [chains[ci * nb + bb] for bb in seqs]
        st = [st_ref[bb] for bb in seqs]
        st_bf = [st[bb].astype(BF16) for bb in seqs]
        v_new = [chs[bb]["u"] - _dot(chs[bb]["wk"], st_bf[bb]) for bb in seqs]
        o_st = [_dot(chs[bb]["q_dec"], st_bf[bb]) for bb in seqs]
        upd = [_dot_tn(chs[bb]["k_dec"], v_new[bb].astype(BF16)) for bb in seqs]
        for bb in seqs:
            outs[bb].append(o_st[bb] + _dot(chs[bb]["a_qk"], bdiag(v_new[bb])))
            st_ref[bb] = st[bb] * chs[bb]["decay"] + jnp.where(same_head, upd[bb], 0.0)
    for bb in range(nb):
        o = jnp.concatenate(outs[bb], axis=0)
        o_ref[bb] = _head_rms(o, ones_bf) * _silu(z_ref[bb])


def gdn(proj3, sm, conv_w):
    b, s, _ = proj3.shape
    t = GDN_TILE
    gw = GROUP_WIDTH
    nb = GDN_BATCH
    col = lambda cidx: pl.BlockSpec((nb, t, gw), lambda i, j: (i, j, cidx))
    return pl.pallas_call(
        _gdn_kernel,
        grid=(b // nb, s // t),
        in_specs=[col(COL_DQ), col(COL_DK), col(COL_DV), col(COL_DZ),
                  pl.BlockSpec((nb, t, SMALL_W), lambda i, j: (i, j, 0)),
                  pl.BlockSpec((CONV_W, 3 * gw), lambda i, j: (0, 0))],
        out_specs=pl.BlockSpec((nb, t, gw), lambda i, j: (i, j, 0)),
        out_shape=jax.ShapeDtypeStruct((b, s, gw), F32),
        scratch_shapes=[pltpu.VMEM((nb, gw, gw), F32)] + [pltpu.VMEM((nb, 8 + t, gw), F32)] * 3,
        compiler_params=_cparams("parallel", "arbitrary"),
        name="gdn",
    )(proj3, proj3, proj3, proj3, sm, conv_w)


def _mix_out(x, ys, gain, w_ref):
    gw = GROUP_WIDTH
    acc = x
    for i, y in enumerate(ys):
        yg = (y * gain[:, i * gw:(i + 1) * gw]).astype(BF16)
        acc = acc + _dot(yg, w_ref[i * gw:(i + 1) * gw, :])
    return acc


def _mem_attention(x, g, wq_ref, kv, wo_ref):
    t = x.shape[0]
    gw = GROUP_WIDTH
    ms = jnp.mean(x * x, axis=-1, keepdims=True)
    h = ((x * lax.rsqrt(ms + EPS)) * g).astype(BF16)
    q = _dot(h, wq_ref[...])
    k = kv[:, 0:gw].astype(BF16)
    v = kv[:, gw:2 * gw].astype(BF16)
    s = _dot_nt(k, _stack_heads(q).astype(BF16)) * (HEAD_DIM ** -0.5)
    s = s - jnp.max(s, axis=0, keepdims=True)
    p = jnp.exp(s)
    p = p * (1.0 / jnp.sum(p, axis=0, keepdims=True))
    ot = _dot_tn(v, p.astype(BF16))
    oc = jnp.concatenate([ot[hh * HEAD_DIM:(hh + 1) * HEAD_DIM, hh * t:(hh + 1) * t] for hh in range(HEADS)], axis=0).T
    return x + _dot(oc.astype(BF16), wo_ref[...])


def _post_mix_kernel(x_ref, ya_ref, yb_ref, yc_ref, yd_ref, gain_ref, wout_ref, gm_ref, wq_ref, kv_ref, wo_ref,
                     g_ref, w_ref, b_ref, x2_ref, h_ref, r_ref, rt_ref, n_ref, cnt_ref):
    @pl.when((pl.program_id(0) == 0) & (pl.program_id(1) == 0))
    def _():
        cnt_ref[...] = jnp.zeros_like(cnt_ref)

    x1 = _mix_out(x_ref[0], (ya_ref[0], yb_ref[0], yc_ref[0], yd_ref[0]), gain_ref[...], wout_ref)
    x = _mem_attention(x1, gm_ref[...], wq_ref, kv_ref[0], wo_ref)
    x2_ref[0] = x
    tm = x.shape[0]
    ms = jnp.mean(x * x, axis=-1, keepdims=True)
    h = (x * lax.rsqrt(ms + EPS)) * g_ref[...]
    h_ref[...] = _pack_bf16_pairs(h)
    hh, hm, _ = _split3(h)
    w = w_ref[...]
    wh, wm, _ = _split3(w)
    logits = _dot(hh, wh) + _dot(hh, wm) + _dot(hm, wh) + b_ref[...]
    lane_i = _iota(logits.shape, 1)
    lane = lane_i.astype(F32)
    lane_grp = (lane_i // EXPERTS_PER_GROUP).astype(F32)
    big = jnp.float32(1e9)
    neg = -jnp.inf
    is_grp = (lane_i >= N_EXPERTS) & (lane_i < N_EXPERTS + N_EXPERT_GROUPS)
    gl = jnp.where(is_grp, logits, neg)
    gmax = jnp.max(gl, axis=-1, keepdims=True)
    p_grp = 1.0 / jnp.sum(jnp.exp(gl - gmax), axis=-1, keepdims=True)
    g_sel = jnp.min(jnp.where(gl == gmax, lane, big), axis=-1, keepdims=True) - N_EXPERTS
    in_grp = (lane_i < N_EXPERTS) & (lane_grp == g_sel)
    el = jnp.where(in_grp, logits, neg)
    m1 = jnp.max(el, axis=-1, keepdims=True)
    i1 = jnp.min(jnp.where(el == m1, lane, big), axis=-1, keepdims=True)
    el2 = jnp.where(lane == i1, neg, el)
    m2 = jnp.max(el2, axis=-1, keepdims=True)
    i2 = jnp.min(jnp.where(el2 == m2, lane, big), axis=-1, keepdims=True)
    e21 = jnp.exp(m2 - m1)
    w1 = 1.0 / (1.0 + e21)
    w2 = e21 / (1.0 + e21)
    oh1 = lane == i1
    oh2 = lane == i2
    onehot = jnp.where(oh1 | oh2, 1.0, 0.0)
    strict = jnp.where(_iota((tm, tm), 0) > _iota((tm, tm), 1), 1.0, 0.0).astype(BF16)
    before = _dot(strict, onehot.astype(BF16)) + cnt_ref[...]
    rank1 = jnp.sum(jnp.where(oh1, before, 0.0), axis=-1, keepdims=True)
    rank2 = jnp.sum(jnp.where(oh2, before, 0.0), axis=-1, keepdims=True)
    cnt = before[tm - 1:tm, :] + onehot[tm - 1:tm, :]
    cnt_ref[...] = cnt
    n_ref[...] = jnp.broadcast_to(cnt, n_ref.shape)
    cols = (i1, i2, p_grp * w1, p_grp * w2, rank1, rank2)
    route = jnp.zeros(logits.shape, F32)
    for j, col in enumerate(cols):
        route = jnp.where(lane_i == j, col, route)
    r_ref[...] = route
    rt_ref[...] = route.T[0:8, :]


def post_mix(x3, ya, yb, yc, yd, gain, w_out_bf16, g_mem, wq_bf16, kv3, wo_bf16, g_ffn, w_router, b_router):
    b, s, d = x3.shape
    t = ROW_TILE
    gw = GROUP_WIDTH
    nt = s // t
    mlen = kv3.shape[1]
    tile = lambda w: pl.BlockSpec((1, t, w), lambda i, j: (i, j, 0))
    const = lambda shape: pl.BlockSpec(shape, lambda i, j: (0,) * len(shape))
    tok = lambda w: pl.BlockSpec((t, w), lambda i, j: (i * nt + j, 0))
    return pl.pallas_call(
        _post_mix_kernel,
        grid=(b, nt),
        in_specs=[tile(d), tile(gw), tile(gw), tile(gw), tile(gw),
                  const((1, 4 * gw)), const((4 * gw, d)),
                  const((1, d)), const((d, gw)), pl.BlockSpec((1, mlen, 2 * gw), lambda i, j: (i, 0, 0)), const((gw, d)),
                  const((1, d)), const((d, SMALL_W)), const((1, SMALL_W))],
        out_specs=[tile(d), tok(d // 2), tok(SMALL_W), pl.BlockSpec((8, t), lambda i, j: (0, i * nt + j)),
                   const((8, SMALL_W))],
        out_shape=[jax.ShapeDtypeStruct((b, s, d), F32), jax.ShapeDtypeStruct((b * s, d // 2), jnp.uint32),
                   jax.ShapeDtypeStruct((b * s, SMALL_W), F32), jax.ShapeDtypeStruct((8, b * s), F32),
                   jax.ShapeDtypeStruct((8, SMALL_W), F32)],
        scratch_shapes=[pltpu.VMEM((1, SMALL_W), F32)],
        compiler_params=_cparams("arbitrary", "arbitrary"),
        name="post_mix",
    )(x3, ya, yb, yc, yd, gain.reshape(1, 4 * gw), w_out_bf16, g_mem.reshape(1, d), wq_bf16, kv3, wo_bf16,
      g_ffn.reshape(1, d), w_router, b_router)


def _experts_kernel(te_ref, nu_ref, run_ref, nxt_ref, xs_ref, wgu_hbm, wdn_hbm, y_ref,
                    wgu_buf, wdn_buf, wgu_bf, wdn_bf, sem, *, layer):
    i = pl.program_id(0)

    def weight_copies(expert, slot):
        return (pltpu.make_async_copy(wgu_hbm.at[layer, expert], wgu_buf.at[slot], sem.at[slot, 0]),
                pltpu.make_async_copy(wdn_hbm.at[layer, expert], wdn_buf.at[slot], sem.at[slot, 1]))

    @pl.when(i >= nu_ref[0])
    def _():
        y_ref[...] = jnp.zeros_like(y_ref)

    @pl.when(i < nu_ref[0])
    def _():
        expert = te_ref[i]
        slot = run_ref[i] % 2

        @pl.when(i == 0)
        def _():
            for c in weight_copies(expert, slot):
                c.start()

        @pl.when((i == 0) | (te_ref[jnp.maximum(i - 1, 0)] != expert))
        def _():
            for c in weight_copies(expert, slot):
                c.wait()
            nxt = nxt_ref[i]

            @pl.when(nxt >= 0)
            def _():
                for c in weight_copies(nxt, 1 - slot):
                    c.start()

            wgu_bf[...] = wgu_buf[slot].astype(BF16)
            wdn_bf[...] = wdn_buf[slot].astype(BF16)

        gu = _dot(_unpack_bf16_pairs(xs_ref[...]).astype(BF16), wgu_bf[...])
        act = _silu(gu[:, 0:D_EXPERT]) * gu[:, D_EXPERT:2 * D_EXPERT]
        y_ref[...] = _pack_bf16_pairs(_dot(act.astype(BF16), wdn_bf[...]))


def moe_experts(xs, tile_expert, n_used, run_index, next_expert, w_gu, w_dn, layer, tm):
    n_rows, dp = xs.shape
    d = 2 * dp
    return pl.pallas_call(
        functools.partial(_experts_kernel, layer=layer),
        grid_spec=pltpu.PrefetchScalarGridSpec(
            num_scalar_prefetch=4,
            grid=(n_rows // tm,),
            in_specs=[pl.BlockSpec((tm, dp), lambda i, te, nu, run, nxt: (jnp.minimum(i, nu[0] - 1), 0)),
                      pl.BlockSpec(memory_space=pl.ANY),
                      pl.BlockSpec(memory_space=pl.ANY)],
            out_specs=pl.BlockSpec((tm, dp), lambda i, te, nu, run, nxt: (i, 0)),
            scratch_shapes=[pltpu.VMEM((2, d, 2 * D_EXPERT), F32), pltpu.VMEM((2, D_EXPERT, d), F32),
                            pltpu.VMEM((d, 2 * D_EXPERT), BF16), pltpu.VMEM((D_EXPERT, d), BF16),
                            pltpu.SemaphoreType.DMA((2, 2))]),
        out_shape=jax.ShapeDtypeStruct((n_rows, dp), jnp.uint32),
        compiler_params=_cparams("arbitrary"),
        name="moe_experts",
    )(tile_expert, n_used, run_index, next_expert, xs, w_gu, w_dn)


def sc_gather_rows(table, idx):
    n_idx = idx.shape[0]
    width = table.shape[1]
    info = plsc.get_sparse_core_info()
    n_workers = info.num_cores * info.num_subcores
    per_worker = n_idx // n_workers
    chunk = SC_GATHER_CHUNK
    assert n_idx % (n_workers * chunk) == 0 and width % info.num_lanes == 0
    mesh = plsc.VectorSubcoreMesh(core_axis_name="c", subcore_axis_name="s")

    def body(table_hbm, idx_hbm, out_hbm, idx_v, rows_v, sem):
        wid = lax.axis_index("s") * info.num_cores + lax.axis_index("c")
        base = wid * per_worker

        @pl.loop(0, per_worker // chunk)
        def _(j):
            off = pl.multiple_of(base + j * chunk, chunk)
            pltpu.sync_copy(idx_hbm.at[pl.ds(off, chunk)], idx_v)
            pltpu.async_copy(table_hbm.at[idx_v], rows_v, sem).wait()
            pltpu.sync_copy(rows_v, out_hbm.at[pl.ds(off, chunk)])

    return pl.kernel(
        body,
        out_type=jax.ShapeDtypeStruct((n_idx, width), table.dtype),
        mesh=mesh,
        scratch_types=[pltpu.VMEM((chunk,), jnp.int32), pltpu.VMEM((chunk, width), table.dtype),
                       pltpu.SemaphoreType.DMA],
        name="sc_gather_rows",
    )(table, idx)


def sc_dispatch_rows(rows, dest, pad_idx, zero_rows):
    m, width = rows.shape
    n_pad = pad_idx.shape[0]
    info = plsc.get_sparse_core_info()
    n_workers = info.num_cores * info.num_subcores
    chunk = SC_GATHER_CHUNK
    assert m % (n_workers * chunk) == 0 and n_pad % (n_workers * chunk) == 0 and zero_rows.shape == (chunk, width)
    rows_per_worker = m // n_workers
    pad_per_worker = n_pad // n_workers
    mesh = plsc.VectorSubcoreMesh(core_axis_name="c", subcore_axis_name="s")

    def body(rows_hbm, dest_hbm, pad_hbm, zero_hbm, out_hbm, idx_v, rows_v, sem):
        wid = lax.axis_index("s") * info.num_cores + lax.axis_index("c")

        @pl.loop(0, rows_per_worker // chunk)
        def _(j):
            off = pl.multiple_of(wid * rows_per_worker + j * chunk, chunk)
            pltpu.sync_copy(rows_hbm.at[pl.ds(off, chunk)], rows_v)
            for slot in range(2):
                pltpu.sync_copy(dest_hbm.at[pl.ds(slot * m + off, chunk)], idx_v)
                pltpu.async_copy(rows_v, out_hbm.at[idx_v], sem).wait()

        pltpu.sync_copy(zero_hbm, rows_v)

        @pl.loop(0, pad_per_worker // chunk)
        def _(j):
            off = pl.multiple_of(wid * pad_per_worker + j * chunk, chunk)
            pltpu.sync_copy(pad_hbm.at[pl.ds(off, chunk)], idx_v)
            pltpu.async_copy(rows_v, out_hbm.at[idx_v], sem).wait()

    return pl.kernel(
        body,
        out_type=jax.ShapeDtypeStruct((2 * m + n_pad, width), rows.dtype),
        mesh=mesh,
        scratch_types=[pltpu.VMEM((chunk,), jnp.int32), pltpu.VMEM((chunk, width), rows.dtype),
                       pltpu.SemaphoreType.DMA],
        name="sc_dispatch_rows",
    )(rows, dest, pad_idx, zero_rows)


def _combine_rows_kernel(x_ref, r_ref, g_ref, y1_ref, y2_ref, o_ref, *, final_norm):
    r = r_ref[...]
    out = x_ref[...] + r[:, 2:3] * _unpack_bf16_pairs(y1_ref[...]) + r[:, 3:4] * _unpack_bf16_pairs(y2_ref[...])
    if final_norm:
        ms = jnp.mean(out * out, axis=-1, keepdims=True)
        out = (out * lax.rsqrt(ms + EPS)) * g_ref[...]
    o_ref[...] = out


def moe_combine_rows(x2, route, y_rows, g_final, tm, final_norm):
    m, d = x2.shape
    steps = m // tm
    return pl.pallas_call(
        functools.partial(_combine_rows_kernel, final_norm=final_norm),
        grid=(steps,),
        in_specs=[pl.BlockSpec((tm, d), lambda i: (i, 0)),
                  pl.BlockSpec((tm, SMALL_W), lambda i: (i, 0)),
                  pl.BlockSpec((1, d), lambda i: (0, 0)),
                  pl.BlockSpec((tm, d // 2), lambda i: (i, 0)),
                  pl.BlockSpec((tm, d // 2), lambda i: (i + steps, 0))],
        out_specs=pl.BlockSpec((tm, d), lambda i: (i, 0)),
        out_shape=jax.ShapeDtypeStruct((m, d), F32),
        compiler_params=_cparams("parallel"),
        name="moe_combine_rows",
    )(x2, route, g_final.reshape(1, d), y_rows, y_rows)


def moe_routed(x2, h, route, route_t, counts, w_gu, w_dn, layer, g_final, final_norm):
    m, d = x2.shape
    te_rows = EXPERT_TILE
    n_rows = 2 * m + N_EXPERTS * te_rows
    cnt = counts[0, 0:N_EXPERTS].astype(jnp.int32)
    padded = (cnt + te_rows - 1) // te_rows * te_rows
    seg_end = jnp.cumsum(padded)
    seg_start = seg_end - padded
    idx = route_t[0:2].astype(jnp.int32)
    onehot = idx[:, :, None] == jnp.arange(N_EXPERTS, dtype=jnp.int32)
    dest = (jnp.sum(jnp.where(onehot, seg_start, 0), axis=-1) + route_t[4:6].astype(jnp.int32)).reshape(2 * m)
    tile_start = jnp.arange(n_rows // te_rows, dtype=jnp.int32) * te_rows
    tile_expert = jnp.minimum(jnp.sum((tile_start[:, None] >= seg_end[None, :]).astype(jnp.int32), axis=1), N_EXPERTS - 1)
    n_used = (seg_end[N_EXPERTS - 1] // te_rows).reshape(1)
    n_pad = n_rows - 2 * m
    experts = jnp.arange(N_EXPERTS, dtype=jnp.int32)
    pad_end = jnp.cumsum(padded - cnt)
    k = jnp.arange(n_pad, dtype=jnp.int32)
    k_expert = jnp.sum((k[:, None] >= pad_end[None, :]).astype(jnp.int32), axis=1)
    first_pad = seg_start + cnt - (pad_end - (padded - cnt))
    in_segment = jnp.sum(jnp.where(k_expert[:, None] == experts[None, :], first_pad[None, :], 0), axis=1) + k
    pad_idx = jnp.where(k_expert < N_EXPERTS, in_segment, seg_end[N_EXPERTS - 1] + k - pad_end[N_EXPERTS - 1])
    xs = sc_dispatch_rows(h, dest, pad_idx, jnp.zeros((SC_GATHER_CHUNK, d // 2), h.dtype))
    n_tiles = n_rows // te_rows
    tile_ids = jnp.arange(n_tiles, dtype=jnp.int32)
    prev_expert = jnp.concatenate([jnp.full((1,), -1, jnp.int32), tile_expert[:-1]])
    run_index = jnp.maximum(jnp.cumsum(((tile_expert != prev_expert) & (tile_ids < n_used[0])).astype(jnp.int32)) - 1, 0)
    next_tile = jnp.take(seg_end, tile_expert) // te_rows
    next_expert = jnp.where(next_tile < n_used[0], jnp.take(tile_expert, jnp.minimum(next_tile, n_tiles - 1)), -1)
    ys = moe_experts(xs, tile_expert, n_used, run_index, next_expert, w_gu, w_dn, layer, tm=te_rows)
    y_rows = sc_gather_rows(ys, dest)
    return moe_combine_rows(x2, route, y_rows, g_final, tm=ROW_TILE, final_norm=final_norm)


def _w_in_prep_kernel(w_ref, o_ref):
    gw = GROUP_WIDTH
    a_end = 7 * gw
    c_start = a_end + HEADS
    c_end = c_start + 5 * gw
    z_start = c_end + 2 * HEADS
    assert a_end % SMALL_W == LANE_FOX and c_end % SMALL_W == LANE_BETA and LANE_GDEC == LANE_BETA + HEADS
    w = w_ref[...]
    o_ref[:, 0:a_end] = w[:, 0:a_end].astype(BF16)
    o_ref[:, a_end:a_end + 5 * gw] = w[:, c_start:c_end].astype(BF16)
    o_ref[:, 12 * gw:13 * gw] = w[:, z_start:z_start + gw].astype(BF16)
    blk_fox = w[:, a_end:a_end + SMALL_W]
    blk_gdn = w[:, c_end - LANE_BETA:c_end - LANE_BETA + SMALL_W]
    lane = _iota(blk_fox.shape, 1)
    small = jnp.where(lane < LANE_BETA, blk_fox, jnp.where(lane < LANE_GDEC + HEADS, blk_gdn, 0.0))
    o_ref[:, 13 * gw:13 * gw + SMALL_W] = small.astype(BF16)
    o_ref[:, 13 * gw + SMALL_W:] = jnp.zeros((w.shape[0], SMALL_W), BF16)


def w_in_prep(w_all, layer):
    _, d, n = w_all.shape
    tr = 128
    return pl.pallas_call(
        _w_in_prep_kernel,
        grid=(d // tr,),
        in_specs=[pl.BlockSpec((None, tr, n), lambda i: (layer, i, 0))],
        out_specs=pl.BlockSpec((tr, N_PROJ), lambda i: (i, 0)),
        out_shape=jax.ShapeDtypeStruct((d, N_PROJ), BF16),
        compiler_params=_cparams("parallel"),
        name="w_in_prep",
    )(w_all)


def _block_diag(w):
    h = w.shape[0]
    eye = jnp.eye(h, dtype=w.dtype)
    return (eye[:, None, :, None] * w[:, :, None, :]).reshape(h * HEAD_DIM, h * HEAD_DIM)


def kernel(x, mem, norm_mix, w_in, hgrn_lb, fox_fb, lru_conv_w, lru_conv_b, lru_wa, lru_ba, lru_wx, lru_bx, lru_lam, gdn_conv_w, gdn_a_log, gdn_dt_bias, mix_gain, w_out, norm_mem, norm_memkv, w_mq, w_mkv, w_mo, norm_ffn, w_rg, b_rg, w_re, b_re, w_e_gu, w_e_dn, norm_final):
    b, s, d = x.shape
    depth = w_in.shape[0]
    t_tok = b * s
    mlen = mem.shape[1]
    gw = GROUP_WIDTH

    lb_all = jnp.cumsum(jax.nn.softmax(hgrn_lb.astype(F32), axis=0), axis=0)
    lb_all = lb_all - lb_all[0]

    x = x.astype(F32)
    for l in range(depth):
        proj = norm_matmul(x.reshape(t_tok, d), norm_mix[l], w_in_prep(w_in, l), tm=ROW_TILE)
        proj3 = proj.reshape(b, s, N_PROJ)
        prm = jnp.zeros((8, SMALL_W), F32)
        prm = prm.at[0, LANE_FOX:LANE_FOX + HEADS].set(fox_fb[l].astype(F32))
        prm = prm.at[0, LANE_GDEC:LANE_GDEC + HEADS].set(gdn_dt_bias[l].astype(F32))
        prm = prm.at[1, LANE_GDEC:LANE_GDEC + HEADS].set(gdn_a_log[l].astype(F32))
        sm, c_exp = small_prep(proj3, prm)

        ya = hgrn2(proj3, lb_all[l])
        yb = fox_attention(proj3, c_exp)
        yc = rglru(proj3, lru_conv_w[l], lru_conv_b[l], _block_diag(lru_wa[l]).astype(BF16), lru_ba[l],
                   _block_diag(lru_wx[l]).astype(BF16), lru_bx[l], lru_lam[l])
        yd = gdn(proj3, sm, gdn_conv_w[l])

        kv = norm_matmul(mem.reshape(b * mlen, d), norm_memkv[l], w_mkv[l].astype(BF16), tm=256)
        w_router = jnp.concatenate([w_re[l], w_rg[l], jnp.zeros((d, SMALL_W - N_EXPERTS - N_EXPERT_GROUPS), F32)], axis=1)
        b_router = jnp.concatenate([b_re[l], b_rg[l], jnp.zeros((SMALL_W - N_EXPERTS - N_EXPERT_GROUPS,), F32)]).reshape(1, SMALL_W)
        x, h, route, route_t, counts = post_mix(x, ya, yb, yc, yd, mix_gain[l], w_out[l].astype(BF16), norm_mem[l],
                                                w_mq[l].astype(BF16), kv.reshape(b, mlen, 2 * gw), w_mo[l].astype(BF16),
                                                norm_ffn[l], w_router, b_router)

        x = moe_routed(x.reshape(t_tok, d), h, route, route_t, counts, w_e_gu, w_e_dn, l,
                       norm_final, final_norm=(l == depth - 1)).reshape(b, s, d)

    return x
```
